```python
import jax, jax.numpy as jnp
from jax import lax
import numpy as np

D_MODEL = 1024
BATCH = 8
SEQ = 2048
DEPTH = 1

CHUNK = 64
MEM_LEN = 256
CONV_DIM = D_MODEL
CONV_WIDTH = 3
SB_HEADS = 16
SB_HEAD_DIM = 64
SB_DIM = SB_HEADS * SB_HEAD_DIM
SB_BLOCK = 128
X_HEADS = 4
X_HEAD_DIM = 256
X_DIM = X_HEADS * X_HEAD_DIM
N_BRANCH = 3
D_FF = 4 * D_MODEL
EPS = 1e-6

IN_SPLITS = [CONV_DIM, CONV_DIM, CONV_DIM, SB_DIM, SB_DIM, SB_DIM, X_DIM, N_BRANCH * D_MODEL]
IN_COLS = int(sum(IN_SPLITS))

kernel_name = "hybrid_conv_stickbreaking_memxattn_block"


def rms_norm(x, g):
    xf = x.astype(jnp.float32)
    y = xf * lax.rsqrt(jnp.mean(xf * xf, axis=-1, keepdims=True) + EPS)
    return (y * g.astype(jnp.float32)).astype(x.dtype)


def causal_depthwise_conv(u, w):
    c = u.shape[-1]
    return lax.conv_general_dilated(
        u, w.astype(u.dtype)[:, None, :], window_strides=(1,),
        padding=[(CONV_WIDTH - 1, 0)],
        dimension_numbers=("NWC", "WIO", "NWC"),
        feature_group_count=c)


def stick_breaking_attention(q, k, v):
    seq = q.shape[2]
    scale = q.shape[-1] ** -0.5
    outs = []
    for start in range(0, seq, SB_BLOCK):
        end = start + SB_BLOCK
        qb = q[:, :, start:end]
        kb = k[:, :, :end]
        vb = v[:, :, :end]
        z = jnp.einsum("bhqd,bhkd->bhqk", qb, kb).astype(jnp.float32) * scale
        t_idx = start + jnp.arange(SB_BLOCK)[:, None]
        s_idx = jnp.arange(end)[None, :]
        past = s_idx < t_idx
        log_beta = jax.nn.log_sigmoid(z)
        log_1mb = jnp.where(past, jax.nn.log_sigmoid(-z), 0.0)
        between = lax.cumsum(log_1mb, axis=3, reverse=True) - log_1mb
        a = jnp.where(past, jnp.exp(log_beta + between), 0.0)
        outs.append(jnp.einsum("bhqk,bhkd->bhqd", a.astype(vb.dtype), vb))
    return jnp.concatenate(outs, axis=2)


def memory_cross_attention(xq, mem_n, w_mem_kv, q_norm_g, k_norm_g):
    b, s, _ = xq.shape
    m = mem_n.shape[1]
    q = xq.reshape(b, s, X_HEADS, X_HEAD_DIM)
    kv = mem_n @ w_mem_kv
    k, v = jnp.split(kv, 2, axis=-1)
    k = k.reshape(b, m, X_HEADS, X_HEAD_DIM)
    v = v.reshape(b, m, X_HEADS, X_HEAD_DIM)
    q = rms_norm(q, q_norm_g)
    k = rms_norm(k, k_norm_g)
    scores = jnp.einsum("bqhd,bkhd->bhqk", q, k).astype(jnp.float32) * (X_HEAD_DIM ** -0.5)
    p = jax.nn.softmax(scores, axis=-1).astype(v.dtype)
    o = jnp.einsum("bhqk,bkhd->bqhd", p, v)
    return o.reshape(b, s, X_DIM)


def _fwd_setup_inputs(seed: int = 0) -> dict:
    key = jax.random.key(seed)
    ks = jax.random.split(key, 20)
    f32 = jnp.float32

    def nrm(k, shape, fan_in):
        return jax.random.normal(k, shape, f32) * (fan_in ** -0.5)

    def gain(k, shape):
        return 1.0 + 0.02 * jax.random.normal(k, shape, f32)

    L = DEPTH
    return {
        "x": jax.random.normal(ks[0], (BATCH, SEQ, D_MODEL), f32),
        "mem": jax.random.normal(ks[1], (BATCH, MEM_LEN, D_MODEL), f32),
        "g_mix": gain(ks[2], (L, D_MODEL)),
        "g_mem": gain(ks[3], (L, D_MODEL)),
        "w_in": nrm(ks[4], (L, D_MODEL, IN_COLS), D_MODEL),
        "conv_w": nrm(ks[5], (L, CONV_WIDTH, CONV_DIM), CONV_WIDTH),
        "w_conv_out": nrm(ks[6], (L, CONV_DIM, D_MODEL), CONV_DIM),
        "w_sb_out": nrm(ks[7], (L, SB_DIM, D_MODEL), SB_DIM),
        "q_norm_g": gain(ks[8], (L, X_HEAD_DIM)),
        "k_norm_g": gain(ks[9], (L, X_HEAD_DIM)),
        "w_mem_kv": nrm(ks[10], (L, D_MODEL, 2 * X_DIM), D_MODEL),
        "w_x_out": nrm(ks[11], (L, X_DIM, D_MODEL), X_DIM),
        "w_out": nrm(ks[12], (L, D_MODEL, D_MODEL), D_MODEL),
        "g_mlp": gain(ks[13], (L, D_MODEL)),
        "w_up": nrm(ks[14], (L, D_MODEL, D_FF), D_MODEL),
        "w_down": nrm(ks[15], (L, D_FF, D_MODEL), D_FF),
    }


def _fwd_reference(x, mem, g_mix, g_mem, w_in, conv_w, w_conv_out, w_sb_out, q_norm_g,
              k_norm_g, w_mem_kv, w_x_out, w_out, g_mlp, w_up, w_down):
    b, s, d = x.shape
    split_idx = list(np.cumsum(IN_SPLITS)[:-1])
    for l in range(DEPTH):
        h = rms_norm(x, g_mix[l])
        proj = h @ w_in[l]
        c_h, c_b, c_c, sq, sk, sv, xq, gate_pre = jnp.split(proj, split_idx, axis=-1)

        y_conv = (c_b * causal_depthwise_conv(c_c * c_h, conv_w[l])) @ w_conv_out[l]

        to_heads = lambda t: t.reshape(b, s, SB_HEADS, SB_HEAD_DIM).transpose(0, 2, 1, 3)
        o_sb = stick_breaking_attention(to_heads(sq), to_heads(sk), to_heads(sv))
        y_sb = o_sb.transpose(0, 2, 1, 3).reshape(b, s, SB_DIM) @ w_sb_out[l]

        mem_n = rms_norm(mem, g_mem[l])
        y_x = memory_cross_attention(xq, mem_n, w_mem_kv[l], q_norm_g[l], k_norm_g[l]) @ w_x_out[l]

        gates = jax.nn.sigmoid(gate_pre.astype(jnp.float32)).astype(x.dtype)
        gates = gates.reshape(b, s, N_BRANCH, d)
        merged = gates[:, :, 0] * y_conv + gates[:, :, 1] * y_sb + gates[:, :, 2] * y_x
        x = x + merged @ w_out[l]

        h2 = rms_norm(x, g_mlp[l])
        x = x + jnp.square(jax.nn.relu(h2 @ w_up[l])) @ w_down[l]
    return x


import jax as _jax
import jax.numpy as _jnp

TWIN_FORMAT = 'train_step'
FWD_PARAMS = ['x', 'mem', 'g_mix', 'g_mem', 'w_in', 'conv_w', 'w_conv_out', 'w_sb_out', 'q_norm_g', 'k_norm_g', 'w_mem_kv', 'w_x_out', 'w_out', 'g_mlp', 'w_up', 'w_down']
TWIN_WEIGHTS = ['g_mix', 'g_mem', 'w_in', 'conv_w', 'w_conv_out', 'w_sb_out', 'q_norm_g', 'k_norm_g', 'w_mem_kv', 'w_x_out', 'w_out', 'g_mlp', 'w_up', 'w_down']
TWIN_DIFF_INPUT = 'x'
TWIN_INPUTS = ['x', 'mem', 'g_mix', 'g_mem', 'w_in', 'conv_w', 'w_conv_out', 'w_sb_out', 'q_norm_g', 'k_norm_g', 'w_mem_kv', 'w_x_out', 'w_out', 'g_mlp', 'w_up', 'w_down', 'loss_target', 'm_g_mix', 'm_g_mem', 'm_w_in', 'm_conv_w', 'm_w_conv_out', 'm_w_sb_out', 'm_q_norm_g', 'm_k_norm_g', 'm_w_mem_kv', 'm_w_x_out', 'm_w_out', 'm_g_mlp', 'm_w_up', 'm_w_down', 'v_g_mix', 'v_g_mem', 'v_w_in', 'v_conv_w', 'v_w_conv_out', 'v_w_sb_out', 'v_q_norm_g', 'v_k_norm_g', 'v_w_mem_kv', 'v_w_x_out', 'v_w_out', 'v_g_mlp', 'v_w_up', 'v_w_down']
TWIN_OUTPUTS = ['loss', 'grad_x', 'grad_g_mix', 'grad_g_mem', 'grad_w_in', 'grad_conv_w', 'grad_w_conv_out', 'grad_w_sb_out', 'grad_q_norm_g', 'grad_k_norm_g', 'grad_w_mem_kv', 'grad_w_x_out', 'grad_w_out', 'grad_g_mlp', 'grad_w_up', 'grad_w_down', 'delta_g_mix', 'delta_g_mem', 'delta_w_in', 'delta_conv_w', 'delta_w_conv_out', 'delta_w_sb_out', 'delta_q_norm_g', 'delta_k_norm_g', 'delta_w_mem_kv', 'delta_w_x_out', 'delta_w_out', 'delta_g_mlp', 'delta_w_up', 'delta_w_down', 'new_m_g_mix', 'new_m_g_mem', 'new_m_w_in', 'new_m_conv_w', 'new_m_w_conv_out', 'new_m_w_sb_out', 'new_m_q_norm_g', 'new_m_k_norm_g', 'new_m_w_mem_kv', 'new_m_w_x_out', 'new_m_w_out', 'new_m_g_mlp', 'new_m_w_up', 'new_m_w_down', 'new_v_g_mix', 'new_v_g_mem', 'new_v_w_in', 'new_v_conv_w', 'new_v_w_conv_out', 'new_v_w_sb_out', 'new_v_q_norm_g', 'new_v_k_norm_g', 'new_v_w_mem_kv', 'new_v_w_x_out', 'new_v_w_out', 'new_v_g_mlp', 'new_v_w_up', 'new_v_w_down']
TWIN_LEAF_KINDS = {'loss': 'loss', 'grad_x': 'grad_x', 'grad_g_mix': 'grad_w', 'grad_g_mem': 'grad_w', 'grad_w_in': 'grad_w', 'grad_conv_w': 'grad_w', 'grad_w_conv_out': 'grad_w', 'grad_w_sb_out': 'grad_w', 'grad_q_norm_g': 'grad_w', 'grad_k_norm_g': 'grad_w', 'grad_w_mem_kv': 'grad_w', 'grad_w_x_out': 'grad_w', 'grad_w_out': 'grad_w', 'grad_g_mlp': 'grad_w', 'grad_w_up': 'grad_w', 'grad_w_down': 'grad_w', 'delta_g_mix': 'delta_w', 'delta_g_mem': 'delta_w', 'delta_w_in': 'delta_w', 'delta_conv_w': 'delta_w', 'delta_w_conv_out': 'delta_w', 'delta_w_sb_out': 'delta_w', 'delta_q_norm_g': 'delta_w', 'delta_k_norm_g': 'delta_w', 'delta_w_mem_kv': 'delta_w', 'delta_w_x_out': 'delta_w', 'delta_w_out': 'delta_w', 'delta_g_mlp': 'delta_w', 'delta_w_up': 'delta_w', 'delta_w_down': 'delta_w', 'new_m_g_mix': 'new_m', 'new_m_g_mem': 'new_m', 'new_m_w_in': 'new_m', 'new_m_conv_w': 'new_m', 'new_m_w_conv_out': 'new_m', 'new_m_w_sb_out': 'new_m', 'new_m_q_norm_g': 'new_m', 'new_m_k_norm_g': 'new_m', 'new_m_w_mem_kv': 'new_m', 'new_m_w_x_out': 'new_m', 'new_m_w_out': 'new_m', 'new_m_g_mlp': 'new_m', 'new_m_w_up': 'new_m', 'new_m_w_down': 'new_m', 'new_v_g_mix': 'new_v', 'new_v_g_mem': 'new_v', 'new_v_w_in': 'new_v', 'new_v_conv_w': 'new_v', 'new_v_w_conv_out': 'new_v', 'new_v_w_sb_out': 'new_v', 'new_v_q_norm_g': 'new_v', 'new_v_k_norm_g': 'new_v', 'new_v_w_mem_kv': 'new_v', 'new_v_w_x_out': 'new_v', 'new_v_w_out': 'new_v', 'new_v_g_mlp': 'new_v', 'new_v_w_up': 'new_v', 'new_v_w_down': 'new_v'}


def _forward(args):
    return _fwd_reference(*[args[k] for k in FWD_PARAMS])


def _output_shape():
    out = _jax.eval_shape(lambda: _forward(_fwd_setup_inputs(0)))
    return out.shape, out.dtype

N_MICROBATCH = 1
ADAM_LR = 0.001
ADAM_B1 = 0.9
ADAM_B2 = 0.999
ADAM_EPS = 1e-08
ADAM_WD = 0.01
ADAM_STEP = 10
PER_EXAMPLE_BATCH_AXIS = {'x': 0, 'mem': 0, 'loss_target': 0}
SHARED_INPUTS = []
_WEIGHT_DTYPES = {'g_mix': _jnp.float32, 'g_mem': _jnp.float32, 'w_in': _jnp.float32, 'conv_w': _jnp.float32, 'w_conv_out': _jnp.float32, 'w_sb_out': _jnp.float32, 'q_norm_g': _jnp.float32, 'k_norm_g': _jnp.float32, 'w_mem_kv': _jnp.float32, 'w_x_out': _jnp.float32, 'w_out': _jnp.float32, 'g_mlp': _jnp.float32, 'w_up': _jnp.float32, 'w_down': _jnp.float32}
MOMENT_SCALE = {'g_mix': 1.705035e+01, 'g_mem': 1.929730e-01, 'w_in': 1.940052e-01, 'conv_w': 2.644654e+00, 'w_conv_out': 3.589678e-01, 'w_sb_out': 2.049393e-01, 'q_norm_g': 1.800320e-01, 'k_norm_g': 1.797950e-01, 'w_mem_kv': 1.383393e-01, 'w_x_out': 1.953917e-01, 'w_out': 5.047055e-01, 'g_mlp': 4.832180e+01, 'w_up': 4.721696e-01, 'w_down': 4.041272e+00}


def _to_microbatches(a, axis):
    t = _jnp.moveaxis(a, axis, 0)
    t = t.reshape((N_MICROBATCH, t.shape[0] // N_MICROBATCH) + t.shape[1:])
    return _jnp.moveaxis(t, 1, axis + 1)


def setup_inputs(seed: int = 0) -> dict:
    inp = _fwd_setup_inputs(seed)
    key = _jax.random.fold_in(_jax.random.key(seed), 7919)
    shape, _ = _output_shape()
    out = dict(inp)
    out["loss_target"] = _jax.random.normal(_jax.random.fold_in(key, 0), shape, _jnp.float32)
    for i, name in enumerate(TWIN_WEIGHTS):
        w = inp[name].astype(_jnp.float32)
        if MOMENT_SCALE is None:
            s = _jnp.sqrt(_jnp.mean(_jnp.square(w)) + 1e-30)
        else:
            s = MOMENT_SCALE[name]
        km, kv = _jax.random.split(_jax.random.fold_in(key, i + 1))
        out[name] = w
        out["m_" + name] = s * _jax.random.normal(km, w.shape, _jnp.float32)
        out["v_" + name] = (s * s) * _jax.random.uniform(kv, w.shape, _jnp.float32, 0.5, 1.5)
    if N_MICROBATCH > 1:
        for name, axis in PER_EXAMPLE_BATCH_AXIS.items():
            out[name] = _to_microbatches(out[name], axis)
    return {'x': out['x'], 'mem': out['mem'], 'g_mix': out['g_mix'], 'g_mem': out['g_mem'], 'w_in': out['w_in'], 'conv_w': out['conv_w'], 'w_conv_out': out['w_conv_out'], 'w_sb_out': out['w_sb_out'], 'q_norm_g': out['q_norm_g'], 'k_norm_g': out['k_norm_g'], 'w_mem_kv': out['w_mem_kv'], 'w_x_out': out['w_x_out'], 'w_out': out['w_out'], 'g_mlp': out['g_mlp'], 'w_up': out['w_up'], 'w_down': out['w_down'], 'loss_target': out['loss_target'], 'm_g_mix': out['m_g_mix'], 'm_g_mem': out['m_g_mem'], 'm_w_in': out['m_w_in'], 'm_conv_w': out['m_conv_w'], 'm_w_conv_out': out['m_w_conv_out'], 'm_w_sb_out': out['m_w_sb_out'], 'm_q_norm_g': out['m_q_norm_g'], 'm_k_norm_g': out['m_k_norm_g'], 'm_w_mem_kv': out['m_w_mem_kv'], 'm_w_x_out': out['m_w_x_out'], 'm_w_out': out['m_w_out'], 'm_g_mlp': out['m_g_mlp'], 'm_w_up': out['m_w_up'], 'm_w_down': out['m_w_down'], 'v_g_mix': out['v_g_mix'], 'v_g_mem': out['v_g_mem'], 'v_w_in': out['v_w_in'], 'v_conv_w': out['v_conv_w'], 'v_w_conv_out': out['v_w_conv_out'], 'v_w_sb_out': out['v_w_sb_out'], 'v_q_norm_g': out['v_q_norm_g'], 'v_k_norm_g': out['v_k_norm_g'], 'v_w_mem_kv': out['v_w_mem_kv'], 'v_w_x_out': out['v_w_x_out'], 'v_w_out': out['v_w_out'], 'v_g_mlp': out['v_g_mlp'], 'v_w_up': out['v_w_up'], 'v_w_down': out['v_w_down']}


def _loss(weights, diff, rest, loss_target):
    with _jax.named_scope("forward"):
        args = {**rest, TWIN_DIFF_INPUT: diff, **{k: w.astype(_WEIGHT_DTYPES[k]) for k, w in weights.items()}}
        y = _forward(args)
    with _jax.named_scope("loss_head"):
        err = _jnp.square(y.astype(_jnp.float32) - loss_target)
        return 0.5 * _jnp.sum(_jnp.mean(err, axis=-1)) if err.ndim else 0.5 * err


def _adamw(w, g, m, v):
    m = ADAM_B1 * m + (1.0 - ADAM_B1) * g
    v = ADAM_B2 * v + (1.0 - ADAM_B2) * _jnp.square(g)
    m_hat = m / (1.0 - ADAM_B1 ** ADAM_STEP)
    v_hat = v / (1.0 - ADAM_B2 ** ADAM_STEP)
    delta = -ADAM_LR * (m_hat / (_jnp.sqrt(v_hat) + ADAM_EPS) + ADAM_WD * w)
    return delta, m, v


def reference(x, mem, g_mix, g_mem, w_in, conv_w, w_conv_out, w_sb_out, q_norm_g, k_norm_g, w_mem_kv, w_x_out, w_out, g_mlp, w_up, w_down, loss_target, m_g_mix, m_g_mem, m_w_in, m_conv_w, m_w_conv_out, m_w_sb_out, m_q_norm_g, m_k_norm_g, m_w_mem_kv, m_w_x_out, m_w_out, m_g_mlp, m_w_up, m_w_down, v_g_mix, v_g_mem, v_w_in, v_conv_w, v_w_conv_out, v_w_sb_out, v_q_norm_g, v_k_norm_g, v_w_mem_kv, v_w_x_out, v_w_out, v_g_mlp, v_w_up, v_w_down):
    given = dict(x=x, mem=mem, g_mix=g_mix, g_mem=g_mem, w_in=w_in, conv_w=conv_w, w_conv_out=w_conv_out, w_sb_out=w_sb_out, q_norm_g=q_norm_g, k_norm_g=k_norm_g, w_mem_kv=w_mem_kv, w_x_out=w_x_out, w_out=w_out, g_mlp=g_mlp, w_up=w_up, w_down=w_down, loss_target=loss_target, m_g_mix=m_g_mix, m_g_mem=m_g_mem, m_w_in=m_w_in, m_conv_w=m_conv_w, m_w_conv_out=m_w_conv_out, m_w_sb_out=m_w_sb_out, m_q_norm_g=m_q_norm_g, m_k_norm_g=m_k_norm_g, m_w_mem_kv=m_w_mem_kv, m_w_x_out=m_w_x_out, m_w_out=m_w_out, m_g_mlp=m_g_mlp, m_w_up=m_w_up, m_w_down=m_w_down, v_g_mix=v_g_mix, v_g_mem=v_g_mem, v_w_in=v_w_in, v_conv_w=v_conv_w, v_w_conv_out=v_w_conv_out, v_w_sb_out=v_w_sb_out, v_q_norm_g=v_q_norm_g, v_k_norm_g=v_k_norm_g, v_w_mem_kv=v_w_mem_kv, v_w_x_out=v_w_x_out, v_w_out=v_w_out, v_g_mlp=v_g_mlp, v_w_up=v_w_up, v_w_down=v_w_down)
    weights = {n: given[n] for n in TWIN_WEIGHTS}
    shared = {n: given[n] for n in SHARED_INPUTS}
    per_example = {n: given[n] for n in ['x', 'mem']}
    grad_fn = _jax.value_and_grad(_loss, argnums=(0, 1))

    def one_microbatch(ex, loss_target):
        ex = dict(ex)
        diff = ex.pop(TWIN_DIFF_INPUT)
        return grad_fn(weights, diff, {**shared, **ex}, loss_target)

    if N_MICROBATCH == 1:
        loss, (grad_w, grad_x) = one_microbatch(per_example, given["loss_target"])
    else:
        def body(carry, xs):
            loss_sum, grad_sum = carry
            l_k, (gw_k, gx_k) = one_microbatch(xs[0], xs[1])
            with _jax.named_scope("update"):
                return (loss_sum + l_k, _jax.tree.map(_jnp.add, grad_sum, gw_k)), gx_k

        init = (_jnp.zeros((), _jnp.float32), _jax.tree.map(_jnp.zeros_like, weights))
        (loss, grad_w), grad_x = _jax.lax.scan(body, init, (per_example, given["loss_target"]))
    with _jax.named_scope("update"):
        delta_w, new_m, new_v = {}, {}, {}
        for n in TWIN_WEIGHTS:
            delta_w[n], new_m[n], new_v[n] = _adamw(weights[n], grad_w[n], given["m_" + n], given["v_" + n])
    return (loss, grad_x, *[grad_w[n] for n in TWIN_WEIGHTS], *[delta_w[n] for n in TWIN_WEIGHTS],
            *[new_m[n] for n in TWIN_WEIGHTS], *[new_v[n] for n in TWIN_WEIGHTS])
```

```python
import functools

import jax
import jax.numpy as jnp
from jax import lax
from jax.experimental import pallas as pl
from jax.experimental.pallas import tpu as pltpu

F32 = jnp.float32
BF16 = jnp.bfloat16
MESH = pl.DeviceIdType.MESH

EPS = 1e-6
N_DEV = 8
D_MODEL = 1024
SB_HEAD_DIM = 64
SB_BLOCK = 128
X_HEADS = 4
X_HEAD_DIM = 256
N_BRANCH = 3
COL_CH, COL_CB, COL_CC, COL_SQ, COL_SK, COL_SV, COL_XQ, COL_GATE = 0, 1, 2, 3, 4, 5, 6, 7

ADAM_LR = 0.001
ADAM_B1 = 0.9
ADAM_B2 = 0.999
ADAM_EPS = 1e-08
ADAM_WD = 0.01
ADAM_STEP = 10

SMALL_TILE = 8


def _dot(a, b):
    return jnp.dot(a, b, preferred_element_type=F32)


def _dot_nt(a, b):
    return lax.dot_general(a, b, (((1,), (1,)), ((), ())), preferred_element_type=F32)


def _dot_tn(a, b):
    return lax.dot_general(a, b, (((0,), (0,)), ((), ())), preferred_element_type=F32)


def _rstd(xf):
    return lax.rsqrt(jnp.mean(xf * xf, axis=-1, keepdims=True) + EPS)


def _sigmoid(z):
    return 1.0 / (1.0 + jnp.exp(-z))


def _log_sigmoid(z):
    return jnp.minimum(z, 0.0) - jnp.log(1.0 + jnp.exp(-jnp.abs(z)))


def _rms_bwd(dy, xhat, r, g):
    dxhat = dy * g
    return r * (dxhat - xhat * jnp.mean(dxhat * xhat, axis=-1, keepdims=True))


def _mesh_pos():
    return lax.axis_index("x"), lax.axis_index("y"), lax.axis_index("c")


ANY = pl.BlockSpec(memory_space=pl.ANY)


def _all_gather(shards):
    n = len(shards)

    def body(*refs):
        ins, outs = refs[:n], refs[n:2 * n]
        send_sems, recv_sems, local_sems = refs[2 * n:]
        x, y, c = _mesh_pos()
        me, sibling = (x, y, c), (x, y, 1 - c)
        chips = [(1 - x, y), (x, 1 - y), (1 - x, 1 - y)]

        def blk(a, px, py, pc):
            return outs[a].at[4 * px + 2 * py + pc]

        def copy(a, k, block, to, src=None):
            return pltpu.make_async_remote_copy(
                src_ref=blk(a, *block) if src is None else src, dst_ref=blk(a, *block),
                send_sem=send_sems.at[a, k], recv_sem=recv_sems.at[a, k], device_id=to, device_id_type=MESH)

        mine = [pltpu.make_async_copy(ins[a], blk(a, *me), local_sems.at[a]) for a in range(n)]
        for cp in mine:
            cp.start()
        first = []
        for a in range(n):
            first.append(copy(a, 0, me, sibling, src=ins[a]))
            first += [copy(a, 1 + j, me, (*chip, c), src=ins[a]) for j, chip in enumerate(chips)]
        for cp in first:
            cp.start()
        passed = []
        for j, chip in enumerate(chips):
            for a in range(n):
                copy(a, 1 + j, (*chip, c), me).wait_recv()
                fwd = copy(a, 4 + j, (*chip, c), sibling)
                fwd.start()
                passed.append(fwd)
        for a in range(n):
            copy(a, 0, sibling, me).wait_recv()
            for j, chip in enumerate(chips):
                copy(a, 4 + j, (*chip, 1 - c), me).wait_recv()
        for cp in first + passed:
            cp.wait_send()
        for cp in mine:
            cp.wait()

    return pl.pallas_call(
        body, name="weights_all_gather",
        out_shape=[jax.ShapeDtypeStruct((N_DEV,) + s.shape, s.dtype) for s in shards],
        in_specs=[ANY] * n, out_specs=[ANY] * n,
        scratch_shapes=[pltpu.SemaphoreType.DMA((n, 7)), pltpu.SemaphoreType.DMA((n, 7)),
                        pltpu.SemaphoreType.DMA((n,))],
    )(*shards)


def _pair_exchange(grads):
    n = len(grads)

    def body(*refs):
        ins, outs = refs[:n], refs[n:2 * n]
        send_sems, recv_sems = refs[2 * n:]
        x, y, c = _mesh_pos()
        xs, ys = (x, 1 - x), (y, 1 - y)
        copies = []
        for a in range(n):
            for r in range(4):
                dx, dy = divmod(r, 2)
                copies.append(pltpu.make_async_remote_copy(
                    src_ref=ins[a].at[4 * xs[dx] + 2 * ys[dy] + (1 - c)], dst_ref=outs[a].at[r],
                    send_sem=send_sems.at[a, r], recv_sem=recv_sems.at[a, r],
                    device_id=(x, y, 1 - c), device_id_type=MESH))
        for cp in copies:
            cp.start()
        for cp in copies:
            cp.wait()

    return pl.pallas_call(
        body, name="grad_pair_exchange",
        out_shape=[jax.ShapeDtypeStruct((4,) + g.shape[1:], g.dtype) for g in grads],
        in_specs=[ANY] * n, out_specs=[ANY] * n,
        scratch_shapes=[pltpu.SemaphoreType.DMA((n, 4)), pltpu.SemaphoreType.DMA((n, 4))],
    )(*grads)


def _chip_exchange(sums):
    n = len(sums)

    def body(*refs):
        ins, outs = refs[:n], refs[n:2 * n]
        send_sems, recv_sems = refs[2 * n:]
        x, y, c = _mesh_pos()
        xs, ys = (x, 1 - x), (y, 1 - y)
        copies = []
        for a in range(n):
            for r in range(1, 4):
                dx, dy = divmod(r, 2)
                copies.append(pltpu.make_async_remote_copy(
                    src_ref=ins[a].at[r], dst_ref=outs[a].at[r - 1],
                    send_sem=send_sems.at[a, r - 1], recv_sem=recv_sems.at[a, r - 1],
                    device_id=(xs[dx], ys[dy], c), device_id_type=MESH))
        for cp in copies:
            cp.start()
        for cp in copies:
            cp.wait()

    return pl.pallas_call(
        body, name="grad_chip_exchange",
        out_shape=[jax.ShapeDtypeStruct((3,) + s.shape[1:], s.dtype) for s in sums],
        in_specs=[ANY] * n, out_specs=[ANY] * n,
        scratch_shapes=[pltpu.SemaphoreType.DMA((n, 3)), pltpu.SemaphoreType.DMA((n, 3))],
    )(*sums)


def _small_all_gather(part):
    rows, cols = part.shape

    def body(in_ref, out_ref, send_sems, recv_sems):
        x, y, c = _mesh_pos()
        xs, ys, cs = (x, 1 - x), (y, 1 - y), (c, 1 - c)
        out_ref[4 * x + 2 * y + c] = in_ref[...]
        copies = []
        for k in range(1, N_DEV):
            dx, dy, dc = k // 4, (k // 2) % 2, k % 2
            copies.append((
                pltpu.make_async_remote_copy(
                    src_ref=in_ref, dst_ref=out_ref.at[4 * x + 2 * y + c],
                    send_sem=send_sems.at[k - 1], recv_sem=recv_sems.at[k - 1],
                    device_id=(xs[dx], ys[dy], cs[dc]), device_id_type=MESH),
                pltpu.make_async_remote_copy(
                    src_ref=in_ref, dst_ref=out_ref.at[4 * xs[dx] + 2 * ys[dy] + cs[dc]],
                    send_sem=send_sems.at[k - 1], recv_sem=recv_sems.at[k - 1],
                    device_id=(xs[dx], ys[dy], cs[dc]), device_id_type=MESH)))
        for send, _ in copies:
            send.start()
        for send, recv in copies:
            recv.wait_recv()
            send.wait_send()

    return pl.pallas_call(
        body, name="small_all_gather",
        out_shape=jax.ShapeDtypeStruct((N_DEV, rows, cols), part.dtype),
        in_specs=[pl.BlockSpec(memory_space=pltpu.VMEM)],
        out_specs=pl.BlockSpec(memory_space=pltpu.VMEM),
        scratch_shapes=[pltpu.SemaphoreType.DMA((N_DEV - 1,)), pltpu.SemaphoreType.DMA((N_DEV - 1,))],
    )(part)


def _in_proj(x, g_mix, w_in_all, tm):
    t, d = x.shape
    nb, _, bw = w_in_all.shape

    def body(x_ref, g_ref, w_ref, proj_ref, h_ref):
        @pl.when(pl.program_id(1) == 0)
        def _():
            xf = x_ref[...]
            h_ref[...] = (xf * _rstd(xf) * g_ref[...]).astype(BF16)

        proj_ref[...] = _dot(h_ref[...], w_ref[...]).astype(BF16)

    return pl.pallas_call(
        body, name="in_proj",
        grid=(t // tm, nb),
        in_specs=[pl.BlockSpec((tm, d), lambda i, j: (i, 0)),
                  pl.BlockSpec((1, d), lambda i, j: (0, 0)),
                  pl.BlockSpec((None, d, bw), lambda i, j: (j, 0, 0))],
        out_specs=[pl.BlockSpec((tm, bw), lambda i, j: (i, j)),
                   pl.BlockSpec((tm, d), lambda i, j: (i, 0))],
        out_shape=[jax.ShapeDtypeStruct((t, nb * bw), BF16), jax.ShapeDtypeStruct((t, d), BF16)],
        compiler_params=pltpu.CompilerParams(dimension_semantics=("parallel", "arbitrary")),
    )(x, g_mix, w_in_all)


def _conv_terms(ch_ref, cb_ref, cc_ref, w_ref):
    ch, cb, cc = ch_ref[...].astype(F32), cb_ref[...].astype(F32), cc_ref[...].astype(F32)
    u = cc * ch
    row = lax.broadcasted_iota(jnp.int32, u.shape, 0)
    u1 = jnp.where(row >= 1, pltpu.roll(u, 1, 0), 0.0)
    u2 = jnp.where(row >= 2, pltpu.roll(u, 2, 0), 0.0)
    w = (w_ref[0:1, :], w_ref[1:2, :], w_ref[2:3, :])
    cv = w[2] * u + w[1] * u1 + w[0] * u2
    return ch, cb, cc, u, u1, u2, cv, w, row


def _conv_fwd(proj, conv_w, cw):
    t = proj.shape[0]
    nper = D_MODEL // cw

    def body(ch_ref, cb_ref, cc_ref, w_ref, a_ref):
        _, cb, _, _, _, _, cv, _, _ = _conv_terms(ch_ref, cb_ref, cc_ref, w_ref)
        a_ref[...] = (cb * cv).astype(BF16)

    def col(piece):
        return pl.BlockSpec((t, cw), lambda j: (0, piece * nper + j))

    return pl.pallas_call(
        body, name="conv_fwd",
        grid=(nper,),
        in_specs=[col(COL_CH), col(COL_CB), col(COL_CC), pl.BlockSpec((3, cw), lambda j: (0, j))],
        out_specs=pl.BlockSpec((t, cw), lambda j: (0, j)),
        out_shape=jax.ShapeDtypeStruct((t, D_MODEL), BF16),
        compiler_params=pltpu.CompilerParams(dimension_semantics=("parallel",)),
    )(proj, proj, proj, conv_w)


def _scan_matrix():
    s = lax.broadcasted_iota(jnp.int32, (SB_BLOCK, SB_BLOCK), 0)
    j = lax.broadcasted_iota(jnp.int32, (SB_BLOCK, SB_BLOCK), 1)
    return jnp.where(j > s, 1.0, 0.0).astype(BF16)


def _suffix_sum(u_mat, xv):
    hi = xv.astype(BF16)
    lo = (xv - hi.astype(F32)).astype(BF16)
    return _dot(u_mat, hi) + _dot(u_mat, lo)


def _head_rows(vt, h):
    row = lax.broadcasted_iota(jnp.int32, vt.shape, 0)
    return jnp.where((row >= h * SB_HEAD_DIM) & (row < (h + 1) * SB_HEAD_DIM), vt, 0.0).astype(BF16)


def _head_lanes(v, h):
    lane = lax.broadcasted_iota(jnp.int32, v.shape, 1)
    return jnp.where((lane >= h * SB_HEAD_DIM) & (lane < (h + 1) * SB_HEAD_DIM), v, 0.0).astype(BF16)


def _sb_probs(kblk, qt_h, u_mat, carry, past):
    z = _dot(kblk, qt_h) * (SB_HEAD_DIM ** -0.5)
    lb = _log_sigmoid(z)
    l1 = lb - z
    if past is not None:
        l1 = jnp.where(past, l1, 0.0)
    a = jnp.exp(lb + _suffix_sum(u_mat, l1) + carry)
    if past is not None:
        a = jnp.where(past, a, 0.0)
    return a, lb, carry + jnp.sum(l1, axis=0, keepdims=True)


def _past_mask():
    s = lax.broadcasted_iota(jnp.int32, (SB_BLOCK, SB_BLOCK), 0)
    t = lax.broadcasted_iota(jnp.int32, (SB_BLOCK, SB_BLOCK), 1)
    return s < t


def _sb_fwd(proj, vt4):
    t = proj.shape[0]
    nq = t // SB_BLOCK
    pairs = D_MODEL // SB_BLOCK

    def body(q_ref, k_ref, vt_ref, o_ref):
        qi = pl.program_id(1)
        qt = q_ref[...].astype(F32).T
        qts = [_head_rows(qt, h) for h in range(2)]
        u_mat = _scan_matrix()

        def block(kb, state, past):
            ks = pl.multiple_of(kb * SB_BLOCK, SB_BLOCK)
            kblk = k_ref[pl.ds(ks, SB_BLOCK), :]
            vt = vt_ref[kb]
            out = []
            for h in range(2):
                acc, carry = state[h]
                a, _, carry = _sb_probs(kblk, qts[h], u_mat, carry, past)
                acc = acc + _dot(vt[h * SB_HEAD_DIM:(h + 1) * SB_HEAD_DIM, :], a.astype(BF16))
                out.append((acc, carry))
            return tuple(out)

        zero = (jnp.zeros((SB_HEAD_DIM, SB_BLOCK), F32), jnp.zeros((1, SB_BLOCK), F32))
        state = block(qi, (zero, zero), _past_mask())
        state = lax.fori_loop(0, qi, lambda i, st: block(qi - 1 - i, st, None), state)
        o_ref[...] = jnp.concatenate([state[0][0], state[1][0]], axis=0).T

    return pl.pallas_call(
        body, name="sb_fwd",
        grid=(pairs, nq),
        in_specs=[pl.BlockSpec((SB_BLOCK, SB_BLOCK), lambda p, i: (i, COL_SQ * pairs + p)),
                  pl.BlockSpec((t, SB_BLOCK), lambda p, i: (0, COL_SK * pairs + p)),
                  pl.BlockSpec((None, nq, SB_BLOCK, SB_BLOCK), lambda p, i: (p, 0, 0, 0))],
        out_specs=pl.BlockSpec((SB_BLOCK, SB_BLOCK), lambda p, i: (i, p)),
        out_shape=jax.ShapeDtypeStruct((t, D_MODEL), F32),
        compiler_params=pltpu.CompilerParams(dimension_semantics=("parallel", "parallel")),
    )(proj, proj, vt4)


def _mem_prep(mem, g_mem, wkv_all, k_norm_g):
    m, d = mem.shape

    def body(mem_ref, g_ref, w_ref, kg_ref, memn_ref, kn_ref, v_ref):
        memf = mem_ref[...]
        memn = (memf * _rstd(memf) * g_ref[...]).astype(BF16)
        memn_ref[...] = memn
        for b in range(N_DEV):
            kv = _dot(memn, w_ref[b])
            if b < X_HEADS:
                kn_ref[:, b * X_HEAD_DIM:(b + 1) * X_HEAD_DIM] = (kv * _rstd(kv) * kg_ref[...]).astype(BF16)
            else:
                h = b - X_HEADS
                v_ref[:, h * X_HEAD_DIM:(h + 1) * X_HEAD_DIM] = kv.astype(BF16)

    return pl.pallas_call(
        body, name="mem_prep",
        out_shape=[jax.ShapeDtypeStruct((m, d), BF16)] * 3,
    )(mem, g_mem, wkv_all, k_norm_g)


def _x_head(xq_ref, qg, kn_ref, h):
    sl = slice(h * X_HEAD_DIM, (h + 1) * X_HEAD_DIM)
    q = xq_ref[:, sl].astype(F32)
    rq = _rstd(q)
    qhat = q * rq
    qn = (qhat * qg).astype(BF16)
    s = _dot_nt(qn, kn_ref[:, sl]) * (X_HEAD_DIM ** -0.5)
    e = jnp.exp(s - jnp.max(s, axis=-1, keepdims=True))
    p = e / jnp.sum(e, axis=-1, keepdims=True)
    return sl, rq, qhat, qn, p


def _x_fwd(proj, q_norm_g, kn, v, tm):
    t = proj.shape[0]
    m = kn.shape[0]

    def body(xq_ref, qg_ref, kn_ref, v_ref, o_ref):
        for h in range(X_HEADS):
            sl, _, _, _, p = _x_head(xq_ref, qg_ref[...], kn_ref, h)
            o_ref[:, sl] = _dot(p.astype(BF16), v_ref[:, sl]).astype(BF16)

    return pl.pallas_call(
        body, name="x_fwd",
        grid=(t // tm,),
        in_specs=[pl.BlockSpec((tm, D_MODEL), lambda i: (i, COL_XQ)),
                  pl.BlockSpec((1, X_HEAD_DIM), lambda i: (0, 0)),
                  pl.BlockSpec((m, D_MODEL), lambda i: (0, 0)),
                  pl.BlockSpec((m, D_MODEL), lambda i: (0, 0))],
        out_specs=pl.BlockSpec((tm, D_MODEL), lambda i: (i, 0)),
        out_shape=jax.ShapeDtypeStruct((t, D_MODEL), BF16),
        compiler_params=pltpu.CompilerParams(dimension_semantics=("parallel",)),
    )(proj, q_norm_g, kn, v)


def _gate_spec(tm, branch):
    return pl.BlockSpec((tm, D_MODEL), lambda i: (i, COL_GATE + branch))


def _merge_fwd(x, proj, a_conv, o_sb, o_x, w_conv_out, w_sb_out, w_x_out, w_out, tm):
    t, d = x.shape

    def body(x_ref, g0_ref, g1_ref, g2_ref, a_ref, s_ref, xo_ref, wc_ref, ws_ref, wx_ref, wo_ref,
             x1_ref, yc_ref, ys_ref, yx_ref, mg_ref):
        merged = jnp.zeros((tm, d), F32)
        for gate_ref, b_ref, w_ref, y_ref in ((g0_ref, a_ref, wc_ref, yc_ref), (g1_ref, s_ref, ws_ref, ys_ref),
                                              (g2_ref, xo_ref, wx_ref, yx_ref)):
            yv = _dot(b_ref[...].astype(BF16), w_ref[...])
            y_ref[...] = yv.astype(BF16)
            merged = merged + _sigmoid(gate_ref[...].astype(F32)) * yv
        mb = merged.astype(BF16)
        mg_ref[...] = mb
        x1_ref[...] = x_ref[...] + _dot(mb, wo_ref[...])

    tile = pl.BlockSpec((tm, d), lambda i: (i, 0))
    wfull = pl.BlockSpec((d, d), lambda i: (0, 0))
    return pl.pallas_call(
        body, name="merge_fwd",
        grid=(t // tm,),
        in_specs=[tile] + [_gate_spec(tm, b) for b in range(N_BRANCH)] + [tile, tile, tile,
                                                                           wfull, wfull, wfull, wfull],
        out_specs=[tile] * 5,
        out_shape=[jax.ShapeDtypeStruct((t, d), F32)] + [jax.ShapeDtypeStruct((t, d), BF16)] * 4,
        compiler_params=pltpu.CompilerParams(dimension_semantics=("parallel",)),
    )(x, proj, proj, proj, a_conv, o_sb, o_x, w_conv_out, w_sb_out, w_x_out, w_out)


def _mlp_fwd(x1, g_mlp, w_up_all, w_down, target, tm):
    t, d = x1.shape
    nb, _, fw = w_up_all.shape

    def body(x1_ref, g_ref, wu_ref, wd_ref, tgt_ref, up_ref, h2_ref, dx2_ref, lsum_ref, acc_ref):
        i, j = pl.program_id(0), pl.program_id(1)

        @pl.when(j == 0)
        def _():
            xf = x1_ref[...]
            h2_ref[...] = (xf * _rstd(xf) * g_ref[...]).astype(BF16)
            acc_ref[...] = jnp.zeros_like(acc_ref)

        @pl.when((i == 0) & (j == 0))
        def _():
            lsum_ref[...] = jnp.zeros_like(lsum_ref)

        up = _dot(h2_ref[...], wu_ref[...])
        up_ref[...] = up.astype(BF16)
        act = jnp.square(jnp.maximum(up, 0.0)).astype(BF16)
        acc_ref[...] += _dot(act, wd_ref[...])

        @pl.when(j == nb - 1)
        def _():
            diff = x1_ref[...] + acc_ref[...] - tgt_ref[...]
            dx2_ref[...] = diff * (1.0 / d)
            lsum_ref[...] += jnp.sum(diff * diff, axis=0, keepdims=True)

    tile = pl.BlockSpec((tm, d), lambda i, j: (i, 0))
    row = pl.BlockSpec((1, d), lambda i, j: (0, 0))
    return pl.pallas_call(
        body, name="mlp_fwd",
        grid=(t // tm, nb),
        in_specs=[tile, row, pl.BlockSpec((None, d, fw), lambda i, j: (j, 0, 0)),
                  pl.BlockSpec((fw, d), lambda i, j: (j, 0)), tile],
        out_specs=[pl.BlockSpec((tm, fw), lambda i, j: (i, j)), tile, tile, row],
        out_shape=[jax.ShapeDtypeStruct((t, nb * fw), BF16), jax.ShapeDtypeStruct((t, d), BF16),
                   jax.ShapeDtypeStruct((t, d), F32), jax.ShapeDtypeStruct((1, d), F32)],
        scratch_shapes=[pltpu.VMEM((tm, d), F32)],
        compiler_params=pltpu.CompilerParams(dimension_semantics=("arbitrary", "arbitrary")),
    )(x1, g_mlp, w_up_all, w_down, target)


def _mlp_bwd(x1, g_mlp, w_up_all, w_down, up, dx2, tm):
    t, d = x1.shape
    nb, _, fw = w_up_all.shape

    def body(x1_ref, g_ref, wu_ref, wd_ref, up_ref, dx2_ref, dup_ref, act_ref, dx1_ref, dg_ref, acc_ref, dyb_ref):
        i, j = pl.program_id(0), pl.program_id(1)

        @pl.when(j == 0)
        def _():
            dyb_ref[...] = dx2_ref[...].astype(BF16)
            acc_ref[...] = jnp.zeros_like(acc_ref)

        @pl.when((i == 0) & (j == 0))
        def _():
            dg_ref[...] = jnp.zeros_like(dg_ref)

        r = jnp.maximum(up_ref[...].astype(F32), 0.0)
        act_ref[...] = jnp.square(r).astype(BF16)
        dup = (_dot_nt(dyb_ref[...], wd_ref[...]) * (2.0 * r)).astype(BF16)
        dup_ref[...] = dup
        acc_ref[...] += _dot_nt(dup, wu_ref[...])

        @pl.when(j == nb - 1)
        def _():
            xf = x1_ref[...]
            rs = _rstd(xf)
            xhat = xf * rs
            dh2 = acc_ref[...]
            dg_ref[...] += jnp.sum(dh2 * xhat, axis=0, keepdims=True)
            dx1_ref[...] = dx2_ref[...] + _rms_bwd(dh2, xhat, rs, g_ref[...])

    tile = pl.BlockSpec((tm, d), lambda i, j: (i, 0))
    row = pl.BlockSpec((1, d), lambda i, j: (0, 0))
    ff = pl.BlockSpec((tm, fw), lambda i, j: (i, j))
    return pl.pallas_call(
        body, name="mlp_bwd",
        grid=(t // tm, nb),
        in_specs=[tile, row, pl.BlockSpec((None, d, fw), lambda i, j: (j, 0, 0)),
                  pl.BlockSpec((fw, d), lambda i, j: (j, 0)), ff, tile],
        out_specs=[ff, ff, tile, row],
        out_shape=[jax.ShapeDtypeStruct((t, nb * fw), BF16), jax.ShapeDtypeStruct((t, nb * fw), BF16),
                   jax.ShapeDtypeStruct((t, d), F32), jax.ShapeDtypeStruct((1, d), F32)],
        scratch_shapes=[pltpu.VMEM((tm, d), F32), pltpu.VMEM((tm, d), BF16)],
        compiler_params=pltpu.CompilerParams(dimension_semantics=("arbitrary", "arbitrary")),
    )(x1, g_mlp, w_up_all, w_down, up, dx2)


def _merge_bwd(dx1, proj, y_conv, y_sb, y_x, w_conv_out, w_sb_out, w_x_out, w_out, tm):
    t, d = dx1.shape

    def body(dx1_ref, g0_ref, g1_ref, g2_ref, yc_ref, ys_ref, yx_ref, wc_ref, ws_ref, wx_ref, wo_ref,
             dgate_ref, dyc_ref, dys_ref, dyx_ref, da_ref, dos_ref, dox_ref):
        dm = _dot_nt(dx1_ref[...].astype(BF16), wo_ref[...])
        for i, (gate_ref, y_ref, w_ref, dy_ref, db_ref) in enumerate(((g0_ref, yc_ref, wc_ref, dyc_ref, da_ref),
                                                                       (g1_ref, ys_ref, ws_ref, dys_ref, dos_ref),
                                                                       (g2_ref, yx_ref, wx_ref, dyx_ref, dox_ref))):
            gt = _sigmoid(gate_ref[...].astype(F32))
            dy = (dm * gt).astype(BF16)
            dy_ref[...] = dy
            dgate_ref[:, i * d:(i + 1) * d] = (dm * y_ref[...].astype(F32) * gt * (1.0 - gt)).astype(BF16)
            db_ref[...] = _dot_nt(dy, w_ref[...]).astype(BF16)

    tile = pl.BlockSpec((tm, d), lambda i: (i, 0))
    wfull = pl.BlockSpec((d, d), lambda i: (0, 0))
    return pl.pallas_call(
        body, name="merge_bwd",
        grid=(t // tm,),
        in_specs=[tile] + [_gate_spec(tm, b) for b in range(N_BRANCH)] + [tile, tile, tile,
                                                                           wfull, wfull, wfull, wfull],
        out_specs=[pl.BlockSpec((tm, N_BRANCH * d), lambda i: (i, 0))] + [tile] * 6,
        out_shape=[jax.ShapeDtypeStruct((t, N_BRANCH * d), BF16)] + [jax.ShapeDtypeStruct((t, d), BF16)] * 6,
        compiler_params=pltpu.CompilerParams(dimension_semantics=("parallel",)),
    )(dx1, proj, proj, proj, y_conv, y_sb, y_x, w_conv_out, w_sb_out, w_x_out, w_out)


def _conv_bwd(proj, conv_w, da, cw):
    t = proj.shape[0]
    nper = D_MODEL // cw

    def body(ch_ref, cb_ref, cc_ref, w_ref, da_ref, dch_ref, dcb_ref, dcc_ref, dw_ref):
        ch, cb, cc, u, u1, u2, cv, w, row = _conv_terms(ch_ref, cb_ref, cc_ref, w_ref)
        dav = da_ref[...].astype(F32)
        dcb_ref[...] = (dav * cv).astype(BF16)
        dcv = dav * cb
        n1 = jnp.where(row < t - 1, pltpu.roll(dcv, t - 1, 0), 0.0)
        n2 = jnp.where(row < t - 2, pltpu.roll(dcv, t - 2, 0), 0.0)
        du = w[2] * dcv + w[1] * n1 + w[0] * n2
        dcc_ref[...] = (du * ch).astype(BF16)
        dch_ref[...] = (du * cc).astype(BF16)
        dw_ref[0:1, :] = jnp.sum(dcv * u2, axis=0, keepdims=True)
        dw_ref[1:2, :] = jnp.sum(dcv * u1, axis=0, keepdims=True)
        dw_ref[2:3, :] = jnp.sum(dcv * u, axis=0, keepdims=True)

    def col(piece):
        return pl.BlockSpec((t, cw), lambda j: (0, piece * nper + j))

    out_col = pl.BlockSpec((t, cw), lambda j: (0, j))
    wspec = pl.BlockSpec((3, cw), lambda j: (0, j))
    return pl.pallas_call(
        body, name="conv_bwd",
        grid=(nper,),
        in_specs=[col(COL_CH), col(COL_CB), col(COL_CC), wspec, out_col],
        out_specs=[out_col, out_col, out_col, wspec],
        out_shape=[jax.ShapeDtypeStruct((t, D_MODEL), BF16)] * 3 + [jax.ShapeDtypeStruct((3, D_MODEL), F32)],
        compiler_params=pltpu.CompilerParams(dimension_semantics=("parallel",)),
    )(proj, proj, proj, conv_w, da)


def _sb_bwd(proj, kt4, vt4, do_sb, o_sb):
    t = proj.shape[0]
    nq = t // SB_BLOCK
    pairs = D_MODEL // SB_BLOCK
    scale = SB_HEAD_DIM ** -0.5

    def body(q_ref, k_ref, v_ref, kt_ref, vt_ref, do_ref, o_ref, dq_ref, dk_ref, dv_ref, dk_acc, dv_acc):
        qi = pl.program_id(1)

        @pl.when(qi == 0)
        def _():
            dk_acc[...] = jnp.zeros_like(dk_acc)
            dv_acc[...] = jnp.zeros_like(dv_acc)

        q = q_ref[...].astype(F32)
        do = do_ref[...].astype(F32)
        qt, dot_ = q.T, do.T
        prod = dot_ * o_ref[...].T
        u_mat = _scan_matrix()
        qts = [_head_rows(qt, h) for h in range(2)]
        dots = [_head_rows(dot_, h) for h in range(2)]
        qms = [_head_lanes(q, h) for h in range(2)]
        doms = [_head_lanes(do, h) for h in range(2)]
        dsum = [jnp.sum(prod[h * SB_HEAD_DIM:(h + 1) * SB_HEAD_DIM, :], axis=0, keepdims=True) for h in range(2)]

        def block(kb, state, past):
            ks = pl.multiple_of(kb * SB_BLOCK, SB_BLOCK)
            kblk = k_ref[pl.ds(ks, SB_BLOCK), :]
            vblk = v_ref[pl.ds(ks, SB_BLOCK), :]
            kt = kt_ref[kb]
            dk_add = jnp.zeros((SB_BLOCK, SB_BLOCK), F32)
            dv_add = jnp.zeros((SB_BLOCK, SB_BLOCK), F32)
            out = []
            for h in range(2):
                dqt, carry_l, carry_g = state[h]
                a, lb, carry_l = _sb_probs(kblk, qts[h], u_mat, carry_l, past)
                ab = a.astype(BF16)
                g = _dot(vblk, dots[h]) * ab.astype(F32)
                before = dsum[h] - (_suffix_sum(u_mat, g) + g + carry_g)
                sig = jnp.exp(lb)
                dz = g * (1.0 - sig) - before * sig
                if past is not None:
                    dz = jnp.where(past, dz, 0.0)
                dzb = (dz * scale).astype(BF16)
                dqt = dqt + _dot(kt[h * SB_HEAD_DIM:(h + 1) * SB_HEAD_DIM, :], dzb)
                dk_add = dk_add + _dot(dzb, qms[h])
                dv_add = dv_add + _dot(ab, doms[h])
                out.append((dqt, carry_l, carry_g + jnp.sum(g, axis=0, keepdims=True)))
            dk_acc[pl.ds(ks, SB_BLOCK), :] += dk_add
            dv_acc[pl.ds(ks, SB_BLOCK), :] += dv_add
            return tuple(out)

        zero = (jnp.zeros((SB_HEAD_DIM, SB_BLOCK), F32), jnp.zeros((1, SB_BLOCK), F32), jnp.zeros((1, SB_BLOCK), F32))
        state = block(qi, (zero, zero), _past_mask())
        state = lax.fori_loop(0, qi, lambda i, st: block(qi - 1 - i, st, None), state)
        dq_ref[...] = jnp.concatenate([state[0][0], state[1][0]], axis=0).T.astype(BF16)

        @pl.when(qi == nq - 1)
        def _():
            dk_ref[...] = dk_acc[...].astype(BF16)
            dv_ref[...] = dv_acc[...].astype(BF16)

    qblk = lambda base: pl.BlockSpec((SB_BLOCK, SB_BLOCK), lambda p, i: (i, base * pairs + p))
    seq = lambda base: pl.BlockSpec((t, SB_BLOCK), lambda p, i: (0, base * pairs + p))
    tr = pl.BlockSpec((None, nq, SB_BLOCK, SB_BLOCK), lambda p, i: (p, 0, 0, 0))
    return pl.pallas_call(
        body, name="sb_bwd",
        grid=(pairs, nq),
        in_specs=[qblk(COL_SQ), seq(COL_SK), seq(COL_SV), tr, tr, qblk(0), qblk(0)],
        out_specs=[qblk(0), seq(0), seq(0)],
        out_shape=[jax.ShapeDtypeStruct((t, D_MODEL), BF16)] * 3,
        scratch_shapes=[pltpu.VMEM((t, SB_BLOCK), F32), pltpu.VMEM((t, SB_BLOCK), F32)],
        compiler_params=pltpu.CompilerParams(dimension_semantics=("parallel", "arbitrary")),
    )(proj, proj, proj, kt4, vt4, do_sb, o_sb)


def _x_bwd(proj, q_norm_g, kn, v, do_x, tm):
    t = proj.shape[0]
    m = kn.shape[0]
    scale = X_HEAD_DIM ** -0.5

    def body(xq_ref, qg_ref, kn_ref, v_ref, do_ref, dxq_ref, dkn_ref, dv_ref, dqg_ref):
        @pl.when(pl.program_id(0) == 0)
        def _():
            dkn_ref[...] = jnp.zeros_like(dkn_ref)
            dv_ref[...] = jnp.zeros_like(dv_ref)
            dqg_ref[...] = jnp.zeros_like(dqg_ref)

        qg = qg_ref[...]
        for h in range(X_HEADS):
            sl, rq, qhat, qn, p = _x_head(xq_ref, qg, kn_ref, h)
            do_h = do_ref[:, sl]
            dp = _dot_nt(do_h, v_ref[:, sl])
            ds = (p * (dp - jnp.sum(dp * p, axis=-1, keepdims=True)) * scale).astype(BF16)
            dqn = _dot(ds, kn_ref[:, sl])
            dkn_ref[:, sl] += _dot_tn(ds, qn)
            dv_ref[:, sl] += _dot_tn(p.astype(BF16), do_h)
            dqg_ref[...] += jnp.sum(dqn * qhat, axis=0, keepdims=True)
            dxq_ref[:, sl] = _rms_bwd(dqn, qhat, rq, qg).astype(BF16)

    full = pl.BlockSpec((m, D_MODEL), lambda i: (0, 0))
    gain = pl.BlockSpec((1, X_HEAD_DIM), lambda i: (0, 0))
    tile = pl.BlockSpec((tm, D_MODEL), lambda i: (i, 0))
    return pl.pallas_call(
        body, name="x_bwd",
        grid=(t // tm,),
        in_specs=[pl.BlockSpec((tm, D_MODEL), lambda i: (i, COL_XQ)), gain, full, full, tile],
        out_specs=[tile, full, full, gain],
        out_shape=[jax.ShapeDtypeStruct((t, D_MODEL), BF16), jax.ShapeDtypeStruct((m, D_MODEL), F32),
                   jax.ShapeDtypeStruct((m, D_MODEL), F32), jax.ShapeDtypeStruct((1, X_HEAD_DIM), F32)],
        compiler_params=pltpu.CompilerParams(dimension_semantics=("arbitrary",)),
    )(proj, q_norm_g, kn, v, do_x)


def _mem_bwd(mem, g_mem, wkv_all, k_norm_g, dkn, dv):
    m, d = mem.shape

    def body(mem_ref, g_ref, w_ref, kg_ref, dkn_ref, dv_ref, dkv_ref, dgm_ref, dkg_ref):
        memf = mem_ref[...]
        mem_hat = memf * _rstd(memf)
        memn = (mem_hat * g_ref[...]).astype(BF16)
        kg = kg_ref[...]
        dmemn = jnp.zeros((m, d), F32)
        dkg = jnp.zeros((1, X_HEAD_DIM), F32)
        for b in range(N_DEV):
            sl = slice(b * X_HEAD_DIM, (b + 1) * X_HEAD_DIM)
            if b < X_HEADS:
                kv = _dot(memn, w_ref[b])
                rk = _rstd(kv)
                khat = kv * rk
                dkn_h = dkn_ref[:, sl]
                dkg = dkg + jnp.sum(dkn_h * khat, axis=0, keepdims=True)
                dblk = _rms_bwd(dkn_h, khat, rk, kg).astype(BF16)
            else:
                hs = slice((b - X_HEADS) * X_HEAD_DIM, (b - X_HEADS + 1) * X_HEAD_DIM)
                dblk = dv_ref[:, hs].astype(BF16)
            dkv_ref[:, sl] = dblk
            dmemn = dmemn + _dot_nt(dblk, w_ref[b])
        dgm_ref[...] = jnp.sum(dmemn * mem_hat, axis=0, keepdims=True)
        dkg_ref[...] = dkg

    return pl.pallas_call(
        body, name="mem_bwd",
        out_shape=[jax.ShapeDtypeStruct((m, 2 * d), BF16), jax.ShapeDtypeStruct((1, d), F32),
                   jax.ShapeDtypeStruct((1, X_HEAD_DIM), F32)],
    )(mem, g_mem, wkv_all, k_norm_g, dkn, dv)


def _in_proj_bwd(x, g_mix, w_in_all, dproj, dx1, tm):
    t, d = x.shape
    nb, _, bw = w_in_all.shape

    def body(x_ref, g_ref, w_ref, dp_ref, dx1_ref, dx_ref, dg_ref, acc_ref):
        i, j = pl.program_id(0), pl.program_id(1)

        @pl.when(j == 0)
        def _():
            acc_ref[...] = jnp.zeros_like(acc_ref)

        @pl.when((i == 0) & (j == 0))
        def _():
            dg_ref[...] = jnp.zeros_like(dg_ref)

        acc_ref[...] += _dot_nt(dp_ref[...], w_ref[...])

        @pl.when(j == nb - 1)
        def _():
            xf = x_ref[...]
            rs = _rstd(xf)
            xhat = xf * rs
            dh = acc_ref[...]
            dg_ref[...] += jnp.sum(dh * xhat, axis=0, keepdims=True)
            dx_ref[...] = dx1_ref[...] + _rms_bwd(dh, xhat, rs, g_ref[...])

    tile = pl.BlockSpec((tm, d), lambda i, j: (i, 0))
    row = pl.BlockSpec((1, d), lambda i, j: (0, 0))
    return pl.pallas_call(
        body, name="in_proj_bwd",
        grid=(t // tm, nb),
        in_specs=[tile, row, pl.BlockSpec((None, d, bw), lambda i, j: (j, 0, 0)),
                  pl.BlockSpec((tm, bw), lambda i, j: (i, j)), tile],
        out_specs=[tile, row],
        out_shape=[jax.ShapeDtypeStruct((t, d), F32), jax.ShapeDtypeStruct((1, d), F32)],
        scratch_shapes=[pltpu.VMEM((tm, d), F32)],
        compiler_params=pltpu.CompilerParams(dimension_semantics=("arbitrary", "arbitrary")),
    )(x, g_mix, w_in_all, dproj, dx1)


def _weight_grad(a, b, bw, tmm, name):
    t, m = a.shape
    n = b.shape[1]
    tmm = min(tmm, m)

    def body(a_ref, b_ref, o_ref):
        o_ref[...] = _dot_tn(a_ref[...].astype(BF16), b_ref[...].astype(BF16)).astype(BF16)

    return pl.pallas_call(
        body, name=name,
        grid=(m // tmm, n // bw),
        in_specs=[pl.BlockSpec((t, tmm), lambda i, j: (0, i)), pl.BlockSpec((t, bw), lambda i, j: (0, j))],
        out_specs=pl.BlockSpec((None, tmm, bw), lambda i, j: (j, i, 0)),
        out_shape=jax.ShapeDtypeStruct((n // bw, m, bw), BF16),
        compiler_params=pltpu.CompilerParams(dimension_semantics=("parallel", "parallel")),
    )(a, b)


def _pair_sum(grad, recv, own_blocks, name):
    _, rows, cols = grad.shape

    def body(idx_ref, g_ref, r_ref, o_ref):
        o_ref[...] = (g_ref[...].astype(F32) + r_ref[...].astype(F32)).astype(BF16)

    return pl.pallas_call(
        body, name=name,
        grid_spec=pltpu.PrefetchScalarGridSpec(
            num_scalar_prefetch=1, grid=(4,),
            in_specs=[pl.BlockSpec((None, rows, cols), lambda r, idx: (idx[r], 0, 0)),
                      pl.BlockSpec((None, rows, cols), lambda r, idx: (r, 0, 0))],
            out_specs=pl.BlockSpec((None, rows, cols), lambda r, idx: (r, 0, 0))),
        out_shape=jax.ShapeDtypeStruct((4, rows, cols), BF16),
        compiler_params=pltpu.CompilerParams(dimension_semantics=("parallel",)),
    )(own_blocks, grad, recv)


def _adamw_math(w, g, m, v):
    m = ADAM_B1 * m + (1.0 - ADAM_B1) * g
    v = ADAM_B2 * v + (1.0 - ADAM_B2) * jnp.square(g)
    m_hat = m / (1.0 - ADAM_B1 ** ADAM_STEP)
    v_hat = v / (1.0 - ADAM_B2 ** ADAM_STEP)
    delta = -ADAM_LR * (m_hat / (jnp.sqrt(v_hat) + ADAM_EPS) + ADAM_WD * w)
    return delta, m, v


def _adamw_sharded(pair_sums, recv, w, m, v, tr, name):
    rows, cols = w.shape
    tr = min(tr, rows)

    def body(h_ref, r_ref, w_ref, m_ref, v_ref, g_out, d_out, m_out, v_out):
        g = h_ref[...].astype(F32)
        for r in range(3):
            g = g + r_ref[r].astype(F32)
        g_out[...] = g
        d_out[...], m_out[...], v_out[...] = _adamw_math(w_ref[...], g, m_ref[...], v_ref[...])

    tile = pl.BlockSpec((tr, cols), lambda i: (i, 0))
    return pl.pallas_call(
        body, name=name,
        grid=(rows // tr,),
        in_specs=[pl.BlockSpec((None, tr, cols), lambda i: (0, i, 0)),
                  pl.BlockSpec((3, tr, cols), lambda i: (0, i, 0)), tile, tile, tile],
        out_specs=[tile] * 4,
        out_shape=[jax.ShapeDtypeStruct((rows, cols), F32)] * 4,
        compiler_params=pltpu.CompilerParams(dimension_semantics=("parallel",)),
    )(pair_sums, recv, w, m, v)


def _small_sum(gathered):
    _, rows, cols = gathered.shape

    def body(g_ref, o_ref):
        total = g_ref[0]
        for dev in range(1, N_DEV):
            total = total + g_ref[dev]
        o_ref[...] = total

    return pl.pallas_call(body, name="small_grad_sum", out_shape=jax.ShapeDtypeStruct((rows, cols), F32))(gathered)


def _adamw_small(w, g, m, v):
    def body(w_ref, g_ref, m_ref, v_ref, d_out, m_out, v_out):
        d_out[...], m_out[...], v_out[...] = _adamw_math(w_ref[...], g_ref[...], m_ref[...], v_ref[...])

    return pl.pallas_call(body, name="adamw_small", out_shape=[jax.ShapeDtypeStruct(w.shape, F32)] * 3)(w, g, m, v)


def _pad_tile(a):
    return jnp.pad(a, ((0, SMALL_TILE - a.shape[0]), (0, D_MODEL - a.shape[1])))


def _pack_small(g_mix, g_mem, g_mlp, q_norm_g, k_norm_g, conv3):
    return jnp.concatenate([_pad_tile(a) for a in (g_mix, g_mem, g_mlp, q_norm_g, k_norm_g, conv3)], axis=0)


def kernel(x, mem, g_mix, g_mem, w_in, conv_w, w_conv_out, w_sb_out, q_norm_g, k_norm_g, w_mem_kv, w_x_out, w_out, g_mlp, w_up, w_down, loss_target, m_g_mix, m_g_mem, m_w_in, m_conv_w, m_w_conv_out, m_w_sb_out, m_q_norm_g, m_k_norm_g, m_w_mem_kv, m_w_x_out, m_w_out, m_g_mlp, m_w_up, m_w_down, v_g_mix, v_g_mem, v_w_in, v_conv_w, v_w_conv_out, v_w_sb_out, v_q_norm_g, v_k_norm_g, v_w_mem_kv, v_w_x_out, v_w_out, v_g_mlp, v_w_up, v_w_down):
    xpos, ypos, cpos = _mesh_pos()
    me = 4 * xpos + 2 * ypos + cpos
    x2d, mem2d, tgt2d = x[0], mem[0], loss_target[0]
    t = x2d.shape[0]
    tm = min(512, t)
    tm_s = min(256, t)

    big = {
        "w_in": (w_in[0], m_w_in[0], v_w_in[0]),
        "w_conv_out": (w_conv_out[0], m_w_conv_out[0], v_w_conv_out[0]),
        "w_sb_out": (w_sb_out[0], m_w_sb_out[0], v_w_sb_out[0]),
        "w_mem_kv": (w_mem_kv[0], m_w_mem_kv[0], v_w_mem_kv[0]),
        "w_x_out": (w_x_out[0], m_w_x_out[0], v_w_x_out[0]),
        "w_out": (w_out[0], m_w_out[0], v_w_out[0]),
        "w_up": (w_up[0], m_w_up[0], v_w_up[0]),
        "w_down": (w_down[0], m_w_down[0], v_w_down[0]),
    }
    names = list(big)
    conv_pad = jnp.pad(conv_w[0], ((0, 8 - 3), (0, 0)))
    gathered = _all_gather([big[n][0].astype(BF16) for n in names] + [conv_pad])
    full = dict(zip(names, gathered[:-1]))
    conv_full = gathered[-1][:, :3, :].transpose(1, 0, 2).reshape(3, D_MODEL)
    w_in_all, wkv_all, w_up_all = full["w_in"], full["w_mem_kv"], full["w_up"]
    rows_full = lambda a: a.reshape(a.shape[0] * a.shape[1], a.shape[2])
    wc, ws, wx, wo, wd = (rows_full(full[n]) for n in ("w_conv_out", "w_sb_out", "w_x_out", "w_out", "w_down"))

    proj, h = _in_proj(x2d, g_mix, w_in_all, tm)
    a_conv = _conv_fwd(proj, conv_full, 256)
    nk = t // SB_BLOCK
    pairs = D_MODEL // SB_BLOCK

    def blocks_t(cols):
        return cols.reshape(nk, SB_BLOCK, pairs, SB_BLOCK).transpose(2, 0, 3, 1)

    kt4 = blocks_t(proj[:, COL_SK * D_MODEL:(COL_SK + 1) * D_MODEL])
    vt4 = blocks_t(proj[:, COL_SV * D_MODEL:(COL_SV + 1) * D_MODEL])
    o_sb = _sb_fwd(proj, vt4)
    mem_n, kn, vmem = _mem_prep(mem2d, g_mem, wkv_all, k_norm_g)
    o_x = _x_fwd(proj, q_norm_g, kn, vmem, tm_s)
    x1, y_conv, y_sb, y_x, merged = _merge_fwd(x2d, proj, a_conv, o_sb, o_x, wc, ws, wx, wo, tm_s)
    up, h2, dx2, lsum = _mlp_fwd(x1, g_mlp, w_up_all, wd, tgt2d, tm)
    loss = lax.psum(0.5 * jnp.sum(lsum) / D_MODEL, ("x", "y", "c"))

    dup, act, dx1, dg_mlp = _mlp_bwd(x1, g_mlp, w_up_all, wd, up, dx2, tm)
    dgate, dy_conv, dy_sb, dy_x, da_conv, do_sb, do_x = _merge_bwd(dx1, proj, y_conv, y_sb, y_x, wc, ws, wx, wo, tm_s)
    dch, dcb, dcc, dconv = _conv_bwd(proj, conv_full, da_conv, 256)
    dq, dk, dv = _sb_bwd(proj, kt4, vt4, do_sb, o_sb)
    dxq, dkn, dvm, dqg = _x_bwd(proj, q_norm_g, kn, vmem, do_x, tm_s)
    dkv, dg_mem, dkg = _mem_bwd(mem2d, g_mem, wkv_all, k_norm_g, dkn, dvm)
    dproj = jnp.concatenate([dch, dcb, dcc, dq, dk, dv, dxq, dgate], axis=1)
    grad_x, dg_mix = _in_proj_bwd(x2d, g_mix, w_in_all, dproj, dx1, tm)

    wgrads = {
        "w_in": _weight_grad(h, dproj, w_in_all.shape[2], 512, "dw_in"),
        "w_conv_out": _weight_grad(a_conv, dy_conv, D_MODEL, 512, "dw_conv_out"),
        "w_sb_out": _weight_grad(o_sb, dy_sb, D_MODEL, 512, "dw_sb_out"),
        "w_mem_kv": _weight_grad(mem_n, dkv, wkv_all.shape[2], 512, "dw_mem_kv"),
        "w_x_out": _weight_grad(o_x, dy_x, D_MODEL, 512, "dw_x_out"),
        "w_out": _weight_grad(merged, dx1, D_MODEL, 512, "dw_out"),
        "w_up": _weight_grad(h2, dup, w_up_all.shape[2], 512, "dw_up"),
        "w_down": _weight_grad(act, dx2, D_MODEL, 512, "dw_down"),
    }
    blocked = [wgrads[n].reshape((N_DEV,) + big[n][0].shape) for n in names]

    from_sibling = _pair_exchange(blocked)
    own_blocks = jnp.stack([4 * (xpos ^ dx) + 2 * (ypos ^ dy) + cpos for dx in (0, 1) for dy in (0, 1)]).astype(jnp.int32)
    pair_sums = [_pair_sum(g, r, own_blocks, "pair_sum_" + n) for n, g, r in zip(names, blocked, from_sibling)]
    from_chips = _chip_exchange(pair_sums)
    res = {}
    for n, hsum, rc in zip(names, pair_sums, from_chips):
        w_sh, m_sh, v_sh = big[n]
        res[n] = _adamw_sharded(hsum, rc, w_sh, m_sh, v_sh, 256, "adamw_" + n)

    part = _pack_small(dg_mix, dg_mem, dg_mlp, dqg, dkg, dconv)
    gsum = _small_sum(_small_all_gather(part))
    conv_cols = lax.dynamic_slice(gsum[5 * SMALL_TILE:], (0, me * (D_MODEL // N_DEV)), (SMALL_TILE, D_MODEL // N_DEV))
    g_small = jnp.concatenate([gsum[:5 * SMALL_TILE], _pad_tile(conv_cols)], axis=0)
    w_small = _pack_small(g_mix, g_mem, g_mlp, q_norm_g, k_norm_g, conv_w[0])
    m_small = _pack_small(m_g_mix, m_g_mem, m_g_mlp, m_q_norm_g, m_k_norm_g, m_conv_w[0])
    v_small = _pack_small(v_g_mix, v_g_mem, v_g_mlp, v_q_norm_g, v_k_norm_g, v_conv_w[0])
    d_small, nm_small, nv_small = _adamw_small(w_small, g_small, m_small, v_small)

    def unpack(p):
        return {"g_mix": p[0:1], "g_mem": p[8:9], "g_mlp": p[16:17], "q_norm_g": p[24:25, :X_HEAD_DIM],
                "k_norm_g": p[32:33, :X_HEAD_DIM], "conv_w": p[40:43, :D_MODEL // N_DEV][None]}

    small = [unpack(p) for p in (g_small, d_small, nm_small, nv_small)]
    order = ["g_mix", "g_mem", "w_in", "conv_w", "w_conv_out", "w_sb_out", "q_norm_g", "k_norm_g", "w_mem_kv",
             "w_x_out", "w_out", "g_mlp", "w_up", "w_down"]
    outs = [loss, grad_x[None]]
    for kind in range(4):
        for n in order:
            outs.append(res[n][kind][None] if n in res else small[kind][n])
    return tuple(outs)
```

```python
import functools

import jax
import jax.numpy as jnp
from jax import lax
from jax.experimental import pallas as pl
from jax.experimental.pallas import tpu as pltpu

F32 = jnp.float32
BF16 = jnp.bfloat16
MESH = pl.DeviceIdType.MESH

EPS = 1e-6
N_DEV = 8
D_MODEL = 1024
SB_HEAD_DIM = 64
SB_BLOCK = 128
SB_QUERY_TILE = 512
X_HEADS = 4
X_HEAD_DIM = 256
N_BRANCH = 3
COL_CH, COL_CB, COL_CC, COL_SQ, COL_SK, COL_SV, COL_XQ, COL_GATE = 0, 1, 2, 3, 4, 5, 6, 7

ADAM_LR = 0.001
ADAM_B1 = 0.9
ADAM_B2 = 0.999
ADAM_EPS = 1e-08
ADAM_WD = 0.01
ADAM_STEP = 10

SMALL_TILE = 8


def _dot(a, b):
    return jnp.dot(a, b, preferred_element_type=F32)


def _dot_nt(a, b):
    return lax.dot_general(a, b, (((1,), (1,)), ((), ())), preferred_element_type=F32)


def _dot_tn(a, b):
    return lax.dot_general(a, b, (((0,), (0,)), ((), ())), preferred_element_type=F32)


def _rstd(xf):
    return lax.rsqrt(jnp.mean(xf * xf, axis=-1, keepdims=True) + EPS)


def _sigmoid(z):
    return 1.0 / (1.0 + jnp.exp(-z))


def _log_sigmoid(z):
    return jnp.minimum(z, 0.0) - jnp.log(1.0 + jnp.exp(-jnp.abs(z)))


def _rms_bwd(dy, xhat, r, g):
    dxhat = dy * g
    return r * (dxhat - xhat * jnp.mean(dxhat * xhat, axis=-1, keepdims=True))


def _mesh_pos():
    return lax.axis_index("x"), lax.axis_index("y"), lax.axis_index("c")


ANY = pl.BlockSpec(memory_space=pl.ANY)


def _all_gather(shards):
    n = len(shards)

    def body(*refs):
        ins, outs = refs[:n], refs[n:2 * n]
        send_sems, recv_sems, local_sems = refs[2 * n:]
        x, y, c = _mesh_pos()
        me, sibling = (x, y, c), (x, y, 1 - c)
        chips = [(1 - x, y), (x, 1 - y), (1 - x, 1 - y)]

        def blk(a, px, py, pc):
            return outs[a].at[4 * px + 2 * py + pc]

        def copy(a, k, block, to, src=None):
            return pltpu.make_async_remote_copy(
                src_ref=blk(a, *block) if src is None else src, dst_ref=blk(a, *block),
                send_sem=send_sems.at[a, k], recv_sem=recv_sems.at[a, k], device_id=to, device_id_type=MESH)

        mine = [pltpu.make_async_copy(ins[a], blk(a, *me), local_sems.at[a]) for a in range(n)]
        for cp in mine:
            cp.start()
        first = []
        for a in range(n):
            first.append(copy(a, 0, me, sibling, src=ins[a]))
            first += [copy(a, 1 + j, me, (*chip, c), src=ins[a]) for j, chip in enumerate(chips)]
        for cp in first:
            cp.start()
        passed = []
        for j, chip in enumerate(chips):
            for a in range(n):
                copy(a, 1 + j, (*chip, c), me).wait_recv()
                fwd = copy(a, 4 + j, (*chip, c), sibling)
                fwd.start()
                passed.append(fwd)
        for a in range(n):
            copy(a, 0, sibling, me).wait_recv()
            for j, chip in enumerate(chips):
                copy(a, 4 + j, (*chip, 1 - c), me).wait_recv()
        for cp in first + passed:
            cp.wait_send()
        for cp in mine:
            cp.wait()

    return pl.pallas_call(
        body, name="weights_all_gather",
        out_shape=[jax.ShapeDtypeStruct((N_DEV,) + s.shape, s.dtype) for s in shards],
        in_specs=[ANY] * n, out_specs=[ANY] * n,
        scratch_shapes=[pltpu.SemaphoreType.DMA((n, 7)), pltpu.SemaphoreType.DMA((n, 7)),
                        pltpu.SemaphoreType.DMA((n,))],
    )(*shards)


def _pair_exchange(grads):
    n = len(grads)

    def body(*refs):
        ins, outs = refs[:n], refs[n:2 * n]
        send_sems, recv_sems = refs[2 * n:]
        x, y, c = _mesh_pos()
        xs, ys = (x, 1 - x), (y, 1 - y)
        copies = []
        for a in range(n):
            for r in range(4):
                dx, dy = divmod(r, 2)
                copies.append(pltpu.make_async_remote_copy(
                    src_ref=ins[a].at[4 * xs[dx] + 2 * ys[dy] + (1 - c)], dst_ref=outs[a].at[r],
                    send_sem=send_sems.at[a, r], recv_sem=recv_sems.at[a, r],
                    device_id=(x, y, 1 - c), device_id_type=MESH))
        for cp in copies:
            cp.start()
        for cp in copies:
            cp.wait()

    return pl.pallas_call(
        body, name="grad_pair_exchange",
        out_shape=[jax.ShapeDtypeStruct((4,) + g.shape[1:], g.dtype) for g in grads],
        in_specs=[ANY] * n, out_specs=[ANY] * n,
        scratch_shapes=[pltpu.SemaphoreType.DMA((n, 4)), pltpu.SemaphoreType.DMA((n, 4))],
    )(*grads)


def _chip_exchange(sums):
    n = len(sums)

    def body(*refs):
        ins, outs = refs[:n], refs[n:2 * n]
        send_sems, recv_sems = refs[2 * n:]
        x, y, c = _mesh_pos()
        xs, ys = (x, 1 - x), (y, 1 - y)
        copies = []
        for a in range(n):
            for r in range(1, 4):
                dx, dy = divmod(r, 2)
                copies.append(pltpu.make_async_remote_copy(
                    src_ref=ins[a].at[r], dst_ref=outs[a].at[r - 1],
                    send_sem=send_sems.at[a, r - 1], recv_sem=recv_sems.at[a, r - 1],
                    device_id=(xs[dx], ys[dy], c), device_id_type=MESH))
        for cp in copies:
            cp.start()
        for cp in copies:
            cp.wait()

    return pl.pallas_call(
        body, name="grad_chip_exchange",
        out_shape=[jax.ShapeDtypeStruct((3,) + s.shape[1:], s.dtype) for s in sums],
        in_specs=[ANY] * n, out_specs=[ANY] * n,
        scratch_shapes=[pltpu.SemaphoreType.DMA((n, 3)), pltpu.SemaphoreType.DMA((n, 3))],
    )(*sums)


def _small_all_gather(part):
    rows, cols = part.shape

    def body(in_ref, out_ref, send_sems, recv_sems):
        x, y, c = _mesh_pos()
        xs, ys, cs = (x, 1 - x), (y, 1 - y), (c, 1 - c)
        out_ref[4 * x + 2 * y + c] = in_ref[...]
        copies = []
        for k in range(1, N_DEV):
            dx, dy, dc = k // 4, (k // 2) % 2, k % 2
            copies.append((
                pltpu.make_async_remote_copy(
                    src_ref=in_ref, dst_ref=out_ref.at[4 * x + 2 * y + c],
                    send_sem=send_sems.at[k - 1], recv_sem=recv_sems.at[k - 1],
                    device_id=(xs[dx], ys[dy], cs[dc]), device_id_type=MESH),
                pltpu.make_async_remote_copy(
                    src_ref=in_ref, dst_ref=out_ref.at[4 * xs[dx] + 2 * ys[dy] + cs[dc]],
                    send_sem=send_sems.at[k - 1], recv_sem=recv_sems.at[k - 1],
                    device_id=(xs[dx], ys[dy], cs[dc]), device_id_type=MESH)))
        for send, _ in copies:
            send.start()
        for send, recv in copies:
            recv.wait_recv()
            send.wait_send()

    return pl.pallas_call(
        body, name="small_all_gather",
        out_shape=jax.ShapeDtypeStruct((N_DEV, rows, cols), part.dtype),
        in_specs=[pl.BlockSpec(memory_space=pltpu.VMEM)],
        out_specs=pl.BlockSpec(memory_space=pltpu.VMEM),
        scratch_shapes=[pltpu.SemaphoreType.DMA((N_DEV - 1,)), pltpu.SemaphoreType.DMA((N_DEV - 1,))],
    )(part)


def _in_proj(x, g_mix, w_in_all, tm):
    t, d = x.shape
    nb, _, bw = w_in_all.shape

    def body(x_ref, g_ref, w_ref, proj_ref, h_ref):
        @pl.when(pl.program_id(1) == 0)
        def _():
            xf = x_ref[...]
            h_ref[...] = (xf * _rstd(xf) * g_ref[...]).astype(BF16)

        proj_ref[...] = _dot(h_ref[...], w_ref[...]).astype(BF16)

    return pl.pallas_call(
        body, name="in_proj",
        grid=(t // tm, nb),
        in_specs=[pl.BlockSpec((tm, d), lambda i, j: (i, 0)),
                  pl.BlockSpec((1, d), lambda i, j: (0, 0)),
                  pl.BlockSpec((None, d, bw), lambda i, j: (j, 0, 0))],
        out_specs=[pl.BlockSpec((tm, bw), lambda i, j: (i, j)),
                   pl.BlockSpec((tm, d), lambda i, j: (i, 0))],
        out_shape=[jax.ShapeDtypeStruct((t, nb * bw), BF16), jax.ShapeDtypeStruct((t, d), BF16)],
        compiler_params=pltpu.CompilerParams(dimension_semantics=("parallel", "arbitrary")),
    )(x, g_mix, w_in_all)


def _conv_terms(ch_ref, cb_ref, cc_ref, w_ref):
    ch, cb, cc = ch_ref[...].astype(F32), cb_ref[...].astype(F32), cc_ref[...].astype(F32)
    u = cc * ch
    row = lax.broadcasted_iota(jnp.int32, u.shape, 0)
    u1 = jnp.where(row >= 1, pltpu.roll(u, 1, 0), 0.0)
    u2 = jnp.where(row >= 2, pltpu.roll(u, 2, 0), 0.0)
    w = (w_ref[0:1, :], w_ref[1:2, :], w_ref[2:3, :])
    cv = w[2] * u + w[1] * u1 + w[0] * u2
    return ch, cb, cc, u, u1, u2, cv, w, row


def _conv_fwd(proj, conv_w, cw):
    t = proj.shape[0]
    nper = D_MODEL // cw

    def body(ch_ref, cb_ref, cc_ref, w_ref, a_ref):
        _, cb, _, _, _, _, cv, _, _ = _conv_terms(ch_ref, cb_ref, cc_ref, w_ref)
        a_ref[...] = (cb * cv).astype(BF16)

    def col(piece):
        return pl.BlockSpec((t, cw), lambda j: (0, piece * nper + j))

    return pl.pallas_call(
        body, name="conv_fwd",
        grid=(nper,),
        in_specs=[col(COL_CH), col(COL_CB), col(COL_CC), pl.BlockSpec((3, cw), lambda j: (0, j))],
        out_specs=pl.BlockSpec((t, cw), lambda j: (0, j)),
        out_shape=jax.ShapeDtypeStruct((t, D_MODEL), BF16),
        compiler_params=pltpu.CompilerParams(dimension_semantics=("parallel",)),
    )(proj, proj, proj, conv_w)


def _scan_matrix():
    s = lax.broadcasted_iota(jnp.int32, (SB_BLOCK, SB_BLOCK), 0)
    j = lax.broadcasted_iota(jnp.int32, (SB_BLOCK, SB_BLOCK), 1)
    return jnp.where(j > s, 1.0, 0.0).astype(BF16)


def _suffix_sum(u_mat, xv):
    hi = xv.astype(BF16)
    lo = (xv - hi.astype(F32)).astype(BF16)
    return _dot(u_mat, hi) + _dot(u_mat, lo)


def _head_rows(vt, h):
    row = lax.broadcasted_iota(jnp.int32, vt.shape, 0)
    return jnp.where((row >= h * SB_HEAD_DIM) & (row < (h + 1) * SB_HEAD_DIM), vt, 0.0).astype(BF16)


def _head_lanes(v, h):
    lane = lax.broadcasted_iota(jnp.int32, v.shape, 1)
    return jnp.where((lane >= h * SB_HEAD_DIM) & (lane < (h + 1) * SB_HEAD_DIM), v, 0.0).astype(BF16)


def _sb_probs(kblk, qt_h, u_mat, carry, past):
    z = _dot(kblk, qt_h) * (SB_HEAD_DIM ** -0.5)
    lb = _log_sigmoid(z)
    l1 = lb - z
    if past is not None:
        l1 = jnp.where(past, l1, 0.0)
    a = jnp.exp(lb + _suffix_sum(u_mat, l1) + carry)
    if past is not None:
        a = jnp.where(past, a, 0.0)
    return a, lb, carry + jnp.sum(l1, axis=0, keepdims=True)


def _sb_sweep(qi, tq, block, state):
    per = tq // SB_BLOCK
    lead = qi * per
    diff = (lax.broadcasted_iota(jnp.int32, (SB_BLOCK, tq), 0) - lax.broadcasted_iota(jnp.int32, (SB_BLOCK, tq), 1))

    def crossing(i, st):
        off = per - 1 - i
        return block(lead + off, st, diff < -off * SB_BLOCK)

    state = lax.fori_loop(0, per, crossing, state)
    return lax.fori_loop(0, lead, lambda i, st: block(lead - 1 - i, st, None), state)


def _sb_fwd(proj, vt4, tq):
    t = proj.shape[0]
    nk = t // SB_BLOCK
    pairs = D_MODEL // SB_BLOCK
    per = tq // SB_BLOCK

    def body(q_ref, k_ref, vt_ref, o_ref):
        qi = pl.program_id(1)
        qt = q_ref[...].astype(F32).T
        qts = [_head_rows(qt, h) for h in range(2)]
        u_mat = _scan_matrix()

        def block(kb, state, past):
            ks = pl.multiple_of(kb * SB_BLOCK, SB_BLOCK)
            kblk = k_ref[pl.ds(ks, SB_BLOCK), :]
            vt = vt_ref[kb]
            out = []
            for h in range(2):
                acc, carry = state[h]
                a, _, carry = _sb_probs(kblk, qts[h], u_mat, carry, past)
                acc = acc + _dot(vt[h * SB_HEAD_DIM:(h + 1) * SB_HEAD_DIM, :], a.astype(BF16))
                out.append((acc, carry))
            return tuple(out)

        zero = (jnp.zeros((SB_HEAD_DIM, tq), F32), jnp.zeros((1, tq), F32))
        state = _sb_sweep(qi, tq, block, (zero, zero))
        o_ref[...] = jnp.concatenate([state[0][0], state[1][0]], axis=0).T

    return pl.pallas_call(
        body, name="sb_fwd",
        grid=(pairs, t // tq),
        in_specs=[pl.BlockSpec((tq, SB_BLOCK), lambda p, i: (i, COL_SQ * pairs + p)),
                  pl.BlockSpec((t, SB_BLOCK), lambda p, i: (0, COL_SK * pairs + p)),
                  pl.BlockSpec((None, nk, SB_BLOCK, SB_BLOCK), lambda p, i: (p, 0, 0, 0))],
        out_specs=pl.BlockSpec((tq, SB_BLOCK), lambda p, i: (i, p)),
        out_shape=jax.ShapeDtypeStruct((t, D_MODEL), F32),
        compiler_params=pltpu.CompilerParams(dimension_semantics=("parallel", "parallel")),
    )(proj, proj, vt4)


def _mem_prep(mem, g_mem, wkv_all, k_norm_g):
    m, d = mem.shape

    def body(mem_ref, g_ref, w_ref, kg_ref, memn_ref, kn_ref, v_ref):
        memf = mem_ref[...]
        memn = (memf * _rstd(memf) * g_ref[...]).astype(BF16)
        memn_ref[...] = memn
        for b in range(N_DEV):
            kv = _dot(memn, w_ref[b])
            if b < X_HEADS:
                kn_ref[:, b * X_HEAD_DIM:(b + 1) * X_HEAD_DIM] = (kv * _rstd(kv) * kg_ref[...]).astype(BF16)
            else:
                h = b - X_HEADS
                v_ref[:, h * X_HEAD_DIM:(h + 1) * X_HEAD_DIM] = kv.astype(BF16)

    return pl.pallas_call(
        body, name="mem_prep",
        out_shape=[jax.ShapeDtypeStruct((m, d), BF16)] * 3,
    )(mem, g_mem, wkv_all, k_norm_g)


def _x_head(xq_ref, qg, kn_ref, h):
    sl = slice(h * X_HEAD_DIM, (h + 1) * X_HEAD_DIM)
    q = xq_ref[:, sl].astype(F32)
    rq = _rstd(q)
    qhat = q * rq
    qn = (qhat * qg).astype(BF16)
    s = _dot_nt(qn, kn_ref[:, sl]) * (X_HEAD_DIM ** -0.5)
    e = jnp.exp(s - jnp.max(s, axis=-1, keepdims=True))
    p = e / jnp.sum(e, axis=-1, keepdims=True)
    return sl, rq, qhat, qn, p


def _x_fwd(proj, q_norm_g, kn, v, tm):
    t = proj.shape[0]
    m = kn.shape[0]

    def body(xq_ref, qg_ref, kn_ref, v_ref, o_ref):
        for h in range(X_HEADS):
            sl, _, _, _, p = _x_head(xq_ref, qg_ref[...], kn_ref, h)
            o_ref[:, sl] = _dot(p.astype(BF16), v_ref[:, sl]).astype(BF16)

    return pl.pallas_call(
        body, name="x_fwd",
        grid=(t // tm,),
        in_specs=[pl.BlockSpec((tm, D_MODEL), lambda i: (i, COL_XQ)),
                  pl.BlockSpec((1, X_HEAD_DIM), lambda i: (0, 0)),
                  pl.BlockSpec((m, D_MODEL), lambda i: (0, 0)),
                  pl.BlockSpec((m, D_MODEL), lambda i: (0, 0))],
        out_specs=pl.BlockSpec((tm, D_MODEL), lambda i: (i, 0)),
        out_shape=jax.ShapeDtypeStruct((t, D_MODEL), BF16),
        compiler_params=pltpu.CompilerParams(dimension_semantics=("parallel",)),
    )(proj, q_norm_g, kn, v)


def _gate_spec(tm, branch):
    return pl.BlockSpec((tm, D_MODEL), lambda i: (i, COL_GATE + branch))


def _merge_fwd(x, proj, a_conv, o_sb, o_x, w_conv_out, w_sb_out, w_x_out, w_out, tm):
    t, d = x.shape

    def body(x_ref, g0_ref, g1_ref, g2_ref, a_ref, s_ref, xo_ref, wc_ref, ws_ref, wx_ref, wo_ref,
             x1_ref, yc_ref, ys_ref, yx_ref, mg_ref):
        merged = jnp.zeros((tm, d), F32)
        for gate_ref, b_ref, w_ref, y_ref in ((g0_ref, a_ref, wc_ref, yc_ref), (g1_ref, s_ref, ws_ref, ys_ref),
                                              (g2_ref, xo_ref, wx_ref, yx_ref)):
            yv = _dot(b_ref[...].astype(BF16), w_ref[...])
            y_ref[...] = yv.astype(BF16)
            merged = merged + _sigmoid(gate_ref[...].astype(F32)) * yv
        mb = merged.astype(BF16)
        mg_ref[...] = mb
        x1_ref[...] = x_ref[...] + _dot(mb, wo_ref[...])

    tile = pl.BlockSpec((tm, d), lambda i: (i, 0))
    wfull = pl.BlockSpec((d, d), lambda i: (0, 0))
    return pl.pallas_call(
        body, name="merge_fwd",
        grid=(t // tm,),
        in_specs=[tile] + [_gate_spec(tm, b) for b in range(N_BRANCH)] + [tile, tile, tile,
                                                                           wfull, wfull, wfull, wfull],
        out_specs=[tile] * 5,
        out_shape=[jax.ShapeDtypeStruct((t, d), F32)] + [jax.ShapeDtypeStruct((t, d), BF16)] * 4,
        compiler_params=pltpu.CompilerParams(dimension_semantics=("parallel",)),
    )(x, proj, proj, proj, a_conv, o_sb, o_x, w_conv_out, w_sb_out, w_x_out, w_out)


def _mlp_fwd(x1, g_mlp, w_up_all, w_down, target, tm):
    t, d = x1.shape
    nb, _, fw = w_up_all.shape

    def body(x1_ref, g_ref, wu_ref, wd_ref, tgt_ref, up_ref, h2_ref, dx2_ref, lsum_ref, acc_ref):
        i, j = pl.program_id(0), pl.program_id(1)

        @pl.when(j == 0)
        def _():
            xf = x1_ref[...]
            h2_ref[...] = (xf * _rstd(xf) * g_ref[...]).astype(BF16)
            acc_ref[...] = jnp.zeros_like(acc_ref)

        @pl.when((i == 0) & (j == 0))
        def _():
            lsum_ref[...] = jnp.zeros_like(lsum_ref)

        up = _dot(h2_ref[...], wu_ref[...])
        up_ref[...] = up.astype(BF16)
        act = jnp.square(jnp.maximum(up, 0.0)).astype(BF16)
        acc_ref[...] += _dot(act, wd_ref[...])

        @pl.when(j == nb - 1)
        def _():
            diff = x1_ref[...] + acc_ref[...] - tgt_ref[...]
            dx2_ref[...] = diff * (1.0 / d)
            lsum_ref[...] += jnp.sum(diff * diff, axis=0, keepdims=True)

    tile = pl.BlockSpec((tm, d), lambda i, j: (i, 0))
    row = pl.BlockSpec((1, d), lambda i, j: (0, 0))
    return pl.pallas_call(
        body, name="mlp_fwd",
        grid=(t // tm, nb),
        in_specs=[tile, row, pl.BlockSpec((None, d, fw), lambda i, j: (j, 0, 0)),
                  pl.BlockSpec((fw, d), lambda i, j: (j, 0)), tile],
        out_specs=[pl.BlockSpec((tm, fw), lambda i, j: (i, j)), tile, tile, row],
        out_shape=[jax.ShapeDtypeStruct((t, nb * fw), BF16), jax.ShapeDtypeStruct((t, d), BF16),
                   jax.ShapeDtypeStruct((t, d), F32), jax.ShapeDtypeStruct((1, d), F32)],
        scratch_shapes=[pltpu.VMEM((tm, d), F32)],
        compiler_params=pltpu.CompilerParams(dimension_semantics=("arbitrary", "arbitrary")),
    )(x1, g_mlp, w_up_all, w_down, target)


def _mlp_bwd(x1, g_mlp, w_up_all, w_down, up, dx2, tm):
    t, d = x1.shape
    nb, _, fw = w_up_all.shape

    def body(x1_ref, g_ref, wu_ref, wd_ref, up_ref, dx2_ref, dup_ref, act_ref, dx1_ref, dg_ref, acc_ref, dyb_ref):
        i, j = pl.program_id(0), pl.program_id(1)

        @pl.when(j == 0)
        def _():
            dyb_ref[...] = dx2_ref[...].astype(BF16)
            acc_ref[...] = jnp.zeros_like(acc_ref)

        @pl.when((i == 0) & (j == 0))
        def _():
            dg_ref[...] = jnp.zeros_like(dg_ref)

        r = jnp.maximum(up_ref[...].astype(F32), 0.0)
        act_ref[...] = jnp.square(r).astype(BF16)
        dup = (_dot_nt(dyb_ref[...], wd_ref[...]) * (2.0 * r)).astype(BF16)
        dup_ref[...] = dup
        acc_ref[...] += _dot_nt(dup, wu_ref[...])

        @pl.when(j == nb - 1)
        def _():
            xf = x1_ref[...]
            rs = _rstd(xf)
            xhat = xf * rs
            dh2 = acc_ref[...]
            dg_ref[...] += jnp.sum(dh2 * xhat, axis=0, keepdims=True)
            dx1_ref[...] = dx2_ref[...] + _rms_bwd(dh2, xhat, rs, g_ref[...])

    tile = pl.BlockSpec((tm, d), lambda i, j: (i, 0))
    row = pl.BlockSpec((1, d), lambda i, j: (0, 0))
    ff = pl.BlockSpec((tm, fw), lambda i, j: (i, j))
    return pl.pallas_call(
        body, name="mlp_bwd",
        grid=(t // tm, nb),
        in_specs=[tile, row, pl.BlockSpec((None, d, fw), lambda i, j: (j, 0, 0)),
                  pl.BlockSpec((fw, d), lambda i, j: (j, 0)), ff, tile],
        out_specs=[ff, ff, tile, row],
        out_shape=[jax.ShapeDtypeStruct((t, nb * fw), BF16), jax.ShapeDtypeStruct((t, nb * fw), BF16),
                   jax.ShapeDtypeStruct((t, d), F32), jax.ShapeDtypeStruct((1, d), F32)],
        scratch_shapes=[pltpu.VMEM((tm, d), F32), pltpu.VMEM((tm, d), BF16)],
        compiler_params=pltpu.CompilerParams(dimension_semantics=("arbitrary", "arbitrary")),
    )(x1, g_mlp, w_up_all, w_down, up, dx2)


def _merge_bwd(dx1, proj, y_conv, y_sb, y_x, w_conv_out, w_sb_out, w_x_out, w_out, tm):
    t, d = dx1.shape

    def body(dx1_ref, g0_ref, g1_ref, g2_ref, yc_ref, ys_ref, yx_ref, wc_ref, ws_ref, wx_ref, wo_ref,
             dgate_ref, dyc_ref, dys_ref, dyx_ref, da_ref, dos_ref, dox_ref):
        dm = _dot_nt(dx1_ref[...].astype(BF16), wo_ref[...])
        for i, (gate_ref, y_ref, w_ref, dy_ref, db_ref) in enumerate(((g0_ref, yc_ref, wc_ref, dyc_ref, da_ref),
                                                                       (g1_ref, ys_ref, ws_ref, dys_ref, dos_ref),
                                                                       (g2_ref, yx_ref, wx_ref, dyx_ref, dox_ref))):
            gt = _sigmoid(gate_ref[...].astype(F32))
            dy = (dm * gt).astype(BF16)
            dy_ref[...] = dy
            dgate_ref[:, i * d:(i + 1) * d] = (dm * y_ref[...].astype(F32) * gt * (1.0 - gt)).astype(BF16)
            db_ref[...] = _dot_nt(dy, w_ref[...]).astype(BF16)

    tile = pl.BlockSpec((tm, d), lambda i: (i, 0))
    wfull = pl.BlockSpec((d, d), lambda i: (0, 0))
    return pl.pallas_call(
        body, name="merge_bwd",
        grid=(t // tm,),
        in_specs=[tile] + [_gate_spec(tm, b) for b in range(N_BRANCH)] + [tile, tile, tile,
                                                                           wfull, wfull, wfull, wfull],
        out_specs=[pl.BlockSpec((tm, N_BRANCH * d), lambda i: (i, 0))] + [tile] * 6,
        out_shape=[jax.ShapeDtypeStruct((t, N_BRANCH * d), BF16)] + [jax.ShapeDtypeStruct((t, d), BF16)] * 6,
        compiler_params=pltpu.CompilerParams(dimension_semantics=("parallel",)),
    )(dx1, proj, proj, proj, y_conv, y_sb, y_x, w_conv_out, w_sb_out, w_x_out, w_out)


def _conv_bwd(proj, conv_w, da, cw):
    t = proj.shape[0]
    nper = D_MODEL // cw

    def body(ch_ref, cb_ref, cc_ref, w_ref, da_ref, dch_ref, dcb_ref, dcc_ref, dw_ref):
        ch, cb, cc, u, u1, u2, cv, w, row = _conv_terms(ch_ref, cb_ref, cc_ref, w_ref)
        dav = da_ref[...].astype(F32)
        dcb_ref[...] = (dav * cv).astype(BF16)
        dcv = dav * cb
        n1 = jnp.where(row < t - 1, pltpu.roll(dcv, t - 1, 0), 0.0)
        n2 = jnp.where(row < t - 2, pltpu.roll(dcv, t - 2, 0), 0.0)
        du = w[2] * dcv + w[1] * n1 + w[0] * n2
        dcc_ref[...] = (du * ch).astype(BF16)
        dch_ref[...] = (du * cc).astype(BF16)
        dw_ref[0:1, :] = jnp.sum(dcv * u2, axis=0, keepdims=True)
        dw_ref[1:2, :] = jnp.sum(dcv * u1, axis=0, keepdims=True)
        dw_ref[2:3, :] = jnp.sum(dcv * u, axis=0, keepdims=True)

    def col(piece):
        return pl.BlockSpec((t, cw), lambda j: (0, piece * nper + j))

    out_col = pl.BlockSpec((t, cw), lambda j: (0, j))
    wspec = pl.BlockSpec((3, cw), lambda j: (0, j))
    return pl.pallas_call(
        body, name="conv_bwd",
        grid=(nper,),
        in_specs=[col(COL_CH), col(COL_CB), col(COL_CC), wspec, out_col],
        out_specs=[out_col, out_col, out_col, wspec],
        out_shape=[jax.ShapeDtypeStruct((t, D_MODEL), BF16)] * 3 + [jax.ShapeDtypeStruct((3, D_MODEL), F32)],
        compiler_params=pltpu.CompilerParams(dimension_semantics=("parallel",)),
    )(proj, proj, proj, conv_w, da)


def _sb_bwd(proj, kt4, vt4, do_sb, o_sb, tq):
    t = proj.shape[0]
    nq = t // tq
    pairs = D_MODEL // SB_BLOCK
    scale = SB_HEAD_DIM ** -0.5

    def body(q_ref, k_ref, v_ref, kt_ref, vt_ref, do_ref, o_ref, dq_ref, dk_ref, dv_ref, dk_acc, dv_acc):
        qi = pl.program_id(1)

        @pl.when(qi == 0)
        def _():
            dk_acc[...] = jnp.zeros_like(dk_acc)
            dv_acc[...] = jnp.zeros_like(dv_acc)

        q = q_ref[...].astype(F32)
        do = do_ref[...].astype(F32)
        qt, dot_ = q.T, do.T
        prod = dot_ * o_ref[...].T
        u_mat = _scan_matrix()
        qts = [_head_rows(qt, h) for h in range(2)]
        dots = [_head_rows(dot_, h) for h in range(2)]
        qms = [_head_lanes(q, h) for h in range(2)]
        doms = [_head_lanes(do, h) for h in range(2)]
        dsum = [jnp.sum(prod[h * SB_HEAD_DIM:(h + 1) * SB_HEAD_DIM, :], axis=0, keepdims=True) for h in range(2)]

        def block(kb, state, past):
            ks = pl.multiple_of(kb * SB_BLOCK, SB_BLOCK)
            kblk = k_ref[pl.ds(ks, SB_BLOCK), :]
            vblk = v_ref[pl.ds(ks, SB_BLOCK), :]
            kt = kt_ref[kb]
            dk_add = jnp.zeros((SB_BLOCK, SB_BLOCK), F32)
            dv_add = jnp.zeros((SB_BLOCK, SB_BLOCK), F32)
            out = []
            for h in range(2):
                dqt, carry_l, carry_g = state[h]
                a, lb, carry_l = _sb_probs(kblk, qts[h], u_mat, carry_l, past)
                ab = a.astype(BF16)
                g = _dot(vblk, dots[h]) * ab.astype(F32)
                before = dsum[h] - (_suffix_sum(u_mat, g) + g + carry_g)
                sig = jnp.exp(lb)
                dz = g * (1.0 - sig) - before * sig
                if past is not None:
                    dz = jnp.where(past, dz, 0.0)
                dzb = (dz * scale).astype(BF16)
                dqt = dqt + _dot(kt[h * SB_HEAD_DIM:(h + 1) * SB_HEAD_DIM, :], dzb)
                dk_add = dk_add + _dot(dzb, qms[h])
                dv_add = dv_add + _dot(ab, doms[h])
                out.append((dqt, carry_l, carry_g + jnp.sum(g, axis=0, keepdims=True)))
            dk_acc[pl.ds(ks, SB_BLOCK), :] += dk_add
            dv_acc[pl.ds(ks, SB_BLOCK), :] += dv_add
            return tuple(out)

        zero = (jnp.zeros((SB_HEAD_DIM, tq), F32), jnp.zeros((1, tq), F32), jnp.zeros((1, tq), F32))
        state = _sb_sweep(qi, tq, block, (zero, zero))
        dq_ref[...] = jnp.concatenate([state[0][0], state[1][0]], axis=0).T.astype(BF16)

        @pl.when(qi == nq - 1)
        def _():
            dk_ref[...] = dk_acc[...].astype(BF16)
            dv_ref[...] = dv_acc[...].astype(BF16)

    qblk = lambda base: pl.BlockSpec((tq, SB_BLOCK), lambda p, i: (i, base * pairs + p))
    seq = lambda base: pl.BlockSpec((t, SB_BLOCK), lambda p, i: (0, base * pairs + p))
    tr = pl.BlockSpec((None, t // SB_BLOCK, SB_BLOCK, SB_BLOCK), lambda p, i: (p, 0, 0, 0))
    return pl.pallas_call(
        body, name="sb_bwd",
        grid=(pairs, nq),
        in_specs=[qblk(COL_SQ), seq(COL_SK), seq(COL_SV), tr, tr, qblk(0), qblk(0)],
        out_specs=[qblk(0), seq(0), seq(0)],
        out_shape=[jax.ShapeDtypeStruct((t, D_MODEL), BF16)] * 3,
        scratch_shapes=[pltpu.VMEM((t, SB_BLOCK), F32), pltpu.VMEM((t, SB_BLOCK), F32)],
        compiler_params=pltpu.CompilerParams(dimension_semantics=("parallel", "arbitrary")),
    )(proj, proj, proj, kt4, vt4, do_sb, o_sb)


def _x_bwd(proj, q_norm_g, kn, v, do_x, tm):
    t = proj.shape[0]
    m = kn.shape[0]
    scale = X_HEAD_DIM ** -0.5

    def body(xq_ref, qg_ref, kn_ref, v_ref, do_ref, dxq_ref, dkn_ref, dv_ref, dqg_ref):
        @pl.when(pl.program_id(0) == 0)
        def _():
            dkn_ref[...] = jnp.zeros_like(dkn_ref)
            dv_ref[...] = jnp.zeros_like(dv_ref)
            dqg_ref[...] = jnp.zeros_like(dqg_ref)

        qg = qg_ref[...]
        for h in range(X_HEADS):
            sl, rq, qhat, qn, p = _x_head(xq_ref, qg, kn_ref, h)
            do_h = do_ref[:, sl]
            dp = _dot_nt(do_h, v_ref[:, sl])
            ds = (p * (dp - jnp.sum(dp * p, axis=-1, keepdims=True)) * scale).astype(BF16)
            dqn = _dot(ds, kn_ref[:, sl])
            dkn_ref[:, sl] += _dot_tn(ds, qn)
            dv_ref[:, sl] += _dot_tn(p.astype(BF16), do_h)
            dqg_ref[...] += jnp.sum(dqn * qhat, axis=0, keepdims=True)
            dxq_ref[:, sl] = _rms_bwd(dqn, qhat, rq, qg).astype(BF16)

    full = pl.BlockSpec((m, D_MODEL), lambda i: (0, 0))
    gain = pl.BlockSpec((1, X_HEAD_DIM), lambda i: (0, 0))
    tile = pl.BlockSpec((tm, D_MODEL), lambda i: (i, 0))
    return pl.pallas_call(
        body, name="x_bwd",
        grid=(t // tm,),
        in_specs=[pl.BlockSpec((tm, D_MODEL), lambda i: (i, COL_XQ)), gain, full, full, tile],
        out_specs=[tile, full, full, gain],
        out_shape=[jax.ShapeDtypeStruct((t, D_MODEL), BF16), jax.ShapeDtypeStruct((m, D_MODEL), F32),
                   jax.ShapeDtypeStruct((m, D_MODEL), F32), jax.ShapeDtypeStruct((1, X_HEAD_DIM), F32)],
        compiler_params=pltpu.CompilerParams(dimension_semantics=("arbitrary",)),
    )(proj, q_norm_g, kn, v, do_x)


def _mem_bwd(mem, g_mem, wkv_all, k_norm_g, dkn, dv):
    m, d = mem.shape

    def body(mem_ref, g_ref, w_ref, kg_ref, dkn_ref, dv_ref, dkv_ref, dgm_ref, dkg_ref):
        memf = mem_ref[...]
        mem_hat = memf * _rstd(memf)
        memn = (mem_hat * g_ref[...]).astype(BF16)
        kg = kg_ref[...]
        dmemn = jnp.zeros((m, d), F32)
        dkg = jnp.zeros((1, X_HEAD_DIM), F32)
        for b in range(N_DEV):
            sl = slice(b * X_HEAD_DIM, (b + 1) * X_HEAD_DIM)
            if b < X_HEADS:
                kv = _dot(memn, w_ref[b])
                rk = _rstd(kv)
                khat = kv * rk
                dkn_h = dkn_ref[:, sl]
                dkg = dkg + jnp.sum(dkn_h * khat, axis=0, keepdims=True)
                dblk = _rms_bwd(dkn_h, khat, rk, kg).astype(BF16)
            else:
                hs = slice((b - X_HEADS) * X_HEAD_DIM, (b - X_HEADS + 1) * X_HEAD_DIM)
                dblk = dv_ref[:, hs].astype(BF16)
            dkv_ref[:, sl] = dblk
            dmemn = dmemn + _dot_nt(dblk, w_ref[b])
        dgm_ref[...] = jnp.sum(dmemn * mem_hat, axis=0, keepdims=True)
        dkg_ref[...] = dkg

    return pl.pallas_call(
        body, name="mem_bwd",
        out_shape=[jax.ShapeDtypeStruct((m, 2 * d), BF16), jax.ShapeDtypeStruct((1, d), F32),
                   jax.ShapeDtypeStruct((1, X_HEAD_DIM), F32)],
    )(mem, g_mem, wkv_all, k_norm_g, dkn, dv)


def _in_proj_bwd(x, g_mix, w_in_all, dproj, dx1, tm):
    t, d = x.shape
    nb, _, bw = w_in_all.shape

    def body(x_ref, g_ref, w_ref, dp_ref, dx1_ref, dx_ref, dg_ref, acc_ref):
        i, j = pl.program_id(0), pl.program_id(1)

        @pl.when(j == 0)
        def _():
            acc_ref[...] = jnp.zeros_like(acc_ref)

        @pl.when((i == 0) & (j == 0))
        def _():
            dg_ref[...] = jnp.zeros_like(dg_ref)

        acc_ref[...] += _dot_nt(dp_ref[...], w_ref[...])

        @pl.when(j == nb - 1)
        def _():
            xf = x_ref[...]
            rs = _rstd(xf)
            xhat = xf * rs
            dh = acc_ref[...]
            dg_ref[...] += jnp.sum(dh * xhat, axis=0, keepdims=True)
            dx_ref[...] = dx1_ref[...] + _rms_bwd(dh, xhat, rs, g_ref[...])

    tile = pl.BlockSpec((tm, d), lambda i, j: (i, 0))
    row = pl.BlockSpec((1, d), lambda i, j: (0, 0))
    return pl.pallas_call(
        body, name="in_proj_bwd",
        grid=(t // tm, nb),
        in_specs=[tile, row, pl.BlockSpec((None, d, bw), lambda i, j: (j, 0, 0)),
                  pl.BlockSpec((tm, bw), lambda i, j: (i, j)), tile],
        out_specs=[tile, row],
        out_shape=[jax.ShapeDtypeStruct((t, d), F32), jax.ShapeDtypeStruct((1, d), F32)],
        scratch_shapes=[pltpu.VMEM((tm, d), F32)],
        compiler_params=pltpu.CompilerParams(dimension_semantics=("arbitrary", "arbitrary")),
    )(x, g_mix, w_in_all, dproj, dx1)


def _weight_grad(a, b, bw, tmm, name):
    t, m = a.shape
    n = b.shape[1]
    tmm = min(tmm, m)

    def body(a_ref, b_ref, o_ref):
        o_ref[...] = _dot_tn(a_ref[...].astype(BF16), b_ref[...].astype(BF16)).astype(BF16)

    return pl.pallas_call(
        body, name=name,
        grid=(m // tmm, n // bw),
        in_specs=[pl.BlockSpec((t, tmm), lambda i, j: (0, i)), pl.BlockSpec((t, bw), lambda i, j: (0, j))],
        out_specs=pl.BlockSpec((None, tmm, bw), lambda i, j: (j, i, 0)),
        out_shape=jax.ShapeDtypeStruct((n // bw, m, bw), BF16),
        compiler_params=pltpu.CompilerParams(dimension_semantics=("parallel", "parallel")),
    )(a, b)


def _pair_sum(grad, recv, own_blocks, name):
    _, rows, cols = grad.shape

    def body(idx_ref, g_ref, r_ref, o_ref):
        o_ref[...] = (g_ref[...].astype(F32) + r_ref[...].astype(F32)).astype(BF16)

    return pl.pallas_call(
        body, name=name,
        grid_spec=pltpu.PrefetchScalarGridSpec(
            num_scalar_prefetch=1, grid=(4,),
            in_specs=[pl.BlockSpec((None, rows, cols), lambda r, idx: (idx[r], 0, 0)),
                      pl.BlockSpec((None, rows, cols), lambda r, idx: (r, 0, 0))],
            out_specs=pl.BlockSpec((None, rows, cols), lambda r, idx: (r, 0, 0))),
        out_shape=jax.ShapeDtypeStruct((4, rows, cols), BF16),
        compiler_params=pltpu.CompilerParams(dimension_semantics=("parallel",)),
    )(own_blocks, grad, recv)


def _adamw_math(w, g, m, v):
    m = ADAM_B1 * m + (1.0 - ADAM_B1) * g
    v = ADAM_B2 * v + (1.0 - ADAM_B2) * jnp.square(g)
    m_hat = m / (1.0 - ADAM_B1 ** ADAM_STEP)
    v_hat = v / (1.0 - ADAM_B2 ** ADAM_STEP)
    delta = -ADAM_LR * (m_hat / (jnp.sqrt(v_hat) + ADAM_EPS) + ADAM_WD * w)
    return delta, m, v


def _adamw_sharded(pair_sums, recv, w, m, v, tr, name):
    rows, cols = w.shape
    tr = min(tr, rows)

    def body(h_ref, r_ref, w_ref, m_ref, v_ref, g_out, d_out, m_out, v_out):
        g = h_ref[...].astype(F32)
        for r in range(3):
            g = g + r_ref[r].astype(F32)
        g_out[...] = g
        d_out[...], m_out[...], v_out[...] = _adamw_math(w_ref[...], g, m_ref[...], v_ref[...])

    tile = pl.BlockSpec((tr, cols), lambda i: (i, 0))
    return pl.pallas_call(
        body, name=name,
        grid=(rows // tr,),
        in_specs=[pl.BlockSpec((None, tr, cols), lambda i: (0, i, 0)),
                  pl.BlockSpec((3, tr, cols), lambda i: (0, i, 0)), tile, tile, tile],
        out_specs=[tile] * 4,
        out_shape=[jax.ShapeDtypeStruct((rows, cols), F32)] * 4,
        compiler_params=pltpu.CompilerParams(dimension_semantics=("parallel",)),
    )(pair_sums, recv, w, m, v)


def _small_sum(gathered):
    _, rows, cols = gathered.shape

    def body(g_ref, o_ref):
        total = g_ref[0]
        for dev in range(1, N_DEV):
            total = total + g_ref[dev]
        o_ref[...] = total

    return pl.pallas_call(body, name="small_grad_sum", out_shape=jax.ShapeDtypeStruct((rows, cols), F32))(gathered)


def _adamw_small(w, g, m, v):
    def body(w_ref, g_ref, m_ref, v_ref, d_out, m_out, v_out):
        d_out[...], m_out[...], v_out[...] = _adamw_math(w_ref[...], g_ref[...], m_ref[...], v_ref[...])

    return pl.pallas_call(body, name="adamw_small", out_shape=[jax.ShapeDtypeStruct(w.shape, F32)] * 3)(w, g, m, v)


def _pad_tile(a):
    return jnp.pad(a, ((0, SMALL_TILE - a.shape[0]), (0, D_MODEL - a.shape[1])))


def _pack_small(g_mix, g_mem, g_mlp, q_norm_g, k_norm_g, conv3):
    return jnp.concatenate([_pad_tile(a) for a in (g_mix, g_mem, g_mlp, q_norm_g, k_norm_g, conv3)], axis=0)


def kernel(x, mem, g_mix, g_mem, w_in, conv_w, w_conv_out, w_sb_out, q_norm_g, k_norm_g, w_mem_kv, w_x_out, w_out, g_mlp, w_up, w_down, loss_target, m_g_mix, m_g_mem, m_w_in, m_conv_w, m_w_conv_out, m_w_sb_out, m_q_norm_g, m_k_norm_g, m_w_mem_kv, m_w_x_out, m_w_out, m_g_mlp, m_w_up, m_w_down, v_g_mix, v_g_mem, v_w_in, v_conv_w, v_w_conv_out, v_w_sb_out, v_q_norm_g, v_k_norm_g, v_w_mem_kv, v_w_x_out, v_w_out, v_g_mlp, v_w_up, v_w_down):
    xpos, ypos, cpos = _mesh_pos()
    me = 4 * xpos + 2 * ypos + cpos
    x2d, mem2d, tgt2d = x[0], mem[0], loss_target[0]
    t = x2d.shape[0]
    tm = min(512, t)
    tm_s = min(256, t)

    big = {
        "w_in": (w_in[0], m_w_in[0], v_w_in[0]),
        "w_conv_out": (w_conv_out[0], m_w_conv_out[0], v_w_conv_out[0]),
        "w_sb_out": (w_sb_out[0], m_w_sb_out[0], v_w_sb_out[0]),
        "w_mem_kv": (w_mem_kv[0], m_w_mem_kv[0], v_w_mem_kv[0]),
        "w_x_out": (w_x_out[0], m_w_x_out[0], v_w_x_out[0]),
        "w_out": (w_out[0], m_w_out[0], v_w_out[0]),
        "w_up": (w_up[0], m_w_up[0], v_w_up[0]),
        "w_down": (w_down[0], m_w_down[0], v_w_down[0]),
    }
    names = list(big)
    conv_pad = jnp.pad(conv_w[0], ((0, 8 - 3), (0, 0)))
    gathered = _all_gather([big[n][0].astype(BF16) for n in names] + [conv_pad])
    full = dict(zip(names, gathered[:-1]))
    conv_full = gathered[-1][:, :3, :].transpose(1, 0, 2).reshape(3, D_MODEL)
    w_in_all, wkv_all, w_up_all = full["w_in"], full["w_mem_kv"], full["w_up"]
    rows_full = lambda a: a.reshape(a.shape[0] * a.shape[1], a.shape[2])
    wc, ws, wx, wo, wd = (rows_full(full[n]) for n in ("w_conv_out", "w_sb_out", "w_x_out", "w_out", "w_down"))

    proj, h = _in_proj(x2d, g_mix, w_in_all, tm)
    a_conv = _conv_fwd(proj, conv_full, 256)
    nk = t // SB_BLOCK
    pairs = D_MODEL // SB_BLOCK

    def blocks_t(cols):
        return cols.reshape(nk, SB_BLOCK, pairs, SB_BLOCK).transpose(2, 0, 3, 1)

    kt4 = blocks_t(proj[:, COL_SK * D_MODEL:(COL_SK + 1) * D_MODEL])
    vt4 = blocks_t(proj[:, COL_SV * D_MODEL:(COL_SV + 1) * D_MODEL])
    tq = min(SB_QUERY_TILE, t)
    o_sb = _sb_fwd(proj, vt4, tq)
    mem_n, kn, vmem = _mem_prep(mem2d, g_mem, wkv_all, k_norm_g)
    o_x = _x_fwd(proj, q_norm_g, kn, vmem, tm_s)
    x1, y_conv, y_sb, y_x, merged = _merge_fwd(x2d, proj, a_conv, o_sb, o_x, wc, ws, wx, wo, tm_s)
    up, h2, dx2, lsum = _mlp_fwd(x1, g_mlp, w_up_all, wd, tgt2d, tm)
    loss = lax.psum(0.5 * jnp.sum(lsum) / D_MODEL, ("x", "y", "c"))

    dup, act, dx1, dg_mlp = _mlp_bwd(x1, g_mlp, w_up_all, wd, up, dx2, tm)
    dgate, dy_conv, dy_sb, dy_x, da_conv, do_sb, do_x = _merge_bwd(dx1, proj, y_conv, y_sb, y_x, wc, ws, wx, wo, tm_s)
    dch, dcb, dcc, dconv = _conv_bwd(proj, conv_full, da_conv, 256)
    dq, dk, dv = _sb_bwd(proj, kt4, vt4, do_sb, o_sb, tq)
    dxq, dkn, dvm, dqg = _x_bwd(proj, q_norm_g, kn, vmem, do_x, tm_s)
    dkv, dg_mem, dkg = _mem_bwd(mem2d, g_mem, wkv_all, k_norm_g, dkn, dvm)
    dproj = jnp.concatenate([dch, dcb, dcc, dq, dk, dv, dxq, dgate], axis=1)
    grad_x, dg_mix = _in_proj_bwd(x2d, g_mix, w_in_all, dproj, dx1, tm)

    wgrads = {
        "w_in": _weight_grad(h, dproj, w_in_all.shape[2], 512, "dw_in"),
        "w_conv_out": _weight_grad(a_conv, dy_conv, D_MODEL, 512, "dw_conv_out"),
        "w_sb_out": _weight_grad(o_sb, dy_sb, D_MODEL, 512, "dw_sb_out"),
        "w_mem_kv": _weight_grad(mem_n, dkv, wkv_all.shape[2], 512, "dw_mem_kv"),
        "w_x_out": _weight_grad(o_x, dy_x, D_MODEL, 512, "dw_x_out"),
        "w_out": _weight_grad(merged, dx1, D_MODEL, 512, "dw_out"),
        "w_up": _weight_grad(h2, dup, w_up_all.shape[2], 512, "dw_up"),
        "w_down": _weight_grad(act, dx2, D_MODEL, 512, "dw_down"),
    }
    blocked = [wgrads[n].reshape((N_DEV,) + big[n][0].shape) for n in names]

    from_sibling = _pair_exchange(blocked)
    own_blocks = jnp.stack([4 * (xpos ^ dx) + 2 * (ypos ^ dy) + cpos for dx in (0, 1) for dy in (0, 1)]).astype(jnp.int32)
    pair_sums = [_pair_sum(g, r, own_blocks, "pair_sum_" + n) for n, g, r in zip(names, blocked, from_sibling)]
    from_chips = _chip_exchange(pair_sums)
    res = {}
    for n, hsum, rc in zip(names, pair_sums, from_chips):
        w_sh, m_sh, v_sh = big[n]
        res[n] = _adamw_sharded(hsum, rc, w_sh, m_sh, v_sh, 256, "adamw_" + n)

    part = _pack_small(dg_mix, dg_mem, dg_mlp, dqg, dkg, dconv)
    gsum = _small_sum(_small_all_gather(part))
    conv_cols = lax.dynamic_slice(gsum[5 * SMALL_TILE:], (0, me * (D_MODEL // N_DEV)), (SMALL_TILE, D_MODEL // N_DEV))
    g_small = jnp.concatenate([gsum[:5 * SMALL_TILE], _pad_tile(conv_cols)], axis=0)
    w_small = _pack_small(g_mix, g_mem, g_mlp, q_norm_g, k_norm_g, conv_w[0])
    m_small = _pack_small(m_g_mix, m_g_mem, m_g_mlp, m_q_norm_g, m_k_norm_g, m_conv_w[0])
    v_small = _pack_small(v_g_mix, v_g_mem, v_g_mlp, v_q_norm_g, v_k_norm_g, v_conv_w[0])
    d_small, nm_small, nv_small = _adamw_small(w_small, g_small, m_small, v_small)

    def unpack(p):
        return {"g_mix": p[0:1], "g_mem": p[8:9], "g_mlp": p[16:17], "q_norm_g": p[24:25, :X_HEAD_DIM],
                "k_norm_g": p[32:33, :X_HEAD_DIM], "conv_w": p[40:43, :D_MODEL // N_DEV][None]}

    small = [unpack(p) for p in (g_small, d_small, nm_small, nv_small)]
    order = ["g_mix", "g_mem", "w_in", "conv_w", "w_conv_out", "w_sb_out", "q_norm_g", "k_norm_g", "w_mem_kv",
             "w_x_out", "w_out", "g_mlp", "w_up", "w_down"]
    outs = [loss, grad_x[None]]
    for kind in range(4):
        for n in order:
            outs.append(res[n][kind][None] if n in res else small[kind][n])
    return tuple(outs)
```

```python
import functools

import jax
import jax.numpy as jnp
from jax import lax
from jax.experimental import pallas as pl
from jax.experimental.pallas import tpu as pltpu

F32 = jnp.float32
BF16 = jnp.bfloat16
MESH = pl.DeviceIdType.MESH

EPS = 1e-6
N_DEV = 8
D_MODEL = 1024
SB_HEAD_DIM = 64
SB_BLOCK = 128
SB_QUERY_TILE = 512
X_HEADS = 4
X_HEAD_DIM = 256
N_BRANCH = 3
COL_CH, COL_CB, COL_CC, COL_SQ, COL_SK, COL_SV, COL_XQ, COL_GATE = 0, 1, 2, 3, 4, 5, 6, 7

ADAM_LR = 0.001
ADAM_B1 = 0.9
ADAM_B2 = 0.999
ADAM_EPS = 1e-08
ADAM_WD = 0.01
ADAM_STEP = 10

SMALL_TILE = 8


def _dot(a, b):
    return jnp.dot(a, b, preferred_element_type=F32)


def _dot_nt(a, b):
    return lax.dot_general(a, b, (((1,), (1,)), ((), ())), preferred_element_type=F32)


def _dot_tn(a, b):
    return lax.dot_general(a, b, (((0,), (0,)), ((), ())), preferred_element_type=F32)


def _rstd(xf):
    return lax.rsqrt(jnp.mean(xf * xf, axis=-1, keepdims=True) + EPS)


def _sigmoid(z):
    return 1.0 / (1.0 + jnp.exp(-z))


def _log_sigmoid(z):
    return jnp.minimum(z, 0.0) - jnp.log(1.0 + jnp.exp(-jnp.abs(z)))


def _rms_bwd(dy, xhat, r, g):
    dxhat = dy * g
    return r * (dxhat - xhat * jnp.mean(dxhat * xhat, axis=-1, keepdims=True))


def _mesh_pos():
    return lax.axis_index("x"), lax.axis_index("y"), lax.axis_index("c")


ANY = pl.BlockSpec(memory_space=pl.ANY)


def _all_gather(shards):
    n = len(shards)

    def body(*refs):
        begin, relay, finish = _gather_phases(refs[:n], refs[n:2 * n], *refs[2 * n:])
        begin()
        relay()
        finish()

    return pl.pallas_call(
        body, name="weights_all_gather",
        out_shape=_gather_shapes(shards),
        in_specs=[ANY] * n, out_specs=[ANY] * n,
        scratch_shapes=_gather_sems(n),
    )(*shards)


def _gather_shapes(shards):
    return [jax.ShapeDtypeStruct((N_DEV,) + s.shape, s.dtype) for s in shards]


def _gather_sems(n):
    return [pltpu.SemaphoreType.DMA((n, 7)), pltpu.SemaphoreType.DMA((n, 7)), pltpu.SemaphoreType.DMA((n,))]


def _gather_phases(ins, outs, send_sems, recv_sems, local_sems):
    n = len(ins)
    x, y, c = _mesh_pos()
    me, sibling = (x, y, c), (x, y, 1 - c)
    chips = [(1 - x, y), (x, 1 - y), (1 - x, 1 - y)]

    def blk(a, px, py, pc):
        return outs[a].at[4 * px + 2 * py + pc]

    def copy(a, k, block, to, src=None):
        return pltpu.make_async_remote_copy(
            src_ref=blk(a, *block) if src is None else src, dst_ref=blk(a, *block),
            send_sem=send_sems.at[a, k], recv_sem=recv_sems.at[a, k], device_id=to, device_id_type=MESH)

    def local(a):
        return pltpu.make_async_copy(ins[a], blk(a, *me), local_sems.at[a])

    def own(a):
        return [copy(a, 0, me, sibling, src=ins[a])] + [copy(a, 1 + j, me, (*chip, c), src=ins[a])
                                                        for j, chip in enumerate(chips)]

    def begin():
        for a in range(n):
            local(a).start()
        for a in range(n):
            for cp in own(a):
                cp.start()

    def relay():
        for j, chip in enumerate(chips):
            for a in range(n):
                copy(a, 1 + j, (*chip, c), me).wait_recv()
                copy(a, 4 + j, (*chip, c), sibling).start()

    def finish():
        for a in range(n):
            copy(a, 0, sibling, me).wait_recv()
            for j, chip in enumerate(chips):
                copy(a, 4 + j, (*chip, 1 - c), me).wait_recv()
        for a in range(n):
            for cp in own(a):
                cp.wait_send()
            for j, chip in enumerate(chips):
                copy(a, 4 + j, (*chip, c), sibling).wait_send()
            local(a).wait()

    return begin, relay, finish


def _pair_exchange(grads, name):
    n = len(grads)

    def body(*refs):
        ins, outs = refs[:n], refs[n:2 * n]
        send_sems, recv_sems = refs[2 * n:]
        x, y, c = _mesh_pos()
        xs, ys = (x, 1 - x), (y, 1 - y)
        copies = []
        for a in range(n):
            for r in range(4):
                dx, dy = divmod(r, 2)
                copies.append(pltpu.make_async_remote_copy(
                    src_ref=ins[a].at[4 * xs[dx] + 2 * ys[dy] + (1 - c)], dst_ref=outs[a].at[r],
                    send_sem=send_sems.at[a, r], recv_sem=recv_sems.at[a, r],
                    device_id=(x, y, 1 - c), device_id_type=MESH))
        for cp in copies:
            cp.start()
        for cp in copies:
            cp.wait()

    return pl.pallas_call(
        body, name=name,
        out_shape=[jax.ShapeDtypeStruct((4,) + g.shape[1:], g.dtype) for g in grads],
        in_specs=[ANY] * n, out_specs=[ANY] * n,
        scratch_shapes=[pltpu.SemaphoreType.DMA((n, 4)), pltpu.SemaphoreType.DMA((n, 4))],
    )(*grads)


def _chip_exchange(sums):
    n = len(sums)

    def body(*refs):
        begin, finish = _chip_exchange_phases(refs[:n], refs[n:2 * n], *refs[2 * n:])
        begin()
        finish()

    return pl.pallas_call(
        body, name="grad_chip_exchange",
        out_shape=_chip_exchange_shapes(sums),
        in_specs=[ANY] * n, out_specs=[ANY] * n,
        scratch_shapes=_chip_exchange_sems(n),
    )(*sums)


def _chip_exchange_shapes(sums):
    return [jax.ShapeDtypeStruct((3,) + s.shape[1:], s.dtype) for s in sums]


def _chip_exchange_sems(n):
    return [pltpu.SemaphoreType.DMA((n, 3)), pltpu.SemaphoreType.DMA((n, 3))]


def _chip_exchange_phases(ins, outs, send_sems, recv_sems):
    x, y, c = _mesh_pos()
    xs, ys = (x, 1 - x), (y, 1 - y)

    def copies():
        out = []
        for a in range(len(ins)):
            for r in range(1, 4):
                dx, dy = divmod(r, 2)
                out.append(pltpu.make_async_remote_copy(
                    src_ref=ins[a].at[r], dst_ref=outs[a].at[r - 1],
                    send_sem=send_sems.at[a, r - 1], recv_sem=recv_sems.at[a, r - 1],
                    device_id=(xs[dx], ys[dy], c), device_id_type=MESH))
        return out

    def begin():
        for cp in copies():
            cp.start()

    def finish():
        for cp in copies():
            cp.wait()

    return begin, finish


def _small_all_gather(part):
    rows, cols = part.shape

    def body(in_ref, out_ref, send_sems, recv_sems):
        x, y, c = _mesh_pos()
        xs, ys, cs = (x, 1 - x), (y, 1 - y), (c, 1 - c)
        out_ref[4 * x + 2 * y + c] = in_ref[...]
        copies = []
        for k in range(1, N_DEV):
            dx, dy, dc = k // 4, (k // 2) % 2, k % 2
            copies.append((
                pltpu.make_async_remote_copy(
                    src_ref=in_ref, dst_ref=out_ref.at[4 * x + 2 * y + c],
                    send_sem=send_sems.at[k - 1], recv_sem=recv_sems.at[k - 1],
                    device_id=(xs[dx], ys[dy], cs[dc]), device_id_type=MESH),
                pltpu.make_async_remote_copy(
                    src_ref=in_ref, dst_ref=out_ref.at[4 * xs[dx] + 2 * ys[dy] + cs[dc]],
                    send_sem=send_sems.at[k - 1], recv_sem=recv_sems.at[k - 1],
                    device_id=(xs[dx], ys[dy], cs[dc]), device_id_type=MESH)))
        for send, _ in copies:
            send.start()
        for send, recv in copies:
            recv.wait_recv()
            send.wait_send()

    return pl.pallas_call(
        body, name="small_all_gather",
        out_shape=jax.ShapeDtypeStruct((N_DEV, rows, cols), part.dtype),
        in_specs=[pl.BlockSpec(memory_space=pltpu.VMEM)],
        out_specs=pl.BlockSpec(memory_space=pltpu.VMEM),
        scratch_shapes=[pltpu.SemaphoreType.DMA((N_DEV - 1,)), pltpu.SemaphoreType.DMA((N_DEV - 1,))],
    )(part)


def _in_proj(x, g_mix, w_in_all, tm):
    t, d = x.shape
    nb, _, bw = w_in_all.shape

    def body(x_ref, g_ref, w_ref, proj_ref, h_ref):
        @pl.when(pl.program_id(1) == 0)
        def _():
            xf = x_ref[...]
            h_ref[...] = (xf * _rstd(xf) * g_ref[...]).astype(BF16)

        proj_ref[...] = _dot(h_ref[...], w_ref[...]).astype(BF16)

    return pl.pallas_call(
        body, name="in_proj",
        grid=(t // tm, nb),
        in_specs=[pl.BlockSpec((tm, d), lambda i, j: (i, 0)),
                  pl.BlockSpec((1, d), lambda i, j: (0, 0)),
                  pl.BlockSpec((None, d, bw), lambda i, j: (j, 0, 0))],
        out_specs=[pl.BlockSpec((tm, bw), lambda i, j: (i, j)),
                   pl.BlockSpec((tm, d), lambda i, j: (i, 0))],
        out_shape=[jax.ShapeDtypeStruct((t, nb * bw), BF16), jax.ShapeDtypeStruct((t, d), BF16)],
        compiler_params=pltpu.CompilerParams(dimension_semantics=("parallel", "arbitrary")),
    )(x, g_mix, w_in_all)


def _conv_terms(ch_ref, cb_ref, cc_ref, w_ref):
    ch, cb, cc = ch_ref[...].astype(F32), cb_ref[...].astype(F32), cc_ref[...].astype(F32)
    u = cc * ch
    row = lax.broadcasted_iota(jnp.int32, u.shape, 0)
    u1 = jnp.where(row >= 1, pltpu.roll(u, 1, 0), 0.0)
    u2 = jnp.where(row >= 2, pltpu.roll(u, 2, 0), 0.0)
    w = (w_ref[0:1, :], w_ref[1:2, :], w_ref[2:3, :])
    cv = w[2] * u + w[1] * u1 + w[0] * u2
    return ch, cb, cc, u, u1, u2, cv, w, row


def _conv_fwd(proj, conv_w, cw):
    t = proj.shape[0]
    nper = D_MODEL // cw

    def body(ch_ref, cb_ref, cc_ref, w_ref, a_ref):
        _, cb, _, _, _, _, cv, _, _ = _conv_terms(ch_ref, cb_ref, cc_ref, w_ref)
        a_ref[...] = (cb * cv).astype(BF16)

    def col(piece):
        return pl.BlockSpec((t, cw), lambda j: (0, piece * nper + j))

    return pl.pallas_call(
        body, name="conv_fwd",
        grid=(nper,),
        in_specs=[col(COL_CH), col(COL_CB), col(COL_CC), pl.BlockSpec((3, cw), lambda j: (0, j))],
        out_specs=pl.BlockSpec((t, cw), lambda j: (0, j)),
        out_shape=jax.ShapeDtypeStruct((t, D_MODEL), BF16),
        compiler_params=pltpu.CompilerParams(dimension_semantics=("parallel",)),
    )(proj, proj, proj, conv_w)


def _scan_matrix():
    s = lax.broadcasted_iota(jnp.int32, (SB_BLOCK, SB_BLOCK), 0)
    j = lax.broadcasted_iota(jnp.int32, (SB_BLOCK, SB_BLOCK), 1)
    return jnp.where(j > s, 1.0, 0.0).astype(BF16)


def _suffix_sum(u_mat, xv):
    hi = xv.astype(BF16)
    lo = (xv - hi.astype(F32)).astype(BF16)
    return _dot(u_mat, hi) + _dot(u_mat, lo)


def _head_rows(vt, h):
    row = lax.broadcasted_iota(jnp.int32, vt.shape, 0)
    return jnp.where((row >= h * SB_HEAD_DIM) & (row < (h + 1) * SB_HEAD_DIM), vt, 0.0).astype(BF16)


def _head_lanes(v, h):
    lane = lax.broadcasted_iota(jnp.int32, v.shape, 1)
    return jnp.where((lane >= h * SB_HEAD_DIM) & (lane < (h + 1) * SB_HEAD_DIM), v, 0.0).astype(BF16)


def _sb_probs(kblk, qt_h, u_mat, carry, past):
    z = _dot(kblk, qt_h)
    lb = _log_sigmoid(z)
    l1 = lb - z
    if past is not None:
        l1 = jnp.where(past, l1, 0.0)
    a = jnp.exp(lb + _suffix_sum(u_mat, l1) + carry)
    if past is not None:
        a = jnp.where(past, a, 0.0)
    return a, lb, carry + jnp.sum(l1, axis=0, keepdims=True)


def _sb_sweep(qi, tq, block, state):
    per = tq // SB_BLOCK
    lead = qi * per
    diff = (lax.broadcasted_iota(jnp.int32, (SB_BLOCK, tq), 0) - lax.broadcasted_iota(jnp.int32, (SB_BLOCK, tq), 1))

    def crossing(i, st):
        off = per - 1 - i
        return block(lead + off, st, diff < -off * SB_BLOCK)

    state = lax.fori_loop(0, per, crossing, state)
    return lax.fori_loop(0, lead, lambda i, st: block(lead - 1 - i, st, None), state)


def _sb_fwd(proj, vt4, tq, shards):
    t = proj.shape[0]
    nk = t // SB_BLOCK
    pairs = D_MODEL // SB_BLOCK
    nq = t // tq
    n = len(shards)

    def body(q_ref, k_ref, vt_ref, *rest):
        o_ref = rest[n]
        begin, relay, finish = _gather_phases(rest[:n], rest[n + 1:2 * n + 1], *rest[2 * n + 1:])
        pi, qi = pl.program_id(0), pl.program_id(1)
        pl.when((pi == 0) & (qi == 0))(begin)
        pl.when((pi == pairs // 2) & (qi == 0))(relay)
        qt = q_ref[...].astype(F32).T * (SB_HEAD_DIM ** -0.5)
        qts = [_head_rows(qt, h) for h in range(2)]
        u_mat = _scan_matrix()

        def block(kb, state, past):
            ks = pl.multiple_of(kb * SB_BLOCK, SB_BLOCK)
            kblk = k_ref[pl.ds(ks, SB_BLOCK), :]
            vt = vt_ref[kb]
            out = []
            for h in range(2):
                acc, carry = state[h]
                a, _, carry = _sb_probs(kblk, qts[h], u_mat, carry, past)
                acc = acc + _dot(vt[h * SB_HEAD_DIM:(h + 1) * SB_HEAD_DIM, :], a.astype(BF16))
                out.append((acc, carry))
            return tuple(out)

        zero = (jnp.zeros((SB_HEAD_DIM, tq), F32), jnp.zeros((1, tq), F32))
        state = _sb_sweep(qi, tq, block, (zero, zero))
        o_ref[...] = jnp.concatenate([state[0][0], state[1][0]], axis=0).T
        pl.when((pi == pairs - 1) & (qi == nq - 1))(finish)

    outs = pl.pallas_call(
        body, name="sb_fwd",
        grid=(pairs, nq),
        in_specs=[pl.BlockSpec((tq, SB_BLOCK), lambda p, i: (i, COL_SQ * pairs + p)),
                  pl.BlockSpec((t, SB_BLOCK), lambda p, i: (0, COL_SK * pairs + p)),
                  pl.BlockSpec((None, nk, SB_BLOCK, SB_BLOCK), lambda p, i: (p, 0, 0, 0))] + [ANY] * n,
        out_specs=[pl.BlockSpec((tq, SB_BLOCK), lambda p, i: (i, p))] + [ANY] * n,
        out_shape=[jax.ShapeDtypeStruct((t, D_MODEL), F32)] + _gather_shapes(shards),
        scratch_shapes=_gather_sems(n),
        compiler_params=pltpu.CompilerParams(dimension_semantics=("arbitrary", "arbitrary")),
    )(proj, proj, vt4, *shards)
    return outs[0], outs[1:]


def _mem_prep(mem, g_mem, wkv_all, k_norm_g):
    m, d = mem.shape

    def body(mem_ref, g_ref, w_ref, kg_ref, memn_ref, kn_ref, v_ref):
        memf = mem_ref[...]
        memn = (memf * _rstd(memf) * g_ref[...]).astype(BF16)
        memn_ref[...] = memn
        for b in range(N_DEV):
            kv = _dot(memn, w_ref[b])
            if b < X_HEADS:
                kn_ref[:, b * X_HEAD_DIM:(b + 1) * X_HEAD_DIM] = (kv * _rstd(kv) * kg_ref[...]).astype(BF16)
            else:
                h = b - X_HEADS
                v_ref[:, h * X_HEAD_DIM:(h + 1) * X_HEAD_DIM] = kv.astype(BF16)

    return pl.pallas_call(
        body, name="mem_prep",
        out_shape=[jax.ShapeDtypeStruct((m, d), BF16)] * 3,
    )(mem, g_mem, wkv_all, k_norm_g)


def _x_head(xq_ref, qg, kn_ref, h):
    sl = slice(h * X_HEAD_DIM, (h + 1) * X_HEAD_DIM)
    q = xq_ref[:, sl].astype(F32)
    rq = _rstd(q)
    qhat = q * rq
    qn = (qhat * qg).astype(BF16)
    s = _dot_nt(qn, kn_ref[:, sl]) * (X_HEAD_DIM ** -0.5)
    e = jnp.exp(s - jnp.max(s, axis=-1, keepdims=True))
    p = e / jnp.sum(e, axis=-1, keepdims=True)
    return sl, rq, qhat, qn, p


def _x_fwd(proj, q_norm_g, kn, v, tm):
    t = proj.shape[0]
    m = kn.shape[0]

    def body(xq_ref, qg_ref, kn_ref, v_ref, o_ref):
        for h in range(X_HEADS):
            sl, _, _, _, p = _x_head(xq_ref, qg_ref[...], kn_ref, h)
            o_ref[:, sl] = _dot(p.astype(BF16), v_ref[:, sl]).astype(BF16)

    return pl.pallas_call(
        body, name="x_fwd",
        grid=(t // tm,),
        in_specs=[pl.BlockSpec((tm, D_MODEL), lambda i: (i, COL_XQ)),
                  pl.BlockSpec((1, X_HEAD_DIM), lambda i: (0, 0)),
                  pl.BlockSpec((m, D_MODEL), lambda i: (0, 0)),
                  pl.BlockSpec((m, D_MODEL), lambda i: (0, 0))],
        out_specs=pl.BlockSpec((tm, D_MODEL), lambda i: (i, 0)),
        out_shape=jax.ShapeDtypeStruct((t, D_MODEL), BF16),
        compiler_params=pltpu.CompilerParams(dimension_semantics=("parallel",)),
    )(proj, q_norm_g, kn, v)


def _gate_spec(tm, branch):
    return pl.BlockSpec((tm, D_MODEL), lambda i: (i, COL_GATE + branch))


def _merge_fwd(x, proj, a_conv, o_sb, o_x, w_conv_out, w_sb_out, w_x_out, w_out, tm):
    t, d = x.shape

    def body(x_ref, g0_ref, g1_ref, g2_ref, a_ref, s_ref, xo_ref, wc_ref, ws_ref, wx_ref, wo_ref,
             x1_ref, yc_ref, ys_ref, yx_ref, mg_ref):
        merged = jnp.zeros((tm, d), F32)
        for gate_ref, b_ref, w_ref, y_ref in ((g0_ref, a_ref, wc_ref, yc_ref), (g1_ref, s_ref, ws_ref, ys_ref),
                                              (g2_ref, xo_ref, wx_ref, yx_ref)):
            yv = _dot(b_ref[...].astype(BF16), w_ref[...])
            y_ref[...] = yv.astype(BF16)
            merged = merged + _sigmoid(gate_ref[...].astype(F32)) * yv
        mb = merged.astype(BF16)
        mg_ref[...] = mb
        x1_ref[...] = x_ref[...] + _dot(mb, wo_ref[...])

    tile = pl.BlockSpec((tm, d), lambda i: (i, 0))
    wfull = pl.BlockSpec((d, d), lambda i: (0, 0))
    return pl.pallas_call(
        body, name="merge_fwd",
        grid=(t // tm,),
        in_specs=[tile] + [_gate_spec(tm, b) for b in range(N_BRANCH)] + [tile, tile, tile,
                                                                           wfull, wfull, wfull, wfull],
        out_specs=[tile] * 5,
        out_shape=[jax.ShapeDtypeStruct((t, d), F32)] + [jax.ShapeDtypeStruct((t, d), BF16)] * 4,
        compiler_params=pltpu.CompilerParams(dimension_semantics=("parallel",)),
    )(x, proj, proj, proj, a_conv, o_sb, o_x, w_conv_out, w_sb_out, w_x_out, w_out)


def _mlp_fwd(x1, g_mlp, w_up_all, w_down, target, tm):
    t, d = x1.shape
    nb, _, fw = w_up_all.shape

    def body(x1_ref, g_ref, wu_ref, wd_ref, tgt_ref, up_ref, h2_ref, dx2_ref, lsum_ref, acc_ref):
        i, j = pl.program_id(0), pl.program_id(1)

        @pl.when(j == 0)
        def _():
            xf = x1_ref[...]
            h2_ref[...] = (xf * _rstd(xf) * g_ref[...]).astype(BF16)
            acc_ref[...] = jnp.zeros_like(acc_ref)

        @pl.when((i == 0) & (j == 0))
        def _():
            lsum_ref[...] = jnp.zeros_like(lsum_ref)

        up = _dot(h2_ref[...], wu_ref[...])
        up_ref[...] = up.astype(BF16)
        act = jnp.square(jnp.maximum(up, 0.0)).astype(BF16)
        acc_ref[...] += _dot(act, wd_ref[...])

        @pl.when(j == nb - 1)
        def _():
            diff = x1_ref[...] + acc_ref[...] - tgt_ref[...]
            dx2_ref[...] = diff * (1.0 / d)
            lsum_ref[...] += jnp.sum(diff * diff, axis=0, keepdims=True)

    tile = pl.BlockSpec((tm, d), lambda i, j: (i, 0))
    row = pl.BlockSpec((1, d), lambda i, j: (0, 0))
    return pl.pallas_call(
        body, name="mlp_fwd",
        grid=(t // tm, nb),
        in_specs=[tile, row, pl.BlockSpec((None, d, fw), lambda i, j: (j, 0, 0)),
                  pl.BlockSpec((fw, d), lambda i, j: (j, 0)), tile],
        out_specs=[pl.BlockSpec((tm, fw), lambda i, j: (i, j)), tile, tile, row],
        out_shape=[jax.ShapeDtypeStruct((t, nb * fw), BF16), jax.ShapeDtypeStruct((t, d), BF16),
                   jax.ShapeDtypeStruct((t, d), F32), jax.ShapeDtypeStruct((1, d), F32)],
        scratch_shapes=[pltpu.VMEM((tm, d), F32)],
        compiler_params=pltpu.CompilerParams(dimension_semantics=("arbitrary", "arbitrary")),
    )(x1, g_mlp, w_up_all, w_down, target)


def _mlp_bwd(x1, g_mlp, w_up_all, w_down, up, dx2, tm):
    t, d = x1.shape
    nb, _, fw = w_up_all.shape

    def body(x1_ref, g_ref, wu_ref, wd_ref, up_ref, dx2_ref, dup_ref, act_ref, dx1_ref, dg_ref, acc_ref, dyb_ref):
        i, j = pl.program_id(0), pl.program_id(1)

        @pl.when(j == 0)
        def _():
            dyb_ref[...] = dx2_ref[...].astype(BF16)
            acc_ref[...] = jnp.zeros_like(acc_ref)

        @pl.when((i == 0) & (j == 0))
        def _():
            dg_ref[...] = jnp.zeros_like(dg_ref)

        r = jnp.maximum(up_ref[...].astype(F32), 0.0)
        act_ref[...] = jnp.square(r).astype(BF16)
        dup = (_dot_nt(dyb_ref[...], wd_ref[...]) * (2.0 * r)).astype(BF16)
        dup_ref[...] = dup
        acc_ref[...] += _dot_nt(dup, wu_ref[...])

        @pl.when(j == nb - 1)
        def _():
            xf = x1_ref[...]
            rs = _rstd(xf)
            xhat = xf * rs
            dh2 = acc_ref[...]
            dg_ref[...] += jnp.sum(dh2 * xhat, axis=0, keepdims=True)
            dx1_ref[...] = dx2_ref[...] + _rms_bwd(dh2, xhat, rs, g_ref[...])

    tile = pl.BlockSpec((tm, d), lambda i, j: (i, 0))
    row = pl.BlockSpec((1, d), lambda i, j: (0, 0))
    ff = pl.BlockSpec((tm, fw), lambda i, j: (i, j))
    return pl.pallas_call(
        body, name="mlp_bwd",
        grid=(t // tm, nb),
        in_specs=[tile, row, pl.BlockSpec((None, d, fw), lambda i, j: (j, 0, 0)),
                  pl.BlockSpec((fw, d), lambda i, j: (j, 0)), ff, tile],
        out_specs=[ff, ff, tile, row],
        out_shape=[jax.ShapeDtypeStruct((t, nb * fw), BF16), jax.ShapeDtypeStruct((t, nb * fw), BF16),
                   jax.ShapeDtypeStruct((t, d), F32), jax.ShapeDtypeStruct((1, d), F32)],
        scratch_shapes=[pltpu.VMEM((tm, d), F32), pltpu.VMEM((tm, d), BF16)],
        compiler_params=pltpu.CompilerParams(dimension_semantics=("arbitrary", "arbitrary")),
    )(x1, g_mlp, w_up_all, w_down, up, dx2)


def _merge_bwd(dx1, proj, y_conv, y_sb, y_x, w_conv_out, w_sb_out, w_x_out, w_out, tm):
    t, d = dx1.shape

    def body(dx1_ref, g0_ref, g1_ref, g2_ref, yc_ref, ys_ref, yx_ref, wc_ref, ws_ref, wx_ref, wo_ref,
             dgate_ref, dyc_ref, dys_ref, dyx_ref, da_ref, dos_ref, dox_ref):
        dm = _dot_nt(dx1_ref[...].astype(BF16), wo_ref[...])
        for i, (gate_ref, y_ref, w_ref, dy_ref, db_ref) in enumerate(((g0_ref, yc_ref, wc_ref, dyc_ref, da_ref),
                                                                       (g1_ref, ys_ref, ws_ref, dys_ref, dos_ref),
                                                                       (g2_ref, yx_ref, wx_ref, dyx_ref, dox_ref))):
            gt = _sigmoid(gate_ref[...].astype(F32))
            dy = (dm * gt).astype(BF16)
            dy_ref[...] = dy
            dgate_ref[:, i * d:(i + 1) * d] = (dm * y_ref[...].astype(F32) * gt * (1.0 - gt)).astype(BF16)
            db_ref[...] = _dot_nt(dy, w_ref[...]).astype(BF16)

    tile = pl.BlockSpec((tm, d), lambda i: (i, 0))
    wfull = pl.BlockSpec((d, d), lambda i: (0, 0))
    return pl.pallas_call(
        body, name="merge_bwd",
        grid=(t // tm,),
        in_specs=[tile] + [_gate_spec(tm, b) for b in range(N_BRANCH)] + [tile, tile, tile,
                                                                           wfull, wfull, wfull, wfull],
        out_specs=[pl.BlockSpec((tm, N_BRANCH * d), lambda i: (i, 0))] + [tile] * 6,
        out_shape=[jax.ShapeDtypeStruct((t, N_BRANCH * d), BF16)] + [jax.ShapeDtypeStruct((t, d), BF16)] * 6,
        compiler_params=pltpu.CompilerParams(dimension_semantics=("parallel",)),
    )(dx1, proj, proj, proj, y_conv, y_sb, y_x, w_conv_out, w_sb_out, w_x_out, w_out)


def _conv_bwd(proj, conv_w, da, cw):
    t = proj.shape[0]
    nper = D_MODEL // cw

    def body(ch_ref, cb_ref, cc_ref, w_ref, da_ref, dch_ref, dcb_ref, dcc_ref, dw_ref):
        ch, cb, cc, u, u1, u2, cv, w, row = _conv_terms(ch_ref, cb_ref, cc_ref, w_ref)
        dav = da_ref[...].astype(F32)
        dcb_ref[...] = (dav * cv).astype(BF16)
        dcv = dav * cb
        n1 = jnp.where(row < t - 1, pltpu.roll(dcv, t - 1, 0), 0.0)
        n2 = jnp.where(row < t - 2, pltpu.roll(dcv, t - 2, 0), 0.0)
        du = w[2] * dcv + w[1] * n1 + w[0] * n2
        dcc_ref[...] = (du * ch).astype(BF16)
        dch_ref[...] = (du * cc).astype(BF16)
        dw_ref[0:1, :] = jnp.sum(dcv * u2, axis=0, keepdims=True)
        dw_ref[1:2, :] = jnp.sum(dcv * u1, axis=0, keepdims=True)
        dw_ref[2:3, :] = jnp.sum(dcv * u, axis=0, keepdims=True)

    def col(piece):
        return pl.BlockSpec((t, cw), lambda j: (0, piece * nper + j))

    out_col = pl.BlockSpec((t, cw), lambda j: (0, j))
    wspec = pl.BlockSpec((3, cw), lambda j: (0, j))
    return pl.pallas_call(
        body, name="conv_bwd",
        grid=(nper,),
        in_specs=[col(COL_CH), col(COL_CB), col(COL_CC), wspec, out_col],
        out_specs=[out_col, out_col, out_col, wspec],
        out_shape=[jax.ShapeDtypeStruct((t, D_MODEL), BF16)] * 3 + [jax.ShapeDtypeStruct((3, D_MODEL), F32)],
        compiler_params=pltpu.CompilerParams(dimension_semantics=("parallel",)),
    )(proj, proj, proj, conv_w, da)


def _sb_bwd(proj, kt4, vt4, do_sb, o_sb, tq, pair_sums):
    t = proj.shape[0]
    nq = t // tq
    pairs = D_MODEL // SB_BLOCK
    scale = SB_HEAD_DIM ** -0.5
    n = len(pair_sums)

    def body(q_ref, k_ref, v_ref, kt_ref, vt_ref, do_ref, o_ref, *rest):
        dq_ref, dk_ref, dv_ref = rest[n:n + 3]
        dk_acc, dv_acc = rest[2 * n + 3:2 * n + 5]
        begin, finish = _chip_exchange_phases(rest[:n], rest[n + 3:2 * n + 3], *rest[2 * n + 5:])
        pi, qi = pl.program_id(0), pl.program_id(1)
        pl.when((pi == 0) & (qi == 0))(begin)

        @pl.when(qi == 0)
        def _():
            dk_acc[...] = jnp.zeros_like(dk_acc)
            dv_acc[...] = jnp.zeros_like(dv_acc)

        q = q_ref[...].astype(F32) * scale
        do = do_ref[...].astype(F32)
        qt, dot_ = q.T, do.T
        prod = dot_ * o_ref[...].T
        u_mat = _scan_matrix()
        qts = [_head_rows(qt, h) for h in range(2)]
        dots = [_head_rows(dot_, h) for h in range(2)]
        qms = [_head_lanes(q, h) for h in range(2)]
        doms = [_head_lanes(do, h) for h in range(2)]
        dsum = [jnp.sum(prod[h * SB_HEAD_DIM:(h + 1) * SB_HEAD_DIM, :], axis=0, keepdims=True) for h in range(2)]

        def block(kb, state, past):
            ks = pl.multiple_of(kb * SB_BLOCK, SB_BLOCK)
            kblk = k_ref[pl.ds(ks, SB_BLOCK), :]
            vblk = v_ref[pl.ds(ks, SB_BLOCK), :]
            kt = kt_ref[kb]
            dk_add = jnp.zeros((SB_BLOCK, SB_BLOCK), F32)
            dv_add = jnp.zeros((SB_BLOCK, SB_BLOCK), F32)
            out = []
            for h in range(2):
                dqt, carry_l, carry_g = state[h]
                a, lb, carry_l = _sb_probs(kblk, qts[h], u_mat, carry_l, past)
                ab = a.astype(BF16)
                g = _dot(vblk, dots[h]) * ab.astype(F32)
                before = dsum[h] - (_suffix_sum(u_mat, g) + g + carry_g)
                sig = jnp.exp(lb)
                dz = g * (1.0 - sig) - before * sig
                if past is not None:
                    dz = jnp.where(past, dz, 0.0)
                dzb = dz.astype(BF16)
                dqt = dqt + _dot(kt[h * SB_HEAD_DIM:(h + 1) * SB_HEAD_DIM, :], dzb)
                dk_add = dk_add + _dot(dzb, qms[h])
                dv_add = dv_add + _dot(ab, doms[h])
                out.append((dqt, carry_l, carry_g + jnp.sum(g, axis=0, keepdims=True)))
            dk_acc[pl.ds(ks, SB_BLOCK), :] += dk_add
            dv_acc[pl.ds(ks, SB_BLOCK), :] += dv_add
            return tuple(out)

        zero = (jnp.zeros((SB_HEAD_DIM, tq), F32), jnp.zeros((1, tq), F32), jnp.zeros((1, tq), F32))
        state = _sb_sweep(qi, tq, block, (zero, zero))
        dq_ref[...] = (jnp.concatenate([state[0][0], state[1][0]], axis=0).T * scale).astype(BF16)

        @pl.when(qi == nq - 1)
        def _():
            dk_ref[...] = dk_acc[...].astype(BF16)
            dv_ref[...] = dv_acc[...].astype(BF16)

        pl.when((pi == pairs - 1) & (qi == nq - 1))(finish)

    qblk = lambda base: pl.BlockSpec((tq, SB_BLOCK), lambda p, i: (i, base * pairs + p))
    seq = lambda base: pl.BlockSpec((t, SB_BLOCK), lambda p, i: (0, base * pairs + p))
    tr = pl.BlockSpec((None, t // SB_BLOCK, SB_BLOCK, SB_BLOCK), lambda p, i: (p, 0, 0, 0))
    outs = pl.pallas_call(
        body, name="sb_bwd",
        grid=(pairs, nq),
        in_specs=[qblk(COL_SQ), seq(COL_SK), seq(COL_SV), tr, tr, qblk(0), qblk(0)] + [ANY] * n,
        out_specs=[qblk(0), seq(0), seq(0)] + [ANY] * n,
        out_shape=[jax.ShapeDtypeStruct((t, D_MODEL), BF16)] * 3 + _chip_exchange_shapes(pair_sums),
        scratch_shapes=[pltpu.VMEM((t, SB_BLOCK), F32), pltpu.VMEM((t, SB_BLOCK), F32)] + _chip_exchange_sems(n),
        compiler_params=pltpu.CompilerParams(dimension_semantics=("arbitrary", "arbitrary")),
    )(proj, proj, proj, kt4, vt4, do_sb, o_sb, *pair_sums)
    return outs[0], outs[1], outs[2], outs[3:]


def _x_bwd(proj, q_norm_g, kn, v, do_x, tm):
    t = proj.shape[0]
    m = kn.shape[0]
    scale = X_HEAD_DIM ** -0.5

    def body(xq_ref, qg_ref, kn_ref, v_ref, do_ref, dxq_ref, dkn_ref, dv_ref, dqg_ref):
        @pl.when(pl.program_id(0) == 0)
        def _():
            dkn_ref[...] = jnp.zeros_like(dkn_ref)
            dv_ref[...] = jnp.zeros_like(dv_ref)
            dqg_ref[...] = jnp.zeros_like(dqg_ref)

        qg = qg_ref[...]
        for h in range(X_HEADS):
            sl, rq, qhat, qn, p = _x_head(xq_ref, qg, kn_ref, h)
            do_h = do_ref[:, sl]
            dp = _dot_nt(do_h, v_ref[:, sl])
            ds = (p * (dp - jnp.sum(dp * p, axis=-1, keepdims=True)) * scale).astype(BF16)
            dqn = _dot(ds, kn_ref[:, sl])
            dkn_ref[:, sl] += _dot_tn(ds, qn)
            dv_ref[:, sl] += _dot_tn(p.astype(BF16), do_h)
            dqg_ref[...] += jnp.sum(dqn * qhat, axis=0, keepdims=True)
            dxq_ref[:, sl] = _rms_bwd(dqn, qhat, rq, qg).astype(BF16)

    full = pl.BlockSpec((m, D_MODEL), lambda i: (0, 0))
    gain = pl.BlockSpec((1, X_HEAD_DIM), lambda i: (0, 0))
    tile = pl.BlockSpec((tm, D_MODEL), lambda i: (i, 0))
    return pl.pallas_call(
        body, name="x_bwd",
        grid=(t // tm,),
        in_specs=[pl.BlockSpec((tm, D_MODEL), lambda i: (i, COL_XQ)), gain, full, full, tile],
        out_specs=[tile, full, full, gain],
        out_shape=[jax.ShapeDtypeStruct((t, D_MODEL), BF16), jax.ShapeDtypeStruct((m, D_MODEL), F32),
                   jax.ShapeDtypeStruct((m, D_MODEL), F32), jax.ShapeDtypeStruct((1, X_HEAD_DIM), F32)],
        compiler_params=pltpu.CompilerParams(dimension_semantics=("arbitrary",)),
    )(proj, q_norm_g, kn, v, do_x)


def _mem_bwd(mem, g_mem, wkv_all, k_norm_g, dkn, dv):
    m, d = mem.shape

    def body(mem_ref, g_ref, w_ref, kg_ref, dkn_ref, dv_ref, dkv_ref, dgm_ref, dkg_ref):
        memf = mem_ref[...]
        mem_hat = memf * _rstd(memf)
        memn = (mem_hat * g_ref[...]).astype(BF16)
        kg = kg_ref[...]
        dmemn = jnp.zeros((m, d), F32)
        dkg = jnp.zeros((1, X_HEAD_DIM), F32)
        for b in range(N_DEV):
            sl = slice(b * X_HEAD_DIM, (b + 1) * X_HEAD_DIM)
            if b < X_HEADS:
                kv = _dot(memn, w_ref[b])
                rk = _rstd(kv)
                khat = kv * rk
                dkn_h = dkn_ref[:, sl]
                dkg = dkg + jnp.sum(dkn_h * khat, axis=0, keepdims=True)
                dblk = _rms_bwd(dkn_h, khat, rk, kg).astype(BF16)
            else:
                hs = slice((b - X_HEADS) * X_HEAD_DIM, (b - X_HEADS + 1) * X_HEAD_DIM)
                dblk = dv_ref[:, hs].astype(BF16)
            dkv_ref[:, sl] = dblk
            dmemn = dmemn + _dot_nt(dblk, w_ref[b])
        dgm_ref[...] = jnp.sum(dmemn * mem_hat, axis=0, keepdims=True)
        dkg_ref[...] = dkg

    return pl.pallas_call(
        body, name="mem_bwd",
        out_shape=[jax.ShapeDtypeStruct((m, 2 * d), BF16), jax.ShapeDtypeStruct((1, d), F32),
                   jax.ShapeDtypeStruct((1, X_HEAD_DIM), F32)],
    )(mem, g_mem, wkv_all, k_norm_g, dkn, dv)


def _in_proj_bwd(x, g_mix, w_in_all, dproj, dx1, tm):
    t, d = x.shape
    nb, _, bw = w_in_all.shape

    def body(x_ref, g_ref, w_ref, dp_ref, dx1_ref, dx_ref, dg_ref, acc_ref):
        i, j = pl.program_id(0), pl.program_id(1)

        @pl.when(j == 0)
        def _():
            acc_ref[...] = jnp.zeros_like(acc_ref)

        @pl.when((i == 0) & (j == 0))
        def _():
            dg_ref[...] = jnp.zeros_like(dg_ref)

        acc_ref[...] += _dot_nt(dp_ref[...], w_ref[...])

        @pl.when(j == nb - 1)
        def _():
            xf = x_ref[...]
            rs = _rstd(xf)
            xhat = xf * rs
            dh = acc_ref[...]
            dg_ref[...] += jnp.sum(dh * xhat, axis=0, keepdims=True)
            dx_ref[...] = dx1_ref[...] + _rms_bwd(dh, xhat, rs, g_ref[...])

    tile = pl.BlockSpec((tm, d), lambda i, j: (i, 0))
    row = pl.BlockSpec((1, d), lambda i, j: (0, 0))
    return pl.pallas_call(
        body, name="in_proj_bwd",
        grid=(t // tm, nb),
        in_specs=[tile, row, pl.BlockSpec((None, d, bw), lambda i, j: (j, 0, 0)),
                  pl.BlockSpec((tm, bw), lambda i, j: (i, j)), tile],
        out_specs=[tile, row],
        out_shape=[jax.ShapeDtypeStruct((t, d), F32), jax.ShapeDtypeStruct((1, d), F32)],
        scratch_shapes=[pltpu.VMEM((tm, d), F32)],
        compiler_params=pltpu.CompilerParams(dimension_semantics=("arbitrary", "arbitrary")),
    )(x, g_mix, w_in_all, dproj, dx1)


def _weight_grad(a, b, bw, tmm, name):
    t, m = a.shape
    n = b.shape[1]
    tmm = min(tmm, m)

    def body(a_ref, b_ref, o_ref):
        o_ref[...] = _dot_tn(a_ref[...].astype(BF16), b_ref[...].astype(BF16)).astype(BF16)

    return pl.pallas_call(
        body, name=name,
        grid=(m // tmm, n // bw),
        in_specs=[pl.BlockSpec((t, tmm), lambda i, j: (0, i)), pl.BlockSpec((t, bw), lambda i, j: (0, j))],
        out_specs=pl.BlockSpec((None, tmm, bw), lambda i, j: (j, i, 0)),
        out_shape=jax.ShapeDtypeStruct((n // bw, m, bw), BF16),
        compiler_params=pltpu.CompilerParams(dimension_semantics=("parallel", "parallel")),
    )(a, b)


def _pair_sum(grad, recv, own_blocks, name):
    _, rows, cols = grad.shape

    def body(idx_ref, g_ref, r_ref, o_ref):
        o_ref[...] = (g_ref[...].astype(F32) + r_ref[...].astype(F32)).astype(BF16)

    return pl.pallas_call(
        body, name=name,
        grid_spec=pltpu.PrefetchScalarGridSpec(
            num_scalar_prefetch=1, grid=(4,),
            in_specs=[pl.BlockSpec((None, rows, cols), lambda r, idx: (idx[r], 0, 0)),
                      pl.BlockSpec((None, rows, cols), lambda r, idx: (r, 0, 0))],
            out_specs=pl.BlockSpec((None, rows, cols), lambda r, idx: (r, 0, 0))),
        out_shape=jax.ShapeDtypeStruct((4, rows, cols), BF16),
        compiler_params=pltpu.CompilerParams(dimension_semantics=("parallel",)),
    )(own_blocks, grad, recv)


def _adamw_math(w, g, m, v):
    m = ADAM_B1 * m + (1.0 - ADAM_B1) * g
    v = ADAM_B2 * v + (1.0 - ADAM_B2) * jnp.square(g)
    m_hat = m / (1.0 - ADAM_B1 ** ADAM_STEP)
    v_hat = v / (1.0 - ADAM_B2 ** ADAM_STEP)
    delta = -ADAM_LR * (m_hat / (jnp.sqrt(v_hat) + ADAM_EPS) + ADAM_WD * w)
    return delta, m, v


def _adamw_sharded(pair_sums, recv, w, m, v, tr, name):
    rows, cols = w.shape
    tr = min(tr, rows)

    def body(h_ref, r_ref, w_ref, m_ref, v_ref, g_out, d_out, m_out, v_out):
        g = h_ref[...].astype(F32)
        for r in range(3):
            g = g + r_ref[r].astype(F32)
        g_out[...] = g
        d_out[...], m_out[...], v_out[...] = _adamw_math(w_ref[...], g, m_ref[...], v_ref[...])

    tile = pl.BlockSpec((tr, cols), lambda i: (i, 0))
    return pl.pallas_call(
        body, name=name,
        grid=(rows // tr,),
        in_specs=[pl.BlockSpec((None, tr, cols), lambda i: (0, i, 0)),
                  pl.BlockSpec((3, tr, cols), lambda i: (0, i, 0)), tile, tile, tile],
        out_specs=[tile] * 4,
        out_shape=[jax.ShapeDtypeStruct((rows, cols), F32)] * 4,
        compiler_params=pltpu.CompilerParams(dimension_semantics=("parallel",)),
    )(pair_sums, recv, w, m, v)


def _small_sum(gathered):
    _, rows, cols = gathered.shape

    def body(g_ref, o_ref):
        total = g_ref[0]
        for dev in range(1, N_DEV):
            total = total + g_ref[dev]
        o_ref[...] = total

    return pl.pallas_call(body, name="small_grad_sum", out_shape=jax.ShapeDtypeStruct((rows, cols), F32))(gathered)


def _adamw_small(w, g, m, v):
    def body(w_ref, g_ref, m_ref, v_ref, d_out, m_out, v_out):
        d_out[...], m_out[...], v_out[...] = _adamw_math(w_ref[...], g_ref[...], m_ref[...], v_ref[...])

    return pl.pallas_call(body, name="adamw_small", out_shape=[jax.ShapeDtypeStruct(w.shape, F32)] * 3)(w, g, m, v)


def _pad_tile(a):
    return jnp.pad(a, ((0, SMALL_TILE - a.shape[0]), (0, D_MODEL - a.shape[1])))


def _pack_small(*pieces):
    return jnp.concatenate([_pad_tile(a) for a in pieces], axis=0)


def kernel(x, mem, g_mix, g_mem, w_in, conv_w, w_conv_out, w_sb_out, q_norm_g, k_norm_g, w_mem_kv, w_x_out, w_out, g_mlp, w_up, w_down, loss_target, m_g_mix, m_g_mem, m_w_in, m_conv_w, m_w_conv_out, m_w_sb_out, m_q_norm_g, m_k_norm_g, m_w_mem_kv, m_w_x_out, m_w_out, m_g_mlp, m_w_up, m_w_down, v_g_mix, v_g_mem, v_w_in, v_conv_w, v_w_conv_out, v_w_sb_out, v_q_norm_g, v_k_norm_g, v_w_mem_kv, v_w_x_out, v_w_out, v_g_mlp, v_w_up, v_w_down):
    xpos, ypos, cpos = _mesh_pos()
    me = 4 * xpos + 2 * ypos + cpos
    x2d, mem2d, tgt2d = x[0], mem[0], loss_target[0]
    t = x2d.shape[0]
    tm = min(512, t)
    tm_s = min(256, t)

    big = {
        "w_in": (w_in[0], m_w_in[0], v_w_in[0]),
        "w_conv_out": (w_conv_out[0], m_w_conv_out[0], v_w_conv_out[0]),
        "w_sb_out": (w_sb_out[0], m_w_sb_out[0], v_w_sb_out[0]),
        "w_mem_kv": (w_mem_kv[0], m_w_mem_kv[0], v_w_mem_kv[0]),
        "w_x_out": (w_x_out[0], m_w_x_out[0], v_w_x_out[0]),
        "w_out": (w_out[0], m_w_out[0], v_w_out[0]),
        "w_up": (w_up[0], m_w_up[0], v_w_up[0]),
        "w_down": (w_down[0], m_w_down[0], v_w_down[0]),
    }
    late = [n for n in big if n != "w_in"]
    conv_pad = jnp.pad(conv_w[0], ((0, 8 - 3), (0, 0)))
    w_in_all, conv_all = _all_gather([big["w_in"][0].astype(BF16), conv_pad])
    conv_full = conv_all[:, :3, :].transpose(1, 0, 2).reshape(3, D_MODEL)

    proj, h = _in_proj(x2d, g_mix, w_in_all, tm)
    a_conv = _conv_fwd(proj, conv_full, 256)
    nk = t // SB_BLOCK
    pairs = D_MODEL // SB_BLOCK

    def blocks_t(cols):
        return cols.reshape(nk, SB_BLOCK, pairs, SB_BLOCK).transpose(2, 0, 3, 1)

    kt4 = blocks_t(proj[:, COL_SK * D_MODEL:(COL_SK + 1) * D_MODEL])
    vt4 = blocks_t(proj[:, COL_SV * D_MODEL:(COL_SV + 1) * D_MODEL])
    tq = min(SB_QUERY_TILE, t)
    o_sb, gathered = _sb_fwd(proj, vt4, tq, [big[n][0].astype(BF16) for n in late])
    full = dict(zip(late, gathered))
    wkv_all, w_up_all = full["w_mem_kv"], full["w_up"]
    rows_full = lambda a: a.reshape(a.shape[0] * a.shape[1], a.shape[2])
    wc, ws, wx, wo, wd = (rows_full(full[n]) for n in ("w_conv_out", "w_sb_out", "w_x_out", "w_out", "w_down"))
    mem_n, kn, vmem = _mem_prep(mem2d, g_mem, wkv_all, k_norm_g)
    o_x = _x_fwd(proj, q_norm_g, kn, vmem, tm_s)
    x1, y_conv, y_sb, y_x, merged = _merge_fwd(x2d, proj, a_conv, o_sb, o_x, wc, ws, wx, wo, tm_s)
    up, h2, dx2, lsum = _mlp_fwd(x1, g_mlp, w_up_all, wd, tgt2d, tm)

    dup, act, dx1, dg_mlp = _mlp_bwd(x1, g_mlp, w_up_all, wd, up, dx2, tm)
    dgate, dy_conv, dy_sb, dy_x, da_conv, do_sb, do_x = _merge_bwd(dx1, proj, y_conv, y_sb, y_x, wc, ws, wx, wo, tm_s)
    dch, dcb, dcc, dconv = _conv_bwd(proj, conv_full, da_conv, 256)
    dxq, dkn, dvm, dqg = _x_bwd(proj, q_norm_g, kn, vmem, do_x, tm_s)
    dkv, dg_mem, dkg = _mem_bwd(mem2d, g_mem, wkv_all, k_norm_g, dkn, dvm)
    wgrads = {
        "w_conv_out": _weight_grad(a_conv, dy_conv, D_MODEL, 512, "dw_conv_out"),
        "w_sb_out": _weight_grad(o_sb, dy_sb, D_MODEL, 512, "dw_sb_out"),
        "w_mem_kv": _weight_grad(mem_n, dkv, wkv_all.shape[2], 512, "dw_mem_kv"),
        "w_x_out": _weight_grad(o_x, dy_x, D_MODEL, 512, "dw_x_out"),
        "w_out": _weight_grad(merged, dx1, D_MODEL, 512, "dw_out"),
        "w_up": _weight_grad(h2, dup, w_up_all.shape[2], 512, "dw_up"),
        "w_down": _weight_grad(act, dx2, D_MODEL, 512, "dw_down"),
    }

    own_blocks = jnp.stack([4 * (xpos ^ dx) + 2 * (ypos ^ dy) + cpos for dx in (0, 1) for dy in (0, 1)]).astype(jnp.int32)

    def pair_reduce(group):
        blocked = [wgrads[n].reshape((N_DEV,) + big[n][0].shape) for n in group]
        from_sibling = _pair_exchange(blocked, "grad_pair_exchange_" + group[0])
        return [_pair_sum(g, r, own_blocks, "pair_sum_" + n) for n, g, r in zip(group, blocked, from_sibling)]

    pair_sums = dict(zip(late, pair_reduce(late)))
    dq, dk, dv, from_chips_late = _sb_bwd(proj, kt4, vt4, do_sb, o_sb, tq, [pair_sums[n] for n in late])
    from_chips = dict(zip(late, from_chips_late))
    dproj = jnp.concatenate([dch, dcb, dcc, dq, dk, dv, dxq, dgate], axis=1)
    grad_x, dg_mix = _in_proj_bwd(x2d, g_mix, w_in_all, dproj, dx1, tm)
    wgrads["w_in"] = _weight_grad(h, dproj, w_in_all.shape[2], 512, "dw_in")
    pair_sums["w_in"], = pair_reduce(["w_in"])
    from_chips["w_in"], = _chip_exchange([pair_sums["w_in"]])
    res = {}
    for n in big:
        w_sh, m_sh, v_sh = big[n]
        res[n] = _adamw_sharded(pair_sums[n], from_chips[n], w_sh, m_sh, v_sh, 256, "adamw_" + n)

    part = _pack_small(dg_mix, dg_mem, dg_mlp, dqg, dkg, dconv, lsum)
    gsum = _small_sum(_small_all_gather(part))
    loss = 0.5 * jnp.sum(gsum[6 * SMALL_TILE]) / D_MODEL
    conv_cols = lax.dynamic_slice(gsum[5 * SMALL_TILE:6 * SMALL_TILE], (0, me * (D_MODEL // N_DEV)),
                                  (SMALL_TILE, D_MODEL // N_DEV))
    g_small = jnp.concatenate([gsum[:5 * SMALL_TILE], _pad_tile(conv_cols)], axis=0)
    w_small = _pack_small(g_mix, g_mem, g_mlp, q_norm_g, k_norm_g, conv_w[0])
    m_small = _pack_small(m_g_mix, m_g_mem, m_g_mlp, m_q_norm_g, m_k_norm_g, m_conv_w[0])
    v_small = _pack_small(v_g_mix, v_g_mem, v_g_mlp, v_q_norm_g, v_k_norm_g, v_conv_w[0])
    d_small, nm_small, nv_small = _adamw_small(w_small, g_small, m_small, v_small)

    def unpack(p):
        return {"g_mix": p[0:1], "g_mem": p[8:9], "g_mlp": p[16:17], "q_norm_g": p[24:25, :X_HEAD_DIM],
                "k_norm_g": p[32:33, :X_HEAD_DIM], "conv_w": p[40:43, :D_MODEL // N_DEV][None]}

    small = [unpack(p) for p in (g_small, d_small, nm_small, nv_small)]
    order = ["g_mix", "g_mem", "w_in", "conv_w", "w_conv_out", "w_sb_out", "q_norm_g", "k_norm_g", "w_mem_kv",
             "w_x_out", "w_out", "g_mlp", "w_up", "w_down"]
    outs = [loss, grad_x[None]]
    for kind in range(4):
        for n in order:
            outs.append(res[n][kind][None] if n in res else small[kind][n])
    return tuple(outs)
```

```python
import functools

import jax
import jax.numpy as jnp
from jax import lax
from jax.experimental import pallas as pl
from jax.experimental.pallas import tpu as pltpu

F32 = jnp.float32
BF16 = jnp.bfloat16
MESH = pl.DeviceIdType.MESH

EPS = 1e-6
N_DEV = 8
D_MODEL = 1024
SB_HEAD_DIM = 64
SB_BLOCK = 128
SB_QUERY_TILE = 512
X_HEADS = 4
X_HEAD_DIM = 256
N_BRANCH = 3
COL_CH, COL_CB, COL_CC, COL_SQ, COL_SK, COL_SV, COL_XQ, COL_GATE = 0, 1, 2, 3, 4, 5, 6, 7

ADAM_LR = 0.001
ADAM_B1 = 0.9
ADAM_B2 = 0.999
ADAM_EPS = 1e-08
ADAM_WD = 0.01
ADAM_STEP = 10

SMALL_TILE = 8


def _dot(a, b):
    return jnp.dot(a, b, preferred_element_type=F32)


def _dot_nt(a, b):
    return lax.dot_general(a, b, (((1,), (1,)), ((), ())), preferred_element_type=F32)


def _dot_tn(a, b):
    return lax.dot_general(a, b, (((0,), (0,)), ((), ())), preferred_element_type=F32)


def _rstd(xf):
    return lax.rsqrt(jnp.mean(xf * xf, axis=-1, keepdims=True) + EPS)


def _sigmoid(z):
    return 1.0 / (1.0 + jnp.exp(-z))


def _log_sigmoid(z):
    return jnp.minimum(z, 0.0) - jnp.log(1.0 + jnp.exp(-jnp.abs(z)))


def _rms_bwd(dy, xhat, r, g):
    dxhat = dy * g
    return r * (dxhat - xhat * jnp.mean(dxhat * xhat, axis=-1, keepdims=True))


def _mesh_pos():
    return lax.axis_index("x"), lax.axis_index("y"), lax.axis_index("c")


ANY = pl.BlockSpec(memory_space=pl.ANY)


def _all_gather(shards):
    n = len(shards)

    def body(*refs):
        begin, relay, finish = _gather_phases(refs[:n], refs[n:2 * n], *refs[2 * n:])
        begin()
        relay()
        finish()

    return pl.pallas_call(
        body, name="weights_all_gather",
        out_shape=_gather_shapes(shards),
        in_specs=[ANY] * n, out_specs=[ANY] * n,
        scratch_shapes=_gather_sems(n),
    )(*shards)


def _gather_shapes(shards):
    return [jax.ShapeDtypeStruct((N_DEV,) + s.shape, s.dtype) for s in shards]


def _gather_sems(n):
    return [pltpu.SemaphoreType.DMA((n, 7)), pltpu.SemaphoreType.DMA((n, 7)), pltpu.SemaphoreType.DMA((n,))]


def _gather_phases(ins, outs, send_sems, recv_sems, local_sems):
    n = len(ins)
    x, y, c = _mesh_pos()
    me, sibling = (x, y, c), (x, y, 1 - c)
    chips = [(1 - x, y), (x, 1 - y), (1 - x, 1 - y)]

    def blk(a, px, py, pc):
        return outs[a].at[4 * px + 2 * py + pc]

    def copy(a, k, block, to, src=None):
        return pltpu.make_async_remote_copy(
            src_ref=blk(a, *block) if src is None else src, dst_ref=blk(a, *block),
            send_sem=send_sems.at[a, k], recv_sem=recv_sems.at[a, k], device_id=to, device_id_type=MESH)

    def local(a):
        return pltpu.make_async_copy(ins[a], blk(a, *me), local_sems.at[a])

    def own(a):
        return [copy(a, 0, me, sibling, src=ins[a])] + [copy(a, 1 + j, me, (*chip, c), src=ins[a])
                                                        for j, chip in enumerate(chips)]

    def begin():
        for a in range(n):
            local(a).start()
        for a in range(n):
            for cp in own(a):
                cp.start()

    def relay():
        for j, chip in enumerate(chips):
            for a in range(n):
                copy(a, 1 + j, (*chip, c), me).wait_recv()
                copy(a, 4 + j, (*chip, c), sibling).start()

    def finish():
        for a in range(n):
            copy(a, 0, sibling, me).wait_recv()
            for j, chip in enumerate(chips):
                copy(a, 4 + j, (*chip, 1 - c), me).wait_recv()
        for a in range(n):
            for cp in own(a):
                cp.wait_send()
            for j, chip in enumerate(chips):
                copy(a, 4 + j, (*chip, c), sibling).wait_send()
            local(a).wait()

    return begin, relay, finish


def _pair_exchange(grads, name):
    n = len(grads)

    def body(*refs):
        ins, outs = refs[:n], refs[n:2 * n]
        send_sems, recv_sems = refs[2 * n:]
        x, y, c = _mesh_pos()
        xs, ys = (x, 1 - x), (y, 1 - y)
        copies = []
        for a in range(n):
            for r in range(4):
                dx, dy = divmod(r, 2)
                copies.append(pltpu.make_async_remote_copy(
                    src_ref=ins[a].at[4 * xs[dx] + 2 * ys[dy] + (1 - c)], dst_ref=outs[a].at[r],
                    send_sem=send_sems.at[a, r], recv_sem=recv_sems.at[a, r],
                    device_id=(x, y, 1 - c), device_id_type=MESH))
        for cp in copies:
            cp.start()
        for cp in copies:
            cp.wait()

    return pl.pallas_call(
        body, name=name,
        out_shape=[jax.ShapeDtypeStruct((4,) + g.shape[1:], g.dtype) for g in grads],
        in_specs=[ANY] * n, out_specs=[ANY] * n,
        scratch_shapes=[pltpu.SemaphoreType.DMA((n, 4)), pltpu.SemaphoreType.DMA((n, 4))],
    )(*grads)


def _chip_exchange(sums):
    n = len(sums)

    def body(*refs):
        begin, finish = _chip_exchange_phases(refs[:n], refs[n:2 * n], *refs[2 * n:])
        begin()
        finish()

    return pl.pallas_call(
        body, name="grad_chip_exchange",
        out_shape=_chip_exchange_shapes(sums),
        in_specs=[ANY] * n, out_specs=[ANY] * n,
        scratch_shapes=_chip_exchange_sems(n),
    )(*sums)


def _chip_exchange_shapes(sums):
    return [jax.ShapeDtypeStruct((3,) + s.shape[1:], s.dtype) for s in sums]


def _chip_exchange_sems(n):
    return [pltpu.SemaphoreType.DMA((n, 3)), pltpu.SemaphoreType.DMA((n, 3))]


def _chip_exchange_phases(ins, outs, send_sems, recv_sems):
    x, y, c = _mesh_pos()
    xs, ys = (x, 1 - x), (y, 1 - y)

    def copies():
        out = []
        for a in range(len(ins)):
            for r in range(1, 4):
                dx, dy = divmod(r, 2)
                out.append(pltpu.make_async_remote_copy(
                    src_ref=ins[a].at[r], dst_ref=outs[a].at[r - 1],
                    send_sem=send_sems.at[a, r - 1], recv_sem=recv_sems.at[a, r - 1],
                    device_id=(xs[dx], ys[dy], c), device_id_type=MESH))
        return out

    def begin():
        for cp in copies():
            cp.start()

    def finish():
        for cp in copies():
            cp.wait()

    return begin, finish


def _small_all_gather(part):
    rows, cols = part.shape

    def body(in_ref, out_ref, send_sems, recv_sems):
        x, y, c = _mesh_pos()
        xs, ys, cs = (x, 1 - x), (y, 1 - y), (c, 1 - c)
        out_ref[4 * x + 2 * y + c] = in_ref[...]
        copies = []
        for k in range(1, N_DEV):
            dx, dy, dc = k // 4, (k // 2) % 2, k % 2
            copies.append((
                pltpu.make_async_remote_copy(
                    src_ref=in_ref, dst_ref=out_ref.at[4 * x + 2 * y + c],
                    send_sem=send_sems.at[k - 1], recv_sem=recv_sems.at[k - 1],
                    device_id=(xs[dx], ys[dy], cs[dc]), device_id_type=MESH),
                pltpu.make_async_remote_copy(
                    src_ref=in_ref, dst_ref=out_ref.at[4 * xs[dx] + 2 * ys[dy] + cs[dc]],
                    send_sem=send_sems.at[k - 1], recv_sem=recv_sems.at[k - 1],
                    device_id=(xs[dx], ys[dy], cs[dc]), device_id_type=MESH)))
        for send, _ in copies:
            send.start()
        for send, recv in copies:
            recv.wait_recv()
            send.wait_send()

    return pl.pallas_call(
        body, name="small_all_gather",
        out_shape=jax.ShapeDtypeStruct((N_DEV, rows, cols), part.dtype),
        in_specs=[pl.BlockSpec(memory_space=pltpu.VMEM)],
        out_specs=pl.BlockSpec(memory_space=pltpu.VMEM),
        scratch_shapes=[pltpu.SemaphoreType.DMA((N_DEV - 1,)), pltpu.SemaphoreType.DMA((N_DEV - 1,))],
    )(part)


def _in_proj(x, g_mix, w_in_all, tm):
    t, d = x.shape
    nb, _, bw = w_in_all.shape

    def body(x_ref, g_ref, w_ref, proj_ref, h_ref):
        @pl.when(pl.program_id(1) == 0)
        def _():
            xf = x_ref[...]
            h_ref[...] = (xf * _rstd(xf) * g_ref[...]).astype(BF16)

        proj_ref[...] = _dot(h_ref[...], w_ref[...]).astype(BF16)

    return pl.pallas_call(
        body, name="in_proj",
        grid=(t // tm, nb),
        in_specs=[pl.BlockSpec((tm, d), lambda i, j: (i, 0)),
                  pl.BlockSpec((1, d), lambda i, j: (0, 0)),
                  pl.BlockSpec((None, d, bw), lambda i, j: (j, 0, 0))],
        out_specs=[pl.BlockSpec((tm, bw), lambda i, j: (i, j)),
                   pl.BlockSpec((tm, d), lambda i, j: (i, 0))],
        out_shape=[jax.ShapeDtypeStruct((t, nb * bw), BF16), jax.ShapeDtypeStruct((t, d), BF16)],
        compiler_params=pltpu.CompilerParams(dimension_semantics=("parallel", "arbitrary")),
    )(x, g_mix, w_in_all)


def _conv_terms(ch_ref, cb_ref, cc_ref, w_ref):
    ch, cb, cc = ch_ref[...].astype(F32), cb_ref[...].astype(F32), cc_ref[...].astype(F32)
    u = cc * ch
    row = lax.broadcasted_iota(jnp.int32, u.shape, 0)
    u1 = jnp.where(row >= 1, pltpu.roll(u, 1, 0), 0.0)
    u2 = jnp.where(row >= 2, pltpu.roll(u, 2, 0), 0.0)
    w = (w_ref[0:1, :], w_ref[1:2, :], w_ref[2:3, :])
    cv = w[2] * u + w[1] * u1 + w[0] * u2
    return ch, cb, cc, u, u1, u2, cv, w, row


def _conv_fwd(proj, conv_w, cw):
    t = proj.shape[0]
    nper = D_MODEL // cw

    def body(ch_ref, cb_ref, cc_ref, w_ref, a_ref):
        _, cb, _, _, _, _, cv, _, _ = _conv_terms(ch_ref, cb_ref, cc_ref, w_ref)
        a_ref[...] = (cb * cv).astype(BF16)

    def col(piece):
        return pl.BlockSpec((t, cw), lambda j: (0, piece * nper + j))

    return pl.pallas_call(
        body, name="conv_fwd",
        grid=(nper,),
        in_specs=[col(COL_CH), col(COL_CB), col(COL_CC), pl.BlockSpec((3, cw), lambda j: (0, j))],
        out_specs=pl.BlockSpec((t, cw), lambda j: (0, j)),
        out_shape=jax.ShapeDtypeStruct((t, D_MODEL), BF16),
        compiler_params=pltpu.CompilerParams(dimension_semantics=("parallel",)),
    )(proj, proj, proj, conv_w)


def _scan_matrix():
    s = lax.broadcasted_iota(jnp.int32, (SB_BLOCK, SB_BLOCK), 0)
    j = lax.broadcasted_iota(jnp.int32, (SB_BLOCK, SB_BLOCK), 1)
    return jnp.where(j > s, 1.0, 0.0).astype(BF16)


def _suffix_sum(u_mat, xv):
    hi = xv.astype(BF16)
    lo = (xv - hi.astype(F32)).astype(BF16)
    return _dot(u_mat, hi) + _dot(u_mat, lo)


def _head_rows(vt, h):
    row = lax.broadcasted_iota(jnp.int32, vt.shape, 0)
    return jnp.where((row >= h * SB_HEAD_DIM) & (row < (h + 1) * SB_HEAD_DIM), vt, 0.0).astype(BF16)


def _head_lanes(v, h):
    lane = lax.broadcasted_iota(jnp.int32, v.shape, 1)
    return jnp.where((lane >= h * SB_HEAD_DIM) & (lane < (h + 1) * SB_HEAD_DIM), v, 0.0).astype(BF16)


def _group_suffix(u_mat, xv, carry):
    nblk = xv.shape[0] // SB_BLOCK
    parts = [None] * nblk
    for j in reversed(range(nblk)):
        xj = xv[j * SB_BLOCK:(j + 1) * SB_BLOCK]
        parts[j] = _suffix_sum(u_mat, xj) + carry
        carry = carry + jnp.sum(xj, axis=0, keepdims=True)
    return jnp.concatenate(parts, axis=0), carry


def _sb_probs(kgrp, qt_h, u_mat, carry, past):
    z = _dot(kgrp, qt_h)
    lb = _log_sigmoid(z)
    l1 = lb - z
    if past is not None:
        l1 = jnp.where(past, l1, 0.0)
    between, carry = _group_suffix(u_mat, l1, carry)
    a = jnp.exp(lb + between)
    if past is not None:
        a = jnp.where(past, a, 0.0)
    return a, lb, carry


def _sb_sweep(qi, tq, group, state):
    past = lax.broadcasted_iota(jnp.int32, (tq, tq), 0) < lax.broadcasted_iota(jnp.int32, (tq, tq), 1)
    state = group(qi, state, past)
    return lax.fori_loop(0, qi, lambda i, st: group(qi - 1 - i, st, None), state)


def _sb_fwd(proj, vt4, tq, shards):
    t = proj.shape[0]
    pairs = D_MODEL // SB_BLOCK
    nq = t // tq
    n = len(shards)

    def body(q_ref, k_ref, vt_ref, *rest):
        o_ref = rest[n]
        begin, relay, finish = _gather_phases(rest[:n], rest[n + 1:2 * n + 1], *rest[2 * n + 1:])
        pi, qi = pl.program_id(0), pl.program_id(1)
        pl.when((pi == 0) & (qi == 0))(begin)
        pl.when((pi == pairs // 2) & (qi == 0))(relay)
        qt = q_ref[...].astype(F32).T * (SB_HEAD_DIM ** -0.5)
        qts = [_head_rows(qt, h) for h in range(2)]
        u_mat = _scan_matrix()

        def group(g, state, past):
            kgrp = k_ref[pl.ds(pl.multiple_of(g * tq, tq), tq), :]
            vt = vt_ref[g]
            out = []
            for h in range(2):
                acc, carry = state[h]
                a, _, carry = _sb_probs(kgrp, qts[h], u_mat, carry, past)
                acc = acc + _dot(vt[h * SB_HEAD_DIM:(h + 1) * SB_HEAD_DIM, :], a.astype(BF16))
                out.append((acc, carry))
            return tuple(out)

        zero = (jnp.zeros((SB_HEAD_DIM, tq), F32), jnp.zeros((1, tq), F32))
        state = _sb_sweep(qi, tq, group, (zero, zero))
        o_ref[...] = jnp.concatenate([state[0][0], state[1][0]], axis=0).T
        pl.when((pi == pairs - 1) & (qi == nq - 1))(finish)

    outs = pl.pallas_call(
        body, name="sb_fwd",
        grid=(pairs, nq),
        in_specs=[pl.BlockSpec((tq, SB_BLOCK), lambda p, i: (i, COL_SQ * pairs + p)),
                  pl.BlockSpec((t, SB_BLOCK), lambda p, i: (0, COL_SK * pairs + p)),
                  pl.BlockSpec((None, nq, SB_BLOCK, tq), lambda p, i: (p, 0, 0, 0))] + [ANY] * n,
        out_specs=[pl.BlockSpec((tq, SB_BLOCK), lambda p, i: (i, p))] + [ANY] * n,
        out_shape=[jax.ShapeDtypeStruct((t, D_MODEL), F32)] + _gather_shapes(shards),
        scratch_shapes=_gather_sems(n),
        compiler_params=pltpu.CompilerParams(dimension_semantics=("arbitrary", "arbitrary")),
    )(proj, proj, vt4, *shards)
    return outs[0], outs[1:]


def _mem_prep(mem, g_mem, wkv_all, k_norm_g):
    m, d = mem.shape

    def body(mem_ref, g_ref, w_ref, kg_ref, memn_ref, kn_ref, v_ref):
        memf = mem_ref[...]
        memn = (memf * _rstd(memf) * g_ref[...]).astype(BF16)
        memn_ref[...] = memn
        for b in range(N_DEV):
            kv = _dot(memn, w_ref[b])
            if b < X_HEADS:
                kn_ref[:, b * X_HEAD_DIM:(b + 1) * X_HEAD_DIM] = (kv * _rstd(kv) * kg_ref[...]).astype(BF16)
            else:
                h = b - X_HEADS
                v_ref[:, h * X_HEAD_DIM:(h + 1) * X_HEAD_DIM] = kv.astype(BF16)

    return pl.pallas_call(
        body, name="mem_prep",
        out_shape=[jax.ShapeDtypeStruct((m, d), BF16)] * 3,
    )(mem, g_mem, wkv_all, k_norm_g)


def _x_head(xq_ref, qg, kn_ref, h):
    sl = slice(h * X_HEAD_DIM, (h + 1) * X_HEAD_DIM)
    q = xq_ref[:, sl].astype(F32)
    rq = _rstd(q)
    qhat = q * rq
    qn = (qhat * qg).astype(BF16)
    s = _dot_nt(qn, kn_ref[:, sl]) * (X_HEAD_DIM ** -0.5)
    e = jnp.exp(s - jnp.max(s, axis=-1, keepdims=True))
    p = e / jnp.sum(e, axis=-1, keepdims=True)
    return sl, rq, qhat, qn, p


def _x_fwd(proj, q_norm_g, kn, v, tm):
    t = proj.shape[0]
    m = kn.shape[0]

    def body(xq_ref, qg_ref, kn_ref, v_ref, o_ref):
        for h in range(X_HEADS):
            sl, _, _, _, p = _x_head(xq_ref, qg_ref[...], kn_ref, h)
            o_ref[:, sl] = _dot(p.astype(BF16), v_ref[:, sl]).astype(BF16)

    return pl.pallas_call(
        body, name="x_fwd",
        grid=(t // tm,),
        in_specs=[pl.BlockSpec((tm, D_MODEL), lambda i: (i, COL_XQ)),
                  pl.BlockSpec((1, X_HEAD_DIM), lambda i: (0, 0)),
                  pl.BlockSpec((m, D_MODEL), lambda i: (0, 0)),
                  pl.BlockSpec((m, D_MODEL), lambda i: (0, 0))],
        out_specs=pl.BlockSpec((tm, D_MODEL), lambda i: (i, 0)),
        out_shape=jax.ShapeDtypeStruct((t, D_MODEL), BF16),
        compiler_params=pltpu.CompilerParams(dimension_semantics=("parallel",)),
    )(proj, q_norm_g, kn, v)


def _gate_spec(tm, branch):
    return pl.BlockSpec((tm, D_MODEL), lambda i: (i, COL_GATE + branch))


def _merge_fwd(x, proj, a_conv, o_sb, o_x, w_conv_out, w_sb_out, w_x_out, w_out, tm):
    t, d = x.shape

    def body(x_ref, g0_ref, g1_ref, g2_ref, a_ref, s_ref, xo_ref, wc_ref, ws_ref, wx_ref, wo_ref,
             x1_ref, yc_ref, ys_ref, yx_ref, mg_ref):
        merged = jnp.zeros((tm, d), F32)
        for gate_ref, b_ref, w_ref, y_ref in ((g0_ref, a_ref, wc_ref, yc_ref), (g1_ref, s_ref, ws_ref, ys_ref),
                                              (g2_ref, xo_ref, wx_ref, yx_ref)):
            yv = _dot(b_ref[...].astype(BF16), w_ref[...])
            y_ref[...] = yv.astype(BF16)
            merged = merged + _sigmoid(gate_ref[...].astype(F32)) * yv
        mb = merged.astype(BF16)
        mg_ref[...] = mb
        x1_ref[...] = x_ref[...] + _dot(mb, wo_ref[...])

    tile = pl.BlockSpec((tm, d), lambda i: (i, 0))
    wfull = pl.BlockSpec((d, d), lambda i: (0, 0))
    return pl.pallas_call(
        body, name="merge_fwd",
        grid=(t // tm,),
        in_specs=[tile] + [_gate_spec(tm, b) for b in range(N_BRANCH)] + [tile, tile, tile,
                                                                           wfull, wfull, wfull, wfull],
        out_specs=[tile] * 5,
        out_shape=[jax.ShapeDtypeStruct((t, d), F32)] + [jax.ShapeDtypeStruct((t, d), BF16)] * 4,
        compiler_params=pltpu.CompilerParams(dimension_semantics=("parallel",)),
    )(x, proj, proj, proj, a_conv, o_sb, o_x, w_conv_out, w_sb_out, w_x_out, w_out)


def _mlp_fwd(x1, g_mlp, w_up_all, w_down, target, tm):
    t, d = x1.shape
    nb, _, fw = w_up_all.shape

    def body(x1_ref, g_ref, wu_ref, wd_ref, tgt_ref, up_ref, h2_ref, dx2_ref, lsum_ref, acc_ref):
        i, j = pl.program_id(0), pl.program_id(1)

        @pl.when(j == 0)
        def _():
            xf = x1_ref[...]
            h2_ref[...] = (xf * _rstd(xf) * g_ref[...]).astype(BF16)
            acc_ref[...] = jnp.zeros_like(acc_ref)

        @pl.when((i == 0) & (j == 0))
        def _():
            lsum_ref[...] = jnp.zeros_like(lsum_ref)

        up = _dot(h2_ref[...], wu_ref[...])
        up_ref[...] = up.astype(BF16)
        act = jnp.square(jnp.maximum(up, 0.0)).astype(BF16)
        acc_ref[...] += _dot(act, wd_ref[...])

        @pl.when(j == nb - 1)
        def _():
            diff = x1_ref[...] + acc_ref[...] - tgt_ref[...]
            dx2_ref[...] = diff * (1.0 / d)
            lsum_ref[...] += jnp.sum(diff * diff, axis=0, keepdims=True)

    tile = pl.BlockSpec((tm, d), lambda i, j: (i, 0))
    row = pl.BlockSpec((1, d), lambda i, j: (0, 0))
    return pl.pallas_call(
        body, name="mlp_fwd",
        grid=(t // tm, nb),
        in_specs=[tile, row, pl.BlockSpec((None, d, fw), lambda i, j: (j, 0, 0)),
                  pl.BlockSpec((fw, d), lambda i, j: (j, 0)), tile],
        out_specs=[pl.BlockSpec((tm, fw), lambda i, j: (i, j)), tile, tile, row],
        out_shape=[jax.ShapeDtypeStruct((t, nb * fw), BF16), jax.ShapeDtypeStruct((t, d), BF16),
                   jax.ShapeDtypeStruct((t, d), F32), jax.ShapeDtypeStruct((1, d), F32)],
        scratch_shapes=[pltpu.VMEM((tm, d), F32)],
        compiler_params=pltpu.CompilerParams(dimension_semantics=("arbitrary", "arbitrary")),
    )(x1, g_mlp, w_up_all, w_down, target)


def _mlp_bwd(x1, g_mlp, w_up_all, w_down, up, dx2, tm):
    t, d = x1.shape
    nb, _, fw = w_up_all.shape

    def body(x1_ref, g_ref, wu_ref, wd_ref, up_ref, dx2_ref, dup_ref, act_ref, dx1_ref, dg_ref, acc_ref, dyb_ref):
        i, j = pl.program_id(0), pl.program_id(1)

        @pl.when(j == 0)
        def _():
            dyb_ref[...] = dx2_ref[...].astype(BF16)
            acc_ref[...] = jnp.zeros_like(acc_ref)

        @pl.when((i == 0) & (j == 0))
        def _():
            dg_ref[...] = jnp.zeros_like(dg_ref)

        r = jnp.maximum(up_ref[...].astype(F32), 0.0)
        act_ref[...] = jnp.square(r).astype(BF16)
        dup = (_dot_nt(dyb_ref[...], wd_ref[...]) * (2.0 * r)).astype(BF16)
        dup_ref[...] = dup
        acc_ref[...] += _dot_nt(dup, wu_ref[...])

        @pl.when(j == nb - 1)
        def _():
            xf = x1_ref[...]
            rs = _rstd(xf)
            xhat = xf * rs
            dh2 = acc_ref[...]
            dg_ref[...] += jnp.sum(dh2 * xhat, axis=0, keepdims=True)
            dx1_ref[...] = dx2_ref[...] + _rms_bwd(dh2, xhat, rs, g_ref[...])

    tile = pl.BlockSpec((tm, d), lambda i, j: (i, 0))
    row = pl.BlockSpec((1, d), lambda i, j: (0, 0))
    ff = pl.BlockSpec((tm, fw), lambda i, j: (i, j))
    return pl.pallas_call(
        body, name="mlp_bwd",
        grid=(t // tm, nb),
        in_specs=[tile, row, pl.BlockSpec((None, d, fw), lambda i, j: (j, 0, 0)),
                  pl.BlockSpec((fw, d), lambda i, j: (j, 0)), ff, tile],
        out_specs=[ff, ff, tile, row],
        out_shape=[jax.ShapeDtypeStruct((t, nb * fw), BF16), jax.ShapeDtypeStruct((t, nb * fw), BF16),
                   jax.ShapeDtypeStruct((t, d), F32), jax.ShapeDtypeStruct((1, d), F32)],
        scratch_shapes=[pltpu.VMEM((tm, d), F32), pltpu.VMEM((tm, d), BF16)],
        compiler_params=pltpu.CompilerParams(dimension_semantics=("arbitrary", "arbitrary")),
    )(x1, g_mlp, w_up_all, w_down, up, dx2)


def _merge_bwd(dx1, proj, y_conv, y_sb, y_x, w_conv_out, w_sb_out, w_x_out, w_out, tm):
    t, d = dx1.shape

    def body(dx1_ref, g0_ref, g1_ref, g2_ref, yc_ref, ys_ref, yx_ref, wc_ref, ws_ref, wx_ref, wo_ref,
             dgate_ref, dyc_ref, dys_ref, dyx_ref, da_ref, dos_ref, dox_ref):
        dm = _dot_nt(dx1_ref[...].astype(BF16), wo_ref[...])
        for i, (gate_ref, y_ref, w_ref, dy_ref, db_ref) in enumerate(((g0_ref, yc_ref, wc_ref, dyc_ref, da_ref),
                                                                       (g1_ref, ys_ref, ws_ref, dys_ref, dos_ref),
                                                                       (g2_ref, yx_ref, wx_ref, dyx_ref, dox_ref))):
            gt = _sigmoid(gate_ref[...].astype(F32))
            dy = (dm * gt).astype(BF16)
            dy_ref[...] = dy
            dgate_ref[:, i * d:(i + 1) * d] = (dm * y_ref[...].astype(F32) * gt * (1.0 - gt)).astype(BF16)
            db_ref[...] = _dot_nt(dy, w_ref[...]).astype(BF16)

    tile = pl.BlockSpec((tm, d), lambda i: (i, 0))
    wfull = pl.BlockSpec((d, d), lambda i: (0, 0))
    return pl.pallas_call(
        body, name="merge_bwd",
        grid=(t // tm,),
        in_specs=[tile] + [_gate_spec(tm, b) for b in range(N_BRANCH)] + [tile, tile, tile,
                                                                           wfull, wfull, wfull, wfull],
        out_specs=[pl.BlockSpec((tm, N_BRANCH * d), lambda i: (i, 0))] + [tile] * 6,
        out_shape=[jax.ShapeDtypeStruct((t, N_BRANCH * d), BF16)] + [jax.ShapeDtypeStruct((t, d), BF16)] * 6,
        compiler_params=pltpu.CompilerParams(dimension_semantics=("parallel",)),
    )(dx1, proj, proj, proj, y_conv, y_sb, y_x, w_conv_out, w_sb_out, w_x_out, w_out)


def _conv_bwd(proj, conv_w, da, cw):
    t = proj.shape[0]
    nper = D_MODEL // cw

    def body(ch_ref, cb_ref, cc_ref, w_ref, da_ref, dch_ref, dcb_ref, dcc_ref, dw_ref):
        ch, cb, cc, u, u1, u2, cv, w, row = _conv_terms(ch_ref, cb_ref, cc_ref, w_ref)
        dav = da_ref[...].astype(F32)
        dcb_ref[...] = (dav * cv).astype(BF16)
        dcv = dav * cb
        n1 = jnp.where(row < t - 1, pltpu.roll(dcv, t - 1, 0), 0.0)
        n2 = jnp.where(row < t - 2, pltpu.roll(dcv, t - 2, 0), 0.0)
        du = w[2] * dcv + w[1] * n1 + w[0] * n2
        dcc_ref[...] = (du * ch).astype(BF16)
        dch_ref[...] = (du * cc).astype(BF16)
        dw_ref[0:1, :] = jnp.sum(dcv * u2, axis=0, keepdims=True)
        dw_ref[1:2, :] = jnp.sum(dcv * u1, axis=0, keepdims=True)
        dw_ref[2:3, :] = jnp.sum(dcv * u, axis=0, keepdims=True)

    def col(piece):
        return pl.BlockSpec((t, cw), lambda j: (0, piece * nper + j))

    out_col = pl.BlockSpec((t, cw), lambda j: (0, j))
    wspec = pl.BlockSpec((3, cw), lambda j: (0, j))
    return pl.pallas_call(
        body, name="conv_bwd",
        grid=(nper,),
        in_specs=[col(COL_CH), col(COL_CB), col(COL_CC), wspec, out_col],
        out_specs=[out_col, out_col, out_col, wspec],
        out_shape=[jax.ShapeDtypeStruct((t, D_MODEL), BF16)] * 3 + [jax.ShapeDtypeStruct((3, D_MODEL), F32)],
        compiler_params=pltpu.CompilerParams(dimension_semantics=("parallel",)),
    )(proj, proj, proj, conv_w, da)


def _sb_bwd(proj, kt4, vt4, do_sb, o_sb, tq, pair_sums):
    t = proj.shape[0]
    nq = t // tq
    pairs = D_MODEL // SB_BLOCK
    scale = SB_HEAD_DIM ** -0.5
    n = len(pair_sums)

    def body(q_ref, k_ref, v_ref, kt_ref, vt_ref, do_ref, o_ref, *rest):
        dq_ref, dk_ref, dv_ref = rest[n:n + 3]
        dk_acc, dv_acc = rest[2 * n + 3:2 * n + 5]
        begin, finish = _chip_exchange_phases(rest[:n], rest[n + 3:2 * n + 3], *rest[2 * n + 5:])
        pi, qi = pl.program_id(0), pl.program_id(1)
        pl.when((pi == 0) & (qi == 0))(begin)

        @pl.when(qi == 0)
        def _():
            dk_acc[...] = jnp.zeros_like(dk_acc)
            dv_acc[...] = jnp.zeros_like(dv_acc)

        q = q_ref[...].astype(F32) * scale
        do = do_ref[...].astype(F32)
        qt, dot_ = q.T, do.T
        prod = dot_ * o_ref[...].T
        u_mat = _scan_matrix()
        qts = [_head_rows(qt, h) for h in range(2)]
        dots = [_head_rows(dot_, h) for h in range(2)]
        qms = [_head_lanes(q, h) for h in range(2)]
        doms = [_head_lanes(do, h) for h in range(2)]
        dsum = [jnp.sum(prod[h * SB_HEAD_DIM:(h + 1) * SB_HEAD_DIM, :], axis=0, keepdims=True) for h in range(2)]

        def group(g, state, past):
            ks = pl.multiple_of(g * tq, tq)
            kgrp = k_ref[pl.ds(ks, tq), :]
            vgrp = v_ref[pl.ds(ks, tq), :]
            kt = kt_ref[g]
            dk_add = jnp.zeros((tq, SB_BLOCK), F32)
            dv_add = jnp.zeros((tq, SB_BLOCK), F32)
            out = []
            for h in range(2):
                dqt, carry_l, carry_g = state[h]
                a, lb, carry_l = _sb_probs(kgrp, qts[h], u_mat, carry_l, past)
                ab = a.astype(BF16)
                gw = _dot(vgrp, dots[h]) * ab.astype(F32)
                after, carry_g = _group_suffix(u_mat, gw, carry_g)
                before = dsum[h] - (after + gw)
                sig = jnp.exp(lb)
                dz = gw * (1.0 - sig) - before * sig
                if past is not None:
                    dz = jnp.where(past, dz, 0.0)
                dzb = dz.astype(BF16)
                dqt = dqt + _dot(kt[h * SB_HEAD_DIM:(h + 1) * SB_HEAD_DIM, :], dzb)
                dk_add = dk_add + _dot(dzb, qms[h])
                dv_add = dv_add + _dot(ab, doms[h])
                out.append((dqt, carry_l, carry_g))
            dk_acc[pl.ds(ks, tq), :] += dk_add
            dv_acc[pl.ds(ks, tq), :] += dv_add
            return tuple(out)

        zero = (jnp.zeros((SB_HEAD_DIM, tq), F32), jnp.zeros((1, tq), F32), jnp.zeros((1, tq), F32))
        state = _sb_sweep(qi, tq, group, (zero, zero))
        dq_ref[...] = (jnp.concatenate([state[0][0], state[1][0]], axis=0).T * scale).astype(BF16)

        @pl.when(qi == nq - 1)
        def _():
            dk_ref[...] = dk_acc[...].astype(BF16)
            dv_ref[...] = dv_acc[...].astype(BF16)

        pl.when((pi == pairs - 1) & (qi == nq - 1))(finish)

    qblk = lambda base: pl.BlockSpec((tq, SB_BLOCK), lambda p, i: (i, base * pairs + p))
    seq = lambda base: pl.BlockSpec((t, SB_BLOCK), lambda p, i: (0, base * pairs + p))
    tr = pl.BlockSpec((None, nq, SB_BLOCK, tq), lambda p, i: (p, 0, 0, 0))
    outs = pl.pallas_call(
        body, name="sb_bwd",
        grid=(pairs, nq),
        in_specs=[qblk(COL_SQ), seq(COL_SK), seq(COL_SV), tr, tr, qblk(0), qblk(0)] + [ANY] * n,
        out_specs=[qblk(0), seq(0), seq(0)] + [ANY] * n,
        out_shape=[jax.ShapeDtypeStruct((t, D_MODEL), BF16)] * 3 + _chip_exchange_shapes(pair_sums),
        scratch_shapes=[pltpu.VMEM((t, SB_BLOCK), F32), pltpu.VMEM((t, SB_BLOCK), F32)] + _chip_exchange_sems(n),
        compiler_params=pltpu.CompilerParams(dimension_semantics=("arbitrary", "arbitrary")),
    )(proj, proj, proj, kt4, vt4, do_sb, o_sb, *pair_sums)
    return outs[0], outs[1], outs[2], outs[3:]


def _x_bwd(proj, q_norm_g, kn, v, do_x, tm):
    t = proj.shape[0]
    m = kn.shape[0]
    scale = X_HEAD_DIM ** -0.5

    def body(xq_ref, qg_ref, kn_ref, v_ref, do_ref, dxq_ref, dkn_ref, dv_ref, dqg_ref):
        @pl.when(pl.program_id(0) == 0)
        def _():
            dkn_ref[...] = jnp.zeros_like(dkn_ref)
            dv_ref[...] = jnp.zeros_like(dv_ref)
            dqg_ref[...] = jnp.zeros_like(dqg_ref)

        qg = qg_ref[...]
        for h in range(X_HEADS):
            sl, rq, qhat, qn, p = _x_head(xq_ref, qg, kn_ref, h)
            do_h = do_ref[:, sl]
            dp = _dot_nt(do_h, v_ref[:, sl])
            ds = (p * (dp - jnp.sum(dp * p, axis=-1, keepdims=True)) * scale).astype(BF16)
            dqn = _dot(ds, kn_ref[:, sl])
            dkn_ref[:, sl] += _dot_tn(ds, qn)
            dv_ref[:, sl] += _dot_tn(p.astype(BF16), do_h)
            dqg_ref[...] += jnp.sum(dqn * qhat, axis=0, keepdims=True)
            dxq_ref[:, sl] = _rms_bwd(dqn, qhat, rq, qg).astype(BF16)

    full = pl.BlockSpec((m, D_MODEL), lambda i: (0, 0))
    gain = pl.BlockSpec((1, X_HEAD_DIM), lambda i: (0, 0))
    tile = pl.BlockSpec((tm, D_MODEL), lambda i: (i, 0))
    return pl.pallas_call(
        body, name="x_bwd",
        grid=(t // tm,),
        in_specs=[pl.BlockSpec((tm, D_MODEL), lambda i: (i, COL_XQ)), gain, full, full, tile],
        out_specs=[tile, full, full, gain],
        out_shape=[jax.ShapeDtypeStruct((t, D_MODEL), BF16), jax.ShapeDtypeStruct((m, D_MODEL), F32),
                   jax.ShapeDtypeStruct((m, D_MODEL), F32), jax.ShapeDtypeStruct((1, X_HEAD_DIM), F32)],
        compiler_params=pltpu.CompilerParams(dimension_semantics=("arbitrary",)),
    )(proj, q_norm_g, kn, v, do_x)


def _mem_bwd(mem, g_mem, wkv_all, k_norm_g, dkn, dv):
    m, d = mem.shape

    def body(mem_ref, g_ref, w_ref, kg_ref, dkn_ref, dv_ref, dkv_ref, dgm_ref, dkg_ref):
        memf = mem_ref[...]
        mem_hat = memf * _rstd(memf)
        memn = (mem_hat * g_ref[...]).astype(BF16)
        kg = kg_ref[...]
        dmemn = jnp.zeros((m, d), F32)
        dkg = jnp.zeros((1, X_HEAD_DIM), F32)
        for b in range(N_DEV):
            sl = slice(b * X_HEAD_DIM, (b + 1) * X_HEAD_DIM)
            if b < X_HEADS:
                kv = _dot(memn, w_ref[b])
                rk = _rstd(kv)
                khat = kv * rk
                dkn_h = dkn_ref[:, sl]
                dkg = dkg + jnp.sum(dkn_h * khat, axis=0, keepdims=True)
                dblk = _rms_bwd(dkn_h, khat, rk, kg).astype(BF16)
            else:
                hs = slice((b - X_HEADS) * X_HEAD_DIM, (b - X_HEADS + 1) * X_HEAD_DIM)
                dblk = dv_ref[:, hs].astype(BF16)
            dkv_ref[:, sl] = dblk
            dmemn = dmemn + _dot_nt(dblk, w_ref[b])
        dgm_ref[...] = jnp.sum(dmemn * mem_hat, axis=0, keepdims=True)
        dkg_ref[...] = dkg

    return pl.pallas_call(
        body, name="mem_bwd",
        out_shape=[jax.ShapeDtypeStruct((m, 2 * d), BF16), jax.ShapeDtypeStruct((1, d), F32),
                   jax.ShapeDtypeStruct((1, X_HEAD_DIM), F32)],
    )(mem, g_mem, wkv_all, k_norm_g, dkn, dv)


def _in_proj_bwd(x, g_mix, w_in_all, dproj, dx1, tm):
    t, d = x.shape
    nb, _, bw = w_in_all.shape

    def body(x_ref, g_ref, w_ref, dp_ref, dx1_ref, dx_ref, dg_ref, acc_ref):
        i, j = pl.program_id(0), pl.program_id(1)

        @pl.when(j == 0)
        def _():
            acc_ref[...] = jnp.zeros_like(acc_ref)

        @pl.when((i == 0) & (j == 0))
        def _():
            dg_ref[...] = jnp.zeros_like(dg_ref)

        acc_ref[...] += _dot_nt(dp_ref[...], w_ref[...])

        @pl.when(j == nb - 1)
        def _():
            xf = x_ref[...]
            rs = _rstd(xf)
            xhat = xf * rs
            dh = acc_ref[...]
            dg_ref[...] += jnp.sum(dh * xhat, axis=0, keepdims=True)
            dx_ref[...] = dx1_ref[...] + _rms_bwd(dh, xhat, rs, g_ref[...])

    tile = pl.BlockSpec((tm, d), lambda i, j: (i, 0))
    row = pl.BlockSpec((1, d), lambda i, j: (0, 0))
    return pl.pallas_call(
        body, name="in_proj_bwd",
        grid=(t // tm, nb),
        in_specs=[tile, row, pl.BlockSpec((None, d, bw), lambda i, j: (j, 0, 0)),
                  pl.BlockSpec((tm, bw), lambda i, j: (i, j)), tile],
        out_specs=[tile, row],
        out_shape=[jax.ShapeDtypeStruct((t, d), F32), jax.ShapeDtypeStruct((1, d), F32)],
        scratch_shapes=[pltpu.VMEM((tm, d), F32)],
        compiler_params=pltpu.CompilerParams(dimension_semantics=("arbitrary", "arbitrary")),
    )(x, g_mix, w_in_all, dproj, dx1)


def _weight_grad(a, b, bw, tmm, name):
    t, m = a.shape
    n = b.shape[1]
    tmm = min(tmm, m)

    def body(a_ref, b_ref, o_ref):
        o_ref[...] = _dot_tn(a_ref[...].astype(BF16), b_ref[...].astype(BF16)).astype(BF16)

    return pl.pallas_call(
        body, name=name,
        grid=(m // tmm, n // bw),
        in_specs=[pl.BlockSpec((t, tmm), lambda i, j: (0, i)), pl.BlockSpec((t, bw), lambda i, j: (0, j))],
        out_specs=pl.BlockSpec((None, tmm, bw), lambda i, j: (j, i, 0)),
        out_shape=jax.ShapeDtypeStruct((n // bw, m, bw), BF16),
        compiler_params=pltpu.CompilerParams(dimension_semantics=("parallel", "parallel")),
    )(a, b)


def _pair_sum(grad, recv, own_blocks, name):
    _, rows, cols = grad.shape

    def body(idx_ref, g_ref, r_ref, o_ref):
        o_ref[...] = (g_ref[...].astype(F32) + r_ref[...].astype(F32)).astype(BF16)

    return pl.pallas_call(
        body, name=name,
        grid_spec=pltpu.PrefetchScalarGridSpec(
            num_scalar_prefetch=1, grid=(4,),
            in_specs=[pl.BlockSpec((None, rows, cols), lambda r, idx: (idx[r], 0, 0)),
                      pl.BlockSpec((None, rows, cols), lambda r, idx: (r, 0, 0))],
            out_specs=pl.BlockSpec((None, rows, cols), lambda r, idx: (r, 0, 0))),
        out_shape=jax.ShapeDtypeStruct((4, rows, cols), BF16),
        compiler_params=pltpu.CompilerParams(dimension_semantics=("parallel",)),
    )(own_blocks, grad, recv)


def _adamw_math(w, g, m, v):
    m = ADAM_B1 * m + (1.0 - ADAM_B1) * g
    v = ADAM_B2 * v + (1.0 - ADAM_B2) * jnp.square(g)
    m_hat = m / (1.0 - ADAM_B1 ** ADAM_STEP)
    v_hat = v / (1.0 - ADAM_B2 ** ADAM_STEP)
    delta = -ADAM_LR * (m_hat / (jnp.sqrt(v_hat) + ADAM_EPS) + ADAM_WD * w)
    return delta, m, v


def _adamw_sharded(pair_sums, recv, w, m, v, tr, name):
    rows, cols = w.shape
    tr = min(tr, rows)

    def body(h_ref, r_ref, w_ref, m_ref, v_ref, g_out, d_out, m_out, v_out):
        g = h_ref[...].astype(F32)
        for r in range(3):
            g = g + r_ref[r].astype(F32)
        g_out[...] = g
        d_out[...], m_out[...], v_out[...] = _adamw_math(w_ref[...], g, m_ref[...], v_ref[...])

    tile = pl.BlockSpec((tr, cols), lambda i: (i, 0))
    return pl.pallas_call(
        body, name=name,
        grid=(rows // tr,),
        in_specs=[pl.BlockSpec((None, tr, cols), lambda i: (0, i, 0)),
                  pl.BlockSpec((3, tr, cols), lambda i: (0, i, 0)), tile, tile, tile],
        out_specs=[tile] * 4,
        out_shape=[jax.ShapeDtypeStruct((rows, cols), F32)] * 4,
        compiler_params=pltpu.CompilerParams(dimension_semantics=("parallel",)),
    )(pair_sums, recv, w, m, v)


def _small_sum(gathered):
    _, rows, cols = gathered.shape

    def body(g_ref, o_ref):
        total = g_ref[0]
        for dev in range(1, N_DEV):
            total = total + g_ref[dev]
        o_ref[...] = total

    return pl.pallas_call(body, name="small_grad_sum", out_shape=jax.ShapeDtypeStruct((rows, cols), F32))(gathered)


def _adamw_small(w, g, m, v):
    def body(w_ref, g_ref, m_ref, v_ref, d_out, m_out, v_out):
        d_out[...], m_out[...], v_out[...] = _adamw_math(w_ref[...], g_ref[...], m_ref[...], v_ref[...])

    return pl.pallas_call(body, name="adamw_small", out_shape=[jax.ShapeDtypeStruct(w.shape, F32)] * 3)(w, g, m, v)


def _pad_tile(a):
    return jnp.pad(a, ((0, SMALL_TILE - a.shape[0]), (0, D_MODEL - a.shape[1])))


def _pack_small(*pieces):
    return jnp.concatenate([_pad_tile(a) for a in pieces], axis=0)


def kernel(x, mem, g_mix, g_mem, w_in, conv_w, w_conv_out, w_sb_out, q_norm_g, k_norm_g, w_mem_kv, w_x_out, w_out, g_mlp, w_up, w_down, loss_target, m_g_mix, m_g_mem, m_w_in, m_conv_w, m_w_conv_out, m_w_sb_out, m_q_norm_g, m_k_norm_g, m_w_mem_kv, m_w_x_out, m_w_out, m_g_mlp, m_w_up, m_w_down, v_g_mix, v_g_mem, v_w_in, v_conv_w, v_w_conv_out, v_w_sb_out, v_q_norm_g, v_k_norm_g, v_w_mem_kv, v_w_x_out, v_w_out, v_g_mlp, v_w_up, v_w_down):
    xpos, ypos, cpos = _mesh_pos()
    me = 4 * xpos + 2 * ypos + cpos
    x2d, mem2d, tgt2d = x[0], mem[0], loss_target[0]
    t = x2d.shape[0]
    tm = min(512, t)
    tm_s = min(256, t)

    big = {
        "w_in": (w_in[0], m_w_in[0], v_w_in[0]),
        "w_conv_out": (w_conv_out[0], m_w_conv_out[0], v_w_conv_out[0]),
        "w_sb_out": (w_sb_out[0], m_w_sb_out[0], v_w_sb_out[0]),
        "w_mem_kv": (w_mem_kv[0], m_w_mem_kv[0], v_w_mem_kv[0]),
        "w_x_out": (w_x_out[0], m_w_x_out[0], v_w_x_out[0]),
        "w_out": (w_out[0], m_w_out[0], v_w_out[0]),
        "w_up": (w_up[0], m_w_up[0], v_w_up[0]),
        "w_down": (w_down[0], m_w_down[0], v_w_down[0]),
    }
    late = [n for n in big if n != "w_in"]
    conv_pad = jnp.pad(conv_w[0], ((0, 8 - 3), (0, 0)))
    w_in_all, conv_all = _all_gather([big["w_in"][0].astype(BF16), conv_pad])
    conv_full = conv_all[:, :3, :].transpose(1, 0, 2).reshape(3, D_MODEL)

    proj, h = _in_proj(x2d, g_mix, w_in_all, tm)
    a_conv = _conv_fwd(proj, conv_full, 256)
    tq = min(SB_QUERY_TILE, t)
    pairs = D_MODEL // SB_BLOCK

    def groups_t(cols):
        return cols.reshape(t // tq, tq, pairs, SB_BLOCK).transpose(2, 0, 3, 1)

    kt4 = groups_t(proj[:, COL_SK * D_MODEL:(COL_SK + 1) * D_MODEL])
    vt4 = groups_t(proj[:, COL_SV * D_MODEL:(COL_SV + 1) * D_MODEL])
    o_sb, gathered = _sb_fwd(proj, vt4, tq, [big[n][0].astype(BF16) for n in late])
    full = dict(zip(late, gathered))
    wkv_all, w_up_all = full["w_mem_kv"], full["w_up"]
    rows_full = lambda a: a.reshape(a.shape[0] * a.shape[1], a.shape[2])
    wc, ws, wx, wo, wd = (rows_full(full[n]) for n in ("w_conv_out", "w_sb_out", "w_x_out", "w_out", "w_down"))
    mem_n, kn, vmem = _mem_prep(mem2d, g_mem, wkv_all, k_norm_g)
    o_x = _x_fwd(proj, q_norm_g, kn, vmem, tm_s)
    x1, y_conv, y_sb, y_x, merged = _merge_fwd(x2d, proj, a_conv, o_sb, o_x, wc, ws, wx, wo, tm_s)
    up, h2, dx2, lsum = _mlp_fwd(x1, g_mlp, w_up_all, wd, tgt2d, tm)

    dup, act, dx1, dg_mlp = _mlp_bwd(x1, g_mlp, w_up_all, wd, up, dx2, tm)
    dgate, dy_conv, dy_sb, dy_x, da_conv, do_sb, do_x = _merge_bwd(dx1, proj, y_conv, y_sb, y_x, wc, ws, wx, wo, tm_s)
    dch, dcb, dcc, dconv = _conv_bwd(proj, conv_full, da_conv, 256)
    dxq, dkn, dvm, dqg = _x_bwd(proj, q_norm_g, kn, vmem, do_x, tm_s)
    dkv, dg_mem, dkg = _mem_bwd(mem2d, g_mem, wkv_all, k_norm_g, dkn, dvm)
    wgrads = {
        "w_conv_out": _weight_grad(a_conv, dy_conv, D_MODEL, 512, "dw_conv_out"),
        "w_sb_out": _weight_grad(o_sb, dy_sb, D_MODEL, 512, "dw_sb_out"),
        "w_mem_kv": _weight_grad(mem_n, dkv, wkv_all.shape[2], 512, "dw_mem_kv"),
        "w_x_out": _weight_grad(o_x, dy_x, D_MODEL, 512, "dw_x_out"),
        "w_out": _weight_grad(merged, dx1, D_MODEL, 512, "dw_out"),
        "w_up": _weight_grad(h2, dup, w_up_all.shape[2], 512, "dw_up"),
        "w_down": _weight_grad(act, dx2, D_MODEL, 512, "dw_down"),
    }

    own_blocks = jnp.stack([4 * (xpos ^ dx) + 2 * (ypos ^ dy) + cpos for dx in (0, 1) for dy in (0, 1)]).astype(jnp.int32)

    def pair_reduce(group):
        blocked = [wgrads[n].reshape((N_DEV,) + big[n][0].shape) for n in group]
        from_sibling = _pair_exchange(blocked, "grad_pair_exchange_" + group[0])
        return [_pair_sum(g, r, own_blocks, "pair_sum_" + n) for n, g, r in zip(group, blocked, from_sibling)]

    pair_sums = dict(zip(late, pair_reduce(late)))
    dq, dk, dv, from_chips_late = _sb_bwd(proj, kt4, vt4, do_sb, o_sb, tq, [pair_sums[n] for n in late])
    from_chips = dict(zip(late, from_chips_late))
    dproj = jnp.concatenate([dch, dcb, dcc, dq, dk, dv, dxq, dgate], axis=1)
    grad_x, dg_mix = _in_proj_bwd(x2d, g_mix, w_in_all, dproj, dx1, tm)
    wgrads["w_in"] = _weight_grad(h, dproj, w_in_all.shape[2], 512, "dw_in")
    pair_sums["w_in"], = pair_reduce(["w_in"])
    from_chips["w_in"], = _chip_exchange([pair_sums["w_in"]])
    res = {}
    for n in big:
        w_sh, m_sh, v_sh = big[n]
        res[n] = _adamw_sharded(pair_sums[n], from_chips[n], w_sh, m_sh, v_sh, 256, "adamw_" + n)

    part = _pack_small(dg_mix, dg_mem, dg_mlp, dqg, dkg, dconv, lsum)
    gsum = _small_sum(_small_all_gather(part))
    loss = 0.5 * jnp.sum(gsum[6 * SMALL_TILE]) / D_MODEL
    conv_cols = lax.dynamic_slice(gsum[5 * SMALL_TILE:6 * SMALL_TILE], (0, me * (D_MODEL // N_DEV)),
                                  (SMALL_TILE, D_MODEL // N_DEV))
    g_small = jnp.concatenate([gsum[:5 * SMALL_TILE], _pad_tile(conv_cols)], axis=0)
    w_small = _pack_small(g_mix, g_mem, g_mlp, q_norm_g, k_norm_g, conv_w[0])
    m_small = _pack_small(m_g_mix, m_g_mem, m_g_mlp, m_q_norm_g, m_k_norm_g, m_conv_w[0])
    v_small = _pack_small(v_g_mix, v_g_mem, v_g_mlp, v_q_norm_g, v_k_norm_g, v_conv_w[0])
    d_small, nm_small, nv_small = _adamw_small(w_small, g_small, m_small, v_small)

    def unpack(p):
        return {"g_mix": p[0:1], "g_mem": p[8:9], "g_mlp": p[16:17], "q_norm_g": p[24:25, :X_HEAD_DIM],
                "k_norm_g": p[32:33, :X_HEAD_DIM], "conv_w": p[40:43, :D_MODEL // N_DEV][None]}

    small = [unpack(p) for p in (g_small, d_small, nm_small, nv_small)]
    order = ["g_mix", "g_mem", "w_in", "conv_w", "w_conv_out", "w_sb_out", "q_norm_g", "k_norm_g", "w_mem_kv",
             "w_x_out", "w_out", "g_mlp", "w_up", "w_down"]
    outs = [loss, grad_x[None]]
    for kind in range(4):
        for n in order:
            outs.append(res[n][kind][None] if n in res else small[kind][n])
    return tuple(outs)
```

```python
import jax
import jax.numpy as jnp
from jax import lax
from jax.experimental import pallas as pl
from jax.experimental.pallas import tpu as pltpu

F32 = jnp.float32
BF16 = jnp.bfloat16
MESH = pl.DeviceIdType.MESH

EPS = 1e-6
N_DEV = 8
D_MODEL = 1024
SB_HEAD_DIM = 64
SB_BLOCK = 128
SB_QUERY_TILE = 512
X_HEADS = 4
X_HEAD_DIM = 256
N_BRANCH = 3
COL_CH, COL_CB, COL_CC, COL_SQ, COL_SK, COL_SV, COL_XQ, COL_GATE = 0, 1, 2, 3, 4, 5, 6, 7

ADAM_LR = 0.001
ADAM_B1 = 0.9
ADAM_B2 = 0.999
ADAM_EPS = 1e-08
ADAM_WD = 0.01
ADAM_STEP = 10

SMALL_TILE = 8


def _dot(a, b):
    return jnp.dot(a, b, preferred_element_type=F32)


def _dot_nt(a, b):
    return lax.dot_general(a, b, (((1,), (1,)), ((), ())), preferred_element_type=F32)


def _dot_tn(a, b):
    return lax.dot_general(a, b, (((0,), (0,)), ((), ())), preferred_element_type=F32)


def _rstd(xf):
    return lax.rsqrt(jnp.mean(xf * xf, axis=-1, keepdims=True) + EPS)


def _sigmoid(z):
    return 1.0 / (1.0 + jnp.exp(-z))


def _log_sigmoid(z):
    return jnp.minimum(z, 0.0) - jnp.log(1.0 + jnp.exp(-jnp.abs(z)))


def _rms_bwd(dy, xhat, r, g):
    dxhat = dy * g
    return r * (dxhat - xhat * jnp.mean(dxhat * xhat, axis=-1, keepdims=True))


def _mesh_pos():
    return lax.axis_index("x"), lax.axis_index("y"), lax.axis_index("c")


ANY = pl.BlockSpec(memory_space=pl.ANY)


def _all_gather(shards):
    n = len(shards)

    def body(*refs):
        begin, relay, finish = _gather_phases(refs[:n], refs[n:2 * n], *refs[2 * n:])
        begin()
        relay()
        finish()

    return pl.pallas_call(
        body, name="weights_all_gather",
        out_shape=_gather_shapes(shards),
        in_specs=[ANY] * n, out_specs=[ANY] * n,
        scratch_shapes=_gather_sems(n),
    )(*shards)


def _gather_shapes(shards):
    return [jax.ShapeDtypeStruct((N_DEV,) + s.shape, s.dtype) for s in shards]


def _gather_sems(n):
    return [pltpu.SemaphoreType.DMA((n, 7)), pltpu.SemaphoreType.DMA((n, 7)), pltpu.SemaphoreType.DMA((n,))]


def _gather_phases(ins, outs, send_sems, recv_sems, local_sems):
    n = len(ins)
    x, y, c = _mesh_pos()
    me, sibling = (x, y, c), (x, y, 1 - c)
    chips = [(1 - x, y), (x, 1 - y), (1 - x, 1 - y)]

    def blk(a, px, py, pc):
        return outs[a].at[4 * px + 2 * py + pc]

    def copy(a, k, block, to, src=None):
        return pltpu.make_async_remote_copy(
            src_ref=blk(a, *block) if src is None else src, dst_ref=blk(a, *block),
            send_sem=send_sems.at[a, k], recv_sem=recv_sems.at[a, k], device_id=to, device_id_type=MESH)

    def local(a):
        return pltpu.make_async_copy(ins[a], blk(a, *me), local_sems.at[a])

    def own(a):
        return [copy(a, 0, me, sibling, src=ins[a])] + [copy(a, 1 + j, me, (*chip, c), src=ins[a])
                                                        for j, chip in enumerate(chips)]

    def begin():
        for a in range(n):
            local(a).start()
        for a in range(n):
            for cp in own(a):
                cp.start()

    def relay():
        for j, chip in enumerate(chips):
            for a in range(n):
                copy(a, 1 + j, (*chip, c), me).wait_recv()
                copy(a, 4 + j, (*chip, c), sibling).start()

    def finish():
        for a in range(n):
            copy(a, 0, sibling, me).wait_recv()
            for j, chip in enumerate(chips):
                copy(a, 4 + j, (*chip, 1 - c), me).wait_recv()
        for a in range(n):
            for cp in own(a):
                cp.wait_send()
            for j, chip in enumerate(chips):
                copy(a, 4 + j, (*chip, c), sibling).wait_send()
            local(a).wait()

    return begin, relay, finish


def _pair_exchange(grads, name):
    n = len(grads)

    def body(*refs):
        ins, outs = refs[:n], refs[n:2 * n]
        send_sems, recv_sems = refs[2 * n:]
        x, y, c = _mesh_pos()
        xs, ys = (x, 1 - x), (y, 1 - y)
        copies = []
        for a in range(n):
            for r in range(4):
                dx, dy = divmod(r, 2)
                copies.append(pltpu.make_async_remote_copy(
                    src_ref=ins[a].at[4 * xs[dx] + 2 * ys[dy] + (1 - c)], dst_ref=outs[a].at[r],
                    send_sem=send_sems.at[a, r], recv_sem=recv_sems.at[a, r],
                    device_id=(x, y, 1 - c), device_id_type=MESH))
        for cp in copies:
            cp.start()
        for cp in copies:
            cp.wait()

    return pl.pallas_call(
        body, name=name,
        out_shape=[jax.ShapeDtypeStruct((4,) + g.shape[1:], g.dtype) for g in grads],
        in_specs=[ANY] * n, out_specs=[ANY] * n,
        scratch_shapes=[pltpu.SemaphoreType.DMA((n, 4)), pltpu.SemaphoreType.DMA((n, 4))],
    )(*grads)


def _chip_exchange_shapes(sums):
    return [jax.ShapeDtypeStruct((3,) + s.shape[1:], s.dtype) for s in sums]


def _chip_exchange_sems(n):
    return [pltpu.SemaphoreType.DMA((n, 3)), pltpu.SemaphoreType.DMA((n, 3))]


def _chip_exchange_phases(ins, outs, send_sems, recv_sems):
    x, y, c = _mesh_pos()
    xs, ys = (x, 1 - x), (y, 1 - y)

    def copies():
        out = []
        for a in range(len(ins)):
            for r in range(1, 4):
                dx, dy = divmod(r, 2)
                out.append(pltpu.make_async_remote_copy(
                    src_ref=ins[a].at[r], dst_ref=outs[a].at[r - 1],
                    send_sem=send_sems.at[a, r - 1], recv_sem=recv_sems.at[a, r - 1],
                    device_id=(xs[dx], ys[dy], c), device_id_type=MESH))
        return out

    def begin():
        for cp in copies():
            cp.start()

    def finish():
        for cp in copies():
            cp.wait()

    return begin, finish


def _small_all_gather(part):
    rows, cols = part.shape

    def body(in_ref, out_ref, send_sems, recv_sems):
        x, y, c = _mesh_pos()
        xs, ys, cs = (x, 1 - x), (y, 1 - y), (c, 1 - c)
        out_ref[4 * x + 2 * y + c] = in_ref[...]
        copies = []
        for k in range(1, N_DEV):
            dx, dy, dc = k // 4, (k // 2) % 2, k % 2
            copies.append((
                pltpu.make_async_remote_copy(
                    src_ref=in_ref, dst_ref=out_ref.at[4 * x + 2 * y + c],
                    send_sem=send_sems.at[k - 1], recv_sem=recv_sems.at[k - 1],
                    device_id=(xs[dx], ys[dy], cs[dc]), device_id_type=MESH),
                pltpu.make_async_remote_copy(
                    src_ref=in_ref, dst_ref=out_ref.at[4 * xs[dx] + 2 * ys[dy] + cs[dc]],
                    send_sem=send_sems.at[k - 1], recv_sem=recv_sems.at[k - 1],
                    device_id=(xs[dx], ys[dy], cs[dc]), device_id_type=MESH)))
        for send, _ in copies:
            send.start()
        for send, recv in copies:
            recv.wait_recv()
            send.wait_send()

    return pl.pallas_call(
        body, name="small_all_gather",
        out_shape=jax.ShapeDtypeStruct((N_DEV, rows, cols), part.dtype),
        in_specs=[pl.BlockSpec(memory_space=pltpu.VMEM)],
        out_specs=pl.BlockSpec(memory_space=pltpu.VMEM),
        scratch_shapes=[pltpu.SemaphoreType.DMA((N_DEV - 1,)), pltpu.SemaphoreType.DMA((N_DEV - 1,))],
    )(part)


def _in_proj(x, g_mix, w_in_all, tm, shards):
    t, d = x.shape
    nb, _, bw = w_in_all.shape
    nt = t // tm
    n = len(shards)

    def body(x_ref, g_ref, w_ref, *rest):
        proj_ref, h_ref = rest[n:n + 2]
        begin, relay, finish = _gather_phases(rest[:n], rest[n + 2:2 * n + 2], *rest[2 * n + 2:])
        i, j = pl.program_id(0), pl.program_id(1)
        pl.when((i == 0) & (j == 0))(begin)
        pl.when((i == nt // 2) & (j == 0))(relay)

        @pl.when(j == 0)
        def _():
            xf = x_ref[...]
            h_ref[...] = (xf * _rstd(xf) * g_ref[...]).astype(BF16)

        proj_ref[...] = _dot(h_ref[...], w_ref[...]).astype(BF16)
        pl.when((i == nt - 1) & (j == nb - 1))(finish)

    outs = pl.pallas_call(
        body, name="in_proj",
        grid=(nt, nb),
        in_specs=[pl.BlockSpec((tm, d), lambda i, j: (i, 0)),
                  pl.BlockSpec((1, d), lambda i, j: (0, 0)),
                  pl.BlockSpec((None, d, bw), lambda i, j: (j, 0, 0))] + [ANY] * n,
        out_specs=[pl.BlockSpec((tm, bw), lambda i, j: (i, j)),
                   pl.BlockSpec((tm, d), lambda i, j: (i, 0))] + [ANY] * n,
        out_shape=[jax.ShapeDtypeStruct((t, nb * bw), BF16), jax.ShapeDtypeStruct((t, d), BF16)] + _gather_shapes(shards),
        scratch_shapes=_gather_sems(n),
        compiler_params=pltpu.CompilerParams(dimension_semantics=("arbitrary", "arbitrary")),
    )(x, g_mix, w_in_all, *shards)
    return outs[0], outs[1], outs[2:]


def _conv_terms(ch_ref, cb_ref, cc_ref, w_ref):
    ch, cb, cc = ch_ref[...].astype(F32), cb_ref[...].astype(F32), cc_ref[...].astype(F32)
    u = cc * ch
    row = lax.broadcasted_iota(jnp.int32, u.shape, 0)
    u1 = jnp.where(row >= 1, pltpu.roll(u, 1, 0), 0.0)
    u2 = jnp.where(row >= 2, pltpu.roll(u, 2, 0), 0.0)
    w = (w_ref[0:1, :], w_ref[1:2, :], w_ref[2:3, :])
    cv = w[2] * u + w[1] * u1 + w[0] * u2
    return ch, cb, cc, u, u1, u2, cv, w, row


def _conv_fwd(proj, conv_w, cw):
    t = proj.shape[0]
    nper = D_MODEL // cw

    def body(ch_ref, cb_ref, cc_ref, w_ref, a_ref):
        _, cb, _, _, _, _, cv, _, _ = _conv_terms(ch_ref, cb_ref, cc_ref, w_ref)
        a_ref[...] = (cb * cv).astype(BF16)

    def col(piece):
        return pl.BlockSpec((t, cw), lambda j: (0, piece * nper + j))

    return pl.pallas_call(
        body, name="conv_fwd",
        grid=(nper,),
        in_specs=[col(COL_CH), col(COL_CB), col(COL_CC), pl.BlockSpec((3, cw), lambda j: (0, j))],
        out_specs=pl.BlockSpec((t, cw), lambda j: (0, j)),
        out_shape=jax.ShapeDtypeStruct((t, D_MODEL), BF16),
        compiler_params=pltpu.CompilerParams(dimension_semantics=("parallel",)),
    )(proj, proj, proj, conv_w)


def _scan_matrix():
    s = lax.broadcasted_iota(jnp.int32, (SB_BLOCK, SB_BLOCK), 0)
    j = lax.broadcasted_iota(jnp.int32, (SB_BLOCK, SB_BLOCK), 1)
    return jnp.where(j > s, 1.0, 0.0).astype(BF16)


def _suffix_sum(u_mat, xv):
    hi = xv.astype(BF16)
    lo = (xv - hi.astype(F32)).astype(BF16)
    return _dot(u_mat, hi) + _dot(u_mat, lo)


def _head_rows(vt, h):
    row = lax.broadcasted_iota(jnp.int32, vt.shape, 0)
    return jnp.where((row >= h * SB_HEAD_DIM) & (row < (h + 1) * SB_HEAD_DIM), vt, 0.0).astype(BF16)


def _head_lanes(v, h):
    lane = lax.broadcasted_iota(jnp.int32, v.shape, 1)
    return jnp.where((lane >= h * SB_HEAD_DIM) & (lane < (h + 1) * SB_HEAD_DIM), v, 0.0).astype(BF16)


def _group_suffix(u_mat, xv, carry):
    nblk = xv.shape[0] // SB_BLOCK
    parts = [None] * nblk
    for j in reversed(range(nblk)):
        xj = xv[j * SB_BLOCK:(j + 1) * SB_BLOCK]
        parts[j] = _suffix_sum(u_mat, xj) + carry
        carry = carry + jnp.sum(xj, axis=0, keepdims=True)
    return jnp.concatenate(parts, axis=0), carry


def _sb_probs(kgrp, qt_h, u_mat, carry, past):
    z = _dot(kgrp, qt_h)
    lb = _log_sigmoid(z)
    l1 = lb - z
    if past is not None:
        l1 = jnp.where(past, l1, 0.0)
    between, carry = _group_suffix(u_mat, l1, carry)
    a = jnp.exp(lb + between)
    if past is not None:
        a = jnp.where(past, a, 0.0)
    return a, lb, carry


def _sb_sweep(qi, tq, group, state):
    past = lax.broadcasted_iota(jnp.int32, (tq, tq), 0) < lax.broadcasted_iota(jnp.int32, (tq, tq), 1)
    state = group(qi, state, past)
    return lax.fori_loop(0, qi, lambda i, st: group(qi - 1 - i, st, None), state)


def _sb_fwd(proj, vt4, tq, shards):
    t = proj.shape[0]
    pairs = D_MODEL // SB_BLOCK
    nq = t // tq
    n = len(shards)

    def body(q_ref, k_ref, vt_ref, *rest):
        o_ref = rest[n]
        begin, relay, finish = _gather_phases(rest[:n], rest[n + 1:2 * n + 1], *rest[2 * n + 1:])
        pi, qi = pl.program_id(0), pl.program_id(1)
        pl.when((pi == 0) & (qi == 0))(begin)
        pl.when((pi == (5 * pairs) // 8) & (qi == 0))(relay)
        qt = q_ref[...].astype(F32).T * (SB_HEAD_DIM ** -0.5)
        qts = [_head_rows(qt, h) for h in range(2)]
        u_mat = _scan_matrix()

        def group(g, state, past):
            kgrp = k_ref[pl.ds(pl.multiple_of(g * tq, tq), tq), :]
            vt = vt_ref[g]
            out = []
            for h in range(2):
                acc, carry = state[h]
                a, _, carry = _sb_probs(kgrp, qts[h], u_mat, carry, past)
                acc = acc + _dot(vt[h * SB_HEAD_DIM:(h + 1) * SB_HEAD_DIM, :], a.astype(BF16))
                out.append((acc, carry))
            return tuple(out)

        zero = (jnp.zeros((SB_HEAD_DIM, tq), F32), jnp.zeros((1, tq), F32))
        state = _sb_sweep(qi, tq, group, (zero, zero))
        o_ref[...] = jnp.concatenate([state[0][0], state[1][0]], axis=0).T
        pl.when((pi == pairs - 1) & (qi == nq - 1))(finish)

    outs = pl.pallas_call(
        body, name="sb_fwd",
        grid=(pairs, nq),
        in_specs=[pl.BlockSpec((tq, SB_BLOCK), lambda p, i: (i, COL_SQ * pairs + p)),
                  pl.BlockSpec((t, SB_BLOCK), lambda p, i: (0, COL_SK * pairs + p)),
                  pl.BlockSpec((None, nq, SB_BLOCK, tq), lambda p, i: (p, 0, 0, 0))] + [ANY] * n,
        out_specs=[pl.BlockSpec((tq, SB_BLOCK), lambda p, i: (i, p))] + [ANY] * n,
        out_shape=[jax.ShapeDtypeStruct((t, D_MODEL), F32)] + _gather_shapes(shards),
        scratch_shapes=_gather_sems(n),
        compiler_params=pltpu.CompilerParams(dimension_semantics=("arbitrary", "arbitrary")),
    )(proj, proj, vt4, *shards)
    return outs[0], outs[1:]


def _mem_prep(mem, g_mem, wkv_all, k_norm_g):
    m, d = mem.shape

    def body(mem_ref, g_ref, w_ref, kg_ref, memn_ref, kn_ref, v_ref):
        memf = mem_ref[...]
        memn = (memf * _rstd(memf) * g_ref[...]).astype(BF16)
        memn_ref[...] = memn
        for b in range(N_DEV):
            kv = _dot(memn, w_ref[b])
            if b < X_HEADS:
                kn_ref[:, b * X_HEAD_DIM:(b + 1) * X_HEAD_DIM] = (kv * _rstd(kv) * kg_ref[...]).astype(BF16)
            else:
                h = b - X_HEADS
                v_ref[:, h * X_HEAD_DIM:(h + 1) * X_HEAD_DIM] = kv.astype(BF16)

    return pl.pallas_call(
        body, name="mem_prep",
        out_shape=[jax.ShapeDtypeStruct((m, d), BF16)] * 3,
    )(mem, g_mem, wkv_all, k_norm_g)


def _x_head(xq_ref, qg, kn_ref, h):
    sl = slice(h * X_HEAD_DIM, (h + 1) * X_HEAD_DIM)
    q = xq_ref[:, sl].astype(F32)
    rq = _rstd(q)
    qhat = q * rq
    qn = (qhat * qg).astype(BF16)
    s = _dot_nt(qn, kn_ref[:, sl]) * (X_HEAD_DIM ** -0.5)
    e = jnp.exp(s - jnp.max(s, axis=-1, keepdims=True))
    p = e / jnp.sum(e, axis=-1, keepdims=True)
    return sl, rq, qhat, qn, p


def _x_fwd(proj, q_norm_g, kn, v, tm):
    t = proj.shape[0]
    m = kn.shape[0]

    def body(xq_ref, qg_ref, kn_ref, v_ref, o_ref):
        for h in range(X_HEADS):
            sl, _, _, _, p = _x_head(xq_ref, qg_ref[...], kn_ref, h)
            o_ref[:, sl] = _dot(p.astype(BF16), v_ref[:, sl]).astype(BF16)

    return pl.pallas_call(
        body, name="x_fwd",
        grid=(t // tm,),
        in_specs=[pl.BlockSpec((tm, D_MODEL), lambda i: (i, COL_XQ)),
                  pl.BlockSpec((1, X_HEAD_DIM), lambda i: (0, 0)),
                  pl.BlockSpec((m, D_MODEL), lambda i: (0, 0)),
                  pl.BlockSpec((m, D_MODEL), lambda i: (0, 0))],
        out_specs=pl.BlockSpec((tm, D_MODEL), lambda i: (i, 0)),
        out_shape=jax.ShapeDtypeStruct((t, D_MODEL), BF16),
        compiler_params=pltpu.CompilerParams(dimension_semantics=("parallel",)),
    )(proj, q_norm_g, kn, v)


def _gate_spec(tm, branch):
    return pl.BlockSpec((tm, D_MODEL), lambda i: (i, COL_GATE + branch))


def _merge_fwd(x, proj, a_conv, o_sb, o_x, w_conv_out, w_sb_out, w_x_out, w_out, tm):
    t, d = x.shape

    def body(x_ref, g0_ref, g1_ref, g2_ref, a_ref, s_ref, xo_ref, wc_ref, ws_ref, wx_ref, wo_ref,
             x1_ref, yc_ref, ys_ref, yx_ref, mg_ref):
        merged = jnp.zeros((tm, d), F32)
        for gate_ref, b_ref, w_ref, y_ref in ((g0_ref, a_ref, wc_ref, yc_ref), (g1_ref, s_ref, ws_ref, ys_ref),
                                              (g2_ref, xo_ref, wx_ref, yx_ref)):
            yv = _dot(b_ref[...].astype(BF16), w_ref[...])
            y_ref[...] = yv.astype(BF16)
            merged = merged + _sigmoid(gate_ref[...].astype(F32)) * yv
        mb = merged.astype(BF16)
        mg_ref[...] = mb
        x1_ref[...] = x_ref[...] + _dot(mb, wo_ref[...])

    tile = pl.BlockSpec((tm, d), lambda i: (i, 0))
    wfull = pl.BlockSpec((d, d), lambda i: (0, 0))
    return pl.pallas_call(
        body, name="merge_fwd",
        grid=(t // tm,),
        in_specs=[tile] + [_gate_spec(tm, b) for b in range(N_BRANCH)] + [tile, tile, tile,
                                                                           wfull, wfull, wfull, wfull],
        out_specs=[tile] * 5,
        out_shape=[jax.ShapeDtypeStruct((t, d), F32)] + [jax.ShapeDtypeStruct((t, d), BF16)] * 4,
        compiler_params=pltpu.CompilerParams(dimension_semantics=("parallel",)),
    )(x, proj, proj, proj, a_conv, o_sb, o_x, w_conv_out, w_sb_out, w_x_out, w_out)


def _mlp_fwd(x1, g_mlp, w_up_all, w_down, target, tm):
    t, d = x1.shape
    nb, _, fw = w_up_all.shape

    def body(x1_ref, g_ref, wu_ref, wd_ref, tgt_ref, up_ref, h2_ref, dx2_ref, lsum_ref, acc_ref):
        i, j = pl.program_id(0), pl.program_id(1)

        @pl.when(j == 0)
        def _():
            xf = x1_ref[...]
            h2_ref[...] = (xf * _rstd(xf) * g_ref[...]).astype(BF16)
            acc_ref[...] = jnp.zeros_like(acc_ref)

        @pl.when((i == 0) & (j == 0))
        def _():
            lsum_ref[...] = jnp.zeros_like(lsum_ref)

        up = _dot(h2_ref[...], wu_ref[...])
        up_ref[...] = up.astype(BF16)
        act = jnp.square(jnp.maximum(up, 0.0)).astype(BF16)
        acc_ref[...] += _dot(act, wd_ref[...])

        @pl.when(j == nb - 1)
        def _():
            diff = x1_ref[...] + acc_ref[...] - tgt_ref[...]
            dx2_ref[...] = diff * (1.0 / d)
            lsum_ref[...] += jnp.sum(diff * diff, axis=0, keepdims=True)

    tile = pl.BlockSpec((tm, d), lambda i, j: (i, 0))
    row = pl.BlockSpec((1, d), lambda i, j: (0, 0))
    return pl.pallas_call(
        body, name="mlp_fwd",
        grid=(t // tm, nb),
        in_specs=[tile, row, pl.BlockSpec((None, d, fw), lambda i, j: (j, 0, 0)),
                  pl.BlockSpec((fw, d), lambda i, j: (j, 0)), tile],
        out_specs=[pl.BlockSpec((tm, fw), lambda i, j: (i, j)), tile, tile, row],
        out_shape=[jax.ShapeDtypeStruct((t, nb * fw), BF16), jax.ShapeDtypeStruct((t, d), BF16),
                   jax.ShapeDtypeStruct((t, d), F32), jax.ShapeDtypeStruct((1, d), F32)],
        scratch_shapes=[pltpu.VMEM((tm, d), F32)],
        compiler_params=pltpu.CompilerParams(dimension_semantics=("arbitrary", "arbitrary")),
    )(x1, g_mlp, w_up_all, w_down, target)


def _mlp_bwd(x1, g_mlp, w_up_all, w_down, up, dx2, tm):
    t, d = x1.shape
    nb, _, fw = w_up_all.shape

    def body(x1_ref, g_ref, wu_ref, wd_ref, up_ref, dx2_ref, dup_ref, act_ref, dx1_ref, dg_ref, acc_ref, dyb_ref):
        i, j = pl.program_id(0), pl.program_id(1)

        @pl.when(j == 0)
        def _():
            dyb_ref[...] = dx2_ref[...].astype(BF16)
            acc_ref[...] = jnp.zeros_like(acc_ref)

        @pl.when((i == 0) & (j == 0))
        def _():
            dg_ref[...] = jnp.zeros_like(dg_ref)

        r = jnp.maximum(up_ref[...].astype(F32), 0.0)
        act_ref[...] = jnp.square(r).astype(BF16)
        dup = (_dot_nt(dyb_ref[...], wd_ref[...]) * (2.0 * r)).astype(BF16)
        dup_ref[...] = dup
        acc_ref[...] += _dot_nt(dup, wu_ref[...])

        @pl.when(j == nb - 1)
        def _():
            xf = x1_ref[...]
            rs = _rstd(xf)
            xhat = xf * rs
            dh2 = acc_ref[...]
            dg_ref[...] += jnp.sum(dh2 * xhat, axis=0, keepdims=True)
            dx1_ref[...] = dx2_ref[...] + _rms_bwd(dh2, xhat, rs, g_ref[...])

    tile = pl.BlockSpec((tm, d), lambda i, j: (i, 0))
    row = pl.BlockSpec((1, d), lambda i, j: (0, 0))
    ff = pl.BlockSpec((tm, fw), lambda i, j: (i, j))
    return pl.pallas_call(
        body, name="mlp_bwd",
        grid=(t // tm, nb),
        in_specs=[tile, row, pl.BlockSpec((None, d, fw), lambda i, j: (j, 0, 0)),
                  pl.BlockSpec((fw, d), lambda i, j: (j, 0)), ff, tile],
        out_specs=[ff, ff, tile, row],
        out_shape=[jax.ShapeDtypeStruct((t, nb * fw), BF16), jax.ShapeDtypeStruct((t, nb * fw), BF16),
                   jax.ShapeDtypeStruct((t, d), F32), jax.ShapeDtypeStruct((1, d), F32)],
        scratch_shapes=[pltpu.VMEM((tm, d), F32), pltpu.VMEM((tm, d), BF16)],
        compiler_params=pltpu.CompilerParams(dimension_semantics=("arbitrary", "arbitrary")),
    )(x1, g_mlp, w_up_all, w_down, up, dx2)


def _merge_bwd(dx1, proj, y_conv, y_sb, y_x, w_conv_out, w_sb_out, w_x_out, w_out, tm):
    t, d = dx1.shape

    def body(dx1_ref, g0_ref, g1_ref, g2_ref, yc_ref, ys_ref, yx_ref, wc_ref, ws_ref, wx_ref, wo_ref,
             dgate_ref, dyc_ref, dys_ref, dyx_ref, da_ref, dos_ref, dox_ref):
        dm = _dot_nt(dx1_ref[...].astype(BF16), wo_ref[...])
        for i, (gate_ref, y_ref, w_ref, dy_ref, db_ref) in enumerate(((g0_ref, yc_ref, wc_ref, dyc_ref, da_ref),
                                                                       (g1_ref, ys_ref, ws_ref, dys_ref, dos_ref),
                                                                       (g2_ref, yx_ref, wx_ref, dyx_ref, dox_ref))):
            gt = _sigmoid(gate_ref[...].astype(F32))
            dy = (dm * gt).astype(BF16)
            dy_ref[...] = dy
            dgate_ref[:, i * d:(i + 1) * d] = (dm * y_ref[...].astype(F32) * gt * (1.0 - gt)).astype(BF16)
            db_ref[...] = _dot_nt(dy, w_ref[...]).astype(BF16)

    tile = pl.BlockSpec((tm, d), lambda i: (i, 0))
    wfull = pl.BlockSpec((d, d), lambda i: (0, 0))
    return pl.pallas_call(
        body, name="merge_bwd",
        grid=(t // tm,),
        in_specs=[tile] + [_gate_spec(tm, b) for b in range(N_BRANCH)] + [tile, tile, tile,
                                                                           wfull, wfull, wfull, wfull],
        out_specs=[pl.BlockSpec((tm, N_BRANCH * d), lambda i: (i, 0))] + [tile] * 6,
        out_shape=[jax.ShapeDtypeStruct((t, N_BRANCH * d), BF16)] + [jax.ShapeDtypeStruct((t, d), BF16)] * 6,
        compiler_params=pltpu.CompilerParams(dimension_semantics=("parallel",)),
    )(dx1, proj, proj, proj, y_conv, y_sb, y_x, w_conv_out, w_sb_out, w_x_out, w_out)


def _conv_bwd(proj, conv_w, da, cw):
    t = proj.shape[0]
    nper = D_MODEL // cw

    def body(ch_ref, cb_ref, cc_ref, w_ref, da_ref, dch_ref, dcb_ref, dcc_ref, dw_ref):
        ch, cb, cc, u, u1, u2, cv, w, row = _conv_terms(ch_ref, cb_ref, cc_ref, w_ref)
        dav = da_ref[...].astype(F32)
        dcb_ref[...] = (dav * cv).astype(BF16)
        dcv = dav * cb
        n1 = jnp.where(row < t - 1, pltpu.roll(dcv, t - 1, 0), 0.0)
        n2 = jnp.where(row < t - 2, pltpu.roll(dcv, t - 2, 0), 0.0)
        du = w[2] * dcv + w[1] * n1 + w[0] * n2
        dcc_ref[...] = (du * ch).astype(BF16)
        dch_ref[...] = (du * cc).astype(BF16)
        dw_ref[0:1, :] = jnp.sum(dcv * u2, axis=0, keepdims=True)
        dw_ref[1:2, :] = jnp.sum(dcv * u1, axis=0, keepdims=True)
        dw_ref[2:3, :] = jnp.sum(dcv * u, axis=0, keepdims=True)

    def col(piece):
        return pl.BlockSpec((t, cw), lambda j: (0, piece * nper + j))

    out_col = pl.BlockSpec((t, cw), lambda j: (0, j))
    wspec = pl.BlockSpec((3, cw), lambda j: (0, j))
    return pl.pallas_call(
        body, name="conv_bwd",
        grid=(nper,),
        in_specs=[col(COL_CH), col(COL_CB), col(COL_CC), wspec, out_col],
        out_specs=[out_col, out_col, out_col, wspec],
        out_shape=[jax.ShapeDtypeStruct((t, D_MODEL), BF16)] * 3 + [jax.ShapeDtypeStruct((3, D_MODEL), F32)],
        compiler_params=pltpu.CompilerParams(dimension_semantics=("parallel",)),
    )(proj, proj, proj, conv_w, da)


def _sb_bwd(proj, kt4, vt4, do_sb, o_sb, tq, pair_sums):
    t = proj.shape[0]
    nq = t // tq
    pairs = D_MODEL // SB_BLOCK
    scale = SB_HEAD_DIM ** -0.5
    n = len(pair_sums)

    def body(q_ref, k_ref, v_ref, kt_ref, vt_ref, do_ref, o_ref, *rest):
        dq_ref, dk_ref, dv_ref = rest[n:n + 3]
        dk_acc, dv_acc = rest[2 * n + 3:2 * n + 5]
        begin, finish = _chip_exchange_phases(rest[:n], rest[n + 3:2 * n + 3], *rest[2 * n + 5:])
        pi, qi = pl.program_id(0), pl.program_id(1)
        pl.when((pi == 0) & (qi == 0))(begin)

        @pl.when(qi == 0)
        def _():
            dk_acc[...] = jnp.zeros_like(dk_acc)
            dv_acc[...] = jnp.zeros_like(dv_acc)

        q = q_ref[...].astype(F32) * scale
        do = do_ref[...].astype(F32)
        qt, dot_ = q.T, do.T
        prod = dot_ * o_ref[...].T
        u_mat = _scan_matrix()
        qts = [_head_rows(qt, h) for h in range(2)]
        dots = [_head_rows(dot_, h) for h in range(2)]
        qms = [_head_lanes(q, h) for h in range(2)]
        doms = [_head_lanes(do, h) for h in range(2)]
        dsum = [jnp.sum(prod[h * SB_HEAD_DIM:(h + 1) * SB_HEAD_DIM, :], axis=0, keepdims=True) for h in range(2)]

        def group(g, state, past):
            ks = pl.multiple_of(g * tq, tq)
            kgrp = k_ref[pl.ds(ks, tq), :]
            vgrp = v_ref[pl.ds(ks, tq), :]
            kt = kt_ref[g]
            dk_add = jnp.zeros((tq, SB_BLOCK), F32)
            dv_add = jnp.zeros((tq, SB_BLOCK), F32)
            out = []
            for h in range(2):
                dqt, carry_l, carry_g = state[h]
                a, lb, carry_l = _sb_probs(kgrp, qts[h], u_mat, carry_l, past)
                ab = a.astype(BF16)
                gw = _dot(vgrp, dots[h]) * ab.astype(F32)
                after, carry_g = _group_suffix(u_mat, gw, carry_g)
                before = dsum[h] - (after + gw)
                sig = jnp.exp(lb)
                dz = gw * (1.0 - sig) - before * sig
                if past is not None:
                    dz = jnp.where(past, dz, 0.0)
                dzb = dz.astype(BF16)
                dqt = dqt + _dot(kt[h * SB_HEAD_DIM:(h + 1) * SB_HEAD_DIM, :], dzb)
                dk_add = dk_add + _dot(dzb, qms[h])
                dv_add = dv_add + _dot(ab, doms[h])
                out.append((dqt, carry_l, carry_g))
            dk_acc[pl.ds(ks, tq), :] += dk_add
            dv_acc[pl.ds(ks, tq), :] += dv_add
            return tuple(out)

        zero = (jnp.zeros((SB_HEAD_DIM, tq), F32), jnp.zeros((1, tq), F32), jnp.zeros((1, tq), F32))
        state = _sb_sweep(qi, tq, group, (zero, zero))
        dq_ref[...] = (jnp.concatenate([state[0][0], state[1][0]], axis=0).T * scale).astype(BF16)

        @pl.when(qi == nq - 1)
        def _():
            dk_ref[...] = dk_acc[...].astype(BF16)
            dv_ref[...] = dv_acc[...].astype(BF16)

        pl.when((pi == pairs - 1) & (qi == nq - 1))(finish)

    qblk = lambda base: pl.BlockSpec((tq, SB_BLOCK), lambda p, i: (i, base * pairs + p))
    seq = lambda base: pl.BlockSpec((t, SB_BLOCK), lambda p, i: (0, base * pairs + p))
    tr = pl.BlockSpec((None, nq, SB_BLOCK, tq), lambda p, i: (p, 0, 0, 0))
    outs = pl.pallas_call(
        body, name="sb_bwd",
        grid=(pairs, nq),
        in_specs=[qblk(COL_SQ), seq(COL_SK), seq(COL_SV), tr, tr, qblk(0), qblk(0)] + [ANY] * n,
        out_specs=[qblk(0), seq(0), seq(0)] + [ANY] * n,
        out_shape=[jax.ShapeDtypeStruct((t, D_MODEL), BF16)] * 3 + _chip_exchange_shapes(pair_sums),
        scratch_shapes=[pltpu.VMEM((t, SB_BLOCK), F32), pltpu.VMEM((t, SB_BLOCK), F32)] + _chip_exchange_sems(n),
        compiler_params=pltpu.CompilerParams(dimension_semantics=("arbitrary", "arbitrary")),
    )(proj, proj, proj, kt4, vt4, do_sb, o_sb, *pair_sums)
    return outs[0], outs[1], outs[2], outs[3:]


def _x_bwd(proj, q_norm_g, kn, v, do_x, tm):
    t = proj.shape[0]
    m = kn.shape[0]
    scale = X_HEAD_DIM ** -0.5

    def body(xq_ref, qg_ref, kn_ref, v_ref, do_ref, dxq_ref, dkn_ref, dv_ref, dqg_ref):
        @pl.when(pl.program_id(0) == 0)
        def _():
            dkn_ref[...] = jnp.zeros_like(dkn_ref)
            dv_ref[...] = jnp.zeros_like(dv_ref)
            dqg_ref[...] = jnp.zeros_like(dqg_ref)

        qg = qg_ref[...]
        for h in range(X_HEADS):
            sl, rq, qhat, qn, p = _x_head(xq_ref, qg, kn_ref, h)
            do_h = do_ref[:, sl]
            dp = _dot_nt(do_h, v_ref[:, sl])
            ds = (p * (dp - jnp.sum(dp * p, axis=-1, keepdims=True)) * scale).astype(BF16)
            dqn = _dot(ds, kn_ref[:, sl])
            dkn_ref[:, sl] += _dot_tn(ds, qn)
            dv_ref[:, sl] += _dot_tn(p.astype(BF16), do_h)
            dqg_ref[...] += jnp.sum(dqn * qhat, axis=0, keepdims=True)
            dxq_ref[:, sl] = _rms_bwd(dqn, qhat, rq, qg).astype(BF16)

    full = pl.BlockSpec((m, D_MODEL), lambda i: (0, 0))
    gain = pl.BlockSpec((1, X_HEAD_DIM), lambda i: (0, 0))
    tile = pl.BlockSpec((tm, D_MODEL), lambda i: (i, 0))
    return pl.pallas_call(
        body, name="x_bwd",
        grid=(t // tm,),
        in_specs=[pl.BlockSpec((tm, D_MODEL), lambda i: (i, COL_XQ)), gain, full, full, tile],
        out_specs=[tile, full, full, gain],
        out_shape=[jax.ShapeDtypeStruct((t, D_MODEL), BF16), jax.ShapeDtypeStruct((m, D_MODEL), F32),
                   jax.ShapeDtypeStruct((m, D_MODEL), F32), jax.ShapeDtypeStruct((1, X_HEAD_DIM), F32)],
        compiler_params=pltpu.CompilerParams(dimension_semantics=("arbitrary",)),
    )(proj, q_norm_g, kn, v, do_x)


def _mem_bwd(mem, g_mem, wkv_all, k_norm_g, dkn, dv):
    m, d = mem.shape

    def body(mem_ref, g_ref, w_ref, kg_ref, dkn_ref, dv_ref, dkv_ref, dgm_ref, dkg_ref):
        memf = mem_ref[...]
        mem_hat = memf * _rstd(memf)
        memn = (mem_hat * g_ref[...]).astype(BF16)
        kg = kg_ref[...]
        dmemn = jnp.zeros((m, d), F32)
        dkg = jnp.zeros((1, X_HEAD_DIM), F32)
        for b in range(N_DEV):
            sl = slice(b * X_HEAD_DIM, (b + 1) * X_HEAD_DIM)
            if b < X_HEADS:
                kv = _dot(memn, w_ref[b])
                rk = _rstd(kv)
                khat = kv * rk
                dkn_h = dkn_ref[:, sl]
                dkg = dkg + jnp.sum(dkn_h * khat, axis=0, keepdims=True)
                dblk = _rms_bwd(dkn_h, khat, rk, kg).astype(BF16)
            else:
                hs = slice((b - X_HEADS) * X_HEAD_DIM, (b - X_HEADS + 1) * X_HEAD_DIM)
                dblk = dv_ref[:, hs].astype(BF16)
            dkv_ref[:, sl] = dblk
            dmemn = dmemn + _dot_nt(dblk, w_ref[b])
        dgm_ref[...] = jnp.sum(dmemn * mem_hat, axis=0, keepdims=True)
        dkg_ref[...] = dkg

    return pl.pallas_call(
        body, name="mem_bwd",
        out_shape=[jax.ShapeDtypeStruct((m, 2 * d), BF16), jax.ShapeDtypeStruct((1, d), F32),
                   jax.ShapeDtypeStruct((1, X_HEAD_DIM), F32)],
    )(mem, g_mem, wkv_all, k_norm_g, dkn, dv)


def _in_proj_bwd(x, g_mix, w_in_all, dproj, dx1, tm, pair_sums):
    t, d = x.shape
    nb, _, bw = w_in_all.shape
    nt = t // tm
    n = len(pair_sums)

    def body(x_ref, g_ref, w_ref, dp_ref, dx1_ref, *rest):
        dx_ref, dg_ref = rest[n:n + 2]
        acc_ref = rest[2 * n + 2]
        begin, finish = _chip_exchange_phases(rest[:n], rest[n + 2:2 * n + 2], *rest[2 * n + 3:])
        i, j = pl.program_id(0), pl.program_id(1)
        pl.when((i == 0) & (j == 0))(begin)

        @pl.when(j == 0)
        def _():
            acc_ref[...] = jnp.zeros_like(acc_ref)

        @pl.when((i == 0) & (j == 0))
        def _():
            dg_ref[...] = jnp.zeros_like(dg_ref)

        acc_ref[...] += _dot_nt(dp_ref[...], w_ref[...])

        @pl.when(j == nb - 1)
        def _():
            xf = x_ref[...]
            rs = _rstd(xf)
            xhat = xf * rs
            dh = acc_ref[...]
            dg_ref[...] += jnp.sum(dh * xhat, axis=0, keepdims=True)
            dx_ref[...] = dx1_ref[...] + _rms_bwd(dh, xhat, rs, g_ref[...])

        pl.when((i == nt - 1) & (j == nb - 1))(finish)

    tile = pl.BlockSpec((tm, d), lambda i, j: (i, 0))
    row = pl.BlockSpec((1, d), lambda i, j: (0, 0))
    outs = pl.pallas_call(
        body, name="in_proj_bwd",
        grid=(nt, nb),
        in_specs=[tile, row, pl.BlockSpec((None, d, bw), lambda i, j: (j, 0, 0)),
                  pl.BlockSpec((tm, bw), lambda i, j: (i, j)), tile] + [ANY] * n,
        out_specs=[tile, row] + [ANY] * n,
        out_shape=[jax.ShapeDtypeStruct((t, d), F32), jax.ShapeDtypeStruct((1, d), F32)] + _chip_exchange_shapes(pair_sums),
        scratch_shapes=[pltpu.VMEM((tm, d), F32)] + _chip_exchange_sems(n),
        compiler_params=pltpu.CompilerParams(dimension_semantics=("arbitrary", "arbitrary")),
    )(x, g_mix, w_in_all, dproj, dx1, *pair_sums)
    return outs[0], outs[1], outs[2:]


def _weight_grad(a, b, bw, tmm, name):
    t, m = a.shape
    n = b.shape[1]
    tmm = min(tmm, m)

    def body(a_ref, b_ref, o_ref):
        o_ref[...] = _dot_tn(a_ref[...].astype(BF16), b_ref[...].astype(BF16)).astype(BF16)

    return pl.pallas_call(
        body, name=name,
        grid=(m // tmm, n // bw),
        in_specs=[pl.BlockSpec((t, tmm), lambda i, j: (0, i)), pl.BlockSpec((t, bw), lambda i, j: (0, j))],
        out_specs=pl.BlockSpec((None, tmm, bw), lambda i, j: (j, i, 0)),
        out_shape=jax.ShapeDtypeStruct((n // bw, m, bw), BF16),
        compiler_params=pltpu.CompilerParams(dimension_semantics=("parallel", "parallel")),
    )(a, b)


def _pair_sum(grad, recv, own_blocks, name):
    _, rows, cols = grad.shape

    def body(idx_ref, g_ref, r_ref, o_ref):
        o_ref[...] = (g_ref[...].astype(F32) + r_ref[...].astype(F32)).astype(BF16)

    return pl.pallas_call(
        body, name=name,
        grid_spec=pltpu.PrefetchScalarGridSpec(
            num_scalar_prefetch=1, grid=(4,),
            in_specs=[pl.BlockSpec((None, rows, cols), lambda r, idx: (idx[r], 0, 0)),
                      pl.BlockSpec((None, rows, cols), lambda r, idx: (r, 0, 0))],
            out_specs=pl.BlockSpec((None, rows, cols), lambda r, idx: (r, 0, 0))),
        out_shape=jax.ShapeDtypeStruct((4, rows, cols), BF16),
        compiler_params=pltpu.CompilerParams(dimension_semantics=("parallel",)),
    )(own_blocks, grad, recv)


def _adamw_math(w, g, m, v):
    m = ADAM_B1 * m + (1.0 - ADAM_B1) * g
    v = ADAM_B2 * v + (1.0 - ADAM_B2) * jnp.square(g)
    m_hat = m / (1.0 - ADAM_B1 ** ADAM_STEP)
    v_hat = v / (1.0 - ADAM_B2 ** ADAM_STEP)
    delta = -ADAM_LR * (m_hat / (jnp.sqrt(v_hat) + ADAM_EPS) + ADAM_WD * w)
    return delta, m, v


def _adamw_sharded(pair_sums, recv, w, m, v, tr, name):
    rows, cols = w.shape
    tr = min(tr, rows)

    def body(h_ref, r_ref, w_ref, m_ref, v_ref, g_out, d_out, m_out, v_out):
        g = h_ref[...].astype(F32)
        for r in range(3):
            g = g + r_ref[r].astype(F32)
        g_out[...] = g
        d_out[...], m_out[...], v_out[...] = _adamw_math(w_ref[...], g, m_ref[...], v_ref[...])

    tile = pl.BlockSpec((tr, cols), lambda i: (i, 0))
    return pl.pallas_call(
        body, name=name,
        grid=(rows // tr,),
        in_specs=[pl.BlockSpec((None, tr, cols), lambda i: (0, i, 0)),
                  pl.BlockSpec((3, tr, cols), lambda i: (0, i, 0)), tile, tile, tile],
        out_specs=[tile] * 4,
        out_shape=[jax.ShapeDtypeStruct((rows, cols), F32)] * 4,
        compiler_params=pltpu.CompilerParams(dimension_semantics=("parallel",)),
    )(pair_sums, recv, w, m, v)


SMALL_ROWS = 16


def _pack_rows(dg_mix, dg_mem, dg_mlp, dqg, dkg, dconv, lsum):
    def body(a_ref, b_ref, c_ref, q_ref, k_ref, cv_ref, l_ref, o_ref):
        o_ref[...] = jnp.zeros_like(o_ref)
        for r, ref in enumerate((a_ref, b_ref, c_ref)):
            o_ref[r:r + 1, :] = ref[...]
        o_ref[3:4, :X_HEAD_DIM] = q_ref[...]
        o_ref[4:5, :X_HEAD_DIM] = k_ref[...]
        o_ref[5:8, :] = cv_ref[...]
        o_ref[8:9, :] = l_ref[...]

    return pl.pallas_call(body, name="small_pack", out_shape=jax.ShapeDtypeStruct((SMALL_ROWS, D_MODEL), F32))(
        dg_mix, dg_mem, dg_mlp, dqg, dkg, dconv, lsum)


def _small_sum(gathered):
    def body(g_ref, o_ref):
        total = g_ref[0]
        for dev in range(1, N_DEV):
            total = total + g_ref[dev]
        o_ref[...] = jnp.zeros_like(o_ref)
        for piece in range(5):
            o_ref[piece * SMALL_TILE:piece * SMALL_TILE + 1, :] = total[piece:piece + 1]
        o_ref[5 * SMALL_TILE:5 * SMALL_TILE + 3, :] = total[5:8]
        o_ref[6 * SMALL_TILE:6 * SMALL_TILE + 1, :] = total[8:9]

    return pl.pallas_call(body, name="small_grad_sum",
                          out_shape=jax.ShapeDtypeStruct((7 * SMALL_TILE, D_MODEL), F32))(gathered)


def _adamw_small(w, g, m, v):
    def body(w_ref, g_ref, m_ref, v_ref, d_out, m_out, v_out):
        d_out[...], m_out[...], v_out[...] = _adamw_math(w_ref[...], g_ref[...], m_ref[...], v_ref[...])

    return pl.pallas_call(body, name="adamw_small", out_shape=[jax.ShapeDtypeStruct(w.shape, F32)] * 3)(w, g, m, v)


def _pad_tile(a):
    return jnp.pad(a, ((0, SMALL_TILE - a.shape[0]), (0, D_MODEL - a.shape[1])))


def _pack_small(*pieces):
    return jnp.concatenate([_pad_tile(a) for a in pieces], axis=0)


def kernel(x, mem, g_mix, g_mem, w_in, conv_w, w_conv_out, w_sb_out, q_norm_g, k_norm_g, w_mem_kv, w_x_out, w_out, g_mlp, w_up, w_down, loss_target, m_g_mix, m_g_mem, m_w_in, m_conv_w, m_w_conv_out, m_w_sb_out, m_q_norm_g, m_k_norm_g, m_w_mem_kv, m_w_x_out, m_w_out, m_g_mlp, m_w_up, m_w_down, v_g_mix, v_g_mem, v_w_in, v_conv_w, v_w_conv_out, v_w_sb_out, v_q_norm_g, v_k_norm_g, v_w_mem_kv, v_w_x_out, v_w_out, v_g_mlp, v_w_up, v_w_down):
    xpos, ypos, cpos = _mesh_pos()
    me = 4 * xpos + 2 * ypos + cpos
    x2d, mem2d, tgt2d = x[0], mem[0], loss_target[0]
    t = x2d.shape[0]
    tm = min(512, t)
    tm_s = min(256, t)

    big = {
        "w_in": (w_in[0], m_w_in[0], v_w_in[0]),
        "w_conv_out": (w_conv_out[0], m_w_conv_out[0], v_w_conv_out[0]),
        "w_sb_out": (w_sb_out[0], m_w_sb_out[0], v_w_sb_out[0]),
        "w_mem_kv": (w_mem_kv[0], m_w_mem_kv[0], v_w_mem_kv[0]),
        "w_x_out": (w_x_out[0], m_w_x_out[0], v_w_x_out[0]),
        "w_out": (w_out[0], m_w_out[0], v_w_out[0]),
        "w_up": (w_up[0], m_w_up[0], v_w_up[0]),
        "w_down": (w_down[0], m_w_down[0], v_w_down[0]),
    }
    late = [n for n in big if n != "w_in"]
    behind_in_proj = ["w_conv_out", "w_sb_out", "w_mem_kv", "w_x_out", "w_out"]
    behind_sb = ["w_up", "w_down"]
    as_bf16 = lambda group: [big[n][0].astype(BF16) for n in group]
    conv_pad = jnp.pad(conv_w[0], ((0, 8 - 3), (0, 0)))
    w_in_all, conv_all = _all_gather([big["w_in"][0].astype(BF16), conv_pad])
    conv_full = conv_all[:, :3, :].transpose(1, 0, 2).reshape(3, D_MODEL)

    proj, h, gathered_a = _in_proj(x2d, g_mix, w_in_all, tm, as_bf16(behind_in_proj))
    a_conv = _conv_fwd(proj, conv_full, 256)
    tq = min(SB_QUERY_TILE, t)
    pairs = D_MODEL // SB_BLOCK

    def groups_t(cols):
        return cols.reshape(t // tq, tq, pairs, SB_BLOCK).transpose(2, 0, 3, 1)

    kt4 = groups_t(proj[:, COL_SK * D_MODEL:(COL_SK + 1) * D_MODEL])
    vt4 = groups_t(proj[:, COL_SV * D_MODEL:(COL_SV + 1) * D_MODEL])
    o_sb, gathered_b = _sb_fwd(proj, vt4, tq, as_bf16(behind_sb))
    full = dict(zip(behind_in_proj + behind_sb, list(gathered_a) + list(gathered_b)))
    wkv_all, w_up_all = full["w_mem_kv"], full["w_up"]
    rows_full = lambda a: a.reshape(a.shape[0] * a.shape[1], a.shape[2])
    wc, ws, wx, wo, wd = (rows_full(full[n]) for n in ("w_conv_out", "w_sb_out", "w_x_out", "w_out", "w_down"))
    mem_n, kn, vmem = _mem_prep(mem2d, g_mem, wkv_all, k_norm_g)
    o_x = _x_fwd(proj, q_norm_g, kn, vmem, tm_s)
    x1, y_conv, y_sb, y_x, merged = _merge_fwd(x2d, proj, a_conv, o_sb, o_x, wc, ws, wx, wo, tm_s)
    up, h2, dx2, lsum = _mlp_fwd(x1, g_mlp, w_up_all, wd, tgt2d, tm)

    dup, act, dx1, dg_mlp = _mlp_bwd(x1, g_mlp, w_up_all, wd, up, dx2, tm)
    dgate, dy_conv, dy_sb, dy_x, da_conv, do_sb, do_x = _merge_bwd(dx1, proj, y_conv, y_sb, y_x, wc, ws, wx, wo, tm_s)
    dch, dcb, dcc, dconv = _conv_bwd(proj, conv_full, da_conv, 256)
    dxq, dkn, dvm, dqg = _x_bwd(proj, q_norm_g, kn, vmem, do_x, tm_s)
    dkv, dg_mem, dkg = _mem_bwd(mem2d, g_mem, wkv_all, k_norm_g, dkn, dvm)
    wgrads = {
        "w_conv_out": _weight_grad(a_conv, dy_conv, D_MODEL, 512, "dw_conv_out"),
        "w_sb_out": _weight_grad(o_sb, dy_sb, D_MODEL, 512, "dw_sb_out"),
        "w_mem_kv": _weight_grad(mem_n, dkv, wkv_all.shape[2], 512, "dw_mem_kv"),
        "w_x_out": _weight_grad(o_x, dy_x, D_MODEL, 512, "dw_x_out"),
        "w_out": _weight_grad(merged, dx1, D_MODEL, 512, "dw_out"),
        "w_up": _weight_grad(h2, dup, w_up_all.shape[2], 512, "dw_up"),
        "w_down": _weight_grad(act, dx2, D_MODEL, 512, "dw_down"),
    }

    own_blocks = jnp.stack([4 * (xpos ^ dx) + 2 * (ypos ^ dy) + cpos for dx in (0, 1) for dy in (0, 1)]).astype(jnp.int32)

    def pair_reduce(group):
        blocked = [wgrads[n].reshape((N_DEV,) + big[n][0].shape) for n in group]
        from_sibling = _pair_exchange(blocked, "grad_pair_exchange_" + group[0])
        return [_pair_sum(g, r, own_blocks, "pair_sum_" + n) for n, g, r in zip(group, blocked, from_sibling)]

    pair_sums = dict(zip(late, pair_reduce(late)))
    dq, dk, dv, from_chips_late = _sb_bwd(proj, kt4, vt4, do_sb, o_sb, tq, [pair_sums[n] for n in late])
    from_chips = dict(zip(late, from_chips_late))
    dproj = jnp.concatenate([dch, dcb, dcc, dq, dk, dv, dxq, dgate], axis=1)
    wgrads["w_in"] = _weight_grad(h, dproj, w_in_all.shape[2], 512, "dw_in")
    pair_sums["w_in"], = pair_reduce(["w_in"])
    grad_x, dg_mix, (from_chips["w_in"],) = _in_proj_bwd(x2d, g_mix, w_in_all, dproj, dx1, tm, [pair_sums["w_in"]])
    res = {}
    for n in big:
        w_sh, m_sh, v_sh = big[n]
        res[n] = _adamw_sharded(pair_sums[n], from_chips[n], w_sh, m_sh, v_sh, 256, "adamw_" + n)

    part = _pack_rows(dg_mix, dg_mem, dg_mlp, dqg, dkg, dconv, lsum)
    gsum = _small_sum(_small_all_gather(part))
    loss = 0.5 * jnp.sum(gsum[6 * SMALL_TILE]) / D_MODEL
    conv_cols = lax.dynamic_slice(gsum[5 * SMALL_TILE:6 * SMALL_TILE], (0, me * (D_MODEL // N_DEV)),
                                  (SMALL_TILE, D_MODEL // N_DEV))
    g_small = jnp.concatenate([gsum[:5 * SMALL_TILE], _pad_tile(conv_cols)], axis=0)
    w_small = _pack_small(g_mix, g_mem, g_mlp, q_norm_g, k_norm_g, conv_w[0])
    m_small = _pack_small(m_g_mix, m_g_mem, m_g_mlp, m_q_norm_g, m_k_norm_g, m_conv_w[0])
    v_small = _pack_small(v_g_mix, v_g_mem, v_g_mlp, v_q_norm_g, v_k_norm_g, v_conv_w[0])
    d_small, nm_small, nv_small = _adamw_small(w_small, g_small, m_small, v_small)

    def unpack(p):
        return {"g_mix": p[0:1], "g_mem": p[8:9], "g_mlp": p[16:17], "q_norm_g": p[24:25, :X_HEAD_DIM],
                "k_norm_g": p[32:33, :X_HEAD_DIM], "conv_w": p[40:43, :D_MODEL // N_DEV][None]}

    small = [unpack(p) for p in (g_small, d_small, nm_small, nv_small)]
    order = ["g_mix", "g_mem", "w_in", "conv_w", "w_conv_out", "w_sb_out", "q_norm_g", "k_norm_g", "w_mem_kv",
             "w_x_out", "w_out", "g_mlp", "w_up", "w_down"]
    outs = [loss, grad_x[None]]
    for kind in range(4):
        for n in order:
            outs.append(res[n][kind][None] if n in res else small[kind][n])
    return tuple(outs)
```

```python
import jax
import jax.numpy as jnp
from jax import lax
from jax.experimental import pallas as pl
from jax.experimental.pallas import tpu as pltpu

F32 = jnp.float32
BF16 = jnp.bfloat16
MESH = pl.DeviceIdType.MESH

EPS = 1e-6
N_DEV = 8
D_MODEL = 1024
SB_HEAD_DIM = 64
SB_BLOCK = 128
SB_QUERY_TILE = 512
X_HEADS = 4
X_HEAD_DIM = 256
N_BRANCH = 3
COL_CH, COL_CB, COL_CC, COL_SQ, COL_SK, COL_SV, COL_XQ, COL_GATE = 0, 1, 2, 3, 4, 5, 6, 7

ADAM_LR = 0.001
ADAM_B1 = 0.9
ADAM_B2 = 0.999
ADAM_EPS = 1e-08
ADAM_WD = 0.01
ADAM_STEP = 10

SMALL_TILE = 8


def _dot(a, b):
    return jnp.dot(a, b, preferred_element_type=F32)


def _dot_nt(a, b):
    return lax.dot_general(a, b, (((1,), (1,)), ((), ())), preferred_element_type=F32)


def _dot_tn(a, b):
    return lax.dot_general(a, b, (((0,), (0,)), ((), ())), preferred_element_type=F32)


def _rstd(xf):
    return lax.rsqrt(jnp.mean(xf * xf, axis=-1, keepdims=True) + EPS)


def _sigmoid(z):
    return 1.0 / (1.0 + jnp.exp(-z))


def _log_sigmoid(z):
    return jnp.minimum(z, 0.0) - jnp.log(1.0 + jnp.exp(-jnp.abs(z)))


def _rms_bwd(dy, xhat, r, g):
    dxhat = dy * g
    return r * (dxhat - xhat * jnp.mean(dxhat * xhat, axis=-1, keepdims=True))


def _mesh_pos():
    return lax.axis_index("x"), lax.axis_index("y"), lax.axis_index("c")


ANY = pl.BlockSpec(memory_space=pl.ANY)


def _all_gather(shards):
    n = len(shards)

    def body(*refs):
        begin, relay, finish = _gather_phases(refs[:n], refs[n:2 * n], *refs[2 * n:])
        begin()
        relay()
        finish()

    return pl.pallas_call(
        body, name="weights_all_gather",
        out_shape=_gather_shapes(shards),
        in_specs=[ANY] * n, out_specs=[ANY] * n,
        scratch_shapes=_gather_sems(n),
    )(*shards)


def _gather_shapes(shards):
    return [jax.ShapeDtypeStruct((N_DEV,) + s.shape, s.dtype) for s in shards]


def _gather_sems(n):
    return [pltpu.SemaphoreType.DMA((n, 7)), pltpu.SemaphoreType.DMA((n, 7)), pltpu.SemaphoreType.DMA((n,))]


def _gather_phases(ins, outs, send_sems, recv_sems, local_sems):
    n = len(ins)
    x, y, c = _mesh_pos()
    me, sibling = (x, y, c), (x, y, 1 - c)
    chips = [(1 - x, y), (x, 1 - y), (1 - x, 1 - y)]

    def blk(a, px, py, pc):
        return outs[a].at[4 * px + 2 * py + pc]

    def copy(a, k, block, to, src=None):
        return pltpu.make_async_remote_copy(
            src_ref=blk(a, *block) if src is None else src, dst_ref=blk(a, *block),
            send_sem=send_sems.at[a, k], recv_sem=recv_sems.at[a, k], device_id=to, device_id_type=MESH)

    def local(a):
        return pltpu.make_async_copy(ins[a], blk(a, *me), local_sems.at[a])

    def own(a):
        return [copy(a, 0, me, sibling, src=ins[a])] + [copy(a, 1 + j, me, (*chip, c), src=ins[a])
                                                        for j, chip in enumerate(chips)]

    def begin():
        for a in range(n):
            local(a).start()
        for a in range(n):
            for cp in own(a):
                cp.start()

    def relay():
        for j, chip in enumerate(chips):
            for a in range(n):
                copy(a, 1 + j, (*chip, c), me).wait_recv()
                copy(a, 4 + j, (*chip, c), sibling).start()

    def finish():
        for a in range(n):
            copy(a, 0, sibling, me).wait_recv()
            for j, chip in enumerate(chips):
                copy(a, 4 + j, (*chip, 1 - c), me).wait_recv()
        for a in range(n):
            for cp in own(a):
                cp.wait_send()
            for j, chip in enumerate(chips):
                copy(a, 4 + j, (*chip, c), sibling).wait_send()
            local(a).wait()

    return begin, relay, finish


def _pair_exchange(grads, name):
    n = len(grads)

    def body(*refs):
        ins, outs = refs[:n], refs[n:2 * n]
        send_sems, recv_sems = refs[2 * n:]
        x, y, c = _mesh_pos()
        xs, ys = (x, 1 - x), (y, 1 - y)
        copies = []
        for a in range(n):
            for r in range(4):
                dx, dy = divmod(r, 2)
                copies.append(pltpu.make_async_remote_copy(
                    src_ref=ins[a].at[4 * xs[dx] + 2 * ys[dy] + (1 - c)], dst_ref=outs[a].at[r],
                    send_sem=send_sems.at[a, r], recv_sem=recv_sems.at[a, r],
                    device_id=(x, y, 1 - c), device_id_type=MESH))
        for cp in copies:
            cp.start()
        for cp in copies:
            cp.wait()

    return pl.pallas_call(
        body, name=name,
        out_shape=[jax.ShapeDtypeStruct((4,) + g.shape[1:], g.dtype) for g in grads],
        in_specs=[ANY] * n, out_specs=[ANY] * n,
        scratch_shapes=[pltpu.SemaphoreType.DMA((n, 4)), pltpu.SemaphoreType.DMA((n, 4))],
    )(*grads)


def _chip_exchange_shapes(sums):
    return [jax.ShapeDtypeStruct((3,) + s.shape[1:], s.dtype) for s in sums]


def _chip_exchange_sems(n):
    return [pltpu.SemaphoreType.DMA((n, 3)), pltpu.SemaphoreType.DMA((n, 3))]


def _chip_exchange_phases(ins, outs, send_sems, recv_sems):
    x, y, c = _mesh_pos()
    xs, ys = (x, 1 - x), (y, 1 - y)

    def copies():
        out = []
        for a in range(len(ins)):
            for r in range(1, 4):
                dx, dy = divmod(r, 2)
                out.append(pltpu.make_async_remote_copy(
                    src_ref=ins[a].at[r], dst_ref=outs[a].at[r - 1],
                    send_sem=send_sems.at[a, r - 1], recv_sem=recv_sems.at[a, r - 1],
                    device_id=(xs[dx], ys[dy], c), device_id_type=MESH))
        return out

    def begin():
        for cp in copies():
            cp.start()

    def finish():
        for cp in copies():
            cp.wait()

    return begin, finish


def _small_all_gather(part):
    rows, cols = part.shape

    def body(in_ref, out_ref, send_sems, recv_sems):
        x, y, c = _mesh_pos()
        xs, ys, cs = (x, 1 - x), (y, 1 - y), (c, 1 - c)
        out_ref[4 * x + 2 * y + c] = in_ref[...]
        copies = []
        for k in range(1, N_DEV):
            dx, dy, dc = k // 4, (k // 2) % 2, k % 2
            copies.append((
                pltpu.make_async_remote_copy(
                    src_ref=in_ref, dst_ref=out_ref.at[4 * x + 2 * y + c],
                    send_sem=send_sems.at[k - 1], recv_sem=recv_sems.at[k - 1],
                    device_id=(xs[dx], ys[dy], cs[dc]), device_id_type=MESH),
                pltpu.make_async_remote_copy(
                    src_ref=in_ref, dst_ref=out_ref.at[4 * xs[dx] + 2 * ys[dy] + cs[dc]],
                    send_sem=send_sems.at[k - 1], recv_sem=recv_sems.at[k - 1],
                    device_id=(xs[dx], ys[dy], cs[dc]), device_id_type=MESH)))
        for send, _ in copies:
            send.start()
        for send, recv in copies:
            recv.wait_recv()
            send.wait_send()

    return pl.pallas_call(
        body, name="small_all_gather",
        out_shape=jax.ShapeDtypeStruct((N_DEV, rows, cols), part.dtype),
        in_specs=[pl.BlockSpec(memory_space=pltpu.VMEM)],
        out_specs=pl.BlockSpec(memory_space=pltpu.VMEM),
        scratch_shapes=[pltpu.SemaphoreType.DMA((N_DEV - 1,)), pltpu.SemaphoreType.DMA((N_DEV - 1,))],
    )(part)


def _in_proj(x, g_mix, w_in_all, tm, shards):
    t, d = x.shape
    nb, _, bw = w_in_all.shape
    nt = t // tm
    n = len(shards)

    def body(x_ref, g_ref, w_ref, *rest):
        proj_ref, h_ref = rest[n:n + 2]
        begin, relay, finish = _gather_phases(rest[:n], rest[n + 2:2 * n + 2], *rest[2 * n + 2:])
        i, j = pl.program_id(0), pl.program_id(1)
        pl.when((i == 0) & (j == 0))(begin)
        pl.when((i == nt - 1) & (j == 0))(relay)

        @pl.when(j == 0)
        def _():
            xf = x_ref[...]
            h_ref[...] = (xf * _rstd(xf) * g_ref[...]).astype(BF16)

        proj_ref[...] = _dot(h_ref[...], w_ref[...]).astype(BF16)
        pl.when((i == nt - 1) & (j == nb - 1))(finish)

    outs = pl.pallas_call(
        body, name="in_proj",
        grid=(nt, nb),
        in_specs=[pl.BlockSpec((tm, d), lambda i, j: (i, 0)),
                  pl.BlockSpec((1, d), lambda i, j: (0, 0)),
                  pl.BlockSpec((None, d, bw), lambda i, j: (j, 0, 0))] + [ANY] * n,
        out_specs=[pl.BlockSpec((tm, bw), lambda i, j: (i, j)),
                   pl.BlockSpec((tm, d), lambda i, j: (i, 0))] + [ANY] * n,
        out_shape=[jax.ShapeDtypeStruct((t, nb * bw), BF16), jax.ShapeDtypeStruct((t, d), BF16)] + _gather_shapes(shards),
        scratch_shapes=_gather_sems(n),
        compiler_params=pltpu.CompilerParams(dimension_semantics=("arbitrary", "arbitrary")),
    )(x, g_mix, w_in_all, *shards)
    return outs[0], outs[1], outs[2:]


def _conv_terms(ch_ref, cb_ref, cc_ref, w_ref):
    ch, cb, cc = ch_ref[...].astype(F32), cb_ref[...].astype(F32), cc_ref[...].astype(F32)
    u = cc * ch
    row = lax.broadcasted_iota(jnp.int32, u.shape, 0)
    u1 = jnp.where(row >= 1, pltpu.roll(u, 1, 0), 0.0)
    u2 = jnp.where(row >= 2, pltpu.roll(u, 2, 0), 0.0)
    w = (w_ref[0:1, :], w_ref[1:2, :], w_ref[2:3, :])
    cv = w[2] * u + w[1] * u1 + w[0] * u2
    return ch, cb, cc, u, u1, u2, cv, w, row


def _conv_fwd(proj, conv_w, cw):
    t = proj.shape[0]
    nper = D_MODEL // cw

    def body(ch_ref, cb_ref, cc_ref, w_ref, a_ref):
        _, cb, _, _, _, _, cv, _, _ = _conv_terms(ch_ref, cb_ref, cc_ref, w_ref)
        a_ref[...] = (cb * cv).astype(BF16)

    def col(piece):
        return pl.BlockSpec((t, cw), lambda j: (0, piece * nper + j))

    return pl.pallas_call(
        body, name="conv_fwd",
        grid=(nper,),
        in_specs=[col(COL_CH), col(COL_CB), col(COL_CC), pl.BlockSpec((3, cw), lambda j: (0, j))],
        out_specs=pl.BlockSpec((t, cw), lambda j: (0, j)),
        out_shape=jax.ShapeDtypeStruct((t, D_MODEL), BF16),
        compiler_params=pltpu.CompilerParams(dimension_semantics=("parallel",)),
    )(proj, proj, proj, conv_w)


def _scan_matrix():
    s = lax.broadcasted_iota(jnp.int32, (SB_BLOCK, SB_BLOCK), 0)
    j = lax.broadcasted_iota(jnp.int32, (SB_BLOCK, SB_BLOCK), 1)
    return jnp.where(j > s, 1.0, 0.0).astype(BF16)


def _suffix_sum(u_mat, xv):
    hi = xv.astype(BF16)
    lo = (xv - hi.astype(F32)).astype(BF16)
    return _dot(u_mat, hi) + _dot(u_mat, lo)


def _head_rows(vt, h):
    row = lax.broadcasted_iota(jnp.int32, vt.shape, 0)
    return jnp.where((row >= h * SB_HEAD_DIM) & (row < (h + 1) * SB_HEAD_DIM), vt, 0.0).astype(BF16)


def _head_lanes(v, h):
    lane = lax.broadcasted_iota(jnp.int32, v.shape, 1)
    return jnp.where((lane >= h * SB_HEAD_DIM) & (lane < (h + 1) * SB_HEAD_DIM), v, 0.0).astype(BF16)


def _group_suffix(u_mat, xv, carry):
    nblk = xv.shape[0] // SB_BLOCK
    parts = [None] * nblk
    for j in reversed(range(nblk)):
        xj = xv[j * SB_BLOCK:(j + 1) * SB_BLOCK]
        parts[j] = _suffix_sum(u_mat, xj) + carry
        carry = carry + jnp.sum(xj, axis=0, keepdims=True)
    return jnp.concatenate(parts, axis=0), carry


def _sb_probs(kgrp, qt_h, u_mat, carry, past):
    z = _dot(kgrp, qt_h)
    lb = _log_sigmoid(z)
    l1 = lb - z
    if past is not None:
        l1 = jnp.where(past, l1, 0.0)
    between, carry = _group_suffix(u_mat, l1, carry)
    a = jnp.exp(lb + between)
    if past is not None:
        a = jnp.where(past, a, 0.0)
    return a, lb, carry


def _sb_sweep(qi, tq, group, state):
    past = lax.broadcasted_iota(jnp.int32, (tq, tq), 0) < lax.broadcasted_iota(jnp.int32, (tq, tq), 1)
    state = group(qi, state, past)
    return lax.fori_loop(0, qi, lambda i, st: group(qi - 1 - i, st, None), state)


def _sb_fwd(proj, vt4, tq, shards):
    t = proj.shape[0]
    pairs = D_MODEL // SB_BLOCK
    nq = t // tq
    n = len(shards)

    def body(q_ref, k_ref, vt_ref, *rest):
        o_ref = rest[n]
        begin, relay, finish = _gather_phases(rest[:n], rest[n + 1:2 * n + 1], *rest[2 * n + 1:])
        pi, qi = pl.program_id(0), pl.program_id(1)
        pl.when((pi == 0) & (qi == 0))(begin)
        pl.when((pi == (3 * pairs) // 4) & (qi == 0))(relay)
        qt = q_ref[...].astype(F32).T * (SB_HEAD_DIM ** -0.5)
        qts = [_head_rows(qt, h) for h in range(2)]
        u_mat = _scan_matrix()

        def group(g, state, past):
            kgrp = k_ref[pl.ds(pl.multiple_of(g * tq, tq), tq), :]
            vt = vt_ref[g]
            out = []
            for h in range(2):
                acc, carry = state[h]
                a, _, carry = _sb_probs(kgrp, qts[h], u_mat, carry, past)
                acc = acc + _dot(vt[h * SB_HEAD_DIM:(h + 1) * SB_HEAD_DIM, :], a.astype(BF16))
                out.append((acc, carry))
            return tuple(out)

        zero = (jnp.zeros((SB_HEAD_DIM, tq), F32), jnp.zeros((1, tq), F32))
        state = _sb_sweep(qi, tq, group, (zero, zero))
        o_ref[...] = jnp.concatenate([state[0][0], state[1][0]], axis=0).T
        pl.when((pi == pairs - 1) & (qi == nq - 1))(finish)

    outs = pl.pallas_call(
        body, name="sb_fwd",
        grid=(pairs, nq),
        in_specs=[pl.BlockSpec((tq, SB_BLOCK), lambda p, i: (i, COL_SQ * pairs + p)),
                  pl.BlockSpec((t, SB_BLOCK), lambda p, i: (0, COL_SK * pairs + p)),
                  pl.BlockSpec((None, nq, SB_BLOCK, tq), lambda p, i: (p, 0, 0, 0))] + [ANY] * n,
        out_specs=[pl.BlockSpec((tq, SB_BLOCK), lambda p, i: (i, p))] + [ANY] * n,
        out_shape=[jax.ShapeDtypeStruct((t, D_MODEL), F32)] + _gather_shapes(shards),
        scratch_shapes=_gather_sems(n),
        compiler_params=pltpu.CompilerParams(dimension_semantics=("arbitrary", "arbitrary")),
    )(proj, proj, vt4, *shards)
    return outs[0], outs[1:]


def _mem_prep(mem, g_mem, wkv_all, k_norm_g):
    m, d = mem.shape

    def body(mem_ref, g_ref, w_ref, kg_ref, memn_ref, kn_ref, v_ref):
        memf = mem_ref[...]
        memn = (memf * _rstd(memf) * g_ref[...]).astype(BF16)
        memn_ref[...] = memn
        for b in range(N_DEV):
            kv = _dot(memn, w_ref[b])
            if b < X_HEADS:
                kn_ref[:, b * X_HEAD_DIM:(b + 1) * X_HEAD_DIM] = (kv * _rstd(kv) * kg_ref[...]).astype(BF16)
            else:
                h = b - X_HEADS
                v_ref[:, h * X_HEAD_DIM:(h + 1) * X_HEAD_DIM] = kv.astype(BF16)

    return pl.pallas_call(
        body, name="mem_prep",
        out_shape=[jax.ShapeDtypeStruct((m, d), BF16)] * 3,
    )(mem, g_mem, wkv_all, k_norm_g)


def _x_head(xq_ref, qg, kn_ref, h):
    sl = slice(h * X_HEAD_DIM, (h + 1) * X_HEAD_DIM)
    q = xq_ref[:, sl].astype(F32)
    rq = _rstd(q)
    qhat = q * rq
    qn = (qhat * qg).astype(BF16)
    s = _dot_nt(qn, kn_ref[:, sl]) * (X_HEAD_DIM ** -0.5)
    e = jnp.exp(s - jnp.max(s, axis=-1, keepdims=True))
    p = e / jnp.sum(e, axis=-1, keepdims=True)
    return sl, rq, qhat, qn, p


def _x_fwd(proj, q_norm_g, kn, v, tm):
    t = proj.shape[0]
    m = kn.shape[0]

    def body(xq_ref, qg_ref, kn_ref, v_ref, o_ref):
        for h in range(X_HEADS):
            sl, _, _, _, p = _x_head(xq_ref, qg_ref[...], kn_ref, h)
            o_ref[:, sl] = _dot(p.astype(BF16), v_ref[:, sl]).astype(BF16)

    return pl.pallas_call(
        body, name="x_fwd",
        grid=(t // tm,),
        in_specs=[pl.BlockSpec((tm, D_MODEL), lambda i: (i, COL_XQ)),
                  pl.BlockSpec((1, X_HEAD_DIM), lambda i: (0, 0)),
                  pl.BlockSpec((m, D_MODEL), lambda i: (0, 0)),
                  pl.BlockSpec((m, D_MODEL), lambda i: (0, 0))],
        out_specs=pl.BlockSpec((tm, D_MODEL), lambda i: (i, 0)),
        out_shape=jax.ShapeDtypeStruct((t, D_MODEL), BF16),
        compiler_params=pltpu.CompilerParams(dimension_semantics=("parallel",)),
    )(proj, q_norm_g, kn, v)


def _gate_spec(tm, branch):
    return pl.BlockSpec((tm, D_MODEL), lambda i: (i, COL_GATE + branch))


def _merge_fwd(x, proj, a_conv, o_sb, o_x, w_conv_out, w_sb_out, w_x_out, w_out, tm):
    t, d = x.shape

    def body(x_ref, g0_ref, g1_ref, g2_ref, a_ref, s_ref, xo_ref, wc_ref, ws_ref, wx_ref, wo_ref,
             x1_ref, yc_ref, ys_ref, yx_ref, mg_ref):
        merged = jnp.zeros((tm, d), F32)
        for gate_ref, b_ref, w_ref, y_ref in ((g0_ref, a_ref, wc_ref, yc_ref), (g1_ref, s_ref, ws_ref, ys_ref),
                                              (g2_ref, xo_ref, wx_ref, yx_ref)):
            yv = _dot(b_ref[...].astype(BF16), w_ref[...])
            y_ref[...] = yv.astype(BF16)
            merged = merged + _sigmoid(gate_ref[...].astype(F32)) * yv
        mb = merged.astype(BF16)
        mg_ref[...] = mb
        x1_ref[...] = x_ref[...] + _dot(mb, wo_ref[...])

    tile = pl.BlockSpec((tm, d), lambda i: (i, 0))
    wfull = pl.BlockSpec((d, d), lambda i: (0, 0))
    return pl.pallas_call(
        body, name="merge_fwd",
        grid=(t // tm,),
        in_specs=[tile] + [_gate_spec(tm, b) for b in range(N_BRANCH)] + [tile, tile, tile,
                                                                           wfull, wfull, wfull, wfull],
        out_specs=[tile] * 5,
        out_shape=[jax.ShapeDtypeStruct((t, d), F32)] + [jax.ShapeDtypeStruct((t, d), BF16)] * 4,
        compiler_params=pltpu.CompilerParams(dimension_semantics=("parallel",)),
    )(x, proj, proj, proj, a_conv, o_sb, o_x, w_conv_out, w_sb_out, w_x_out, w_out)


def _mlp_fwd(x1, g_mlp, w_up_all, w_down, target, tm):
    t, d = x1.shape
    nb, _, fw = w_up_all.shape

    def body(x1_ref, g_ref, wu_ref, wd_ref, tgt_ref, up_ref, h2_ref, dx2_ref, lsum_ref, acc_ref):
        i, j = pl.program_id(0), pl.program_id(1)

        @pl.when(j == 0)
        def _():
            xf = x1_ref[...]
            h2_ref[...] = (xf * _rstd(xf) * g_ref[...]).astype(BF16)
            acc_ref[...] = jnp.zeros_like(acc_ref)

        @pl.when((i == 0) & (j == 0))
        def _():
            lsum_ref[...] = jnp.zeros_like(lsum_ref)

        up = _dot(h2_ref[...], wu_ref[...])
        up_ref[...] = up.astype(BF16)
        act = jnp.square(jnp.maximum(up, 0.0)).astype(BF16)
        acc_ref[...] += _dot(act, wd_ref[...])

        @pl.when(j == nb - 1)
        def _():
            diff = x1_ref[...] + acc_ref[...] - tgt_ref[...]
            dx2_ref[...] = diff * (1.0 / d)
            lsum_ref[...] += jnp.sum(diff * diff, axis=0, keepdims=True)

    tile = pl.BlockSpec((tm, d), lambda i, j: (i, 0))
    row = pl.BlockSpec((1, d), lambda i, j: (0, 0))
    return pl.pallas_call(
        body, name="mlp_fwd",
        grid=(t // tm, nb),
        in_specs=[tile, row, pl.BlockSpec((None, d, fw), lambda i, j: (j, 0, 0)),
                  pl.BlockSpec((fw, d), lambda i, j: (j, 0)), tile],
        out_specs=[pl.BlockSpec((tm, fw), lambda i, j: (i, j)), tile, tile, row],
        out_shape=[jax.ShapeDtypeStruct((t, nb * fw), BF16), jax.ShapeDtypeStruct((t, d), BF16),
                   jax.ShapeDtypeStruct((t, d), F32), jax.ShapeDtypeStruct((1, d), F32)],
        scratch_shapes=[pltpu.VMEM((tm, d), F32)],
        compiler_params=pltpu.CompilerParams(dimension_semantics=("arbitrary", "arbitrary")),
    )(x1, g_mlp, w_up_all, w_down, target)


def _mlp_bwd(x1, g_mlp, w_up_all, w_down, up, dx2, tm):
    t, d = x1.shape
    nb, _, fw = w_up_all.shape

    def body(x1_ref, g_ref, wu_ref, wd_ref, up_ref, dx2_ref, dup_ref, act_ref, dx1_ref, dg_ref, acc_ref, dyb_ref):
        i, j = pl.program_id(0), pl.program_id(1)

        @pl.when(j == 0)
        def _():
            dyb_ref[...] = dx2_ref[...].astype(BF16)
            acc_ref[...] = jnp.zeros_like(acc_ref)

        @pl.when((i == 0) & (j == 0))
        def _():
            dg_ref[...] = jnp.zeros_like(dg_ref)

        r = jnp.maximum(up_ref[...].astype(F32), 0.0)
        act_ref[...] = jnp.square(r).astype(BF16)
        dup = (_dot_nt(dyb_ref[...], wd_ref[...]) * (2.0 * r)).astype(BF16)
        dup_ref[...] = dup
        acc_ref[...] += _dot_nt(dup, wu_ref[...])

        @pl.when(j == nb - 1)
        def _():
            xf = x1_ref[...]
            rs = _rstd(xf)
            xhat = xf * rs
            dh2 = acc_ref[...]
            dg_ref[...] += jnp.sum(dh2 * xhat, axis=0, keepdims=True)
            dx1_ref[...] = dx2_ref[...] + _rms_bwd(dh2, xhat, rs, g_ref[...])

    tile = pl.BlockSpec((tm, d), lambda i, j: (i, 0))
    row = pl.BlockSpec((1, d), lambda i, j: (0, 0))
    ff = pl.BlockSpec((tm, fw), lambda i, j: (i, j))
    return pl.pallas_call(
        body, name="mlp_bwd",
        grid=(t // tm, nb),
        in_specs=[tile, row, pl.BlockSpec((None, d, fw), lambda i, j: (j, 0, 0)),
                  pl.BlockSpec((fw, d), lambda i, j: (j, 0)), ff, tile],
        out_specs=[ff, ff, tile, row],
        out_shape=[jax.ShapeDtypeStruct((t, nb * fw), BF16), jax.ShapeDtypeStruct((t, nb * fw), BF16),
                   jax.ShapeDtypeStruct((t, d), F32), jax.ShapeDtypeStruct((1, d), F32)],
        scratch_shapes=[pltpu.VMEM((tm, d), F32), pltpu.VMEM((tm, d), BF16)],
        compiler_params=pltpu.CompilerParams(dimension_semantics=("arbitrary", "arbitrary")),
    )(x1, g_mlp, w_up_all, w_down, up, dx2)


def _merge_bwd(dx1, proj, y_conv, y_sb, y_x, w_conv_out, w_sb_out, w_x_out, w_out, tm):
    t, d = dx1.shape

    def body(dx1_ref, g0_ref, g1_ref, g2_ref, yc_ref, ys_ref, yx_ref, wc_ref, ws_ref, wx_ref, wo_ref,
             dgate_ref, dyc_ref, dys_ref, dyx_ref, da_ref, dos_ref, dox_ref):
        dm = _dot_nt(dx1_ref[...].astype(BF16), wo_ref[...])
        for i, (gate_ref, y_ref, w_ref, dy_ref, db_ref) in enumerate(((g0_ref, yc_ref, wc_ref, dyc_ref, da_ref),
                                                                       (g1_ref, ys_ref, ws_ref, dys_ref, dos_ref),
                                                                       (g2_ref, yx_ref, wx_ref, dyx_ref, dox_ref))):
            gt = _sigmoid(gate_ref[...].astype(F32))
            dy = (dm * gt).astype(BF16)
            dy_ref[...] = dy
            dgate_ref[:, i * d:(i + 1) * d] = (dm * y_ref[...].astype(F32) * gt * (1.0 - gt)).astype(BF16)
            db_ref[...] = _dot_nt(dy, w_ref[...]).astype(BF16)

    tile = pl.BlockSpec((tm, d), lambda i: (i, 0))
    wfull = pl.BlockSpec((d, d), lambda i: (0, 0))
    return pl.pallas_call(
        body, name="merge_bwd",
        grid=(t // tm,),
        in_specs=[tile] + [_gate_spec(tm, b) for b in range(N_BRANCH)] + [tile, tile, tile,
                                                                           wfull, wfull, wfull, wfull],
        out_specs=[pl.BlockSpec((tm, N_BRANCH * d), lambda i: (i, 0))] + [tile] * 6,
        out_shape=[jax.ShapeDtypeStruct((t, N_BRANCH * d), BF16)] + [jax.ShapeDtypeStruct((t, d), BF16)] * 6,
        compiler_params=pltpu.CompilerParams(dimension_semantics=("parallel",)),
    )(dx1, proj, proj, proj, y_conv, y_sb, y_x, w_conv_out, w_sb_out, w_x_out, w_out)


def _conv_bwd(proj, conv_w, da, cw):
    t = proj.shape[0]
    nper = D_MODEL // cw

    def body(ch_ref, cb_ref, cc_ref, w_ref, da_ref, dch_ref, dcb_ref, dcc_ref, dw_ref):
        ch, cb, cc, u, u1, u2, cv, w, row = _conv_terms(ch_ref, cb_ref, cc_ref, w_ref)
        dav = da_ref[...].astype(F32)
        dcb_ref[...] = (dav * cv).astype(BF16)
        dcv = dav * cb
        n1 = jnp.where(row < t - 1, pltpu.roll(dcv, t - 1, 0), 0.0)
        n2 = jnp.where(row < t - 2, pltpu.roll(dcv, t - 2, 0), 0.0)
        du = w[2] * dcv + w[1] * n1 + w[0] * n2
        dcc_ref[...] = (du * ch).astype(BF16)
        dch_ref[...] = (du * cc).astype(BF16)
        dw_ref[0:1, :] = jnp.sum(dcv * u2, axis=0, keepdims=True)
        dw_ref[1:2, :] = jnp.sum(dcv * u1, axis=0, keepdims=True)
        dw_ref[2:3, :] = jnp.sum(dcv * u, axis=0, keepdims=True)

    def col(piece):
        return pl.BlockSpec((t, cw), lambda j: (0, piece * nper + j))

    out_col = pl.BlockSpec((t, cw), lambda j: (0, j))
    wspec = pl.BlockSpec((3, cw), lambda j: (0, j))
    return pl.pallas_call(
        body, name="conv_bwd",
        grid=(nper,),
        in_specs=[col(COL_CH), col(COL_CB), col(COL_CC), wspec, out_col],
        out_specs=[out_col, out_col, out_col, wspec],
        out_shape=[jax.ShapeDtypeStruct((t, D_MODEL), BF16)] * 3 + [jax.ShapeDtypeStruct((3, D_MODEL), F32)],
        compiler_params=pltpu.CompilerParams(dimension_semantics=("parallel",)),
    )(proj, proj, proj, conv_w, da)


def _sb_bwd(proj, kt4, vt4, do_sb, o_sb, tq, pair_sums):
    t = proj.shape[0]
    nq = t // tq
    pairs = D_MODEL // SB_BLOCK
    scale = SB_HEAD_DIM ** -0.5
    n = len(pair_sums)

    def body(q_ref, k_ref, v_ref, kt_ref, vt_ref, do_ref, o_ref, *rest):
        dq_ref, dk_ref, dv_ref = rest[n:n + 3]
        dk_acc, dv_acc = rest[2 * n + 3:2 * n + 5]
        begin, finish = _chip_exchange_phases(rest[:n], rest[n + 3:2 * n + 3], *rest[2 * n + 5:])
        pi, qi = pl.program_id(0), pl.program_id(1)
        pl.when((pi == 0) & (qi == 0))(begin)

        @pl.when(qi == 0)
        def _():
            dk_acc[...] = jnp.zeros_like(dk_acc)
            dv_acc[...] = jnp.zeros_like(dv_acc)

        q = q_ref[...].astype(F32) * scale
        do = do_ref[...].astype(F32)
        qt, dot_ = q.T, do.T
        prod = dot_ * o_ref[...].T
        u_mat = _scan_matrix()
        qts = [_head_rows(qt, h) for h in range(2)]
        dots = [_head_rows(dot_, h) for h in range(2)]
        qms = [_head_lanes(q, h) for h in range(2)]
        doms = [_head_lanes(do, h) for h in range(2)]
        dsum = [jnp.sum(prod[h * SB_HEAD_DIM:(h + 1) * SB_HEAD_DIM, :], axis=0, keepdims=True) for h in range(2)]

        def group(g, state, past):
            ks = pl.multiple_of(g * tq, tq)
            kgrp = k_ref[pl.ds(ks, tq), :]
            vgrp = v_ref[pl.ds(ks, tq), :]
            kt = kt_ref[g]
            dk_add = jnp.zeros((tq, SB_BLOCK), F32)
            dv_add = jnp.zeros((tq, SB_BLOCK), F32)
            out = []
            for h in range(2):
                dqt, carry_l, carry_g = state[h]
                a, lb, carry_l = _sb_probs(kgrp, qts[h], u_mat, carry_l, past)
                ab = a.astype(BF16)
                gw = _dot(vgrp, dots[h]) * ab.astype(F32)
                after, carry_g = _group_suffix(u_mat, gw, carry_g)
                before = dsum[h] - (after + gw)
                sig = jnp.exp(lb)
                dz = gw * (1.0 - sig) - before * sig
                if past is not None:
                    dz = jnp.where(past, dz, 0.0)
                dzb = dz.astype(BF16)
                dqt = dqt + _dot(kt[h * SB_HEAD_DIM:(h + 1) * SB_HEAD_DIM, :], dzb)
                dk_add = dk_add + _dot(dzb, qms[h])
                dv_add = dv_add + _dot(ab, doms[h])
                out.append((dqt, carry_l, carry_g))
            dk_acc[pl.ds(ks, tq), :] += dk_add
            dv_acc[pl.ds(ks, tq), :] += dv_add
            return tuple(out)

        zero = (jnp.zeros((SB_HEAD_DIM, tq), F32), jnp.zeros((1, tq), F32), jnp.zeros((1, tq), F32))
        state = _sb_sweep(qi, tq, group, (zero, zero))
        dq_ref[...] = (jnp.concatenate([state[0][0], state[1][0]], axis=0).T * scale).astype(BF16)

        @pl.when(qi == nq - 1)
        def _():
            dk_ref[...] = dk_acc[...].astype(BF16)
            dv_ref[...] = dv_acc[...].astype(BF16)

        pl.when((pi == pairs - 1) & (qi == nq - 1))(finish)

    qblk = lambda base: pl.BlockSpec((tq, SB_BLOCK), lambda p, i: (i, base * pairs + p))
    seq = lambda base: pl.BlockSpec((t, SB_BLOCK), lambda p, i: (0, base * pairs + p))
    tr = pl.BlockSpec((None, nq, SB_BLOCK, tq), lambda p, i: (p, 0, 0, 0))
    outs = pl.pallas_call(
        body, name="sb_bwd",
        grid=(pairs, nq),
        in_specs=[qblk(COL_SQ), seq(COL_SK), seq(COL_SV), tr, tr, qblk(0), qblk(0)] + [ANY] * n,
        out_specs=[qblk(0), seq(0), seq(0)] + [ANY] * n,
        out_shape=[jax.ShapeDtypeStruct((t, D_MODEL), BF16)] * 3 + _chip_exchange_shapes(pair_sums),
        scratch_shapes=[pltpu.VMEM((t, SB_BLOCK), F32), pltpu.VMEM((t, SB_BLOCK), F32)] + _chip_exchange_sems(n),
        compiler_params=pltpu.CompilerParams(dimension_semantics=("arbitrary", "arbitrary")),
    )(proj, proj, proj, kt4, vt4, do_sb, o_sb, *pair_sums)
    return outs[0], outs[1], outs[2], outs[3:]


def _x_bwd(proj, q_norm_g, kn, v, do_x, tm):
    t = proj.shape[0]
    m = kn.shape[0]
    scale = X_HEAD_DIM ** -0.5

    def body(xq_ref, qg_ref, kn_ref, v_ref, do_ref, dxq_ref, dkn_ref, dv_ref, dqg_ref):
        @pl.when(pl.program_id(0) == 0)
        def _():
            dkn_ref[...] = jnp.zeros_like(dkn_ref)
            dv_ref[...] = jnp.zeros_like(dv_ref)
            dqg_ref[...] = jnp.zeros_like(dqg_ref)

        qg = qg_ref[...]
        for h in range(X_HEADS):
            sl, rq, qhat, qn, p = _x_head(xq_ref, qg, kn_ref, h)
            do_h = do_ref[:, sl]
            dp = _dot_nt(do_h, v_ref[:, sl])
            ds = (p * (dp - jnp.sum(dp * p, axis=-1, keepdims=True)) * scale).astype(BF16)
            dqn = _dot(ds, kn_ref[:, sl])
            dkn_ref[:, sl] += _dot_tn(ds, qn)
            dv_ref[:, sl] += _dot_tn(p.astype(BF16), do_h)
            dqg_ref[...] += jnp.sum(dqn * qhat, axis=0, keepdims=True)
            dxq_ref[:, sl] = _rms_bwd(dqn, qhat, rq, qg).astype(BF16)

    full = pl.BlockSpec((m, D_MODEL), lambda i: (0, 0))
    gain = pl.BlockSpec((1, X_HEAD_DIM), lambda i: (0, 0))
    tile = pl.BlockSpec((tm, D_MODEL), lambda i: (i, 0))
    return pl.pallas_call(
        body, name="x_bwd",
        grid=(t // tm,),
        in_specs=[pl.BlockSpec((tm, D_MODEL), lambda i: (i, COL_XQ)), gain, full, full, tile],
        out_specs=[tile, full, full, gain],
        out_shape=[jax.ShapeDtypeStruct((t, D_MODEL), BF16), jax.ShapeDtypeStruct((m, D_MODEL), F32),
                   jax.ShapeDtypeStruct((m, D_MODEL), F32), jax.ShapeDtypeStruct((1, X_HEAD_DIM), F32)],
        compiler_params=pltpu.CompilerParams(dimension_semantics=("arbitrary",)),
    )(proj, q_norm_g, kn, v, do_x)


def _mem_bwd(mem, g_mem, wkv_all, k_norm_g, dkn, dv):
    m, d = mem.shape

    def body(mem_ref, g_ref, w_ref, kg_ref, dkn_ref, dv_ref, dkv_ref, dgm_ref, dkg_ref):
        memf = mem_ref[...]
        mem_hat = memf * _rstd(memf)
        memn = (mem_hat * g_ref[...]).astype(BF16)
        kg = kg_ref[...]
        dmemn = jnp.zeros((m, d), F32)
        dkg = jnp.zeros((1, X_HEAD_DIM), F32)
        for b in range(N_DEV):
            sl = slice(b * X_HEAD_DIM, (b + 1) * X_HEAD_DIM)
            if b < X_HEADS:
                kv = _dot(memn, w_ref[b])
                rk = _rstd(kv)
                khat = kv * rk
                dkn_h = dkn_ref[:, sl]
                dkg = dkg + jnp.sum(dkn_h * khat, axis=0, keepdims=True)
                dblk = _rms_bwd(dkn_h, khat, rk, kg).astype(BF16)
            else:
                hs = slice((b - X_HEADS) * X_HEAD_DIM, (b - X_HEADS + 1) * X_HEAD_DIM)
                dblk = dv_ref[:, hs].astype(BF16)
            dkv_ref[:, sl] = dblk
            dmemn = dmemn + _dot_nt(dblk, w_ref[b])
        dgm_ref[...] = jnp.sum(dmemn * mem_hat, axis=0, keepdims=True)
        dkg_ref[...] = dkg

    return pl.pallas_call(
        body, name="mem_bwd",
        out_shape=[jax.ShapeDtypeStruct((m, 2 * d), BF16), jax.ShapeDtypeStruct((1, d), F32),
                   jax.ShapeDtypeStruct((1, X_HEAD_DIM), F32)],
    )(mem, g_mem, wkv_all, k_norm_g, dkn, dv)


def _in_proj_bwd(x, g_mix, w_in_all, dproj, dx1, tm, pair_sums):
    t, d = x.shape
    nb, _, bw = w_in_all.shape
    nt = t // tm
    n = len(pair_sums)

    def body(x_ref, g_ref, w_ref, dp_ref, dx1_ref, *rest):
        dx_ref, dg_ref = rest[n:n + 2]
        acc_ref = rest[2 * n + 2]
        begin, finish = _chip_exchange_phases(rest[:n], rest[n + 2:2 * n + 2], *rest[2 * n + 3:])
        i, j = pl.program_id(0), pl.program_id(1)
        pl.when((i == 0) & (j == 0))(begin)

        @pl.when(j == 0)
        def _():
            acc_ref[...] = jnp.zeros_like(acc_ref)

        @pl.when((i == 0) & (j == 0))
        def _():
            dg_ref[...] = jnp.zeros_like(dg_ref)

        acc_ref[...] += _dot_nt(dp_ref[...], w_ref[...])

        @pl.when(j == nb - 1)
        def _():
            xf = x_ref[...]
            rs = _rstd(xf)
            xhat = xf * rs
            dh = acc_ref[...]
            dg_ref[...] += jnp.sum(dh * xhat, axis=0, keepdims=True)
            dx_ref[...] = dx1_ref[...] + _rms_bwd(dh, xhat, rs, g_ref[...])

        pl.when((i == nt - 1) & (j == nb - 1))(finish)

    tile = pl.BlockSpec((tm, d), lambda i, j: (i, 0))
    row = pl.BlockSpec((1, d), lambda i, j: (0, 0))
    outs = pl.pallas_call(
        body, name="in_proj_bwd",
        grid=(nt, nb),
        in_specs=[tile, row, pl.BlockSpec((None, d, bw), lambda i, j: (j, 0, 0)),
                  pl.BlockSpec((tm, bw), lambda i, j: (i, j)), tile] + [ANY] * n,
        out_specs=[tile, row] + [ANY] * n,
        out_shape=[jax.ShapeDtypeStruct((t, d), F32), jax.ShapeDtypeStruct((1, d), F32)] + _chip_exchange_shapes(pair_sums),
        scratch_shapes=[pltpu.VMEM((tm, d), F32)] + _chip_exchange_sems(n),
        compiler_params=pltpu.CompilerParams(dimension_semantics=("arbitrary", "arbitrary")),
    )(x, g_mix, w_in_all, dproj, dx1, *pair_sums)
    return outs[0], outs[1], outs[2:]


def _weight_grad(a, b, bw, tmm, name):
    t, m = a.shape
    n = b.shape[1]
    tmm = min(tmm, m)

    def body(a_ref, b_ref, o_ref):
        o_ref[...] = _dot_tn(a_ref[...].astype(BF16), b_ref[...].astype(BF16)).astype(BF16)

    return pl.pallas_call(
        body, name=name,
        grid=(m // tmm, n // bw),
        in_specs=[pl.BlockSpec((t, tmm), lambda i, j: (0, i)), pl.BlockSpec((t, bw), lambda i, j: (0, j))],
        out_specs=pl.BlockSpec((None, tmm, bw), lambda i, j: (j, i, 0)),
        out_shape=jax.ShapeDtypeStruct((n // bw, m, bw), BF16),
        compiler_params=pltpu.CompilerParams(dimension_semantics=("parallel", "parallel")),
    )(a, b)


def _pair_sum(grad, recv, own_blocks, name):
    _, rows, cols = grad.shape

    def body(idx_ref, g_ref, r_ref, o_ref):
        o_ref[...] = (g_ref[...].astype(F32) + r_ref[...].astype(F32)).astype(BF16)

    return pl.pallas_call(
        body, name=name,
        grid_spec=pltpu.PrefetchScalarGridSpec(
            num_scalar_prefetch=1, grid=(4,),
            in_specs=[pl.BlockSpec((None, rows, cols), lambda r, idx: (idx[r], 0, 0)),
                      pl.BlockSpec((None, rows, cols), lambda r, idx: (r, 0, 0))],
            out_specs=pl.BlockSpec((None, rows, cols), lambda r, idx: (r, 0, 0))),
        out_shape=jax.ShapeDtypeStruct((4, rows, cols), BF16),
        compiler_params=pltpu.CompilerParams(dimension_semantics=("parallel",)),
    )(own_blocks, grad, recv)


def _adamw_math(w, g, m, v):
    m = ADAM_B1 * m + (1.0 - ADAM_B1) * g
    v = ADAM_B2 * v + (1.0 - ADAM_B2) * jnp.square(g)
    m_hat = m / (1.0 - ADAM_B1 ** ADAM_STEP)
    v_hat = v / (1.0 - ADAM_B2 ** ADAM_STEP)
    delta = -ADAM_LR * (m_hat / (jnp.sqrt(v_hat) + ADAM_EPS) + ADAM_WD * w)
    return delta, m, v


def _adamw_sharded(pair_sums, recv, w, m, v, tr, name):
    rows, cols = w.shape
    tr = min(tr, rows)

    def body(h_ref, r_ref, w_ref, m_ref, v_ref, g_out, d_out, m_out, v_out):
        g = h_ref[...].astype(F32)
        for r in range(3):
            g = g + r_ref[r].astype(F32)
        g_out[...] = g
        d_out[...], m_out[...], v_out[...] = _adamw_math(w_ref[...], g, m_ref[...], v_ref[...])

    tile = pl.BlockSpec((tr, cols), lambda i: (i, 0))
    return pl.pallas_call(
        body, name=name,
        grid=(rows // tr,),
        in_specs=[pl.BlockSpec((None, tr, cols), lambda i: (0, i, 0)),
                  pl.BlockSpec((3, tr, cols), lambda i: (0, i, 0)), tile, tile, tile],
        out_specs=[tile] * 4,
        out_shape=[jax.ShapeDtypeStruct((rows, cols), F32)] * 4,
        compiler_params=pltpu.CompilerParams(dimension_semantics=("parallel",)),
    )(pair_sums, recv, w, m, v)


SMALL_ROWS = 16


def _pack_rows(dg_mix, dg_mem, dg_mlp, dqg, dkg, dconv, lsum):
    def body(a_ref, b_ref, c_ref, q_ref, k_ref, cv_ref, l_ref, o_ref):
        o_ref[...] = jnp.zeros_like(o_ref)
        for r, ref in enumerate((a_ref, b_ref, c_ref)):
            o_ref[r:r + 1, :] = ref[...]
        o_ref[3:4, :X_HEAD_DIM] = q_ref[...]
        o_ref[4:5, :X_HEAD_DIM] = k_ref[...]
        o_ref[5:8, :] = cv_ref[...]
        o_ref[8:9, :] = l_ref[...]

    return pl.pallas_call(body, name="small_pack", out_shape=jax.ShapeDtypeStruct((SMALL_ROWS, D_MODEL), F32))(
        dg_mix, dg_mem, dg_mlp, dqg, dkg, dconv, lsum)


def _small_sum(gathered):
    def body(g_ref, o_ref):
        total = g_ref[0]
        for dev in range(1, N_DEV):
            total = total + g_ref[dev]
        o_ref[...] = jnp.zeros_like(o_ref)
        for piece in range(5):
            o_ref[piece * SMALL_TILE:piece * SMALL_TILE + 1, :] = total[piece:piece + 1]
        o_ref[5 * SMALL_TILE:5 * SMALL_TILE + 3, :] = total[5:8]
        o_ref[6 * SMALL_TILE:6 * SMALL_TILE + 1, :] = total[8:9]

    return pl.pallas_call(body, name="small_grad_sum",
                          out_shape=jax.ShapeDtypeStruct((7 * SMALL_TILE, D_MODEL), F32))(gathered)


def _adamw_small(w, g, m, v):
    def body(w_ref, g_ref, m_ref, v_ref, d_out, m_out, v_out):
        d_out[...], m_out[...], v_out[...] = _adamw_math(w_ref[...], g_ref[...], m_ref[...], v_ref[...])

    return pl.pallas_call(body, name="adamw_small", out_shape=[jax.ShapeDtypeStruct(w.shape, F32)] * 3)(w, g, m, v)


def _pad_tile(a):
    return jnp.pad(a, ((0, SMALL_TILE - a.shape[0]), (0, D_MODEL - a.shape[1])))


def _pack_small(*pieces):
    return jnp.concatenate([_pad_tile(a) for a in pieces], axis=0)


def kernel(x, mem, g_mix, g_mem, w_in, conv_w, w_conv_out, w_sb_out, q_norm_g, k_norm_g, w_mem_kv, w_x_out, w_out, g_mlp, w_up, w_down, loss_target, m_g_mix, m_g_mem, m_w_in, m_conv_w, m_w_conv_out, m_w_sb_out, m_q_norm_g, m_k_norm_g, m_w_mem_kv, m_w_x_out, m_w_out, m_g_mlp, m_w_up, m_w_down, v_g_mix, v_g_mem, v_w_in, v_conv_w, v_w_conv_out, v_w_sb_out, v_q_norm_g, v_k_norm_g, v_w_mem_kv, v_w_x_out, v_w_out, v_g_mlp, v_w_up, v_w_down):
    xpos, ypos, cpos = _mesh_pos()
    me = 4 * xpos + 2 * ypos + cpos
    x2d, mem2d, tgt2d = x[0], mem[0], loss_target[0]
    t = x2d.shape[0]
    tm = min(512, t)
    tm_s = min(256, t)

    big = {
        "w_in": (w_in[0], m_w_in[0], v_w_in[0]),
        "w_conv_out": (w_conv_out[0], m_w_conv_out[0], v_w_conv_out[0]),
        "w_sb_out": (w_sb_out[0], m_w_sb_out[0], v_w_sb_out[0]),
        "w_mem_kv": (w_mem_kv[0], m_w_mem_kv[0], v_w_mem_kv[0]),
        "w_x_out": (w_x_out[0], m_w_x_out[0], v_w_x_out[0]),
        "w_out": (w_out[0], m_w_out[0], v_w_out[0]),
        "w_up": (w_up[0], m_w_up[0], v_w_up[0]),
        "w_down": (w_down[0], m_w_down[0], v_w_down[0]),
    }
    late = [n for n in big if n != "w_in"]
    behind_in_proj = ["w_mem_kv", "w_conv_out", "w_sb_out"]
    behind_sb = ["w_x_out", "w_out", "w_up", "w_down"]
    as_bf16 = lambda group: [big[n][0].astype(BF16) for n in group]
    conv_pad = jnp.pad(conv_w[0], ((0, 8 - 3), (0, 0)))
    w_in_all, conv_all = _all_gather([big["w_in"][0].astype(BF16), conv_pad])
    conv_full = conv_all[:, :3, :].transpose(1, 0, 2).reshape(3, D_MODEL)

    proj, h, gathered_a = _in_proj(x2d, g_mix, w_in_all, tm, as_bf16(behind_in_proj))
    a_conv = _conv_fwd(proj, conv_full, 256)
    tq = min(SB_QUERY_TILE, t)
    pairs = D_MODEL // SB_BLOCK

    def groups_t(cols):
        return cols.reshape(t // tq, tq, pairs, SB_BLOCK).transpose(2, 0, 3, 1)

    kt4 = groups_t(proj[:, COL_SK * D_MODEL:(COL_SK + 1) * D_MODEL])
    vt4 = groups_t(proj[:, COL_SV * D_MODEL:(COL_SV + 1) * D_MODEL])
    o_sb, gathered_b = _sb_fwd(proj, vt4, tq, as_bf16(behind_sb))
    full = dict(zip(behind_in_proj + behind_sb, list(gathered_a) + list(gathered_b)))
    wkv_all, w_up_all = full["w_mem_kv"], full["w_up"]
    rows_full = lambda a: a.reshape(a.shape[0] * a.shape[1], a.shape[2])
    wc, ws, wx, wo, wd = (rows_full(full[n]) for n in ("w_conv_out", "w_sb_out", "w_x_out", "w_out", "w_down"))
    mem_n, kn, vmem = _mem_prep(mem2d, g_mem, wkv_all, k_norm_g)
    o_x = _x_fwd(proj, q_norm_g, kn, vmem, tm_s)
    x1, y_conv, y_sb, y_x, merged = _merge_fwd(x2d, proj, a_conv, o_sb, o_x, wc, ws, wx, wo, tm_s)
    up, h2, dx2, lsum = _mlp_fwd(x1, g_mlp, w_up_all, wd, tgt2d, tm)

    dup, act, dx1, dg_mlp = _mlp_bwd(x1, g_mlp, w_up_all, wd, up, dx2, tm)
    dgate, dy_conv, dy_sb, dy_x, da_conv, do_sb, do_x = _merge_bwd(dx1, proj, y_conv, y_sb, y_x, wc, ws, wx, wo, tm_s)
    dch, dcb, dcc, dconv = _conv_bwd(proj, conv_full, da_conv, 256)
    dxq, dkn, dvm, dqg = _x_bwd(proj, q_norm_g, kn, vmem, do_x, tm_s)
    dkv, dg_mem, dkg = _mem_bwd(mem2d, g_mem, wkv_all, k_norm_g, dkn, dvm)
    wgrads = {
        "w_conv_out": _weight_grad(a_conv, dy_conv, D_MODEL, 512, "dw_conv_out"),
        "w_sb_out": _weight_grad(o_sb, dy_sb, D_MODEL, 512, "dw_sb_out"),
        "w_mem_kv": _weight_grad(mem_n, dkv, wkv_all.shape[2], 512, "dw_mem_kv"),
        "w_x_out": _weight_grad(o_x, dy_x, D_MODEL, 512, "dw_x_out"),
        "w_out": _weight_grad(merged, dx1, D_MODEL, 512, "dw_out"),
        "w_up": _weight_grad(h2, dup, w_up_all.shape[2], 512, "dw_up"),
        "w_down": _weight_grad(act, dx2, D_MODEL, 512, "dw_down"),
    }

    own_blocks = jnp.stack([4 * (xpos ^ dx) + 2 * (ypos ^ dy) + cpos for dx in (0, 1) for dy in (0, 1)]).astype(jnp.int32)

    def pair_reduce(group):
        blocked = [wgrads[n].reshape((N_DEV,) + big[n][0].shape) for n in group]
        from_sibling = _pair_exchange(blocked, "grad_pair_exchange_" + group[0])
        return [_pair_sum(g, r, own_blocks, "pair_sum_" + n) for n, g, r in zip(group, blocked, from_sibling)]

    pair_sums = dict(zip(late, pair_reduce(late)))
    dq, dk, dv, from_chips_late = _sb_bwd(proj, kt4, vt4, do_sb, o_sb, tq, [pair_sums[n] for n in late])
    from_chips = dict(zip(late, from_chips_late))
    dproj = jnp.concatenate([dch, dcb, dcc, dq, dk, dv, dxq, dgate], axis=1)
    wgrads["w_in"] = _weight_grad(h, dproj, w_in_all.shape[2], 512, "dw_in")
    pair_sums["w_in"], = pair_reduce(["w_in"])
    grad_x, dg_mix, (from_chips["w_in"],) = _in_proj_bwd(x2d, g_mix, w_in_all, dproj, dx1, tm, [pair_sums["w_in"]])
    res = {}
    for n in big:
        w_sh, m_sh, v_sh = big[n]
        res[n] = _adamw_sharded(pair_sums[n], from_chips[n], w_sh, m_sh, v_sh, 256, "adamw_" + n)

    part = _pack_rows(dg_mix, dg_mem, dg_mlp, dqg, dkg, dconv, lsum)
    gsum = _small_sum(_small_all_gather(part))
    loss = 0.5 * jnp.sum(gsum[6 * SMALL_TILE]) / D_MODEL
    conv_cols = lax.dynamic_slice(gsum[5 * SMALL_TILE:6 * SMALL_TILE], (0, me * (D_MODEL // N_DEV)),
                                  (SMALL_TILE, D_MODEL // N_DEV))
    g_small = jnp.concatenate([gsum[:5 * SMALL_TILE], _pad_tile(conv_cols)], axis=0)
    w_small = _pack_small(g_mix, g_mem, g_mlp, q_norm_g, k_norm_g, conv_w[0])
    m_small = _pack_small(m_g_mix, m_g_mem, m_g_mlp, m_q_norm_g, m_k_norm_g, m_conv_w[0])
    v_small = _pack_small(v_g_mix, v_g_mem, v_g_mlp, v_q_norm_g, v_k_norm_g, v_conv_w[0])
    d_small, nm_small, nv_small = _adamw_small(w_small, g_small, m_small, v_small)

    def unpack(p):
        return {"g_mix": p[0:1], "g_mem": p[8:9], "g_mlp": p[16:17], "q_norm_g": p[24:25, :X_HEAD_DIM],
                "k_norm_g": p[32:33, :X_HEAD_DIM], "conv_w": p[40:43, :D_MODEL // N_DEV][None]}

    small = [unpack(p) for p in (g_small, d_small, nm_small, nv_small)]
    order = ["g_mix", "g_mem", "w_in", "conv_w", "w_conv_out", "w_sb_out", "q_norm_g", "k_norm_g", "w_mem_kv",
             "w_x_out", "w_out", "g_mlp", "w_up", "w_down"]
    outs = [loss, grad_x[None]]
    for kind in range(4):
        for n in order:
            outs.append(res[n][kind][None] if n in res else small[kind][n])
    return tuple(outs)
```

```python
import jax
import jax.numpy as jnp
from jax import lax
from jax.experimental import pallas as pl
from jax.experimental.pallas import tpu as pltpu

F32 = jnp.float32
BF16 = jnp.bfloat16
MESH = pl.DeviceIdType.MESH

EPS = 1e-6
N_DEV = 8
D_MODEL = 1024
SB_HEAD_DIM = 64
SB_BLOCK = 128
SB_QUERY_TILE = 512
X_HEADS = 4
X_HEAD_DIM = 256
N_BRANCH = 3
COL_CH, COL_CB, COL_CC, COL_SQ, COL_SK, COL_SV, COL_XQ, COL_GATE = 0, 1, 2, 3, 4, 5, 6, 7

ADAM_LR = 0.001
ADAM_B1 = 0.9
ADAM_B2 = 0.999
ADAM_EPS = 1e-08
ADAM_WD = 0.01
ADAM_STEP = 10

SMALL_TILE = 8


def _dot(a, b):
    return jnp.dot(a, b, preferred_element_type=F32)


def _dot_nt(a, b):
    return lax.dot_general(a, b, (((1,), (1,)), ((), ())), preferred_element_type=F32)


def _dot_tn(a, b):
    return lax.dot_general(a, b, (((0,), (0,)), ((), ())), preferred_element_type=F32)


def _rstd(xf):
    return lax.rsqrt(jnp.mean(xf * xf, axis=-1, keepdims=True) + EPS)


def _sigmoid(z):
    return 1.0 / (1.0 + jnp.exp(-z))


def _log_sigmoid(z):
    return jnp.minimum(z, 0.0) - jnp.log(1.0 + jnp.exp(-jnp.abs(z)))


def _rms_bwd(dy, xhat, r, g):
    dxhat = dy * g
    return r * (dxhat - xhat * jnp.mean(dxhat * xhat, axis=-1, keepdims=True))


def _mesh_pos():
    return lax.axis_index("x"), lax.axis_index("y"), lax.axis_index("c")


ANY = pl.BlockSpec(memory_space=pl.ANY)


def _all_gather(shards):
    n = len(shards)

    def body(*refs):
        begin, relay, finish = _gather_phases(refs[:n], refs[n:2 * n], *refs[2 * n:])
        begin()
        relay()
        finish()

    return pl.pallas_call(
        body, name="weights_all_gather",
        out_shape=_gather_shapes(shards),
        in_specs=[ANY] * n, out_specs=[ANY] * n,
        scratch_shapes=_gather_sems(n),
    )(*shards)


def _gather_shapes(shards):
    return [jax.ShapeDtypeStruct((N_DEV,) + s.shape, s.dtype) for s in shards]


def _gather_sems(n):
    return [pltpu.SemaphoreType.DMA((n, 7)), pltpu.SemaphoreType.DMA((n, 7)), pltpu.SemaphoreType.DMA((n,))]


def _gather_phases(ins, outs, send_sems, recv_sems, local_sems):
    n = len(ins)
    x, y, c = _mesh_pos()
    me, sibling = (x, y, c), (x, y, 1 - c)
    chips = [(1 - x, y), (x, 1 - y), (1 - x, 1 - y)]

    def blk(a, px, py, pc):
        return outs[a].at[4 * px + 2 * py + pc]

    def copy(a, k, block, to, src=None):
        return pltpu.make_async_remote_copy(
            src_ref=blk(a, *block) if src is None else src, dst_ref=blk(a, *block),
            send_sem=send_sems.at[a, k], recv_sem=recv_sems.at[a, k], device_id=to, device_id_type=MESH)

    def local(a):
        return pltpu.make_async_copy(ins[a], blk(a, *me), local_sems.at[a])

    def own(a):
        return [copy(a, 0, me, sibling, src=ins[a])] + [copy(a, 1 + j, me, (*chip, c), src=ins[a])
                                                        for j, chip in enumerate(chips)]

    def begin():
        for a in range(n):
            local(a).start()
        for a in range(n):
            for cp in own(a):
                cp.start()

    def relay():
        for j, chip in enumerate(chips):
            for a in range(n):
                copy(a, 1 + j, (*chip, c), me).wait_recv()
                copy(a, 4 + j, (*chip, c), sibling).start()

    def finish():
        for a in range(n):
            copy(a, 0, sibling, me).wait_recv()
            for j, chip in enumerate(chips):
                copy(a, 4 + j, (*chip, 1 - c), me).wait_recv()
        for a in range(n):
            for cp in own(a):
                cp.wait_send()
            for j, chip in enumerate(chips):
                copy(a, 4 + j, (*chip, c), sibling).wait_send()
            local(a).wait()

    return begin, relay, finish


def _pair_exchange(grads, name):
    n = len(grads)

    def body(*refs):
        ins, outs = refs[:n], refs[n:2 * n]
        send_sems, recv_sems = refs[2 * n:]
        x, y, c = _mesh_pos()
        xs, ys = (x, 1 - x), (y, 1 - y)
        copies = []
        for a in range(n):
            for r in range(4):
                dx, dy = divmod(r, 2)
                copies.append(pltpu.make_async_remote_copy(
                    src_ref=ins[a].at[4 * xs[dx] + 2 * ys[dy] + (1 - c)], dst_ref=outs[a].at[r],
                    send_sem=send_sems.at[a, r], recv_sem=recv_sems.at[a, r],
                    device_id=(x, y, 1 - c), device_id_type=MESH))
        for cp in copies:
            cp.start()
        for cp in copies:
            cp.wait()

    return pl.pallas_call(
        body, name=name,
        out_shape=[jax.ShapeDtypeStruct((4,) + g.shape[1:], g.dtype) for g in grads],
        in_specs=[ANY] * n, out_specs=[ANY] * n,
        scratch_shapes=[pltpu.SemaphoreType.DMA((n, 4)), pltpu.SemaphoreType.DMA((n, 4))],
    )(*grads)


def _chip_exchange_shapes(sums):
    return [jax.ShapeDtypeStruct((3,) + s.shape[1:], s.dtype) for s in sums]


def _chip_exchange_sems(n):
    return [pltpu.SemaphoreType.DMA((n, 3)), pltpu.SemaphoreType.DMA((n, 3))]


def _chip_exchange_phases(ins, outs, send_sems, recv_sems):
    x, y, c = _mesh_pos()
    xs, ys = (x, 1 - x), (y, 1 - y)

    def copies():
        out = []
        for a in range(len(ins)):
            for r in range(1, 4):
                dx, dy = divmod(r, 2)
                out.append(pltpu.make_async_remote_copy(
                    src_ref=ins[a].at[r], dst_ref=outs[a].at[r - 1],
                    send_sem=send_sems.at[a, r - 1], recv_sem=recv_sems.at[a, r - 1],
                    device_id=(xs[dx], ys[dy], c), device_id_type=MESH))
        return out

    def begin():
        for cp in copies():
            cp.start()

    def finish():
        for cp in copies():
            cp.wait()

    return begin, finish


def _small_all_gather(part):
    rows, cols = part.shape

    def body(in_ref, out_ref, send_sems, recv_sems):
        x, y, c = _mesh_pos()
        xs, ys, cs = (x, 1 - x), (y, 1 - y), (c, 1 - c)
        out_ref[4 * x + 2 * y + c] = in_ref[...]
        copies = []
        for k in range(1, N_DEV):
            dx, dy, dc = k // 4, (k // 2) % 2, k % 2
            copies.append((
                pltpu.make_async_remote_copy(
                    src_ref=in_ref, dst_ref=out_ref.at[4 * x + 2 * y + c],
                    send_sem=send_sems.at[k - 1], recv_sem=recv_sems.at[k - 1],
                    device_id=(xs[dx], ys[dy], cs[dc]), device_id_type=MESH),
                pltpu.make_async_remote_copy(
                    src_ref=in_ref, dst_ref=out_ref.at[4 * xs[dx] + 2 * ys[dy] + cs[dc]],
                    send_sem=send_sems.at[k - 1], recv_sem=recv_sems.at[k - 1],
                    device_id=(xs[dx], ys[dy], cs[dc]), device_id_type=MESH)))
        for send, _ in copies:
            send.start()
        for send, recv in copies:
            recv.wait_recv()
            send.wait_send()

    return pl.pallas_call(
        body, name="small_all_gather",
        out_shape=jax.ShapeDtypeStruct((N_DEV, rows, cols), part.dtype),
        in_specs=[pl.BlockSpec(memory_space=pltpu.VMEM)],
        out_specs=pl.BlockSpec(memory_space=pltpu.VMEM),
        scratch_shapes=[pltpu.SemaphoreType.DMA((N_DEV - 1,)), pltpu.SemaphoreType.DMA((N_DEV - 1,))],
    )(part)


def _in_proj(x, g_mix, w_in_all, tm, shards):
    t, d = x.shape
    nb, _, bw = w_in_all.shape
    nt = t // tm
    n = len(shards)

    def body(x_ref, g_ref, w_ref, *rest):
        proj_ref, h_ref = rest[n:n + 2]
        begin, relay, finish = _gather_phases(rest[:n], rest[n + 2:2 * n + 2], *rest[2 * n + 2:])
        i, j = pl.program_id(0), pl.program_id(1)
        pl.when((i == 0) & (j == 0))(begin)
        pl.when((i == nt - 1) & (j == 0))(relay)

        @pl.when(j == 0)
        def _():
            xf = x_ref[...]
            h_ref[...] = (xf * _rstd(xf) * g_ref[...]).astype(BF16)

        proj_ref[...] = _dot(h_ref[...], w_ref[...]).astype(BF16)
        pl.when((i == nt - 1) & (j == nb - 1))(finish)

    outs = pl.pallas_call(
        body, name="in_proj",
        grid=(nt, nb),
        in_specs=[pl.BlockSpec((tm, d), lambda i, j: (i, 0)),
                  pl.BlockSpec((1, d), lambda i, j: (0, 0)),
                  pl.BlockSpec((None, d, bw), lambda i, j: (j, 0, 0))] + [ANY] * n,
        out_specs=[pl.BlockSpec((tm, bw), lambda i, j: (i, j)),
                   pl.BlockSpec((tm, d), lambda i, j: (i, 0))] + [ANY] * n,
        out_shape=[jax.ShapeDtypeStruct((t, nb * bw), BF16), jax.ShapeDtypeStruct((t, d), BF16)] + _gather_shapes(shards),
        scratch_shapes=_gather_sems(n),
        compiler_params=pltpu.CompilerParams(dimension_semantics=("arbitrary", "arbitrary")),
    )(x, g_mix, w_in_all, *shards)
    return outs[0], outs[1], outs[2:]


def _conv_terms(ch_ref, cb_ref, cc_ref, w_ref):
    ch, cb, cc = ch_ref[...].astype(F32), cb_ref[...].astype(F32), cc_ref[...].astype(F32)
    u = cc * ch
    row = lax.broadcasted_iota(jnp.int32, u.shape, 0)
    u1 = jnp.where(row >= 1, pltpu.roll(u, 1, 0), 0.0)
    u2 = jnp.where(row >= 2, pltpu.roll(u, 2, 0), 0.0)
    w = (w_ref[0:1, :], w_ref[1:2, :], w_ref[2:3, :])
    cv = w[2] * u + w[1] * u1 + w[0] * u2
    return ch, cb, cc, u, u1, u2, cv, w, row


def _conv_fwd(proj, conv_w, cw):
    t = proj.shape[0]
    nper = D_MODEL // cw

    def body(ch_ref, cb_ref, cc_ref, w_ref, a_ref):
        _, cb, _, _, _, _, cv, _, _ = _conv_terms(ch_ref, cb_ref, cc_ref, w_ref)
        a_ref[...] = (cb * cv).astype(BF16)

    def col(piece):
        return pl.BlockSpec((t, cw), lambda j: (0, piece * nper + j))

    return pl.pallas_call(
        body, name="conv_fwd",
        grid=(nper,),
        in_specs=[col(COL_CH), col(COL_CB), col(COL_CC), pl.BlockSpec((3, cw), lambda j: (0, j))],
        out_specs=pl.BlockSpec((t, cw), lambda j: (0, j)),
        out_shape=jax.ShapeDtypeStruct((t, D_MODEL), BF16),
        compiler_params=pltpu.CompilerParams(dimension_semantics=("parallel",)),
    )(proj, proj, proj, conv_w)


def _scan_matrix():
    s = lax.broadcasted_iota(jnp.int32, (SB_BLOCK, SB_BLOCK), 0)
    j = lax.broadcasted_iota(jnp.int32, (SB_BLOCK, SB_BLOCK), 1)
    return jnp.where(j > s, 1.0, 0.0).astype(BF16)


def _suffix_sum(u_mat, xv):
    hi = xv.astype(BF16)
    lo = (xv - hi.astype(F32)).astype(BF16)
    return _dot(u_mat, hi) + _dot(u_mat, lo)


def _head_rows(vt, h):
    row = lax.broadcasted_iota(jnp.int32, vt.shape, 0)
    return jnp.where((row >= h * SB_HEAD_DIM) & (row < (h + 1) * SB_HEAD_DIM), vt, 0.0).astype(BF16)


def _head_lanes(v, h):
    lane = lax.broadcasted_iota(jnp.int32, v.shape, 1)
    return jnp.where((lane >= h * SB_HEAD_DIM) & (lane < (h + 1) * SB_HEAD_DIM), v, 0.0).astype(BF16)


def _group_suffix(u_mat, xv, carry):
    nblk = xv.shape[0] // SB_BLOCK
    parts = [None] * nblk
    for j in reversed(range(nblk)):
        xj = xv[j * SB_BLOCK:(j + 1) * SB_BLOCK]
        parts[j] = _suffix_sum(u_mat, xj) + carry
        carry = carry + jnp.sum(xj, axis=0, keepdims=True)
    return jnp.concatenate(parts, axis=0), carry


def _sb_probs(kgrp, qt_h, u_mat, carry, past):
    z = _dot(kgrp, qt_h)
    lb = _log_sigmoid(z)
    l1 = lb - z
    if past is not None:
        l1 = jnp.where(past, l1, 0.0)
    between, carry = _group_suffix(u_mat, l1, carry)
    a = jnp.exp(lb + between)
    if past is not None:
        a = jnp.where(past, a, 0.0)
    return a, z, carry


def _sb_schedule(nq):
    steps = [(qi, g) for qi in range(nq) for g in range(qi, -1, -1)]
    return jnp.asarray([s[0] for s in steps], jnp.int32), jnp.asarray([s[1] for s in steps], jnp.int32)


def _past_mask(tq):
    return lax.broadcasted_iota(jnp.int32, (tq, tq), 0) < lax.broadcasted_iota(jnp.int32, (tq, tq), 1)


def _sb_fwd(proj, vt4, tq, shards):
    t = proj.shape[0]
    pairs = D_MODEL // SB_BLOCK
    nq = t // tq
    qi_tab, g_tab = _sb_schedule(nq)
    ns = qi_tab.shape[0]
    n = len(shards)

    def body(qi_ref, g_ref, q_ref, k_ref, vt_ref, *rest):
        o_ref, a_ref, z_ref = rest[n:n + 3]
        acc_ref, carry_ref, qt_ref = rest[2 * n + 3:2 * n + 6]
        begin, relay, finish = _gather_phases(rest[:n], rest[n + 3:2 * n + 3], *rest[2 * n + 6:])
        pi, si = pl.program_id(0), pl.program_id(1)
        diagonal = g_ref[si] == qi_ref[si]
        pl.when((pi == 0) & (si == 0))(begin)
        pl.when((pi == (3 * pairs) // 4) & (si == 0))(relay)

        @pl.when(diagonal)
        def _():
            acc_ref[...] = jnp.zeros_like(acc_ref)
            carry_ref[...] = jnp.zeros_like(carry_ref)
            qt = q_ref[...].astype(F32).T * (SB_HEAD_DIM ** -0.5)
            for h in range(2):
                qt_ref[h] = _head_rows(qt, h)

        def step(past):
            u_mat = _scan_matrix()
            for h in range(2):
                a, z, carry = _sb_probs(k_ref[...], qt_ref[h], u_mat, carry_ref[h], past)
                ab = a.astype(BF16)
                a_ref[h] = ab
                z_ref[h] = z.astype(BF16)
                acc_ref[h] += _dot(vt_ref[h * SB_HEAD_DIM:(h + 1) * SB_HEAD_DIM, :], ab)
                carry_ref[h] = carry

        pl.when(diagonal)(lambda: step(_past_mask(tq)))
        pl.when(jnp.logical_not(diagonal))(lambda: step(None))

        @pl.when(g_ref[si] == 0)
        def _():
            o_ref[...] = jnp.concatenate([acc_ref[0], acc_ref[1]], axis=0).T

        pl.when((pi == pairs - 1) & (si == ns - 1))(finish)

    tile = pl.BlockSpec((None, None, 2, tq, tq), lambda p, s, qt_, gt_: (p, s, 0, 0, 0))
    tiles = jax.ShapeDtypeStruct((pairs, ns, 2, tq, tq), BF16)
    outs = pl.pallas_call(
        body, name="sb_fwd",
        grid_spec=pltpu.PrefetchScalarGridSpec(
            num_scalar_prefetch=2, grid=(pairs, ns),
            in_specs=[pl.BlockSpec((tq, SB_BLOCK), lambda p, s, qt_, gt_: (qt_[s], COL_SQ * pairs + p)),
                      pl.BlockSpec((tq, SB_BLOCK), lambda p, s, qt_, gt_: (gt_[s], COL_SK * pairs + p)),
                      pl.BlockSpec((None, None, SB_BLOCK, tq), lambda p, s, qt_, gt_: (p, gt_[s], 0, 0))] + [ANY] * n,
            out_specs=[pl.BlockSpec((tq, SB_BLOCK), lambda p, s, qt_, gt_: (qt_[s], p)), tile, tile] + [ANY] * n,
            scratch_shapes=[pltpu.VMEM((2, SB_HEAD_DIM, tq), F32), pltpu.VMEM((2, 1, tq), F32),
                            pltpu.VMEM((2, SB_BLOCK, tq), BF16)] + _gather_sems(n)),
        out_shape=[jax.ShapeDtypeStruct((t, D_MODEL), F32), tiles, tiles] + _gather_shapes(shards),
        compiler_params=pltpu.CompilerParams(dimension_semantics=("arbitrary", "arbitrary")),
    )(qi_tab, g_tab, proj, proj, vt4, *shards)
    return outs[0], outs[1], outs[2], outs[3:]


def _mem_prep(mem, g_mem, wkv_all, k_norm_g):
    m, d = mem.shape

    def body(mem_ref, g_ref, w_ref, kg_ref, memn_ref, kn_ref, v_ref):
        memf = mem_ref[...]
        memn = (memf * _rstd(memf) * g_ref[...]).astype(BF16)
        memn_ref[...] = memn
        for b in range(N_DEV):
            kv = _dot(memn, w_ref[b])
            if b < X_HEADS:
                kn_ref[:, b * X_HEAD_DIM:(b + 1) * X_HEAD_DIM] = (kv * _rstd(kv) * kg_ref[...]).astype(BF16)
            else:
                h = b - X_HEADS
                v_ref[:, h * X_HEAD_DIM:(h + 1) * X_HEAD_DIM] = kv.astype(BF16)

    return pl.pallas_call(
        body, name="mem_prep",
        out_shape=[jax.ShapeDtypeStruct((m, d), BF16)] * 3,
    )(mem, g_mem, wkv_all, k_norm_g)


def _x_head(xq_ref, qg, kn_ref, h):
    sl = slice(h * X_HEAD_DIM, (h + 1) * X_HEAD_DIM)
    q = xq_ref[:, sl].astype(F32)
    rq = _rstd(q)
    qhat = q * rq
    qn = (qhat * qg).astype(BF16)
    s = _dot_nt(qn, kn_ref[:, sl]) * (X_HEAD_DIM ** -0.5)
    e = jnp.exp(s - jnp.max(s, axis=-1, keepdims=True))
    p = e / jnp.sum(e, axis=-1, keepdims=True)
    return sl, rq, qhat, qn, p


def _x_fwd(proj, q_norm_g, kn, v, tm):
    t = proj.shape[0]
    m = kn.shape[0]

    def body(xq_ref, qg_ref, kn_ref, v_ref, o_ref):
        for h in range(X_HEADS):
            sl, _, _, _, p = _x_head(xq_ref, qg_ref[...], kn_ref, h)
            o_ref[:, sl] = _dot(p.astype(BF16), v_ref[:, sl]).astype(BF16)

    return pl.pallas_call(
        body, name="x_fwd",
        grid=(t // tm,),
        in_specs=[pl.BlockSpec((tm, D_MODEL), lambda i: (i, COL_XQ)),
                  pl.BlockSpec((1, X_HEAD_DIM), lambda i: (0, 0)),
                  pl.BlockSpec((m, D_MODEL), lambda i: (0, 0)),
                  pl.BlockSpec((m, D_MODEL), lambda i: (0, 0))],
        out_specs=pl.BlockSpec((tm, D_MODEL), lambda i: (i, 0)),
        out_shape=jax.ShapeDtypeStruct((t, D_MODEL), BF16),
        compiler_params=pltpu.CompilerParams(dimension_semantics=("parallel",)),
    )(proj, q_norm_g, kn, v)


def _gate_spec(tm, branch):
    return pl.BlockSpec((tm, D_MODEL), lambda i: (i, COL_GATE + branch))


def _merge_fwd(x, proj, a_conv, o_sb, o_x, w_conv_out, w_sb_out, w_x_out, w_out, tm):
    t, d = x.shape

    def body(x_ref, g0_ref, g1_ref, g2_ref, a_ref, s_ref, xo_ref, wc_ref, ws_ref, wx_ref, wo_ref,
             x1_ref, yc_ref, ys_ref, yx_ref, mg_ref):
        merged = jnp.zeros((tm, d), F32)
        for gate_ref, b_ref, w_ref, y_ref in ((g0_ref, a_ref, wc_ref, yc_ref), (g1_ref, s_ref, ws_ref, ys_ref),
                                              (g2_ref, xo_ref, wx_ref, yx_ref)):
            yv = _dot(b_ref[...].astype(BF16), w_ref[...])
            y_ref[...] = yv.astype(BF16)
            merged = merged + _sigmoid(gate_ref[...].astype(F32)) * yv
        mb = merged.astype(BF16)
        mg_ref[...] = mb
        x1_ref[...] = x_ref[...] + _dot(mb, wo_ref[...])

    tile = pl.BlockSpec((tm, d), lambda i: (i, 0))
    wfull = pl.BlockSpec((d, d), lambda i: (0, 0))
    return pl.pallas_call(
        body, name="merge_fwd",
        grid=(t // tm,),
        in_specs=[tile] + [_gate_spec(tm, b) for b in range(N_BRANCH)] + [tile, tile, tile,
                                                                           wfull, wfull, wfull, wfull],
        out_specs=[tile] * 5,
        out_shape=[jax.ShapeDtypeStruct((t, d), F32)] + [jax.ShapeDtypeStruct((t, d), BF16)] * 4,
        compiler_params=pltpu.CompilerParams(dimension_semantics=("parallel",)),
    )(x, proj, proj, proj, a_conv, o_sb, o_x, w_conv_out, w_sb_out, w_x_out, w_out)


def _mlp_fwd(x1, g_mlp, w_up_all, w_down, target, tm):
    t, d = x1.shape
    nb, _, fw = w_up_all.shape

    def body(x1_ref, g_ref, wu_ref, wd_ref, tgt_ref, up_ref, h2_ref, dx2_ref, lsum_ref, acc_ref):
        i, j = pl.program_id(0), pl.program_id(1)

        @pl.when(j == 0)
        def _():
            xf = x1_ref[...]
            h2_ref[...] = (xf * _rstd(xf) * g_ref[...]).astype(BF16)
            acc_ref[...] = jnp.zeros_like(acc_ref)

        @pl.when((i == 0) & (j == 0))
        def _():
            lsum_ref[...] = jnp.zeros_like(lsum_ref)

        up = _dot(h2_ref[...], wu_ref[...])
        up_ref[...] = up.astype(BF16)
        act = jnp.square(jnp.maximum(up, 0.0)).astype(BF16)
        acc_ref[...] += _dot(act, wd_ref[...])

        @pl.when(j == nb - 1)
        def _():
            diff = x1_ref[...] + acc_ref[...] - tgt_ref[...]
            dx2_ref[...] = diff * (1.0 / d)
            lsum_ref[...] += jnp.sum(diff * diff, axis=0, keepdims=True)

    tile = pl.BlockSpec((tm, d), lambda i, j: (i, 0))
    row = pl.BlockSpec((1, d), lambda i, j: (0, 0))
    return pl.pallas_call(
        body, name="mlp_fwd",
        grid=(t // tm, nb),
        in_specs=[tile, row, pl.BlockSpec((None, d, fw), lambda i, j: (j, 0, 0)),
                  pl.BlockSpec((fw, d), lambda i, j: (j, 0)), tile],
        out_specs=[pl.BlockSpec((tm, fw), lambda i, j: (i, j)), tile, tile, row],
        out_shape=[jax.ShapeDtypeStruct((t, nb * fw), BF16), jax.ShapeDtypeStruct((t, d), BF16),
                   jax.ShapeDtypeStruct((t, d), F32), jax.ShapeDtypeStruct((1, d), F32)],
        scratch_shapes=[pltpu.VMEM((tm, d), F32)],
        compiler_params=pltpu.CompilerParams(dimension_semantics=("arbitrary", "arbitrary")),
    )(x1, g_mlp, w_up_all, w_down, target)


def _mlp_bwd(x1, g_mlp, w_up_all, w_down, up, dx2, tm):
    t, d = x1.shape
    nb, _, fw = w_up_all.shape

    def body(x1_ref, g_ref, wu_ref, wd_ref, up_ref, dx2_ref, dup_ref, act_ref, dx1_ref, dg_ref, acc_ref, dyb_ref):
        i, j = pl.program_id(0), pl.program_id(1)

        @pl.when(j == 0)
        def _():
            dyb_ref[...] = dx2_ref[...].astype(BF16)
            acc_ref[...] = jnp.zeros_like(acc_ref)

        @pl.when((i == 0) & (j == 0))
        def _():
            dg_ref[...] = jnp.zeros_like(dg_ref)

        r = jnp.maximum(up_ref[...].astype(F32), 0.0)
        act_ref[...] = jnp.square(r).astype(BF16)
        dup = (_dot_nt(dyb_ref[...], wd_ref[...]) * (2.0 * r)).astype(BF16)
        dup_ref[...] = dup
        acc_ref[...] += _dot_nt(dup, wu_ref[...])

        @pl.when(j == nb - 1)
        def _():
            xf = x1_ref[...]
            rs = _rstd(xf)
            xhat = xf * rs
            dh2 = acc_ref[...]
            dg_ref[...] += jnp.sum(dh2 * xhat, axis=0, keepdims=True)
            dx1_ref[...] = dx2_ref[...] + _rms_bwd(dh2, xhat, rs, g_ref[...])

    tile = pl.BlockSpec((tm, d), lambda i, j: (i, 0))
    row = pl.BlockSpec((1, d), lambda i, j: (0, 0))
    ff = pl.BlockSpec((tm, fw), lambda i, j: (i, j))
    return pl.pallas_call(
        body, name="mlp_bwd",
        grid=(t // tm, nb),
        in_specs=[tile, row, pl.BlockSpec((None, d, fw), lambda i, j: (j, 0, 0)),
                  pl.BlockSpec((fw, d), lambda i, j: (j, 0)), ff, tile],
        out_specs=[ff, ff, tile, row],
        out_shape=[jax.ShapeDtypeStruct((t, nb * fw), BF16), jax.ShapeDtypeStruct((t, nb * fw), BF16),
                   jax.ShapeDtypeStruct((t, d), F32), jax.ShapeDtypeStruct((1, d), F32)],
        scratch_shapes=[pltpu.VMEM((tm, d), F32), pltpu.VMEM((tm, d), BF16)],
        compiler_params=pltpu.CompilerParams(dimension_semantics=("arbitrary", "arbitrary")),
    )(x1, g_mlp, w_up_all, w_down, up, dx2)


def _merge_bwd(dx1, proj, y_conv, y_sb, y_x, w_conv_out, w_sb_out, w_x_out, w_out, tm):
    t, d = dx1.shape

    def body(dx1_ref, g0_ref, g1_ref, g2_ref, yc_ref, ys_ref, yx_ref, wc_ref, ws_ref, wx_ref, wo_ref,
             dgate_ref, dyc_ref, dys_ref, dyx_ref, da_ref, dos_ref, dox_ref):
        dm = _dot_nt(dx1_ref[...].astype(BF16), wo_ref[...])
        for i, (gate_ref, y_ref, w_ref, dy_ref, db_ref) in enumerate(((g0_ref, yc_ref, wc_ref, dyc_ref, da_ref),
                                                                       (g1_ref, ys_ref, ws_ref, dys_ref, dos_ref),
                                                                       (g2_ref, yx_ref, wx_ref, dyx_ref, dox_ref))):
            gt = _sigmoid(gate_ref[...].astype(F32))
            dy = (dm * gt).astype(BF16)
            dy_ref[...] = dy
            dgate_ref[:, i * d:(i + 1) * d] = (dm * y_ref[...].astype(F32) * gt * (1.0 - gt)).astype(BF16)
            db_ref[...] = _dot_nt(dy, w_ref[...]).astype(BF16)

    tile = pl.BlockSpec((tm, d), lambda i: (i, 0))
    wfull = pl.BlockSpec((d, d), lambda i: (0, 0))
    return pl.pallas_call(
        body, name="merge_bwd",
        grid=(t // tm,),
        in_specs=[tile] + [_gate_spec(tm, b) for b in range(N_BRANCH)] + [tile, tile, tile,
                                                                           wfull, wfull, wfull, wfull],
        out_specs=[pl.BlockSpec((tm, N_BRANCH * d), lambda i: (i, 0))] + [tile] * 6,
        out_shape=[jax.ShapeDtypeStruct((t, N_BRANCH * d), BF16)] + [jax.ShapeDtypeStruct((t, d), BF16)] * 6,
        compiler_params=pltpu.CompilerParams(dimension_semantics=("parallel",)),
    )(dx1, proj, proj, proj, y_conv, y_sb, y_x, w_conv_out, w_sb_out, w_x_out, w_out)


def _conv_bwd(proj, conv_w, da, cw):
    t = proj.shape[0]
    nper = D_MODEL // cw

    def body(ch_ref, cb_ref, cc_ref, w_ref, da_ref, dch_ref, dcb_ref, dcc_ref, dw_ref):
        ch, cb, cc, u, u1, u2, cv, w, row = _conv_terms(ch_ref, cb_ref, cc_ref, w_ref)
        dav = da_ref[...].astype(F32)
        dcb_ref[...] = (dav * cv).astype(BF16)
        dcv = dav * cb
        n1 = jnp.where(row < t - 1, pltpu.roll(dcv, t - 1, 0), 0.0)
        n2 = jnp.where(row < t - 2, pltpu.roll(dcv, t - 2, 0), 0.0)
        du = w[2] * dcv + w[1] * n1 + w[0] * n2
        dcc_ref[...] = (du * ch).astype(BF16)
        dch_ref[...] = (du * cc).astype(BF16)
        dw_ref[0:1, :] = jnp.sum(dcv * u2, axis=0, keepdims=True)
        dw_ref[1:2, :] = jnp.sum(dcv * u1, axis=0, keepdims=True)
        dw_ref[2:3, :] = jnp.sum(dcv * u, axis=0, keepdims=True)

    def col(piece):
        return pl.BlockSpec((t, cw), lambda j: (0, piece * nper + j))

    out_col = pl.BlockSpec((t, cw), lambda j: (0, j))
    wspec = pl.BlockSpec((3, cw), lambda j: (0, j))
    return pl.pallas_call(
        body, name="conv_bwd",
        grid=(nper,),
        in_specs=[col(COL_CH), col(COL_CB), col(COL_CC), wspec, out_col],
        out_specs=[out_col, out_col, out_col, wspec],
        out_shape=[jax.ShapeDtypeStruct((t, D_MODEL), BF16)] * 3 + [jax.ShapeDtypeStruct((3, D_MODEL), F32)],
        compiler_params=pltpu.CompilerParams(dimension_semantics=("parallel",)),
    )(proj, proj, proj, conv_w, da)


def _sb_bwd(proj, kt4, do_sb, o_sb, weights, logits, tq, pair_sums):
    t = proj.shape[0]
    nq = t // tq
    pairs = D_MODEL // SB_BLOCK
    scale = SB_HEAD_DIM ** -0.5
    qi_tab, g_tab = _sb_schedule(nq)
    ns = qi_tab.shape[0]
    n = len(pair_sums)

    def body(qi_ref, g_ref, q_ref, v_ref, kt_ref, do_ref, o_ref, a_ref, z_ref, *rest):
        dq_ref, dk_ref, dv_ref = rest[n:n + 3]
        dk_acc, dv_acc, dqt_ref, carry_ref, qm_ref, dom_ref, dot_ref, dsum_ref = rest[2 * n + 3:2 * n + 11]
        begin, finish = _chip_exchange_phases(rest[:n], rest[n + 3:2 * n + 3], *rest[2 * n + 11:])
        pi, si = pl.program_id(0), pl.program_id(1)
        diagonal = g_ref[si] == qi_ref[si]
        pl.when((pi == 0) & (si == 0))(begin)

        @pl.when(si == 0)
        def _():
            dk_acc[...] = jnp.zeros_like(dk_acc)
            dv_acc[...] = jnp.zeros_like(dv_acc)

        @pl.when(diagonal)
        def _():
            dqt_ref[...] = jnp.zeros_like(dqt_ref)
            carry_ref[...] = jnp.zeros_like(carry_ref)
            q = q_ref[...].astype(F32) * scale
            do = do_ref[...].astype(F32)
            dot_ = do.T
            prod = dot_ * o_ref[...].T
            for h in range(2):
                rows = slice(h * SB_HEAD_DIM, (h + 1) * SB_HEAD_DIM)
                qm_ref[h] = _head_lanes(q, h)
                dom_ref[h] = _head_lanes(do, h)
                dot_ref[h] = _head_rows(dot_, h)
                dsum_ref[h] = jnp.sum(prod[rows, :], axis=0, keepdims=True)

        def step(past):
            u_mat = _scan_matrix()
            ks = pl.multiple_of(g_ref[si] * tq, tq)
            dk_add = jnp.zeros((tq, SB_BLOCK), F32)
            dv_add = jnp.zeros((tq, SB_BLOCK), F32)
            for h in range(2):
                rows = slice(h * SB_HEAD_DIM, (h + 1) * SB_HEAD_DIM)
                ab = a_ref[h]
                gw = _dot(v_ref[...], dot_ref[h]) * ab.astype(F32)
                after, carry = _group_suffix(u_mat, gw, carry_ref[h])
                sig = pl.reciprocal(1.0 + jnp.exp(-z_ref[h].astype(F32)), approx=True)
                dz = gw - sig * (dsum_ref[h] - after)
                if past is not None:
                    dz = jnp.where(past, dz, 0.0)
                dzb = dz.astype(BF16)
                dqt_ref[h] += _dot(kt_ref[rows, :], dzb)
                dk_add = dk_add + _dot(dzb, qm_ref[h])
                dv_add = dv_add + _dot(ab, dom_ref[h])
                carry_ref[h] = carry
            dk_acc[pl.ds(ks, tq), :] += dk_add
            dv_acc[pl.ds(ks, tq), :] += dv_add

        pl.when(diagonal)(lambda: step(_past_mask(tq)))
        pl.when(jnp.logical_not(diagonal))(lambda: step(None))

        @pl.when(g_ref[si] == 0)
        def _():
            dq_ref[...] = (jnp.concatenate([dqt_ref[0], dqt_ref[1]], axis=0).T * scale).astype(BF16)

        @pl.when(si == ns - 1)
        def _():
            dk_ref[...] = dk_acc[...].astype(BF16)
            dv_ref[...] = dv_acc[...].astype(BF16)

        pl.when((pi == pairs - 1) & (si == ns - 1))(finish)

    qblk = lambda base: pl.BlockSpec((tq, SB_BLOCK), lambda p, s, qt_, gt_: (qt_[s], base * pairs + p))
    kgrp = lambda base: pl.BlockSpec((tq, SB_BLOCK), lambda p, s, qt_, gt_: (gt_[s], base * pairs + p))
    seq = pl.BlockSpec((t, SB_BLOCK), lambda p, s, qt_, gt_: (0, p))
    tr = pl.BlockSpec((None, None, SB_BLOCK, tq), lambda p, s, qt_, gt_: (p, gt_[s], 0, 0))
    tile = pl.BlockSpec((None, None, 2, tq, tq), lambda p, s, qt_, gt_: (p, s, 0, 0, 0))
    outs = pl.pallas_call(
        body, name="sb_bwd",
        grid_spec=pltpu.PrefetchScalarGridSpec(
            num_scalar_prefetch=2, grid=(pairs, ns),
            in_specs=[qblk(COL_SQ), kgrp(COL_SV), tr, qblk(0), qblk(0), tile, tile] + [ANY] * n,
            out_specs=[qblk(0), seq, seq] + [ANY] * n,
            scratch_shapes=[pltpu.VMEM((t, SB_BLOCK), F32), pltpu.VMEM((t, SB_BLOCK), F32),
                            pltpu.VMEM((2, SB_HEAD_DIM, tq), F32), pltpu.VMEM((2, 1, tq), F32),
                            pltpu.VMEM((2, tq, SB_BLOCK), BF16), pltpu.VMEM((2, tq, SB_BLOCK), BF16),
                            pltpu.VMEM((2, SB_BLOCK, tq), BF16), pltpu.VMEM((2, 1, tq), F32)] + _chip_exchange_sems(n)),
        out_shape=[jax.ShapeDtypeStruct((t, D_MODEL), BF16)] * 3 + _chip_exchange_shapes(pair_sums),
        compiler_params=pltpu.CompilerParams(dimension_semantics=("arbitrary", "arbitrary")),
    )(qi_tab, g_tab, proj, proj, kt4, do_sb, o_sb, weights, logits, *pair_sums)
    return outs[0], outs[1], outs[2], outs[3:]


def _x_bwd(proj, q_norm_g, kn, v, do_x, tm):
    t = proj.shape[0]
    m = kn.shape[0]
    scale = X_HEAD_DIM ** -0.5

    def body(xq_ref, qg_ref, kn_ref, v_ref, do_ref, dxq_ref, dkn_ref, dv_ref, dqg_ref):
        @pl.when(pl.program_id(0) == 0)
        def _():
            dkn_ref[...] = jnp.zeros_like(dkn_ref)
            dv_ref[...] = jnp.zeros_like(dv_ref)
            dqg_ref[...] = jnp.zeros_like(dqg_ref)

        qg = qg_ref[...]
        for h in range(X_HEADS):
            sl, rq, qhat, qn, p = _x_head(xq_ref, qg, kn_ref, h)
            do_h = do_ref[:, sl]
            dp = _dot_nt(do_h, v_ref[:, sl])
            ds = (p * (dp - jnp.sum(dp * p, axis=-1, keepdims=True)) * scale).astype(BF16)
            dqn = _dot(ds, kn_ref[:, sl])
            dkn_ref[:, sl] += _dot_tn(ds, qn)
            dv_ref[:, sl] += _dot_tn(p.astype(BF16), do_h)
            dqg_ref[...] += jnp.sum(dqn * qhat, axis=0, keepdims=True)
            dxq_ref[:, sl] = _rms_bwd(dqn, qhat, rq, qg).astype(BF16)

    full = pl.BlockSpec((m, D_MODEL), lambda i: (0, 0))
    gain = pl.BlockSpec((1, X_HEAD_DIM), lambda i: (0, 0))
    tile = pl.BlockSpec((tm, D_MODEL), lambda i: (i, 0))
    return pl.pallas_call(
        body, name="x_bwd",
        grid=(t // tm,),
        in_specs=[pl.BlockSpec((tm, D_MODEL), lambda i: (i, COL_XQ)), gain, full, full, tile],
        out_specs=[tile, full, full, gain],
        out_shape=[jax.ShapeDtypeStruct((t, D_MODEL), BF16), jax.ShapeDtypeStruct((m, D_MODEL), F32),
                   jax.ShapeDtypeStruct((m, D_MODEL), F32), jax.ShapeDtypeStruct((1, X_HEAD_DIM), F32)],
        compiler_params=pltpu.CompilerParams(dimension_semantics=("arbitrary",)),
    )(proj, q_norm_g, kn, v, do_x)


def _mem_bwd(mem, g_mem, wkv_all, k_norm_g, dkn, dv):
    m, d = mem.shape

    def body(mem_ref, g_ref, w_ref, kg_ref, dkn_ref, dv_ref, dkv_ref, dgm_ref, dkg_ref):
        memf = mem_ref[...]
        mem_hat = memf * _rstd(memf)
        memn = (mem_hat * g_ref[...]).astype(BF16)
        kg = kg_ref[...]
        dmemn = jnp.zeros((m, d), F32)
        dkg = jnp.zeros((1, X_HEAD_DIM), F32)
        for b in range(N_DEV):
            sl = slice(b * X_HEAD_DIM, (b + 1) * X_HEAD_DIM)
            if b < X_HEADS:
                kv = _dot(memn, w_ref[b])
                rk = _rstd(kv)
                khat = kv * rk
                dkn_h = dkn_ref[:, sl]
                dkg = dkg + jnp.sum(dkn_h * khat, axis=0, keepdims=True)
                dblk = _rms_bwd(dkn_h, khat, rk, kg).astype(BF16)
            else:
                hs = slice((b - X_HEADS) * X_HEAD_DIM, (b - X_HEADS + 1) * X_HEAD_DIM)
                dblk = dv_ref[:, hs].astype(BF16)
            dkv_ref[:, sl] = dblk
            dmemn = dmemn + _dot_nt(dblk, w_ref[b])
        dgm_ref[...] = jnp.sum(dmemn * mem_hat, axis=0, keepdims=True)
        dkg_ref[...] = dkg

    return pl.pallas_call(
        body, name="mem_bwd",
        out_shape=[jax.ShapeDtypeStruct((m, 2 * d), BF16), jax.ShapeDtypeStruct((1, d), F32),
                   jax.ShapeDtypeStruct((1, X_HEAD_DIM), F32)],
    )(mem, g_mem, wkv_all, k_norm_g, dkn, dv)


def _in_proj_bwd(x, g_mix, w_in_all, dproj, dx1, tm, pair_sums):
    t, d = x.shape
    nb, _, bw = w_in_all.shape
    nt = t // tm
    n = len(pair_sums)

    def body(x_ref, g_ref, w_ref, dp_ref, dx1_ref, *rest):
        dx_ref, dg_ref = rest[n:n + 2]
        acc_ref = rest[2 * n + 2]
        begin, finish = _chip_exchange_phases(rest[:n], rest[n + 2:2 * n + 2], *rest[2 * n + 3:])
        i, j = pl.program_id(0), pl.program_id(1)
        pl.when((i == 0) & (j == 0))(begin)

        @pl.when(j == 0)
        def _():
            acc_ref[...] = jnp.zeros_like(acc_ref)

        @pl.when((i == 0) & (j == 0))
        def _():
            dg_ref[...] = jnp.zeros_like(dg_ref)

        acc_ref[...] += _dot_nt(dp_ref[...], w_ref[...])

        @pl.when(j == nb - 1)
        def _():
            xf = x_ref[...]
            rs = _rstd(xf)
            xhat = xf * rs
            dh = acc_ref[...]
            dg_ref[...] += jnp.sum(dh * xhat, axis=0, keepdims=True)
            dx_ref[...] = dx1_ref[...] + _rms_bwd(dh, xhat, rs, g_ref[...])

        pl.when((i == nt - 1) & (j == nb - 1))(finish)

    tile = pl.BlockSpec((tm, d), lambda i, j: (i, 0))
    row = pl.BlockSpec((1, d), lambda i, j: (0, 0))
    outs = pl.pallas_call(
        body, name="in_proj_bwd",
        grid=(nt, nb),
        in_specs=[tile, row, pl.BlockSpec((None, d, bw), lambda i, j: (j, 0, 0)),
                  pl.BlockSpec((tm, bw), lambda i, j: (i, j)), tile] + [ANY] * n,
        out_specs=[tile, row] + [ANY] * n,
        out_shape=[jax.ShapeDtypeStruct((t, d), F32), jax.ShapeDtypeStruct((1, d), F32)] + _chip_exchange_shapes(pair_sums),
        scratch_shapes=[pltpu.VMEM((tm, d), F32)] + _chip_exchange_sems(n),
        compiler_params=pltpu.CompilerParams(dimension_semantics=("arbitrary", "arbitrary")),
    )(x, g_mix, w_in_all, dproj, dx1, *pair_sums)
    return outs[0], outs[1], outs[2:]


def _weight_grad(a, b, bw, tmm, name):
    t, m = a.shape
    n = b.shape[1]
    tmm = min(tmm, m)

    def body(a_ref, b_ref, o_ref):
        o_ref[...] = _dot_tn(a_ref[...].astype(BF16), b_ref[...].astype(BF16)).astype(BF16)

    return pl.pallas_call(
        body, name=name,
        grid=(m // tmm, n // bw),
        in_specs=[pl.BlockSpec((t, tmm), lambda i, j: (0, i)), pl.BlockSpec((t, bw), lambda i, j: (0, j))],
        out_specs=pl.BlockSpec((None, tmm, bw), lambda i, j: (j, i, 0)),
        out_shape=jax.ShapeDtypeStruct((n // bw, m, bw), BF16),
        compiler_params=pltpu.CompilerParams(dimension_semantics=("parallel", "parallel")),
    )(a, b)


def _pair_sum(grad, recv, own_blocks, name):
    _, rows, cols = grad.shape

    def body(idx_ref, g_ref, r_ref, o_ref):
        o_ref[...] = (g_ref[...].astype(F32) + r_ref[...].astype(F32)).astype(BF16)

    return pl.pallas_call(
        body, name=name,
        grid_spec=pltpu.PrefetchScalarGridSpec(
            num_scalar_prefetch=1, grid=(4,),
            in_specs=[pl.BlockSpec((None, rows, cols), lambda r, idx: (idx[r], 0, 0)),
                      pl.BlockSpec((None, rows, cols), lambda r, idx: (r, 0, 0))],
            out_specs=pl.BlockSpec((None, rows, cols), lambda r, idx: (r, 0, 0))),
        out_shape=jax.ShapeDtypeStruct((4, rows, cols), BF16),
        compiler_params=pltpu.CompilerParams(dimension_semantics=("parallel",)),
    )(own_blocks, grad, recv)


def _adamw_math(w, g, m, v):
    m = ADAM_B1 * m + (1.0 - ADAM_B1) * g
    v = ADAM_B2 * v + (1.0 - ADAM_B2) * jnp.square(g)
    m_hat = m / (1.0 - ADAM_B1 ** ADAM_STEP)
    v_hat = v / (1.0 - ADAM_B2 ** ADAM_STEP)
    delta = -ADAM_LR * (m_hat / (jnp.sqrt(v_hat) + ADAM_EPS) + ADAM_WD * w)
    return delta, m, v


def _adamw_sharded(pair_sums, recv, w, m, v, tr, name):
    rows, cols = w.shape
    tr = min(tr, rows)

    def body(h_ref, r_ref, w_ref, m_ref, v_ref, g_out, d_out, m_out, v_out):
        g = h_ref[...].astype(F32)
        for r in range(3):
            g = g + r_ref[r].astype(F32)
        g_out[...] = g
        d_out[...], m_out[...], v_out[...] = _adamw_math(w_ref[...], g, m_ref[...], v_ref[...])

    tile = pl.BlockSpec((tr, cols), lambda i: (i, 0))
    return pl.pallas_call(
        body, name=name,
        grid=(rows // tr,),
        in_specs=[pl.BlockSpec((None, tr, cols), lambda i: (0, i, 0)),
                  pl.BlockSpec((3, tr, cols), lambda i: (0, i, 0)), tile, tile, tile],
        out_specs=[tile] * 4,
        out_shape=[jax.ShapeDtypeStruct((rows, cols), F32)] * 4,
        compiler_params=pltpu.CompilerParams(dimension_semantics=("parallel",)),
    )(pair_sums, recv, w, m, v)


SMALL_ROWS = 16


def _pack_rows(dg_mix, dg_mem, dg_mlp, dqg, dkg, dconv, lsum):
    def body(a_ref, b_ref, c_ref, q_ref, k_ref, cv_ref, l_ref, o_ref):
        o_ref[...] = jnp.zeros_like(o_ref)
        for r, ref in enumerate((a_ref, b_ref, c_ref)):
            o_ref[r:r + 1, :] = ref[...]
        o_ref[3:4, :X_HEAD_DIM] = q_ref[...]
        o_ref[4:5, :X_HEAD_DIM] = k_ref[...]
        o_ref[5:8, :] = cv_ref[...]
        o_ref[8:9, :] = l_ref[...]

    return pl.pallas_call(body, name="small_pack", out_shape=jax.ShapeDtypeStruct((SMALL_ROWS, D_MODEL), F32))(
        dg_mix, dg_mem, dg_mlp, dqg, dkg, dconv, lsum)


def _small_sum(gathered):
    def body(g_ref, o_ref):
        total = g_ref[0]
        for dev in range(1, N_DEV):
            total = total + g_ref[dev]
        o_ref[...] = jnp.zeros_like(o_ref)
        for piece in range(5):
            o_ref[piece * SMALL_TILE:piece * SMALL_TILE + 1, :] = total[piece:piece + 1]
        o_ref[5 * SMALL_TILE:5 * SMALL_TILE + 3, :] = total[5:8]
        o_ref[6 * SMALL_TILE:6 * SMALL_TILE + 1, :] = total[8:9]

    return pl.pallas_call(body, name="small_grad_sum",
                          out_shape=jax.ShapeDtypeStruct((7 * SMALL_TILE, D_MODEL), F32))(gathered)


def _adamw_small(w, g, m, v):
    def body(w_ref, g_ref, m_ref, v_ref, d_out, m_out, v_out):
        d_out[...], m_out[...], v_out[...] = _adamw_math(w_ref[...], g_ref[...], m_ref[...], v_ref[...])

    return pl.pallas_call(body, name="adamw_small", out_shape=[jax.ShapeDtypeStruct(w.shape, F32)] * 3)(w, g, m, v)


def _pad_tile(a):
    return jnp.pad(a, ((0, SMALL_TILE - a.shape[0]), (0, D_MODEL - a.shape[1])))


def _pack_small(*pieces):
    return jnp.concatenate([_pad_tile(a) for a in pieces], axis=0)


def kernel(x, mem, g_mix, g_mem, w_in, conv_w, w_conv_out, w_sb_out, q_norm_g, k_norm_g, w_mem_kv, w_x_out, w_out, g_mlp, w_up, w_down, loss_target, m_g_mix, m_g_mem, m_w_in, m_conv_w, m_w_conv_out, m_w_sb_out, m_q_norm_g, m_k_norm_g, m_w_mem_kv, m_w_x_out, m_w_out, m_g_mlp, m_w_up, m_w_down, v_g_mix, v_g_mem, v_w_in, v_conv_w, v_w_conv_out, v_w_sb_out, v_q_norm_g, v_k_norm_g, v_w_mem_kv, v_w_x_out, v_w_out, v_g_mlp, v_w_up, v_w_down):
    xpos, ypos, cpos = _mesh_pos()
    me = 4 * xpos + 2 * ypos + cpos
    x2d, mem2d, tgt2d = x[0], mem[0], loss_target[0]
    t = x2d.shape[0]
    tm = min(512, t)
    tm_s = min(256, t)

    big = {
        "w_in": (w_in[0], m_w_in[0], v_w_in[0]),
        "w_conv_out": (w_conv_out[0], m_w_conv_out[0], v_w_conv_out[0]),
        "w_sb_out": (w_sb_out[0], m_w_sb_out[0], v_w_sb_out[0]),
        "w_mem_kv": (w_mem_kv[0], m_w_mem_kv[0], v_w_mem_kv[0]),
        "w_x_out": (w_x_out[0], m_w_x_out[0], v_w_x_out[0]),
        "w_out": (w_out[0], m_w_out[0], v_w_out[0]),
        "w_up": (w_up[0], m_w_up[0], v_w_up[0]),
        "w_down": (w_down[0], m_w_down[0], v_w_down[0]),
    }
    late = [n for n in big if n != "w_in"]
    behind_in_proj = ["w_mem_kv", "w_conv_out", "w_sb_out"]
    behind_sb = ["w_x_out", "w_out", "w_up", "w_down"]
    as_bf16 = lambda group: [big[n][0].astype(BF16) for n in group]
    conv_pad = jnp.pad(conv_w[0], ((0, 8 - 3), (0, 0)))
    w_in_all, conv_all = _all_gather([big["w_in"][0].astype(BF16), conv_pad])
    conv_full = conv_all[:, :3, :].transpose(1, 0, 2).reshape(3, D_MODEL)

    proj, h, gathered_a = _in_proj(x2d, g_mix, w_in_all, tm, as_bf16(behind_in_proj))
    a_conv = _conv_fwd(proj, conv_full, 256)
    tq = min(SB_QUERY_TILE, t)
    pairs = D_MODEL // SB_BLOCK

    def groups_t(cols):
        return cols.reshape(t // tq, tq, pairs, SB_BLOCK).transpose(2, 0, 3, 1)

    kt4 = groups_t(proj[:, COL_SK * D_MODEL:(COL_SK + 1) * D_MODEL])
    vt4 = groups_t(proj[:, COL_SV * D_MODEL:(COL_SV + 1) * D_MODEL])
    o_sb, sb_weights, sb_logits, gathered_b = _sb_fwd(proj, vt4, tq, as_bf16(behind_sb))
    full = dict(zip(behind_in_proj + behind_sb, list(gathered_a) + list(gathered_b)))
    wkv_all, w_up_all = full["w_mem_kv"], full["w_up"]
    rows_full = lambda a: a.reshape(a.shape[0] * a.shape[1], a.shape[2])
    wc, ws, wx, wo, wd = (rows_full(full[n]) for n in ("w_conv_out", "w_sb_out", "w_x_out", "w_out", "w_down"))
    mem_n, kn, vmem = _mem_prep(mem2d, g_mem, wkv_all, k_norm_g)
    o_x = _x_fwd(proj, q_norm_g, kn, vmem, tm_s)
    x1, y_conv, y_sb, y_x, merged = _merge_fwd(x2d, proj, a_conv, o_sb, o_x, wc, ws, wx, wo, tm_s)
    up, h2, dx2, lsum = _mlp_fwd(x1, g_mlp, w_up_all, wd, tgt2d, tm)

    dup, act, dx1, dg_mlp = _mlp_bwd(x1, g_mlp, w_up_all, wd, up, dx2, tm)
    dgate, dy_conv, dy_sb, dy_x, da_conv, do_sb, do_x = _merge_bwd(dx1, proj, y_conv, y_sb, y_x, wc, ws, wx, wo, tm_s)
    dch, dcb, dcc, dconv = _conv_bwd(proj, conv_full, da_conv, 256)
    dxq, dkn, dvm, dqg = _x_bwd(proj, q_norm_g, kn, vmem, do_x, tm_s)
    dkv, dg_mem, dkg = _mem_bwd(mem2d, g_mem, wkv_all, k_norm_g, dkn, dvm)
    wgrads = {
        "w_conv_out": _weight_grad(a_conv, dy_conv, D_MODEL, 512, "dw_conv_out"),
        "w_sb_out": _weight_grad(o_sb, dy_sb, D_MODEL, 512, "dw_sb_out"),
        "w_mem_kv": _weight_grad(mem_n, dkv, wkv_all.shape[2], 512, "dw_mem_kv"),
        "w_x_out": _weight_grad(o_x, dy_x, D_MODEL, 512, "dw_x_out"),
        "w_out": _weight_grad(merged, dx1, D_MODEL, 512, "dw_out"),
        "w_up": _weight_grad(h2, dup, w_up_all.shape[2], 512, "dw_up"),
        "w_down": _weight_grad(act, dx2, D_MODEL, 512, "dw_down"),
    }

    own_blocks = jnp.stack([4 * (xpos ^ dx) + 2 * (ypos ^ dy) + cpos for dx in (0, 1) for dy in (0, 1)]).astype(jnp.int32)

    def pair_reduce(group):
        blocked = [wgrads[n].reshape((N_DEV,) + big[n][0].shape) for n in group]
        from_sibling = _pair_exchange(blocked, "grad_pair_exchange_" + group[0])
        return [_pair_sum(g, r, own_blocks, "pair_sum_" + n) for n, g, r in zip(group, blocked, from_sibling)]

    pair_sums = dict(zip(late, pair_reduce(late)))
    dq, dk, dv, from_chips_late = _sb_bwd(proj, kt4, do_sb, o_sb, sb_weights, sb_logits, tq,
                                          [pair_sums[n] for n in late])
    from_chips = dict(zip(late, from_chips_late))
    dproj = jnp.concatenate([dch, dcb, dcc, dq, dk, dv, dxq, dgate], axis=1)
    wgrads["w_in"] = _weight_grad(h, dproj, w_in_all.shape[2], 512, "dw_in")
    pair_sums["w_in"], = pair_reduce(["w_in"])
    grad_x, dg_mix, (from_chips["w_in"],) = _in_proj_bwd(x2d, g_mix, w_in_all, dproj, dx1, tm, [pair_sums["w_in"]])
    res = {}
    for n in big:
        w_sh, m_sh, v_sh = big[n]
        res[n] = _adamw_sharded(pair_sums[n], from_chips[n], w_sh, m_sh, v_sh, 256, "adamw_" + n)

    part = _pack_rows(dg_mix, dg_mem, dg_mlp, dqg, dkg, dconv, lsum)
    gsum = _small_sum(_small_all_gather(part))
    loss = 0.5 * jnp.sum(gsum[6 * SMALL_TILE]) / D_MODEL
    conv_cols = lax.dynamic_slice(gsum[5 * SMALL_TILE:6 * SMALL_TILE], (0, me * (D_MODEL // N_DEV)),
                                  (SMALL_TILE, D_MODEL // N_DEV))
    g_small = jnp.concatenate([gsum[:5 * SMALL_TILE], _pad_tile(conv_cols)], axis=0)
    w_small = _pack_small(g_mix, g_mem, g_mlp, q_norm_g, k_norm_g, conv_w[0])
    m_small = _pack_small(m_g_mix, m_g_mem, m_g_mlp, m_q_norm_g, m_k_norm_g, m_conv_w[0])
    v_small = _pack_small(v_g_mix, v_g_mem, v_g_mlp, v_q_norm_g, v_k_norm_g, v_conv_w[0])
    d_small, nm_small, nv_small = _adamw_small(w_small, g_small, m_small, v_small)

    def unpack(p):
        return {"g_mix": p[0:1], "g_mem": p[8:9], "g_mlp": p[16:17], "q_norm_g": p[24:25, :X_HEAD_DIM],
                "k_norm_g": p[32:33, :X_HEAD_DIM], "conv_w": p[40:43, :D_MODEL // N_DEV][None]}

    small = [unpack(p) for p in (g_small, d_small, nm_small, nv_small)]
    order = ["g_mix", "g_mem", "w_in", "conv_w", "w_conv_out", "w_sb_out", "q_norm_g", "k_norm_g", "w_mem_kv",
             "w_x_out", "w_out", "g_mlp", "w_up", "w_down"]
    outs = [loss, grad_x[None]]
    for kind in range(4):
        for n in order:
            outs.append(res[n][kind][None] if n in res else small[kind][n])
    return tuple(outs)
```

```python
import jax
import jax.numpy as jnp
from jax import lax
from jax.experimental import pallas as pl
from jax.experimental.pallas import tpu as pltpu

F32 = jnp.float32
BF16 = jnp.bfloat16
MESH = pl.DeviceIdType.MESH

EPS = 1e-6
N_DEV = 8
D_MODEL = 1024
SB_HEAD_DIM = 64
SB_BLOCK = 128
SB_QUERY_TILE = 512
X_HEADS = 4
X_HEAD_DIM = 256
N_BRANCH = 3
COL_CH, COL_CB, COL_CC, COL_SQ, COL_SK, COL_SV, COL_XQ, COL_GATE = 0, 1, 2, 3, 4, 5, 6, 7

ADAM_LR = 0.001
ADAM_B1 = 0.9
ADAM_B2 = 0.999
ADAM_EPS = 1e-08
ADAM_WD = 0.01
ADAM_STEP = 10

SMALL_TILE = 8


def _dot(a, b):
    return jnp.dot(a, b, preferred_element_type=F32)


def _dot_nt(a, b):
    return lax.dot_general(a, b, (((1,), (1,)), ((), ())), preferred_element_type=F32)


def _dot_tn(a, b):
    return lax.dot_general(a, b, (((0,), (0,)), ((), ())), preferred_element_type=F32)


def _rstd(xf):
    return lax.rsqrt(jnp.mean(xf * xf, axis=-1, keepdims=True) + EPS)


def _sigmoid(z):
    return 1.0 / (1.0 + jnp.exp(-z))


def _log_sigmoid(z):
    return jnp.minimum(z, 0.0) - jnp.log(1.0 + jnp.exp(-jnp.abs(z)))


def _rms_bwd(dy, xhat, r, g):
    dxhat = dy * g
    return r * (dxhat - xhat * jnp.mean(dxhat * xhat, axis=-1, keepdims=True))


def _mesh_pos():
    return lax.axis_index("x"), lax.axis_index("y"), lax.axis_index("c")


ANY = pl.BlockSpec(memory_space=pl.ANY)


def _gather_shapes(shards):
    return [jax.ShapeDtypeStruct((N_DEV,) + s.shape, s.dtype) for s in shards]


def _gather_sems(n):
    return [pltpu.SemaphoreType.DMA((n, 7)), pltpu.SemaphoreType.DMA((n, 7)), pltpu.SemaphoreType.DMA((n,))]


def _gather_phases(ins, outs, send_sems, recv_sems, local_sems, by_arrival=False):
    n = len(ins)
    x, y, c = _mesh_pos()
    me, sibling = (x, y, c), (x, y, 1 - c)
    chips = [(1 - x, y), (x, 1 - y), (1 - x, 1 - y)]

    def blk(a, px, py, pc):
        return outs[a].at[4 * px + 2 * py + pc]

    def copy(a, k, block, to, src=None):
        return pltpu.make_async_remote_copy(
            src_ref=blk(a, *block) if src is None else src, dst_ref=blk(a, *block),
            send_sem=send_sems.at[a, k], recv_sem=recv_sems.at[a, k], device_id=to, device_id_type=MESH)

    def local(a):
        return pltpu.make_async_copy(ins[a], blk(a, *me), local_sems.at[a])

    def own(a):
        return [copy(a, 0, me, sibling, src=ins[a])] + [copy(a, 1 + j, me, (*chip, c), src=ins[a])
                                                        for j, chip in enumerate(chips)]

    def begin():
        for a in range(n):
            local(a).start()
        for a in range(n):
            for cp in own(a):
                cp.start()

    def arrive(order):
        for a in range(n):
            if order == 0:
                copy(a, 0, sibling, me).wait_recv()
            elif order <= 3:
                chip = chips[order - 1]
                copy(a, order, (*chip, c), me).wait_recv()
                copy(a, 3 + order, (*chip, c), sibling).start()
            else:
                copy(a, order, (*chips[order - 4], 1 - c), me).wait_recv()

    def relay():
        for order in (1, 2, 3):
            arrive(order)

    def drain():
        for a in range(n):
            for cp in own(a):
                cp.wait_send()
            for j, chip in enumerate(chips):
                copy(a, 4 + j, (*chip, c), sibling).wait_send()
            local(a).wait()

    def finish():
        for order in (0, 4, 5, 6):
            arrive(order)
        drain()

    if by_arrival:
        return begin, arrive, drain
    return begin, relay, finish


def _pair_exchange(grads, name):
    n = len(grads)

    def body(*refs):
        ins, outs = refs[:n], refs[n:2 * n]
        send_sems, recv_sems = refs[2 * n:]
        x, y, c = _mesh_pos()
        xs, ys = (x, 1 - x), (y, 1 - y)
        copies = []
        for a in range(n):
            for r in range(4):
                dx, dy = divmod(r, 2)
                copies.append(pltpu.make_async_remote_copy(
                    src_ref=ins[a].at[4 * xs[dx] + 2 * ys[dy] + (1 - c)], dst_ref=outs[a].at[r],
                    send_sem=send_sems.at[a, r], recv_sem=recv_sems.at[a, r],
                    device_id=(x, y, 1 - c), device_id_type=MESH))
        for cp in copies:
            cp.start()
        for cp in copies:
            cp.wait()

    return pl.pallas_call(
        body, name=name,
        out_shape=[jax.ShapeDtypeStruct((4,) + g.shape[1:], g.dtype) for g in grads],
        in_specs=[ANY] * n, out_specs=[ANY] * n,
        scratch_shapes=[pltpu.SemaphoreType.DMA((n, 4)), pltpu.SemaphoreType.DMA((n, 4))],
    )(*grads)


def _chip_exchange_shapes(sums):
    return [jax.ShapeDtypeStruct((3,) + s.shape[1:], s.dtype) for s in sums]


def _chip_exchange_sems(n):
    return [pltpu.SemaphoreType.DMA((n, 3)), pltpu.SemaphoreType.DMA((n, 3))]


def _chip_exchange_phases(ins, outs, send_sems, recv_sems):
    x, y, c = _mesh_pos()
    xs, ys = (x, 1 - x), (y, 1 - y)

    def copies():
        out = []
        for a in range(len(ins)):
            for r in range(1, 4):
                dx, dy = divmod(r, 2)
                out.append(pltpu.make_async_remote_copy(
                    src_ref=ins[a].at[r], dst_ref=outs[a].at[r - 1],
                    send_sem=send_sems.at[a, r - 1], recv_sem=recv_sems.at[a, r - 1],
                    device_id=(xs[dx], ys[dy], c), device_id_type=MESH))
        return out

    def begin():
        for cp in copies():
            cp.start()

    def finish():
        for cp in copies():
            cp.wait()

    return begin, finish


def _small_all_gather(part):
    rows, cols = part.shape

    def body(in_ref, out_ref, send_sems, recv_sems):
        x, y, c = _mesh_pos()
        xs, ys, cs = (x, 1 - x), (y, 1 - y), (c, 1 - c)
        out_ref[4 * x + 2 * y + c] = in_ref[...]
        copies = []
        for k in range(1, N_DEV):
            dx, dy, dc = k // 4, (k // 2) % 2, k % 2
            copies.append((
                pltpu.make_async_remote_copy(
                    src_ref=in_ref, dst_ref=out_ref.at[4 * x + 2 * y + c],
                    send_sem=send_sems.at[k - 1], recv_sem=recv_sems.at[k - 1],
                    device_id=(xs[dx], ys[dy], cs[dc]), device_id_type=MESH),
                pltpu.make_async_remote_copy(
                    src_ref=in_ref, dst_ref=out_ref.at[4 * xs[dx] + 2 * ys[dy] + cs[dc]],
                    send_sem=send_sems.at[k - 1], recv_sem=recv_sems.at[k - 1],
                    device_id=(xs[dx], ys[dy], cs[dc]), device_id_type=MESH)))
        for send, _ in copies:
            send.start()
        for send, recv in copies:
            recv.wait_recv()
            send.wait_send()

    return pl.pallas_call(
        body, name="small_all_gather",
        out_shape=jax.ShapeDtypeStruct((N_DEV, rows, cols), part.dtype),
        in_specs=[pl.BlockSpec(memory_space=pltpu.VMEM)],
        out_specs=pl.BlockSpec(memory_space=pltpu.VMEM),
        scratch_shapes=[pltpu.SemaphoreType.DMA((N_DEV - 1,)), pltpu.SemaphoreType.DMA((N_DEV - 1,))],
    )(part)


def _in_proj(x, g_mix, arrival_blocks, tm, shards):
    t, d = x.shape
    bw = shards[0].shape[1]
    nt = t // tm
    n = len(shards)

    def body(blocks_ref, x_ref, g_ref, *rest):
        w_shard = rest[0]
        proj_ref, h_ref = rest[n:n + 2]
        w_all = rest[n + 2]
        h_scr, w_buf, fetch_sems = rest[2 * n + 2:2 * n + 5]
        begin, arrive, drain = _gather_phases(rest[:n], rest[n + 2:2 * n + 2], *rest[2 * n + 5:], by_arrival=True)
        j, i = pl.program_id(0), pl.program_id(1)
        slot = lax.rem(j, 2)

        def fetch(src, into):
            return pltpu.make_async_copy(src, w_buf.at[into], fetch_sems.at[into])

        @pl.when((j == 0) & (i == 0))
        def _():
            begin()
            fetch(w_shard, 0).start()

        @pl.when(j == 0)
        def _():
            xf = x_ref[...]
            hv = (xf * _rstd(xf) * g_ref[...]).astype(BF16)
            h_ref[...] = hv
            h_scr[pl.ds(pl.multiple_of(i * tm, tm), tm), :] = hv

        @pl.when(i == 0)
        def _():
            fetch(w_shard, slot).wait()

        proj_ref[...] = _dot(h_scr[pl.ds(pl.multiple_of(i * tm, tm), tm), :], w_buf[slot]).astype(BF16)

        for nxt in range(1, N_DEV):
            @pl.when((i == nt - 1) & (j == nxt - 1))
            def _():
                arrive(nxt - 1)
                fetch(w_all.at[blocks_ref[nxt]], 1 - slot).start()

        pl.when((i == nt - 1) & (j == N_DEV - 1))(drain)

    first_pass = lambda j, i, blocks: (jnp.where(j == 0, i, nt - 1), 0)
    outs = pl.pallas_call(
        body, name="in_proj",
        grid_spec=pltpu.PrefetchScalarGridSpec(
            num_scalar_prefetch=1, grid=(N_DEV, nt),
            in_specs=[pl.BlockSpec((tm, d), first_pass), pl.BlockSpec((1, d), lambda j, i, blocks: (0, 0))] + [ANY] * n,
            out_specs=[pl.BlockSpec((tm, bw), lambda j, i, blocks: (i, blocks[j])),
                       pl.BlockSpec((tm, d), first_pass)] + [ANY] * n,
            scratch_shapes=[pltpu.VMEM((t, d), BF16), pltpu.VMEM((2, d, bw), BF16), pltpu.SemaphoreType.DMA((2,))]
            + _gather_sems(n)),
        out_shape=[jax.ShapeDtypeStruct((t, N_DEV * bw), BF16), jax.ShapeDtypeStruct((t, d), BF16)] + _gather_shapes(shards),
        compiler_params=pltpu.CompilerParams(dimension_semantics=("arbitrary", "arbitrary")),
    )(arrival_blocks, x, g_mix, *shards)
    return outs[0], outs[1], outs[2:]


def _conv_terms(ch_ref, cb_ref, cc_ref, w_ref):
    ch, cb, cc = ch_ref[...].astype(F32), cb_ref[...].astype(F32), cc_ref[...].astype(F32)
    u = cc * ch
    row = lax.broadcasted_iota(jnp.int32, u.shape, 0)
    u1 = jnp.where(row >= 1, pltpu.roll(u, 1, 0), 0.0)
    u2 = jnp.where(row >= 2, pltpu.roll(u, 2, 0), 0.0)
    w = (w_ref[0:1, :], w_ref[1:2, :], w_ref[2:3, :])
    cv = w[2] * u + w[1] * u1 + w[0] * u2
    return ch, cb, cc, u, u1, u2, cv, w, row


def _conv_fwd(proj, conv_w, cw):
    t = proj.shape[0]
    nper = D_MODEL // cw

    def body(ch_ref, cb_ref, cc_ref, w_ref, a_ref):
        _, cb, _, _, _, _, cv, _, _ = _conv_terms(ch_ref, cb_ref, cc_ref, w_ref)
        a_ref[...] = (cb * cv).astype(BF16)

    def col(piece):
        return pl.BlockSpec((t, cw), lambda j: (0, piece * nper + j))

    return pl.pallas_call(
        body, name="conv_fwd",
        grid=(nper,),
        in_specs=[col(COL_CH), col(COL_CB), col(COL_CC), pl.BlockSpec((3, cw), lambda j: (0, j))],
        out_specs=pl.BlockSpec((t, cw), lambda j: (0, j)),
        out_shape=jax.ShapeDtypeStruct((t, D_MODEL), BF16),
        compiler_params=pltpu.CompilerParams(dimension_semantics=("parallel",)),
    )(proj, proj, proj, conv_w)


def _scan_matrix():
    s = lax.broadcasted_iota(jnp.int32, (SB_BLOCK, SB_BLOCK), 0)
    j = lax.broadcasted_iota(jnp.int32, (SB_BLOCK, SB_BLOCK), 1)
    return jnp.where(j > s, 1.0, 0.0).astype(BF16)


def _suffix_sum(u_mat, xv):
    hi = xv.astype(BF16)
    lo = (xv - hi.astype(F32)).astype(BF16)
    return _dot(u_mat, hi) + _dot(u_mat, lo)


def _head_rows(vt, h):
    row = lax.broadcasted_iota(jnp.int32, vt.shape, 0)
    return jnp.where((row >= h * SB_HEAD_DIM) & (row < (h + 1) * SB_HEAD_DIM), vt, 0.0).astype(BF16)


def _head_lanes(v, h):
    lane = lax.broadcasted_iota(jnp.int32, v.shape, 1)
    return jnp.where((lane >= h * SB_HEAD_DIM) & (lane < (h + 1) * SB_HEAD_DIM), v, 0.0).astype(BF16)


def _group_suffix(u_mat, xv, carry):
    nblk = xv.shape[0] // SB_BLOCK
    parts = [None] * nblk
    for j in reversed(range(nblk)):
        xj = xv[j * SB_BLOCK:(j + 1) * SB_BLOCK]
        parts[j] = _suffix_sum(u_mat, xj) + carry
        carry = carry + jnp.sum(xj, axis=0, keepdims=True)
    return jnp.concatenate(parts, axis=0), carry


def _sb_probs(kgrp, qt_h, u_mat, carry, past):
    z = _dot(kgrp, qt_h)
    lb = _log_sigmoid(z)
    l1 = lb - z
    if past is not None:
        l1 = jnp.where(past, l1, 0.0)
    between, carry = _group_suffix(u_mat, l1, carry)
    a = jnp.exp(lb + between)
    if past is not None:
        a = jnp.where(past, a, 0.0)
    return a, z, carry


def _sb_schedule(nq):
    steps = [(qi, g) for qi in range(nq) for g in range(qi, -1, -1)]
    return jnp.asarray([s[0] for s in steps], jnp.int32), jnp.asarray([s[1] for s in steps], jnp.int32)


def _past_mask(tq):
    return lax.broadcasted_iota(jnp.int32, (tq, tq), 0) < lax.broadcasted_iota(jnp.int32, (tq, tq), 1)


def _sb_fwd(proj, vt4, tq, shards):
    t = proj.shape[0]
    pairs = D_MODEL // SB_BLOCK
    nq = t // tq
    qi_tab, g_tab = _sb_schedule(nq)
    ns = qi_tab.shape[0]
    n = len(shards)

    def body(qi_ref, g_ref, q_ref, k_ref, vt_ref, *rest):
        o_ref, a_ref, z_ref = rest[n:n + 3]
        acc_ref, carry_ref, qt_ref = rest[2 * n + 3:2 * n + 6]
        begin, relay, finish = _gather_phases(rest[:n], rest[n + 3:2 * n + 3], *rest[2 * n + 6:])
        pi, si = pl.program_id(0), pl.program_id(1)
        diagonal = g_ref[si] == qi_ref[si]
        pl.when((pi == 0) & (si == 0))(begin)
        pl.when((pi == (7 * pairs) // 8) & (si == 0))(relay)

        @pl.when(diagonal)
        def _():
            acc_ref[...] = jnp.zeros_like(acc_ref)
            carry_ref[...] = jnp.zeros_like(carry_ref)
            qt = q_ref[...].astype(F32).T * (SB_HEAD_DIM ** -0.5)
            for h in range(2):
                qt_ref[h] = _head_rows(qt, h)

        def step(past):
            u_mat = _scan_matrix()
            for h in range(2):
                a, z, carry = _sb_probs(k_ref[...], qt_ref[h], u_mat, carry_ref[h], past)
                ab = a.astype(BF16)
                a_ref[h] = ab
                z_ref[h] = z.astype(BF16)
                acc_ref[h] += _dot(vt_ref[h * SB_HEAD_DIM:(h + 1) * SB_HEAD_DIM, :], ab)
                carry_ref[h] = carry

        pl.when(diagonal)(lambda: step(_past_mask(tq)))
        pl.when(jnp.logical_not(diagonal))(lambda: step(None))

        @pl.when(g_ref[si] == 0)
        def _():
            o_ref[...] = jnp.concatenate([acc_ref[0], acc_ref[1]], axis=0).T

        pl.when((pi == pairs - 1) & (si == ns - 1))(finish)

    tile = pl.BlockSpec((None, None, 2, tq, tq), lambda p, s, qt_, gt_: (p, s, 0, 0, 0))
    tiles = jax.ShapeDtypeStruct((pairs, ns, 2, tq, tq), BF16)
    outs = pl.pallas_call(
        body, name="sb_fwd",
        grid_spec=pltpu.PrefetchScalarGridSpec(
            num_scalar_prefetch=2, grid=(pairs, ns),
            in_specs=[pl.BlockSpec((tq, SB_BLOCK), lambda p, s, qt_, gt_: (qt_[s], COL_SQ * pairs + p)),
                      pl.BlockSpec((tq, SB_BLOCK), lambda p, s, qt_, gt_: (gt_[s], COL_SK * pairs + p)),
                      pl.BlockSpec((None, None, SB_BLOCK, tq), lambda p, s, qt_, gt_: (p, gt_[s], 0, 0))] + [ANY] * n,
            out_specs=[pl.BlockSpec((tq, SB_BLOCK), lambda p, s, qt_, gt_: (qt_[s], p)), tile, tile] + [ANY] * n,
            scratch_shapes=[pltpu.VMEM((2, SB_HEAD_DIM, tq), F32), pltpu.VMEM((2, 1, tq), F32),
                            pltpu.VMEM((2, SB_BLOCK, tq), BF16)] + _gather_sems(n)),
        out_shape=[jax.ShapeDtypeStruct((t, D_MODEL), F32), tiles, tiles] + _gather_shapes(shards),
        compiler_params=pltpu.CompilerParams(dimension_semantics=("arbitrary", "arbitrary")),
    )(qi_tab, g_tab, proj, proj, vt4, *shards)
    return outs[0], outs[1], outs[2], outs[3:]


def _mem_prep(mem, g_mem, wkv_all, k_norm_g):
    m, d = mem.shape

    def body(mem_ref, g_ref, w_ref, kg_ref, memn_ref, kn_ref, v_ref):
        memf = mem_ref[...]
        memn = (memf * _rstd(memf) * g_ref[...]).astype(BF16)
        memn_ref[...] = memn
        for b in range(N_DEV):
            kv = _dot(memn, w_ref[b])
            if b < X_HEADS:
                kn_ref[:, b * X_HEAD_DIM:(b + 1) * X_HEAD_DIM] = (kv * _rstd(kv) * kg_ref[...]).astype(BF16)
            else:
                h = b - X_HEADS
                v_ref[:, h * X_HEAD_DIM:(h + 1) * X_HEAD_DIM] = kv.astype(BF16)

    return pl.pallas_call(
        body, name="mem_prep",
        out_shape=[jax.ShapeDtypeStruct((m, d), BF16)] * 3,
    )(mem, g_mem, wkv_all, k_norm_g)


def _x_head(xq_ref, qg, kn_ref, h):
    sl = slice(h * X_HEAD_DIM, (h + 1) * X_HEAD_DIM)
    q = xq_ref[:, sl].astype(F32)
    rq = _rstd(q)
    qhat = q * rq
    qn = (qhat * qg).astype(BF16)
    s = _dot_nt(qn, kn_ref[:, sl]) * (X_HEAD_DIM ** -0.5)
    e = jnp.exp(s - jnp.max(s, axis=-1, keepdims=True))
    p = e / jnp.sum(e, axis=-1, keepdims=True)
    return sl, rq, qhat, qn, p


def _x_fwd(proj, q_norm_g, kn, v, tm):
    t = proj.shape[0]
    m = kn.shape[0]

    def body(xq_ref, qg_ref, kn_ref, v_ref, o_ref):
        for h in range(X_HEADS):
            sl, _, _, _, p = _x_head(xq_ref, qg_ref[...], kn_ref, h)
            o_ref[:, sl] = _dot(p.astype(BF16), v_ref[:, sl]).astype(BF16)

    return pl.pallas_call(
        body, name="x_fwd",
        grid=(t // tm,),
        in_specs=[pl.BlockSpec((tm, D_MODEL), lambda i: (i, COL_XQ)),
                  pl.BlockSpec((1, X_HEAD_DIM), lambda i: (0, 0)),
                  pl.BlockSpec((m, D_MODEL), lambda i: (0, 0)),
                  pl.BlockSpec((m, D_MODEL), lambda i: (0, 0))],
        out_specs=pl.BlockSpec((tm, D_MODEL), lambda i: (i, 0)),
        out_shape=jax.ShapeDtypeStruct((t, D_MODEL), BF16),
        compiler_params=pltpu.CompilerParams(dimension_semantics=("parallel",)),
    )(proj, q_norm_g, kn, v)


def _gate_spec(tm, branch):
    return pl.BlockSpec((tm, D_MODEL), lambda i: (i, COL_GATE + branch))


def _merge_fwd(x, proj, a_conv, o_sb, o_x, w_conv_out, w_sb_out, w_x_out, w_out, tm):
    t, d = x.shape

    def body(x_ref, g0_ref, g1_ref, g2_ref, a_ref, s_ref, xo_ref, wc_ref, ws_ref, wx_ref, wo_ref,
             x1_ref, yc_ref, ys_ref, yx_ref, mg_ref):
        merged = jnp.zeros((tm, d), F32)
        for gate_ref, b_ref, w_ref, y_ref in ((g0_ref, a_ref, wc_ref, yc_ref), (g1_ref, s_ref, ws_ref, ys_ref),
                                              (g2_ref, xo_ref, wx_ref, yx_ref)):
            yv = _dot(b_ref[...].astype(BF16), w_ref[...])
            y_ref[...] = yv.astype(BF16)
            merged = merged + _sigmoid(gate_ref[...].astype(F32)) * yv
        mb = merged.astype(BF16)
        mg_ref[...] = mb
        x1_ref[...] = x_ref[...] + _dot(mb, wo_ref[...])

    tile = pl.BlockSpec((tm, d), lambda i: (i, 0))
    wfull = pl.BlockSpec((d, d), lambda i: (0, 0))
    return pl.pallas_call(
        body, name="merge_fwd",
        grid=(t // tm,),
        in_specs=[tile] + [_gate_spec(tm, b) for b in range(N_BRANCH)] + [tile, tile, tile,
                                                                           wfull, wfull, wfull, wfull],
        out_specs=[tile] * 5,
        out_shape=[jax.ShapeDtypeStruct((t, d), F32)] + [jax.ShapeDtypeStruct((t, d), BF16)] * 4,
        compiler_params=pltpu.CompilerParams(dimension_semantics=("parallel",)),
    )(x, proj, proj, proj, a_conv, o_sb, o_x, w_conv_out, w_sb_out, w_x_out, w_out)


def _mlp_fwd(x1, g_mlp, w_up_all, w_down, target, tm):
    t, d = x1.shape
    nb, _, fw = w_up_all.shape

    def body(x1_ref, g_ref, wu_ref, wd_ref, tgt_ref, up_ref, h2_ref, dx2_ref, lsum_ref, acc_ref):
        i, j = pl.program_id(0), pl.program_id(1)

        @pl.when(j == 0)
        def _():
            xf = x1_ref[...]
            h2_ref[...] = (xf * _rstd(xf) * g_ref[...]).astype(BF16)
            acc_ref[...] = jnp.zeros_like(acc_ref)

        @pl.when((i == 0) & (j == 0))
        def _():
            lsum_ref[...] = jnp.zeros_like(lsum_ref)

        up = _dot(h2_ref[...], wu_ref[...])
        up_ref[...] = up.astype(BF16)
        act = jnp.square(jnp.maximum(up, 0.0)).astype(BF16)
        acc_ref[...] += _dot(act, wd_ref[...])

        @pl.when(j == nb - 1)
        def _():
            diff = x1_ref[...] + acc_ref[...] - tgt_ref[...]
            dx2_ref[...] = diff * (1.0 / d)
            lsum_ref[...] += jnp.sum(diff * diff, axis=0, keepdims=True)

    tile = pl.BlockSpec((tm, d), lambda i, j: (i, 0))
    row = pl.BlockSpec((1, d), lambda i, j: (0, 0))
    return pl.pallas_call(
        body, name="mlp_fwd",
        grid=(t // tm, nb),
        in_specs=[tile, row, pl.BlockSpec((None, d, fw), lambda i, j: (j, 0, 0)),
                  pl.BlockSpec((fw, d), lambda i, j: (j, 0)), tile],
        out_specs=[pl.BlockSpec((tm, fw), lambda i, j: (i, j)), tile, tile, row],
        out_shape=[jax.ShapeDtypeStruct((t, nb * fw), BF16), jax.ShapeDtypeStruct((t, d), BF16),
                   jax.ShapeDtypeStruct((t, d), F32), jax.ShapeDtypeStruct((1, d), F32)],
        scratch_shapes=[pltpu.VMEM((tm, d), F32)],
        compiler_params=pltpu.CompilerParams(dimension_semantics=("arbitrary", "arbitrary")),
    )(x1, g_mlp, w_up_all, w_down, target)


def _mlp_bwd(x1, g_mlp, w_up_all, w_down, up, dx2, tm):
    t, d = x1.shape
    nb, _, fw = w_up_all.shape

    def body(x1_ref, g_ref, wu_ref, wd_ref, up_ref, dx2_ref, dup_ref, act_ref, dx1_ref, dg_ref, acc_ref, dyb_ref):
        i, j = pl.program_id(0), pl.program_id(1)

        @pl.when(j == 0)
        def _():
            dyb_ref[...] = dx2_ref[...].astype(BF16)
            acc_ref[...] = jnp.zeros_like(acc_ref)

        @pl.when((i == 0) & (j == 0))
        def _():
            dg_ref[...] = jnp.zeros_like(dg_ref)

        r = jnp.maximum(up_ref[...].astype(F32), 0.0)
        act_ref[...] = jnp.square(r).astype(BF16)
        dup = (_dot_nt(dyb_ref[...], wd_ref[...]) * (2.0 * r)).astype(BF16)
        dup_ref[...] = dup
        acc_ref[...] += _dot_nt(dup, wu_ref[...])

        @pl.when(j == nb - 1)
        def _():
            xf = x1_ref[...]
            rs = _rstd(xf)
            xhat = xf * rs
            dh2 = acc_ref[...]
            dg_ref[...] += jnp.sum(dh2 * xhat, axis=0, keepdims=True)
            dx1_ref[...] = dx2_ref[...] + _rms_bwd(dh2, xhat, rs, g_ref[...])

    tile = pl.BlockSpec((tm, d), lambda i, j: (i, 0))
    row = pl.BlockSpec((1, d), lambda i, j: (0, 0))
    ff = pl.BlockSpec((tm, fw), lambda i, j: (i, j))
    return pl.pallas_call(
        body, name="mlp_bwd",
        grid=(t // tm, nb),
        in_specs=[tile, row, pl.BlockSpec((None, d, fw), lambda i, j: (j, 0, 0)),
                  pl.BlockSpec((fw, d), lambda i, j: (j, 0)), ff, tile],
        out_specs=[ff, ff, tile, row],
        out_shape=[jax.ShapeDtypeStruct((t, nb * fw), BF16), jax.ShapeDtypeStruct((t, nb * fw), BF16),
                   jax.ShapeDtypeStruct((t, d), F32), jax.ShapeDtypeStruct((1, d), F32)],
        scratch_shapes=[pltpu.VMEM((tm, d), F32), pltpu.VMEM((tm, d), BF16)],
        compiler_params=pltpu.CompilerParams(dimension_semantics=("arbitrary", "arbitrary")),
    )(x1, g_mlp, w_up_all, w_down, up, dx2)


def _merge_bwd(dx1, proj, y_conv, y_sb, y_x, w_conv_out, w_sb_out, w_x_out, w_out, tm):
    t, d = dx1.shape

    def body(dx1_ref, g0_ref, g1_ref, g2_ref, yc_ref, ys_ref, yx_ref, wc_ref, ws_ref, wx_ref, wo_ref,
             dgate_ref, dyc_ref, dys_ref, dyx_ref, da_ref, dos_ref, dox_ref):
        dm = _dot_nt(dx1_ref[...].astype(BF16), wo_ref[...])
        for i, (gate_ref, y_ref, w_ref, dy_ref, db_ref) in enumerate(((g0_ref, yc_ref, wc_ref, dyc_ref, da_ref),
                                                                       (g1_ref, ys_ref, ws_ref, dys_ref, dos_ref),
                                                                       (g2_ref, yx_ref, wx_ref, dyx_ref, dox_ref))):
            gt = _sigmoid(gate_ref[...].astype(F32))
            dy = (dm * gt).astype(BF16)
            dy_ref[...] = dy
            dgate_ref[:, i * d:(i + 1) * d] = (dm * y_ref[...].astype(F32) * gt * (1.0 - gt)).astype(BF16)
            db_ref[...] = _dot_nt(dy, w_ref[...]).astype(BF16)

    tile = pl.BlockSpec((tm, d), lambda i: (i, 0))
    wfull = pl.BlockSpec((d, d), lambda i: (0, 0))
    return pl.pallas_call(
        body, name="merge_bwd",
        grid=(t // tm,),
        in_specs=[tile] + [_gate_spec(tm, b) for b in range(N_BRANCH)] + [tile, tile, tile,
                                                                           wfull, wfull, wfull, wfull],
        out_specs=[pl.BlockSpec((tm, N_BRANCH * d), lambda i: (i, 0))] + [tile] * 6,
        out_shape=[jax.ShapeDtypeStruct((t, N_BRANCH * d), BF16)] + [jax.ShapeDtypeStruct((t, d), BF16)] * 6,
        compiler_params=pltpu.CompilerParams(dimension_semantics=("parallel",)),
    )(dx1, proj, proj, proj, y_conv, y_sb, y_x, w_conv_out, w_sb_out, w_x_out, w_out)


def _conv_bwd(proj, conv_w, da, cw):
    t = proj.shape[0]
    nper = D_MODEL // cw

    def body(ch_ref, cb_ref, cc_ref, w_ref, da_ref, dch_ref, dcb_ref, dcc_ref, dw_ref):
        ch, cb, cc, u, u1, u2, cv, w, row = _conv_terms(ch_ref, cb_ref, cc_ref, w_ref)
        dav = da_ref[...].astype(F32)
        dcb_ref[...] = (dav * cv).astype(BF16)
        dcv = dav * cb
        n1 = jnp.where(row < t - 1, pltpu.roll(dcv, t - 1, 0), 0.0)
        n2 = jnp.where(row < t - 2, pltpu.roll(dcv, t - 2, 0), 0.0)
        du = w[2] * dcv + w[1] * n1 + w[0] * n2
        dcc_ref[...] = (du * ch).astype(BF16)
        dch_ref[...] = (du * cc).astype(BF16)
        dw_ref[0:1, :] = jnp.sum(dcv * u2, axis=0, keepdims=True)
        dw_ref[1:2, :] = jnp.sum(dcv * u1, axis=0, keepdims=True)
        dw_ref[2:3, :] = jnp.sum(dcv * u, axis=0, keepdims=True)

    def col(piece):
        return pl.BlockSpec((t, cw), lambda j: (0, piece * nper + j))

    out_col = pl.BlockSpec((t, cw), lambda j: (0, j))
    wspec = pl.BlockSpec((3, cw), lambda j: (0, j))
    return pl.pallas_call(
        body, name="conv_bwd",
        grid=(nper,),
        in_specs=[col(COL_CH), col(COL_CB), col(COL_CC), wspec, out_col],
        out_specs=[out_col, out_col, out_col, wspec],
        out_shape=[jax.ShapeDtypeStruct((t, D_MODEL), BF16)] * 3 + [jax.ShapeDtypeStruct((3, D_MODEL), F32)],
        compiler_params=pltpu.CompilerParams(dimension_semantics=("parallel",)),
    )(proj, proj, proj, conv_w, da)


def _sb_bwd(proj, kt4, do_sb, o_sb, weights, logits, tq, pair_sums):
    t = proj.shape[0]
    nq = t // tq
    pairs = D_MODEL // SB_BLOCK
    scale = SB_HEAD_DIM ** -0.5
    qi_tab, g_tab = _sb_schedule(nq)
    ns = qi_tab.shape[0]
    n = len(pair_sums)

    def body(qi_ref, g_ref, q_ref, v_ref, kt_ref, do_ref, o_ref, a_ref, z_ref, *rest):
        dq_ref, dk_ref, dv_ref = rest[n:n + 3]
        dk_acc, dv_acc, dqt_ref, carry_ref, qm_ref, dom_ref, dot_ref, dsum_ref = rest[2 * n + 3:2 * n + 11]
        begin, finish = _chip_exchange_phases(rest[:n], rest[n + 3:2 * n + 3], *rest[2 * n + 11:])
        pi, si = pl.program_id(0), pl.program_id(1)
        diagonal = g_ref[si] == qi_ref[si]
        pl.when((pi == 0) & (si == 0))(begin)

        @pl.when(si == 0)
        def _():
            dk_acc[...] = jnp.zeros_like(dk_acc)
            dv_acc[...] = jnp.zeros_like(dv_acc)

        @pl.when(diagonal)
        def _():
            dqt_ref[...] = jnp.zeros_like(dqt_ref)
            carry_ref[...] = jnp.zeros_like(carry_ref)
            q = q_ref[...].astype(F32) * scale
            do = do_ref[...].astype(F32)
            dot_ = do.T
            prod = dot_ * o_ref[...].T
            for h in range(2):
                rows = slice(h * SB_HEAD_DIM, (h + 1) * SB_HEAD_DIM)
                qm_ref[h] = _head_lanes(q, h)
                dom_ref[h] = _head_lanes(do, h)
                dot_ref[h] = _head_rows(dot_, h)
                dsum_ref[h] = jnp.sum(prod[rows, :], axis=0, keepdims=True)

        def step(past):
            u_mat = _scan_matrix()
            ks = pl.multiple_of(g_ref[si] * tq, tq)
            dk_add = jnp.zeros((tq, SB_BLOCK), F32)
            dv_add = jnp.zeros((tq, SB_BLOCK), F32)
            for h in range(2):
                rows = slice(h * SB_HEAD_DIM, (h + 1) * SB_HEAD_DIM)
                ab = a_ref[h]
                gw = _dot(v_ref[...], dot_ref[h]) * ab.astype(F32)
                after, carry = _group_suffix(u_mat, gw, carry_ref[h])
                sig = pl.reciprocal(1.0 + jnp.exp(-z_ref[h].astype(F32)), approx=True)
                dz = gw - sig * (dsum_ref[h] - after)
                if past is not None:
                    dz = jnp.where(past, dz, 0.0)
                dzb = dz.astype(BF16)
                dqt_ref[h] += _dot(kt_ref[rows, :], dzb)
                dk_add = dk_add + _dot(dzb, qm_ref[h])
                dv_add = dv_add + _dot(ab, dom_ref[h])
                carry_ref[h] = carry
            dk_acc[pl.ds(ks, tq), :] += dk_add
            dv_acc[pl.ds(ks, tq), :] += dv_add

        pl.when(diagonal)(lambda: step(_past_mask(tq)))
        pl.when(jnp.logical_not(diagonal))(lambda: step(None))

        @pl.when(g_ref[si] == 0)
        def _():
            dq_ref[...] = (jnp.concatenate([dqt_ref[0], dqt_ref[1]], axis=0).T * scale).astype(BF16)

        @pl.when(si == ns - 1)
        def _():
            dk_ref[...] = dk_acc[...].astype(BF16)
            dv_ref[...] = dv_acc[...].astype(BF16)

        pl.when((pi == pairs - 1) & (si == ns - 1))(finish)

    qblk = lambda base: pl.BlockSpec((tq, SB_BLOCK), lambda p, s, qt_, gt_: (qt_[s], base * pairs + p))
    kgrp = lambda base: pl.BlockSpec((tq, SB_BLOCK), lambda p, s, qt_, gt_: (gt_[s], base * pairs + p))
    seq = pl.BlockSpec((t, SB_BLOCK), lambda p, s, qt_, gt_: (0, p))
    tr = pl.BlockSpec((None, None, SB_BLOCK, tq), lambda p, s, qt_, gt_: (p, gt_[s], 0, 0))
    tile = pl.BlockSpec((None, None, 2, tq, tq), lambda p, s, qt_, gt_: (p, s, 0, 0, 0))
    outs = pl.pallas_call(
        body, name="sb_bwd",
        grid_spec=pltpu.PrefetchScalarGridSpec(
            num_scalar_prefetch=2, grid=(pairs, ns),
            in_specs=[qblk(COL_SQ), kgrp(COL_SV), tr, qblk(0), qblk(0), tile, tile] + [ANY] * n,
            out_specs=[qblk(0), seq, seq] + [ANY] * n,
            scratch_shapes=[pltpu.VMEM((t, SB_BLOCK), F32), pltpu.VMEM((t, SB_BLOCK), F32),
                            pltpu.VMEM((2, SB_HEAD_DIM, tq), F32), pltpu.VMEM((2, 1, tq), F32),
                            pltpu.VMEM((2, tq, SB_BLOCK), BF16), pltpu.VMEM((2, tq, SB_BLOCK), BF16),
                            pltpu.VMEM((2, SB_BLOCK, tq), BF16), pltpu.VMEM((2, 1, tq), F32)] + _chip_exchange_sems(n)),
        out_shape=[jax.ShapeDtypeStruct((t, D_MODEL), BF16)] * 3 + _chip_exchange_shapes(pair_sums),
        compiler_params=pltpu.CompilerParams(dimension_semantics=("arbitrary", "arbitrary")),
    )(qi_tab, g_tab, proj, proj, kt4, do_sb, o_sb, weights, logits, *pair_sums)
    return outs[0], outs[1], outs[2], outs[3:]


def _x_bwd(proj, q_norm_g, kn, v, do_x, tm):
    t = proj.shape[0]
    m = kn.shape[0]
    scale = X_HEAD_DIM ** -0.5

    def body(xq_ref, qg_ref, kn_ref, v_ref, do_ref, dxq_ref, dkn_ref, dv_ref, dqg_ref):
        @pl.when(pl.program_id(0) == 0)
        def _():
            dkn_ref[...] = jnp.zeros_like(dkn_ref)
            dv_ref[...] = jnp.zeros_like(dv_ref)
            dqg_ref[...] = jnp.zeros_like(dqg_ref)

        qg = qg_ref[...]
        for h in range(X_HEADS):
            sl, rq, qhat, qn, p = _x_head(xq_ref, qg, kn_ref, h)
            do_h = do_ref[:, sl]
            dp = _dot_nt(do_h, v_ref[:, sl])
            ds = (p * (dp - jnp.sum(dp * p, axis=-1, keepdims=True)) * scale).astype(BF16)
            dqn = _dot(ds, kn_ref[:, sl])
            dkn_ref[:, sl] += _dot_tn(ds, qn)
            dv_ref[:, sl] += _dot_tn(p.astype(BF16), do_h)
            dqg_ref[...] += jnp.sum(dqn * qhat, axis=0, keepdims=True)
            dxq_ref[:, sl] = _rms_bwd(dqn, qhat, rq, qg).astype(BF16)

    full = pl.BlockSpec((m, D_MODEL), lambda i: (0, 0))
    gain = pl.BlockSpec((1, X_HEAD_DIM), lambda i: (0, 0))
    tile = pl.BlockSpec((tm, D_MODEL), lambda i: (i, 0))
    return pl.pallas_call(
        body, name="x_bwd",
        grid=(t // tm,),
        in_specs=[pl.BlockSpec((tm, D_MODEL), lambda i: (i, COL_XQ)), gain, full, full, tile],
        out_specs=[tile, full, full, gain],
        out_shape=[jax.ShapeDtypeStruct((t, D_MODEL), BF16), jax.ShapeDtypeStruct((m, D_MODEL), F32),
                   jax.ShapeDtypeStruct((m, D_MODEL), F32), jax.ShapeDtypeStruct((1, X_HEAD_DIM), F32)],
        compiler_params=pltpu.CompilerParams(dimension_semantics=("arbitrary",)),
    )(proj, q_norm_g, kn, v, do_x)


def _mem_bwd(mem, g_mem, wkv_all, k_norm_g, dkn, dv):
    m, d = mem.shape

    def body(mem_ref, g_ref, w_ref, kg_ref, dkn_ref, dv_ref, dkv_ref, dgm_ref, dkg_ref):
        memf = mem_ref[...]
        mem_hat = memf * _rstd(memf)
        memn = (mem_hat * g_ref[...]).astype(BF16)
        kg = kg_ref[...]
        dmemn = jnp.zeros((m, d), F32)
        dkg = jnp.zeros((1, X_HEAD_DIM), F32)
        for b in range(N_DEV):
            sl = slice(b * X_HEAD_DIM, (b + 1) * X_HEAD_DIM)
            if b < X_HEADS:
                kv = _dot(memn, w_ref[b])
                rk = _rstd(kv)
                khat = kv * rk
                dkn_h = dkn_ref[:, sl]
                dkg = dkg + jnp.sum(dkn_h * khat, axis=0, keepdims=True)
                dblk = _rms_bwd(dkn_h, khat, rk, kg).astype(BF16)
            else:
                hs = slice((b - X_HEADS) * X_HEAD_DIM, (b - X_HEADS + 1) * X_HEAD_DIM)
                dblk = dv_ref[:, hs].astype(BF16)
            dkv_ref[:, sl] = dblk
            dmemn = dmemn + _dot_nt(dblk, w_ref[b])
        dgm_ref[...] = jnp.sum(dmemn * mem_hat, axis=0, keepdims=True)
        dkg_ref[...] = dkg

    return pl.pallas_call(
        body, name="mem_bwd",
        out_shape=[jax.ShapeDtypeStruct((m, 2 * d), BF16), jax.ShapeDtypeStruct((1, d), F32),
                   jax.ShapeDtypeStruct((1, X_HEAD_DIM), F32)],
    )(mem, g_mem, wkv_all, k_norm_g, dkn, dv)


def _in_proj_bwd(x, g_mix, w_in_all, dproj, dx1, tm, pair_sums):
    t, d = x.shape
    nb, _, bw = w_in_all.shape
    nt = t // tm
    n = len(pair_sums)

    def body(x_ref, g_ref, w_ref, dp_ref, dx1_ref, *rest):
        dx_ref, dg_ref = rest[n:n + 2]
        acc_ref = rest[2 * n + 2]
        begin, finish = _chip_exchange_phases(rest[:n], rest[n + 2:2 * n + 2], *rest[2 * n + 3:])
        i, j = pl.program_id(0), pl.program_id(1)
        pl.when((i == 0) & (j == 0))(begin)

        @pl.when(j == 0)
        def _():
            acc_ref[...] = jnp.zeros_like(acc_ref)

        @pl.when((i == 0) & (j == 0))
        def _():
            dg_ref[...] = jnp.zeros_like(dg_ref)

        acc_ref[...] += _dot_nt(dp_ref[...], w_ref[...])

        @pl.when(j == nb - 1)
        def _():
            xf = x_ref[...]
            rs = _rstd(xf)
            xhat = xf * rs
            dh = acc_ref[...]
            dg_ref[...] += jnp.sum(dh * xhat, axis=0, keepdims=True)
            dx_ref[...] = dx1_ref[...] + _rms_bwd(dh, xhat, rs, g_ref[...])

        pl.when((i == nt - 1) & (j == nb - 1))(finish)

    tile = pl.BlockSpec((tm, d), lambda i, j: (i, 0))
    row = pl.BlockSpec((1, d), lambda i, j: (0, 0))
    outs = pl.pallas_call(
        body, name="in_proj_bwd",
        grid=(nt, nb),
        in_specs=[tile, row, pl.BlockSpec((None, d, bw), lambda i, j: (j, 0, 0)),
                  pl.BlockSpec((tm, bw), lambda i, j: (i, j)), tile] + [ANY] * n,
        out_specs=[tile, row] + [ANY] * n,
        out_shape=[jax.ShapeDtypeStruct((t, d), F32), jax.ShapeDtypeStruct((1, d), F32)] + _chip_exchange_shapes(pair_sums),
        scratch_shapes=[pltpu.VMEM((tm, d), F32)] + _chip_exchange_sems(n),
        compiler_params=pltpu.CompilerParams(dimension_semantics=("arbitrary", "arbitrary")),
    )(x, g_mix, w_in_all, dproj, dx1, *pair_sums)
    return outs[0], outs[1], outs[2:]


def _weight_grad(a, b, bw, tmm, name):
    t, m = a.shape
    n = b.shape[1]
    tmm = min(tmm, m)

    def body(a_ref, b_ref, o_ref):
        o_ref[...] = _dot_tn(a_ref[...].astype(BF16), b_ref[...].astype(BF16)).astype(BF16)

    return pl.pallas_call(
        body, name=name,
        grid=(m // tmm, n // bw),
        in_specs=[pl.BlockSpec((t, tmm), lambda i, j: (0, i)), pl.BlockSpec((t, bw), lambda i, j: (0, j))],
        out_specs=pl.BlockSpec((None, tmm, bw), lambda i, j: (j, i, 0)),
        out_shape=jax.ShapeDtypeStruct((n // bw, m, bw), BF16),
        compiler_params=pltpu.CompilerParams(dimension_semantics=("parallel", "parallel")),
    )(a, b)


def _pair_sum(grad, recv, own_blocks, name):
    _, rows, cols = grad.shape

    def body(idx_ref, g_ref, r_ref, o_ref):
        o_ref[...] = (g_ref[...].astype(F32) + r_ref[...].astype(F32)).astype(BF16)

    return pl.pallas_call(
        body, name=name,
        grid_spec=pltpu.PrefetchScalarGridSpec(
            num_scalar_prefetch=1, grid=(4,),
            in_specs=[pl.BlockSpec((None, rows, cols), lambda r, idx: (idx[r], 0, 0)),
                      pl.BlockSpec((None, rows, cols), lambda r, idx: (r, 0, 0))],
            out_specs=pl.BlockSpec((None, rows, cols), lambda r, idx: (r, 0, 0))),
        out_shape=jax.ShapeDtypeStruct((4, rows, cols), BF16),
        compiler_params=pltpu.CompilerParams(dimension_semantics=("parallel",)),
    )(own_blocks, grad, recv)


def _adamw_math(w, g, m, v):
    m = ADAM_B1 * m + (1.0 - ADAM_B1) * g
    v = ADAM_B2 * v + (1.0 - ADAM_B2) * jnp.square(g)
    m_hat = m / (1.0 - ADAM_B1 ** ADAM_STEP)
    v_hat = v / (1.0 - ADAM_B2 ** ADAM_STEP)
    delta = -ADAM_LR * (m_hat / (jnp.sqrt(v_hat) + ADAM_EPS) + ADAM_WD * w)
    return delta, m, v


def _adamw_sharded(pair_sums, recv, w, m, v, tr, name):
    rows, cols = w.shape
    tr = min(tr, rows)

    def body(h_ref, r_ref, w_ref, m_ref, v_ref, g_out, d_out, m_out, v_out):
        g = h_ref[...].astype(F32)
        for r in range(3):
            g = g + r_ref[r].astype(F32)
        g_out[...] = g
        d_out[...], m_out[...], v_out[...] = _adamw_math(w_ref[...], g, m_ref[...], v_ref[...])

    tile = pl.BlockSpec((tr, cols), lambda i: (i, 0))
    return pl.pallas_call(
        body, name=name,
        grid=(rows // tr,),
        in_specs=[pl.BlockSpec((None, tr, cols), lambda i: (0, i, 0)),
                  pl.BlockSpec((3, tr, cols), lambda i: (0, i, 0)), tile, tile, tile],
        out_specs=[tile] * 4,
        out_shape=[jax.ShapeDtypeStruct((rows, cols), F32)] * 4,
        compiler_params=pltpu.CompilerParams(dimension_semantics=("parallel",)),
    )(pair_sums, recv, w, m, v)


SMALL_ROWS = 16


def _pack_rows(dg_mix, dg_mem, dg_mlp, dqg, dkg, dconv, lsum):
    def body(a_ref, b_ref, c_ref, q_ref, k_ref, cv_ref, l_ref, o_ref):
        o_ref[...] = jnp.zeros_like(o_ref)
        for r, ref in enumerate((a_ref, b_ref, c_ref)):
            o_ref[r:r + 1, :] = ref[...]
        o_ref[3:4, :X_HEAD_DIM] = q_ref[...]
        o_ref[4:5, :X_HEAD_DIM] = k_ref[...]
        o_ref[5:8, :] = cv_ref[...]
        o_ref[8:9, :] = l_ref[...]

    return pl.pallas_call(body, name="small_pack", out_shape=jax.ShapeDtypeStruct((SMALL_ROWS, D_MODEL), F32))(
        dg_mix, dg_mem, dg_mlp, dqg, dkg, dconv, lsum)


def _small_sum(gathered):
    def body(g_ref, o_ref):
        total = g_ref[0]
        for dev in range(1, N_DEV):
            total = total + g_ref[dev]
        o_ref[...] = jnp.zeros_like(o_ref)
        for piece in range(5):
            o_ref[piece * SMALL_TILE:piece * SMALL_TILE + 1, :] = total[piece:piece + 1]
        o_ref[5 * SMALL_TILE:5 * SMALL_TILE + 3, :] = total[5:8]
        o_ref[6 * SMALL_TILE:6 * SMALL_TILE + 1, :] = total[8:9]

    return pl.pallas_call(body, name="small_grad_sum",
                          out_shape=jax.ShapeDtypeStruct((7 * SMALL_TILE, D_MODEL), F32))(gathered)


def _adamw_small(w, g, m, v):
    def body(w_ref, g_ref, m_ref, v_ref, d_out, m_out, v_out):
        d_out[...], m_out[...], v_out[...] = _adamw_math(w_ref[...], g_ref[...], m_ref[...], v_ref[...])

    return pl.pallas_call(body, name="adamw_small", out_shape=[jax.ShapeDtypeStruct(w.shape, F32)] * 3)(w, g, m, v)


def _pad_tile(a):
    return jnp.pad(a, ((0, SMALL_TILE - a.shape[0]), (0, D_MODEL - a.shape[1])))


def _pack_small(*pieces):
    return jnp.concatenate([_pad_tile(a) for a in pieces], axis=0)


def kernel(x, mem, g_mix, g_mem, w_in, conv_w, w_conv_out, w_sb_out, q_norm_g, k_norm_g, w_mem_kv, w_x_out, w_out, g_mlp, w_up, w_down, loss_target, m_g_mix, m_g_mem, m_w_in, m_conv_w, m_w_conv_out, m_w_sb_out, m_q_norm_g, m_k_norm_g, m_w_mem_kv, m_w_x_out, m_w_out, m_g_mlp, m_w_up, m_w_down, v_g_mix, v_g_mem, v_w_in, v_conv_w, v_w_conv_out, v_w_sb_out, v_q_norm_g, v_k_norm_g, v_w_mem_kv, v_w_x_out, v_w_out, v_g_mlp, v_w_up, v_w_down):
    xpos, ypos, cpos = _mesh_pos()
    me = 4 * xpos + 2 * ypos + cpos
    x2d, mem2d, tgt2d = x[0], mem[0], loss_target[0]
    t = x2d.shape[0]
    tm = min(512, t)
    tm_s = min(256, t)

    big = {
        "w_in": (w_in[0], m_w_in[0], v_w_in[0]),
        "w_conv_out": (w_conv_out[0], m_w_conv_out[0], v_w_conv_out[0]),
        "w_sb_out": (w_sb_out[0], m_w_sb_out[0], v_w_sb_out[0]),
        "w_mem_kv": (w_mem_kv[0], m_w_mem_kv[0], v_w_mem_kv[0]),
        "w_x_out": (w_x_out[0], m_w_x_out[0], v_w_x_out[0]),
        "w_out": (w_out[0], m_w_out[0], v_w_out[0]),
        "w_up": (w_up[0], m_w_up[0], v_w_up[0]),
        "w_down": (w_down[0], m_w_down[0], v_w_down[0]),
    }
    late = [n for n in big if n != "w_in"]
    as_bf16 = lambda group: [big[n][0].astype(BF16) for n in group]
    conv_pad = jnp.pad(conv_w[0], ((0, 8 - 3), (0, 0)))
    chips = [(1 - xpos, ypos), (xpos, 1 - ypos), (1 - xpos, 1 - ypos)]
    arrival_blocks = jnp.stack([me, 4 * xpos + 2 * ypos + (1 - cpos)]
                               + [4 * cx + 2 * cy + cpos for cx, cy in chips]
                               + [4 * cx + 2 * cy + (1 - cpos) for cx, cy in chips]).astype(jnp.int32)

    proj, h, (w_in_all, conv_all) = _in_proj(x2d, g_mix, arrival_blocks, tm, [big["w_in"][0].astype(BF16), conv_pad])
    conv_full = conv_all[:, :3, :].transpose(1, 0, 2).reshape(3, D_MODEL)
    a_conv = _conv_fwd(proj, conv_full, 256)
    tq = min(SB_QUERY_TILE, t)
    pairs = D_MODEL // SB_BLOCK

    def groups_t(cols):
        return cols.reshape(t // tq, tq, pairs, SB_BLOCK).transpose(2, 0, 3, 1)

    kt4 = groups_t(proj[:, COL_SK * D_MODEL:(COL_SK + 1) * D_MODEL])
    vt4 = groups_t(proj[:, COL_SV * D_MODEL:(COL_SV + 1) * D_MODEL])
    o_sb, sb_weights, sb_logits, gathered = _sb_fwd(proj, vt4, tq, as_bf16(late))
    full = dict(zip(late, gathered))
    wkv_all, w_up_all = full["w_mem_kv"], full["w_up"]
    rows_full = lambda a: a.reshape(a.shape[0] * a.shape[1], a.shape[2])
    wc, ws, wx, wo, wd = (rows_full(full[n]) for n in ("w_conv_out", "w_sb_out", "w_x_out", "w_out", "w_down"))
    mem_n, kn, vmem = _mem_prep(mem2d, g_mem, wkv_all, k_norm_g)
    o_x = _x_fwd(proj, q_norm_g, kn, vmem, tm_s)
    x1, y_conv, y_sb, y_x, merged = _merge_fwd(x2d, proj, a_conv, o_sb, o_x, wc, ws, wx, wo, tm_s)
    up, h2, dx2, lsum = _mlp_fwd(x1, g_mlp, w_up_all, wd, tgt2d, tm)

    dup, act, dx1, dg_mlp = _mlp_bwd(x1, g_mlp, w_up_all, wd, up, dx2, tm)
    dgate, dy_conv, dy_sb, dy_x, da_conv, do_sb, do_x = _merge_bwd(dx1, proj, y_conv, y_sb, y_x, wc, ws, wx, wo, tm_s)
    dch, dcb, dcc, dconv = _conv_bwd(proj, conv_full, da_conv, 256)
    dxq, dkn, dvm, dqg = _x_bwd(proj, q_norm_g, kn, vmem, do_x, tm_s)
    dkv, dg_mem, dkg = _mem_bwd(mem2d, g_mem, wkv_all, k_norm_g, dkn, dvm)
    wgrads = {
        "w_conv_out": _weight_grad(a_conv, dy_conv, D_MODEL, 512, "dw_conv_out"),
        "w_sb_out": _weight_grad(o_sb, dy_sb, D_MODEL, 512, "dw_sb_out"),
        "w_mem_kv": _weight_grad(mem_n, dkv, wkv_all.shape[2], 512, "dw_mem_kv"),
        "w_x_out": _weight_grad(o_x, dy_x, D_MODEL, 512, "dw_x_out"),
        "w_out": _weight_grad(merged, dx1, D_MODEL, 512, "dw_out"),
        "w_up": _weight_grad(h2, dup, w_up_all.shape[2], 512, "dw_up"),
        "w_down": _weight_grad(act, dx2, D_MODEL, 512, "dw_down"),
    }

    own_blocks = jnp.stack([4 * (xpos ^ dx) + 2 * (ypos ^ dy) + cpos for dx in (0, 1) for dy in (0, 1)]).astype(jnp.int32)

    def pair_reduce(group):
        blocked = [wgrads[n].reshape((N_DEV,) + big[n][0].shape) for n in group]
        from_sibling = _pair_exchange(blocked, "grad_pair_exchange_" + group[0])
        return [_pair_sum(g, r, own_blocks, "pair_sum_" + n) for n, g, r in zip(group, blocked, from_sibling)]

    pair_sums = dict(zip(late, pair_reduce(late)))
    dq, dk, dv, from_chips_late = _sb_bwd(proj, kt4, do_sb, o_sb, sb_weights, sb_logits, tq,
                                          [pair_sums[n] for n in late])
    from_chips = dict(zip(late, from_chips_late))
    dproj = jnp.concatenate([dch, dcb, dcc, dq, dk, dv, dxq, dgate], axis=1)
    wgrads["w_in"] = _weight_grad(h, dproj, w_in_all.shape[2], 512, "dw_in")
    pair_sums["w_in"], = pair_reduce(["w_in"])
    grad_x, dg_mix, (from_chips["w_in"],) = _in_proj_bwd(x2d, g_mix, w_in_all, dproj, dx1, tm, [pair_sums["w_in"]])
    res = {}
    for n in big:
        w_sh, m_sh, v_sh = big[n]
        res[n] = _adamw_sharded(pair_sums[n], from_chips[n], w_sh, m_sh, v_sh, 256, "adamw_" + n)

    part = _pack_rows(dg_mix, dg_mem, dg_mlp, dqg, dkg, dconv, lsum)
    gsum = _small_sum(_small_all_gather(part))
    loss = 0.5 * jnp.sum(gsum[6 * SMALL_TILE]) / D_MODEL
    conv_cols = lax.dynamic_slice(gsum[5 * SMALL_TILE:6 * SMALL_TILE], (0, me * (D_MODEL // N_DEV)),
                                  (SMALL_TILE, D_MODEL // N_DEV))
    g_small = jnp.concatenate([gsum[:5 * SMALL_TILE], _pad_tile(conv_cols)], axis=0)
    w_small = _pack_small(g_mix, g_mem, g_mlp, q_norm_g, k_norm_g, conv_w[0])
    m_small = _pack_small(m_g_mix, m_g_mem, m_g_mlp, m_q_norm_g, m_k_norm_g, m_conv_w[0])
    v_small = _pack_small(v_g_mix, v_g_mem, v_g_mlp, v_q_norm_g, v_k_norm_g, v_conv_w[0])
    d_small, nm_small, nv_small = _adamw_small(w_small, g_small, m_small, v_small)

    def unpack(p):
        return {"g_mix": p[0:1], "g_mem": p[8:9], "g_mlp": p[16:17], "q_norm_g": p[24:25, :X_HEAD_DIM],
                "k_norm_g": p[32:33, :X_HEAD_DIM], "conv_w": p[40:43, :D_MODEL // N_DEV][None]}

    small = [unpack(p) for p in (g_small, d_small, nm_small, nv_small)]
    order = ["g_mix", "g_mem", "w_in", "conv_w", "w_conv_out", "w_sb_out", "q_norm_g", "k_norm_g", "w_mem_kv",
             "w_x_out", "w_out", "g_mlp", "w_up", "w_down"]
    outs = [loss, grad_x[None]]
    for kind in range(4):
        for n in order:
            outs.append(res[n][kind][None] if n in res else small[kind][n])
    return tuple(outs)
```

```python
import jax
import jax.numpy as jnp
from jax import lax
from jax.experimental import pallas as pl
from jax.experimental.pallas import tpu as pltpu

F32 = jnp.float32
BF16 = jnp.bfloat16
MESH = pl.DeviceIdType.MESH

EPS = 1e-6
N_DEV = 8
D_MODEL = 1024
SB_HEAD_DIM = 64
SB_BLOCK = 128
SB_QUERY_TILE = 512
X_HEADS = 4
X_HEAD_DIM = 256
N_BRANCH = 3
COL_CH, COL_CB, COL_CC, COL_SQ, COL_SK, COL_SV, COL_XQ, COL_GATE = 0, 1, 2, 3, 4, 5, 6, 7

ADAM_LR = 0.001
ADAM_B1 = 0.9
ADAM_B2 = 0.999
ADAM_EPS = 1e-08
ADAM_WD = 0.01
ADAM_STEP = 10

SMALL_TILE = 8


def _dot(a, b):
    return jnp.dot(a, b, preferred_element_type=F32)


def _dot_nt(a, b):
    return lax.dot_general(a, b, (((1,), (1,)), ((), ())), preferred_element_type=F32)


def _dot_tn(a, b):
    return lax.dot_general(a, b, (((0,), (0,)), ((), ())), preferred_element_type=F32)


def _rstd(xf):
    return lax.rsqrt(jnp.mean(xf * xf, axis=-1, keepdims=True) + EPS)


def _sigmoid(z):
    return 1.0 / (1.0 + jnp.exp(-z))


def _log_sigmoid(z):
    return jnp.minimum(z, 0.0) - jnp.log(1.0 + jnp.exp(-jnp.abs(z)))


def _rms_bwd(dy, xhat, r, g):
    dxhat = dy * g
    return r * (dxhat - xhat * jnp.mean(dxhat * xhat, axis=-1, keepdims=True))


def _mesh_pos():
    return lax.axis_index("x"), lax.axis_index("y"), lax.axis_index("c")


ANY = pl.BlockSpec(memory_space=pl.ANY)


def _gather_shapes(shards):
    return [jax.ShapeDtypeStruct((N_DEV,) + s.shape, s.dtype) for s in shards]


def _gather_sems(n):
    return [pltpu.SemaphoreType.DMA((n, 7)), pltpu.SemaphoreType.DMA((n, 7)), pltpu.SemaphoreType.DMA((n,))]


def _gather_phases(ins, outs, send_sems, recv_sems, local_sems, by_arrival=False):
    n = len(ins)
    x, y, c = _mesh_pos()
    me, sibling = (x, y, c), (x, y, 1 - c)
    chips = [(1 - x, y), (x, 1 - y), (1 - x, 1 - y)]

    def blk(a, px, py, pc):
        return outs[a].at[4 * px + 2 * py + pc]

    def copy(a, k, block, to, src=None):
        return pltpu.make_async_remote_copy(
            src_ref=blk(a, *block) if src is None else src, dst_ref=blk(a, *block),
            send_sem=send_sems.at[a, k], recv_sem=recv_sems.at[a, k], device_id=to, device_id_type=MESH)

    def local(a):
        return pltpu.make_async_copy(ins[a], blk(a, *me), local_sems.at[a])

    def own(a):
        return [copy(a, 0, me, sibling, src=ins[a])] + [copy(a, 1 + j, me, (*chip, c), src=ins[a])
                                                        for j, chip in enumerate(chips)]

    def begin():
        for a in range(n):
            local(a).start()
        for a in range(n):
            for cp in own(a):
                cp.start()

    def arrive(order):
        for a in range(n):
            if order == 0:
                copy(a, 0, sibling, me).wait_recv()
            elif order <= 3:
                chip = chips[order - 1]
                copy(a, order, (*chip, c), me).wait_recv()
                copy(a, 3 + order, (*chip, c), sibling).start()
            else:
                copy(a, order, (*chips[order - 4], 1 - c), me).wait_recv()

    def relay():
        for order in (1, 2, 3):
            arrive(order)

    def drain():
        for a in range(n):
            for cp in own(a):
                cp.wait_send()
            for j, chip in enumerate(chips):
                copy(a, 4 + j, (*chip, c), sibling).wait_send()
            local(a).wait()

    def finish():
        for order in (0, 4, 5, 6):
            arrive(order)
        drain()

    if by_arrival:
        return begin, arrive, drain
    return begin, relay, finish


def _pair_exchange(grads, name):
    n = len(grads)

    def body(*refs):
        begin, finish = _pair_exchange_phases(refs[:n], refs[n:2 * n], *refs[2 * n:])
        begin()
        finish()

    return pl.pallas_call(
        body, name=name,
        out_shape=_pair_exchange_shapes(grads),
        in_specs=[ANY] * n, out_specs=[ANY] * n,
        scratch_shapes=_pair_exchange_sems(n),
    )(*grads)


def _pair_exchange_shapes(grads):
    return [jax.ShapeDtypeStruct((4,) + g.shape[1:], g.dtype) for g in grads]


def _pair_exchange_sems(n):
    return [pltpu.SemaphoreType.DMA((n, 4)), pltpu.SemaphoreType.DMA((n, 4))]


def _pair_exchange_phases(ins, outs, send_sems, recv_sems):
    x, y, c = _mesh_pos()
    xs, ys = (x, 1 - x), (y, 1 - y)

    def copies():
        out = []
        for a in range(len(ins)):
            for r in range(4):
                dx, dy = divmod(r, 2)
                out.append(pltpu.make_async_remote_copy(
                    src_ref=ins[a].at[4 * xs[dx] + 2 * ys[dy] + (1 - c)], dst_ref=outs[a].at[r],
                    send_sem=send_sems.at[a, r], recv_sem=recv_sems.at[a, r],
                    device_id=(x, y, 1 - c), device_id_type=MESH))
        return out

    def begin():
        for cp in copies():
            cp.start()

    def finish():
        for cp in copies():
            cp.wait()

    return begin, finish


def _chip_exchange_shapes(sums):
    return [jax.ShapeDtypeStruct((3,) + s.shape[1:], s.dtype) for s in sums]


def _chip_exchange_sems(n):
    return [pltpu.SemaphoreType.DMA((n, 3)), pltpu.SemaphoreType.DMA((n, 3))]


def _chip_exchange_phases(ins, outs, send_sems, recv_sems):
    x, y, c = _mesh_pos()
    xs, ys = (x, 1 - x), (y, 1 - y)

    def copies():
        out = []
        for a in range(len(ins)):
            for r in range(1, 4):
                dx, dy = divmod(r, 2)
                out.append(pltpu.make_async_remote_copy(
                    src_ref=ins[a].at[r], dst_ref=outs[a].at[r - 1],
                    send_sem=send_sems.at[a, r - 1], recv_sem=recv_sems.at[a, r - 1],
                    device_id=(xs[dx], ys[dy], c), device_id_type=MESH))
        return out

    def begin():
        for cp in copies():
            cp.start()

    def finish():
        for cp in copies():
            cp.wait()

    return begin, finish


def _small_all_gather(part):
    rows, cols = part.shape

    def body(in_ref, out_ref, send_sems, recv_sems):
        x, y, c = _mesh_pos()
        xs, ys, cs = (x, 1 - x), (y, 1 - y), (c, 1 - c)
        out_ref[4 * x + 2 * y + c] = in_ref[...]
        copies = []
        for k in range(1, N_DEV):
            dx, dy, dc = k // 4, (k // 2) % 2, k % 2
            copies.append((
                pltpu.make_async_remote_copy(
                    src_ref=in_ref, dst_ref=out_ref.at[4 * x + 2 * y + c],
                    send_sem=send_sems.at[k - 1], recv_sem=recv_sems.at[k - 1],
                    device_id=(xs[dx], ys[dy], cs[dc]), device_id_type=MESH),
                pltpu.make_async_remote_copy(
                    src_ref=in_ref, dst_ref=out_ref.at[4 * xs[dx] + 2 * ys[dy] + cs[dc]],
                    send_sem=send_sems.at[k - 1], recv_sem=recv_sems.at[k - 1],
                    device_id=(xs[dx], ys[dy], cs[dc]), device_id_type=MESH)))
        for send, _ in copies:
            send.start()
        for send, recv in copies:
            recv.wait_recv()
            send.wait_send()

    return pl.pallas_call(
        body, name="small_all_gather",
        out_shape=jax.ShapeDtypeStruct((N_DEV, rows, cols), part.dtype),
        in_specs=[pl.BlockSpec(memory_space=pltpu.VMEM)],
        out_specs=pl.BlockSpec(memory_space=pltpu.VMEM),
        scratch_shapes=[pltpu.SemaphoreType.DMA((N_DEV - 1,)), pltpu.SemaphoreType.DMA((N_DEV - 1,))],
    )(part)


ARRIVAL_ORDER = (0, 1, 2, 4, 5, 3, 6)


def _arrival_blocks(xpos, ypos, cpos):
    chips = [(1 - xpos, ypos), (xpos, 1 - ypos), (1 - xpos, 1 - ypos)]
    by_order = ([4 * xpos + 2 * ypos + (1 - cpos)] + [4 * cx + 2 * cy + cpos for cx, cy in chips]
                + [4 * cx + 2 * cy + (1 - cpos) for cx, cy in chips])
    return jnp.stack([4 * xpos + 2 * ypos + cpos] + [by_order[o] for o in ARRIVAL_ORDER]).astype(jnp.int32)


def _in_proj(x, g_mix, arrival_blocks, tm, shards):
    t, d = x.shape
    bw = shards[0].shape[1]
    nt = t // tm
    n = len(shards)

    def body(blocks_ref, x_ref, g_ref, *rest):
        w_shard = rest[0]
        proj_ref, h_ref = rest[n:n + 2]
        w_all = rest[n + 2]
        h_scr, w_buf, fetch_sems = rest[2 * n + 2:2 * n + 5]
        begin, arrive, drain = _gather_phases(rest[:n], rest[n + 2:2 * n + 2], *rest[2 * n + 5:], by_arrival=True)
        j, i = pl.program_id(0), pl.program_id(1)
        slot = lax.rem(j, 2)

        def fetch(src, into):
            return pltpu.make_async_copy(src, w_buf.at[into], fetch_sems.at[into])

        @pl.when((j == 0) & (i == 0))
        def _():
            begin()
            fetch(w_shard, 0).start()

        @pl.when(j == 0)
        def _():
            xf = x_ref[...]
            hv = (xf * _rstd(xf) * g_ref[...]).astype(BF16)
            h_ref[...] = hv
            h_scr[pl.ds(pl.multiple_of(i * tm, tm), tm), :] = hv

        @pl.when(i == 0)
        def _():
            fetch(w_shard, slot).wait()

        proj_ref[...] = _dot(h_scr[pl.ds(pl.multiple_of(i * tm, tm), tm), :], w_buf[slot]).astype(BF16)

        for nxt in range(1, N_DEV):
            @pl.when((i == nt - 1) & (j == nxt - 1))
            def _():
                arrive(ARRIVAL_ORDER[nxt - 1])
                fetch(w_all.at[blocks_ref[nxt]], 1 - slot).start()

        pl.when((i == nt - 1) & (j == N_DEV - 1))(drain)

    first_pass = lambda j, i, blocks: (jnp.where(j == 0, i, nt - 1), 0)
    outs = pl.pallas_call(
        body, name="in_proj",
        grid_spec=pltpu.PrefetchScalarGridSpec(
            num_scalar_prefetch=1, grid=(N_DEV, nt),
            in_specs=[pl.BlockSpec((tm, d), first_pass), pl.BlockSpec((1, d), lambda j, i, blocks: (0, 0))] + [ANY] * n,
            out_specs=[pl.BlockSpec((tm, bw), lambda j, i, blocks: (i, blocks[j])),
                       pl.BlockSpec((tm, d), first_pass)] + [ANY] * n,
            scratch_shapes=[pltpu.VMEM((t, d), BF16), pltpu.VMEM((2, d, bw), BF16), pltpu.SemaphoreType.DMA((2,))]
            + _gather_sems(n)),
        out_shape=[jax.ShapeDtypeStruct((t, N_DEV * bw), BF16), jax.ShapeDtypeStruct((t, d), BF16)] + _gather_shapes(shards),
        compiler_params=pltpu.CompilerParams(dimension_semantics=("arbitrary", "arbitrary")),
    )(arrival_blocks, x, g_mix, *shards)
    return outs[0], outs[1], outs[2:]


def _conv_terms(ch_ref, cb_ref, cc_ref, w_ref):
    ch, cb, cc = ch_ref[...].astype(F32), cb_ref[...].astype(F32), cc_ref[...].astype(F32)
    u = cc * ch
    row = lax.broadcasted_iota(jnp.int32, u.shape, 0)
    u1 = jnp.where(row >= 1, pltpu.roll(u, 1, 0), 0.0)
    u2 = jnp.where(row >= 2, pltpu.roll(u, 2, 0), 0.0)
    w = (w_ref[0:1, :], w_ref[1:2, :], w_ref[2:3, :])
    cv = w[2] * u + w[1] * u1 + w[0] * u2
    return ch, cb, cc, u, u1, u2, cv, w, row


def _conv_fwd(proj, conv_w, cw):
    t = proj.shape[0]
    nper = D_MODEL // cw

    def body(ch_ref, cb_ref, cc_ref, w_ref, a_ref):
        _, cb, _, _, _, _, cv, _, _ = _conv_terms(ch_ref, cb_ref, cc_ref, w_ref)
        a_ref[...] = (cb * cv).astype(BF16)

    def col(piece):
        return pl.BlockSpec((t, cw), lambda j: (0, piece * nper + j))

    return pl.pallas_call(
        body, name="conv_fwd",
        grid=(nper,),
        in_specs=[col(COL_CH), col(COL_CB), col(COL_CC), pl.BlockSpec((3, cw), lambda j: (0, j))],
        out_specs=pl.BlockSpec((t, cw), lambda j: (0, j)),
        out_shape=jax.ShapeDtypeStruct((t, D_MODEL), BF16),
        compiler_params=pltpu.CompilerParams(dimension_semantics=("parallel",)),
    )(proj, proj, proj, conv_w)


def _scan_matrix():
    s = lax.broadcasted_iota(jnp.int32, (SB_BLOCK, SB_BLOCK), 0)
    j = lax.broadcasted_iota(jnp.int32, (SB_BLOCK, SB_BLOCK), 1)
    return jnp.where(j > s, 1.0, 0.0).astype(BF16)


def _suffix_sum(u_mat, xv):
    hi = xv.astype(BF16)
    lo = (xv - hi.astype(F32)).astype(BF16)
    return _dot(u_mat, hi) + _dot(u_mat, lo)


def _head_rows(vt, h):
    row = lax.broadcasted_iota(jnp.int32, vt.shape, 0)
    return jnp.where((row >= h * SB_HEAD_DIM) & (row < (h + 1) * SB_HEAD_DIM), vt, 0.0).astype(BF16)


def _head_lanes(v, h):
    lane = lax.broadcasted_iota(jnp.int32, v.shape, 1)
    return jnp.where((lane >= h * SB_HEAD_DIM) & (lane < (h + 1) * SB_HEAD_DIM), v, 0.0).astype(BF16)


def _group_suffix(u_mat, xv, carry):
    nblk = xv.shape[0] // SB_BLOCK
    parts = [None] * nblk
    for j in reversed(range(nblk)):
        xj = xv[j * SB_BLOCK:(j + 1) * SB_BLOCK]
        parts[j] = _suffix_sum(u_mat, xj) + carry
        carry = carry + jnp.sum(xj, axis=0, keepdims=True)
    return jnp.concatenate(parts, axis=0), carry


def _sb_probs(kgrp, qt_h, u_mat, carry, past):
    z = _dot(kgrp, qt_h)
    lb = _log_sigmoid(z)
    l1 = lb - z
    if past is not None:
        l1 = jnp.where(past, l1, 0.0)
    between, carry = _group_suffix(u_mat, l1, carry)
    a = jnp.exp(lb + between)
    if past is not None:
        a = jnp.where(past, a, 0.0)
    return a, z, carry


def _sb_schedule(nq):
    steps = [(qi, g) for qi in range(nq) for g in range(qi, -1, -1)]
    return jnp.asarray([s[0] for s in steps], jnp.int32), jnp.asarray([s[1] for s in steps], jnp.int32)


def _past_mask(tq):
    return lax.broadcasted_iota(jnp.int32, (tq, tq), 0) < lax.broadcasted_iota(jnp.int32, (tq, tq), 1)


def _sb_fwd(proj, vt4, tq, shards):
    t = proj.shape[0]
    pairs = D_MODEL // SB_BLOCK
    nq = t // tq
    qi_tab, g_tab = _sb_schedule(nq)
    ns = qi_tab.shape[0]
    n = len(shards)

    def body(qi_ref, g_ref, q_ref, k_ref, vt_ref, *rest):
        o_ref, a_ref, z_ref = rest[n:n + 3]
        acc_ref, carry_ref, qt_ref = rest[2 * n + 3:2 * n + 6]
        begin, relay, finish = _gather_phases(rest[:n], rest[n + 3:2 * n + 3], *rest[2 * n + 6:])
        pi, si = pl.program_id(0), pl.program_id(1)
        diagonal = g_ref[si] == qi_ref[si]
        pl.when((pi == 0) & (si == 0))(begin)
        pl.when((pi == (7 * pairs) // 8) & (si == 0))(relay)

        @pl.when(diagonal)
        def _():
            acc_ref[...] = jnp.zeros_like(acc_ref)
            carry_ref[...] = jnp.zeros_like(carry_ref)
            qt = q_ref[...].astype(F32).T * (SB_HEAD_DIM ** -0.5)
            for h in range(2):
                qt_ref[h] = _head_rows(qt, h)

        def step(past):
            u_mat = _scan_matrix()
            for h in range(2):
                a, z, carry = _sb_probs(k_ref[...], qt_ref[h], u_mat, carry_ref[h], past)
                ab = a.astype(BF16)
                a_ref[h] = ab
                z_ref[h] = z.astype(BF16)
                acc_ref[h] += _dot(vt_ref[h * SB_HEAD_DIM:(h + 1) * SB_HEAD_DIM, :], ab)
                carry_ref[h] = carry

        pl.when(diagonal)(lambda: step(_past_mask(tq)))
        pl.when(jnp.logical_not(diagonal))(lambda: step(None))

        @pl.when(g_ref[si] == 0)
        def _():
            o_ref[...] = jnp.concatenate([acc_ref[0], acc_ref[1]], axis=0).T

        pl.when((pi == pairs - 1) & (si == ns - 1))(finish)

    tile = pl.BlockSpec((None, None, 2, tq, tq), lambda p, s, qt_, gt_: (p, s, 0, 0, 0))
    tiles = jax.ShapeDtypeStruct((pairs, ns, 2, tq, tq), BF16)
    outs = pl.pallas_call(
        body, name="sb_fwd",
        grid_spec=pltpu.PrefetchScalarGridSpec(
            num_scalar_prefetch=2, grid=(pairs, ns),
            in_specs=[pl.BlockSpec((tq, SB_BLOCK), lambda p, s, qt_, gt_: (qt_[s], COL_SQ * pairs + p)),
                      pl.BlockSpec((tq, SB_BLOCK), lambda p, s, qt_, gt_: (gt_[s], COL_SK * pairs + p)),
                      pl.BlockSpec((None, None, SB_BLOCK, tq), lambda p, s, qt_, gt_: (p, gt_[s], 0, 0))] + [ANY] * n,
            out_specs=[pl.BlockSpec((tq, SB_BLOCK), lambda p, s, qt_, gt_: (qt_[s], p)), tile, tile] + [ANY] * n,
            scratch_shapes=[pltpu.VMEM((2, SB_HEAD_DIM, tq), F32), pltpu.VMEM((2, 1, tq), F32),
                            pltpu.VMEM((2, SB_BLOCK, tq), BF16)] + _gather_sems(n)),
        out_shape=[jax.ShapeDtypeStruct((t, D_MODEL), F32), tiles, tiles] + _gather_shapes(shards),
        compiler_params=pltpu.CompilerParams(dimension_semantics=("arbitrary", "arbitrary")),
    )(qi_tab, g_tab, proj, proj, vt4, *shards)
    return outs[0], outs[1], outs[2], outs[3:]


def _mem_prep(mem, g_mem, wkv_all, k_norm_g):
    m, d = mem.shape

    def body(mem_ref, g_ref, w_ref, kg_ref, memn_ref, kn_ref, v_ref):
        memf = mem_ref[...]
        memn = (memf * _rstd(memf) * g_ref[...]).astype(BF16)
        memn_ref[...] = memn
        for b in range(N_DEV):
            kv = _dot(memn, w_ref[b])
            if b < X_HEADS:
                kn_ref[:, b * X_HEAD_DIM:(b + 1) * X_HEAD_DIM] = (kv * _rstd(kv) * kg_ref[...]).astype(BF16)
            else:
                h = b - X_HEADS
                v_ref[:, h * X_HEAD_DIM:(h + 1) * X_HEAD_DIM] = kv.astype(BF16)

    return pl.pallas_call(
        body, name="mem_prep",
        out_shape=[jax.ShapeDtypeStruct((m, d), BF16)] * 3,
    )(mem, g_mem, wkv_all, k_norm_g)


def _x_head(xq_ref, qg, kn_ref, h):
    sl = slice(h * X_HEAD_DIM, (h + 1) * X_HEAD_DIM)
    q = xq_ref[:, sl].astype(F32)
    rq = _rstd(q)
    qhat = q * rq
    qn = (qhat * qg).astype(BF16)
    s = _dot_nt(qn, kn_ref[:, sl]) * (X_HEAD_DIM ** -0.5)
    e = jnp.exp(s - jnp.max(s, axis=-1, keepdims=True))
    p = e / jnp.sum(e, axis=-1, keepdims=True)
    return sl, rq, qhat, qn, p


def _x_fwd(proj, q_norm_g, kn, v, tm):
    t = proj.shape[0]
    m = kn.shape[0]

    def body(xq_ref, qg_ref, kn_ref, v_ref, o_ref):
        for h in range(X_HEADS):
            sl, _, _, _, p = _x_head(xq_ref, qg_ref[...], kn_ref, h)
            o_ref[:, sl] = _dot(p.astype(BF16), v_ref[:, sl]).astype(BF16)

    return pl.pallas_call(
        body, name="x_fwd",
        grid=(t // tm,),
        in_specs=[pl.BlockSpec((tm, D_MODEL), lambda i: (i, COL_XQ)),
                  pl.BlockSpec((1, X_HEAD_DIM), lambda i: (0, 0)),
                  pl.BlockSpec((m, D_MODEL), lambda i: (0, 0)),
                  pl.BlockSpec((m, D_MODEL), lambda i: (0, 0))],
        out_specs=pl.BlockSpec((tm, D_MODEL), lambda i: (i, 0)),
        out_shape=jax.ShapeDtypeStruct((t, D_MODEL), BF16),
        compiler_params=pltpu.CompilerParams(dimension_semantics=("parallel",)),
    )(proj, q_norm_g, kn, v)


def _gate_spec(tm, branch):
    return pl.BlockSpec((tm, D_MODEL), lambda i: (i, COL_GATE + branch))


def _merge_fwd(x, proj, a_conv, o_sb, o_x, w_conv_out, w_sb_out, w_x_out, w_out, tm):
    t, d = x.shape

    def body(x_ref, g0_ref, g1_ref, g2_ref, a_ref, s_ref, xo_ref, wc_ref, ws_ref, wx_ref, wo_ref,
             x1_ref, yc_ref, ys_ref, yx_ref, mg_ref):
        merged = jnp.zeros((tm, d), F32)
        for gate_ref, b_ref, w_ref, y_ref in ((g0_ref, a_ref, wc_ref, yc_ref), (g1_ref, s_ref, ws_ref, ys_ref),
                                              (g2_ref, xo_ref, wx_ref, yx_ref)):
            yv = _dot(b_ref[...].astype(BF16), w_ref[...])
            y_ref[...] = yv.astype(BF16)
            merged = merged + _sigmoid(gate_ref[...].astype(F32)) * yv
        mb = merged.astype(BF16)
        mg_ref[...] = mb
        x1_ref[...] = x_ref[...] + _dot(mb, wo_ref[...])

    tile = pl.BlockSpec((tm, d), lambda i: (i, 0))
    wfull = pl.BlockSpec((d, d), lambda i: (0, 0))
    return pl.pallas_call(
        body, name="merge_fwd",
        grid=(t // tm,),
        in_specs=[tile] + [_gate_spec(tm, b) for b in range(N_BRANCH)] + [tile, tile, tile,
                                                                           wfull, wfull, wfull, wfull],
        out_specs=[tile] * 5,
        out_shape=[jax.ShapeDtypeStruct((t, d), F32)] + [jax.ShapeDtypeStruct((t, d), BF16)] * 4,
        compiler_params=pltpu.CompilerParams(dimension_semantics=("parallel",)),
    )(x, proj, proj, proj, a_conv, o_sb, o_x, w_conv_out, w_sb_out, w_x_out, w_out)


def _mlp_fwd(x1, g_mlp, w_up_all, w_down, target, tm):
    t, d = x1.shape
    nb, _, fw = w_up_all.shape

    def body(x1_ref, g_ref, wu_ref, wd_ref, tgt_ref, up_ref, h2_ref, dx2_ref, lsum_ref, acc_ref):
        i, j = pl.program_id(0), pl.program_id(1)

        @pl.when(j == 0)
        def _():
            xf = x1_ref[...]
            h2_ref[...] = (xf * _rstd(xf) * g_ref[...]).astype(BF16)
            acc_ref[...] = jnp.zeros_like(acc_ref)

        @pl.when((i == 0) & (j == 0))
        def _():
            lsum_ref[...] = jnp.zeros_like(lsum_ref)

        up = _dot(h2_ref[...], wu_ref[...])
        up_ref[...] = up.astype(BF16)
        act = jnp.square(jnp.maximum(up, 0.0)).astype(BF16)
        acc_ref[...] += _dot(act, wd_ref[...])

        @pl.when(j == nb - 1)
        def _():
            diff = x1_ref[...] + acc_ref[...] - tgt_ref[...]
            dx2_ref[...] = diff * (1.0 / d)
            lsum_ref[...] += jnp.sum(diff * diff, axis=0, keepdims=True)

    tile = pl.BlockSpec((tm, d), lambda i, j: (i, 0))
    row = pl.BlockSpec((1, d), lambda i, j: (0, 0))
    return pl.pallas_call(
        body, name="mlp_fwd",
        grid=(t // tm, nb),
        in_specs=[tile, row, pl.BlockSpec((None, d, fw), lambda i, j: (j, 0, 0)),
                  pl.BlockSpec((fw, d), lambda i, j: (j, 0)), tile],
        out_specs=[pl.BlockSpec((tm, fw), lambda i, j: (i, j)), tile, tile, row],
        out_shape=[jax.ShapeDtypeStruct((t, nb * fw), BF16), jax.ShapeDtypeStruct((t, d), BF16),
                   jax.ShapeDtypeStruct((t, d), F32), jax.ShapeDtypeStruct((1, d), F32)],
        scratch_shapes=[pltpu.VMEM((tm, d), F32)],
        compiler_params=pltpu.CompilerParams(dimension_semantics=("arbitrary", "arbitrary")),
    )(x1, g_mlp, w_up_all, w_down, target)


def _mlp_bwd(x1, g_mlp, w_up_all, w_down, up, dx2, tm):
    t, d = x1.shape
    nb, _, fw = w_up_all.shape

    def body(x1_ref, g_ref, wu_ref, wd_ref, up_ref, dx2_ref, dup_ref, act_ref, dx1_ref, dg_ref, acc_ref, dyb_ref):
        i, j = pl.program_id(0), pl.program_id(1)

        @pl.when(j == 0)
        def _():
            dyb_ref[...] = dx2_ref[...].astype(BF16)
            acc_ref[...] = jnp.zeros_like(acc_ref)

        @pl.when((i == 0) & (j == 0))
        def _():
            dg_ref[...] = jnp.zeros_like(dg_ref)

        r = jnp.maximum(up_ref[...].astype(F32), 0.0)
        act_ref[...] = jnp.square(r).astype(BF16)
        dup = (_dot_nt(dyb_ref[...], wd_ref[...]) * (2.0 * r)).astype(BF16)
        dup_ref[...] = dup
        acc_ref[...] += _dot_nt(dup, wu_ref[...])

        @pl.when(j == nb - 1)
        def _():
            xf = x1_ref[...]
            rs = _rstd(xf)
            xhat = xf * rs
            dh2 = acc_ref[...]
            dg_ref[...] += jnp.sum(dh2 * xhat, axis=0, keepdims=True)
            dx1_ref[...] = dx2_ref[...] + _rms_bwd(dh2, xhat, rs, g_ref[...])

    tile = pl.BlockSpec((tm, d), lambda i, j: (i, 0))
    row = pl.BlockSpec((1, d), lambda i, j: (0, 0))
    ff = pl.BlockSpec((tm, fw), lambda i, j: (i, j))
    return pl.pallas_call(
        body, name="mlp_bwd",
        grid=(t // tm, nb),
        in_specs=[tile, row, pl.BlockSpec((None, d, fw), lambda i, j: (j, 0, 0)),
                  pl.BlockSpec((fw, d), lambda i, j: (j, 0)), ff, tile],
        out_specs=[ff, ff, tile, row],
        out_shape=[jax.ShapeDtypeStruct((t, nb * fw), BF16), jax.ShapeDtypeStruct((t, nb * fw), BF16),
                   jax.ShapeDtypeStruct((t, d), F32), jax.ShapeDtypeStruct((1, d), F32)],
        scratch_shapes=[pltpu.VMEM((tm, d), F32), pltpu.VMEM((tm, d), BF16)],
        compiler_params=pltpu.CompilerParams(dimension_semantics=("arbitrary", "arbitrary")),
    )(x1, g_mlp, w_up_all, w_down, up, dx2)


def _merge_bwd(dx1, proj, y_conv, y_sb, y_x, w_conv_out, w_sb_out, w_x_out, w_out, tm):
    t, d = dx1.shape

    def body(dx1_ref, g0_ref, g1_ref, g2_ref, yc_ref, ys_ref, yx_ref, wc_ref, ws_ref, wx_ref, wo_ref,
             dgate_ref, dyc_ref, dys_ref, dyx_ref, da_ref, dos_ref, dox_ref):
        dm = _dot_nt(dx1_ref[...].astype(BF16), wo_ref[...])
        for i, (gate_ref, y_ref, w_ref, dy_ref, db_ref) in enumerate(((g0_ref, yc_ref, wc_ref, dyc_ref, da_ref),
                                                                       (g1_ref, ys_ref, ws_ref, dys_ref, dos_ref),
                                                                       (g2_ref, yx_ref, wx_ref, dyx_ref, dox_ref))):
            gt = _sigmoid(gate_ref[...].astype(F32))
            dy = (dm * gt).astype(BF16)
            dy_ref[...] = dy
            dgate_ref[:, i * d:(i + 1) * d] = (dm * y_ref[...].astype(F32) * gt * (1.0 - gt)).astype(BF16)
            db_ref[...] = _dot_nt(dy, w_ref[...]).astype(BF16)

    tile = pl.BlockSpec((tm, d), lambda i: (i, 0))
    wfull = pl.BlockSpec((d, d), lambda i: (0, 0))
    return pl.pallas_call(
        body, name="merge_bwd",
        grid=(t // tm,),
        in_specs=[tile] + [_gate_spec(tm, b) for b in range(N_BRANCH)] + [tile, tile, tile,
                                                                           wfull, wfull, wfull, wfull],
        out_specs=[pl.BlockSpec((tm, N_BRANCH * d), lambda i: (i, 0))] + [tile] * 6,
        out_shape=[jax.ShapeDtypeStruct((t, N_BRANCH * d), BF16)] + [jax.ShapeDtypeStruct((t, d), BF16)] * 6,
        compiler_params=pltpu.CompilerParams(dimension_semantics=("parallel",)),
    )(dx1, proj, proj, proj, y_conv, y_sb, y_x, w_conv_out, w_sb_out, w_x_out, w_out)


def _conv_bwd(proj, conv_w, da, cw):
    t = proj.shape[0]
    nper = D_MODEL // cw

    def body(ch_ref, cb_ref, cc_ref, w_ref, da_ref, dch_ref, dcb_ref, dcc_ref, dw_ref):
        ch, cb, cc, u, u1, u2, cv, w, row = _conv_terms(ch_ref, cb_ref, cc_ref, w_ref)
        dav = da_ref[...].astype(F32)
        dcb_ref[...] = (dav * cv).astype(BF16)
        dcv = dav * cb
        n1 = jnp.where(row < t - 1, pltpu.roll(dcv, t - 1, 0), 0.0)
        n2 = jnp.where(row < t - 2, pltpu.roll(dcv, t - 2, 0), 0.0)
        du = w[2] * dcv + w[1] * n1 + w[0] * n2
        dcc_ref[...] = (du * ch).astype(BF16)
        dch_ref[...] = (du * cc).astype(BF16)
        dw_ref[0:1, :] = jnp.sum(dcv * u2, axis=0, keepdims=True)
        dw_ref[1:2, :] = jnp.sum(dcv * u1, axis=0, keepdims=True)
        dw_ref[2:3, :] = jnp.sum(dcv * u, axis=0, keepdims=True)

    def col(piece):
        return pl.BlockSpec((t, cw), lambda j: (0, piece * nper + j))

    out_col = pl.BlockSpec((t, cw), lambda j: (0, j))
    wspec = pl.BlockSpec((3, cw), lambda j: (0, j))
    return pl.pallas_call(
        body, name="conv_bwd",
        grid=(nper,),
        in_specs=[col(COL_CH), col(COL_CB), col(COL_CC), wspec, out_col],
        out_specs=[out_col, out_col, out_col, wspec],
        out_shape=[jax.ShapeDtypeStruct((t, D_MODEL), BF16)] * 3 + [jax.ShapeDtypeStruct((3, D_MODEL), F32)],
        compiler_params=pltpu.CompilerParams(dimension_semantics=("parallel",)),
    )(proj, proj, proj, conv_w, da)


def _sb_bwd(proj, kt4, do_sb, o_sb, weights, logits, tq, pair_sums):
    t = proj.shape[0]
    nq = t // tq
    pairs = D_MODEL // SB_BLOCK
    scale = SB_HEAD_DIM ** -0.5
    qi_tab, g_tab = _sb_schedule(nq)
    ns = qi_tab.shape[0]
    n = len(pair_sums)

    def body(qi_ref, g_ref, q_ref, v_ref, kt_ref, do_ref, o_ref, a_ref, z_ref, *rest):
        dq_ref, dk_ref, dv_ref = rest[n:n + 3]
        dk_acc, dv_acc, dqt_ref, carry_ref, qm_ref, dom_ref, dot_ref, dsum_ref = rest[2 * n + 3:2 * n + 11]
        begin, finish = _chip_exchange_phases(rest[:n], rest[n + 3:2 * n + 3], *rest[2 * n + 11:])
        pi, si = pl.program_id(0), pl.program_id(1)
        diagonal = g_ref[si] == qi_ref[si]
        pl.when((pi == 0) & (si == 0))(begin)

        @pl.when(si == 0)
        def _():
            dk_acc[...] = jnp.zeros_like(dk_acc)
            dv_acc[...] = jnp.zeros_like(dv_acc)

        @pl.when(diagonal)
        def _():
            dqt_ref[...] = jnp.zeros_like(dqt_ref)
            carry_ref[...] = jnp.zeros_like(carry_ref)
            q = q_ref[...].astype(F32) * scale
            do = do_ref[...].astype(F32)
            dot_ = do.T
            prod = dot_ * o_ref[...].T
            for h in range(2):
                rows = slice(h * SB_HEAD_DIM, (h + 1) * SB_HEAD_DIM)
                qm_ref[h] = _head_lanes(q, h)
                dom_ref[h] = _head_lanes(do, h)
                dot_ref[h] = _head_rows(dot_, h)
                dsum_ref[h] = jnp.sum(prod[rows, :], axis=0, keepdims=True)

        def step(past):
            u_mat = _scan_matrix()
            ks = pl.multiple_of(g_ref[si] * tq, tq)
            dk_add = jnp.zeros((tq, SB_BLOCK), F32)
            dv_add = jnp.zeros((tq, SB_BLOCK), F32)
            for h in range(2):
                rows = slice(h * SB_HEAD_DIM, (h + 1) * SB_HEAD_DIM)
                ab = a_ref[h]
                gw = _dot(v_ref[...], dot_ref[h]) * ab.astype(F32)
                after, carry = _group_suffix(u_mat, gw, carry_ref[h])
                sig = pl.reciprocal(1.0 + jnp.exp(-z_ref[h].astype(F32)), approx=True)
                dz = gw - sig * (dsum_ref[h] - after)
                if past is not None:
                    dz = jnp.where(past, dz, 0.0)
                dzb = dz.astype(BF16)
                dqt_ref[h] += _dot(kt_ref[rows, :], dzb)
                dk_add = dk_add + _dot(dzb, qm_ref[h])
                dv_add = dv_add + _dot(ab, dom_ref[h])
                carry_ref[h] = carry
            dk_acc[pl.ds(ks, tq), :] += dk_add
            dv_acc[pl.ds(ks, tq), :] += dv_add

        pl.when(diagonal)(lambda: step(_past_mask(tq)))
        pl.when(jnp.logical_not(diagonal))(lambda: step(None))

        @pl.when(g_ref[si] == 0)
        def _():
            dq_ref[...] = (jnp.concatenate([dqt_ref[0], dqt_ref[1]], axis=0).T * scale).astype(BF16)

        @pl.when(si == ns - 1)
        def _():
            dk_ref[...] = dk_acc[...].astype(BF16)
            dv_ref[...] = dv_acc[...].astype(BF16)

        pl.when((pi == pairs - 1) & (si == ns - 1))(finish)

    qblk = lambda base: pl.BlockSpec((tq, SB_BLOCK), lambda p, s, qt_, gt_: (qt_[s], base * pairs + p))
    kgrp = lambda base: pl.BlockSpec((tq, SB_BLOCK), lambda p, s, qt_, gt_: (gt_[s], base * pairs + p))
    seq = pl.BlockSpec((t, SB_BLOCK), lambda p, s, qt_, gt_: (0, p))
    tr = pl.BlockSpec((None, None, SB_BLOCK, tq), lambda p, s, qt_, gt_: (p, gt_[s], 0, 0))
    tile = pl.BlockSpec((None, None, 2, tq, tq), lambda p, s, qt_, gt_: (p, s, 0, 0, 0))
    outs = pl.pallas_call(
        body, name="sb_bwd",
        grid_spec=pltpu.PrefetchScalarGridSpec(
            num_scalar_prefetch=2, grid=(pairs, ns),
            in_specs=[qblk(COL_SQ), kgrp(COL_SV), tr, qblk(0), qblk(0), tile, tile] + [ANY] * n,
            out_specs=[qblk(0), seq, seq] + [ANY] * n,
            scratch_shapes=[pltpu.VMEM((t, SB_BLOCK), F32), pltpu.VMEM((t, SB_BLOCK), F32),
                            pltpu.VMEM((2, SB_HEAD_DIM, tq), F32), pltpu.VMEM((2, 1, tq), F32),
                            pltpu.VMEM((2, tq, SB_BLOCK), BF16), pltpu.VMEM((2, tq, SB_BLOCK), BF16),
                            pltpu.VMEM((2, SB_BLOCK, tq), BF16), pltpu.VMEM((2, 1, tq), F32)] + _chip_exchange_sems(n)),
        out_shape=[jax.ShapeDtypeStruct((t, D_MODEL), BF16)] * 3 + _chip_exchange_shapes(pair_sums),
        compiler_params=pltpu.CompilerParams(dimension_semantics=("arbitrary", "arbitrary")),
    )(qi_tab, g_tab, proj, proj, kt4, do_sb, o_sb, weights, logits, *pair_sums)
    return outs[0], outs[1], outs[2], outs[3:]


def _x_bwd(proj, q_norm_g, kn, v, do_x, tm, grads):
    t = proj.shape[0]
    m = kn.shape[0]
    scale = X_HEAD_DIM ** -0.5
    nt = t // tm
    n = len(grads)

    def body(xq_ref, qg_ref, kn_ref, v_ref, do_ref, *rest):
        dxq_ref, dkn_ref, dv_ref, dqg_ref = rest[n:n + 4]
        begin, finish = _pair_exchange_phases(rest[:n], rest[n + 4:2 * n + 4], *rest[2 * n + 4:])

        @pl.when(pl.program_id(0) == 0)
        def _():
            begin()
            dkn_ref[...] = jnp.zeros_like(dkn_ref)
            dv_ref[...] = jnp.zeros_like(dv_ref)
            dqg_ref[...] = jnp.zeros_like(dqg_ref)

        qg = qg_ref[...]
        for h in range(X_HEADS):
            sl, rq, qhat, qn, p = _x_head(xq_ref, qg, kn_ref, h)
            do_h = do_ref[:, sl]
            dp = _dot_nt(do_h, v_ref[:, sl])
            ds = (p * (dp - jnp.sum(dp * p, axis=-1, keepdims=True)) * scale).astype(BF16)
            dqn = _dot(ds, kn_ref[:, sl])
            dkn_ref[:, sl] += _dot_tn(ds, qn)
            dv_ref[:, sl] += _dot_tn(p.astype(BF16), do_h)
            dqg_ref[...] += jnp.sum(dqn * qhat, axis=0, keepdims=True)
            dxq_ref[:, sl] = _rms_bwd(dqn, qhat, rq, qg).astype(BF16)

        pl.when(pl.program_id(0) == nt - 1)(finish)

    full = pl.BlockSpec((m, D_MODEL), lambda i: (0, 0))
    gain = pl.BlockSpec((1, X_HEAD_DIM), lambda i: (0, 0))
    tile = pl.BlockSpec((tm, D_MODEL), lambda i: (i, 0))
    outs = pl.pallas_call(
        body, name="x_bwd",
        grid=(nt,),
        in_specs=[pl.BlockSpec((tm, D_MODEL), lambda i: (i, COL_XQ)), gain, full, full, tile] + [ANY] * n,
        out_specs=[tile, full, full, gain] + [ANY] * n,
        out_shape=[jax.ShapeDtypeStruct((t, D_MODEL), BF16), jax.ShapeDtypeStruct((m, D_MODEL), F32),
                   jax.ShapeDtypeStruct((m, D_MODEL), F32), jax.ShapeDtypeStruct((1, X_HEAD_DIM), F32)]
        + _pair_exchange_shapes(grads),
        scratch_shapes=_pair_exchange_sems(n),
        compiler_params=pltpu.CompilerParams(dimension_semantics=("arbitrary",)),
    )(proj, q_norm_g, kn, v, do_x, *grads)
    return outs[0], outs[1], outs[2], outs[3], outs[4:]


def _mem_bwd(mem, g_mem, wkv_all, k_norm_g, dkn, dv):
    m, d = mem.shape

    def body(mem_ref, g_ref, w_ref, kg_ref, dkn_ref, dv_ref, dkv_ref, dgm_ref, dkg_ref):
        memf = mem_ref[...]
        mem_hat = memf * _rstd(memf)
        memn = (mem_hat * g_ref[...]).astype(BF16)
        kg = kg_ref[...]
        dmemn = jnp.zeros((m, d), F32)
        dkg = jnp.zeros((1, X_HEAD_DIM), F32)
        for b in range(N_DEV):
            sl = slice(b * X_HEAD_DIM, (b + 1) * X_HEAD_DIM)
            if b < X_HEADS:
                kv = _dot(memn, w_ref[b])
                rk = _rstd(kv)
                khat = kv * rk
                dkn_h = dkn_ref[:, sl]
                dkg = dkg + jnp.sum(dkn_h * khat, axis=0, keepdims=True)
                dblk = _rms_bwd(dkn_h, khat, rk, kg).astype(BF16)
            else:
                hs = slice((b - X_HEADS) * X_HEAD_DIM, (b - X_HEADS + 1) * X_HEAD_DIM)
                dblk = dv_ref[:, hs].astype(BF16)
            dkv_ref[:, sl] = dblk
            dmemn = dmemn + _dot_nt(dblk, w_ref[b])
        dgm_ref[...] = jnp.sum(dmemn * mem_hat, axis=0, keepdims=True)
        dkg_ref[...] = dkg

    return pl.pallas_call(
        body, name="mem_bwd",
        out_shape=[jax.ShapeDtypeStruct((m, 2 * d), BF16), jax.ShapeDtypeStruct((1, d), F32),
                   jax.ShapeDtypeStruct((1, X_HEAD_DIM), F32)],
    )(mem, g_mem, wkv_all, k_norm_g, dkn, dv)


def _in_proj_bwd(x, g_mix, w_in_all, dproj, dx1, tm, pair_sums):
    t, d = x.shape
    nb, _, bw = w_in_all.shape
    nt = t // tm
    n = len(pair_sums)

    def body(x_ref, g_ref, w_ref, dp_ref, dx1_ref, *rest):
        dx_ref, dg_ref = rest[n:n + 2]
        acc_ref = rest[2 * n + 2]
        begin, finish = _chip_exchange_phases(rest[:n], rest[n + 2:2 * n + 2], *rest[2 * n + 3:])
        i, j = pl.program_id(0), pl.program_id(1)
        pl.when((i == 0) & (j == 0))(begin)

        @pl.when(j == 0)
        def _():
            acc_ref[...] = jnp.zeros_like(acc_ref)

        @pl.when((i == 0) & (j == 0))
        def _():
            dg_ref[...] = jnp.zeros_like(dg_ref)

        acc_ref[...] += _dot_nt(dp_ref[...], w_ref[...])

        @pl.when(j == nb - 1)
        def _():
            xf = x_ref[...]
            rs = _rstd(xf)
            xhat = xf * rs
            dh = acc_ref[...]
            dg_ref[...] += jnp.sum(dh * xhat, axis=0, keepdims=True)
            dx_ref[...] = dx1_ref[...] + _rms_bwd(dh, xhat, rs, g_ref[...])

        pl.when((i == nt - 1) & (j == nb - 1))(finish)

    tile = pl.BlockSpec((tm, d), lambda i, j: (i, 0))
    row = pl.BlockSpec((1, d), lambda i, j: (0, 0))
    outs = pl.pallas_call(
        body, name="in_proj_bwd",
        grid=(nt, nb),
        in_specs=[tile, row, pl.BlockSpec((None, d, bw), lambda i, j: (j, 0, 0)),
                  pl.BlockSpec((tm, bw), lambda i, j: (i, j)), tile] + [ANY] * n,
        out_specs=[tile, row] + [ANY] * n,
        out_shape=[jax.ShapeDtypeStruct((t, d), F32), jax.ShapeDtypeStruct((1, d), F32)] + _chip_exchange_shapes(pair_sums),
        scratch_shapes=[pltpu.VMEM((tm, d), F32)] + _chip_exchange_sems(n),
        compiler_params=pltpu.CompilerParams(dimension_semantics=("arbitrary", "arbitrary")),
    )(x, g_mix, w_in_all, dproj, dx1, *pair_sums)
    return outs[0], outs[1], outs[2:]


def _weight_grad(a, b, bw, tmm, name):
    t, m = a.shape
    n = b.shape[1]
    tmm = min(tmm, m)

    def body(a_ref, b_ref, o_ref):
        o_ref[...] = _dot_tn(a_ref[...].astype(BF16), b_ref[...].astype(BF16)).astype(BF16)

    return pl.pallas_call(
        body, name=name,
        grid=(m // tmm, n // bw),
        in_specs=[pl.BlockSpec((t, tmm), lambda i, j: (0, i)), pl.BlockSpec((t, bw), lambda i, j: (0, j))],
        out_specs=pl.BlockSpec((None, tmm, bw), lambda i, j: (j, i, 0)),
        out_shape=jax.ShapeDtypeStruct((n // bw, m, bw), BF16),
        compiler_params=pltpu.CompilerParams(dimension_semantics=("parallel", "parallel")),
    )(a, b)


def _pair_sum(grad, recv, own_blocks, name):
    _, rows, cols = grad.shape

    def body(idx_ref, g_ref, r_ref, o_ref):
        o_ref[...] = (g_ref[...].astype(F32) + r_ref[...].astype(F32)).astype(BF16)

    return pl.pallas_call(
        body, name=name,
        grid_spec=pltpu.PrefetchScalarGridSpec(
            num_scalar_prefetch=1, grid=(4,),
            in_specs=[pl.BlockSpec((None, rows, cols), lambda r, idx: (idx[r], 0, 0)),
                      pl.BlockSpec((None, rows, cols), lambda r, idx: (r, 0, 0))],
            out_specs=pl.BlockSpec((None, rows, cols), lambda r, idx: (r, 0, 0))),
        out_shape=jax.ShapeDtypeStruct((4, rows, cols), BF16),
        compiler_params=pltpu.CompilerParams(dimension_semantics=("parallel",)),
    )(own_blocks, grad, recv)


def _adamw_math(w, g, m, v):
    m = ADAM_B1 * m + (1.0 - ADAM_B1) * g
    v = ADAM_B2 * v + (1.0 - ADAM_B2) * jnp.square(g)
    m_hat = m / (1.0 - ADAM_B1 ** ADAM_STEP)
    v_hat = v / (1.0 - ADAM_B2 ** ADAM_STEP)
    delta = -ADAM_LR * (m_hat / (jnp.sqrt(v_hat) + ADAM_EPS) + ADAM_WD * w)
    return delta, m, v


def _adamw_sharded(pair_sums, recv, w, m, v, tr, name):
    rows, cols = w.shape
    tr = min(tr, rows)

    def body(h_ref, r_ref, w_ref, m_ref, v_ref, g_out, d_out, m_out, v_out):
        g = h_ref[...].astype(F32)
        for r in range(3):
            g = g + r_ref[r].astype(F32)
        g_out[...] = g
        d_out[...], m_out[...], v_out[...] = _adamw_math(w_ref[...], g, m_ref[...], v_ref[...])

    tile = pl.BlockSpec((tr, cols), lambda i: (i, 0))
    return pl.pallas_call(
        body, name=name,
        grid=(rows // tr,),
        in_specs=[pl.BlockSpec((None, tr, cols), lambda i: (0, i, 0)),
                  pl.BlockSpec((3, tr, cols), lambda i: (0, i, 0)), tile, tile, tile],
        out_specs=[tile] * 4,
        out_shape=[jax.ShapeDtypeStruct((rows, cols), F32)] * 4,
        compiler_params=pltpu.CompilerParams(dimension_semantics=("parallel",)),
    )(pair_sums, recv, w, m, v)


SMALL_ROWS = 16


def _pack_rows(dg_mix, dg_mem, dg_mlp, dqg, dkg, dconv, lsum):
    def body(a_ref, b_ref, c_ref, q_ref, k_ref, cv_ref, l_ref, o_ref):
        o_ref[...] = jnp.zeros_like(o_ref)
        for r, ref in enumerate((a_ref, b_ref, c_ref)):
            o_ref[r:r + 1, :] = ref[...]
        o_ref[3:4, :X_HEAD_DIM] = q_ref[...]
        o_ref[4:5, :X_HEAD_DIM] = k_ref[...]
        o_ref[5:8, :] = cv_ref[...]
        o_ref[8:9, :] = l_ref[...]

    return pl.pallas_call(body, name="small_pack", out_shape=jax.ShapeDtypeStruct((SMALL_ROWS, D_MODEL), F32))(
        dg_mix, dg_mem, dg_mlp, dqg, dkg, dconv, lsum)


def _small_sum(gathered):
    def body(g_ref, o_ref):
        total = g_ref[0]
        for dev in range(1, N_DEV):
            total = total + g_ref[dev]
        o_ref[...] = jnp.zeros_like(o_ref)
        for piece in range(5):
            o_ref[piece * SMALL_TILE:piece * SMALL_TILE + 1, :] = total[piece:piece + 1]
        o_ref[5 * SMALL_TILE:5 * SMALL_TILE + 3, :] = total[5:8]
        o_ref[6 * SMALL_TILE:6 * SMALL_TILE + 1, :] = total[8:9]

    return pl.pallas_call(body, name="small_grad_sum",
                          out_shape=jax.ShapeDtypeStruct((7 * SMALL_TILE, D_MODEL), F32))(gathered)


def _adamw_small(w, g, m, v):
    def body(w_ref, g_ref, m_ref, v_ref, d_out, m_out, v_out):
        d_out[...], m_out[...], v_out[...] = _adamw_math(w_ref[...], g_ref[...], m_ref[...], v_ref[...])

    return pl.pallas_call(body, name="adamw_small", out_shape=[jax.ShapeDtypeStruct(w.shape, F32)] * 3)(w, g, m, v)


def _pad_tile(a):
    return jnp.pad(a, ((0, SMALL_TILE - a.shape[0]), (0, D_MODEL - a.shape[1])))


def _pack_small(*pieces):
    return jnp.concatenate([_pad_tile(a) for a in pieces], axis=0)


def kernel(x, mem, g_mix, g_mem, w_in, conv_w, w_conv_out, w_sb_out, q_norm_g, k_norm_g, w_mem_kv, w_x_out, w_out, g_mlp, w_up, w_down, loss_target, m_g_mix, m_g_mem, m_w_in, m_conv_w, m_w_conv_out, m_w_sb_out, m_q_norm_g, m_k_norm_g, m_w_mem_kv, m_w_x_out, m_w_out, m_g_mlp, m_w_up, m_w_down, v_g_mix, v_g_mem, v_w_in, v_conv_w, v_w_conv_out, v_w_sb_out, v_q_norm_g, v_k_norm_g, v_w_mem_kv, v_w_x_out, v_w_out, v_g_mlp, v_w_up, v_w_down):
    xpos, ypos, cpos = _mesh_pos()
    me = 4 * xpos + 2 * ypos + cpos
    x2d, mem2d, tgt2d = x[0], mem[0], loss_target[0]
    t = x2d.shape[0]
    tm = min(512, t)
    tm_s = min(256, t)

    big = {
        "w_in": (w_in[0], m_w_in[0], v_w_in[0]),
        "w_conv_out": (w_conv_out[0], m_w_conv_out[0], v_w_conv_out[0]),
        "w_sb_out": (w_sb_out[0], m_w_sb_out[0], v_w_sb_out[0]),
        "w_mem_kv": (w_mem_kv[0], m_w_mem_kv[0], v_w_mem_kv[0]),
        "w_x_out": (w_x_out[0], m_w_x_out[0], v_w_x_out[0]),
        "w_out": (w_out[0], m_w_out[0], v_w_out[0]),
        "w_up": (w_up[0], m_w_up[0], v_w_up[0]),
        "w_down": (w_down[0], m_w_down[0], v_w_down[0]),
    }
    late = [n for n in big if n != "w_in"]
    as_bf16 = lambda group: [big[n][0].astype(BF16) for n in group]
    conv_pad = jnp.pad(conv_w[0], ((0, 8 - 3), (0, 0)))

    proj, h, (w_in_all, conv_all) = _in_proj(x2d, g_mix, _arrival_blocks(xpos, ypos, cpos), tm,
                                             [big["w_in"][0].astype(BF16), conv_pad])
    conv_full = conv_all[:, :3, :].transpose(1, 0, 2).reshape(3, D_MODEL)
    a_conv = _conv_fwd(proj, conv_full, 256)
    tq = min(SB_QUERY_TILE, t)
    pairs = D_MODEL // SB_BLOCK

    def groups_t(cols):
        return cols.reshape(t // tq, tq, pairs, SB_BLOCK).transpose(2, 0, 3, 1)

    kt4 = groups_t(proj[:, COL_SK * D_MODEL:(COL_SK + 1) * D_MODEL])
    vt4 = groups_t(proj[:, COL_SV * D_MODEL:(COL_SV + 1) * D_MODEL])
    o_sb, sb_weights, sb_logits, gathered = _sb_fwd(proj, vt4, tq, as_bf16(late))
    full = dict(zip(late, gathered))
    wkv_all, w_up_all = full["w_mem_kv"], full["w_up"]
    rows_full = lambda a: a.reshape(a.shape[0] * a.shape[1], a.shape[2])
    wc, ws, wx, wo, wd = (rows_full(full[n]) for n in ("w_conv_out", "w_sb_out", "w_x_out", "w_out", "w_down"))
    mem_n, kn, vmem = _mem_prep(mem2d, g_mem, wkv_all, k_norm_g)
    o_x = _x_fwd(proj, q_norm_g, kn, vmem, tm_s)
    x1, y_conv, y_sb, y_x, merged = _merge_fwd(x2d, proj, a_conv, o_sb, o_x, wc, ws, wx, wo, tm_s)
    up, h2, dx2, lsum = _mlp_fwd(x1, g_mlp, w_up_all, wd, tgt2d, tm)

    dup, act, dx1, dg_mlp = _mlp_bwd(x1, g_mlp, w_up_all, wd, up, dx2, tm)
    dgate, dy_conv, dy_sb, dy_x, da_conv, do_sb, do_x = _merge_bwd(dx1, proj, y_conv, y_sb, y_x, wc, ws, wx, wo, tm_s)
    dch, dcb, dcc, dconv = _conv_bwd(proj, conv_full, da_conv, 256)
    wgrads = {
        "w_conv_out": _weight_grad(a_conv, dy_conv, D_MODEL, 512, "dw_conv_out"),
        "w_sb_out": _weight_grad(o_sb, dy_sb, D_MODEL, 512, "dw_sb_out"),
        "w_x_out": _weight_grad(o_x, dy_x, D_MODEL, 512, "dw_x_out"),
        "w_out": _weight_grad(merged, dx1, D_MODEL, 512, "dw_out"),
        "w_up": _weight_grad(h2, dup, w_up_all.shape[2], 512, "dw_up"),
        "w_down": _weight_grad(act, dx2, D_MODEL, 512, "dw_down"),
    }

    own_blocks = jnp.stack([4 * (xpos ^ dx) + 2 * (ypos ^ dy) + cpos for dx in (0, 1) for dy in (0, 1)]).astype(jnp.int32)
    blocked = lambda n: wgrads[n].reshape((N_DEV,) + big[n][0].shape)
    pair_sum = lambda n, from_sibling: _pair_sum(blocked(n), from_sibling, own_blocks, "pair_sum_" + n)

    behind_x = list(wgrads)
    dxq, dkn, dvm, dqg, from_sibling = _x_bwd(proj, q_norm_g, kn, vmem, do_x, tm_s, [blocked(n) for n in behind_x])
    pair_sums = {n: pair_sum(n, r) for n, r in zip(behind_x, from_sibling)}
    dkv, dg_mem, dkg = _mem_bwd(mem2d, g_mem, wkv_all, k_norm_g, dkn, dvm)
    wgrads["w_mem_kv"] = _weight_grad(mem_n, dkv, wkv_all.shape[2], 512, "dw_mem_kv")
    pair_sums["w_mem_kv"] = pair_sum("w_mem_kv", _pair_exchange([blocked("w_mem_kv")], "grad_pair_exchange_w_mem_kv")[0])
    dq, dk, dv, from_chips_late = _sb_bwd(proj, kt4, do_sb, o_sb, sb_weights, sb_logits, tq,
                                          [pair_sums[n] for n in late])
    from_chips = dict(zip(late, from_chips_late))
    dproj = jnp.concatenate([dch, dcb, dcc, dq, dk, dv, dxq, dgate], axis=1)
    wgrads["w_in"] = _weight_grad(h, dproj, w_in_all.shape[2], 512, "dw_in")
    pair_sums["w_in"] = pair_sum("w_in", _pair_exchange([blocked("w_in")], "grad_pair_exchange_w_in")[0])
    grad_x, dg_mix, (from_chips["w_in"],) = _in_proj_bwd(x2d, g_mix, w_in_all, dproj, dx1, tm, [pair_sums["w_in"]])
    res = {}
    for n in big:
        w_sh, m_sh, v_sh = big[n]
        res[n] = _adamw_sharded(pair_sums[n], from_chips[n], w_sh, m_sh, v_sh, 256, "adamw_" + n)

    part = _pack_rows(dg_mix, dg_mem, dg_mlp, dqg, dkg, dconv, lsum)
    gsum = _small_sum(_small_all_gather(part))
    loss = 0.5 * jnp.sum(gsum[6 * SMALL_TILE]) / D_MODEL
    conv_cols = lax.dynamic_slice(gsum[5 * SMALL_TILE:6 * SMALL_TILE], (0, me * (D_MODEL // N_DEV)),
                                  (SMALL_TILE, D_MODEL // N_DEV))
    g_small = jnp.concatenate([gsum[:5 * SMALL_TILE], _pad_tile(conv_cols)], axis=0)
    w_small = _pack_small(g_mix, g_mem, g_mlp, q_norm_g, k_norm_g, conv_w[0])
    m_small = _pack_small(m_g_mix, m_g_mem, m_g_mlp, m_q_norm_g, m_k_norm_g, m_conv_w[0])
    v_small = _pack_small(v_g_mix, v_g_mem, v_g_mlp, v_q_norm_g, v_k_norm_g, v_conv_w[0])
    d_small, nm_small, nv_small = _adamw_small(w_small, g_small, m_small, v_small)

    def unpack(p):
        return {"g_mix": p[0:1], "g_mem": p[8:9], "g_mlp": p[16:17], "q_norm_g": p[24:25, :X_HEAD_DIM],
                "k_norm_g": p[32:33, :X_HEAD_DIM], "conv_w": p[40:43, :D_MODEL // N_DEV][None]}

    small = [unpack(p) for p in (g_small, d_small, nm_small, nv_small)]
    order = ["g_mix", "g_mem", "w_in", "conv_w", "w_conv_out", "w_sb_out", "q_norm_g", "k_norm_g", "w_mem_kv",
             "w_x_out", "w_out", "g_mlp", "w_up", "w_down"]
    outs = [loss, grad_x[None]]
    for kind in range(4):
        for n in order:
            outs.append(res[n][kind][None] if n in res else small[kind][n])
    return tuple(outs)
```

```python
import jax
import jax.numpy as jnp
from jax import lax
from jax.experimental import pallas as pl
from jax.experimental.pallas import tpu as pltpu

F32 = jnp.float32
BF16 = jnp.bfloat16
MESH = pl.DeviceIdType.MESH

EPS = 1e-6
N_DEV = 8
D_MODEL = 1024
SB_HEAD_DIM = 64
SB_BLOCK = 128
SB_QUERY_TILE = 512
X_HEADS = 4
X_HEAD_DIM = 256
N_BRANCH = 3
COL_CH, COL_CB, COL_CC, COL_SQ, COL_SK, COL_SV, COL_XQ, COL_GATE = 0, 1, 2, 3, 4, 5, 6, 7

ADAM_LR = 0.001
ADAM_B1 = 0.9
ADAM_B2 = 0.999
ADAM_EPS = 1e-08
ADAM_WD = 0.01
ADAM_STEP = 10

SMALL_TILE = 8


def _dot(a, b):
    return jnp.dot(a, b, preferred_element_type=F32)


def _dot_nt(a, b):
    return lax.dot_general(a, b, (((1,), (1,)), ((), ())), preferred_element_type=F32)


def _dot_tn(a, b):
    return lax.dot_general(a, b, (((0,), (0,)), ((), ())), preferred_element_type=F32)


def _rstd(xf):
    return lax.rsqrt(jnp.mean(xf * xf, axis=-1, keepdims=True) + EPS)


def _sigmoid(z):
    return 1.0 / (1.0 + jnp.exp(-z))


def _log_sigmoid(z):
    return jnp.minimum(z, 0.0) - jnp.log(1.0 + jnp.exp(-jnp.abs(z)))


def _rms_bwd(dy, xhat, r, g):
    dxhat = dy * g
    return r * (dxhat - xhat * jnp.mean(dxhat * xhat, axis=-1, keepdims=True))


def _mesh_pos():
    return lax.axis_index("x"), lax.axis_index("y"), lax.axis_index("c")


ANY = pl.BlockSpec(memory_space=pl.ANY)


def _gather_shapes(shards):
    return [jax.ShapeDtypeStruct((N_DEV,) + s.shape, s.dtype) for s in shards]


def _gather_sems(n):
    return [pltpu.SemaphoreType.DMA((n, 7)), pltpu.SemaphoreType.DMA((n, 7)), pltpu.SemaphoreType.DMA((n,))]


def _gather_phases(ins, outs, send_sems, recv_sems, local_sems, by_arrival=False):
    n = len(ins)
    x, y, c = _mesh_pos()
    me, sibling = (x, y, c), (x, y, 1 - c)
    chips = [(1 - x, y), (x, 1 - y), (1 - x, 1 - y)]

    def blk(a, px, py, pc):
        return outs[a].at[4 * px + 2 * py + pc]

    def copy(a, k, block, to, src=None):
        return pltpu.make_async_remote_copy(
            src_ref=blk(a, *block) if src is None else src, dst_ref=blk(a, *block),
            send_sem=send_sems.at[a, k], recv_sem=recv_sems.at[a, k], device_id=to, device_id_type=MESH)

    def local(a):
        return pltpu.make_async_copy(ins[a], blk(a, *me), local_sems.at[a])

    def own(a):
        return [copy(a, 0, me, sibling, src=ins[a])] + [copy(a, 1 + j, me, (*chips[j], c), src=ins[a]) for j in range(2)]

    def onward(a, via):
        return copy(a, 3, (*chips[via], c), (*chips[1 - via], c))

    def begin():
        for a in range(n):
            local(a).start()
        for a in range(n):
            for cp in own(a):
                cp.start()

    def arrive(order):
        for a in range(n):
            if order == 0:
                copy(a, 0, sibling, me).wait_recv()
            elif order <= 3:
                chip = chips[order - 1]
                copy(a, order, (*chip, c), me).wait_recv()
                copy(a, 3 + order, (*chip, c), sibling).start()
                if order <= 2:
                    pl.when(c == order - 1)(onward(a, order - 1).start)
            else:
                copy(a, order, (*chips[order - 4], 1 - c), me).wait_recv()

    def relay():
        for order in (1, 2):
            arrive(order)

    def drain():
        for a in range(n):
            for cp in own(a):
                cp.wait_send()
            for via in range(2):
                pl.when(c == via)(onward(a, via).wait_send)
            for j, chip in enumerate(chips):
                copy(a, 4 + j, (*chip, c), sibling).wait_send()
            local(a).wait()

    def finish():
        for order in (3, 0, 4, 5, 6):
            arrive(order)
        drain()

    if by_arrival:
        return begin, arrive, drain
    return begin, relay, finish


def _pair_exchange(grads, name):
    n = len(grads)

    def body(*refs):
        begin, finish = _pair_exchange_phases(refs[:n], refs[n:2 * n], *refs[2 * n:])
        begin()
        finish()

    return pl.pallas_call(
        body, name=name,
        out_shape=_pair_exchange_shapes(grads),
        in_specs=[ANY] * n, out_specs=[ANY] * n,
        scratch_shapes=_pair_exchange_sems(n),
    )(*grads)


def _pair_exchange_shapes(grads):
    return [jax.ShapeDtypeStruct((4,) + g.shape[1:], g.dtype) for g in grads]


def _pair_exchange_sems(n):
    return [pltpu.SemaphoreType.DMA((n, 4)), pltpu.SemaphoreType.DMA((n, 4))]


def _pair_exchange_phases(ins, outs, send_sems, recv_sems):
    x, y, c = _mesh_pos()
    xs, ys = (x, 1 - x), (y, 1 - y)

    def copies():
        out = []
        for a in range(len(ins)):
            for r in range(4):
                dx, dy = divmod(r, 2)
                out.append(pltpu.make_async_remote_copy(
                    src_ref=ins[a].at[4 * xs[dx] + 2 * ys[dy] + (1 - c)], dst_ref=outs[a].at[r],
                    send_sem=send_sems.at[a, r], recv_sem=recv_sems.at[a, r],
                    device_id=(x, y, 1 - c), device_id_type=MESH))
        return out

    def begin():
        for cp in copies():
            cp.start()

    def finish():
        for cp in copies():
            cp.wait()

    return begin, finish


def _chip_exchange_shapes(sums):
    return [jax.ShapeDtypeStruct((3,) + s.shape[1:], s.dtype) for s in sums]


def _chip_exchange_sems(n):
    return [pltpu.SemaphoreType.DMA((n, 3)), pltpu.SemaphoreType.DMA((n, 3))]


def _chip_exchange_phases(ins, outs, send_sems, recv_sems):
    x, y, c = _mesh_pos()
    xs, ys = (x, 1 - x), (y, 1 - y)

    def copies():
        out = []
        for a in range(len(ins)):
            for r in range(1, 4):
                dx, dy = divmod(r, 2)
                out.append(pltpu.make_async_remote_copy(
                    src_ref=ins[a].at[r], dst_ref=outs[a].at[r - 1],
                    send_sem=send_sems.at[a, r - 1], recv_sem=recv_sems.at[a, r - 1],
                    device_id=(xs[dx], ys[dy], c), device_id_type=MESH))
        return out

    def begin():
        for cp in copies():
            cp.start()

    def finish():
        for cp in copies():
            cp.wait()

    return begin, finish


def _small_all_gather(part):
    rows, cols = part.shape

    def body(in_ref, out_ref, send_sems, recv_sems):
        x, y, c = _mesh_pos()
        xs, ys, cs = (x, 1 - x), (y, 1 - y), (c, 1 - c)
        out_ref[4 * x + 2 * y + c] = in_ref[...]
        copies = []
        for k in range(1, N_DEV):
            dx, dy, dc = k // 4, (k // 2) % 2, k % 2
            copies.append((
                pltpu.make_async_remote_copy(
                    src_ref=in_ref, dst_ref=out_ref.at[4 * x + 2 * y + c],
                    send_sem=send_sems.at[k - 1], recv_sem=recv_sems.at[k - 1],
                    device_id=(xs[dx], ys[dy], cs[dc]), device_id_type=MESH),
                pltpu.make_async_remote_copy(
                    src_ref=in_ref, dst_ref=out_ref.at[4 * xs[dx] + 2 * ys[dy] + cs[dc]],
                    send_sem=send_sems.at[k - 1], recv_sem=recv_sems.at[k - 1],
                    device_id=(xs[dx], ys[dy], cs[dc]), device_id_type=MESH)))
        for send, _ in copies:
            send.start()
        for send, recv in copies:
            recv.wait_recv()
            send.wait_send()

    return pl.pallas_call(
        body, name="small_all_gather",
        out_shape=jax.ShapeDtypeStruct((N_DEV, rows, cols), part.dtype),
        in_specs=[pl.BlockSpec(memory_space=pltpu.VMEM)],
        out_specs=pl.BlockSpec(memory_space=pltpu.VMEM),
        scratch_shapes=[pltpu.SemaphoreType.DMA((N_DEV - 1,)), pltpu.SemaphoreType.DMA((N_DEV - 1,))],
    )(part)


ARRIVAL_ORDER = (0, 1, 2, 4, 5, 3, 6)


def _arrival_blocks(xpos, ypos, cpos):
    chips = [(1 - xpos, ypos), (xpos, 1 - ypos), (1 - xpos, 1 - ypos)]
    by_order = ([4 * xpos + 2 * ypos + (1 - cpos)] + [4 * cx + 2 * cy + cpos for cx, cy in chips]
                + [4 * cx + 2 * cy + (1 - cpos) for cx, cy in chips])
    return jnp.stack([4 * xpos + 2 * ypos + cpos] + [by_order[o] for o in ARRIVAL_ORDER]).astype(jnp.int32)


def _in_proj(x, g_mix, arrival_blocks, tm, shards):
    t, d = x.shape
    bw = shards[0].shape[1]
    nt = t // tm
    n = len(shards)

    def body(blocks_ref, x_ref, g_ref, *rest):
        w_shard = rest[0]
        proj_ref, h_ref = rest[n:n + 2]
        w_all = rest[n + 2]
        h_scr, w_buf, fetch_sems = rest[2 * n + 2:2 * n + 5]
        begin, arrive, drain = _gather_phases(rest[:n], rest[n + 2:2 * n + 2], *rest[2 * n + 5:], by_arrival=True)
        j, i = pl.program_id(0), pl.program_id(1)
        slot = lax.rem(j, 2)

        def fetch(src, into):
            return pltpu.make_async_copy(src, w_buf.at[into], fetch_sems.at[into])

        @pl.when((j == 0) & (i == 0))
        def _():
            begin()
            fetch(w_shard, 0).start()

        @pl.when(j == 0)
        def _():
            xf = x_ref[...]
            hv = (xf * _rstd(xf) * g_ref[...]).astype(BF16)
            h_ref[...] = hv
            h_scr[pl.ds(pl.multiple_of(i * tm, tm), tm), :] = hv

        @pl.when(i == 0)
        def _():
            fetch(w_shard, slot).wait()

        proj_ref[...] = _dot(h_scr[pl.ds(pl.multiple_of(i * tm, tm), tm), :], w_buf[slot]).astype(BF16)

        for nxt in range(1, N_DEV):
            @pl.when((i == nt - 1) & (j == nxt - 1))
            def _():
                arrive(ARRIVAL_ORDER[nxt - 1])
                fetch(w_all.at[blocks_ref[nxt]], 1 - slot).start()

        pl.when((i == nt - 1) & (j == N_DEV - 1))(drain)

    first_pass = lambda j, i, blocks: (jnp.where(j == 0, i, nt - 1), 0)
    outs = pl.pallas_call(
        body, name="in_proj",
        grid_spec=pltpu.PrefetchScalarGridSpec(
            num_scalar_prefetch=1, grid=(N_DEV, nt),
            in_specs=[pl.BlockSpec((tm, d), first_pass), pl.BlockSpec((1, d), lambda j, i, blocks: (0, 0))] + [ANY] * n,
            out_specs=[pl.BlockSpec((tm, bw), lambda j, i, blocks: (i, blocks[j])),
                       pl.BlockSpec((tm, d), first_pass)] + [ANY] * n,
            scratch_shapes=[pltpu.VMEM((t, d), BF16), pltpu.VMEM((2, d, bw), BF16), pltpu.SemaphoreType.DMA((2,))]
            + _gather_sems(n)),
        out_shape=[jax.ShapeDtypeStruct((t, N_DEV * bw), BF16), jax.ShapeDtypeStruct((t, d), BF16)] + _gather_shapes(shards),
        compiler_params=pltpu.CompilerParams(dimension_semantics=("arbitrary", "arbitrary")),
    )(arrival_blocks, x, g_mix, *shards)
    return outs[0], outs[1], outs[2:]


def _conv_terms(ch_ref, cb_ref, cc_ref, w_ref):
    ch, cb, cc = ch_ref[...].astype(F32), cb_ref[...].astype(F32), cc_ref[...].astype(F32)
    u = cc * ch
    row = lax.broadcasted_iota(jnp.int32, u.shape, 0)
    u1 = jnp.where(row >= 1, pltpu.roll(u, 1, 0), 0.0)
    u2 = jnp.where(row >= 2, pltpu.roll(u, 2, 0), 0.0)
    w = (w_ref[0:1, :], w_ref[1:2, :], w_ref[2:3, :])
    cv = w[2] * u + w[1] * u1 + w[0] * u2
    return ch, cb, cc, u, u1, u2, cv, w, row


def _conv_fwd(proj, conv_w, cw):
    t = proj.shape[0]
    nper = D_MODEL // cw

    def body(ch_ref, cb_ref, cc_ref, w_ref, a_ref):
        _, cb, _, _, _, _, cv, _, _ = _conv_terms(ch_ref, cb_ref, cc_ref, w_ref)
        a_ref[...] = (cb * cv).astype(BF16)

    def col(piece):
        return pl.BlockSpec((t, cw), lambda j: (0, piece * nper + j))

    return pl.pallas_call(
        body, name="conv_fwd",
        grid=(nper,),
        in_specs=[col(COL_CH), col(COL_CB), col(COL_CC), pl.BlockSpec((3, cw), lambda j: (0, j))],
        out_specs=pl.BlockSpec((t, cw), lambda j: (0, j)),
        out_shape=jax.ShapeDtypeStruct((t, D_MODEL), BF16),
        compiler_params=pltpu.CompilerParams(dimension_semantics=("parallel",)),
    )(proj, proj, proj, conv_w)


def _scan_matrix():
    s = lax.broadcasted_iota(jnp.int32, (SB_BLOCK, SB_BLOCK), 0)
    j = lax.broadcasted_iota(jnp.int32, (SB_BLOCK, SB_BLOCK), 1)
    return jnp.where(j > s, 1.0, 0.0).astype(BF16)


def _suffix_sum(u_mat, xv):
    hi = xv.astype(BF16)
    lo = (xv - hi.astype(F32)).astype(BF16)
    return _dot(u_mat, hi) + _dot(u_mat, lo)


def _head_rows(vt, h):
    row = lax.broadcasted_iota(jnp.int32, vt.shape, 0)
    return jnp.where((row >= h * SB_HEAD_DIM) & (row < (h + 1) * SB_HEAD_DIM), vt, 0.0).astype(BF16)


def _head_lanes(v, h):
    lane = lax.broadcasted_iota(jnp.int32, v.shape, 1)
    return jnp.where((lane >= h * SB_HEAD_DIM) & (lane < (h + 1) * SB_HEAD_DIM), v, 0.0).astype(BF16)


def _group_suffix(u_mat, xv, carry):
    nblk = xv.shape[0] // SB_BLOCK
    parts = [None] * nblk
    for j in reversed(range(nblk)):
        xj = xv[j * SB_BLOCK:(j + 1) * SB_BLOCK]
        parts[j] = _suffix_sum(u_mat, xj) + carry
        carry = carry + jnp.sum(xj, axis=0, keepdims=True)
    return jnp.concatenate(parts, axis=0), carry


def _sb_probs(kgrp, qt_h, u_mat, carry, past):
    z = _dot(kgrp, qt_h)
    lb = _log_sigmoid(z)
    l1 = lb - z
    if past is not None:
        l1 = jnp.where(past, l1, 0.0)
    between, carry = _group_suffix(u_mat, l1, carry)
    a = jnp.exp(lb + between)
    if past is not None:
        a = jnp.where(past, a, 0.0)
    return a, z, carry


def _sb_schedule(nq):
    steps = [(qi, g) for qi in range(nq) for g in range(qi, -1, -1)]
    return jnp.asarray([s[0] for s in steps], jnp.int32), jnp.asarray([s[1] for s in steps], jnp.int32)


def _past_mask(tq):
    return lax.broadcasted_iota(jnp.int32, (tq, tq), 0) < lax.broadcasted_iota(jnp.int32, (tq, tq), 1)


def _sb_fwd(proj, vt4, tq, shards):
    t = proj.shape[0]
    pairs = D_MODEL // SB_BLOCK
    nq = t // tq
    qi_tab, g_tab = _sb_schedule(nq)
    ns = qi_tab.shape[0]
    n = len(shards)

    def body(qi_ref, g_ref, q_ref, k_ref, vt_ref, *rest):
        o_ref, a_ref, z_ref = rest[n:n + 3]
        acc_ref, carry_ref, qt_ref = rest[2 * n + 3:2 * n + 6]
        begin, relay, finish = _gather_phases(rest[:n], rest[n + 3:2 * n + 3], *rest[2 * n + 6:])
        pi, si = pl.program_id(0), pl.program_id(1)
        diagonal = g_ref[si] == qi_ref[si]
        pl.when((pi == 0) & (si == 0))(begin)
        pl.when((pi == pairs // 2) & (si == 0))(relay)

        @pl.when(diagonal)
        def _():
            acc_ref[...] = jnp.zeros_like(acc_ref)
            carry_ref[...] = jnp.zeros_like(carry_ref)
            qt = q_ref[...].astype(F32).T * (SB_HEAD_DIM ** -0.5)
            for h in range(2):
                qt_ref[h] = _head_rows(qt, h)

        def step(past):
            u_mat = _scan_matrix()
            for h in range(2):
                a, z, carry = _sb_probs(k_ref[...], qt_ref[h], u_mat, carry_ref[h], past)
                ab = a.astype(BF16)
                a_ref[h] = ab
                z_ref[h] = z.astype(BF16)
                acc_ref[h] += _dot(vt_ref[h * SB_HEAD_DIM:(h + 1) * SB_HEAD_DIM, :], ab)
                carry_ref[h] = carry

        pl.when(diagonal)(lambda: step(_past_mask(tq)))
        pl.when(jnp.logical_not(diagonal))(lambda: step(None))

        @pl.when(g_ref[si] == 0)
        def _():
            o_ref[...] = jnp.concatenate([acc_ref[0], acc_ref[1]], axis=0).T

        pl.when((pi == pairs - 1) & (si == ns - 1))(finish)

    tile = pl.BlockSpec((None, None, 2, tq, tq), lambda p, s, qt_, gt_: (p, s, 0, 0, 0))
    tiles = jax.ShapeDtypeStruct((pairs, ns, 2, tq, tq), BF16)
    outs = pl.pallas_call(
        body, name="sb_fwd",
        grid_spec=pltpu.PrefetchScalarGridSpec(
            num_scalar_prefetch=2, grid=(pairs, ns),
            in_specs=[pl.BlockSpec((tq, SB_BLOCK), lambda p, s, qt_, gt_: (qt_[s], COL_SQ * pairs + p)),
                      pl.BlockSpec((tq, SB_BLOCK), lambda p, s, qt_, gt_: (gt_[s], COL_SK * pairs + p)),
                      pl.BlockSpec((None, None, SB_BLOCK, tq), lambda p, s, qt_, gt_: (p, gt_[s], 0, 0))] + [ANY] * n,
            out_specs=[pl.BlockSpec((tq, SB_BLOCK), lambda p, s, qt_, gt_: (qt_[s], p)), tile, tile] + [ANY] * n,
            scratch_shapes=[pltpu.VMEM((2, SB_HEAD_DIM, tq), F32), pltpu.VMEM((2, 1, tq), F32),
                            pltpu.VMEM((2, SB_BLOCK, tq), BF16)] + _gather_sems(n)),
        out_shape=[jax.ShapeDtypeStruct((t, D_MODEL), F32), tiles, tiles] + _gather_shapes(shards),
        compiler_params=pltpu.CompilerParams(dimension_semantics=("arbitrary", "arbitrary")),
    )(qi_tab, g_tab, proj, proj, vt4, *shards)
    return outs[0], outs[1], outs[2], outs[3:]


def _mem_prep(mem, g_mem, wkv_all, k_norm_g):
    m, d = mem.shape

    def body(mem_ref, g_ref, w_ref, kg_ref, memn_ref, kn_ref, v_ref):
        memf = mem_ref[...]
        memn = (memf * _rstd(memf) * g_ref[...]).astype(BF16)
        memn_ref[...] = memn
        for b in range(N_DEV):
            kv = _dot(memn, w_ref[b])
            if b < X_HEADS:
                kn_ref[:, b * X_HEAD_DIM:(b + 1) * X_HEAD_DIM] = (kv * _rstd(kv) * kg_ref[...]).astype(BF16)
            else:
                h = b - X_HEADS
                v_ref[:, h * X_HEAD_DIM:(h + 1) * X_HEAD_DIM] = kv.astype(BF16)

    return pl.pallas_call(
        body, name="mem_prep",
        out_shape=[jax.ShapeDtypeStruct((m, d), BF16)] * 3,
    )(mem, g_mem, wkv_all, k_norm_g)


def _x_head(xq_ref, qg, kn_ref, h):
    sl = slice(h * X_HEAD_DIM, (h + 1) * X_HEAD_DIM)
    q = xq_ref[:, sl].astype(F32)
    rq = _rstd(q)
    qhat = q * rq
    qn = (qhat * qg).astype(BF16)
    s = _dot_nt(qn, kn_ref[:, sl]) * (X_HEAD_DIM ** -0.5)
    e = jnp.exp(s - jnp.max(s, axis=-1, keepdims=True))
    p = e / jnp.sum(e, axis=-1, keepdims=True)
    return sl, rq, qhat, qn, p


def _x_fwd(proj, q_norm_g, kn, v, tm):
    t = proj.shape[0]
    m = kn.shape[0]

    def body(xq_ref, qg_ref, kn_ref, v_ref, o_ref):
        for h in range(X_HEADS):
            sl, _, _, _, p = _x_head(xq_ref, qg_ref[...], kn_ref, h)
            o_ref[:, sl] = _dot(p.astype(BF16), v_ref[:, sl]).astype(BF16)

    return pl.pallas_call(
        body, name="x_fwd",
        grid=(t // tm,),
        in_specs=[pl.BlockSpec((tm, D_MODEL), lambda i: (i, COL_XQ)),
                  pl.BlockSpec((1, X_HEAD_DIM), lambda i: (0, 0)),
                  pl.BlockSpec((m, D_MODEL), lambda i: (0, 0)),
                  pl.BlockSpec((m, D_MODEL), lambda i: (0, 0))],
        out_specs=pl.BlockSpec((tm, D_MODEL), lambda i: (i, 0)),
        out_shape=jax.ShapeDtypeStruct((t, D_MODEL), BF16),
        compiler_params=pltpu.CompilerParams(dimension_semantics=("parallel",)),
    )(proj, q_norm_g, kn, v)


def _gate_spec(tm, branch):
    return pl.BlockSpec((tm, D_MODEL), lambda i: (i, COL_GATE + branch))


def _merge_fwd(x, proj, a_conv, o_sb, o_x, w_conv_out, w_sb_out, w_x_out, w_out, tm):
    t, d = x.shape

    def body(x_ref, g0_ref, g1_ref, g2_ref, a_ref, s_ref, xo_ref, wc_ref, ws_ref, wx_ref, wo_ref,
             x1_ref, yc_ref, ys_ref, yx_ref, mg_ref):
        merged = jnp.zeros((tm, d), F32)
        for gate_ref, b_ref, w_ref, y_ref in ((g0_ref, a_ref, wc_ref, yc_ref), (g1_ref, s_ref, ws_ref, ys_ref),
                                              (g2_ref, xo_ref, wx_ref, yx_ref)):
            yv = _dot(b_ref[...].astype(BF16), w_ref[...])
            y_ref[...] = yv.astype(BF16)
            merged = merged + _sigmoid(gate_ref[...].astype(F32)) * yv
        mb = merged.astype(BF16)
        mg_ref[...] = mb
        x1_ref[...] = x_ref[...] + _dot(mb, wo_ref[...])

    tile = pl.BlockSpec((tm, d), lambda i: (i, 0))
    wfull = pl.BlockSpec((d, d), lambda i: (0, 0))
    return pl.pallas_call(
        body, name="merge_fwd",
        grid=(t // tm,),
        in_specs=[tile] + [_gate_spec(tm, b) for b in range(N_BRANCH)] + [tile, tile, tile,
                                                                           wfull, wfull, wfull, wfull],
        out_specs=[tile] * 5,
        out_shape=[jax.ShapeDtypeStruct((t, d), F32)] + [jax.ShapeDtypeStruct((t, d), BF16)] * 4,
        compiler_params=pltpu.CompilerParams(dimension_semantics=("parallel",)),
    )(x, proj, proj, proj, a_conv, o_sb, o_x, w_conv_out, w_sb_out, w_x_out, w_out)


def _mlp_fwd(x1, g_mlp, w_up_all, w_down, target, tm):
    t, d = x1.shape
    nb, _, fw = w_up_all.shape

    def body(x1_ref, g_ref, wu_ref, wd_ref, tgt_ref, up_ref, h2_ref, dx2_ref, lsum_ref, acc_ref):
        i, j = pl.program_id(0), pl.program_id(1)

        @pl.when(j == 0)
        def _():
            xf = x1_ref[...]
            h2_ref[...] = (xf * _rstd(xf) * g_ref[...]).astype(BF16)
            acc_ref[...] = jnp.zeros_like(acc_ref)

        @pl.when((i == 0) & (j == 0))
        def _():
            lsum_ref[...] = jnp.zeros_like(lsum_ref)

        up = _dot(h2_ref[...], wu_ref[...])
        up_ref[...] = up.astype(BF16)
        act = jnp.square(jnp.maximum(up, 0.0)).astype(BF16)
        acc_ref[...] += _dot(act, wd_ref[...])

        @pl.when(j == nb - 1)
        def _():
            diff = x1_ref[...] + acc_ref[...] - tgt_ref[...]
            dx2_ref[...] = diff * (1.0 / d)
            lsum_ref[...] += jnp.sum(diff * diff, axis=0, keepdims=True)

    tile = pl.BlockSpec((tm, d), lambda i, j: (i, 0))
    row = pl.BlockSpec((1, d), lambda i, j: (0, 0))
    return pl.pallas_call(
        body, name="mlp_fwd",
        grid=(t // tm, nb),
        in_specs=[tile, row, pl.BlockSpec((None, d, fw), lambda i, j: (j, 0, 0)),
                  pl.BlockSpec((fw, d), lambda i, j: (j, 0)), tile],
        out_specs=[pl.BlockSpec((tm, fw), lambda i, j: (i, j)), tile, tile, row],
        out_shape=[jax.ShapeDtypeStruct((t, nb * fw), BF16), jax.ShapeDtypeStruct((t, d), BF16),
                   jax.ShapeDtypeStruct((t, d), F32), jax.ShapeDtypeStruct((1, d), F32)],
        scratch_shapes=[pltpu.VMEM((tm, d), F32)],
        compiler_params=pltpu.CompilerParams(dimension_semantics=("arbitrary", "arbitrary")),
    )(x1, g_mlp, w_up_all, w_down, target)


def _mlp_bwd(x1, g_mlp, w_up_all, w_down, up, dx2, tm):
    t, d = x1.shape
    nb, _, fw = w_up_all.shape

    def body(x1_ref, g_ref, wu_ref, wd_ref, up_ref, dx2_ref, dup_ref, act_ref, dx1_ref, dg_ref, acc_ref, dyb_ref):
        i, j = pl.program_id(0), pl.program_id(1)

        @pl.when(j == 0)
        def _():
            dyb_ref[...] = dx2_ref[...].astype(BF16)
            acc_ref[...] = jnp.zeros_like(acc_ref)

        @pl.when((i == 0) & (j == 0))
        def _():
            dg_ref[...] = jnp.zeros_like(dg_ref)

        r = jnp.maximum(up_ref[...].astype(F32), 0.0)
        act_ref[...] = jnp.square(r).astype(BF16)
        dup = (_dot_nt(dyb_ref[...], wd_ref[...]) * (2.0 * r)).astype(BF16)
        dup_ref[...] = dup
        acc_ref[...] += _dot_nt(dup, wu_ref[...])

        @pl.when(j == nb - 1)
        def _():
            xf = x1_ref[...]
            rs = _rstd(xf)
            xhat = xf * rs
            dh2 = acc_ref[...]
            dg_ref[...] += jnp.sum(dh2 * xhat, axis=0, keepdims=True)
            dx1_ref[...] = dx2_ref[...] + _rms_bwd(dh2, xhat, rs, g_ref[...])

    tile = pl.BlockSpec((tm, d), lambda i, j: (i, 0))
    row = pl.BlockSpec((1, d), lambda i, j: (0, 0))
    ff = pl.BlockSpec((tm, fw), lambda i, j: (i, j))
    return pl.pallas_call(
        body, name="mlp_bwd",
        grid=(t // tm, nb),
        in_specs=[tile, row, pl.BlockSpec((None, d, fw), lambda i, j: (j, 0, 0)),
                  pl.BlockSpec((fw, d), lambda i, j: (j, 0)), ff, tile],
        out_specs=[ff, ff, tile, row],
        out_shape=[jax.ShapeDtypeStruct((t, nb * fw), BF16), jax.ShapeDtypeStruct((t, nb * fw), BF16),
                   jax.ShapeDtypeStruct((t, d), F32), jax.ShapeDtypeStruct((1, d), F32)],
        scratch_shapes=[pltpu.VMEM((tm, d), F32), pltpu.VMEM((tm, d), BF16)],
        compiler_params=pltpu.CompilerParams(dimension_semantics=("arbitrary", "arbitrary")),
    )(x1, g_mlp, w_up_all, w_down, up, dx2)


def _merge_bwd(dx1, proj, y_conv, y_sb, y_x, w_conv_out, w_sb_out, w_x_out, w_out, tm):
    t, d = dx1.shape

    def body(dx1_ref, g0_ref, g1_ref, g2_ref, yc_ref, ys_ref, yx_ref, wc_ref, ws_ref, wx_ref, wo_ref,
             dgate_ref, dyc_ref, dys_ref, dyx_ref, da_ref, dos_ref, dox_ref):
        dm = _dot_nt(dx1_ref[...].astype(BF16), wo_ref[...])
        for i, (gate_ref, y_ref, w_ref, dy_ref, db_ref) in enumerate(((g0_ref, yc_ref, wc_ref, dyc_ref, da_ref),
                                                                       (g1_ref, ys_ref, ws_ref, dys_ref, dos_ref),
                                                                       (g2_ref, yx_ref, wx_ref, dyx_ref, dox_ref))):
            gt = _sigmoid(gate_ref[...].astype(F32))
            dy = (dm * gt).astype(BF16)
            dy_ref[...] = dy
            dgate_ref[:, i * d:(i + 1) * d] = (dm * y_ref[...].astype(F32) * gt * (1.0 - gt)).astype(BF16)
            db_ref[...] = _dot_nt(dy, w_ref[...]).astype(BF16)

    tile = pl.BlockSpec((tm, d), lambda i: (i, 0))
    wfull = pl.BlockSpec((d, d), lambda i: (0, 0))
    return pl.pallas_call(
        body, name="merge_bwd",
        grid=(t // tm,),
        in_specs=[tile] + [_gate_spec(tm, b) for b in range(N_BRANCH)] + [tile, tile, tile,
                                                                           wfull, wfull, wfull, wfull],
        out_specs=[pl.BlockSpec((tm, N_BRANCH * d), lambda i: (i, 0))] + [tile] * 6,
        out_shape=[jax.ShapeDtypeStruct((t, N_BRANCH * d), BF16)] + [jax.ShapeDtypeStruct((t, d), BF16)] * 6,
        compiler_params=pltpu.CompilerParams(dimension_semantics=("parallel",)),
    )(dx1, proj, proj, proj, y_conv, y_sb, y_x, w_conv_out, w_sb_out, w_x_out, w_out)


def _conv_bwd(proj, conv_w, da, cw):
    t = proj.shape[0]
    nper = D_MODEL // cw

    def body(ch_ref, cb_ref, cc_ref, w_ref, da_ref, dch_ref, dcb_ref, dcc_ref, dw_ref):
        ch, cb, cc, u, u1, u2, cv, w, row = _conv_terms(ch_ref, cb_ref, cc_ref, w_ref)
        dav = da_ref[...].astype(F32)
        dcb_ref[...] = (dav * cv).astype(BF16)
        dcv = dav * cb
        n1 = jnp.where(row < t - 1, pltpu.roll(dcv, t - 1, 0), 0.0)
        n2 = jnp.where(row < t - 2, pltpu.roll(dcv, t - 2, 0), 0.0)
        du = w[2] * dcv + w[1] * n1 + w[0] * n2
        dcc_ref[...] = (du * ch).astype(BF16)
        dch_ref[...] = (du * cc).astype(BF16)
        dw_ref[0:1, :] = jnp.sum(dcv * u2, axis=0, keepdims=True)
        dw_ref[1:2, :] = jnp.sum(dcv * u1, axis=0, keepdims=True)
        dw_ref[2:3, :] = jnp.sum(dcv * u, axis=0, keepdims=True)

    def col(piece):
        return pl.BlockSpec((t, cw), lambda j: (0, piece * nper + j))

    out_col = pl.BlockSpec((t, cw), lambda j: (0, j))
    wspec = pl.BlockSpec((3, cw), lambda j: (0, j))
    return pl.pallas_call(
        body, name="conv_bwd",
        grid=(nper,),
        in_specs=[col(COL_CH), col(COL_CB), col(COL_CC), wspec, out_col],
        out_specs=[out_col, out_col, out_col, wspec],
        out_shape=[jax.ShapeDtypeStruct((t, D_MODEL), BF16)] * 3 + [jax.ShapeDtypeStruct((3, D_MODEL), F32)],
        compiler_params=pltpu.CompilerParams(dimension_semantics=("parallel",)),
    )(proj, proj, proj, conv_w, da)


def _sb_bwd(proj, kt4, do_sb, o_sb, weights, logits, tq, pair_sums):
    t = proj.shape[0]
    nq = t // tq
    pairs = D_MODEL // SB_BLOCK
    scale = SB_HEAD_DIM ** -0.5
    qi_tab, g_tab = _sb_schedule(nq)
    ns = qi_tab.shape[0]
    n = len(pair_sums)

    def body(qi_ref, g_ref, q_ref, v_ref, kt_ref, do_ref, o_ref, a_ref, z_ref, *rest):
        dq_ref, dk_ref, dv_ref = rest[n:n + 3]
        dk_acc, dv_acc, dqt_ref, carry_ref, qm_ref, dom_ref, dot_ref, dsum_ref = rest[2 * n + 3:2 * n + 11]
        begin, finish = _chip_exchange_phases(rest[:n], rest[n + 3:2 * n + 3], *rest[2 * n + 11:])
        pi, si = pl.program_id(0), pl.program_id(1)
        diagonal = g_ref[si] == qi_ref[si]
        pl.when((pi == 0) & (si == 0))(begin)

        @pl.when(si == 0)
        def _():
            dk_acc[...] = jnp.zeros_like(dk_acc)
            dv_acc[...] = jnp.zeros_like(dv_acc)

        @pl.when(diagonal)
        def _():
            dqt_ref[...] = jnp.zeros_like(dqt_ref)
            carry_ref[...] = jnp.zeros_like(carry_ref)
            q = q_ref[...].astype(F32) * scale
            do = do_ref[...].astype(F32)
            dot_ = do.T
            prod = dot_ * o_ref[...].T
            for h in range(2):
                rows = slice(h * SB_HEAD_DIM, (h + 1) * SB_HEAD_DIM)
                qm_ref[h] = _head_lanes(q, h)
                dom_ref[h] = _head_lanes(do, h)
                dot_ref[h] = _head_rows(dot_, h)
                dsum_ref[h] = jnp.sum(prod[rows, :], axis=0, keepdims=True)

        def step(past):
            u_mat = _scan_matrix()
            ks = pl.multiple_of(g_ref[si] * tq, tq)
            dk_add = jnp.zeros((tq, SB_BLOCK), F32)
            dv_add = jnp.zeros((tq, SB_BLOCK), F32)
            for h in range(2):
                rows = slice(h * SB_HEAD_DIM, (h + 1) * SB_HEAD_DIM)
                ab = a_ref[h]
                gw = _dot(v_ref[...], dot_ref[h]) * ab.astype(F32)
                after, carry = _group_suffix(u_mat, gw, carry_ref[h])
                sig = pl.reciprocal(1.0 + jnp.exp(-z_ref[h].astype(F32)), approx=True)
                dz = gw - sig * (dsum_ref[h] - after)
                if past is not None:
                    dz = jnp.where(past, dz, 0.0)
                dzb = dz.astype(BF16)
                dqt_ref[h] += _dot(kt_ref[rows, :], dzb)
                dk_add = dk_add + _dot(dzb, qm_ref[h])
                dv_add = dv_add + _dot(ab, dom_ref[h])
                carry_ref[h] = carry
            dk_acc[pl.ds(ks, tq), :] += dk_add
            dv_acc[pl.ds(ks, tq), :] += dv_add

        pl.when(diagonal)(lambda: step(_past_mask(tq)))
        pl.when(jnp.logical_not(diagonal))(lambda: step(None))

        @pl.when(g_ref[si] == 0)
        def _():
            dq_ref[...] = (jnp.concatenate([dqt_ref[0], dqt_ref[1]], axis=0).T * scale).astype(BF16)

        @pl.when(si == ns - 1)
        def _():
            dk_ref[...] = dk_acc[...].astype(BF16)
            dv_ref[...] = dv_acc[...].astype(BF16)

        pl.when((pi == pairs - 1) & (si == ns - 1))(finish)

    qblk = lambda base: pl.BlockSpec((tq, SB_BLOCK), lambda p, s, qt_, gt_: (qt_[s], base * pairs + p))
    kgrp = lambda base: pl.BlockSpec((tq, SB_BLOCK), lambda p, s, qt_, gt_: (gt_[s], base * pairs + p))
    seq = pl.BlockSpec((t, SB_BLOCK), lambda p, s, qt_, gt_: (0, p))
    tr = pl.BlockSpec((None, None, SB_BLOCK, tq), lambda p, s, qt_, gt_: (p, gt_[s], 0, 0))
    tile = pl.BlockSpec((None, None, 2, tq, tq), lambda p, s, qt_, gt_: (p, s, 0, 0, 0))
    outs = pl.pallas_call(
        body, name="sb_bwd",
        grid_spec=pltpu.PrefetchScalarGridSpec(
            num_scalar_prefetch=2, grid=(pairs, ns),
            in_specs=[qblk(COL_SQ), kgrp(COL_SV), tr, qblk(0), qblk(0), tile, tile] + [ANY] * n,
            out_specs=[qblk(0), seq, seq] + [ANY] * n,
            scratch_shapes=[pltpu.VMEM((t, SB_BLOCK), F32), pltpu.VMEM((t, SB_BLOCK), F32),
                            pltpu.VMEM((2, SB_HEAD_DIM, tq), F32), pltpu.VMEM((2, 1, tq), F32),
                            pltpu.VMEM((2, tq, SB_BLOCK), BF16), pltpu.VMEM((2, tq, SB_BLOCK), BF16),
                            pltpu.VMEM((2, SB_BLOCK, tq), BF16), pltpu.VMEM((2, 1, tq), F32)] + _chip_exchange_sems(n)),
        out_shape=[jax.ShapeDtypeStruct((t, D_MODEL), BF16)] * 3 + _chip_exchange_shapes(pair_sums),
        compiler_params=pltpu.CompilerParams(dimension_semantics=("arbitrary", "arbitrary")),
    )(qi_tab, g_tab, proj, proj, kt4, do_sb, o_sb, weights, logits, *pair_sums)
    return outs[0], outs[1], outs[2], outs[3:]


def _x_bwd(proj, q_norm_g, kn, v, do_x, tm, grads):
    t = proj.shape[0]
    m = kn.shape[0]
    scale = X_HEAD_DIM ** -0.5
    nt = t // tm
    n = len(grads)

    def body(xq_ref, qg_ref, kn_ref, v_ref, do_ref, *rest):
        dxq_ref, dkn_ref, dv_ref, dqg_ref = rest[n:n + 4]
        begin, finish = _pair_exchange_phases(rest[:n], rest[n + 4:2 * n + 4], *rest[2 * n + 4:])

        @pl.when(pl.program_id(0) == 0)
        def _():
            begin()
            dkn_ref[...] = jnp.zeros_like(dkn_ref)
            dv_ref[...] = jnp.zeros_like(dv_ref)
            dqg_ref[...] = jnp.zeros_like(dqg_ref)

        qg = qg_ref[...]
        for h in range(X_HEADS):
            sl, rq, qhat, qn, p = _x_head(xq_ref, qg, kn_ref, h)
            do_h = do_ref[:, sl]
            dp = _dot_nt(do_h, v_ref[:, sl])
            ds = (p * (dp - jnp.sum(dp * p, axis=-1, keepdims=True)) * scale).astype(BF16)
            dqn = _dot(ds, kn_ref[:, sl])
            dkn_ref[:, sl] += _dot_tn(ds, qn)
            dv_ref[:, sl] += _dot_tn(p.astype(BF16), do_h)
            dqg_ref[...] += jnp.sum(dqn * qhat, axis=0, keepdims=True)
            dxq_ref[:, sl] = _rms_bwd(dqn, qhat, rq, qg).astype(BF16)

        pl.when(pl.program_id(0) == nt - 1)(finish)

    full = pl.BlockSpec((m, D_MODEL), lambda i: (0, 0))
    gain = pl.BlockSpec((1, X_HEAD_DIM), lambda i: (0, 0))
    tile = pl.BlockSpec((tm, D_MODEL), lambda i: (i, 0))
    outs = pl.pallas_call(
        body, name="x_bwd",
        grid=(nt,),
        in_specs=[pl.BlockSpec((tm, D_MODEL), lambda i: (i, COL_XQ)), gain, full, full, tile] + [ANY] * n,
        out_specs=[tile, full, full, gain] + [ANY] * n,
        out_shape=[jax.ShapeDtypeStruct((t, D_MODEL), BF16), jax.ShapeDtypeStruct((m, D_MODEL), F32),
                   jax.ShapeDtypeStruct((m, D_MODEL), F32), jax.ShapeDtypeStruct((1, X_HEAD_DIM), F32)]
        + _pair_exchange_shapes(grads),
        scratch_shapes=_pair_exchange_sems(n),
        compiler_params=pltpu.CompilerParams(dimension_semantics=("arbitrary",)),
    )(proj, q_norm_g, kn, v, do_x, *grads)
    return outs[0], outs[1], outs[2], outs[3], outs[4:]


def _mem_bwd(mem, g_mem, wkv_all, k_norm_g, dkn, dv):
    m, d = mem.shape

    def body(mem_ref, g_ref, w_ref, kg_ref, dkn_ref, dv_ref, dkv_ref, dgm_ref, dkg_ref):
        memf = mem_ref[...]
        mem_hat = memf * _rstd(memf)
        memn = (mem_hat * g_ref[...]).astype(BF16)
        kg = kg_ref[...]
        dmemn = jnp.zeros((m, d), F32)
        dkg = jnp.zeros((1, X_HEAD_DIM), F32)
        for b in range(N_DEV):
            sl = slice(b * X_HEAD_DIM, (b + 1) * X_HEAD_DIM)
            if b < X_HEADS:
                kv = _dot(memn, w_ref[b])
                rk = _rstd(kv)
                khat = kv * rk
                dkn_h = dkn_ref[:, sl]
                dkg = dkg + jnp.sum(dkn_h * khat, axis=0, keepdims=True)
                dblk = _rms_bwd(dkn_h, khat, rk, kg).astype(BF16)
            else:
                hs = slice((b - X_HEADS) * X_HEAD_DIM, (b - X_HEADS + 1) * X_HEAD_DIM)
                dblk = dv_ref[:, hs].astype(BF16)
            dkv_ref[:, sl] = dblk
            dmemn = dmemn + _dot_nt(dblk, w_ref[b])
        dgm_ref[...] = jnp.sum(dmemn * mem_hat, axis=0, keepdims=True)
        dkg_ref[...] = dkg

    return pl.pallas_call(
        body, name="mem_bwd",
        out_shape=[jax.ShapeDtypeStruct((m, 2 * d), BF16), jax.ShapeDtypeStruct((1, d), F32),
                   jax.ShapeDtypeStruct((1, X_HEAD_DIM), F32)],
    )(mem, g_mem, wkv_all, k_norm_g, dkn, dv)


def _in_proj_bwd(x, g_mix, w_in_all, dproj, dx1, tm, pair_sums):
    t, d = x.shape
    nb, _, bw = w_in_all.shape
    nt = t // tm
    n = len(pair_sums)

    def body(x_ref, g_ref, w_ref, dp_ref, dx1_ref, *rest):
        dx_ref, dg_ref = rest[n:n + 2]
        acc_ref = rest[2 * n + 2]
        begin, finish = _chip_exchange_phases(rest[:n], rest[n + 2:2 * n + 2], *rest[2 * n + 3:])
        i, j = pl.program_id(0), pl.program_id(1)
        pl.when((i == 0) & (j == 0))(begin)

        @pl.when(j == 0)
        def _():
            acc_ref[...] = jnp.zeros_like(acc_ref)

        @pl.when((i == 0) & (j == 0))
        def _():
            dg_ref[...] = jnp.zeros_like(dg_ref)

        acc_ref[...] += _dot_nt(dp_ref[...], w_ref[...])

        @pl.when(j == nb - 1)
        def _():
            xf = x_ref[...]
            rs = _rstd(xf)
            xhat = xf * rs
            dh = acc_ref[...]
            dg_ref[...] += jnp.sum(dh * xhat, axis=0, keepdims=True)
            dx_ref[...] = dx1_ref[...] + _rms_bwd(dh, xhat, rs, g_ref[...])

        pl.when((i == nt - 1) & (j == nb - 1))(finish)

    tile = pl.BlockSpec((tm, d), lambda i, j: (i, 0))
    row = pl.BlockSpec((1, d), lambda i, j: (0, 0))
    outs = pl.pallas_call(
        body, name="in_proj_bwd",
        grid=(nt, nb),
        in_specs=[tile, row, pl.BlockSpec((None, d, bw), lambda i, j: (j, 0, 0)),
                  pl.BlockSpec((tm, bw), lambda i, j: (i, j)), tile] + [ANY] * n,
        out_specs=[tile, row] + [ANY] * n,
        out_shape=[jax.ShapeDtypeStruct((t, d), F32), jax.ShapeDtypeStruct((1, d), F32)] + _chip_exchange_shapes(pair_sums),
        scratch_shapes=[pltpu.VMEM((tm, d), F32)] + _chip_exchange_sems(n),
        compiler_params=pltpu.CompilerParams(dimension_semantics=("arbitrary", "arbitrary")),
    )(x, g_mix, w_in_all, dproj, dx1, *pair_sums)
    return outs[0], outs[1], outs[2:]


def _weight_grad(a, b, bw, tmm, name):
    t, m = a.shape
    n = b.shape[1]
    tmm = min(tmm, m)

    def body(a_ref, b_ref, o_ref):
        o_ref[...] = _dot_tn(a_ref[...].astype(BF16), b_ref[...].astype(BF16)).astype(BF16)

    return pl.pallas_call(
        body, name=name,
        grid=(m // tmm, n // bw),
        in_specs=[pl.BlockSpec((t, tmm), lambda i, j: (0, i)), pl.BlockSpec((t, bw), lambda i, j: (0, j))],
        out_specs=pl.BlockSpec((None, tmm, bw), lambda i, j: (j, i, 0)),
        out_shape=jax.ShapeDtypeStruct((n // bw, m, bw), BF16),
        compiler_params=pltpu.CompilerParams(dimension_semantics=("parallel", "parallel")),
    )(a, b)


def _pair_sum(grad, recv, own_blocks, name):
    _, rows, cols = grad.shape

    def body(idx_ref, g_ref, r_ref, o_ref):
        o_ref[...] = (g_ref[...].astype(F32) + r_ref[...].astype(F32)).astype(BF16)

    return pl.pallas_call(
        body, name=name,
        grid_spec=pltpu.PrefetchScalarGridSpec(
            num_scalar_prefetch=1, grid=(4,),
            in_specs=[pl.BlockSpec((None, rows, cols), lambda r, idx: (idx[r], 0, 0)),
                      pl.BlockSpec((None, rows, cols), lambda r, idx: (r, 0, 0))],
            out_specs=pl.BlockSpec((None, rows, cols), lambda r, idx: (r, 0, 0))),
        out_shape=jax.ShapeDtypeStruct((4, rows, cols), BF16),
        compiler_params=pltpu.CompilerParams(dimension_semantics=("parallel",)),
    )(own_blocks, grad, recv)


def _adamw_math(w, g, m, v):
    m = ADAM_B1 * m + (1.0 - ADAM_B1) * g
    v = ADAM_B2 * v + (1.0 - ADAM_B2) * jnp.square(g)
    m_hat = m / (1.0 - ADAM_B1 ** ADAM_STEP)
    v_hat = v / (1.0 - ADAM_B2 ** ADAM_STEP)
    delta = -ADAM_LR * (m_hat / (jnp.sqrt(v_hat) + ADAM_EPS) + ADAM_WD * w)
    return delta, m, v


def _adamw_sharded(pair_sums, recv, w, m, v, tr, name):
    rows, cols = w.shape
    tr = min(tr, rows)

    def body(h_ref, r_ref, w_ref, m_ref, v_ref, g_out, d_out, m_out, v_out):
        g = h_ref[...].astype(F32)
        for r in range(3):
            g = g + r_ref[r].astype(F32)
        g_out[...] = g
        d_out[...], m_out[...], v_out[...] = _adamw_math(w_ref[...], g, m_ref[...], v_ref[...])

    tile = pl.BlockSpec((tr, cols), lambda i: (i, 0))
    return pl.pallas_call(
        body, name=name,
        grid=(rows // tr,),
        in_specs=[pl.BlockSpec((None, tr, cols), lambda i: (0, i, 0)),
                  pl.BlockSpec((3, tr, cols), lambda i: (0, i, 0)), tile, tile, tile],
        out_specs=[tile] * 4,
        out_shape=[jax.ShapeDtypeStruct((rows, cols), F32)] * 4,
        compiler_params=pltpu.CompilerParams(dimension_semantics=("parallel",)),
    )(pair_sums, recv, w, m, v)


SMALL_ROWS = 16


def _pack_rows(dg_mix, dg_mem, dg_mlp, dqg, dkg, dconv, lsum):
    def body(a_ref, b_ref, c_ref, q_ref, k_ref, cv_ref, l_ref, o_ref):
        o_ref[...] = jnp.zeros_like(o_ref)
        for r, ref in enumerate((a_ref, b_ref, c_ref)):
            o_ref[r:r + 1, :] = ref[...]
        o_ref[3:4, :X_HEAD_DIM] = q_ref[...]
        o_ref[4:5, :X_HEAD_DIM] = k_ref[...]
        o_ref[5:8, :] = cv_ref[...]
        o_ref[8:9, :] = l_ref[...]

    return pl.pallas_call(body, name="small_pack", out_shape=jax.ShapeDtypeStruct((SMALL_ROWS, D_MODEL), F32))(
        dg_mix, dg_mem, dg_mlp, dqg, dkg, dconv, lsum)


def _small_sum(gathered):
    def body(g_ref, o_ref):
        total = g_ref[0]
        for dev in range(1, N_DEV):
            total = total + g_ref[dev]
        o_ref[...] = jnp.zeros_like(o_ref)
        for piece in range(5):
            o_ref[piece * SMALL_TILE:piece * SMALL_TILE + 1, :] = total[piece:piece + 1]
        o_ref[5 * SMALL_TILE:5 * SMALL_TILE + 3, :] = total[5:8]
        o_ref[6 * SMALL_TILE:6 * SMALL_TILE + 1, :] = total[8:9]

    return pl.pallas_call(body, name="small_grad_sum",
                          out_shape=jax.ShapeDtypeStruct((7 * SMALL_TILE, D_MODEL), F32))(gathered)


def _adamw_small(w, g, m, v):
    def body(w_ref, g_ref, m_ref, v_ref, d_out, m_out, v_out):
        d_out[...], m_out[...], v_out[...] = _adamw_math(w_ref[...], g_ref[...], m_ref[...], v_ref[...])

    return pl.pallas_call(body, name="adamw_small", out_shape=[jax.ShapeDtypeStruct(w.shape, F32)] * 3)(w, g, m, v)


def _pad_tile(a):
    return jnp.pad(a, ((0, SMALL_TILE - a.shape[0]), (0, D_MODEL - a.shape[1])))


def _pack_small(*pieces):
    return jnp.concatenate([_pad_tile(a) for a in pieces], axis=0)


def kernel(x, mem, g_mix, g_mem, w_in, conv_w, w_conv_out, w_sb_out, q_norm_g, k_norm_g, w_mem_kv, w_x_out, w_out, g_mlp, w_up, w_down, loss_target, m_g_mix, m_g_mem, m_w_in, m_conv_w, m_w_conv_out, m_w_sb_out, m_q_norm_g, m_k_norm_g, m_w_mem_kv, m_w_x_out, m_w_out, m_g_mlp, m_w_up, m_w_down, v_g_mix, v_g_mem, v_w_in, v_conv_w, v_w_conv_out, v_w_sb_out, v_q_norm_g, v_k_norm_g, v_w_mem_kv, v_w_x_out, v_w_out, v_g_mlp, v_w_up, v_w_down):
    xpos, ypos, cpos = _mesh_pos()
    me = 4 * xpos + 2 * ypos + cpos
    x2d, mem2d, tgt2d = x[0], mem[0], loss_target[0]
    t = x2d.shape[0]
    tm = min(512, t)
    tm_s = min(256, t)

    big = {
        "w_in": (w_in[0], m_w_in[0], v_w_in[0]),
        "w_conv_out": (w_conv_out[0], m_w_conv_out[0], v_w_conv_out[0]),
        "w_sb_out": (w_sb_out[0], m_w_sb_out[0], v_w_sb_out[0]),
        "w_mem_kv": (w_mem_kv[0], m_w_mem_kv[0], v_w_mem_kv[0]),
        "w_x_out": (w_x_out[0], m_w_x_out[0], v_w_x_out[0]),
        "w_out": (w_out[0], m_w_out[0], v_w_out[0]),
        "w_up": (w_up[0], m_w_up[0], v_w_up[0]),
        "w_down": (w_down[0], m_w_down[0], v_w_down[0]),
    }
    late = [n for n in big if n != "w_in"]
    as_bf16 = lambda group: [big[n][0].astype(BF16) for n in group]
    conv_pad = jnp.pad(conv_w[0], ((0, 8 - 3), (0, 0)))

    proj, h, (w_in_all, conv_all) = _in_proj(x2d, g_mix, _arrival_blocks(xpos, ypos, cpos), tm,
                                             [big["w_in"][0].astype(BF16), conv_pad])
    conv_full = conv_all[:, :3, :].transpose(1, 0, 2).reshape(3, D_MODEL)
    a_conv = _conv_fwd(proj, conv_full, 256)
    tq = min(SB_QUERY_TILE, t)
    pairs = D_MODEL // SB_BLOCK

    def groups_t(cols):
        return cols.reshape(t // tq, tq, pairs, SB_BLOCK).transpose(2, 0, 3, 1)

    kt4 = groups_t(proj[:, COL_SK * D_MODEL:(COL_SK + 1) * D_MODEL])
    vt4 = groups_t(proj[:, COL_SV * D_MODEL:(COL_SV + 1) * D_MODEL])
    o_sb, sb_weights, sb_logits, gathered = _sb_fwd(proj, vt4, tq, as_bf16(late))
    full = dict(zip(late, gathered))
    wkv_all, w_up_all = full["w_mem_kv"], full["w_up"]
    rows_full = lambda a: a.reshape(a.shape[0] * a.shape[1], a.shape[2])
    wc, ws, wx, wo, wd = (rows_full(full[n]) for n in ("w_conv_out", "w_sb_out", "w_x_out", "w_out", "w_down"))
    mem_n, kn, vmem = _mem_prep(mem2d, g_mem, wkv_all, k_norm_g)
    o_x = _x_fwd(proj, q_norm_g, kn, vmem, tm_s)
    x1, y_conv, y_sb, y_x, merged = _merge_fwd(x2d, proj, a_conv, o_sb, o_x, wc, ws, wx, wo, tm_s)
    up, h2, dx2, lsum = _mlp_fwd(x1, g_mlp, w_up_all, wd, tgt2d, tm)

    dup, act, dx1, dg_mlp = _mlp_bwd(x1, g_mlp, w_up_all, wd, up, dx2, tm)
    dgate, dy_conv, dy_sb, dy_x, da_conv, do_sb, do_x = _merge_bwd(dx1, proj, y_conv, y_sb, y_x, wc, ws, wx, wo, tm_s)
    dch, dcb, dcc, dconv = _conv_bwd(proj, conv_full, da_conv, 256)
    wgrads = {
        "w_conv_out": _weight_grad(a_conv, dy_conv, D_MODEL, 512, "dw_conv_out"),
        "w_sb_out": _weight_grad(o_sb, dy_sb, D_MODEL, 512, "dw_sb_out"),
        "w_x_out": _weight_grad(o_x, dy_x, D_MODEL, 512, "dw_x_out"),
        "w_out": _weight_grad(merged, dx1, D_MODEL, 512, "dw_out"),
        "w_up": _weight_grad(h2, dup, w_up_all.shape[2], 512, "dw_up"),
        "w_down": _weight_grad(act, dx2, D_MODEL, 512, "dw_down"),
    }

    own_blocks = jnp.stack([4 * (xpos ^ dx) + 2 * (ypos ^ dy) + cpos for dx in (0, 1) for dy in (0, 1)]).astype(jnp.int32)
    blocked = lambda n: wgrads[n].reshape((N_DEV,) + big[n][0].shape)
    pair_sum = lambda n, from_sibling: _pair_sum(blocked(n), from_sibling, own_blocks, "pair_sum_" + n)

    behind_x = list(wgrads)
    dxq, dkn, dvm, dqg, from_sibling = _x_bwd(proj, q_norm_g, kn, vmem, do_x, tm_s, [blocked(n) for n in behind_x])
    pair_sums = {n: pair_sum(n, r) for n, r in zip(behind_x, from_sibling)}
    dkv, dg_mem, dkg = _mem_bwd(mem2d, g_mem, wkv_all, k_norm_g, dkn, dvm)
    wgrads["w_mem_kv"] = _weight_grad(mem_n, dkv, wkv_all.shape[2], 512, "dw_mem_kv")
    pair_sums["w_mem_kv"] = pair_sum("w_mem_kv", _pair_exchange([blocked("w_mem_kv")], "grad_pair_exchange_w_mem_kv")[0])
    dq, dk, dv, from_chips_late = _sb_bwd(proj, kt4, do_sb, o_sb, sb_weights, sb_logits, tq,
                                          [pair_sums[n] for n in late])
    from_chips = dict(zip(late, from_chips_late))
    dproj = jnp.concatenate([dch, dcb, dcc, dq, dk, dv, dxq, dgate], axis=1)
    wgrads["w_in"] = _weight_grad(h, dproj, w_in_all.shape[2], 512, "dw_in")
    pair_sums["w_in"] = pair_sum("w_in", _pair_exchange([blocked("w_in")], "grad_pair_exchange_w_in")[0])
    grad_x, dg_mix, (from_chips["w_in"],) = _in_proj_bwd(x2d, g_mix, w_in_all, dproj, dx1, tm, [pair_sums["w_in"]])
    res = {}
    for n in big:
        w_sh, m_sh, v_sh = big[n]
        res[n] = _adamw_sharded(pair_sums[n], from_chips[n], w_sh, m_sh, v_sh, 256, "adamw_" + n)

    part = _pack_rows(dg_mix, dg_mem, dg_mlp, dqg, dkg, dconv, lsum)
    gsum = _small_sum(_small_all_gather(part))
    loss = 0.5 * jnp.sum(gsum[6 * SMALL_TILE]) / D_MODEL
    conv_cols = lax.dynamic_slice(gsum[5 * SMALL_TILE:6 * SMALL_TILE], (0, me * (D_MODEL // N_DEV)),
                                  (SMALL_TILE, D_MODEL // N_DEV))
    g_small = jnp.concatenate([gsum[:5 * SMALL_TILE], _pad_tile(conv_cols)], axis=0)
    w_small = _pack_small(g_mix, g_mem, g_mlp, q_norm_g, k_norm_g, conv_w[0])
    m_small = _pack_small(m_g_mix, m_g_mem, m_g_mlp, m_q_norm_g, m_k_norm_g, m_conv_w[0])
    v_small = _pack_small(v_g_mix, v_g_mem, v_g_mlp, v_q_norm_g, v_k_norm_g, v_conv_w[0])
    d_small, nm_small, nv_small = _adamw_small(w_small, g_small, m_small, v_small)

    def unpack(p):
        return {"g_mix": p[0:1], "g_mem": p[8:9], "g_mlp": p[16:17], "q_norm_g": p[24:25, :X_HEAD_DIM],
                "k_norm_g": p[32:33, :X_HEAD_DIM], "conv_w": p[40:43, :D_MODEL // N_DEV][None]}

    small = [unpack(p) for p in (g_small, d_small, nm_small, nv_small)]
    order = ["g_mix", "g_mem", "w_in", "conv_w", "w_conv_out", "w_sb_out", "q_norm_g", "k_norm_g", "w_mem_kv",
             "w_x_out", "w_out", "g_mlp", "w_up", "w_down"]
    outs = [loss, grad_x[None]]
    for kind in range(4):
        for n in order:
            outs.append(res[n][kind][None] if n in res else small[kind][n])
    return tuple(outs)
```

```python
import jax
import jax.numpy as jnp
from jax import lax
from jax.experimental import pallas as pl
from jax.experimental.pallas import tpu as pltpu

F32 = jnp.float32
BF16 = jnp.bfloat16
MESH = pl.DeviceIdType.MESH

EPS = 1e-6
N_DEV = 8
D_MODEL = 1024
SB_HEAD_DIM = 64
SB_BLOCK = 128
SB_QUERY_TILE = 512
X_HEADS = 4
X_HEAD_DIM = 256
N_BRANCH = 3
COL_CH, COL_CB, COL_CC, COL_SQ, COL_SK, COL_SV, COL_XQ, COL_GATE = 0, 1, 2, 3, 4, 5, 6, 7

ADAM_LR = 0.001
ADAM_B1 = 0.9
ADAM_B2 = 0.999
ADAM_EPS = 1e-08
ADAM_WD = 0.01
ADAM_STEP = 10

SMALL_TILE = 8


def _dot(a, b):
    return jnp.dot(a, b, preferred_element_type=F32)


def _dot_nt(a, b):
    return lax.dot_general(a, b, (((1,), (1,)), ((), ())), preferred_element_type=F32)


def _dot_tn(a, b):
    return lax.dot_general(a, b, (((0,), (0,)), ((), ())), preferred_element_type=F32)


def _rstd(xf):
    return lax.rsqrt(jnp.mean(xf * xf, axis=-1, keepdims=True) + EPS)


def _sigmoid(z):
    return 1.0 / (1.0 + jnp.exp(-z))


def _rms_bwd(dy, xhat, r, g):
    dxhat = dy * g
    return r * (dxhat - xhat * jnp.mean(dxhat * xhat, axis=-1, keepdims=True))


def _mesh_pos():
    return lax.axis_index("x"), lax.axis_index("y"), lax.axis_index("c")


ANY = pl.BlockSpec(memory_space=pl.ANY)


def _gather_shapes(shards):
    return [jax.ShapeDtypeStruct((N_DEV,) + s.shape, s.dtype) for s in shards]


def _gather_sems(n):
    return [pltpu.SemaphoreType.DMA((n, 7)), pltpu.SemaphoreType.DMA((n, 7)), pltpu.SemaphoreType.DMA((n,))]


def _gather_phases(ins, outs, send_sems, recv_sems, local_sems, by_arrival=False):
    n = len(ins)
    x, y, c = _mesh_pos()
    me, sibling = (x, y, c), (x, y, 1 - c)
    chips = [(1 - x, y), (x, 1 - y), (1 - x, 1 - y)]

    def blk(a, px, py, pc):
        return outs[a].at[4 * px + 2 * py + pc]

    def copy(a, k, block, to, src=None):
        return pltpu.make_async_remote_copy(
            src_ref=blk(a, *block) if src is None else src, dst_ref=blk(a, *block),
            send_sem=send_sems.at[a, k], recv_sem=recv_sems.at[a, k], device_id=to, device_id_type=MESH)

    def local(a):
        return pltpu.make_async_copy(ins[a], blk(a, *me), local_sems.at[a])

    def own(a):
        return [copy(a, 0, me, sibling, src=ins[a])] + [copy(a, 1 + j, me, (*chips[j], c), src=ins[a]) for j in range(2)]

    def onward(a, via):
        return copy(a, 3, (*chips[via], c), (*chips[1 - via], c))

    def begin():
        for a in range(n):
            local(a).start()
        for a in range(n):
            for cp in own(a):
                cp.start()

    def arrive(order):
        for a in range(n):
            if order == 0:
                copy(a, 0, sibling, me).wait_recv()
            elif order <= 3:
                chip = chips[order - 1]
                copy(a, order, (*chip, c), me).wait_recv()
                copy(a, 3 + order, (*chip, c), sibling).start()
                if order <= 2:
                    pl.when(c == order - 1)(onward(a, order - 1).start)
            else:
                copy(a, order, (*chips[order - 4], 1 - c), me).wait_recv()

    def relay():
        for order in (1, 2):
            arrive(order)

    def drain():
        for a in range(n):
            for cp in own(a):
                cp.wait_send()
            for via in range(2):
                pl.when(c == via)(onward(a, via).wait_send)
            for j, chip in enumerate(chips):
                copy(a, 4 + j, (*chip, c), sibling).wait_send()
            local(a).wait()

    def finish():
        for order in (3, 0, 4, 5, 6):
            arrive(order)
        drain()

    if by_arrival:
        return begin, arrive, drain
    return begin, relay, finish


def _pair_exchange(grads, name):
    n = len(grads)

    def body(*refs):
        begin, finish = _pair_exchange_phases(refs[:n], refs[n:2 * n], *refs[2 * n:])
        begin()
        finish()

    return pl.pallas_call(
        body, name=name,
        out_shape=_pair_exchange_shapes(grads),
        in_specs=[ANY] * n, out_specs=[ANY] * n,
        scratch_shapes=_pair_exchange_sems(n),
    )(*grads)


def _pair_exchange_shapes(grads):
    return [jax.ShapeDtypeStruct((4,) + g.shape[1:], g.dtype) for g in grads]


def _pair_exchange_sems(n):
    return [pltpu.SemaphoreType.DMA((n, 4)), pltpu.SemaphoreType.DMA((n, 4))]


def _pair_exchange_phases(ins, outs, send_sems, recv_sems):
    x, y, c = _mesh_pos()
    xs, ys = (x, 1 - x), (y, 1 - y)

    def copies():
        out = []
        for a in range(len(ins)):
            for r in range(4):
                dx, dy = divmod(r, 2)
                out.append(pltpu.make_async_remote_copy(
                    src_ref=ins[a].at[4 * xs[dx] + 2 * ys[dy] + (1 - c)], dst_ref=outs[a].at[r],
                    send_sem=send_sems.at[a, r], recv_sem=recv_sems.at[a, r],
                    device_id=(x, y, 1 - c), device_id_type=MESH))
        return out

    def begin():
        for cp in copies():
            cp.start()

    def finish():
        for cp in copies():
            cp.wait()

    return begin, finish


def _chip_exchange_shapes(sums):
    return [jax.ShapeDtypeStruct((3,) + s.shape[1:], s.dtype) for s in sums]


def _chip_exchange_sems(n):
    return [pltpu.SemaphoreType.DMA((n, 3)), pltpu.SemaphoreType.DMA((n, 3))]


def _chip_exchange_phases(ins, outs, send_sems, recv_sems):
    x, y, c = _mesh_pos()
    xs, ys = (x, 1 - x), (y, 1 - y)

    def copies():
        out = []
        for a in range(len(ins)):
            for r in range(1, 4):
                dx, dy = divmod(r, 2)
                out.append(pltpu.make_async_remote_copy(
                    src_ref=ins[a].at[r], dst_ref=outs[a].at[r - 1],
                    send_sem=send_sems.at[a, r - 1], recv_sem=recv_sems.at[a, r - 1],
                    device_id=(xs[dx], ys[dy], c), device_id_type=MESH))
        return out

    def begin():
        for cp in copies():
            cp.start()

    def finish():
        for cp in copies():
            cp.wait()

    return begin, finish


def _small_all_gather(part):
    rows, cols = part.shape

    def body(in_ref, out_ref, send_sems, recv_sems):
        x, y, c = _mesh_pos()
        xs, ys, cs = (x, 1 - x), (y, 1 - y), (c, 1 - c)
        out_ref[4 * x + 2 * y + c] = in_ref[...]
        copies = []
        for k in range(1, N_DEV):
            dx, dy, dc = k // 4, (k // 2) % 2, k % 2
            copies.append((
                pltpu.make_async_remote_copy(
                    src_ref=in_ref, dst_ref=out_ref.at[4 * x + 2 * y + c],
                    send_sem=send_sems.at[k - 1], recv_sem=recv_sems.at[k - 1],
                    device_id=(xs[dx], ys[dy], cs[dc]), device_id_type=MESH),
                pltpu.make_async_remote_copy(
                    src_ref=in_ref, dst_ref=out_ref.at[4 * xs[dx] + 2 * ys[dy] + cs[dc]],
                    send_sem=send_sems.at[k - 1], recv_sem=recv_sems.at[k - 1],
                    device_id=(xs[dx], ys[dy], cs[dc]), device_id_type=MESH)))
        for send, _ in copies:
            send.start()
        for send, recv in copies:
            recv.wait_recv()
            send.wait_send()

    return pl.pallas_call(
        body, name="small_all_gather",
        out_shape=jax.ShapeDtypeStruct((N_DEV, rows, cols), part.dtype),
        in_specs=[pl.BlockSpec(memory_space=pltpu.VMEM)],
        out_specs=pl.BlockSpec(memory_space=pltpu.VMEM),
        scratch_shapes=[pltpu.SemaphoreType.DMA((N_DEV - 1,)), pltpu.SemaphoreType.DMA((N_DEV - 1,))],
    )(part)


ARRIVAL_ORDER = (0, 1, 2, 4, 5, 3, 6)


def _arrival_blocks(xpos, ypos, cpos):
    chips = [(1 - xpos, ypos), (xpos, 1 - ypos), (1 - xpos, 1 - ypos)]
    by_order = ([4 * xpos + 2 * ypos + (1 - cpos)] + [4 * cx + 2 * cy + cpos for cx, cy in chips]
                + [4 * cx + 2 * cy + (1 - cpos) for cx, cy in chips])
    return jnp.stack([4 * xpos + 2 * ypos + cpos] + [by_order[o] for o in ARRIVAL_ORDER]).astype(jnp.int32)


def _in_proj(x, g_mix, arrival_blocks, tm, shards):
    t, d = x.shape
    bw = shards[0].shape[1]
    nt = t // tm
    n = len(shards)

    def body(blocks_ref, x_ref, g_ref, *rest):
        w_shard = rest[0]
        proj_ref, h_ref = rest[n:n + 2]
        w_all = rest[n + 2]
        h_scr, w_buf, fetch_sems = rest[2 * n + 2:2 * n + 5]
        begin, arrive, drain = _gather_phases(rest[:n], rest[n + 2:2 * n + 2], *rest[2 * n + 5:], by_arrival=True)
        j, i = pl.program_id(0), pl.program_id(1)
        slot = lax.rem(j, 2)

        def fetch(src, into):
            return pltpu.make_async_copy(src, w_buf.at[into], fetch_sems.at[into])

        @pl.when((j == 0) & (i == 0))
        def _():
            begin()
            fetch(w_shard, 0).start()

        @pl.when(j == 0)
        def _():
            xf = x_ref[...]
            hv = (xf * _rstd(xf) * g_ref[...]).astype(BF16)
            h_ref[...] = hv
            h_scr[pl.ds(pl.multiple_of(i * tm, tm), tm), :] = hv

        @pl.when(i == 0)
        def _():
            fetch(w_shard, slot).wait()

        proj_ref[...] = _dot(h_scr[pl.ds(pl.multiple_of(i * tm, tm), tm), :], w_buf[slot]).astype(BF16)

        for nxt in range(1, N_DEV):
            @pl.when((i == nt - 1) & (j == nxt - 1))
            def _():
                arrive(ARRIVAL_ORDER[nxt - 1])
                fetch(w_all.at[blocks_ref[nxt]], 1 - slot).start()

        pl.when((i == nt - 1) & (j == N_DEV - 1))(drain)

    first_pass = lambda j, i, blocks: (jnp.where(j == 0, i, nt - 1), 0)
    outs = pl.pallas_call(
        body, name="in_proj",
        grid_spec=pltpu.PrefetchScalarGridSpec(
            num_scalar_prefetch=1, grid=(N_DEV, nt),
            in_specs=[pl.BlockSpec((tm, d), first_pass), pl.BlockSpec((1, d), lambda j, i, blocks: (0, 0))] + [ANY] * n,
            out_specs=[pl.BlockSpec((tm, bw), lambda j, i, blocks: (i, blocks[j])),
                       pl.BlockSpec((tm, d), first_pass)] + [ANY] * n,
            scratch_shapes=[pltpu.VMEM((t, d), BF16), pltpu.VMEM((2, d, bw), BF16), pltpu.SemaphoreType.DMA((2,))]
            + _gather_sems(n)),
        out_shape=[jax.ShapeDtypeStruct((t, N_DEV * bw), BF16), jax.ShapeDtypeStruct((t, d), BF16)] + _gather_shapes(shards),
        compiler_params=pltpu.CompilerParams(dimension_semantics=("arbitrary", "arbitrary")),
    )(arrival_blocks, x, g_mix, *shards)
    return outs[0], outs[1], outs[2:]


def _conv_terms(ch_ref, cb_ref, cc_ref, w_ref):
    ch, cb, cc = ch_ref[...].astype(F32), cb_ref[...].astype(F32), cc_ref[...].astype(F32)
    u = cc * ch
    row = lax.broadcasted_iota(jnp.int32, u.shape, 0)
    u1 = jnp.where(row >= 1, pltpu.roll(u, 1, 0), 0.0)
    u2 = jnp.where(row >= 2, pltpu.roll(u, 2, 0), 0.0)
    w = (w_ref[0:1, :], w_ref[1:2, :], w_ref[2:3, :])
    cv = w[2] * u + w[1] * u1 + w[0] * u2
    return ch, cb, cc, u, u1, u2, cv, w, row


def _conv_fwd(proj, conv_w, cw):
    t = proj.shape[0]
    nper = D_MODEL // cw

    def body(ch_ref, cb_ref, cc_ref, w_ref, a_ref):
        _, cb, _, _, _, _, cv, _, _ = _conv_terms(ch_ref, cb_ref, cc_ref, w_ref)
        a_ref[...] = (cb * cv).astype(BF16)

    def col(piece):
        return pl.BlockSpec((t, cw), lambda j: (0, piece * nper + j))

    return pl.pallas_call(
        body, name="conv_fwd",
        grid=(nper,),
        in_specs=[col(COL_CH), col(COL_CB), col(COL_CC), pl.BlockSpec((3, cw), lambda j: (0, j))],
        out_specs=pl.BlockSpec((t, cw), lambda j: (0, j)),
        out_shape=jax.ShapeDtypeStruct((t, D_MODEL), BF16),
        compiler_params=pltpu.CompilerParams(dimension_semantics=("parallel",)),
    )(proj, proj, proj, conv_w)


def _scan_matrix(inclusive, value):
    s = lax.broadcasted_iota(jnp.int32, (SB_BLOCK, 2 * SB_BLOCK), 0)
    j = lax.rem(lax.broadcasted_iota(jnp.int32, (SB_BLOCK, 2 * SB_BLOCK), 1), SB_BLOCK)
    return jnp.where((j >= s) if inclusive else (j > s), value, 0.0).astype(BF16)


def _suffix_sum(u_mat, xv):
    hi = xv.astype(BF16)
    lo = (xv - hi.astype(F32)).astype(BF16)
    return _dot(u_mat, jnp.concatenate([hi, lo], axis=0))


def _head_rows(vt, h):
    row = lax.broadcasted_iota(jnp.int32, vt.shape, 0)
    return jnp.where((row >= h * SB_HEAD_DIM) & (row < (h + 1) * SB_HEAD_DIM), vt, 0.0).astype(BF16)


def _head_lanes(v, h):
    lane = lax.broadcasted_iota(jnp.int32, v.shape, 1)
    return jnp.where((lane >= h * SB_HEAD_DIM) & (lane < (h + 1) * SB_HEAD_DIM), v, 0.0).astype(BF16)


def _group_suffix(u_mat, xv, carry, negate=False):
    nblk = xv.shape[0] // SB_BLOCK
    parts = [None] * nblk
    for j in reversed(range(nblk)):
        xj = xv[j * SB_BLOCK:(j + 1) * SB_BLOCK]
        parts[j] = _suffix_sum(u_mat, xj) + carry
        total = jnp.sum(xj, axis=0, keepdims=True)
        carry = carry - total if negate else carry + total
    return jnp.concatenate(parts, axis=0), carry


def _sb_probs(kgrp, qt_h, carry, past):
    z = _dot(kgrp, qt_h)
    softplus = jnp.maximum(z, 0.0) + jnp.log(1.0 + jnp.exp(-jnp.abs(z)))
    if past is not None:
        softplus = jnp.where(past, softplus, 0.0)
    later, carry = _group_suffix(_scan_matrix(True, -1.0), softplus, carry, negate=True)
    a = jnp.exp(z + later)
    if past is not None:
        a = jnp.where(past, a, 0.0)
    return a, z, carry


def _sb_schedule(nq):
    steps = [(qi, g) for qi in range(nq) for g in range(qi, -1, -1)]
    return jnp.asarray([s[0] for s in steps], jnp.int32), jnp.asarray([s[1] for s in steps], jnp.int32)


def _past_mask(tq):
    return lax.broadcasted_iota(jnp.int32, (tq, tq), 0) < lax.broadcasted_iota(jnp.int32, (tq, tq), 1)


def _sb_fwd(proj, vt4, tq, shards):
    t = proj.shape[0]
    pairs = D_MODEL // SB_BLOCK
    nq = t // tq
    qi_tab, g_tab = _sb_schedule(nq)
    ns = qi_tab.shape[0]
    n = len(shards)

    def body(qi_ref, g_ref, q_ref, k_ref, vt_ref, *rest):
        o_ref, a_ref, z_ref = rest[n:n + 3]
        acc_ref, carry_ref, qt_ref = rest[2 * n + 3:2 * n + 6]
        begin, relay, finish = _gather_phases(rest[:n], rest[n + 3:2 * n + 3], *rest[2 * n + 6:])
        pi, si = pl.program_id(0), pl.program_id(1)
        diagonal = g_ref[si] == qi_ref[si]
        pl.when((pi == 0) & (si == 0))(begin)
        pl.when((pi == pairs // 2) & (si == 0))(relay)

        @pl.when(diagonal)
        def _():
            acc_ref[...] = jnp.zeros_like(acc_ref)
            carry_ref[...] = jnp.zeros_like(carry_ref)
            qt = q_ref[...].astype(F32).T * (SB_HEAD_DIM ** -0.5)
            for h in range(2):
                qt_ref[h] = _head_rows(qt, h)

        def step(past):
            for h in range(2):
                a, z, carry = _sb_probs(k_ref[...], qt_ref[h], carry_ref[h], past)
                ab = a.astype(BF16)
                a_ref[h] = ab
                z_ref[h] = z.astype(BF16)
                acc_ref[h] += _dot(vt_ref[h * SB_HEAD_DIM:(h + 1) * SB_HEAD_DIM, :], ab)
                carry_ref[h] = carry

        pl.when(diagonal)(lambda: step(_past_mask(tq)))
        pl.when(jnp.logical_not(diagonal))(lambda: step(None))

        @pl.when(g_ref[si] == 0)
        def _():
            o_ref[...] = jnp.concatenate([acc_ref[0], acc_ref[1]], axis=0).T

        pl.when((pi == pairs - 1) & (si == ns - 1))(finish)

    tile = pl.BlockSpec((None, None, 2, tq, tq), lambda p, s, qt_, gt_: (p, s, 0, 0, 0))
    tiles = jax.ShapeDtypeStruct((pairs, ns, 2, tq, tq), BF16)
    outs = pl.pallas_call(
        body, name="sb_fwd",
        grid_spec=pltpu.PrefetchScalarGridSpec(
            num_scalar_prefetch=2, grid=(pairs, ns),
            in_specs=[pl.BlockSpec((tq, SB_BLOCK), lambda p, s, qt_, gt_: (qt_[s], COL_SQ * pairs + p)),
                      pl.BlockSpec((tq, SB_BLOCK), lambda p, s, qt_, gt_: (gt_[s], COL_SK * pairs + p)),
                      pl.BlockSpec((None, None, SB_BLOCK, tq), lambda p, s, qt_, gt_: (p, gt_[s], 0, 0))] + [ANY] * n,
            out_specs=[pl.BlockSpec((tq, SB_BLOCK), lambda p, s, qt_, gt_: (qt_[s], p)), tile, tile] + [ANY] * n,
            scratch_shapes=[pltpu.VMEM((2, SB_HEAD_DIM, tq), F32), pltpu.VMEM((2, 1, tq), F32),
                            pltpu.VMEM((2, SB_BLOCK, tq), BF16)] + _gather_sems(n)),
        out_shape=[jax.ShapeDtypeStruct((t, D_MODEL), F32), tiles, tiles] + _gather_shapes(shards),
        compiler_params=pltpu.CompilerParams(dimension_semantics=("arbitrary", "arbitrary")),
    )(qi_tab, g_tab, proj, proj, vt4, *shards)
    return outs[0], outs[1], outs[2], outs[3:]


def _mem_prep(mem, g_mem, wkv_all, k_norm_g):
    m, d = mem.shape

    def body(mem_ref, g_ref, w_ref, kg_ref, memn_ref, kn_ref, v_ref):
        memf = mem_ref[...]
        memn = (memf * _rstd(memf) * g_ref[...]).astype(BF16)
        memn_ref[...] = memn
        for b in range(N_DEV):
            kv = _dot(memn, w_ref[b])
            if b < X_HEADS:
                kn_ref[:, b * X_HEAD_DIM:(b + 1) * X_HEAD_DIM] = (kv * _rstd(kv) * kg_ref[...]).astype(BF16)
            else:
                h = b - X_HEADS
                v_ref[:, h * X_HEAD_DIM:(h + 1) * X_HEAD_DIM] = kv.astype(BF16)

    return pl.pallas_call(
        body, name="mem_prep",
        out_shape=[jax.ShapeDtypeStruct((m, d), BF16)] * 3,
    )(mem, g_mem, wkv_all, k_norm_g)


def _x_head(xq_ref, qg, kn_ref, h):
    sl = slice(h * X_HEAD_DIM, (h + 1) * X_HEAD_DIM)
    q = xq_ref[:, sl].astype(F32)
    rq = _rstd(q)
    qhat = q * rq
    qn = (qhat * qg).astype(BF16)
    s = _dot_nt(qn, kn_ref[:, sl]) * (X_HEAD_DIM ** -0.5)
    e = jnp.exp(s - jnp.max(s, axis=-1, keepdims=True))
    p = e / jnp.sum(e, axis=-1, keepdims=True)
    return sl, rq, qhat, qn, p


def _x_fwd(proj, q_norm_g, kn, v, tm):
    t = proj.shape[0]
    m = kn.shape[0]

    def body(xq_ref, qg_ref, kn_ref, v_ref, o_ref):
        for h in range(X_HEADS):
            sl, _, _, _, p = _x_head(xq_ref, qg_ref[...], kn_ref, h)
            o_ref[:, sl] = _dot(p.astype(BF16), v_ref[:, sl]).astype(BF16)

    return pl.pallas_call(
        body, name="x_fwd",
        grid=(t // tm,),
        in_specs=[pl.BlockSpec((tm, D_MODEL), lambda i: (i, COL_XQ)),
                  pl.BlockSpec((1, X_HEAD_DIM), lambda i: (0, 0)),
                  pl.BlockSpec((m, D_MODEL), lambda i: (0, 0)),
                  pl.BlockSpec((m, D_MODEL), lambda i: (0, 0))],
        out_specs=pl.BlockSpec((tm, D_MODEL), lambda i: (i, 0)),
        out_shape=jax.ShapeDtypeStruct((t, D_MODEL), BF16),
        compiler_params=pltpu.CompilerParams(dimension_semantics=("parallel",)),
    )(proj, q_norm_g, kn, v)


def _gate_spec(tm, branch):
    return pl.BlockSpec((tm, D_MODEL), lambda i: (i, COL_GATE + branch))


def _merge_fwd(x, proj, a_conv, o_sb, o_x, w_conv_out, w_sb_out, w_x_out, w_out, tm):
    t, d = x.shape

    def body(x_ref, g0_ref, g1_ref, g2_ref, a_ref, s_ref, xo_ref, wc_ref, ws_ref, wx_ref, wo_ref,
             x1_ref, yc_ref, ys_ref, yx_ref, mg_ref):
        merged = jnp.zeros((tm, d), F32)
        for gate_ref, b_ref, w_ref, y_ref in ((g0_ref, a_ref, wc_ref, yc_ref), (g1_ref, s_ref, ws_ref, ys_ref),
                                              (g2_ref, xo_ref, wx_ref, yx_ref)):
            yv = _dot(b_ref[...].astype(BF16), w_ref[...])
            y_ref[...] = yv.astype(BF16)
            merged = merged + _sigmoid(gate_ref[...].astype(F32)) * yv
        mb = merged.astype(BF16)
        mg_ref[...] = mb
        x1_ref[...] = x_ref[...] + _dot(mb, wo_ref[...])

    tile = pl.BlockSpec((tm, d), lambda i: (i, 0))
    wfull = pl.BlockSpec((d, d), lambda i: (0, 0))
    return pl.pallas_call(
        body, name="merge_fwd",
        grid=(t // tm,),
        in_specs=[tile] + [_gate_spec(tm, b) for b in range(N_BRANCH)] + [tile, tile, tile,
                                                                           wfull, wfull, wfull, wfull],
        out_specs=[tile] * 5,
        out_shape=[jax.ShapeDtypeStruct((t, d), F32)] + [jax.ShapeDtypeStruct((t, d), BF16)] * 4,
        compiler_params=pltpu.CompilerParams(dimension_semantics=("parallel",)),
    )(x, proj, proj, proj, a_conv, o_sb, o_x, w_conv_out, w_sb_out, w_x_out, w_out)


def _mlp_fwd(x1, g_mlp, w_up_all, w_down, target, tm):
    t, d = x1.shape
    nb, _, fw = w_up_all.shape

    def body(x1_ref, g_ref, wu_ref, wd_ref, tgt_ref, up_ref, h2_ref, dx2_ref, lsum_ref, acc_ref):
        i, j = pl.program_id(0), pl.program_id(1)

        @pl.when(j == 0)
        def _():
            xf = x1_ref[...]
            h2_ref[...] = (xf * _rstd(xf) * g_ref[...]).astype(BF16)
            acc_ref[...] = jnp.zeros_like(acc_ref)

        @pl.when((i == 0) & (j == 0))
        def _():
            lsum_ref[...] = jnp.zeros_like(lsum_ref)

        up = _dot(h2_ref[...], wu_ref[...])
        up_ref[...] = up.astype(BF16)
        act = jnp.square(jnp.maximum(up, 0.0)).astype(BF16)
        acc_ref[...] += _dot(act, wd_ref[...])

        @pl.when(j == nb - 1)
        def _():
            diff = x1_ref[...] + acc_ref[...] - tgt_ref[...]
            dx2_ref[...] = diff * (1.0 / d)
            lsum_ref[...] += jnp.sum(diff * diff, axis=0, keepdims=True)

    tile = pl.BlockSpec((tm, d), lambda i, j: (i, 0))
    row = pl.BlockSpec((1, d), lambda i, j: (0, 0))
    return pl.pallas_call(
        body, name="mlp_fwd",
        grid=(t // tm, nb),
        in_specs=[tile, row, pl.BlockSpec((None, d, fw), lambda i, j: (j, 0, 0)),
                  pl.BlockSpec((fw, d), lambda i, j: (j, 0)), tile],
        out_specs=[pl.BlockSpec((tm, fw), lambda i, j: (i, j)), tile, tile, row],
        out_shape=[jax.ShapeDtypeStruct((t, nb * fw), BF16), jax.ShapeDtypeStruct((t, d), BF16),
                   jax.ShapeDtypeStruct((t, d), F32), jax.ShapeDtypeStruct((1, d), F32)],
        scratch_shapes=[pltpu.VMEM((tm, d), F32)],
        compiler_params=pltpu.CompilerParams(dimension_semantics=("arbitrary", "arbitrary")),
    )(x1, g_mlp, w_up_all, w_down, target)


def _mlp_bwd(x1, g_mlp, w_up_all, w_down, up, dx2, tm):
    t, d = x1.shape
    nb, _, fw = w_up_all.shape

    def body(x1_ref, g_ref, wu_ref, wd_ref, up_ref, dx2_ref, dup_ref, act_ref, dx1_ref, dg_ref, acc_ref, dyb_ref):
        i, j = pl.program_id(0), pl.program_id(1)

        @pl.when(j == 0)
        def _():
            dyb_ref[...] = dx2_ref[...].astype(BF16)
            acc_ref[...] = jnp.zeros_like(acc_ref)

        @pl.when((i == 0) & (j == 0))
        def _():
            dg_ref[...] = jnp.zeros_like(dg_ref)

        r = jnp.maximum(up_ref[...].astype(F32), 0.0)
        act_ref[...] = jnp.square(r).astype(BF16)
        dup = (_dot_nt(dyb_ref[...], wd_ref[...]) * (2.0 * r)).astype(BF16)
        dup_ref[...] = dup
        acc_ref[...] += _dot_nt(dup, wu_ref[...])

        @pl.when(j == nb - 1)
        def _():
            xf = x1_ref[...]
            rs = _rstd(xf)
            xhat = xf * rs
            dh2 = acc_ref[...]
            dg_ref[...] += jnp.sum(dh2 * xhat, axis=0, keepdims=True)
            dx1_ref[...] = dx2_ref[...] + _rms_bwd(dh2, xhat, rs, g_ref[...])

    tile = pl.BlockSpec((tm, d), lambda i, j: (i, 0))
    row = pl.BlockSpec((1, d), lambda i, j: (0, 0))
    ff = pl.BlockSpec((tm, fw), lambda i, j: (i, j))
    return pl.pallas_call(
        body, name="mlp_bwd",
        grid=(t // tm, nb),
        in_specs=[tile, row, pl.BlockSpec((None, d, fw), lambda i, j: (j, 0, 0)),
                  pl.BlockSpec((fw, d), lambda i, j: (j, 0)), ff, tile],
        out_specs=[ff, ff, tile, row],
        out_shape=[jax.ShapeDtypeStruct((t, nb * fw), BF16), jax.ShapeDtypeStruct((t, nb * fw), BF16),
                   jax.ShapeDtypeStruct((t, d), F32), jax.ShapeDtypeStruct((1, d), F32)],
        scratch_shapes=[pltpu.VMEM((tm, d), F32), pltpu.VMEM((tm, d), BF16)],
        compiler_params=pltpu.CompilerParams(dimension_semantics=("arbitrary", "arbitrary")),
    )(x1, g_mlp, w_up_all, w_down, up, dx2)


def _merge_bwd(dx1, proj, y_conv, y_sb, y_x, w_conv_out, w_sb_out, w_x_out, w_out, tm):
    t, d = dx1.shape

    def body(dx1_ref, g0_ref, g1_ref, g2_ref, yc_ref, ys_ref, yx_ref, wc_ref, ws_ref, wx_ref, wo_ref,
             dgate_ref, dyc_ref, dys_ref, dyx_ref, da_ref, dos_ref, dox_ref):
        dm = _dot_nt(dx1_ref[...].astype(BF16), wo_ref[...])
        for i, (gate_ref, y_ref, w_ref, dy_ref, db_ref) in enumerate(((g0_ref, yc_ref, wc_ref, dyc_ref, da_ref),
                                                                       (g1_ref, ys_ref, ws_ref, dys_ref, dos_ref),
                                                                       (g2_ref, yx_ref, wx_ref, dyx_ref, dox_ref))):
            gt = _sigmoid(gate_ref[...].astype(F32))
            dy = (dm * gt).astype(BF16)
            dy_ref[...] = dy
            dgate_ref[:, i * d:(i + 1) * d] = (dm * y_ref[...].astype(F32) * gt * (1.0 - gt)).astype(BF16)
            db_ref[...] = _dot_nt(dy, w_ref[...]).astype(BF16)

    tile = pl.BlockSpec((tm, d), lambda i: (i, 0))
    wfull = pl.BlockSpec((d, d), lambda i: (0, 0))
    return pl.pallas_call(
        body, name="merge_bwd",
        grid=(t // tm,),
        in_specs=[tile] + [_gate_spec(tm, b) for b in range(N_BRANCH)] + [tile, tile, tile,
                                                                           wfull, wfull, wfull, wfull],
        out_specs=[pl.BlockSpec((tm, N_BRANCH * d), lambda i: (i, 0))] + [tile] * 6,
        out_shape=[jax.ShapeDtypeStruct((t, N_BRANCH * d), BF16)] + [jax.ShapeDtypeStruct((t, d), BF16)] * 6,
        compiler_params=pltpu.CompilerParams(dimension_semantics=("parallel",)),
    )(dx1, proj, proj, proj, y_conv, y_sb, y_x, w_conv_out, w_sb_out, w_x_out, w_out)


def _conv_bwd(proj, conv_w, da, cw):
    t = proj.shape[0]
    nper = D_MODEL // cw

    def body(ch_ref, cb_ref, cc_ref, w_ref, da_ref, dch_ref, dcb_ref, dcc_ref, dw_ref):
        ch, cb, cc, u, u1, u2, cv, w, row = _conv_terms(ch_ref, cb_ref, cc_ref, w_ref)
        dav = da_ref[...].astype(F32)
        dcb_ref[...] = (dav * cv).astype(BF16)
        dcv = dav * cb
        n1 = jnp.where(row < t - 1, pltpu.roll(dcv, t - 1, 0), 0.0)
        n2 = jnp.where(row < t - 2, pltpu.roll(dcv, t - 2, 0), 0.0)
        du = w[2] * dcv + w[1] * n1 + w[0] * n2
        dcc_ref[...] = (du * ch).astype(BF16)
        dch_ref[...] = (du * cc).astype(BF16)
        dw_ref[0:1, :] = jnp.sum(dcv * u2, axis=0, keepdims=True)
        dw_ref[1:2, :] = jnp.sum(dcv * u1, axis=0, keepdims=True)
        dw_ref[2:3, :] = jnp.sum(dcv * u, axis=0, keepdims=True)

    def col(piece):
        return pl.BlockSpec((t, cw), lambda j: (0, piece * nper + j))

    out_col = pl.BlockSpec((t, cw), lambda j: (0, j))
    wspec = pl.BlockSpec((3, cw), lambda j: (0, j))
    return pl.pallas_call(
        body, name="conv_bwd",
        grid=(nper,),
        in_specs=[col(COL_CH), col(COL_CB), col(COL_CC), wspec, out_col],
        out_specs=[out_col, out_col, out_col, wspec],
        out_shape=[jax.ShapeDtypeStruct((t, D_MODEL), BF16)] * 3 + [jax.ShapeDtypeStruct((3, D_MODEL), F32)],
        compiler_params=pltpu.CompilerParams(dimension_semantics=("parallel",)),
    )(proj, proj, proj, conv_w, da)


def _sb_bwd(proj, kt4, do_sb, o_sb, weights, logits, tq, pair_sums):
    t = proj.shape[0]
    nq = t // tq
    pairs = D_MODEL // SB_BLOCK
    scale = SB_HEAD_DIM ** -0.5
    qi_tab, g_tab = _sb_schedule(nq)
    ns = qi_tab.shape[0]
    n = len(pair_sums)

    def body(qi_ref, g_ref, q_ref, v_ref, kt_ref, do_ref, o_ref, a_ref, z_ref, *rest):
        dq_ref, dk_ref, dv_ref = rest[n:n + 3]
        dk_acc, dv_acc, dqt_ref, carry_ref, qm_ref, dom_ref, dot_ref, dsum_ref = rest[2 * n + 3:2 * n + 11]
        begin, finish = _chip_exchange_phases(rest[:n], rest[n + 3:2 * n + 3], *rest[2 * n + 11:])
        pi, si = pl.program_id(0), pl.program_id(1)
        diagonal = g_ref[si] == qi_ref[si]
        pl.when((pi == 0) & (si == 0))(begin)

        @pl.when(si == 0)
        def _():
            dk_acc[...] = jnp.zeros_like(dk_acc)
            dv_acc[...] = jnp.zeros_like(dv_acc)

        @pl.when(diagonal)
        def _():
            dqt_ref[...] = jnp.zeros_like(dqt_ref)
            carry_ref[...] = jnp.zeros_like(carry_ref)
            q = q_ref[...].astype(F32) * scale
            do = do_ref[...].astype(F32)
            dot_ = do.T
            prod = dot_ * o_ref[...].T
            for h in range(2):
                rows = slice(h * SB_HEAD_DIM, (h + 1) * SB_HEAD_DIM)
                qm_ref[h] = _head_lanes(q, h)
                dom_ref[h] = _head_lanes(do, h)
                dot_ref[h] = _head_rows(dot_, h)
                dsum_ref[h] = jnp.sum(prod[rows, :], axis=0, keepdims=True)

        def step(past):
            u_mat = _scan_matrix(False, 1.0)
            ks = pl.multiple_of(g_ref[si] * tq, tq)
            dk_add = jnp.zeros((tq, SB_BLOCK), F32)
            dv_add = jnp.zeros((tq, SB_BLOCK), F32)
            for h in range(2):
                rows = slice(h * SB_HEAD_DIM, (h + 1) * SB_HEAD_DIM)
                ab = a_ref[h]
                gw = _dot(v_ref[...], dot_ref[h]) * ab.astype(F32)
                after, carry = _group_suffix(u_mat, gw, carry_ref[h])
                sig = pl.reciprocal(1.0 + jnp.exp(-z_ref[h].astype(F32)), approx=True)
                dz = gw - sig * (dsum_ref[h] - after)
                if past is not None:
                    dz = jnp.where(past, dz, 0.0)
                dzb = dz.astype(BF16)
                dqt_ref[h] += _dot(kt_ref[rows, :], dzb)
                dk_add = dk_add + _dot(dzb, qm_ref[h])
                dv_add = dv_add + _dot(ab, dom_ref[h])
                carry_ref[h] = carry
            dk_acc[pl.ds(ks, tq), :] += dk_add
            dv_acc[pl.ds(ks, tq), :] += dv_add

        pl.when(diagonal)(lambda: step(_past_mask(tq)))
        pl.when(jnp.logical_not(diagonal))(lambda: step(None))

        @pl.when(g_ref[si] == 0)
        def _():
            dq_ref[...] = (jnp.concatenate([dqt_ref[0], dqt_ref[1]], axis=0).T * scale).astype(BF16)

        @pl.when(si == ns - 1)
        def _():
            dk_ref[...] = dk_acc[...].astype(BF16)
            dv_ref[...] = dv_acc[...].astype(BF16)

        pl.when((pi == pairs - 1) & (si == ns - 1))(finish)

    qblk = lambda base: pl.BlockSpec((tq, SB_BLOCK), lambda p, s, qt_, gt_: (qt_[s], base * pairs + p))
    kgrp = lambda base: pl.BlockSpec((tq, SB_BLOCK), lambda p, s, qt_, gt_: (gt_[s], base * pairs + p))
    seq = pl.BlockSpec((t, SB_BLOCK), lambda p, s, qt_, gt_: (0, p))
    tr = pl.BlockSpec((None, None, SB_BLOCK, tq), lambda p, s, qt_, gt_: (p, gt_[s], 0, 0))
    tile = pl.BlockSpec((None, None, 2, tq, tq), lambda p, s, qt_, gt_: (p, s, 0, 0, 0))
    outs = pl.pallas_call(
        body, name="sb_bwd",
        grid_spec=pltpu.PrefetchScalarGridSpec(
            num_scalar_prefetch=2, grid=(pairs, ns),
            in_specs=[qblk(COL_SQ), kgrp(COL_SV), tr, qblk(0), qblk(0), tile, tile] + [ANY] * n,
            out_specs=[qblk(0), seq, seq] + [ANY] * n,
            scratch_shapes=[pltpu.VMEM((t, SB_BLOCK), F32), pltpu.VMEM((t, SB_BLOCK), F32),
                            pltpu.VMEM((2, SB_HEAD_DIM, tq), F32), pltpu.VMEM((2, 1, tq), F32),
                            pltpu.VMEM((2, tq, SB_BLOCK), BF16), pltpu.VMEM((2, tq, SB_BLOCK), BF16),
                            pltpu.VMEM((2, SB_BLOCK, tq), BF16), pltpu.VMEM((2, 1, tq), F32)] + _chip_exchange_sems(n)),
        out_shape=[jax.ShapeDtypeStruct((t, D_MODEL), BF16)] * 3 + _chip_exchange_shapes(pair_sums),
        compiler_params=pltpu.CompilerParams(dimension_semantics=("arbitrary", "arbitrary")),
    )(qi_tab, g_tab, proj, proj, kt4, do_sb, o_sb, weights, logits, *pair_sums)
    return outs[0], outs[1], outs[2], outs[3:]


def _x_bwd(proj, q_norm_g, kn, v, do_x, tm, grads):
    t = proj.shape[0]
    m = kn.shape[0]
    scale = X_HEAD_DIM ** -0.5
    nt = t // tm
    n = len(grads)

    def body(xq_ref, qg_ref, kn_ref, v_ref, do_ref, *rest):
        dxq_ref, dkn_ref, dv_ref, dqg_ref = rest[n:n + 4]
        begin, finish = _pair_exchange_phases(rest[:n], rest[n + 4:2 * n + 4], *rest[2 * n + 4:])

        @pl.when(pl.program_id(0) == 0)
        def _():
            begin()
            dkn_ref[...] = jnp.zeros_like(dkn_ref)
            dv_ref[...] = jnp.zeros_like(dv_ref)
            dqg_ref[...] = jnp.zeros_like(dqg_ref)

        qg = qg_ref[...]
        for h in range(X_HEADS):
            sl, rq, qhat, qn, p = _x_head(xq_ref, qg, kn_ref, h)
            do_h = do_ref[:, sl]
            dp = _dot_nt(do_h, v_ref[:, sl])
            ds = (p * (dp - jnp.sum(dp * p, axis=-1, keepdims=True)) * scale).astype(BF16)
            dqn = _dot(ds, kn_ref[:, sl])
            dkn_ref[:, sl] += _dot_tn(ds, qn)
            dv_ref[:, sl] += _dot_tn(p.astype(BF16), do_h)
            dqg_ref[...] += jnp.sum(dqn * qhat, axis=0, keepdims=True)
            dxq_ref[:, sl] = _rms_bwd(dqn, qhat, rq, qg).astype(BF16)

        pl.when(pl.program_id(0) == nt - 1)(finish)

    full = pl.BlockSpec((m, D_MODEL), lambda i: (0, 0))
    gain = pl.BlockSpec((1, X_HEAD_DIM), lambda i: (0, 0))
    tile = pl.BlockSpec((tm, D_MODEL), lambda i: (i, 0))
    outs = pl.pallas_call(
        body, name="x_bwd",
        grid=(nt,),
        in_specs=[pl.BlockSpec((tm, D_MODEL), lambda i: (i, COL_XQ)), gain, full, full, tile] + [ANY] * n,
        out_specs=[tile, full, full, gain] + [ANY] * n,
        out_shape=[jax.ShapeDtypeStruct((t, D_MODEL), BF16), jax.ShapeDtypeStruct((m, D_MODEL), F32),
                   jax.ShapeDtypeStruct((m, D_MODEL), F32), jax.ShapeDtypeStruct((1, X_HEAD_DIM), F32)]
        + _pair_exchange_shapes(grads),
        scratch_shapes=_pair_exchange_sems(n),
        compiler_params=pltpu.CompilerParams(dimension_semantics=("arbitrary",)),
    )(proj, q_norm_g, kn, v, do_x, *grads)
    return outs[0], outs[1], outs[2], outs[3], outs[4:]


def _mem_bwd(mem, g_mem, wkv_all, k_norm_g, dkn, dv):
    m, d = mem.shape

    def body(mem_ref, g_ref, w_ref, kg_ref, dkn_ref, dv_ref, dkv_ref, dgm_ref, dkg_ref):
        memf = mem_ref[...]
        mem_hat = memf * _rstd(memf)
        memn = (mem_hat * g_ref[...]).astype(BF16)
        kg = kg_ref[...]
        dmemn = jnp.zeros((m, d), F32)
        dkg = jnp.zeros((1, X_HEAD_DIM), F32)
        for b in range(N_DEV):
            sl = slice(b * X_HEAD_DIM, (b + 1) * X_HEAD_DIM)
            if b < X_HEADS:
                kv = _dot(memn, w_ref[b])
                rk = _rstd(kv)
                khat = kv * rk
                dkn_h = dkn_ref[:, sl]
                dkg = dkg + jnp.sum(dkn_h * khat, axis=0, keepdims=True)
                dblk = _rms_bwd(dkn_h, khat, rk, kg).astype(BF16)
            else:
                hs = slice((b - X_HEADS) * X_HEAD_DIM, (b - X_HEADS + 1) * X_HEAD_DIM)
                dblk = dv_ref[:, hs].astype(BF16)
            dkv_ref[:, sl] = dblk
            dmemn = dmemn + _dot_nt(dblk, w_ref[b])
        dgm_ref[...] = jnp.sum(dmemn * mem_hat, axis=0, keepdims=True)
        dkg_ref[...] = dkg

    return pl.pallas_call(
        body, name="mem_bwd",
        out_shape=[jax.ShapeDtypeStruct((m, 2 * d), BF16), jax.ShapeDtypeStruct((1, d), F32),
                   jax.ShapeDtypeStruct((1, X_HEAD_DIM), F32)],
    )(mem, g_mem, wkv_all, k_norm_g, dkn, dv)


def _in_proj_bwd(x, g_mix, w_in_all, dproj, dx1, tm, pair_sums):
    t, d = x.shape
    nb, _, bw = w_in_all.shape
    nt = t // tm
    n = len(pair_sums)

    def body(x_ref, g_ref, w_ref, dp_ref, dx1_ref, *rest):
        dx_ref, dg_ref = rest[n:n + 2]
        acc_ref = rest[2 * n + 2]
        begin, finish = _chip_exchange_phases(rest[:n], rest[n + 2:2 * n + 2], *rest[2 * n + 3:])
        i, j = pl.program_id(0), pl.program_id(1)
        pl.when((i == 0) & (j == 0))(begin)

        @pl.when(j == 0)
        def _():
            acc_ref[...] = jnp.zeros_like(acc_ref)

        @pl.when((i == 0) & (j == 0))
        def _():
            dg_ref[...] = jnp.zeros_like(dg_ref)

        acc_ref[...] += _dot_nt(dp_ref[...], w_ref[...])

        @pl.when(j == nb - 1)
        def _():
            xf = x_ref[...]
            rs = _rstd(xf)
            xhat = xf * rs
            dh = acc_ref[...]
            dg_ref[...] += jnp.sum(dh * xhat, axis=0, keepdims=True)
            dx_ref[...] = dx1_ref[...] + _rms_bwd(dh, xhat, rs, g_ref[...])

        pl.when((i == nt - 1) & (j == nb - 1))(finish)

    tile = pl.BlockSpec((tm, d), lambda i, j: (i, 0))
    row = pl.BlockSpec((1, d), lambda i, j: (0, 0))
    outs = pl.pallas_call(
        body, name="in_proj_bwd",
        grid=(nt, nb),
        in_specs=[tile, row, pl.BlockSpec((None, d, bw), lambda i, j: (j, 0, 0)),
                  pl.BlockSpec((tm, bw), lambda i, j: (i, j)), tile] + [ANY] * n,
        out_specs=[tile, row] + [ANY] * n,
        out_shape=[jax.ShapeDtypeStruct((t, d), F32), jax.ShapeDtypeStruct((1, d), F32)] + _chip_exchange_shapes(pair_sums),
        scratch_shapes=[pltpu.VMEM((tm, d), F32)] + _chip_exchange_sems(n),
        compiler_params=pltpu.CompilerParams(dimension_semantics=("arbitrary", "arbitrary")),
    )(x, g_mix, w_in_all, dproj, dx1, *pair_sums)
    return outs[0], outs[1], outs[2:]


def _weight_grad(a, b, bw, tmm, name):
    t, m = a.shape
    n = b.shape[1]
    tmm = min(tmm, m)

    def body(a_ref, b_ref, o_ref):
        o_ref[...] = _dot_tn(a_ref[...].astype(BF16), b_ref[...].astype(BF16)).astype(BF16)

    return pl.pallas_call(
        body, name=name,
        grid=(m // tmm, n // bw),
        in_specs=[pl.BlockSpec((t, tmm), lambda i, j: (0, i)), pl.BlockSpec((t, bw), lambda i, j: (0, j))],
        out_specs=pl.BlockSpec((None, tmm, bw), lambda i, j: (j, i, 0)),
        out_shape=jax.ShapeDtypeStruct((n // bw, m, bw), BF16),
        compiler_params=pltpu.CompilerParams(dimension_semantics=("parallel", "parallel")),
    )(a, b)


def _pair_sum(grad, recv, own_blocks, name):
    _, rows, cols = grad.shape

    def body(idx_ref, g_ref, r_ref, o_ref):
        o_ref[...] = (g_ref[...].astype(F32) + r_ref[...].astype(F32)).astype(BF16)

    return pl.pallas_call(
        body, name=name,
        grid_spec=pltpu.PrefetchScalarGridSpec(
            num_scalar_prefetch=1, grid=(4,),
            in_specs=[pl.BlockSpec((None, rows, cols), lambda r, idx: (idx[r], 0, 0)),
                      pl.BlockSpec((None, rows, cols), lambda r, idx: (r, 0, 0))],
            out_specs=pl.BlockSpec((None, rows, cols), lambda r, idx: (r, 0, 0))),
        out_shape=jax.ShapeDtypeStruct((4, rows, cols), BF16),
        compiler_params=pltpu.CompilerParams(dimension_semantics=("parallel",)),
    )(own_blocks, grad, recv)


def _adamw_math(w, g, m, v):
    m = ADAM_B1 * m + (1.0 - ADAM_B1) * g
    v = ADAM_B2 * v + (1.0 - ADAM_B2) * jnp.square(g)
    m_hat = m / (1.0 - ADAM_B1 ** ADAM_STEP)
    v_hat = v / (1.0 - ADAM_B2 ** ADAM_STEP)
    delta = -ADAM_LR * (m_hat / (jnp.sqrt(v_hat) + ADAM_EPS) + ADAM_WD * w)
    return delta, m, v


def _adamw_sharded(pair_sums, recv, w, m, v, tr, name):
    rows, cols = w.shape
    tr = min(tr, rows)

    def body(h_ref, r_ref, w_ref, m_ref, v_ref, g_out, d_out, m_out, v_out):
        g = h_ref[...].astype(F32)
        for r in range(3):
            g = g + r_ref[r].astype(F32)
        g_out[...] = g
        d_out[...], m_out[...], v_out[...] = _adamw_math(w_ref[...], g, m_ref[...], v_ref[...])

    tile = pl.BlockSpec((tr, cols), lambda i: (i, 0))
    return pl.pallas_call(
        body, name=name,
        grid=(rows // tr,),
        in_specs=[pl.BlockSpec((None, tr, cols), lambda i: (0, i, 0)),
                  pl.BlockSpec((3, tr, cols), lambda i: (0, i, 0)), tile, tile, tile],
        out_specs=[tile] * 4,
        out_shape=[jax.ShapeDtypeStruct((rows, cols), F32)] * 4,
        compiler_params=pltpu.CompilerParams(dimension_semantics=("parallel",)),
    )(pair_sums, recv, w, m, v)


SMALL_ROWS = 16


def _pack_rows(dg_mix, dg_mem, dg_mlp, dqg, dkg, dconv, lsum):
    def body(a_ref, b_ref, c_ref, q_ref, k_ref, cv_ref, l_ref, o_ref):
        o_ref[...] = jnp.zeros_like(o_ref)
        for r, ref in enumerate((a_ref, b_ref, c_ref)):
            o_ref[r:r + 1, :] = ref[...]
        o_ref[3:4, :X_HEAD_DIM] = q_ref[...]
        o_ref[4:5, :X_HEAD_DIM] = k_ref[...]
        o_ref[5:8, :] = cv_ref[...]
        o_ref[8:9, :] = l_ref[...]

    return pl.pallas_call(body, name="small_pack", out_shape=jax.ShapeDtypeStruct((SMALL_ROWS, D_MODEL), F32))(
        dg_mix, dg_mem, dg_mlp, dqg, dkg, dconv, lsum)


def _small_sum(gathered):
    def body(g_ref, o_ref):
        total = g_ref[0]
        for dev in range(1, N_DEV):
            total = total + g_ref[dev]
        o_ref[...] = jnp.zeros_like(o_ref)
        for piece in range(5):
            o_ref[piece * SMALL_TILE:piece * SMALL_TILE + 1, :] = total[piece:piece + 1]
        o_ref[5 * SMALL_TILE:5 * SMALL_TILE + 3, :] = total[5:8]
        o_ref[6 * SMALL_TILE:6 * SMALL_TILE + 1, :] = total[8:9]

    return pl.pallas_call(body, name="small_grad_sum",
                          out_shape=jax.ShapeDtypeStruct((7 * SMALL_TILE, D_MODEL), F32))(gathered)


def _adamw_small(w, g, m, v):
    def body(w_ref, g_ref, m_ref, v_ref, d_out, m_out, v_out):
        d_out[...], m_out[...], v_out[...] = _adamw_math(w_ref[...], g_ref[...], m_ref[...], v_ref[...])

    return pl.pallas_call(body, name="adamw_small", out_shape=[jax.ShapeDtypeStruct(w.shape, F32)] * 3)(w, g, m, v)


def _pad_tile(a):
    return jnp.pad(a, ((0, SMALL_TILE - a.shape[0]), (0, D_MODEL - a.shape[1])))


def _pack_small(*pieces):
    return jnp.concatenate([_pad_tile(a) for a in pieces], axis=0)


def kernel(x, mem, g_mix, g_mem, w_in, conv_w, w_conv_out, w_sb_out, q_norm_g, k_norm_g, w_mem_kv, w_x_out, w_out, g_mlp, w_up, w_down, loss_target, m_g_mix, m_g_mem, m_w_in, m_conv_w, m_w_conv_out, m_w_sb_out, m_q_norm_g, m_k_norm_g, m_w_mem_kv, m_w_x_out, m_w_out, m_g_mlp, m_w_up, m_w_down, v_g_mix, v_g_mem, v_w_in, v_conv_w, v_w_conv_out, v_w_sb_out, v_q_norm_g, v_k_norm_g, v_w_mem_kv, v_w_x_out, v_w_out, v_g_mlp, v_w_up, v_w_down):
    xpos, ypos, cpos = _mesh_pos()
    me = 4 * xpos + 2 * ypos + cpos
    x2d, mem2d, tgt2d = x[0], mem[0], loss_target[0]
    t = x2d.shape[0]
    tm = min(512, t)
    tm_s = min(256, t)

    big = {
        "w_in": (w_in[0], m_w_in[0], v_w_in[0]),
        "w_conv_out": (w_conv_out[0], m_w_conv_out[0], v_w_conv_out[0]),
        "w_sb_out": (w_sb_out[0], m_w_sb_out[0], v_w_sb_out[0]),
        "w_mem_kv": (w_mem_kv[0], m_w_mem_kv[0], v_w_mem_kv[0]),
        "w_x_out": (w_x_out[0], m_w_x_out[0], v_w_x_out[0]),
        "w_out": (w_out[0], m_w_out[0], v_w_out[0]),
        "w_up": (w_up[0], m_w_up[0], v_w_up[0]),
        "w_down": (w_down[0], m_w_down[0], v_w_down[0]),
    }
    late = [n for n in big if n != "w_in"]
    as_bf16 = lambda group: [big[n][0].astype(BF16) for n in group]
    conv_pad = jnp.pad(conv_w[0], ((0, 8 - 3), (0, 0)))

    proj, h, (w_in_all, conv_all) = _in_proj(x2d, g_mix, _arrival_blocks(xpos, ypos, cpos), tm,
                                             [big["w_in"][0].astype(BF16), conv_pad])
    conv_full = conv_all[:, :3, :].transpose(1, 0, 2).reshape(3, D_MODEL)
    a_conv = _conv_fwd(proj, conv_full, 256)
    tq = min(SB_QUERY_TILE, t)
    pairs = D_MODEL // SB_BLOCK

    def groups_t(cols):
        return cols.reshape(t // tq, tq, pairs, SB_BLOCK).transpose(2, 0, 3, 1)

    kt4 = groups_t(proj[:, COL_SK * D_MODEL:(COL_SK + 1) * D_MODEL])
    vt4 = groups_t(proj[:, COL_SV * D_MODEL:(COL_SV + 1) * D_MODEL])
    o_sb, sb_weights, sb_logits, gathered = _sb_fwd(proj, vt4, tq, as_bf16(late))
    full = dict(zip(late, gathered))
    wkv_all, w_up_all = full["w_mem_kv"], full["w_up"]
    rows_full = lambda a: a.reshape(a.shape[0] * a.shape[1], a.shape[2])
    wc, ws, wx, wo, wd = (rows_full(full[n]) for n in ("w_conv_out", "w_sb_out", "w_x_out", "w_out", "w_down"))
    mem_n, kn, vmem = _mem_prep(mem2d, g_mem, wkv_all, k_norm_g)
    o_x = _x_fwd(proj, q_norm_g, kn, vmem, tm_s)
    x1, y_conv, y_sb, y_x, merged = _merge_fwd(x2d, proj, a_conv, o_sb, o_x, wc, ws, wx, wo, tm_s)
    up, h2, dx2, lsum = _mlp_fwd(x1, g_mlp, w_up_all, wd, tgt2d, tm)

    dup, act, dx1, dg_mlp = _mlp_bwd(x1, g_mlp, w_up_all, wd, up, dx2, tm)
    dgate, dy_conv, dy_sb, dy_x, da_conv, do_sb, do_x = _merge_bwd(dx1, proj, y_conv, y_sb, y_x, wc, ws, wx, wo, tm_s)
    dch, dcb, dcc, dconv = _conv_bwd(proj, conv_full, da_conv, 256)
    wgrads = {
        "w_conv_out": _weight_grad(a_conv, dy_conv, D_MODEL, 512, "dw_conv_out"),
        "w_sb_out": _weight_grad(o_sb, dy_sb, D_MODEL, 512, "dw_sb_out"),
        "w_x_out": _weight_grad(o_x, dy_x, D_MODEL, 512, "dw_x_out"),
        "w_out": _weight_grad(merged, dx1, D_MODEL, 512, "dw_out"),
        "w_up": _weight_grad(h2, dup, w_up_all.shape[2], 512, "dw_up"),
        "w_down": _weight_grad(act, dx2, D_MODEL, 512, "dw_down"),
    }

    own_blocks = jnp.stack([4 * (xpos ^ dx) + 2 * (ypos ^ dy) + cpos for dx in (0, 1) for dy in (0, 1)]).astype(jnp.int32)
    blocked = lambda n: wgrads[n].reshape((N_DEV,) + big[n][0].shape)
    pair_sum = lambda n, from_sibling: _pair_sum(blocked(n), from_sibling, own_blocks, "pair_sum_" + n)

    behind_x = list(wgrads)
    dxq, dkn, dvm, dqg, from_sibling = _x_bwd(proj, q_norm_g, kn, vmem, do_x, tm_s, [blocked(n) for n in behind_x])
    pair_sums = {n: pair_sum(n, r) for n, r in zip(behind_x, from_sibling)}
    dkv, dg_mem, dkg = _mem_bwd(mem2d, g_mem, wkv_all, k_norm_g, dkn, dvm)
    wgrads["w_mem_kv"] = _weight_grad(mem_n, dkv, wkv_all.shape[2], 512, "dw_mem_kv")
    pair_sums["w_mem_kv"] = pair_sum("w_mem_kv", _pair_exchange([blocked("w_mem_kv")], "grad_pair_exchange_w_mem_kv")[0])
    dq, dk, dv, from_chips_late = _sb_bwd(proj, kt4, do_sb, o_sb, sb_weights, sb_logits, tq,
                                          [pair_sums[n] for n in late])
    from_chips = dict(zip(late, from_chips_late))
    dproj = jnp.concatenate([dch, dcb, dcc, dq, dk, dv, dxq, dgate], axis=1)
    wgrads["w_in"] = _weight_grad(h, dproj, w_in_all.shape[2], 512, "dw_in")
    pair_sums["w_in"] = pair_sum("w_in", _pair_exchange([blocked("w_in")], "grad_pair_exchange_w_in")[0])
    grad_x, dg_mix, (from_chips["w_in"],) = _in_proj_bwd(x2d, g_mix, w_in_all, dproj, dx1, tm, [pair_sums["w_in"]])
    res = {}
    for n in big:
        w_sh, m_sh, v_sh = big[n]
        res[n] = _adamw_sharded(pair_sums[n], from_chips[n], w_sh, m_sh, v_sh, 256, "adamw_" + n)

    part = _pack_rows(dg_mix, dg_mem, dg_mlp, dqg, dkg, dconv, lsum)
    gsum = _small_sum(_small_all_gather(part))
    loss = 0.5 * jnp.sum(gsum[6 * SMALL_TILE]) / D_MODEL
    conv_cols = lax.dynamic_slice(gsum[5 * SMALL_TILE:6 * SMALL_TILE], (0, me * (D_MODEL // N_DEV)),
                                  (SMALL_TILE, D_MODEL // N_DEV))
    g_small = jnp.concatenate([gsum[:5 * SMALL_TILE], _pad_tile(conv_cols)], axis=0)
    w_small = _pack_small(g_mix, g_mem, g_mlp, q_norm_g, k_norm_g, conv_w[0])
    m_small = _pack_small(m_g_mix, m_g_mem, m_g_mlp, m_q_norm_g, m_k_norm_g, m_conv_w[0])
    v_small = _pack_small(v_g_mix, v_g_mem, v_g_mlp, v_q_norm_g, v_k_norm_g, v_conv_w[0])
    d_small, nm_small, nv_small = _adamw_small(w_small, g_small, m_small, v_small)

    def unpack(p):
        return {"g_mix": p[0:1], "g_mem": p[8:9], "g_mlp": p[16:17], "q_norm_g": p[24:25, :X_HEAD_DIM],
                "k_norm_g": p[32:33, :X_HEAD_DIM], "conv_w": p[40:43, :D_MODEL // N_DEV][None]}

    small = [unpack(p) for p in (g_small, d_small, nm_small, nv_small)]
    order = ["g_mix", "g_mem", "w_in", "conv_w", "w_conv_out", "w_sb_out", "q_norm_g", "k_norm_g", "w_mem_kv",
             "w_x_out", "w_out", "g_mlp", "w_up", "w_down"]
    outs = [loss, grad_x[None]]
    for kind in range(4):
        for n in order:
            outs.append(res[n][kind][None] if n in res else small[kind][n])
    return tuple(outs)
```

```python
import jax
import jax.numpy as jnp
from jax import lax
from jax.experimental import pallas as pl
from jax.experimental.pallas import tpu as pltpu

F32 = jnp.float32
BF16 = jnp.bfloat16
MESH = pl.DeviceIdType.MESH

EPS = 1e-6
N_DEV = 8
D_MODEL = 1024
SB_HEAD_DIM = 64
SB_BLOCK = 128
SB_QUERY_TILE = 512
X_HEADS = 4
X_HEAD_DIM = 256
N_BRANCH = 3
COL_CH, COL_CB, COL_CC, COL_SQ, COL_SK, COL_SV, COL_XQ, COL_GATE = 0, 1, 2, 3, 4, 5, 6, 7

ADAM_LR = 0.001
ADAM_B1 = 0.9
ADAM_B2 = 0.999
ADAM_EPS = 1e-08
ADAM_WD = 0.01
ADAM_STEP = 10

SMALL_TILE = 8


def _dot(a, b):
    return jnp.dot(a, b, preferred_element_type=F32)


def _dot_nt(a, b):
    return lax.dot_general(a, b, (((1,), (1,)), ((), ())), preferred_element_type=F32)


def _dot_tn(a, b):
    return lax.dot_general(a, b, (((0,), (0,)), ((), ())), preferred_element_type=F32)


def _rstd(xf):
    return lax.rsqrt(jnp.mean(xf * xf, axis=-1, keepdims=True) + EPS)


def _sigmoid(z):
    return 1.0 / (1.0 + jnp.exp(-z))


def _rms_bwd(dy, xhat, r, g):
    dxhat = dy * g
    return r * (dxhat - xhat * jnp.mean(dxhat * xhat, axis=-1, keepdims=True))


def _mesh_pos():
    return lax.axis_index("x"), lax.axis_index("y"), lax.axis_index("c")


ANY = pl.BlockSpec(memory_space=pl.ANY)


def _gather_shapes(shards):
    return [jax.ShapeDtypeStruct((N_DEV,) + s.shape, s.dtype) for s in shards]


def _gather_sems(n):
    return [pltpu.SemaphoreType.DMA((n, 7)), pltpu.SemaphoreType.DMA((n, 7)), pltpu.SemaphoreType.DMA((n,))]


def _gather_phases(ins, outs, send_sems, recv_sems, local_sems, by_arrival=False):
    n = len(ins)
    x, y, c = _mesh_pos()
    me, sibling = (x, y, c), (x, y, 1 - c)
    chips = [(1 - x, y), (x, 1 - y), (1 - x, 1 - y)]

    def blk(a, px, py, pc):
        return outs[a].at[4 * px + 2 * py + pc]

    def copy(a, k, block, to, src=None):
        return pltpu.make_async_remote_copy(
            src_ref=blk(a, *block) if src is None else src, dst_ref=blk(a, *block),
            send_sem=send_sems.at[a, k], recv_sem=recv_sems.at[a, k], device_id=to, device_id_type=MESH)

    def local(a):
        return pltpu.make_async_copy(ins[a], blk(a, *me), local_sems.at[a])

    def own(a):
        return [copy(a, 0, me, sibling, src=ins[a])] + [copy(a, 1 + j, me, (*chips[j], c), src=ins[a]) for j in range(2)]

    def onward(a, via):
        return copy(a, 3, (*chips[via], c), (*chips[1 - via], c))

    def begin():
        for a in range(n):
            local(a).start()
        for a in range(n):
            for cp in own(a):
                cp.start()

    def arrive(order):
        for a in range(n):
            if order == 0:
                copy(a, 0, sibling, me).wait_recv()
            elif order <= 3:
                chip = chips[order - 1]
                copy(a, order, (*chip, c), me).wait_recv()
                copy(a, 3 + order, (*chip, c), sibling).start()
                if order <= 2:
                    pl.when(c == order - 1)(onward(a, order - 1).start)
            else:
                copy(a, order, (*chips[order - 4], 1 - c), me).wait_recv()

    def relay():
        for order in (1, 2):
            arrive(order)

    def drain():
        for a in range(n):
            for cp in own(a):
                cp.wait_send()
            for via in range(2):
                pl.when(c == via)(onward(a, via).wait_send)
            for j, chip in enumerate(chips):
                copy(a, 4 + j, (*chip, c), sibling).wait_send()
            local(a).wait()

    def finish():
        for order in (3, 0, 4, 5, 6):
            arrive(order)
        drain()

    if by_arrival:
        return begin, arrive, drain
    return begin, relay, finish


def _pair_exchange(grads, name):
    n = len(grads)

    def body(*refs):
        begin, finish = _pair_exchange_phases(refs[:n], refs[n:2 * n], *refs[2 * n:])
        begin()
        finish()

    return pl.pallas_call(
        body, name=name,
        out_shape=_pair_exchange_shapes(grads),
        in_specs=[ANY] * n, out_specs=[ANY] * n,
        scratch_shapes=_pair_exchange_sems(n),
    )(*grads)


def _pair_exchange_shapes(grads):
    return [jax.ShapeDtypeStruct((4,) + g.shape[1:], g.dtype) for g in grads]


def _pair_exchange_sems(n):
    return [pltpu.SemaphoreType.DMA((n, 4)), pltpu.SemaphoreType.DMA((n, 4))]


def _pair_exchange_phases(ins, outs, send_sems, recv_sems, by_slot=False):
    x, y, c = _mesh_pos()
    xs, ys = (x, 1 - x), (y, 1 - y)

    def copies():
        out = []
        for a in range(len(ins)):
            for r in range(4):
                dx, dy = divmod(r, 2)
                out.append(pltpu.make_async_remote_copy(
                    src_ref=ins[a].at[r if by_slot else 4 * xs[dx] + 2 * ys[dy] + (1 - c)], dst_ref=outs[a].at[r],
                    send_sem=send_sems.at[a, r], recv_sem=recv_sems.at[a, r],
                    device_id=(x, y, 1 - c), device_id_type=MESH))
        return out

    def begin():
        for cp in copies():
            cp.start()

    def finish():
        for cp in copies():
            cp.wait()

    return begin, finish


def _chip_exchange_shapes(sums):
    return [jax.ShapeDtypeStruct((3,) + s.shape[1:], s.dtype) for s in sums]


def _chip_exchange_sems(n):
    return [pltpu.SemaphoreType.DMA((n, 3)), pltpu.SemaphoreType.DMA((n, 3))]


def _chip_exchange_phases(ins, outs, send_sems, recv_sems):
    x, y, c = _mesh_pos()
    xs, ys = (x, 1 - x), (y, 1 - y)

    def copies():
        out = []
        for a in range(len(ins)):
            for r in range(1, 4):
                dx, dy = divmod(r, 2)
                out.append(pltpu.make_async_remote_copy(
                    src_ref=ins[a].at[r], dst_ref=outs[a].at[r - 1],
                    send_sem=send_sems.at[a, r - 1], recv_sem=recv_sems.at[a, r - 1],
                    device_id=(xs[dx], ys[dy], c), device_id_type=MESH))
        return out

    def begin():
        for cp in copies():
            cp.start()

    def finish():
        for cp in copies():
            cp.wait()

    return begin, finish


def _small_all_gather(part):
    rows, cols = part.shape

    def body(in_ref, out_ref, send_sems, recv_sems):
        x, y, c = _mesh_pos()
        xs, ys, cs = (x, 1 - x), (y, 1 - y), (c, 1 - c)
        out_ref[4 * x + 2 * y + c] = in_ref[...]
        copies = []
        for k in range(1, N_DEV):
            dx, dy, dc = k // 4, (k // 2) % 2, k % 2
            copies.append((
                pltpu.make_async_remote_copy(
                    src_ref=in_ref, dst_ref=out_ref.at[4 * x + 2 * y + c],
                    send_sem=send_sems.at[k - 1], recv_sem=recv_sems.at[k - 1],
                    device_id=(xs[dx], ys[dy], cs[dc]), device_id_type=MESH),
                pltpu.make_async_remote_copy(
                    src_ref=in_ref, dst_ref=out_ref.at[4 * xs[dx] + 2 * ys[dy] + cs[dc]],
                    send_sem=send_sems.at[k - 1], recv_sem=recv_sems.at[k - 1],
                    device_id=(xs[dx], ys[dy], cs[dc]), device_id_type=MESH)))
        for send, _ in copies:
            send.start()
        for send, recv in copies:
            recv.wait_recv()
            send.wait_send()

    return pl.pallas_call(
        body, name="small_all_gather",
        out_shape=jax.ShapeDtypeStruct((N_DEV, rows, cols), part.dtype),
        in_specs=[pl.BlockSpec(memory_space=pltpu.VMEM)],
        out_specs=pl.BlockSpec(memory_space=pltpu.VMEM),
        scratch_shapes=[pltpu.SemaphoreType.DMA((N_DEV - 1,)), pltpu.SemaphoreType.DMA((N_DEV - 1,))],
    )(part)


ARRIVAL_ORDER = (0, 1, 2, 4, 5, 3, 6)


def _arrival_blocks(xpos, ypos, cpos):
    chips = [(1 - xpos, ypos), (xpos, 1 - ypos), (1 - xpos, 1 - ypos)]
    by_order = ([4 * xpos + 2 * ypos + (1 - cpos)] + [4 * cx + 2 * cy + cpos for cx, cy in chips]
                + [4 * cx + 2 * cy + (1 - cpos) for cx, cy in chips])
    return jnp.stack([4 * xpos + 2 * ypos + cpos] + [by_order[o] for o in ARRIVAL_ORDER]).astype(jnp.int32)


def _in_proj(x, g_mix, arrival_blocks, tm, shards):
    t, d = x.shape
    bw = shards[0].shape[1]
    nt = t // tm
    n = len(shards)

    def body(blocks_ref, x_ref, g_ref, *rest):
        w_shard = rest[0]
        proj_ref, h_ref = rest[n:n + 2]
        w_all = rest[n + 2]
        h_scr, w_buf, fetch_sems = rest[2 * n + 2:2 * n + 5]
        begin, arrive, drain = _gather_phases(rest[:n], rest[n + 2:2 * n + 2], *rest[2 * n + 5:], by_arrival=True)
        j, i = pl.program_id(0), pl.program_id(1)
        slot = lax.rem(j, 2)

        def fetch(src, into):
            return pltpu.make_async_copy(src, w_buf.at[into], fetch_sems.at[into])

        @pl.when((j == 0) & (i == 0))
        def _():
            begin()
            fetch(w_shard, 0).start()

        @pl.when(j == 0)
        def _():
            xf = x_ref[...]
            hv = (xf * _rstd(xf) * g_ref[...]).astype(BF16)
            h_ref[...] = hv
            h_scr[pl.ds(pl.multiple_of(i * tm, tm), tm), :] = hv

        @pl.when(i == 0)
        def _():
            fetch(w_shard, slot).wait()

        proj_ref[...] = _dot(h_scr[pl.ds(pl.multiple_of(i * tm, tm), tm), :], w_buf[slot]).astype(BF16)

        for nxt in range(1, N_DEV):
            @pl.when((i == nt - 1) & (j == nxt - 1))
            def _():
                arrive(ARRIVAL_ORDER[nxt - 1])
                fetch(w_all.at[blocks_ref[nxt]], 1 - slot).start()

        pl.when((i == nt - 1) & (j == N_DEV - 1))(drain)

    first_pass = lambda j, i, blocks: (jnp.where(j == 0, i, nt - 1), 0)
    outs = pl.pallas_call(
        body, name="in_proj",
        grid_spec=pltpu.PrefetchScalarGridSpec(
            num_scalar_prefetch=1, grid=(N_DEV, nt),
            in_specs=[pl.BlockSpec((tm, d), first_pass), pl.BlockSpec((1, d), lambda j, i, blocks: (0, 0))] + [ANY] * n,
            out_specs=[pl.BlockSpec((tm, bw), lambda j, i, blocks: (i, blocks[j])),
                       pl.BlockSpec((tm, d), first_pass)] + [ANY] * n,
            scratch_shapes=[pltpu.VMEM((t, d), BF16), pltpu.VMEM((2, d, bw), BF16), pltpu.SemaphoreType.DMA((2,))]
            + _gather_sems(n)),
        out_shape=[jax.ShapeDtypeStruct((t, N_DEV * bw), BF16), jax.ShapeDtypeStruct((t, d), BF16)] + _gather_shapes(shards),
        compiler_params=pltpu.CompilerParams(dimension_semantics=("arbitrary", "arbitrary")),
    )(arrival_blocks, x, g_mix, *shards)
    return outs[0], outs[1], outs[2:]


def _conv_terms(ch_ref, cb_ref, cc_ref, w_ref):
    ch, cb, cc = ch_ref[...].astype(F32), cb_ref[...].astype(F32), cc_ref[...].astype(F32)
    u = cc * ch
    row = lax.broadcasted_iota(jnp.int32, u.shape, 0)
    u1 = jnp.where(row >= 1, pltpu.roll(u, 1, 0), 0.0)
    u2 = jnp.where(row >= 2, pltpu.roll(u, 2, 0), 0.0)
    w = (w_ref[0:1, :], w_ref[1:2, :], w_ref[2:3, :])
    cv = w[2] * u + w[1] * u1 + w[0] * u2
    return ch, cb, cc, u, u1, u2, cv, w, row


def _conv_fwd(proj, conv_w, cw):
    t = proj.shape[0]
    nper = D_MODEL // cw

    def body(ch_ref, cb_ref, cc_ref, w_ref, a_ref):
        _, cb, _, _, _, _, cv, _, _ = _conv_terms(ch_ref, cb_ref, cc_ref, w_ref)
        a_ref[...] = (cb * cv).astype(BF16)

    def col(piece):
        return pl.BlockSpec((t, cw), lambda j: (0, piece * nper + j))

    return pl.pallas_call(
        body, name="conv_fwd",
        grid=(nper,),
        in_specs=[col(COL_CH), col(COL_CB), col(COL_CC), pl.BlockSpec((3, cw), lambda j: (0, j))],
        out_specs=pl.BlockSpec((t, cw), lambda j: (0, j)),
        out_shape=jax.ShapeDtypeStruct((t, D_MODEL), BF16),
        compiler_params=pltpu.CompilerParams(dimension_semantics=("parallel",)),
    )(proj, proj, proj, conv_w)


def _scan_matrix(inclusive, value):
    s = lax.broadcasted_iota(jnp.int32, (SB_BLOCK, 2 * SB_BLOCK), 0)
    j = lax.rem(lax.broadcasted_iota(jnp.int32, (SB_BLOCK, 2 * SB_BLOCK), 1), SB_BLOCK)
    return jnp.where((j >= s) if inclusive else (j > s), value, 0.0).astype(BF16)


def _suffix_sum(u_mat, xv):
    hi = xv.astype(BF16)
    lo = (xv - hi.astype(F32)).astype(BF16)
    return _dot(u_mat, jnp.concatenate([hi, lo], axis=0))


def _head_rows(vt, h):
    row = lax.broadcasted_iota(jnp.int32, vt.shape, 0)
    return jnp.where((row >= h * SB_HEAD_DIM) & (row < (h + 1) * SB_HEAD_DIM), vt, 0.0).astype(BF16)


def _head_lanes(v, h):
    lane = lax.broadcasted_iota(jnp.int32, v.shape, 1)
    return jnp.where((lane >= h * SB_HEAD_DIM) & (lane < (h + 1) * SB_HEAD_DIM), v, 0.0).astype(BF16)


def _group_suffix(u_mat, xv, carry, negate=False):
    nblk = xv.shape[0] // SB_BLOCK
    parts = [None] * nblk
    for j in reversed(range(nblk)):
        xj = xv[j * SB_BLOCK:(j + 1) * SB_BLOCK]
        parts[j] = _suffix_sum(u_mat, xj) + carry
        total = jnp.sum(xj, axis=0, keepdims=True)
        carry = carry - total if negate else carry + total
    return jnp.concatenate(parts, axis=0), carry


def _sb_probs(kgrp, qt_h, carry, past):
    z = _dot(kgrp, qt_h)
    softplus = jnp.maximum(z, 0.0) + jnp.log(1.0 + jnp.exp(-jnp.abs(z)))
    if past is not None:
        softplus = jnp.where(past, softplus, 0.0)
    later, carry = _group_suffix(_scan_matrix(True, -1.0), softplus, carry, negate=True)
    a = jnp.exp(z + later)
    if past is not None:
        a = jnp.where(past, a, 0.0)
    return a, z, carry


def _sb_schedule(nq):
    steps = [(qi, g) for qi in range(nq) for g in range(qi, -1, -1)]
    return jnp.asarray([s[0] for s in steps], jnp.int32), jnp.asarray([s[1] for s in steps], jnp.int32)


def _past_mask(tq):
    return lax.broadcasted_iota(jnp.int32, (tq, tq), 0) < lax.broadcasted_iota(jnp.int32, (tq, tq), 1)


def _sb_fwd(proj, vt4, tq, shards):
    t = proj.shape[0]
    pairs = D_MODEL // SB_BLOCK
    nq = t // tq
    qi_tab, g_tab = _sb_schedule(nq)
    ns = qi_tab.shape[0]
    n = len(shards)

    def body(qi_ref, g_ref, q_ref, k_ref, vt_ref, *rest):
        o_ref, a_ref, z_ref = rest[n:n + 3]
        acc_ref, carry_ref, qt_ref = rest[2 * n + 3:2 * n + 6]
        begin, relay, finish = _gather_phases(rest[:n], rest[n + 3:2 * n + 3], *rest[2 * n + 6:])
        pi, si = pl.program_id(0), pl.program_id(1)
        diagonal = g_ref[si] == qi_ref[si]
        pl.when((pi == 0) & (si == 0))(begin)
        pl.when((pi == pairs // 2) & (si == 0))(relay)

        @pl.when(diagonal)
        def _():
            acc_ref[...] = jnp.zeros_like(acc_ref)
            carry_ref[...] = jnp.zeros_like(carry_ref)
            qt = q_ref[...].astype(F32).T * (SB_HEAD_DIM ** -0.5)
            for h in range(2):
                qt_ref[h] = _head_rows(qt, h)

        def step(past):
            for h in range(2):
                a, z, carry = _sb_probs(k_ref[...], qt_ref[h], carry_ref[h], past)
                ab = a.astype(BF16)
                a_ref[h] = ab
                z_ref[h] = z.astype(BF16)
                acc_ref[h] += _dot(vt_ref[h * SB_HEAD_DIM:(h + 1) * SB_HEAD_DIM, :], ab)
                carry_ref[h] = carry

        pl.when(diagonal)(lambda: step(_past_mask(tq)))
        pl.when(jnp.logical_not(diagonal))(lambda: step(None))

        @pl.when(g_ref[si] == 0)
        def _():
            o_ref[...] = jnp.concatenate([acc_ref[0], acc_ref[1]], axis=0).T

        pl.when((pi == pairs - 1) & (si == ns - 1))(finish)

    tile = pl.BlockSpec((None, None, 2, tq, tq), lambda p, s, qt_, gt_: (p, s, 0, 0, 0))
    tiles = jax.ShapeDtypeStruct((pairs, ns, 2, tq, tq), BF16)
    outs = pl.pallas_call(
        body, name="sb_fwd",
        grid_spec=pltpu.PrefetchScalarGridSpec(
            num_scalar_prefetch=2, grid=(pairs, ns),
            in_specs=[pl.BlockSpec((tq, SB_BLOCK), lambda p, s, qt_, gt_: (qt_[s], COL_SQ * pairs + p)),
                      pl.BlockSpec((tq, SB_BLOCK), lambda p, s, qt_, gt_: (gt_[s], COL_SK * pairs + p)),
                      pl.BlockSpec((None, None, SB_BLOCK, tq), lambda p, s, qt_, gt_: (p, gt_[s], 0, 0))] + [ANY] * n,
            out_specs=[pl.BlockSpec((tq, SB_BLOCK), lambda p, s, qt_, gt_: (qt_[s], p)), tile, tile] + [ANY] * n,
            scratch_shapes=[pltpu.VMEM((2, SB_HEAD_DIM, tq), F32), pltpu.VMEM((2, 1, tq), F32),
                            pltpu.VMEM((2, SB_BLOCK, tq), BF16)] + _gather_sems(n)),
        out_shape=[jax.ShapeDtypeStruct((t, D_MODEL), F32), tiles, tiles] + _gather_shapes(shards),
        compiler_params=pltpu.CompilerParams(dimension_semantics=("arbitrary", "arbitrary")),
    )(qi_tab, g_tab, proj, proj, vt4, *shards)
    return outs[0], outs[1], outs[2], outs[3:]


def _mem_prep(mem, g_mem, wkv_all, k_norm_g):
    m, d = mem.shape

    def body(mem_ref, g_ref, w_ref, kg_ref, memn_ref, kn_ref, v_ref):
        memf = mem_ref[...]
        memn = (memf * _rstd(memf) * g_ref[...]).astype(BF16)
        memn_ref[...] = memn
        for b in range(N_DEV):
            kv = _dot(memn, w_ref[b])
            if b < X_HEADS:
                kn_ref[:, b * X_HEAD_DIM:(b + 1) * X_HEAD_DIM] = (kv * _rstd(kv) * kg_ref[...]).astype(BF16)
            else:
                h = b - X_HEADS
                v_ref[:, h * X_HEAD_DIM:(h + 1) * X_HEAD_DIM] = kv.astype(BF16)

    return pl.pallas_call(
        body, name="mem_prep",
        out_shape=[jax.ShapeDtypeStruct((m, d), BF16)] * 3,
    )(mem, g_mem, wkv_all, k_norm_g)


def _x_head(xq_ref, qg, kn_ref, h):
    sl = slice(h * X_HEAD_DIM, (h + 1) * X_HEAD_DIM)
    q = xq_ref[:, sl].astype(F32)
    rq = _rstd(q)
    qhat = q * rq
    qn = (qhat * qg).astype(BF16)
    s = _dot_nt(qn, kn_ref[:, sl]) * (X_HEAD_DIM ** -0.5)
    e = jnp.exp(s - jnp.max(s, axis=-1, keepdims=True))
    p = e / jnp.sum(e, axis=-1, keepdims=True)
    return sl, rq, qhat, qn, p


def _x_fwd(proj, q_norm_g, kn, v, tm):
    t = proj.shape[0]
    m = kn.shape[0]

    def body(xq_ref, qg_ref, kn_ref, v_ref, o_ref):
        for h in range(X_HEADS):
            sl, _, _, _, p = _x_head(xq_ref, qg_ref[...], kn_ref, h)
            o_ref[:, sl] = _dot(p.astype(BF16), v_ref[:, sl]).astype(BF16)

    return pl.pallas_call(
        body, name="x_fwd",
        grid=(t // tm,),
        in_specs=[pl.BlockSpec((tm, D_MODEL), lambda i: (i, COL_XQ)),
                  pl.BlockSpec((1, X_HEAD_DIM), lambda i: (0, 0)),
                  pl.BlockSpec((m, D_MODEL), lambda i: (0, 0)),
                  pl.BlockSpec((m, D_MODEL), lambda i: (0, 0))],
        out_specs=pl.BlockSpec((tm, D_MODEL), lambda i: (i, 0)),
        out_shape=jax.ShapeDtypeStruct((t, D_MODEL), BF16),
        compiler_params=pltpu.CompilerParams(dimension_semantics=("parallel",)),
    )(proj, q_norm_g, kn, v)


def _gate_spec(tm, branch):
    return pl.BlockSpec((tm, D_MODEL), lambda i: (i, COL_GATE + branch))


def _merge_fwd(x, proj, a_conv, o_sb, o_x, w_conv_out, w_sb_out, w_x_out, w_out, tm):
    t, d = x.shape

    def body(x_ref, g0_ref, g1_ref, g2_ref, a_ref, s_ref, xo_ref, wc_ref, ws_ref, wx_ref, wo_ref,
             x1_ref, yc_ref, ys_ref, yx_ref, mg_ref):
        merged = jnp.zeros((tm, d), F32)
        for gate_ref, b_ref, w_ref, y_ref in ((g0_ref, a_ref, wc_ref, yc_ref), (g1_ref, s_ref, ws_ref, ys_ref),
                                              (g2_ref, xo_ref, wx_ref, yx_ref)):
            yv = _dot(b_ref[...].astype(BF16), w_ref[...])
            y_ref[...] = yv.astype(BF16)
            merged = merged + _sigmoid(gate_ref[...].astype(F32)) * yv
        mb = merged.astype(BF16)
        mg_ref[...] = mb
        x1_ref[...] = x_ref[...] + _dot(mb, wo_ref[...])

    tile = pl.BlockSpec((tm, d), lambda i: (i, 0))
    wfull = pl.BlockSpec((d, d), lambda i: (0, 0))
    return pl.pallas_call(
        body, name="merge_fwd",
        grid=(t // tm,),
        in_specs=[tile] + [_gate_spec(tm, b) for b in range(N_BRANCH)] + [tile, tile, tile,
                                                                           wfull, wfull, wfull, wfull],
        out_specs=[tile] * 5,
        out_shape=[jax.ShapeDtypeStruct((t, d), F32)] + [jax.ShapeDtypeStruct((t, d), BF16)] * 4,
        compiler_params=pltpu.CompilerParams(dimension_semantics=("parallel",)),
    )(x, proj, proj, proj, a_conv, o_sb, o_x, w_conv_out, w_sb_out, w_x_out, w_out)


def _mlp_fwd(x1, g_mlp, w_up_all, w_down, target, tm):
    t, d = x1.shape
    nb, _, fw = w_up_all.shape

    def body(x1_ref, g_ref, wu_ref, wd_ref, tgt_ref, up_ref, h2_ref, dx2_ref, lsum_ref, acc_ref):
        i, j = pl.program_id(0), pl.program_id(1)

        @pl.when(j == 0)
        def _():
            xf = x1_ref[...]
            h2_ref[...] = (xf * _rstd(xf) * g_ref[...]).astype(BF16)
            acc_ref[...] = jnp.zeros_like(acc_ref)

        @pl.when((i == 0) & (j == 0))
        def _():
            lsum_ref[...] = jnp.zeros_like(lsum_ref)

        up = _dot(h2_ref[...], wu_ref[...])
        up_ref[...] = up.astype(BF16)
        act = jnp.square(jnp.maximum(up, 0.0)).astype(BF16)
        acc_ref[...] += _dot(act, wd_ref[...])

        @pl.when(j == nb - 1)
        def _():
            diff = x1_ref[...] + acc_ref[...] - tgt_ref[...]
            dx2_ref[...] = diff * (1.0 / d)
            lsum_ref[...] += jnp.sum(diff * diff, axis=0, keepdims=True)

    tile = pl.BlockSpec((tm, d), lambda i, j: (i, 0))
    row = pl.BlockSpec((1, d), lambda i, j: (0, 0))
    return pl.pallas_call(
        body, name="mlp_fwd",
        grid=(t // tm, nb),
        in_specs=[tile, row, pl.BlockSpec((None, d, fw), lambda i, j: (j, 0, 0)),
                  pl.BlockSpec((fw, d), lambda i, j: (j, 0)), tile],
        out_specs=[pl.BlockSpec((tm, fw), lambda i, j: (i, j)), tile, tile, row],
        out_shape=[jax.ShapeDtypeStruct((t, nb * fw), BF16), jax.ShapeDtypeStruct((t, d), BF16),
                   jax.ShapeDtypeStruct((t, d), F32), jax.ShapeDtypeStruct((1, d), F32)],
        scratch_shapes=[pltpu.VMEM((tm, d), F32)],
        compiler_params=pltpu.CompilerParams(dimension_semantics=("arbitrary", "arbitrary")),
    )(x1, g_mlp, w_up_all, w_down, target)


def _mlp_bwd(x1, g_mlp, w_up_all, w_down, up, dx2, tm):
    t, d = x1.shape
    nb, _, fw = w_up_all.shape

    def body(x1_ref, g_ref, wu_ref, wd_ref, up_ref, dx2_ref, dup_ref, act_ref, dx1_ref, dg_ref, acc_ref, dyb_ref):
        i, j = pl.program_id(0), pl.program_id(1)

        @pl.when(j == 0)
        def _():
            dyb_ref[...] = dx2_ref[...].astype(BF16)
            acc_ref[...] = jnp.zeros_like(acc_ref)

        @pl.when((i == 0) & (j == 0))
        def _():
            dg_ref[...] = jnp.zeros_like(dg_ref)

        r = jnp.maximum(up_ref[...].astype(F32), 0.0)
        act_ref[...] = jnp.square(r).astype(BF16)
        dup = (_dot_nt(dyb_ref[...], wd_ref[...]) * (2.0 * r)).astype(BF16)
        dup_ref[...] = dup
        acc_ref[...] += _dot_nt(dup, wu_ref[...])

        @pl.when(j == nb - 1)
        def _():
            xf = x1_ref[...]
            rs = _rstd(xf)
            xhat = xf * rs
            dh2 = acc_ref[...]
            dg_ref[...] += jnp.sum(dh2 * xhat, axis=0, keepdims=True)
            dx1_ref[...] = dx2_ref[...] + _rms_bwd(dh2, xhat, rs, g_ref[...])

    tile = pl.BlockSpec((tm, d), lambda i, j: (i, 0))
    row = pl.BlockSpec((1, d), lambda i, j: (0, 0))
    ff = pl.BlockSpec((tm, fw), lambda i, j: (i, j))
    return pl.pallas_call(
        body, name="mlp_bwd",
        grid=(t // tm, nb),
        in_specs=[tile, row, pl.BlockSpec((None, d, fw), lambda i, j: (j, 0, 0)),
                  pl.BlockSpec((fw, d), lambda i, j: (j, 0)), ff, tile],
        out_specs=[ff, ff, tile, row],
        out_shape=[jax.ShapeDtypeStruct((t, nb * fw), BF16), jax.ShapeDtypeStruct((t, nb * fw), BF16),
                   jax.ShapeDtypeStruct((t, d), F32), jax.ShapeDtypeStruct((1, d), F32)],
        scratch_shapes=[pltpu.VMEM((tm, d), F32), pltpu.VMEM((tm, d), BF16)],
        compiler_params=pltpu.CompilerParams(dimension_semantics=("arbitrary", "arbitrary")),
    )(x1, g_mlp, w_up_all, w_down, up, dx2)


def _merge_bwd(dx1, proj, y_conv, y_sb, y_x, w_conv_out, w_sb_out, w_x_out, w_out, tm):
    t, d = dx1.shape

    def body(dx1_ref, g0_ref, g1_ref, g2_ref, yc_ref, ys_ref, yx_ref, wc_ref, ws_ref, wx_ref, wo_ref,
             dgate_ref, dyc_ref, dys_ref, dyx_ref, da_ref, dos_ref, dox_ref):
        dm = _dot_nt(dx1_ref[...].astype(BF16), wo_ref[...])
        for i, (gate_ref, y_ref, w_ref, dy_ref, db_ref) in enumerate(((g0_ref, yc_ref, wc_ref, dyc_ref, da_ref),
                                                                       (g1_ref, ys_ref, ws_ref, dys_ref, dos_ref),
                                                                       (g2_ref, yx_ref, wx_ref, dyx_ref, dox_ref))):
            gt = _sigmoid(gate_ref[...].astype(F32))
            dy = (dm * gt).astype(BF16)
            dy_ref[...] = dy
            dgate_ref[:, i * d:(i + 1) * d] = (dm * y_ref[...].astype(F32) * gt * (1.0 - gt)).astype(BF16)
            db_ref[...] = _dot_nt(dy, w_ref[...]).astype(BF16)

    tile = pl.BlockSpec((tm, d), lambda i: (i, 0))
    wfull = pl.BlockSpec((d, d), lambda i: (0, 0))
    return pl.pallas_call(
        body, name="merge_bwd",
        grid=(t // tm,),
        in_specs=[tile] + [_gate_spec(tm, b) for b in range(N_BRANCH)] + [tile, tile, tile,
                                                                           wfull, wfull, wfull, wfull],
        out_specs=[pl.BlockSpec((tm, N_BRANCH * d), lambda i: (i, 0))] + [tile] * 6,
        out_shape=[jax.ShapeDtypeStruct((t, N_BRANCH * d), BF16)] + [jax.ShapeDtypeStruct((t, d), BF16)] * 6,
        compiler_params=pltpu.CompilerParams(dimension_semantics=("parallel",)),
    )(dx1, proj, proj, proj, y_conv, y_sb, y_x, w_conv_out, w_sb_out, w_x_out, w_out)


def _conv_bwd(proj, conv_w, da, cw):
    t = proj.shape[0]
    nper = D_MODEL // cw

    def body(ch_ref, cb_ref, cc_ref, w_ref, da_ref, dch_ref, dcb_ref, dcc_ref, dw_ref):
        ch, cb, cc, u, u1, u2, cv, w, row = _conv_terms(ch_ref, cb_ref, cc_ref, w_ref)
        dav = da_ref[...].astype(F32)
        dcb_ref[...] = (dav * cv).astype(BF16)
        dcv = dav * cb
        n1 = jnp.where(row < t - 1, pltpu.roll(dcv, t - 1, 0), 0.0)
        n2 = jnp.where(row < t - 2, pltpu.roll(dcv, t - 2, 0), 0.0)
        du = w[2] * dcv + w[1] * n1 + w[0] * n2
        dcc_ref[...] = (du * ch).astype(BF16)
        dch_ref[...] = (du * cc).astype(BF16)
        dw_ref[0:1, :] = jnp.sum(dcv * u2, axis=0, keepdims=True)
        dw_ref[1:2, :] = jnp.sum(dcv * u1, axis=0, keepdims=True)
        dw_ref[2:3, :] = jnp.sum(dcv * u, axis=0, keepdims=True)

    def col(piece):
        return pl.BlockSpec((t, cw), lambda j: (0, piece * nper + j))

    out_col = pl.BlockSpec((t, cw), lambda j: (0, j))
    wspec = pl.BlockSpec((3, cw), lambda j: (0, j))
    return pl.pallas_call(
        body, name="conv_bwd",
        grid=(nper,),
        in_specs=[col(COL_CH), col(COL_CB), col(COL_CC), wspec, out_col],
        out_specs=[out_col, out_col, out_col, wspec],
        out_shape=[jax.ShapeDtypeStruct((t, D_MODEL), BF16)] * 3 + [jax.ShapeDtypeStruct((3, D_MODEL), F32)],
        compiler_params=pltpu.CompilerParams(dimension_semantics=("parallel",)),
    )(proj, proj, proj, conv_w, da)


def _sb_bwd(proj, kt4, do_sb, o_sb, weights, logits, tq, pair_sums):
    t = proj.shape[0]
    nq = t // tq
    pairs = D_MODEL // SB_BLOCK
    scale = SB_HEAD_DIM ** -0.5
    qi_tab, g_tab = _sb_schedule(nq)
    ns = qi_tab.shape[0]
    n = len(pair_sums)

    def body(qi_ref, g_ref, q_ref, v_ref, kt_ref, do_ref, o_ref, a_ref, z_ref, *rest):
        dq_ref, dk_ref, dv_ref = rest[n:n + 3]
        dk_acc, dv_acc, dqt_ref, carry_ref, qm_ref, dom_ref, dot_ref, dsum_ref = rest[2 * n + 3:2 * n + 11]
        begin, finish = _chip_exchange_phases(rest[:n], rest[n + 3:2 * n + 3], *rest[2 * n + 11:])
        pi, si = pl.program_id(0), pl.program_id(1)
        diagonal = g_ref[si] == qi_ref[si]
        pl.when((pi == 0) & (si == 0))(begin)

        @pl.when(si == 0)
        def _():
            dk_acc[...] = jnp.zeros_like(dk_acc)
            dv_acc[...] = jnp.zeros_like(dv_acc)

        @pl.when(diagonal)
        def _():
            dqt_ref[...] = jnp.zeros_like(dqt_ref)
            carry_ref[...] = jnp.zeros_like(carry_ref)
            q = q_ref[...].astype(F32) * scale
            do = do_ref[...].astype(F32)
            dot_ = do.T
            prod = dot_ * o_ref[...].T
            for h in range(2):
                rows = slice(h * SB_HEAD_DIM, (h + 1) * SB_HEAD_DIM)
                qm_ref[h] = _head_lanes(q, h)
                dom_ref[h] = _head_lanes(do, h)
                dot_ref[h] = _head_rows(dot_, h)
                dsum_ref[h] = jnp.sum(prod[rows, :], axis=0, keepdims=True)

        def step(past):
            u_mat = _scan_matrix(False, 1.0)
            ks = pl.multiple_of(g_ref[si] * tq, tq)
            dk_add = jnp.zeros((tq, SB_BLOCK), F32)
            dv_add = jnp.zeros((tq, SB_BLOCK), F32)
            for h in range(2):
                rows = slice(h * SB_HEAD_DIM, (h + 1) * SB_HEAD_DIM)
                ab = a_ref[h]
                gw = _dot(v_ref[...], dot_ref[h]) * ab.astype(F32)
                after, carry = _group_suffix(u_mat, gw, carry_ref[h])
                sig = pl.reciprocal(1.0 + jnp.exp(-z_ref[h].astype(F32)), approx=True)
                dz = gw - sig * (dsum_ref[h] - after)
                if past is not None:
                    dz = jnp.where(past, dz, 0.0)
                dzb = dz.astype(BF16)
                dqt_ref[h] += _dot(kt_ref[rows, :], dzb)
                dk_add = dk_add + _dot(dzb, qm_ref[h])
                dv_add = dv_add + _dot(ab, dom_ref[h])
                carry_ref[h] = carry
            dk_acc[pl.ds(ks, tq), :] += dk_add
            dv_acc[pl.ds(ks, tq), :] += dv_add

        pl.when(diagonal)(lambda: step(_past_mask(tq)))
        pl.when(jnp.logical_not(diagonal))(lambda: step(None))

        @pl.when(g_ref[si] == 0)
        def _():
            dq_ref[...] = (jnp.concatenate([dqt_ref[0], dqt_ref[1]], axis=0).T * scale).astype(BF16)

        @pl.when(si == ns - 1)
        def _():
            dk_ref[...] = dk_acc[...].astype(BF16)
            dv_ref[...] = dv_acc[...].astype(BF16)

        pl.when((pi == pairs - 1) & (si == ns - 1))(finish)

    qblk = lambda base: pl.BlockSpec((tq, SB_BLOCK), lambda p, s, qt_, gt_: (qt_[s], base * pairs + p))
    kgrp = lambda base: pl.BlockSpec((tq, SB_BLOCK), lambda p, s, qt_, gt_: (gt_[s], base * pairs + p))
    seq = pl.BlockSpec((t, SB_BLOCK), lambda p, s, qt_, gt_: (0, p))
    tr = pl.BlockSpec((None, None, SB_BLOCK, tq), lambda p, s, qt_, gt_: (p, gt_[s], 0, 0))
    tile = pl.BlockSpec((None, None, 2, tq, tq), lambda p, s, qt_, gt_: (p, s, 0, 0, 0))
    outs = pl.pallas_call(
        body, name="sb_bwd",
        grid_spec=pltpu.PrefetchScalarGridSpec(
            num_scalar_prefetch=2, grid=(pairs, ns),
            in_specs=[qblk(COL_SQ), kgrp(COL_SV), tr, qblk(0), qblk(0), tile, tile] + [ANY] * n,
            out_specs=[qblk(0), seq, seq] + [ANY] * n,
            scratch_shapes=[pltpu.VMEM((t, SB_BLOCK), F32), pltpu.VMEM((t, SB_BLOCK), F32),
                            pltpu.VMEM((2, SB_HEAD_DIM, tq), F32), pltpu.VMEM((2, 1, tq), F32),
                            pltpu.VMEM((2, tq, SB_BLOCK), BF16), pltpu.VMEM((2, tq, SB_BLOCK), BF16),
                            pltpu.VMEM((2, SB_BLOCK, tq), BF16), pltpu.VMEM((2, 1, tq), F32)] + _chip_exchange_sems(n)),
        out_shape=[jax.ShapeDtypeStruct((t, D_MODEL), BF16)] * 3 + _chip_exchange_shapes(pair_sums),
        compiler_params=pltpu.CompilerParams(dimension_semantics=("arbitrary", "arbitrary")),
    )(qi_tab, g_tab, proj, proj, kt4, do_sb, o_sb, weights, logits, *pair_sums)
    return outs[0], outs[1], outs[2], outs[3:]


def _x_bwd(proj, q_norm_g, kn, v, do_x, tm, grads):
    t = proj.shape[0]
    m = kn.shape[0]
    scale = X_HEAD_DIM ** -0.5
    nt = t // tm
    n = len(grads)

    def body(xq_ref, qg_ref, kn_ref, v_ref, do_ref, *rest):
        dxq_ref, dkn_ref, dv_ref, dqg_ref = rest[n:n + 4]
        begin, finish = _pair_exchange_phases(rest[:n], rest[n + 4:2 * n + 4], *rest[2 * n + 4:])

        @pl.when(pl.program_id(0) == 0)
        def _():
            begin()
            dkn_ref[...] = jnp.zeros_like(dkn_ref)
            dv_ref[...] = jnp.zeros_like(dv_ref)
            dqg_ref[...] = jnp.zeros_like(dqg_ref)

        qg = qg_ref[...]
        for h in range(X_HEADS):
            sl, rq, qhat, qn, p = _x_head(xq_ref, qg, kn_ref, h)
            do_h = do_ref[:, sl]
            dp = _dot_nt(do_h, v_ref[:, sl])
            ds = (p * (dp - jnp.sum(dp * p, axis=-1, keepdims=True)) * scale).astype(BF16)
            dqn = _dot(ds, kn_ref[:, sl])
            dkn_ref[:, sl] += _dot_tn(ds, qn)
            dv_ref[:, sl] += _dot_tn(p.astype(BF16), do_h)
            dqg_ref[...] += jnp.sum(dqn * qhat, axis=0, keepdims=True)
            dxq_ref[:, sl] = _rms_bwd(dqn, qhat, rq, qg).astype(BF16)

        pl.when(pl.program_id(0) == nt - 1)(finish)

    full = pl.BlockSpec((m, D_MODEL), lambda i: (0, 0))
    gain = pl.BlockSpec((1, X_HEAD_DIM), lambda i: (0, 0))
    tile = pl.BlockSpec((tm, D_MODEL), lambda i: (i, 0))
    outs = pl.pallas_call(
        body, name="x_bwd",
        grid=(nt,),
        in_specs=[pl.BlockSpec((tm, D_MODEL), lambda i: (i, COL_XQ)), gain, full, full, tile] + [ANY] * n,
        out_specs=[tile, full, full, gain] + [ANY] * n,
        out_shape=[jax.ShapeDtypeStruct((t, D_MODEL), BF16), jax.ShapeDtypeStruct((m, D_MODEL), F32),
                   jax.ShapeDtypeStruct((m, D_MODEL), F32), jax.ShapeDtypeStruct((1, X_HEAD_DIM), F32)]
        + _pair_exchange_shapes(grads),
        scratch_shapes=_pair_exchange_sems(n),
        compiler_params=pltpu.CompilerParams(dimension_semantics=("arbitrary",)),
    )(proj, q_norm_g, kn, v, do_x, *grads)
    return outs[0], outs[1], outs[2], outs[3], outs[4:]


def _mem_bwd(mem, g_mem, wkv_all, k_norm_g, dkn, dv):
    m, d = mem.shape

    def body(mem_ref, g_ref, w_ref, kg_ref, dkn_ref, dv_ref, dkv_ref, dgm_ref, dkg_ref):
        memf = mem_ref[...]
        mem_hat = memf * _rstd(memf)
        memn = (mem_hat * g_ref[...]).astype(BF16)
        kg = kg_ref[...]
        dmemn = jnp.zeros((m, d), F32)
        dkg = jnp.zeros((1, X_HEAD_DIM), F32)
        for b in range(N_DEV):
            sl = slice(b * X_HEAD_DIM, (b + 1) * X_HEAD_DIM)
            if b < X_HEADS:
                kv = _dot(memn, w_ref[b])
                rk = _rstd(kv)
                khat = kv * rk
                dkn_h = dkn_ref[:, sl]
                dkg = dkg + jnp.sum(dkn_h * khat, axis=0, keepdims=True)
                dblk = _rms_bwd(dkn_h, khat, rk, kg).astype(BF16)
            else:
                hs = slice((b - X_HEADS) * X_HEAD_DIM, (b - X_HEADS + 1) * X_HEAD_DIM)
                dblk = dv_ref[:, hs].astype(BF16)
            dkv_ref[:, sl] = dblk
            dmemn = dmemn + _dot_nt(dblk, w_ref[b])
        dgm_ref[...] = jnp.sum(dmemn * mem_hat, axis=0, keepdims=True)
        dkg_ref[...] = dkg

    return pl.pallas_call(
        body, name="mem_bwd",
        out_shape=[jax.ShapeDtypeStruct((m, 2 * d), BF16), jax.ShapeDtypeStruct((1, d), F32),
                   jax.ShapeDtypeStruct((1, X_HEAD_DIM), F32)],
    )(mem, g_mem, wkv_all, k_norm_g, dkn, dv)


def _in_proj_bwd(x, g_mix, w_in_all, dproj, dx1, tm, pair_sums):
    t, d = x.shape
    nb, _, bw = w_in_all.shape
    nt = t // tm
    n = len(pair_sums)

    def body(x_ref, g_ref, w_ref, dp_ref, dx1_ref, *rest):
        dx_ref, dg_ref = rest[n:n + 2]
        acc_ref = rest[2 * n + 2]
        begin, finish = _chip_exchange_phases(rest[:n], rest[n + 2:2 * n + 2], *rest[2 * n + 3:])
        i, j = pl.program_id(0), pl.program_id(1)
        pl.when((i == 0) & (j == 0))(begin)

        @pl.when(j == 0)
        def _():
            acc_ref[...] = jnp.zeros_like(acc_ref)

        @pl.when((i == 0) & (j == 0))
        def _():
            dg_ref[...] = jnp.zeros_like(dg_ref)

        acc_ref[...] += _dot_nt(dp_ref[...], w_ref[...])

        @pl.when(j == nb - 1)
        def _():
            xf = x_ref[...]
            rs = _rstd(xf)
            xhat = xf * rs
            dh = acc_ref[...]
            dg_ref[...] += jnp.sum(dh * xhat, axis=0, keepdims=True)
            dx_ref[...] = dx1_ref[...] + _rms_bwd(dh, xhat, rs, g_ref[...])

        pl.when((i == nt - 1) & (j == nb - 1))(finish)

    tile = pl.BlockSpec((tm, d), lambda i, j: (i, 0))
    row = pl.BlockSpec((1, d), lambda i, j: (0, 0))
    outs = pl.pallas_call(
        body, name="in_proj_bwd",
        grid=(nt, nb),
        in_specs=[tile, row, pl.BlockSpec((None, d, bw), lambda i, j: (j, 0, 0)),
                  pl.BlockSpec((tm, bw), lambda i, j: (i, j)), tile] + [ANY] * n,
        out_specs=[tile, row] + [ANY] * n,
        out_shape=[jax.ShapeDtypeStruct((t, d), F32), jax.ShapeDtypeStruct((1, d), F32)] + _chip_exchange_shapes(pair_sums),
        scratch_shapes=[pltpu.VMEM((tm, d), F32)] + _chip_exchange_sems(n),
        compiler_params=pltpu.CompilerParams(dimension_semantics=("arbitrary", "arbitrary")),
    )(x, g_mix, w_in_all, dproj, dx1, *pair_sums)
    return outs[0], outs[1], outs[2:]


def _weight_grad(a, b, bw, tmm, name):
    t, m = a.shape
    n = b.shape[1]
    tmm = min(tmm, m)

    def body(a_ref, b_ref, o_ref):
        o_ref[...] = _dot_tn(a_ref[...].astype(BF16), b_ref[...].astype(BF16)).astype(BF16)

    return pl.pallas_call(
        body, name=name,
        grid=(m // tmm, n // bw),
        in_specs=[pl.BlockSpec((t, tmm), lambda i, j: (0, i)), pl.BlockSpec((t, bw), lambda i, j: (0, j))],
        out_specs=pl.BlockSpec((None, tmm, bw), lambda i, j: (j, i, 0)),
        out_shape=jax.ShapeDtypeStruct((n // bw, m, bw), BF16),
        compiler_params=pltpu.CompilerParams(dimension_semantics=("parallel", "parallel")),
    )(a, b)


def _weight_grad_blocks(a, b, bw, blocks, tmm, name, send=()):
    t, m = a.shape
    tmm = min(tmm, m)
    nm = m // tmm
    n = len(send)

    def body(blocks_ref, a_ref, b_ref, *rest):
        o_ref = rest[n]
        i, j = pl.program_id(0), pl.program_id(1)
        if n:
            begin, finish = _pair_exchange_phases(rest[:n], rest[n + 1:2 * n + 1], *rest[2 * n + 1:], by_slot=True)
            pl.when((i == 0) & (j == 0))(begin)
        o_ref[...] = _dot_tn(a_ref[...].astype(BF16), b_ref[...].astype(BF16)).astype(BF16)
        if n:
            pl.when((i == nm - 1) & (j == 3))(finish)

    outs = pl.pallas_call(
        body, name=name,
        grid_spec=pltpu.PrefetchScalarGridSpec(
            num_scalar_prefetch=1, grid=(nm, 4),
            in_specs=[pl.BlockSpec((t, tmm), lambda i, j, blk: (0, i)),
                      pl.BlockSpec((t, bw), lambda i, j, blk: (0, blk[j]))] + [ANY] * n,
            out_specs=[pl.BlockSpec((None, tmm, bw), lambda i, j, blk: (j, i, 0))] + [ANY] * n,
            scratch_shapes=_pair_exchange_sems(n) if n else []),
        out_shape=[jax.ShapeDtypeStruct((4, m, bw), BF16)] + _pair_exchange_shapes(send),
        compiler_params=pltpu.CompilerParams(dimension_semantics=("arbitrary", "arbitrary")),
    )(blocks, a, b, *send)
    return outs[0], outs[1:]


def _pair_sum(grads, recvs, own_blocks, name):
    k = len(grads)
    _, rows, cols = grads[0].shape

    def body(idx_ref, *refs):
        for a in range(k):
            refs[2 * k + a][...] = (refs[a][...].astype(F32) + refs[k + a][...].astype(F32)).astype(BF16)

    slot = pl.BlockSpec((None, rows, cols), lambda r, idx: (r, 0, 0))
    return pl.pallas_call(
        body, name=name,
        grid_spec=pltpu.PrefetchScalarGridSpec(
            num_scalar_prefetch=1, grid=(4,),
            in_specs=[pl.BlockSpec((None, rows, cols), lambda r, idx: (idx[r], 0, 0))] * k + [slot] * k,
            out_specs=[slot] * k),
        out_shape=[jax.ShapeDtypeStruct((4, rows, cols), BF16)] * k,
        compiler_params=pltpu.CompilerParams(dimension_semantics=("parallel",)),
    )(own_blocks, *grads, *recvs)


def _adamw_math(w, g, m, v):
    m = ADAM_B1 * m + (1.0 - ADAM_B1) * g
    v = ADAM_B2 * v + (1.0 - ADAM_B2) * jnp.square(g)
    m_hat = m / (1.0 - ADAM_B1 ** ADAM_STEP)
    v_hat = v / (1.0 - ADAM_B2 ** ADAM_STEP)
    delta = -ADAM_LR * (m_hat / (jnp.sqrt(v_hat) + ADAM_EPS) + ADAM_WD * w)
    return delta, m, v


def _adamw_sharded(shards, tr, name):
    k = len(shards)
    rows, cols = shards[0][2].shape
    tr = min(tr, rows)

    def body(*refs):
        for a in range(k):
            h_ref, r_ref, w_ref, m_ref, v_ref = refs[5 * a:5 * a + 5]
            g_out, d_out, m_out, v_out = refs[5 * k + 4 * a:5 * k + 4 * a + 4]
            g = h_ref[...].astype(F32)
            for r in range(3):
                g = g + r_ref[r].astype(F32)
            g_out[...] = g
            d_out[...], m_out[...], v_out[...] = _adamw_math(w_ref[...], g, m_ref[...], v_ref[...])

    tile = pl.BlockSpec((tr, cols), lambda i: (i, 0))
    outs = pl.pallas_call(
        body, name=name,
        grid=(rows // tr,),
        in_specs=[pl.BlockSpec((None, tr, cols), lambda i: (0, i, 0)),
                  pl.BlockSpec((3, tr, cols), lambda i: (0, i, 0)), tile, tile, tile] * k,
        out_specs=[tile] * (4 * k),
        out_shape=[jax.ShapeDtypeStruct((rows, cols), F32)] * (4 * k),
        compiler_params=pltpu.CompilerParams(dimension_semantics=("parallel",)),
    )(*[op for shard in shards for op in shard])
    return [tuple(outs[4 * a:4 * a + 4]) for a in range(k)]


SMALL_ROWS = 16


def _pack_rows(dg_mix, dg_mem, dg_mlp, dqg, dkg, dconv, lsum):
    def body(a_ref, b_ref, c_ref, q_ref, k_ref, cv_ref, l_ref, o_ref):
        o_ref[...] = jnp.zeros_like(o_ref)
        for r, ref in enumerate((a_ref, b_ref, c_ref)):
            o_ref[r:r + 1, :] = ref[...]
        o_ref[3:4, :X_HEAD_DIM] = q_ref[...]
        o_ref[4:5, :X_HEAD_DIM] = k_ref[...]
        o_ref[5:8, :] = cv_ref[...]
        o_ref[8:9, :] = l_ref[...]

    return pl.pallas_call(body, name="small_pack", out_shape=jax.ShapeDtypeStruct((SMALL_ROWS, D_MODEL), F32))(
        dg_mix, dg_mem, dg_mlp, dqg, dkg, dconv, lsum)


def _small_sum(gathered):
    def body(g_ref, o_ref):
        total = g_ref[0]
        for dev in range(1, N_DEV):
            total = total + g_ref[dev]
        o_ref[...] = jnp.zeros_like(o_ref)
        for piece in range(5):
            o_ref[piece * SMALL_TILE:piece * SMALL_TILE + 1, :] = total[piece:piece + 1]
        o_ref[5 * SMALL_TILE:5 * SMALL_TILE + 3, :] = total[5:8]
        o_ref[6 * SMALL_TILE:6 * SMALL_TILE + 1, :] = total[8:9]

    return pl.pallas_call(body, name="small_grad_sum",
                          out_shape=jax.ShapeDtypeStruct((7 * SMALL_TILE, D_MODEL), F32))(gathered)


def _adamw_small(w, g, m, v):
    def body(w_ref, g_ref, m_ref, v_ref, d_out, m_out, v_out):
        d_out[...], m_out[...], v_out[...] = _adamw_math(w_ref[...], g_ref[...], m_ref[...], v_ref[...])

    return pl.pallas_call(body, name="adamw_small", out_shape=[jax.ShapeDtypeStruct(w.shape, F32)] * 3)(w, g, m, v)


def _pad_tile(a):
    return jnp.pad(a, ((0, SMALL_TILE - a.shape[0]), (0, D_MODEL - a.shape[1])))


def _pack_small(*pieces):
    return jnp.concatenate([_pad_tile(a) for a in pieces], axis=0)


def kernel(x, mem, g_mix, g_mem, w_in, conv_w, w_conv_out, w_sb_out, q_norm_g, k_norm_g, w_mem_kv, w_x_out, w_out, g_mlp, w_up, w_down, loss_target, m_g_mix, m_g_mem, m_w_in, m_conv_w, m_w_conv_out, m_w_sb_out, m_q_norm_g, m_k_norm_g, m_w_mem_kv, m_w_x_out, m_w_out, m_g_mlp, m_w_up, m_w_down, v_g_mix, v_g_mem, v_w_in, v_conv_w, v_w_conv_out, v_w_sb_out, v_q_norm_g, v_k_norm_g, v_w_mem_kv, v_w_x_out, v_w_out, v_g_mlp, v_w_up, v_w_down):
    xpos, ypos, cpos = _mesh_pos()
    me = 4 * xpos + 2 * ypos + cpos
    x2d, mem2d, tgt2d = x[0], mem[0], loss_target[0]
    t = x2d.shape[0]
    tm = min(512, t)
    tm_s = min(256, t)

    big = {
        "w_in": (w_in[0], m_w_in[0], v_w_in[0]),
        "w_conv_out": (w_conv_out[0], m_w_conv_out[0], v_w_conv_out[0]),
        "w_sb_out": (w_sb_out[0], m_w_sb_out[0], v_w_sb_out[0]),
        "w_mem_kv": (w_mem_kv[0], m_w_mem_kv[0], v_w_mem_kv[0]),
        "w_x_out": (w_x_out[0], m_w_x_out[0], v_w_x_out[0]),
        "w_out": (w_out[0], m_w_out[0], v_w_out[0]),
        "w_up": (w_up[0], m_w_up[0], v_w_up[0]),
        "w_down": (w_down[0], m_w_down[0], v_w_down[0]),
    }
    late = [n for n in big if n != "w_in"]
    as_bf16 = lambda group: [big[n][0].astype(BF16) for n in group]
    conv_pad = jnp.pad(conv_w[0], ((0, 8 - 3), (0, 0)))

    proj, h, (w_in_all, conv_all) = _in_proj(x2d, g_mix, _arrival_blocks(xpos, ypos, cpos), tm,
                                             [big["w_in"][0].astype(BF16), conv_pad])
    conv_full = conv_all[:, :3, :].transpose(1, 0, 2).reshape(3, D_MODEL)
    a_conv = _conv_fwd(proj, conv_full, 256)
    tq = min(SB_QUERY_TILE, t)
    pairs = D_MODEL // SB_BLOCK

    def groups_t(cols):
        return cols.reshape(t // tq, tq, pairs, SB_BLOCK).transpose(2, 0, 3, 1)

    kt4 = groups_t(proj[:, COL_SK * D_MODEL:(COL_SK + 1) * D_MODEL])
    vt4 = groups_t(proj[:, COL_SV * D_MODEL:(COL_SV + 1) * D_MODEL])
    o_sb, sb_weights, sb_logits, gathered = _sb_fwd(proj, vt4, tq, as_bf16(late))
    full = dict(zip(late, gathered))
    wkv_all, w_up_all = full["w_mem_kv"], full["w_up"]
    rows_full = lambda a: a.reshape(a.shape[0] * a.shape[1], a.shape[2])
    wc, ws, wx, wo, wd = (rows_full(full[n]) for n in ("w_conv_out", "w_sb_out", "w_x_out", "w_out", "w_down"))
    mem_n, kn, vmem = _mem_prep(mem2d, g_mem, wkv_all, k_norm_g)
    o_x = _x_fwd(proj, q_norm_g, kn, vmem, tm_s)
    x1, y_conv, y_sb, y_x, merged = _merge_fwd(x2d, proj, a_conv, o_sb, o_x, wc, ws, wx, wo, tm_s)
    up, h2, dx2, lsum = _mlp_fwd(x1, g_mlp, w_up_all, wd, tgt2d, tm)

    dup, act, dx1, dg_mlp = _mlp_bwd(x1, g_mlp, w_up_all, wd, up, dx2, tm)
    dgate, dy_conv, dy_sb, dy_x, da_conv, do_sb, do_x = _merge_bwd(dx1, proj, y_conv, y_sb, y_x, wc, ws, wx, wo, tm_s)
    dch, dcb, dcc, dconv = _conv_bwd(proj, conv_full, da_conv, 256)
    wgrads = {
        "w_conv_out": _weight_grad(a_conv, dy_conv, D_MODEL, 512, "dw_conv_out"),
        "w_sb_out": _weight_grad(o_sb, dy_sb, D_MODEL, 512, "dw_sb_out"),
        "w_x_out": _weight_grad(o_x, dy_x, D_MODEL, 512, "dw_x_out"),
        "w_out": _weight_grad(merged, dx1, D_MODEL, 512, "dw_out"),
        "w_up": _weight_grad(h2, dup, w_up_all.shape[2], 512, "dw_up"),
        "w_down": _weight_grad(act, dx2, D_MODEL, 512, "dw_down"),
    }

    own_blocks = jnp.stack([4 * (xpos ^ dx) + 2 * (ypos ^ dy) + cpos for dx in (0, 1) for dy in (0, 1)]).astype(jnp.int32)
    sibling_blocks = own_blocks + (1 - 2 * cpos)
    blocked = lambda n: wgrads[n].reshape((N_DEV,) + big[n][0].shape)
    same_shape = ["w_conv_out", "w_sb_out", "w_x_out", "w_out"]

    def pair_sum(group, from_sibling):
        return dict(zip(group, _pair_sum([blocked(n) for n in group], from_sibling, own_blocks, "pair_sum_" + group[0])))

    behind_x = same_shape + ["w_up", "w_down"]
    dxq, dkn, dvm, dqg, from_sibling = _x_bwd(proj, q_norm_g, kn, vmem, do_x, tm_s, [blocked(n) for n in behind_x])
    pair_sums = pair_sum(same_shape, from_sibling[:4])
    pair_sums.update(pair_sum(["w_up"], from_sibling[4:5]))
    pair_sums.update(pair_sum(["w_down"], from_sibling[5:6]))
    dkv, dg_mem, dkg = _mem_bwd(mem2d, g_mem, wkv_all, k_norm_g, dkn, dvm)
    wgrads["w_mem_kv"] = _weight_grad(mem_n, dkv, wkv_all.shape[2], 512, "dw_mem_kv")
    pair_sums.update(pair_sum(["w_mem_kv"], _pair_exchange([blocked("w_mem_kv")], "grad_pair_exchange_w_mem_kv")))
    dq, dk, dv, from_chips_late = _sb_bwd(proj, kt4, do_sb, o_sb, sb_weights, sb_logits, tq,
                                          [pair_sums[n] for n in late])
    from_chips = dict(zip(late, from_chips_late))
    dproj = jnp.concatenate([dch, dcb, dcc, dq, dk, dv, dxq, dgate], axis=1)
    bw_in = w_in_all.shape[2]
    dw_in_sibling, _ = _weight_grad_blocks(h, dproj, bw_in, sibling_blocks, 512, "dw_in_sibling")
    dw_in_own, (dw_in_recv,) = _weight_grad_blocks(h, dproj, bw_in, own_blocks, 512, "dw_in_own", send=[dw_in_sibling])
    pair_sums["w_in"], = _pair_sum([dw_in_own], [dw_in_recv], jnp.arange(4, dtype=jnp.int32), "pair_sum_w_in")
    grad_x, dg_mix, (from_chips["w_in"],) = _in_proj_bwd(x2d, g_mix, w_in_all, dproj, dx1, tm, [pair_sums["w_in"]])
    res = {}
    for group in [same_shape] + [[n] for n in big if n not in same_shape]:
        updates = _adamw_sharded([(pair_sums[n], from_chips[n]) + big[n] for n in group], 256, "adamw_" + group[0])
        res.update(zip(group, updates))

    part = _pack_rows(dg_mix, dg_mem, dg_mlp, dqg, dkg, dconv, lsum)
    gsum = _small_sum(_small_all_gather(part))
    loss = 0.5 * jnp.sum(gsum[6 * SMALL_TILE]) / D_MODEL
    conv_cols = lax.dynamic_slice(gsum[5 * SMALL_TILE:6 * SMALL_TILE], (0, me * (D_MODEL // N_DEV)),
                                  (SMALL_TILE, D_MODEL // N_DEV))
    g_small = jnp.concatenate([gsum[:5 * SMALL_TILE], _pad_tile(conv_cols)], axis=0)
    w_small = _pack_small(g_mix, g_mem, g_mlp, q_norm_g, k_norm_g, conv_w[0])
    m_small = _pack_small(m_g_mix, m_g_mem, m_g_mlp, m_q_norm_g, m_k_norm_g, m_conv_w[0])
    v_small = _pack_small(v_g_mix, v_g_mem, v_g_mlp, v_q_norm_g, v_k_norm_g, v_conv_w[0])
    d_small, nm_small, nv_small = _adamw_small(w_small, g_small, m_small, v_small)

    def unpack(p):
        return {"g_mix": p[0:1], "g_mem": p[8:9], "g_mlp": p[16:17], "q_norm_g": p[24:25, :X_HEAD_DIM],
                "k_norm_g": p[32:33, :X_HEAD_DIM], "conv_w": p[40:43, :D_MODEL // N_DEV][None]}

    small = [unpack(p) for p in (g_small, d_small, nm_small, nv_small)]
    order = ["g_mix", "g_mem", "w_in", "conv_w", "w_conv_out", "w_sb_out", "q_norm_g", "k_norm_g", "w_mem_kv",
             "w_x_out", "w_out", "g_mlp", "w_up", "w_down"]
    outs = [loss, grad_x[None]]
    for kind in range(4):
        for n in order:
            outs.append(res[n][kind][None] if n in res else small[kind][n])
    return tuple(outs)
```

```python
import jax
import jax.numpy as jnp
from jax import lax
from jax.experimental import pallas as pl
from jax.experimental.pallas import tpu as pltpu

F32 = jnp.float32
BF16 = jnp.bfloat16
MESH = pl.DeviceIdType.MESH

EPS = 1e-6
N_DEV = 8
D_MODEL = 1024
SB_HEAD_DIM = 64
SB_BLOCK = 128
SB_QUERY_TILE = 512
X_HEADS = 4
X_HEAD_DIM = 256
N_BRANCH = 3
COL_CH, COL_CB, COL_CC, COL_SQ, COL_SK, COL_SV, COL_XQ, COL_GATE = 0, 1, 2, 3, 4, 5, 6, 7

ADAM_LR = 0.001
ADAM_B1 = 0.9
ADAM_B2 = 0.999
ADAM_EPS = 1e-08
ADAM_WD = 0.01
ADAM_STEP = 10

SMALL_TILE = 8


def _dot(a, b):
    return jnp.dot(a, b, preferred_element_type=F32)


def _dot_nt(a, b):
    return lax.dot_general(a, b, (((1,), (1,)), ((), ())), preferred_element_type=F32)


def _dot_tn(a, b):
    return lax.dot_general(a, b, (((0,), (0,)), ((), ())), preferred_element_type=F32)


def _rstd(xf):
    return lax.rsqrt(jnp.mean(xf * xf, axis=-1, keepdims=True) + EPS)


def _sigmoid(z):
    return 1.0 / (1.0 + jnp.exp(-z))


def _rms_bwd(dy, xhat, r, g):
    dxhat = dy * g
    return r * (dxhat - xhat * jnp.mean(dxhat * xhat, axis=-1, keepdims=True))


def _mesh_pos():
    return lax.axis_index("x"), lax.axis_index("y"), lax.axis_index("c")


ANY = pl.BlockSpec(memory_space=pl.ANY)


def _gather_shapes(shards):
    return [jax.ShapeDtypeStruct((N_DEV,) + s.shape, s.dtype) for s in shards]


def _gather_sems(n):
    return [pltpu.SemaphoreType.DMA((n, 7)), pltpu.SemaphoreType.DMA((n, 7)), pltpu.SemaphoreType.DMA((n,))]


def _gather_chips(x, y, c):
    return [(x ^ (1 - c), y ^ c), (x ^ c, y ^ (1 - c)), (1 - x, 1 - y)]


def _relayed_chip(chips, order):
    return chips[(1, 0, 2)[order - 4]]


def _gather_phases(ins, outs, send_sems, recv_sems, local_sems, by_arrival=False):
    n = len(ins)
    x, y, c = _mesh_pos()
    me, sibling = (x, y, c), (x, y, 1 - c)
    chips = _gather_chips(x, y, c)

    def blk(a, px, py, pc):
        return outs[a].at[4 * px + 2 * py + pc]

    def copy(a, k, block, to, src=None):
        return pltpu.make_async_remote_copy(
            src_ref=blk(a, *block) if src is None else src, dst_ref=blk(a, *block),
            send_sem=send_sems.at[a, k], recv_sem=recv_sems.at[a, k], device_id=to, device_id_type=MESH)

    def local(a):
        return pltpu.make_async_copy(ins[a], blk(a, *me), local_sems.at[a])

    def own(a):
        return [copy(a, 0, me, sibling, src=ins[a])] + [copy(a, 1 + j, me, (*chips[j], c), src=ins[a]) for j in range(2)]

    def onward(a):
        return copy(a, 3, (*chips[0], c), (*chips[1], c))

    def begin():
        for a in range(n):
            local(a).start()
        for a in range(n):
            for cp in own(a)[:2 if by_arrival else 3]:
                cp.start()

    def arrive(order):
        for a in range(n):
            if order == 0:
                copy(a, 0, sibling, me).wait_recv()
            elif order <= 3:
                chip = chips[order - 1]
                copy(a, order, (*chip, c), me).wait_recv()
                copy(a, 3 + order, (*chip, c), sibling).start()
                if order == 1:
                    if by_arrival:
                        own(a)[2].start()
                    onward(a).start()
            else:
                copy(a, order, (*_relayed_chip(chips, order), 1 - c), me).wait_recv()

    def relay():
        for order in (1, 2):
            arrive(order)

    def drain():
        for a in range(n):
            for cp in own(a):
                cp.wait_send()
            onward(a).wait_send()
            for j, chip in enumerate(chips):
                copy(a, 4 + j, (*chip, c), sibling).wait_send()
            local(a).wait()

    def finish():
        for order in (3, 0, 4, 5, 6):
            arrive(order)
        drain()

    if by_arrival:
        return begin, arrive, drain
    return begin, relay, finish


def _pair_exchange(grads, name):
    n = len(grads)

    def body(*refs):
        begin, finish = _pair_exchange_phases(refs[:n], refs[n:2 * n], *refs[2 * n:])
        begin()
        finish()

    return pl.pallas_call(
        body, name=name,
        out_shape=_pair_exchange_shapes(grads),
        in_specs=[ANY] * n, out_specs=[ANY] * n,
        scratch_shapes=_pair_exchange_sems(n),
    )(*grads)


def _pair_exchange_shapes(grads):
    return [jax.ShapeDtypeStruct((4,) + g.shape[1:], g.dtype) for g in grads]


def _pair_exchange_sems(n):
    return [pltpu.SemaphoreType.DMA((n, 4)), pltpu.SemaphoreType.DMA((n, 4))]


def _pair_exchange_phases(ins, outs, send_sems, recv_sems, by_slot=False):
    x, y, c = _mesh_pos()
    xs, ys = (x, 1 - x), (y, 1 - y)

    def copies():
        out = []
        for a in range(len(ins)):
            for r in range(4):
                dx, dy = divmod(r, 2)
                out.append(pltpu.make_async_remote_copy(
                    src_ref=ins[a].at[r if by_slot else 4 * xs[dx] + 2 * ys[dy] + (1 - c)], dst_ref=outs[a].at[r],
                    send_sem=send_sems.at[a, r], recv_sem=recv_sems.at[a, r],
                    device_id=(x, y, 1 - c), device_id_type=MESH))
        return out

    def begin():
        for cp in copies():
            cp.start()

    def finish():
        for cp in copies():
            cp.wait()

    return begin, finish


def _chip_exchange_shapes(sums):
    return [jax.ShapeDtypeStruct((3,) + s.shape[1:], s.dtype) for s in sums]


def _chip_exchange_sems(n):
    return [pltpu.SemaphoreType.DMA((n, 3)), pltpu.SemaphoreType.DMA((n, 3))]


def _chip_exchange_phases(ins, outs, send_sems, recv_sems):
    x, y, c = _mesh_pos()
    xs, ys = (x, 1 - x), (y, 1 - y)

    def copies():
        out = []
        for a in range(len(ins)):
            for r in range(1, 4):
                dx, dy = divmod(r, 2)
                out.append(pltpu.make_async_remote_copy(
                    src_ref=ins[a].at[r], dst_ref=outs[a].at[r - 1],
                    send_sem=send_sems.at[a, r - 1], recv_sem=recv_sems.at[a, r - 1],
                    device_id=(xs[dx], ys[dy], c), device_id_type=MESH))
        return out

    def begin():
        for cp in copies():
            cp.start()

    def finish():
        for cp in copies():
            cp.wait()

    return begin, finish


def _small_all_gather(part):
    rows, cols = part.shape

    def body(in_ref, out_ref, send_sems, recv_sems):
        x, y, c = _mesh_pos()
        xs, ys, cs = (x, 1 - x), (y, 1 - y), (c, 1 - c)
        out_ref[4 * x + 2 * y + c] = in_ref[...]
        copies = []
        for k in range(1, N_DEV):
            dx, dy, dc = k // 4, (k // 2) % 2, k % 2
            copies.append((
                pltpu.make_async_remote_copy(
                    src_ref=in_ref, dst_ref=out_ref.at[4 * x + 2 * y + c],
                    send_sem=send_sems.at[k - 1], recv_sem=recv_sems.at[k - 1],
                    device_id=(xs[dx], ys[dy], cs[dc]), device_id_type=MESH),
                pltpu.make_async_remote_copy(
                    src_ref=in_ref, dst_ref=out_ref.at[4 * xs[dx] + 2 * ys[dy] + cs[dc]],
                    send_sem=send_sems.at[k - 1], recv_sem=recv_sems.at[k - 1],
                    device_id=(xs[dx], ys[dy], cs[dc]), device_id_type=MESH)))
        for send, _ in copies:
            send.start()
        for send, recv in copies:
            recv.wait_recv()
            send.wait_send()

    return pl.pallas_call(
        body, name="small_all_gather",
        out_shape=jax.ShapeDtypeStruct((N_DEV, rows, cols), part.dtype),
        in_specs=[pl.BlockSpec(memory_space=pltpu.VMEM)],
        out_specs=pl.BlockSpec(memory_space=pltpu.VMEM),
        scratch_shapes=[pltpu.SemaphoreType.DMA((N_DEV - 1,)), pltpu.SemaphoreType.DMA((N_DEV - 1,))],
    )(part)


ARRIVAL_ORDER = (0, 1, 4, 2, 5, 3, 6)


def _arrival_blocks(xpos, ypos, cpos):
    chips = _gather_chips(xpos, ypos, cpos)
    by_order = ([4 * xpos + 2 * ypos + (1 - cpos)] + [4 * cx + 2 * cy + cpos for cx, cy in chips]
                + [4 * cx + 2 * cy + (1 - cpos) for cx, cy in (_relayed_chip(chips, o) for o in (4, 5, 6))])
    return jnp.stack([4 * xpos + 2 * ypos + cpos] + [by_order[o] for o in ARRIVAL_ORDER]).astype(jnp.int32)


def _in_proj(x, g_mix, arrival_blocks, tm, shards):
    t, d = x.shape
    bw = shards[0].shape[1]
    nt = t // tm
    n = len(shards)

    def body(blocks_ref, x_ref, g_ref, *rest):
        w_shard = rest[0]
        proj_ref, h_ref = rest[n:n + 2]
        w_all = rest[n + 2]
        h_scr, w_buf, fetch_sems = rest[2 * n + 2:2 * n + 5]
        begin, arrive, drain = _gather_phases(rest[:n], rest[n + 2:2 * n + 2], *rest[2 * n + 5:], by_arrival=True)
        j, i = pl.program_id(0), pl.program_id(1)
        slot = lax.rem(j, 2)

        def fetch(src, into):
            return pltpu.make_async_copy(src, w_buf.at[into], fetch_sems.at[into])

        @pl.when((j == 0) & (i == 0))
        def _():
            begin()
            fetch(w_shard, 0).start()

        @pl.when(j == 0)
        def _():
            xf = x_ref[...]
            hv = (xf * _rstd(xf) * g_ref[...]).astype(BF16)
            h_ref[...] = hv
            h_scr[pl.ds(pl.multiple_of(i * tm, tm), tm), :] = hv

        @pl.when(i == 0)
        def _():
            fetch(w_shard, slot).wait()

        proj_ref[...] = _dot(h_scr[pl.ds(pl.multiple_of(i * tm, tm), tm), :], w_buf[slot]).astype(BF16)

        for nxt in range(1, N_DEV):
            @pl.when((i == nt - 1) & (j == nxt - 1))
            def _():
                arrive(ARRIVAL_ORDER[nxt - 1])
                fetch(w_all.at[blocks_ref[nxt]], 1 - slot).start()

        pl.when((i == nt - 1) & (j == N_DEV - 1))(drain)

    first_pass = lambda j, i, blocks: (jnp.where(j == 0, i, nt - 1), 0)
    outs = pl.pallas_call(
        body, name="in_proj",
        grid_spec=pltpu.PrefetchScalarGridSpec(
            num_scalar_prefetch=1, grid=(N_DEV, nt),
            in_specs=[pl.BlockSpec((tm, d), first_pass), pl.BlockSpec((1, d), lambda j, i, blocks: (0, 0))] + [ANY] * n,
            out_specs=[pl.BlockSpec((tm, bw), lambda j, i, blocks: (i, blocks[j])),
                       pl.BlockSpec((tm, d), first_pass)] + [ANY] * n,
            scratch_shapes=[pltpu.VMEM((t, d), BF16), pltpu.VMEM((2, d, bw), BF16), pltpu.SemaphoreType.DMA((2,))]
            + _gather_sems(n)),
        out_shape=[jax.ShapeDtypeStruct((t, N_DEV * bw), BF16), jax.ShapeDtypeStruct((t, d), BF16)] + _gather_shapes(shards),
        compiler_params=pltpu.CompilerParams(dimension_semantics=("arbitrary", "arbitrary")),
    )(arrival_blocks, x, g_mix, *shards)
    return outs[0], outs[1], outs[2:]


def _conv_terms(ch_ref, cb_ref, cc_ref, w_ref):
    ch, cb, cc = ch_ref[...].astype(F32), cb_ref[...].astype(F32), cc_ref[...].astype(F32)
    u = cc * ch
    row = lax.broadcasted_iota(jnp.int32, u.shape, 0)
    u1 = jnp.where(row >= 1, pltpu.roll(u, 1, 0), 0.0)
    u2 = jnp.where(row >= 2, pltpu.roll(u, 2, 0), 0.0)
    w = (w_ref[0:1, :], w_ref[1:2, :], w_ref[2:3, :])
    cv = w[2] * u + w[1] * u1 + w[0] * u2
    return ch, cb, cc, u, u1, u2, cv, w, row


def _conv_fwd(proj, conv_w, cw):
    t = proj.shape[0]
    nper = D_MODEL // cw

    def body(ch_ref, cb_ref, cc_ref, w_ref, a_ref):
        _, cb, _, _, _, _, cv, _, _ = _conv_terms(ch_ref, cb_ref, cc_ref, w_ref)
        a_ref[...] = (cb * cv).astype(BF16)

    def col(piece):
        return pl.BlockSpec((t, cw), lambda j: (0, piece * nper + j))

    return pl.pallas_call(
        body, name="conv_fwd",
        grid=(nper,),
        in_specs=[col(COL_CH), col(COL_CB), col(COL_CC), pl.BlockSpec((3, cw), lambda j: (0, j))],
        out_specs=pl.BlockSpec((t, cw), lambda j: (0, j)),
        out_shape=jax.ShapeDtypeStruct((t, D_MODEL), BF16),
        compiler_params=pltpu.CompilerParams(dimension_semantics=("parallel",)),
    )(proj, proj, proj, conv_w)


def _scan_matrix(inclusive, value):
    s = lax.broadcasted_iota(jnp.int32, (SB_BLOCK, 2 * SB_BLOCK), 0)
    j = lax.rem(lax.broadcasted_iota(jnp.int32, (SB_BLOCK, 2 * SB_BLOCK), 1), SB_BLOCK)
    return jnp.where((j >= s) if inclusive else (j > s), value, 0.0).astype(BF16)


def _suffix_sum(u_mat, xv):
    hi = xv.astype(BF16)
    lo = (xv - hi.astype(F32)).astype(BF16)
    return _dot(u_mat, jnp.concatenate([hi, lo], axis=0))


def _head_rows(vt, h):
    row = lax.broadcasted_iota(jnp.int32, vt.shape, 0)
    return jnp.where((row >= h * SB_HEAD_DIM) & (row < (h + 1) * SB_HEAD_DIM), vt, 0.0).astype(BF16)


def _head_lanes(v, h):
    lane = lax.broadcasted_iota(jnp.int32, v.shape, 1)
    return jnp.where((lane >= h * SB_HEAD_DIM) & (lane < (h + 1) * SB_HEAD_DIM), v, 0.0).astype(BF16)


def _group_suffix(u_mat, xv, carry, negate=False):
    nblk = xv.shape[0] // SB_BLOCK
    parts = [None] * nblk
    for j in reversed(range(nblk)):
        xj = xv[j * SB_BLOCK:(j + 1) * SB_BLOCK]
        parts[j] = _suffix_sum(u_mat, xj) + carry
        total = jnp.sum(xj, axis=0, keepdims=True)
        carry = carry - total if negate else carry + total
    return jnp.concatenate(parts, axis=0), carry


def _sb_probs(kgrp, qt_h, carry, past):
    z = _dot(kgrp, qt_h)
    softplus = jnp.maximum(z, 0.0) + jnp.log(1.0 + jnp.exp(-jnp.abs(z)))
    if past is not None:
        softplus = jnp.where(past, softplus, 0.0)
    later, carry = _group_suffix(_scan_matrix(True, -1.0), softplus, carry, negate=True)
    a = jnp.exp(z + later)
    if past is not None:
        a = jnp.where(past, a, 0.0)
    return a, z, carry


def _sb_schedule(nq):
    steps = [(qi, g) for qi in range(nq) for g in range(qi, -1, -1)]
    return jnp.asarray([s[0] for s in steps], jnp.int32), jnp.asarray([s[1] for s in steps], jnp.int32)


def _past_mask(tq):
    return lax.broadcasted_iota(jnp.int32, (tq, tq), 0) < lax.broadcasted_iota(jnp.int32, (tq, tq), 1)


def _sb_fwd(proj, vt4, tq, shards):
    t = proj.shape[0]
    pairs = D_MODEL // SB_BLOCK
    nq = t // tq
    qi_tab, g_tab = _sb_schedule(nq)
    ns = qi_tab.shape[0]
    n = len(shards)

    def body(qi_ref, g_ref, q_ref, k_ref, vt_ref, *rest):
        o_ref, a_ref, z_ref = rest[n:n + 3]
        acc_ref, carry_ref, qt_ref = rest[2 * n + 3:2 * n + 6]
        begin, relay, finish = _gather_phases(rest[:n], rest[n + 3:2 * n + 3], *rest[2 * n + 6:])
        pi, si = pl.program_id(0), pl.program_id(1)
        diagonal = g_ref[si] == qi_ref[si]
        pl.when((pi == 0) & (si == 0))(begin)
        pl.when((pi == pairs // 2) & (si == 0))(relay)

        @pl.when(diagonal)
        def _():
            acc_ref[...] = jnp.zeros_like(acc_ref)
            carry_ref[...] = jnp.zeros_like(carry_ref)
            qt = q_ref[...].astype(F32).T * (SB_HEAD_DIM ** -0.5)
            for h in range(2):
                qt_ref[h] = _head_rows(qt, h)

        def step(past):
            for h in range(2):
                a, z, carry = _sb_probs(k_ref[...], qt_ref[h], carry_ref[h], past)
                ab = a.astype(BF16)
                a_ref[h] = ab
                z_ref[h] = z.astype(BF16)
                acc_ref[h] += _dot(vt_ref[h * SB_HEAD_DIM:(h + 1) * SB_HEAD_DIM, :], ab)
                carry_ref[h] = carry

        pl.when(diagonal)(lambda: step(_past_mask(tq)))
        pl.when(jnp.logical_not(diagonal))(lambda: step(None))

        @pl.when(g_ref[si] == 0)
        def _():
            o_ref[...] = jnp.concatenate([acc_ref[0], acc_ref[1]], axis=0).T

        pl.when((pi == pairs - 1) & (si == ns - 1))(finish)

    tile = pl.BlockSpec((None, None, 2, tq, tq), lambda p, s, qt_, gt_: (p, s, 0, 0, 0))
    tiles = jax.ShapeDtypeStruct((pairs, ns, 2, tq, tq), BF16)
    outs = pl.pallas_call(
        body, name="sb_fwd",
        grid_spec=pltpu.PrefetchScalarGridSpec(
            num_scalar_prefetch=2, grid=(pairs, ns),
            in_specs=[pl.BlockSpec((tq, SB_BLOCK), lambda p, s, qt_, gt_: (qt_[s], COL_SQ * pairs + p)),
                      pl.BlockSpec((tq, SB_BLOCK), lambda p, s, qt_, gt_: (gt_[s], COL_SK * pairs + p)),
                      pl.BlockSpec((None, None, SB_BLOCK, tq), lambda p, s, qt_, gt_: (p, gt_[s], 0, 0))] + [ANY] * n,
            out_specs=[pl.BlockSpec((tq, SB_BLOCK), lambda p, s, qt_, gt_: (qt_[s], p)), tile, tile] + [ANY] * n,
            scratch_shapes=[pltpu.VMEM((2, SB_HEAD_DIM, tq), F32), pltpu.VMEM((2, 1, tq), F32),
                            pltpu.VMEM((2, SB_BLOCK, tq), BF16)] + _gather_sems(n)),
        out_shape=[jax.ShapeDtypeStruct((t, D_MODEL), F32), tiles, tiles] + _gather_shapes(shards),
        compiler_params=pltpu.CompilerParams(dimension_semantics=("arbitrary", "arbitrary")),
    )(qi_tab, g_tab, proj, proj, vt4, *shards)
    return outs[0], outs[1], outs[2], outs[3:]


def _mem_prep(mem, g_mem, wkv_all, k_norm_g):
    m, d = mem.shape

    def body(mem_ref, g_ref, w_ref, kg_ref, memn_ref, kn_ref, v_ref):
        memf = mem_ref[...]
        memn = (memf * _rstd(memf) * g_ref[...]).astype(BF16)
        memn_ref[...] = memn
        for b in range(N_DEV):
            kv = _dot(memn, w_ref[b])
            if b < X_HEADS:
                kn_ref[:, b * X_HEAD_DIM:(b + 1) * X_HEAD_DIM] = (kv * _rstd(kv) * kg_ref[...]).astype(BF16)
            else:
                h = b - X_HEADS
                v_ref[:, h * X_HEAD_DIM:(h + 1) * X_HEAD_DIM] = kv.astype(BF16)

    return pl.pallas_call(
        body, name="mem_prep",
        out_shape=[jax.ShapeDtypeStruct((m, d), BF16)] * 3,
    )(mem, g_mem, wkv_all, k_norm_g)


def _x_head(xq_ref, qg, kn_ref, h):
    sl = slice(h * X_HEAD_DIM, (h + 1) * X_HEAD_DIM)
    q = xq_ref[:, sl].astype(F32)
    rq = _rstd(q)
    qhat = q * rq
    qn = (qhat * qg).astype(BF16)
    s = _dot_nt(qn, kn_ref[:, sl]) * (X_HEAD_DIM ** -0.5)
    e = jnp.exp(s - jnp.max(s, axis=-1, keepdims=True))
    p = e / jnp.sum(e, axis=-1, keepdims=True)
    return sl, rq, qhat, qn, p


def _x_fwd(proj, q_norm_g, kn, v, tm):
    t = proj.shape[0]
    m = kn.shape[0]

    def body(xq_ref, qg_ref, kn_ref, v_ref, o_ref):
        for h in range(X_HEADS):
            sl, _, _, _, p = _x_head(xq_ref, qg_ref[...], kn_ref, h)
            o_ref[:, sl] = _dot(p.astype(BF16), v_ref[:, sl]).astype(BF16)

    return pl.pallas_call(
        body, name="x_fwd",
        grid=(t // tm,),
        in_specs=[pl.BlockSpec((tm, D_MODEL), lambda i: (i, COL_XQ)),
                  pl.BlockSpec((1, X_HEAD_DIM), lambda i: (0, 0)),
                  pl.BlockSpec((m, D_MODEL), lambda i: (0, 0)),
                  pl.BlockSpec((m, D_MODEL), lambda i: (0, 0))],
        out_specs=pl.BlockSpec((tm, D_MODEL), lambda i: (i, 0)),
        out_shape=jax.ShapeDtypeStruct((t, D_MODEL), BF16),
        compiler_params=pltpu.CompilerParams(dimension_semantics=("parallel",)),
    )(proj, q_norm_g, kn, v)


def _gate_spec(tm, branch):
    return pl.BlockSpec((tm, D_MODEL), lambda i: (i, COL_GATE + branch))


def _merge_fwd(x, proj, a_conv, o_sb, o_x, w_conv_out, w_sb_out, w_x_out, w_out, tm):
    t, d = x.shape

    def body(x_ref, g0_ref, g1_ref, g2_ref, a_ref, s_ref, xo_ref, wc_ref, ws_ref, wx_ref, wo_ref,
             x1_ref, yc_ref, ys_ref, yx_ref, mg_ref):
        merged = jnp.zeros((tm, d), F32)
        for gate_ref, b_ref, w_ref, y_ref in ((g0_ref, a_ref, wc_ref, yc_ref), (g1_ref, s_ref, ws_ref, ys_ref),
                                              (g2_ref, xo_ref, wx_ref, yx_ref)):
            yv = _dot(b_ref[...].astype(BF16), w_ref[...])
            y_ref[...] = yv.astype(BF16)
            merged = merged + _sigmoid(gate_ref[...].astype(F32)) * yv
        mb = merged.astype(BF16)
        mg_ref[...] = mb
        x1_ref[...] = x_ref[...] + _dot(mb, wo_ref[...])

    tile = pl.BlockSpec((tm, d), lambda i: (i, 0))
    wfull = pl.BlockSpec((d, d), lambda i: (0, 0))
    return pl.pallas_call(
        body, name="merge_fwd",
        grid=(t // tm,),
        in_specs=[tile] + [_gate_spec(tm, b) for b in range(N_BRANCH)] + [tile, tile, tile,
                                                                           wfull, wfull, wfull, wfull],
        out_specs=[tile] * 5,
        out_shape=[jax.ShapeDtypeStruct((t, d), F32)] + [jax.ShapeDtypeStruct((t, d), BF16)] * 4,
        compiler_params=pltpu.CompilerParams(dimension_semantics=("parallel",)),
    )(x, proj, proj, proj, a_conv, o_sb, o_x, w_conv_out, w_sb_out, w_x_out, w_out)


def _mlp_fwd(x1, g_mlp, w_up_all, w_down, target, tm):
    t, d = x1.shape
    nb, _, fw = w_up_all.shape

    def body(x1_ref, g_ref, wu_ref, wd_ref, tgt_ref, up_ref, h2_ref, dx2_ref, lsum_ref, acc_ref):
        i, j = pl.program_id(0), pl.program_id(1)

        @pl.when(j == 0)
        def _():
            xf = x1_ref[...]
            h2_ref[...] = (xf * _rstd(xf) * g_ref[...]).astype(BF16)
            acc_ref[...] = jnp.zeros_like(acc_ref)

        @pl.when((i == 0) & (j == 0))
        def _():
            lsum_ref[...] = jnp.zeros_like(lsum_ref)

        up = _dot(h2_ref[...], wu_ref[...])
        up_ref[...] = up.astype(BF16)
        act = jnp.square(jnp.maximum(up, 0.0)).astype(BF16)
        acc_ref[...] += _dot(act, wd_ref[...])

        @pl.when(j == nb - 1)
        def _():
            diff = x1_ref[...] + acc_ref[...] - tgt_ref[...]
            dx2_ref[...] = diff * (1.0 / d)
            lsum_ref[...] += jnp.sum(diff * diff, axis=0, keepdims=True)

    tile = pl.BlockSpec((tm, d), lambda i, j: (i, 0))
    row = pl.BlockSpec((1, d), lambda i, j: (0, 0))
    return pl.pallas_call(
        body, name="mlp_fwd",
        grid=(t // tm, nb),
        in_specs=[tile, row, pl.BlockSpec((None, d, fw), lambda i, j: (j, 0, 0)),
                  pl.BlockSpec((fw, d), lambda i, j: (j, 0)), tile],
        out_specs=[pl.BlockSpec((tm, fw), lambda i, j: (i, j)), tile, tile, row],
        out_shape=[jax.ShapeDtypeStruct((t, nb * fw), BF16), jax.ShapeDtypeStruct((t, d), BF16),
                   jax.ShapeDtypeStruct((t, d), F32), jax.ShapeDtypeStruct((1, d), F32)],
        scratch_shapes=[pltpu.VMEM((tm, d), F32)],
        compiler_params=pltpu.CompilerParams(dimension_semantics=("arbitrary", "arbitrary")),
    )(x1, g_mlp, w_up_all, w_down, target)


def _mlp_bwd(x1, g_mlp, w_up_all, w_down, up, dx2, tm):
    t, d = x1.shape
    nb, _, fw = w_up_all.shape

    def body(x1_ref, g_ref, wu_ref, wd_ref, up_ref, dx2_ref, dup_ref, act_ref, dx1_ref, dg_ref, acc_ref, dyb_ref):
        i, j = pl.program_id(0), pl.program_id(1)

        @pl.when(j == 0)
        def _():
            dyb_ref[...] = dx2_ref[...].astype(BF16)
            acc_ref[...] = jnp.zeros_like(acc_ref)

        @pl.when((i == 0) & (j == 0))
        def _():
            dg_ref[...] = jnp.zeros_like(dg_ref)

        r = jnp.maximum(up_ref[...].astype(F32), 0.0)
        act_ref[...] = jnp.square(r).astype(BF16)
        dup = (_dot_nt(dyb_ref[...], wd_ref[...]) * (2.0 * r)).astype(BF16)
        dup_ref[...] = dup
        acc_ref[...] += _dot_nt(dup, wu_ref[...])

        @pl.when(j == nb - 1)
        def _():
            xf = x1_ref[...]
            rs = _rstd(xf)
            xhat = xf * rs
            dh2 = acc_ref[...]
            dg_ref[...] += jnp.sum(dh2 * xhat, axis=0, keepdims=True)
            dx1_ref[...] = dx2_ref[...] + _rms_bwd(dh2, xhat, rs, g_ref[...])

    tile = pl.BlockSpec((tm, d), lambda i, j: (i, 0))
    row = pl.BlockSpec((1, d), lambda i, j: (0, 0))
    ff = pl.BlockSpec((tm, fw), lambda i, j: (i, j))
    return pl.pallas_call(
        body, name="mlp_bwd",
        grid=(t // tm, nb),
        in_specs=[tile, row, pl.BlockSpec((None, d, fw), lambda i, j: (j, 0, 0)),
                  pl.BlockSpec((fw, d), lambda i, j: (j, 0)), ff, tile],
        out_specs=[ff, ff, tile, row],
        out_shape=[jax.ShapeDtypeStruct((t, nb * fw), BF16), jax.ShapeDtypeStruct((t, nb * fw), BF16),
                   jax.ShapeDtypeStruct((t, d), F32), jax.ShapeDtypeStruct((1, d), F32)],
        scratch_shapes=[pltpu.VMEM((tm, d), F32), pltpu.VMEM((tm, d), BF16)],
        compiler_params=pltpu.CompilerParams(dimension_semantics=("arbitrary", "arbitrary")),
    )(x1, g_mlp, w_up_all, w_down, up, dx2)


def _merge_bwd(dx1, proj, y_conv, y_sb, y_x, w_conv_out, w_sb_out, w_x_out, w_out, tm):
    t, d = dx1.shape

    def body(dx1_ref, g0_ref, g1_ref, g2_ref, yc_ref, ys_ref, yx_ref, wc_ref, ws_ref, wx_ref, wo_ref,
             dgate_ref, dyc_ref, dys_ref, dyx_ref, da_ref, dos_ref, dox_ref):
        dm = _dot_nt(dx1_ref[...].astype(BF16), wo_ref[...])
        for i, (gate_ref, y_ref, w_ref, dy_ref, db_ref) in enumerate(((g0_ref, yc_ref, wc_ref, dyc_ref, da_ref),
                                                                       (g1_ref, ys_ref, ws_ref, dys_ref, dos_ref),
                                                                       (g2_ref, yx_ref, wx_ref, dyx_ref, dox_ref))):
            gt = _sigmoid(gate_ref[...].astype(F32))
            dy = (dm * gt).astype(BF16)
            dy_ref[...] = dy
            dgate_ref[:, i * d:(i + 1) * d] = (dm * y_ref[...].astype(F32) * gt * (1.0 - gt)).astype(BF16)
            db_ref[...] = _dot_nt(dy, w_ref[...]).astype(BF16)

    tile = pl.BlockSpec((tm, d), lambda i: (i, 0))
    wfull = pl.BlockSpec((d, d), lambda i: (0, 0))
    return pl.pallas_call(
        body, name="merge_bwd",
        grid=(t // tm,),
        in_specs=[tile] + [_gate_spec(tm, b) for b in range(N_BRANCH)] + [tile, tile, tile,
                                                                           wfull, wfull, wfull, wfull],
        out_specs=[pl.BlockSpec((tm, N_BRANCH * d), lambda i: (i, 0))] + [tile] * 6,
        out_shape=[jax.ShapeDtypeStruct((t, N_BRANCH * d), BF16)] + [jax.ShapeDtypeStruct((t, d), BF16)] * 6,
        compiler_params=pltpu.CompilerParams(dimension_semantics=("parallel",)),
    )(dx1, proj, proj, proj, y_conv, y_sb, y_x, w_conv_out, w_sb_out, w_x_out, w_out)


def _conv_bwd(proj, conv_w, da, cw):
    t = proj.shape[0]
    nper = D_MODEL // cw

    def body(ch_ref, cb_ref, cc_ref, w_ref, da_ref, dch_ref, dcb_ref, dcc_ref, dw_ref):
        ch, cb, cc, u, u1, u2, cv, w, row = _conv_terms(ch_ref, cb_ref, cc_ref, w_ref)
        dav = da_ref[...].astype(F32)
        dcb_ref[...] = (dav * cv).astype(BF16)
        dcv = dav * cb
        n1 = jnp.where(row < t - 1, pltpu.roll(dcv, t - 1, 0), 0.0)
        n2 = jnp.where(row < t - 2, pltpu.roll(dcv, t - 2, 0), 0.0)
        du = w[2] * dcv + w[1] * n1 + w[0] * n2
        dcc_ref[...] = (du * ch).astype(BF16)
        dch_ref[...] = (du * cc).astype(BF16)
        dw_ref[0:1, :] = jnp.sum(dcv * u2, axis=0, keepdims=True)
        dw_ref[1:2, :] = jnp.sum(dcv * u1, axis=0, keepdims=True)
        dw_ref[2:3, :] = jnp.sum(dcv * u, axis=0, keepdims=True)

    def col(piece):
        return pl.BlockSpec((t, cw), lambda j: (0, piece * nper + j))

    out_col = pl.BlockSpec((t, cw), lambda j: (0, j))
    wspec = pl.BlockSpec((3, cw), lambda j: (0, j))
    return pl.pallas_call(
        body, name="conv_bwd",
        grid=(nper,),
        in_specs=[col(COL_CH), col(COL_CB), col(COL_CC), wspec, out_col],
        out_specs=[out_col, out_col, out_col, wspec],
        out_shape=[jax.ShapeDtypeStruct((t, D_MODEL), BF16)] * 3 + [jax.ShapeDtypeStruct((3, D_MODEL), F32)],
        compiler_params=pltpu.CompilerParams(dimension_semantics=("parallel",)),
    )(proj, proj, proj, conv_w, da)


def _sb_bwd(proj, kt4, do_sb, o_sb, weights, logits, tq, pair_sums):
    t = proj.shape[0]
    nq = t // tq
    pairs = D_MODEL // SB_BLOCK
    scale = SB_HEAD_DIM ** -0.5
    qi_tab, g_tab = _sb_schedule(nq)
    ns = qi_tab.shape[0]
    n = len(pair_sums)

    def body(qi_ref, g_ref, q_ref, v_ref, kt_ref, do_ref, o_ref, a_ref, z_ref, *rest):
        dq_ref, dk_ref, dv_ref = rest[n:n + 3]
        dk_acc, dv_acc, dqt_ref, carry_ref, qm_ref, dom_ref, dot_ref, dsum_ref = rest[2 * n + 3:2 * n + 11]
        begin, finish = _chip_exchange_phases(rest[:n], rest[n + 3:2 * n + 3], *rest[2 * n + 11:])
        pi, si = pl.program_id(0), pl.program_id(1)
        diagonal = g_ref[si] == qi_ref[si]
        pl.when((pi == 0) & (si == 0))(begin)

        @pl.when(si == 0)
        def _():
            dk_acc[...] = jnp.zeros_like(dk_acc)
            dv_acc[...] = jnp.zeros_like(dv_acc)

        @pl.when(diagonal)
        def _():
            dqt_ref[...] = jnp.zeros_like(dqt_ref)
            carry_ref[...] = jnp.zeros_like(carry_ref)
            q = q_ref[...].astype(F32) * scale
            do = do_ref[...].astype(F32)
            dot_ = do.T
            prod = dot_ * o_ref[...].T
            for h in range(2):
                rows = slice(h * SB_HEAD_DIM, (h + 1) * SB_HEAD_DIM)
                qm_ref[h] = _head_lanes(q, h)
                dom_ref[h] = _head_lanes(do, h)
                dot_ref[h] = _head_rows(dot_, h)
                dsum_ref[h] = jnp.sum(prod[rows, :], axis=0, keepdims=True)

        def step(past):
            u_mat = _scan_matrix(False, 1.0)
            ks = pl.multiple_of(g_ref[si] * tq, tq)
            dk_add = jnp.zeros((tq, SB_BLOCK), F32)
            dv_add = jnp.zeros((tq, SB_BLOCK), F32)
            for h in range(2):
                rows = slice(h * SB_HEAD_DIM, (h + 1) * SB_HEAD_DIM)
                ab = a_ref[h]
                gw = _dot(v_ref[...], dot_ref[h]) * ab.astype(F32)
                after, carry = _group_suffix(u_mat, gw, carry_ref[h])
                sig = pl.reciprocal(1.0 + jnp.exp(-z_ref[h].astype(F32)), approx=True)
                dz = gw - sig * (dsum_ref[h] - after)
                if past is not None:
                    dz = jnp.where(past, dz, 0.0)
                dzb = dz.astype(BF16)
                dqt_ref[h] += _dot(kt_ref[rows, :], dzb)
                dk_add = dk_add + _dot(dzb, qm_ref[h])
                dv_add = dv_add + _dot(ab, dom_ref[h])
                carry_ref[h] = carry
            dk_acc[pl.ds(ks, tq), :] += dk_add
            dv_acc[pl.ds(ks, tq), :] += dv_add

        pl.when(diagonal)(lambda: step(_past_mask(tq)))
        pl.when(jnp.logical_not(diagonal))(lambda: step(None))

        @pl.when(g_ref[si] == 0)
        def _():
            dq_ref[...] = (jnp.concatenate([dqt_ref[0], dqt_ref[1]], axis=0).T * scale).astype(BF16)

        @pl.when(si == ns - 1)
        def _():
            dk_ref[...] = dk_acc[...].astype(BF16)
            dv_ref[...] = dv_acc[...].astype(BF16)

        pl.when((pi == pairs - 1) & (si == ns - 1))(finish)

    qblk = lambda base: pl.BlockSpec((tq, SB_BLOCK), lambda p, s, qt_, gt_: (qt_[s], base * pairs + p))
    kgrp = lambda base: pl.BlockSpec((tq, SB_BLOCK), lambda p, s, qt_, gt_: (gt_[s], base * pairs + p))
    seq = pl.BlockSpec((t, SB_BLOCK), lambda p, s, qt_, gt_: (0, p))
    tr = pl.BlockSpec((None, None, SB_BLOCK, tq), lambda p, s, qt_, gt_: (p, gt_[s], 0, 0))
    tile = pl.BlockSpec((None, None, 2, tq, tq), lambda p, s, qt_, gt_: (p, s, 0, 0, 0))
    outs = pl.pallas_call(
        body, name="sb_bwd",
        grid_spec=pltpu.PrefetchScalarGridSpec(
            num_scalar_prefetch=2, grid=(pairs, ns),
            in_specs=[qblk(COL_SQ), kgrp(COL_SV), tr, qblk(0), qblk(0), tile, tile] + [ANY] * n,
            out_specs=[qblk(0), seq, seq] + [ANY] * n,
            scratch_shapes=[pltpu.VMEM((t, SB_BLOCK), F32), pltpu.VMEM((t, SB_BLOCK), F32),
                            pltpu.VMEM((2, SB_HEAD_DIM, tq), F32), pltpu.VMEM((2, 1, tq), F32),
                            pltpu.VMEM((2, tq, SB_BLOCK), BF16), pltpu.VMEM((2, tq, SB_BLOCK), BF16),
                            pltpu.VMEM((2, SB_BLOCK, tq), BF16), pltpu.VMEM((2, 1, tq), F32)] + _chip_exchange_sems(n)),
        out_shape=[jax.ShapeDtypeStruct((t, D_MODEL), BF16)] * 3 + _chip_exchange_shapes(pair_sums),
        compiler_params=pltpu.CompilerParams(dimension_semantics=("arbitrary", "arbitrary")),
    )(qi_tab, g_tab, proj, proj, kt4, do_sb, o_sb, weights, logits, *pair_sums)
    return outs[0], outs[1], outs[2], outs[3:]


def _x_bwd(proj, q_norm_g, kn, v, do_x, tm, grads):
    t = proj.shape[0]
    m = kn.shape[0]
    scale = X_HEAD_DIM ** -0.5
    nt = t // tm
    n = len(grads)

    def body(xq_ref, qg_ref, kn_ref, v_ref, do_ref, *rest):
        dxq_ref, dkn_ref, dv_ref, dqg_ref = rest[n:n + 4]
        begin, finish = _pair_exchange_phases(rest[:n], rest[n + 4:2 * n + 4], *rest[2 * n + 4:])

        @pl.when(pl.program_id(0) == 0)
        def _():
            begin()
            dkn_ref[...] = jnp.zeros_like(dkn_ref)
            dv_ref[...] = jnp.zeros_like(dv_ref)
            dqg_ref[...] = jnp.zeros_like(dqg_ref)

        qg = qg_ref[...]
        for h in range(X_HEADS):
            sl, rq, qhat, qn, p = _x_head(xq_ref, qg, kn_ref, h)
            do_h = do_ref[:, sl]
            dp = _dot_nt(do_h, v_ref[:, sl])
            ds = (p * (dp - jnp.sum(dp * p, axis=-1, keepdims=True)) * scale).astype(BF16)
            dqn = _dot(ds, kn_ref[:, sl])
            dkn_ref[:, sl] += _dot_tn(ds, qn)
            dv_ref[:, sl] += _dot_tn(p.astype(BF16), do_h)
            dqg_ref[...] += jnp.sum(dqn * qhat, axis=0, keepdims=True)
            dxq_ref[:, sl] = _rms_bwd(dqn, qhat, rq, qg).astype(BF16)

        pl.when(pl.program_id(0) == nt - 1)(finish)

    full = pl.BlockSpec((m, D_MODEL), lambda i: (0, 0))
    gain = pl.BlockSpec((1, X_HEAD_DIM), lambda i: (0, 0))
    tile = pl.BlockSpec((tm, D_MODEL), lambda i: (i, 0))
    outs = pl.pallas_call(
        body, name="x_bwd",
        grid=(nt,),
        in_specs=[pl.BlockSpec((tm, D_MODEL), lambda i: (i, COL_XQ)), gain, full, full, tile] + [ANY] * n,
        out_specs=[tile, full, full, gain] + [ANY] * n,
        out_shape=[jax.ShapeDtypeStruct((t, D_MODEL), BF16), jax.ShapeDtypeStruct((m, D_MODEL), F32),
                   jax.ShapeDtypeStruct((m, D_MODEL), F32), jax.ShapeDtypeStruct((1, X_HEAD_DIM), F32)]
        + _pair_exchange_shapes(grads),
        scratch_shapes=_pair_exchange_sems(n),
        compiler_params=pltpu.CompilerParams(dimension_semantics=("arbitrary",)),
    )(proj, q_norm_g, kn, v, do_x, *grads)
    return outs[0], outs[1], outs[2], outs[3], outs[4:]


def _mem_bwd(mem, g_mem, wkv_all, k_norm_g, dkn, dv):
    m, d = mem.shape

    def body(mem_ref, g_ref, w_ref, kg_ref, dkn_ref, dv_ref, dkv_ref, dgm_ref, dkg_ref):
        memf = mem_ref[...]
        mem_hat = memf * _rstd(memf)
        memn = (mem_hat * g_ref[...]).astype(BF16)
        kg = kg_ref[...]
        dmemn = jnp.zeros((m, d), F32)
        dkg = jnp.zeros((1, X_HEAD_DIM), F32)
        for b in range(N_DEV):
            sl = slice(b * X_HEAD_DIM, (b + 1) * X_HEAD_DIM)
            if b < X_HEADS:
                kv = _dot(memn, w_ref[b])
                rk = _rstd(kv)
                khat = kv * rk
                dkn_h = dkn_ref[:, sl]
                dkg = dkg + jnp.sum(dkn_h * khat, axis=0, keepdims=True)
                dblk = _rms_bwd(dkn_h, khat, rk, kg).astype(BF16)
            else:
                hs = slice((b - X_HEADS) * X_HEAD_DIM, (b - X_HEADS + 1) * X_HEAD_DIM)
                dblk = dv_ref[:, hs].astype(BF16)
            dkv_ref[:, sl] = dblk
            dmemn = dmemn + _dot_nt(dblk, w_ref[b])
        dgm_ref[...] = jnp.sum(dmemn * mem_hat, axis=0, keepdims=True)
        dkg_ref[...] = dkg

    return pl.pallas_call(
        body, name="mem_bwd",
        out_shape=[jax.ShapeDtypeStruct((m, 2 * d), BF16), jax.ShapeDtypeStruct((1, d), F32),
                   jax.ShapeDtypeStruct((1, X_HEAD_DIM), F32)],
    )(mem, g_mem, wkv_all, k_norm_g, dkn, dv)


def _in_proj_bwd(x, g_mix, w_in_all, dproj, dx1, tm, pair_sums):
    t, d = x.shape
    nb, _, bw = w_in_all.shape
    nt = t // tm
    n = len(pair_sums)

    def body(x_ref, g_ref, w_ref, dp_ref, dx1_ref, *rest):
        dx_ref, dg_ref = rest[n:n + 2]
        acc_ref = rest[2 * n + 2]
        begin, finish = _chip_exchange_phases(rest[:n], rest[n + 2:2 * n + 2], *rest[2 * n + 3:])
        i, j = pl.program_id(0), pl.program_id(1)
        pl.when((i == 0) & (j == 0))(begin)

        @pl.when(j == 0)
        def _():
            acc_ref[...] = jnp.zeros_like(acc_ref)

        @pl.when((i == 0) & (j == 0))
        def _():
            dg_ref[...] = jnp.zeros_like(dg_ref)

        acc_ref[...] += _dot_nt(dp_ref[...], w_ref[...])

        @pl.when(j == nb - 1)
        def _():
            xf = x_ref[...]
            rs = _rstd(xf)
            xhat = xf * rs
            dh = acc_ref[...]
            dg_ref[...] += jnp.sum(dh * xhat, axis=0, keepdims=True)
            dx_ref[...] = dx1_ref[...] + _rms_bwd(dh, xhat, rs, g_ref[...])

        pl.when((i == nt - 1) & (j == nb - 1))(finish)

    tile = pl.BlockSpec((tm, d), lambda i, j: (i, 0))
    row = pl.BlockSpec((1, d), lambda i, j: (0, 0))
    outs = pl.pallas_call(
        body, name="in_proj_bwd",
        grid=(nt, nb),
        in_specs=[tile, row, pl.BlockSpec((None, d, bw), lambda i, j: (j, 0, 0)),
                  pl.BlockSpec((tm, bw), lambda i, j: (i, j)), tile] + [ANY] * n,
        out_specs=[tile, row] + [ANY] * n,
        out_shape=[jax.ShapeDtypeStruct((t, d), F32), jax.ShapeDtypeStruct((1, d), F32)] + _chip_exchange_shapes(pair_sums),
        scratch_shapes=[pltpu.VMEM((tm, d), F32)] + _chip_exchange_sems(n),
        compiler_params=pltpu.CompilerParams(dimension_semantics=("arbitrary", "arbitrary")),
    )(x, g_mix, w_in_all, dproj, dx1, *pair_sums)
    return outs[0], outs[1], outs[2:]


def _weight_grad(a, b, bw, tmm, name):
    t, m = a.shape
    n = b.shape[1]
    tmm = min(tmm, m)

    def body(a_ref, b_ref, o_ref):
        o_ref[...] = _dot_tn(a_ref[...].astype(BF16), b_ref[...].astype(BF16)).astype(BF16)

    return pl.pallas_call(
        body, name=name,
        grid=(m // tmm, n // bw),
        in_specs=[pl.BlockSpec((t, tmm), lambda i, j: (0, i)), pl.BlockSpec((t, bw), lambda i, j: (0, j))],
        out_specs=pl.BlockSpec((None, tmm, bw), lambda i, j: (j, i, 0)),
        out_shape=jax.ShapeDtypeStruct((n // bw, m, bw), BF16),
        compiler_params=pltpu.CompilerParams(dimension_semantics=("parallel", "parallel")),
    )(a, b)


def _weight_grad_blocks(a, b, bw, blocks, tmm, name, send=()):
    t, m = a.shape
    tmm = min(tmm, m)
    nm = m // tmm
    n = len(send)

    def body(blocks_ref, a_ref, b_ref, *rest):
        o_ref = rest[n]
        i, j = pl.program_id(0), pl.program_id(1)
        if n:
            begin, finish = _pair_exchange_phases(rest[:n], rest[n + 1:2 * n + 1], *rest[2 * n + 1:], by_slot=True)
            pl.when((i == 0) & (j == 0))(begin)
        o_ref[...] = _dot_tn(a_ref[...].astype(BF16), b_ref[...].astype(BF16)).astype(BF16)
        if n:
            pl.when((i == nm - 1) & (j == 3))(finish)

    outs = pl.pallas_call(
        body, name=name,
        grid_spec=pltpu.PrefetchScalarGridSpec(
            num_scalar_prefetch=1, grid=(nm, 4),
            in_specs=[pl.BlockSpec((t, tmm), lambda i, j, blk: (0, i)),
                      pl.BlockSpec((t, bw), lambda i, j, blk: (0, blk[j]))] + [ANY] * n,
            out_specs=[pl.BlockSpec((None, tmm, bw), lambda i, j, blk: (j, i, 0))] + [ANY] * n,
            scratch_shapes=_pair_exchange_sems(n) if n else []),
        out_shape=[jax.ShapeDtypeStruct((4, m, bw), BF16)] + _pair_exchange_shapes(send),
        compiler_params=pltpu.CompilerParams(dimension_semantics=("arbitrary", "arbitrary")),
    )(blocks, a, b, *send)
    return outs[0], outs[1:]


def _pair_sum(grads, recvs, own_blocks, name):
    k = len(grads)
    _, rows, cols = grads[0].shape

    def body(idx_ref, *refs):
        for a in range(k):
            refs[2 * k + a][...] = (refs[a][...].astype(F32) + refs[k + a][...].astype(F32)).astype(BF16)

    slot = pl.BlockSpec((None, rows, cols), lambda r, idx: (r, 0, 0))
    return pl.pallas_call(
        body, name=name,
        grid_spec=pltpu.PrefetchScalarGridSpec(
            num_scalar_prefetch=1, grid=(4,),
            in_specs=[pl.BlockSpec((None, rows, cols), lambda r, idx: (idx[r], 0, 0))] * k + [slot] * k,
            out_specs=[slot] * k),
        out_shape=[jax.ShapeDtypeStruct((4, rows, cols), BF16)] * k,
        compiler_params=pltpu.CompilerParams(dimension_semantics=("parallel",)),
    )(own_blocks, *grads, *recvs)


def _adamw_math(w, g, m, v):
    m = ADAM_B1 * m + (1.0 - ADAM_B1) * g
    v = ADAM_B2 * v + (1.0 - ADAM_B2) * jnp.square(g)
    m_hat = m / (1.0 - ADAM_B1 ** ADAM_STEP)
    v_hat = v / (1.0 - ADAM_B2 ** ADAM_STEP)
    delta = -ADAM_LR * (m_hat / (jnp.sqrt(v_hat) + ADAM_EPS) + ADAM_WD * w)
    return delta, m, v


def _adamw_sharded(shards, tr, name):
    k = len(shards)
    rows, cols = shards[0][2].shape
    tr = min(tr, rows)

    def body(*refs):
        for a in range(k):
            h_ref, r_ref, w_ref, m_ref, v_ref = refs[5 * a:5 * a + 5]
            g_out, d_out, m_out, v_out = refs[5 * k + 4 * a:5 * k + 4 * a + 4]
            g = h_ref[...].astype(F32)
            for r in range(3):
                g = g + r_ref[r].astype(F32)
            g_out[...] = g
            d_out[...], m_out[...], v_out[...] = _adamw_math(w_ref[...], g, m_ref[...], v_ref[...])

    tile = pl.BlockSpec((tr, cols), lambda i: (i, 0))
    outs = pl.pallas_call(
        body, name=name,
        grid=(rows // tr,),
        in_specs=[pl.BlockSpec((None, tr, cols), lambda i: (0, i, 0)),
                  pl.BlockSpec((3, tr, cols), lambda i: (0, i, 0)), tile, tile, tile] * k,
        out_specs=[tile] * (4 * k),
        out_shape=[jax.ShapeDtypeStruct((rows, cols), F32)] * (4 * k),
        compiler_params=pltpu.CompilerParams(dimension_semantics=("parallel",)),
    )(*[op for shard in shards for op in shard])
    return [tuple(outs[4 * a:4 * a + 4]) for a in range(k)]


SMALL_ROWS = 16


def _pack_rows(dg_mix, dg_mem, dg_mlp, dqg, dkg, dconv, lsum):
    def body(a_ref, b_ref, c_ref, q_ref, k_ref, cv_ref, l_ref, o_ref):
        o_ref[...] = jnp.zeros_like(o_ref)
        for r, ref in enumerate((a_ref, b_ref, c_ref)):
            o_ref[r:r + 1, :] = ref[...]
        o_ref[3:4, :X_HEAD_DIM] = q_ref[...]
        o_ref[4:5, :X_HEAD_DIM] = k_ref[...]
        o_ref[5:8, :] = cv_ref[...]
        o_ref[8:9, :] = l_ref[...]

    return pl.pallas_call(body, name="small_pack", out_shape=jax.ShapeDtypeStruct((SMALL_ROWS, D_MODEL), F32))(
        dg_mix, dg_mem, dg_mlp, dqg, dkg, dconv, lsum)


def _small_sum(gathered):
    def body(g_ref, o_ref):
        total = g_ref[0]
        for dev in range(1, N_DEV):
            total = total + g_ref[dev]
        o_ref[...] = jnp.zeros_like(o_ref)
        for piece in range(5):
            o_ref[piece * SMALL_TILE:piece * SMALL_TILE + 1, :] = total[piece:piece + 1]
        o_ref[5 * SMALL_TILE:5 * SMALL_TILE + 3, :] = total[5:8]
        o_ref[6 * SMALL_TILE:6 * SMALL_TILE + 1, :] = total[8:9]

    return pl.pallas_call(body, name="small_grad_sum",
                          out_shape=jax.ShapeDtypeStruct((7 * SMALL_TILE, D_MODEL), F32))(gathered)


def _adamw_small(w, g, m, v):
    def body(w_ref, g_ref, m_ref, v_ref, d_out, m_out, v_out):
        d_out[...], m_out[...], v_out[...] = _adamw_math(w_ref[...], g_ref[...], m_ref[...], v_ref[...])

    return pl.pallas_call(body, name="adamw_small", out_shape=[jax.ShapeDtypeStruct(w.shape, F32)] * 3)(w, g, m, v)


def _pad_tile(a):
    return jnp.pad(a, ((0, SMALL_TILE - a.shape[0]), (0, D_MODEL - a.shape[1])))


def _pack_small(*pieces):
    return jnp.concatenate([_pad_tile(a) for a in pieces], axis=0)


def kernel(x, mem, g_mix, g_mem, w_in, conv_w, w_conv_out, w_sb_out, q_norm_g, k_norm_g, w_mem_kv, w_x_out, w_out, g_mlp, w_up, w_down, loss_target, m_g_mix, m_g_mem, m_w_in, m_conv_w, m_w_conv_out, m_w_sb_out, m_q_norm_g, m_k_norm_g, m_w_mem_kv, m_w_x_out, m_w_out, m_g_mlp, m_w_up, m_w_down, v_g_mix, v_g_mem, v_w_in, v_conv_w, v_w_conv_out, v_w_sb_out, v_q_norm_g, v_k_norm_g, v_w_mem_kv, v_w_x_out, v_w_out, v_g_mlp, v_w_up, v_w_down):
    xpos, ypos, cpos = _mesh_pos()
    me = 4 * xpos + 2 * ypos + cpos
    x2d, mem2d, tgt2d = x[0], mem[0], loss_target[0]
    t = x2d.shape[0]
    tm = min(512, t)
    tm_s = min(256, t)

    big = {
        "w_in": (w_in[0], m_w_in[0], v_w_in[0]),
        "w_conv_out": (w_conv_out[0], m_w_conv_out[0], v_w_conv_out[0]),
        "w_sb_out": (w_sb_out[0], m_w_sb_out[0], v_w_sb_out[0]),
        "w_mem_kv": (w_mem_kv[0], m_w_mem_kv[0], v_w_mem_kv[0]),
        "w_x_out": (w_x_out[0], m_w_x_out[0], v_w_x_out[0]),
        "w_out": (w_out[0], m_w_out[0], v_w_out[0]),
        "w_up": (w_up[0], m_w_up[0], v_w_up[0]),
        "w_down": (w_down[0], m_w_down[0], v_w_down[0]),
    }
    late = [n for n in big if n != "w_in"]
    as_bf16 = lambda group: [big[n][0].astype(BF16) for n in group]
    conv_pad = jnp.pad(conv_w[0], ((0, 8 - 3), (0, 0)))

    proj, h, (w_in_all, conv_all) = _in_proj(x2d, g_mix, _arrival_blocks(xpos, ypos, cpos), tm,
                                             [big["w_in"][0].astype(BF16), conv_pad])
    conv_full = conv_all[:, :3, :].transpose(1, 0, 2).reshape(3, D_MODEL)
    a_conv = _conv_fwd(proj, conv_full, 256)
    tq = min(SB_QUERY_TILE, t)
    pairs = D_MODEL // SB_BLOCK

    def groups_t(cols):
        return cols.reshape(t // tq, tq, pairs, SB_BLOCK).transpose(2, 0, 3, 1)

    kt4 = groups_t(proj[:, COL_SK * D_MODEL:(COL_SK + 1) * D_MODEL])
    vt4 = groups_t(proj[:, COL_SV * D_MODEL:(COL_SV + 1) * D_MODEL])
    o_sb, sb_weights, sb_logits, gathered = _sb_fwd(proj, vt4, tq, as_bf16(late))
    full = dict(zip(late, gathered))
    wkv_all, w_up_all = full["w_mem_kv"], full["w_up"]
    rows_full = lambda a: a.reshape(a.shape[0] * a.shape[1], a.shape[2])
    wc, ws, wx, wo, wd = (rows_full(full[n]) for n in ("w_conv_out", "w_sb_out", "w_x_out", "w_out", "w_down"))
    mem_n, kn, vmem = _mem_prep(mem2d, g_mem, wkv_all, k_norm_g)
    o_x = _x_fwd(proj, q_norm_g, kn, vmem, tm_s)
    x1, y_conv, y_sb, y_x, merged = _merge_fwd(x2d, proj, a_conv, o_sb, o_x, wc, ws, wx, wo, tm_s)
    up, h2, dx2, lsum = _mlp_fwd(x1, g_mlp, w_up_all, wd, tgt2d, tm)

    dup, act, dx1, dg_mlp = _mlp_bwd(x1, g_mlp, w_up_all, wd, up, dx2, tm)
    dgate, dy_conv, dy_sb, dy_x, da_conv, do_sb, do_x = _merge_bwd(dx1, proj, y_conv, y_sb, y_x, wc, ws, wx, wo, tm_s)
    dch, dcb, dcc, dconv = _conv_bwd(proj, conv_full, da_conv, 256)
    wgrads = {
        "w_conv_out": _weight_grad(a_conv, dy_conv, D_MODEL, 512, "dw_conv_out"),
        "w_sb_out": _weight_grad(o_sb, dy_sb, D_MODEL, 512, "dw_sb_out"),
        "w_x_out": _weight_grad(o_x, dy_x, D_MODEL, 512, "dw_x_out"),
        "w_out": _weight_grad(merged, dx1, D_MODEL, 512, "dw_out"),
        "w_up": _weight_grad(h2, dup, w_up_all.shape[2], 512, "dw_up"),
        "w_down": _weight_grad(act, dx2, D_MODEL, 512, "dw_down"),
    }

    own_blocks = jnp.stack([4 * (xpos ^ dx) + 2 * (ypos ^ dy) + cpos for dx in (0, 1) for dy in (0, 1)]).astype(jnp.int32)
    sibling_blocks = own_blocks + (1 - 2 * cpos)
    blocked = lambda n: wgrads[n].reshape((N_DEV,) + big[n][0].shape)
    same_shape = ["w_conv_out", "w_sb_out", "w_x_out", "w_out"]

    def pair_sum(group, from_sibling):
        return dict(zip(group, _pair_sum([blocked(n) for n in group], from_sibling, own_blocks, "pair_sum_" + group[0])))

    behind_x = same_shape + ["w_up", "w_down"]
    dxq, dkn, dvm, dqg, from_sibling = _x_bwd(proj, q_norm_g, kn, vmem, do_x, tm_s, [blocked(n) for n in behind_x])
    pair_sums = pair_sum(same_shape, from_sibling[:4])
    pair_sums.update(pair_sum(["w_up"], from_sibling[4:5]))
    pair_sums.update(pair_sum(["w_down"], from_sibling[5:6]))
    dkv, dg_mem, dkg = _mem_bwd(mem2d, g_mem, wkv_all, k_norm_g, dkn, dvm)
    wgrads["w_mem_kv"] = _weight_grad(mem_n, dkv, wkv_all.shape[2], 512, "dw_mem_kv")
    pair_sums.update(pair_sum(["w_mem_kv"], _pair_exchange([blocked("w_mem_kv")], "grad_pair_exchange_w_mem_kv")))
    dq, dk, dv, from_chips_late = _sb_bwd(proj, kt4, do_sb, o_sb, sb_weights, sb_logits, tq,
                                          [pair_sums[n] for n in late])
    from_chips = dict(zip(late, from_chips_late))
    dproj = jnp.concatenate([dch, dcb, dcc, dq, dk, dv, dxq, dgate], axis=1)
    bw_in = w_in_all.shape[2]
    dw_in_sibling, _ = _weight_grad_blocks(h, dproj, bw_in, sibling_blocks, 512, "dw_in_sibling")
    dw_in_own, (dw_in_recv,) = _weight_grad_blocks(h, dproj, bw_in, own_blocks, 512, "dw_in_own", send=[dw_in_sibling])
    pair_sums["w_in"], = _pair_sum([dw_in_own], [dw_in_recv], jnp.arange(4, dtype=jnp.int32), "pair_sum_w_in")
    grad_x, dg_mix, (from_chips["w_in"],) = _in_proj_bwd(x2d, g_mix, w_in_all, dproj, dx1, tm, [pair_sums["w_in"]])
    res = {}
    for group in [same_shape] + [[n] for n in big if n not in same_shape]:
        updates = _adamw_sharded([(pair_sums[n], from_chips[n]) + big[n] for n in group], 256, "adamw_" + group[0])
        res.update(zip(group, updates))

    part = _pack_rows(dg_mix, dg_mem, dg_mlp, dqg, dkg, dconv, lsum)
    gsum = _small_sum(_small_all_gather(part))
    loss = 0.5 * jnp.sum(gsum[6 * SMALL_TILE]) / D_MODEL
    conv_cols = lax.dynamic_slice(gsum[5 * SMALL_TILE:6 * SMALL_TILE], (0, me * (D_MODEL // N_DEV)),
                                  (SMALL_TILE, D_MODEL // N_DEV))
    g_small = jnp.concatenate([gsum[:5 * SMALL_TILE], _pad_tile(conv_cols)], axis=0)
    w_small = _pack_small(g_mix, g_mem, g_mlp, q_norm_g, k_norm_g, conv_w[0])
    m_small = _pack_small(m_g_mix, m_g_mem, m_g_mlp, m_q_norm_g, m_k_norm_g, m_conv_w[0])
    v_small = _pack_small(v_g_mix, v_g_mem, v_g_mlp, v_q_norm_g, v_k_norm_g, v_conv_w[0])
    d_small, nm_small, nv_small = _adamw_small(w_small, g_small, m_small, v_small)

    def unpack(p):
        return {"g_mix": p[0:1], "g_mem": p[8:9], "g_mlp": p[16:17], "q_norm_g": p[24:25, :X_HEAD_DIM],
                "k_norm_g": p[32:33, :X_HEAD_DIM], "conv_w": p[40:43, :D_MODEL // N_DEV][None]}

    small = [unpack(p) for p in (g_small, d_small, nm_small, nv_small)]
    order = ["g_mix", "g_mem", "w_in", "conv_w", "w_conv_out", "w_sb_out", "q_norm_g", "k_norm_g", "w_mem_kv",
             "w_x_out", "w_out", "g_mlp", "w_up", "w_down"]
    outs = [loss, grad_x[None]]
    for kind in range(4):
        for n in order:
            outs.append(res[n][kind][None] if n in res else small[kind][n])
    return tuple(outs)
```

```python
import jax
import jax.numpy as jnp
from jax import lax
from jax.experimental import pallas as pl
from jax.experimental.pallas import tpu as pltpu

F32 = jnp.float32
BF16 = jnp.bfloat16
MESH = pl.DeviceIdType.MESH

EPS = 1e-6
N_DEV = 8
D_MODEL = 1024
SB_HEAD_DIM = 64
SB_BLOCK = 128
SB_QUERY_TILE = 512
X_HEADS = 4
X_HEAD_DIM = 256
N_BRANCH = 3
COL_CH, COL_CB, COL_CC, COL_SQ, COL_SK, COL_SV, COL_XQ, COL_GATE = 0, 1, 2, 3, 4, 5, 6, 7

ADAM_LR = 0.001
ADAM_B1 = 0.9
ADAM_B2 = 0.999
ADAM_EPS = 1e-08
ADAM_WD = 0.01
ADAM_STEP = 10

SMALL_TILE = 8


def _dot(a, b):
    return jnp.dot(a, b, preferred_element_type=F32)


def _dot_nt(a, b):
    return lax.dot_general(a, b, (((1,), (1,)), ((), ())), preferred_element_type=F32)


def _dot_tn(a, b):
    return lax.dot_general(a, b, (((0,), (0,)), ((), ())), preferred_element_type=F32)


def _rstd(xf):
    return lax.rsqrt(jnp.mean(xf * xf, axis=-1, keepdims=True) + EPS)


def _sigmoid(z):
    return 1.0 / (1.0 + jnp.exp(-z))


def _rms_bwd(dy, xhat, r, g):
    dxhat = dy * g
    return r * (dxhat - xhat * jnp.mean(dxhat * xhat, axis=-1, keepdims=True))


def _mesh_pos():
    return lax.axis_index("x"), lax.axis_index("y"), lax.axis_index("c")


ANY = pl.BlockSpec(memory_space=pl.ANY)


def _gather_shapes(shards):
    return [jax.ShapeDtypeStruct((N_DEV,) + s.shape, s.dtype) for s in shards]


def _gather_sems(n):
    return [pltpu.SemaphoreType.DMA((n, 7)), pltpu.SemaphoreType.DMA((n, 7)), pltpu.SemaphoreType.DMA((n,))]


def _gather_chips(x, y, c):
    return [(x ^ (1 - c), y ^ c), (x ^ c, y ^ (1 - c)), (1 - x, 1 - y)]


def _relayed_chip(chips, order):
    return chips[(1, 0, 2)[order - 4]]


def _gather_phases(ins, outs, send_sems, recv_sems, local_sems, by_arrival=False):
    n = len(ins)
    x, y, c = _mesh_pos()
    me, sibling = (x, y, c), (x, y, 1 - c)
    chips = _gather_chips(x, y, c)

    def blk(a, px, py, pc):
        return outs[a].at[4 * px + 2 * py + pc]

    def copy(a, k, block, to, src=None):
        return pltpu.make_async_remote_copy(
            src_ref=blk(a, *block) if src is None else src, dst_ref=blk(a, *block),
            send_sem=send_sems.at[a, k], recv_sem=recv_sems.at[a, k], device_id=to, device_id_type=MESH)

    def local(a):
        return pltpu.make_async_copy(ins[a], blk(a, *me), local_sems.at[a])

    def own(a):
        return [copy(a, 0, me, sibling, src=ins[a])] + [copy(a, 1 + j, me, (*chips[j], c), src=ins[a]) for j in range(2)]

    def onward(a):
        return copy(a, 3, (*chips[0], c), (*chips[1], c))

    def begin():
        for a in range(n):
            local(a).start()
        for a in range(n):
            for cp in own(a):
                cp.start()

    def arrive(order):
        for a in range(n):
            if order == 0:
                copy(a, 0, sibling, me).wait_recv()
            elif order <= 3:
                chip = chips[order - 1]
                copy(a, order, (*chip, c), me).wait_recv()
                copy(a, 3 + order, (*chip, c), sibling).start()
                if order == 1:
                    onward(a).start()
            else:
                copy(a, order, (*_relayed_chip(chips, order), 1 - c), me).wait_recv()

    def relay():
        for order in (1, 2):
            arrive(order)

    def drain():
        for a in range(n):
            for cp in own(a):
                cp.wait_send()
            onward(a).wait_send()
            for j, chip in enumerate(chips):
                copy(a, 4 + j, (*chip, c), sibling).wait_send()
            local(a).wait()

    def finish():
        for order in (3, 0, 4, 5, 6):
            arrive(order)
        drain()

    if by_arrival:
        return begin, arrive, drain
    return begin, relay, finish


def _pair_exchange(grads, name):
    n = len(grads)

    def body(*refs):
        begin, finish = _pair_exchange_phases(refs[:n], refs[n:2 * n], *refs[2 * n:])
        begin()
        finish()

    return pl.pallas_call(
        body, name=name,
        out_shape=_pair_exchange_shapes(grads),
        in_specs=[ANY] * n, out_specs=[ANY] * n,
        scratch_shapes=_pair_exchange_sems(n),
    )(*grads)


def _pair_exchange_shapes(grads):
    return [jax.ShapeDtypeStruct((4,) + g.shape[1:], g.dtype) for g in grads]


def _pair_exchange_sems(n):
    return [pltpu.SemaphoreType.DMA((n, 4)), pltpu.SemaphoreType.DMA((n, 4))]


def _pair_exchange_phases(ins, outs, send_sems, recv_sems, by_slot=False):
    x, y, c = _mesh_pos()
    xs, ys = (x, 1 - x), (y, 1 - y)

    def copies():
        out = []
        for a in range(len(ins)):
            for r in range(4):
                dx, dy = divmod(r, 2)
                out.append(pltpu.make_async_remote_copy(
                    src_ref=ins[a].at[r if by_slot else 4 * xs[dx] + 2 * ys[dy] + (1 - c)], dst_ref=outs[a].at[r],
                    send_sem=send_sems.at[a, r], recv_sem=recv_sems.at[a, r],
                    device_id=(x, y, 1 - c), device_id_type=MESH))
        return out

    def begin():
        for cp in copies():
            cp.start()

    def finish():
        for cp in copies():
            cp.wait()

    return begin, finish


def _chip_exchange_shapes(sums):
    return [jax.ShapeDtypeStruct((3,) + s.shape[1:], s.dtype) for s in sums]


def _chip_exchange_sems(n):
    return [pltpu.SemaphoreType.DMA((n, 3)), pltpu.SemaphoreType.DMA((n, 3))]


def _chip_exchange_phases(ins, outs, send_sems, recv_sems):
    x, y, c = _mesh_pos()
    xs, ys = (x, 1 - x), (y, 1 - y)

    def copies():
        out = []
        for a in range(len(ins)):
            for r in range(1, 4):
                dx, dy = divmod(r, 2)
                out.append(pltpu.make_async_remote_copy(
                    src_ref=ins[a].at[r], dst_ref=outs[a].at[r - 1],
                    send_sem=send_sems.at[a, r - 1], recv_sem=recv_sems.at[a, r - 1],
                    device_id=(xs[dx], ys[dy], c), device_id_type=MESH))
        return out

    def begin():
        for cp in copies():
            cp.start()

    def finish():
        for cp in copies():
            cp.wait()

    return begin, finish


def _small_all_gather(part):
    rows, cols = part.shape

    def body(in_ref, out_ref, send_sems, recv_sems):
        x, y, c = _mesh_pos()
        xs, ys, cs = (x, 1 - x), (y, 1 - y), (c, 1 - c)
        out_ref[4 * x + 2 * y + c] = in_ref[...]
        copies = []
        for k in range(1, N_DEV):
            dx, dy, dc = k // 4, (k // 2) % 2, k % 2
            copies.append((
                pltpu.make_async_remote_copy(
                    src_ref=in_ref, dst_ref=out_ref.at[4 * x + 2 * y + c],
                    send_sem=send_sems.at[k - 1], recv_sem=recv_sems.at[k - 1],
                    device_id=(xs[dx], ys[dy], cs[dc]), device_id_type=MESH),
                pltpu.make_async_remote_copy(
                    src_ref=in_ref, dst_ref=out_ref.at[4 * xs[dx] + 2 * ys[dy] + cs[dc]],
                    send_sem=send_sems.at[k - 1], recv_sem=recv_sems.at[k - 1],
                    device_id=(xs[dx], ys[dy], cs[dc]), device_id_type=MESH)))
        for send, _ in copies:
            send.start()
        for send, recv in copies:
            recv.wait_recv()
            send.wait_send()

    return pl.pallas_call(
        body, name="small_all_gather",
        out_shape=jax.ShapeDtypeStruct((N_DEV, rows, cols), part.dtype),
        in_specs=[pl.BlockSpec(memory_space=pltpu.VMEM)],
        out_specs=pl.BlockSpec(memory_space=pltpu.VMEM),
        scratch_shapes=[pltpu.SemaphoreType.DMA((N_DEV - 1,)), pltpu.SemaphoreType.DMA((N_DEV - 1,))],
    )(part)


ARRIVAL_ORDER = (0, 1, 4, 2, 5, 3, 6)


def _arrival_blocks(xpos, ypos, cpos):
    chips = _gather_chips(xpos, ypos, cpos)
    by_order = ([4 * xpos + 2 * ypos + (1 - cpos)] + [4 * cx + 2 * cy + cpos for cx, cy in chips]
                + [4 * cx + 2 * cy + (1 - cpos) for cx, cy in (_relayed_chip(chips, o) for o in (4, 5, 6))])
    return jnp.stack([4 * xpos + 2 * ypos + cpos] + [by_order[o] for o in ARRIVAL_ORDER]).astype(jnp.int32)


def _in_proj(x, g_mix, arrival_blocks, tm, shards):
    t, d = x.shape
    bw = shards[0].shape[1]
    nt = t // tm
    n = len(shards)

    def body(blocks_ref, x_ref, g_ref, *rest):
        w_shard = rest[0]
        proj_ref, h_ref = rest[n:n + 2]
        w_all = rest[n + 2]
        h_scr, w_buf, fetch_sems = rest[2 * n + 2:2 * n + 5]
        begin, arrive, drain = _gather_phases(rest[:n], rest[n + 2:2 * n + 2], *rest[2 * n + 5:], by_arrival=True)
        j, i = pl.program_id(0), pl.program_id(1)
        slot = lax.rem(j, 2)

        def fetch(src, into):
            return pltpu.make_async_copy(src, w_buf.at[into], fetch_sems.at[into])

        @pl.when((j == 0) & (i == 0))
        def _():
            begin()
            fetch(w_shard, 0).start()

        @pl.when(j == 0)
        def _():
            xf = x_ref[...]
            hv = (xf * _rstd(xf) * g_ref[...]).astype(BF16)
            h_ref[...] = hv
            h_scr[pl.ds(pl.multiple_of(i * tm, tm), tm), :] = hv

        @pl.when(i == 0)
        def _():
            fetch(w_shard, slot).wait()

        proj_ref[...] = _dot(h_scr[pl.ds(pl.multiple_of(i * tm, tm), tm), :], w_buf[slot]).astype(BF16)

        for nxt in range(1, N_DEV):
            @pl.when((i == nt // 2) & (j == nxt - 1))
            def _():
                arrive(ARRIVAL_ORDER[nxt - 1])
                fetch(w_all.at[blocks_ref[nxt]], 1 - slot).start()

        pl.when((i == nt - 1) & (j == N_DEV - 1))(drain)

    first_pass = lambda j, i, blocks: (jnp.where(j == 0, i, nt - 1), 0)
    outs = pl.pallas_call(
        body, name="in_proj",
        grid_spec=pltpu.PrefetchScalarGridSpec(
            num_scalar_prefetch=1, grid=(N_DEV, nt),
            in_specs=[pl.BlockSpec((tm, d), first_pass), pl.BlockSpec((1, d), lambda j, i, blocks: (0, 0))] + [ANY] * n,
            out_specs=[pl.BlockSpec((tm, bw), lambda j, i, blocks: (i, blocks[j])),
                       pl.BlockSpec((tm, d), first_pass)] + [ANY] * n,
            scratch_shapes=[pltpu.VMEM((t, d), BF16), pltpu.VMEM((2, d, bw), BF16), pltpu.SemaphoreType.DMA((2,))]
            + _gather_sems(n)),
        out_shape=[jax.ShapeDtypeStruct((t, N_DEV * bw), BF16), jax.ShapeDtypeStruct((t, d), BF16)] + _gather_shapes(shards),
        compiler_params=pltpu.CompilerParams(dimension_semantics=("arbitrary", "arbitrary")),
    )(arrival_blocks, x, g_mix, *shards)
    return outs[0], outs[1], outs[2:]


def _conv_terms(ch_ref, cb_ref, cc_ref, w_ref):
    ch, cb, cc = ch_ref[...].astype(F32), cb_ref[...].astype(F32), cc_ref[...].astype(F32)
    u = cc * ch
    row = lax.broadcasted_iota(jnp.int32, u.shape, 0)
    u1 = jnp.where(row >= 1, pltpu.roll(u, 1, 0), 0.0)
    u2 = jnp.where(row >= 2, pltpu.roll(u, 2, 0), 0.0)
    w = (w_ref[0:1, :], w_ref[1:2, :], w_ref[2:3, :])
    cv = w[2] * u + w[1] * u1 + w[0] * u2
    return ch, cb, cc, u, u1, u2, cv, w, row


def _conv_fwd(proj, conv_w, cw):
    t = proj.shape[0]
    nper = D_MODEL // cw

    def body(ch_ref, cb_ref, cc_ref, w_ref, a_ref):
        _, cb, _, _, _, _, cv, _, _ = _conv_terms(ch_ref, cb_ref, cc_ref, w_ref)
        a_ref[...] = (cb * cv).astype(BF16)

    def col(piece):
        return pl.BlockSpec((t, cw), lambda j: (0, piece * nper + j))

    return pl.pallas_call(
        body, name="conv_fwd",
        grid=(nper,),
        in_specs=[col(COL_CH), col(COL_CB), col(COL_CC), pl.BlockSpec((3, cw), lambda j: (0, j))],
        out_specs=pl.BlockSpec((t, cw), lambda j: (0, j)),
        out_shape=jax.ShapeDtypeStruct((t, D_MODEL), BF16),
        compiler_params=pltpu.CompilerParams(dimension_semantics=("parallel",)),
    )(proj, proj, proj, conv_w)


def _scan_matrix(inclusive, value):
    s = lax.broadcasted_iota(jnp.int32, (SB_BLOCK, 2 * SB_BLOCK), 0)
    j = lax.rem(lax.broadcasted_iota(jnp.int32, (SB_BLOCK, 2 * SB_BLOCK), 1), SB_BLOCK)
    return jnp.where((j >= s) if inclusive else (j > s), value, 0.0).astype(BF16)


def _suffix_sum(u_mat, xv):
    hi = xv.astype(BF16)
    lo = (xv - hi.astype(F32)).astype(BF16)
    return _dot(u_mat, jnp.concatenate([hi, lo], axis=0))


def _head_rows(vt, h):
    row = lax.broadcasted_iota(jnp.int32, vt.shape, 0)
    return jnp.where((row >= h * SB_HEAD_DIM) & (row < (h + 1) * SB_HEAD_DIM), vt, 0.0).astype(BF16)


def _head_lanes(v, h):
    lane = lax.broadcasted_iota(jnp.int32, v.shape, 1)
    return jnp.where((lane >= h * SB_HEAD_DIM) & (lane < (h + 1) * SB_HEAD_DIM), v, 0.0).astype(BF16)


def _group_suffix(u_mat, xv, carry, negate=False):
    nblk = xv.shape[0] // SB_BLOCK
    parts = [None] * nblk
    for j in reversed(range(nblk)):
        xj = xv[j * SB_BLOCK:(j + 1) * SB_BLOCK]
        parts[j] = _suffix_sum(u_mat, xj) + carry
        total = jnp.sum(xj, axis=0, keepdims=True)
        carry = carry - total if negate else carry + total
    return jnp.concatenate(parts, axis=0), carry


def _sb_probs(kgrp, qt_h, carry, past):
    z = _dot(kgrp, qt_h)
    softplus = jnp.maximum(z, 0.0) + jnp.log(1.0 + jnp.exp(-jnp.abs(z)))
    if past is not None:
        softplus = jnp.where(past, softplus, 0.0)
    later, carry = _group_suffix(_scan_matrix(True, -1.0), softplus, carry, negate=True)
    a = jnp.exp(z + later)
    if past is not None:
        a = jnp.where(past, a, 0.0)
    return a, z, carry


def _sb_schedule(nq):
    steps = [(qi, g) for qi in range(nq) for g in range(qi, -1, -1)]
    return jnp.asarray([s[0] for s in steps], jnp.int32), jnp.asarray([s[1] for s in steps], jnp.int32)


def _past_mask(tq):
    return lax.broadcasted_iota(jnp.int32, (tq, tq), 0) < lax.broadcasted_iota(jnp.int32, (tq, tq), 1)


def _sb_fwd(proj, vt4, tq, shards):
    t = proj.shape[0]
    pairs = D_MODEL // SB_BLOCK
    nq = t // tq
    qi_tab, g_tab = _sb_schedule(nq)
    ns = qi_tab.shape[0]
    n = len(shards)

    def body(qi_ref, g_ref, q_ref, k_ref, vt_ref, *rest):
        o_ref, a_ref, z_ref = rest[n:n + 3]
        acc_ref, carry_ref, qt_ref = rest[2 * n + 3:2 * n + 6]
        begin, relay, finish = _gather_phases(rest[:n], rest[n + 3:2 * n + 3], *rest[2 * n + 6:])
        pi, si = pl.program_id(0), pl.program_id(1)
        diagonal = g_ref[si] == qi_ref[si]
        pl.when((pi == 0) & (si == 0))(begin)
        pl.when((pi == pairs // 2) & (si == 0))(relay)

        @pl.when(diagonal)
        def _():
            acc_ref[...] = jnp.zeros_like(acc_ref)
            carry_ref[...] = jnp.zeros_like(carry_ref)
            qt = q_ref[...].astype(F32).T * (SB_HEAD_DIM ** -0.5)
            for h in range(2):
                qt_ref[h] = _head_rows(qt, h)

        def step(past):
            for h in range(2):
                a, z, carry = _sb_probs(k_ref[...], qt_ref[h], carry_ref[h], past)
                ab = a.astype(BF16)
                a_ref[h] = ab
                z_ref[h] = z.astype(BF16)
                acc_ref[h] += _dot(vt_ref[h * SB_HEAD_DIM:(h + 1) * SB_HEAD_DIM, :], ab)
                carry_ref[h] = carry

        pl.when(diagonal)(lambda: step(_past_mask(tq)))
        pl.when(jnp.logical_not(diagonal))(lambda: step(None))

        @pl.when(g_ref[si] == 0)
        def _():
            o_ref[...] = jnp.concatenate([acc_ref[0], acc_ref[1]], axis=0).T

        pl.when((pi == pairs - 1) & (si == ns - 1))(finish)

    tile = pl.BlockSpec((None, None, 2, tq, tq), lambda p, s, qt_, gt_: (p, s, 0, 0, 0))
    tiles = jax.ShapeDtypeStruct((pairs, ns, 2, tq, tq), BF16)
    outs = pl.pallas_call(
        body, name="sb_fwd",
        grid_spec=pltpu.PrefetchScalarGridSpec(
            num_scalar_prefetch=2, grid=(pairs, ns),
            in_specs=[pl.BlockSpec((tq, SB_BLOCK), lambda p, s, qt_, gt_: (qt_[s], COL_SQ * pairs + p)),
                      pl.BlockSpec((tq, SB_BLOCK), lambda p, s, qt_, gt_: (gt_[s], COL_SK * pairs + p)),
                      pl.BlockSpec((None, None, SB_BLOCK, tq), lambda p, s, qt_, gt_: (p, gt_[s], 0, 0))] + [ANY] * n,
            out_specs=[pl.BlockSpec((tq, SB_BLOCK), lambda p, s, qt_, gt_: (qt_[s], p)), tile, tile] + [ANY] * n,
            scratch_shapes=[pltpu.VMEM((2, SB_HEAD_DIM, tq), F32), pltpu.VMEM((2, 1, tq), F32),
                            pltpu.VMEM((2, SB_BLOCK, tq), BF16)] + _gather_sems(n)),
        out_shape=[jax.ShapeDtypeStruct((t, D_MODEL), F32), tiles, tiles] + _gather_shapes(shards),
        compiler_params=pltpu.CompilerParams(dimension_semantics=("arbitrary", "arbitrary")),
    )(qi_tab, g_tab, proj, proj, vt4, *shards)
    return outs[0], outs[1], outs[2], outs[3:]


def _mem_prep(mem, g_mem, wkv_all, k_norm_g):
    m, d = mem.shape

    def body(mem_ref, g_ref, w_ref, kg_ref, memn_ref, kn_ref, v_ref):
        memf = mem_ref[...]
        memn = (memf * _rstd(memf) * g_ref[...]).astype(BF16)
        memn_ref[...] = memn
        for b in range(N_DEV):
            kv = _dot(memn, w_ref[b])
            if b < X_HEADS:
                kn_ref[:, b * X_HEAD_DIM:(b + 1) * X_HEAD_DIM] = (kv * _rstd(kv) * kg_ref[...]).astype(BF16)
            else:
                h = b - X_HEADS
                v_ref[:, h * X_HEAD_DIM:(h + 1) * X_HEAD_DIM] = kv.astype(BF16)

    return pl.pallas_call(
        body, name="mem_prep",
        out_shape=[jax.ShapeDtypeStruct((m, d), BF16)] * 3,
    )(mem, g_mem, wkv_all, k_norm_g)


def _x_head(xq_ref, qg, kn_ref, h):
    sl = slice(h * X_HEAD_DIM, (h + 1) * X_HEAD_DIM)
    q = xq_ref[:, sl].astype(F32)
    rq = _rstd(q)
    qhat = q * rq
    qn = (qhat * qg).astype(BF16)
    s = _dot_nt(qn, kn_ref[:, sl]) * (X_HEAD_DIM ** -0.5)
    e = jnp.exp(s - jnp.max(s, axis=-1, keepdims=True))
    p = e / jnp.sum(e, axis=-1, keepdims=True)
    return sl, rq, qhat, qn, p


def _x_fwd(proj, q_norm_g, kn, v, tm):
    t = proj.shape[0]
    m = kn.shape[0]

    def body(xq_ref, qg_ref, kn_ref, v_ref, o_ref):
        for h in range(X_HEADS):
            sl, _, _, _, p = _x_head(xq_ref, qg_ref[...], kn_ref, h)
            o_ref[:, sl] = _dot(p.astype(BF16), v_ref[:, sl]).astype(BF16)

    return pl.pallas_call(
        body, name="x_fwd",
        grid=(t // tm,),
        in_specs=[pl.BlockSpec((tm, D_MODEL), lambda i: (i, COL_XQ)),
                  pl.BlockSpec((1, X_HEAD_DIM), lambda i: (0, 0)),
                  pl.BlockSpec((m, D_MODEL), lambda i: (0, 0)),
                  pl.BlockSpec((m, D_MODEL), lambda i: (0, 0))],
        out_specs=pl.BlockSpec((tm, D_MODEL), lambda i: (i, 0)),
        out_shape=jax.ShapeDtypeStruct((t, D_MODEL), BF16),
        compiler_params=pltpu.CompilerParams(dimension_semantics=("parallel",)),
    )(proj, q_norm_g, kn, v)


def _gate_spec(tm, branch):
    return pl.BlockSpec((tm, D_MODEL), lambda i: (i, COL_GATE + branch))


def _merge_fwd(x, proj, a_conv, o_sb, o_x, w_conv_out, w_sb_out, w_x_out, w_out, tm):
    t, d = x.shape

    def body(x_ref, g0_ref, g1_ref, g2_ref, a_ref, s_ref, xo_ref, wc_ref, ws_ref, wx_ref, wo_ref,
             x1_ref, yc_ref, ys_ref, yx_ref, mg_ref):
        merged = jnp.zeros((tm, d), F32)
        for gate_ref, b_ref, w_ref, y_ref in ((g0_ref, a_ref, wc_ref, yc_ref), (g1_ref, s_ref, ws_ref, ys_ref),
                                              (g2_ref, xo_ref, wx_ref, yx_ref)):
            yv = _dot(b_ref[...].astype(BF16), w_ref[...])
            y_ref[...] = yv.astype(BF16)
            merged = merged + _sigmoid(gate_ref[...].astype(F32)) * yv
        mb = merged.astype(BF16)
        mg_ref[...] = mb
        x1_ref[...] = x_ref[...] + _dot(mb, wo_ref[...])

    tile = pl.BlockSpec((tm, d), lambda i: (i, 0))
    wfull = pl.BlockSpec((d, d), lambda i: (0, 0))
    return pl.pallas_call(
        body, name="merge_fwd",
        grid=(t // tm,),
        in_specs=[tile] + [_gate_spec(tm, b) for b in range(N_BRANCH)] + [tile, tile, tile,
                                                                           wfull, wfull, wfull, wfull],
        out_specs=[tile] * 5,
        out_shape=[jax.ShapeDtypeStruct((t, d), F32)] + [jax.ShapeDtypeStruct((t, d), BF16)] * 4,
        compiler_params=pltpu.CompilerParams(dimension_semantics=("parallel",)),
    )(x, proj, proj, proj, a_conv, o_sb, o_x, w_conv_out, w_sb_out, w_x_out, w_out)


def _mlp_fwd(x1, g_mlp, w_up_all, w_down, target, tm):
    t, d = x1.shape
    nb, _, fw = w_up_all.shape

    def body(x1_ref, g_ref, wu_ref, wd_ref, tgt_ref, up_ref, h2_ref, dx2_ref, lsum_ref, acc_ref):
        i, j = pl.program_id(0), pl.program_id(1)

        @pl.when(j == 0)
        def _():
            xf = x1_ref[...]
            h2_ref[...] = (xf * _rstd(xf) * g_ref[...]).astype(BF16)
            acc_ref[...] = jnp.zeros_like(acc_ref)

        @pl.when((i == 0) & (j == 0))
        def _():
            lsum_ref[...] = jnp.zeros_like(lsum_ref)

        up = _dot(h2_ref[...], wu_ref[...])
        up_ref[...] = up.astype(BF16)
        act = jnp.square(jnp.maximum(up, 0.0)).astype(BF16)
        acc_ref[...] += _dot(act, wd_ref[...])

        @pl.when(j == nb - 1)
        def _():
            diff = x1_ref[...] + acc_ref[...] - tgt_ref[...]
            dx2_ref[...] = diff * (1.0 / d)
            lsum_ref[...] += jnp.sum(diff * diff, axis=0, keepdims=True)

    tile = pl.BlockSpec((tm, d), lambda i, j: (i, 0))
    row = pl.BlockSpec((1, d), lambda i, j: (0, 0))
    return pl.pallas_call(
        body, name="mlp_fwd",
        grid=(t // tm, nb),
        in_specs=[tile, row, pl.BlockSpec((None, d, fw), lambda i, j: (j, 0, 0)),
                  pl.BlockSpec((fw, d), lambda i, j: (j, 0)), tile],
        out_specs=[pl.BlockSpec((tm, fw), lambda i, j: (i, j)), tile, tile, row],
        out_shape=[jax.ShapeDtypeStruct((t, nb * fw), BF16), jax.ShapeDtypeStruct((t, d), BF16),
                   jax.ShapeDtypeStruct((t, d), F32), jax.ShapeDtypeStruct((1, d), F32)],
        scratch_shapes=[pltpu.VMEM((tm, d), F32)],
        compiler_params=pltpu.CompilerParams(dimension_semantics=("arbitrary", "arbitrary")),
    )(x1, g_mlp, w_up_all, w_down, target)


def _mlp_bwd(x1, g_mlp, w_up_all, w_down, up, dx2, tm):
    t, d = x1.shape
    nb, _, fw = w_up_all.shape

    def body(x1_ref, g_ref, wu_ref, wd_ref, up_ref, dx2_ref, dup_ref, act_ref, dx1_ref, dg_ref, acc_ref, dyb_ref):
        i, j = pl.program_id(0), pl.program_id(1)

        @pl.when(j == 0)
        def _():
            dyb_ref[...] = dx2_ref[...].astype(BF16)
            acc_ref[...] = jnp.zeros_like(acc_ref)

        @pl.when((i == 0) & (j == 0))
        def _():
            dg_ref[...] = jnp.zeros_like(dg_ref)

        r = jnp.maximum(up_ref[...].astype(F32), 0.0)
        act_ref[...] = jnp.square(r).astype(BF16)
        dup = (_dot_nt(dyb_ref[...], wd_ref[...]) * (2.0 * r)).astype(BF16)
        dup_ref[...] = dup
        acc_ref[...] += _dot_nt(dup, wu_ref[...])

        @pl.when(j == nb - 1)
        def _():
            xf = x1_ref[...]
            rs = _rstd(xf)
            xhat = xf * rs
            dh2 = acc_ref[...]
            dg_ref[...] += jnp.sum(dh2 * xhat, axis=0, keepdims=True)
            dx1_ref[...] = dx2_ref[...] + _rms_bwd(dh2, xhat, rs, g_ref[...])

    tile = pl.BlockSpec((tm, d), lambda i, j: (i, 0))
    row = pl.BlockSpec((1, d), lambda i, j: (0, 0))
    ff = pl.BlockSpec((tm, fw), lambda i, j: (i, j))
    return pl.pallas_call(
        body, name="mlp_bwd",
        grid=(t // tm, nb),
        in_specs=[tile, row, pl.BlockSpec((None, d, fw), lambda i, j: (j, 0, 0)),
                  pl.BlockSpec((fw, d), lambda i, j: (j, 0)), ff, tile],
        out_specs=[ff, ff, tile, row],
        out_shape=[jax.ShapeDtypeStruct((t, nb * fw), BF16), jax.ShapeDtypeStruct((t, nb * fw), BF16),
                   jax.ShapeDtypeStruct((t, d), F32), jax.ShapeDtypeStruct((1, d), F32)],
        scratch_shapes=[pltpu.VMEM((tm, d), F32), pltpu.VMEM((tm, d), BF16)],
        compiler_params=pltpu.CompilerParams(dimension_semantics=("arbitrary", "arbitrary")),
    )(x1, g_mlp, w_up_all, w_down, up, dx2)


def _merge_bwd(dx1, proj, y_conv, y_sb, y_x, w_conv_out, w_sb_out, w_x_out, w_out, tm):
    t, d = dx1.shape

    def body(dx1_ref, g0_ref, g1_ref, g2_ref, yc_ref, ys_ref, yx_ref, wc_ref, ws_ref, wx_ref, wo_ref,
             dgate_ref, dyc_ref, dys_ref, dyx_ref, da_ref, dos_ref, dox_ref):
        dm = _dot_nt(dx1_ref[...].astype(BF16), wo_ref[...])
        for i, (gate_ref, y_ref, w_ref, dy_ref, db_ref) in enumerate(((g0_ref, yc_ref, wc_ref, dyc_ref, da_ref),
                                                                       (g1_ref, ys_ref, ws_ref, dys_ref, dos_ref),
                                                                       (g2_ref, yx_ref, wx_ref, dyx_ref, dox_ref))):
            gt = _sigmoid(gate_ref[...].astype(F32))
            dy = (dm * gt).astype(BF16)
            dy_ref[...] = dy
            dgate_ref[:, i * d:(i + 1) * d] = (dm * y_ref[...].astype(F32) * gt * (1.0 - gt)).astype(BF16)
            db_ref[...] = _dot_nt(dy, w_ref[...]).astype(BF16)

    tile = pl.BlockSpec((tm, d), lambda i: (i, 0))
    wfull = pl.BlockSpec((d, d), lambda i: (0, 0))
    return pl.pallas_call(
        body, name="merge_bwd",
        grid=(t // tm,),
        in_specs=[tile] + [_gate_spec(tm, b) for b in range(N_BRANCH)] + [tile, tile, tile,
                                                                           wfull, wfull, wfull, wfull],
        out_specs=[pl.BlockSpec((tm, N_BRANCH * d), lambda i: (i, 0))] + [tile] * 6,
        out_shape=[jax.ShapeDtypeStruct((t, N_BRANCH * d), BF16)] + [jax.ShapeDtypeStruct((t, d), BF16)] * 6,
        compiler_params=pltpu.CompilerParams(dimension_semantics=("parallel",)),
    )(dx1, proj, proj, proj, y_conv, y_sb, y_x, w_conv_out, w_sb_out, w_x_out, w_out)


def _conv_bwd(proj, conv_w, da, cw):
    t = proj.shape[0]
    nper = D_MODEL // cw

    def body(ch_ref, cb_ref, cc_ref, w_ref, da_ref, dch_ref, dcb_ref, dcc_ref, dw_ref):
        ch, cb, cc, u, u1, u2, cv, w, row = _conv_terms(ch_ref, cb_ref, cc_ref, w_ref)
        dav = da_ref[...].astype(F32)
        dcb_ref[...] = (dav * cv).astype(BF16)
        dcv = dav * cb
        n1 = jnp.where(row < t - 1, pltpu.roll(dcv, t - 1, 0), 0.0)
        n2 = jnp.where(row < t - 2, pltpu.roll(dcv, t - 2, 0), 0.0)
        du = w[2] * dcv + w[1] * n1 + w[0] * n2
        dcc_ref[...] = (du * ch).astype(BF16)
        dch_ref[...] = (du * cc).astype(BF16)
        dw_ref[0:1, :] = jnp.sum(dcv * u2, axis=0, keepdims=True)
        dw_ref[1:2, :] = jnp.sum(dcv * u1, axis=0, keepdims=True)
        dw_ref[2:3, :] = jnp.sum(dcv * u, axis=0, keepdims=True)

    def col(piece):
        return pl.BlockSpec((t, cw), lambda j: (0, piece * nper + j))

    out_col = pl.BlockSpec((t, cw), lambda j: (0, j))
    wspec = pl.BlockSpec((3, cw), lambda j: (0, j))
    return pl.pallas_call(
        body, name="conv_bwd",
        grid=(nper,),
        in_specs=[col(COL_CH), col(COL_CB), col(COL_CC), wspec, out_col],
        out_specs=[out_col, out_col, out_col, wspec],
        out_shape=[jax.ShapeDtypeStruct((t, D_MODEL), BF16)] * 3 + [jax.ShapeDtypeStruct((3, D_MODEL), F32)],
        compiler_params=pltpu.CompilerParams(dimension_semantics=("parallel",)),
    )(proj, proj, proj, conv_w, da)


def _sb_bwd(proj, kt4, do_sb, o_sb, weights, logits, tq, pair_sums):
    t = proj.shape[0]
    nq = t // tq
    pairs = D_MODEL // SB_BLOCK
    scale = SB_HEAD_DIM ** -0.5
    qi_tab, g_tab = _sb_schedule(nq)
    ns = qi_tab.shape[0]
    n = len(pair_sums)

    def body(qi_ref, g_ref, q_ref, v_ref, kt_ref, do_ref, o_ref, a_ref, z_ref, *rest):
        dq_ref, dk_ref, dv_ref = rest[n:n + 3]
        dk_acc, dv_acc, dqt_ref, carry_ref, qm_ref, dom_ref, dot_ref, dsum_ref = rest[2 * n + 3:2 * n + 11]
        begin, finish = _chip_exchange_phases(rest[:n], rest[n + 3:2 * n + 3], *rest[2 * n + 11:])
        pi, si = pl.program_id(0), pl.program_id(1)
        diagonal = g_ref[si] == qi_ref[si]
        pl.when((pi == 0) & (si == 0))(begin)

        @pl.when(si == 0)
        def _():
            dk_acc[...] = jnp.zeros_like(dk_acc)
            dv_acc[...] = jnp.zeros_like(dv_acc)

        @pl.when(diagonal)
        def _():
            dqt_ref[...] = jnp.zeros_like(dqt_ref)
            carry_ref[...] = jnp.zeros_like(carry_ref)
            q = q_ref[...].astype(F32) * scale
            do = do_ref[...].astype(F32)
            dot_ = do.T
            prod = dot_ * o_ref[...].T
            for h in range(2):
                rows = slice(h * SB_HEAD_DIM, (h + 1) * SB_HEAD_DIM)
                qm_ref[h] = _head_lanes(q, h)
                dom_ref[h] = _head_lanes(do, h)
                dot_ref[h] = _head_rows(dot_, h)
                dsum_ref[h] = jnp.sum(prod[rows, :], axis=0, keepdims=True)

        def step(past):
            u_mat = _scan_matrix(False, 1.0)
            ks = pl.multiple_of(g_ref[si] * tq, tq)
            dk_add = jnp.zeros((tq, SB_BLOCK), F32)
            dv_add = jnp.zeros((tq, SB_BLOCK), F32)
            for h in range(2):
                rows = slice(h * SB_HEAD_DIM, (h + 1) * SB_HEAD_DIM)
                ab = a_ref[h]
                gw = _dot(v_ref[...], dot_ref[h]) * ab.astype(F32)
                after, carry = _group_suffix(u_mat, gw, carry_ref[h])
                sig = pl.reciprocal(1.0 + jnp.exp(-z_ref[h].astype(F32)), approx=True)
                dz = gw - sig * (dsum_ref[h] - after)
                if past is not None:
                    dz = jnp.where(past, dz, 0.0)
                dzb = dz.astype(BF16)
                dqt_ref[h] += _dot(kt_ref[rows, :], dzb)
                dk_add = dk_add + _dot(dzb, qm_ref[h])
                dv_add = dv_add + _dot(ab, dom_ref[h])
                carry_ref[h] = carry
            dk_acc[pl.ds(ks, tq), :] += dk_add
            dv_acc[pl.ds(ks, tq), :] += dv_add

        pl.when(diagonal)(lambda: step(_past_mask(tq)))
        pl.when(jnp.logical_not(diagonal))(lambda: step(None))

        @pl.when(g_ref[si] == 0)
        def _():
            dq_ref[...] = (jnp.concatenate([dqt_ref[0], dqt_ref[1]], axis=0).T * scale).astype(BF16)

        @pl.when(si == ns - 1)
        def _():
            dk_ref[...] = dk_acc[...].astype(BF16)
            dv_ref[...] = dv_acc[...].astype(BF16)

        pl.when((pi == pairs - 1) & (si == ns - 1))(finish)

    qblk = lambda base: pl.BlockSpec((tq, SB_BLOCK), lambda p, s, qt_, gt_: (qt_[s], base * pairs + p))
    kgrp = lambda base: pl.BlockSpec((tq, SB_BLOCK), lambda p, s, qt_, gt_: (gt_[s], base * pairs + p))
    seq = pl.BlockSpec((t, SB_BLOCK), lambda p, s, qt_, gt_: (0, p))
    tr = pl.BlockSpec((None, None, SB_BLOCK, tq), lambda p, s, qt_, gt_: (p, gt_[s], 0, 0))
    tile = pl.BlockSpec((None, None, 2, tq, tq), lambda p, s, qt_, gt_: (p, s, 0, 0, 0))
    outs = pl.pallas_call(
        body, name="sb_bwd",
        grid_spec=pltpu.PrefetchScalarGridSpec(
            num_scalar_prefetch=2, grid=(pairs, ns),
            in_specs=[qblk(COL_SQ), kgrp(COL_SV), tr, qblk(0), qblk(0), tile, tile] + [ANY] * n,
            out_specs=[qblk(0), seq, seq] + [ANY] * n,
            scratch_shapes=[pltpu.VMEM((t, SB_BLOCK), F32), pltpu.VMEM((t, SB_BLOCK), F32),
                            pltpu.VMEM((2, SB_HEAD_DIM, tq), F32), pltpu.VMEM((2, 1, tq), F32),
                            pltpu.VMEM((2, tq, SB_BLOCK), BF16), pltpu.VMEM((2, tq, SB_BLOCK), BF16),
                            pltpu.VMEM((2, SB_BLOCK, tq), BF16), pltpu.VMEM((2, 1, tq), F32)] + _chip_exchange_sems(n)),
        out_shape=[jax.ShapeDtypeStruct((t, D_MODEL), BF16)] * 3 + _chip_exchange_shapes(pair_sums),
        compiler_params=pltpu.CompilerParams(dimension_semantics=("arbitrary", "arbitrary")),
    )(qi_tab, g_tab, proj, proj, kt4, do_sb, o_sb, weights, logits, *pair_sums)
    return outs[0], outs[1], outs[2], outs[3:]


def _x_bwd(proj, q_norm_g, kn, v, do_x, tm, grads):
    t = proj.shape[0]
    m = kn.shape[0]
    scale = X_HEAD_DIM ** -0.5
    nt = t // tm
    n = len(grads)

    def body(xq_ref, qg_ref, kn_ref, v_ref, do_ref, *rest):
        dxq_ref, dkn_ref, dv_ref, dqg_ref = rest[n:n + 4]
        begin, finish = _pair_exchange_phases(rest[:n], rest[n + 4:2 * n + 4], *rest[2 * n + 4:])

        @pl.when(pl.program_id(0) == 0)
        def _():
            begin()
            dkn_ref[...] = jnp.zeros_like(dkn_ref)
            dv_ref[...] = jnp.zeros_like(dv_ref)
            dqg_ref[...] = jnp.zeros_like(dqg_ref)

        qg = qg_ref[...]
        for h in range(X_HEADS):
            sl, rq, qhat, qn, p = _x_head(xq_ref, qg, kn_ref, h)
            do_h = do_ref[:, sl]
            dp = _dot_nt(do_h, v_ref[:, sl])
            ds = (p * (dp - jnp.sum(dp * p, axis=-1, keepdims=True)) * scale).astype(BF16)
            dqn = _dot(ds, kn_ref[:, sl])
            dkn_ref[:, sl] += _dot_tn(ds, qn)
            dv_ref[:, sl] += _dot_tn(p.astype(BF16), do_h)
            dqg_ref[...] += jnp.sum(dqn * qhat, axis=0, keepdims=True)
            dxq_ref[:, sl] = _rms_bwd(dqn, qhat, rq, qg).astype(BF16)

        pl.when(pl.program_id(0) == nt - 1)(finish)

    full = pl.BlockSpec((m, D_MODEL), lambda i: (0, 0))
    gain = pl.BlockSpec((1, X_HEAD_DIM), lambda i: (0, 0))
    tile = pl.BlockSpec((tm, D_MODEL), lambda i: (i, 0))
    outs = pl.pallas_call(
        body, name="x_bwd",
        grid=(nt,),
        in_specs=[pl.BlockSpec((tm, D_MODEL), lambda i: (i, COL_XQ)), gain, full, full, tile] + [ANY] * n,
        out_specs=[tile, full, full, gain] + [ANY] * n,
        out_shape=[jax.ShapeDtypeStruct((t, D_MODEL), BF16), jax.ShapeDtypeStruct((m, D_MODEL), F32),
                   jax.ShapeDtypeStruct((m, D_MODEL), F32), jax.ShapeDtypeStruct((1, X_HEAD_DIM), F32)]
        + _pair_exchange_shapes(grads),
        scratch_shapes=_pair_exchange_sems(n),
        compiler_params=pltpu.CompilerParams(dimension_semantics=("arbitrary",)),
    )(proj, q_norm_g, kn, v, do_x, *grads)
    return outs[0], outs[1], outs[2], outs[3], outs[4:]


def _mem_bwd(mem, g_mem, wkv_all, k_norm_g, dkn, dv):
    m, d = mem.shape

    def body(mem_ref, g_ref, w_ref, kg_ref, dkn_ref, dv_ref, dkv_ref, dgm_ref, dkg_ref):
        memf = mem_ref[...]
        mem_hat = memf * _rstd(memf)
        memn = (mem_hat * g_ref[...]).astype(BF16)
        kg = kg_ref[...]
        dmemn = jnp.zeros((m, d), F32)
        dkg = jnp.zeros((1, X_HEAD_DIM), F32)
        for b in range(N_DEV):
            sl = slice(b * X_HEAD_DIM, (b + 1) * X_HEAD_DIM)
            if b < X_HEADS:
                kv = _dot(memn, w_ref[b])
                rk = _rstd(kv)
                khat = kv * rk
                dkn_h = dkn_ref[:, sl]
                dkg = dkg + jnp.sum(dkn_h * khat, axis=0, keepdims=True)
                dblk = _rms_bwd(dkn_h, khat, rk, kg).astype(BF16)
            else:
                hs = slice((b - X_HEADS) * X_HEAD_DIM, (b - X_HEADS + 1) * X_HEAD_DIM)
                dblk = dv_ref[:, hs].astype(BF16)
            dkv_ref[:, sl] = dblk
            dmemn = dmemn + _dot_nt(dblk, w_ref[b])
        dgm_ref[...] = jnp.sum(dmemn * mem_hat, axis=0, keepdims=True)
        dkg_ref[...] = dkg

    return pl.pallas_call(
        body, name="mem_bwd",
        out_shape=[jax.ShapeDtypeStruct((m, 2 * d), BF16), jax.ShapeDtypeStruct((1, d), F32),
                   jax.ShapeDtypeStruct((1, X_HEAD_DIM), F32)],
    )(mem, g_mem, wkv_all, k_norm_g, dkn, dv)


def _in_proj_bwd(x, g_mix, w_in_all, dproj, dx1, tm, pair_sums):
    t, d = x.shape
    nb, _, bw = w_in_all.shape
    nt = t // tm
    n = len(pair_sums)

    def body(x_ref, g_ref, w_ref, dp_ref, dx1_ref, *rest):
        dx_ref, dg_ref = rest[n:n + 2]
        acc_ref = rest[2 * n + 2]
        begin, finish = _chip_exchange_phases(rest[:n], rest[n + 2:2 * n + 2], *rest[2 * n + 3:])
        i, j = pl.program_id(0), pl.program_id(1)
        pl.when((i == 0) & (j == 0))(begin)

        @pl.when(j == 0)
        def _():
            acc_ref[...] = jnp.zeros_like(acc_ref)

        @pl.when((i == 0) & (j == 0))
        def _():
            dg_ref[...] = jnp.zeros_like(dg_ref)

        acc_ref[...] += _dot_nt(dp_ref[...], w_ref[...])

        @pl.when(j == nb - 1)
        def _():
            xf = x_ref[...]
            rs = _rstd(xf)
            xhat = xf * rs
            dh = acc_ref[...]
            dg_ref[...] += jnp.sum(dh * xhat, axis=0, keepdims=True)
            dx_ref[...] = dx1_ref[...] + _rms_bwd(dh, xhat, rs, g_ref[...])

        pl.when((i == nt - 1) & (j == nb - 1))(finish)

    tile = pl.BlockSpec((tm, d), lambda i, j: (i, 0))
    row = pl.BlockSpec((1, d), lambda i, j: (0, 0))
    outs = pl.pallas_call(
        body, name="in_proj_bwd",
        grid=(nt, nb),
        in_specs=[tile, row, pl.BlockSpec((None, d, bw), lambda i, j: (j, 0, 0)),
                  pl.BlockSpec((tm, bw), lambda i, j: (i, j)), tile] + [ANY] * n,
        out_specs=[tile, row] + [ANY] * n,
        out_shape=[jax.ShapeDtypeStruct((t, d), F32), jax.ShapeDtypeStruct((1, d), F32)] + _chip_exchange_shapes(pair_sums),
        scratch_shapes=[pltpu.VMEM((tm, d), F32)] + _chip_exchange_sems(n),
        compiler_params=pltpu.CompilerParams(dimension_semantics=("arbitrary", "arbitrary")),
    )(x, g_mix, w_in_all, dproj, dx1, *pair_sums)
    return outs[0], outs[1], outs[2:]


def _weight_grad(a, b, bw, tmm, name):
    t, m = a.shape
    n = b.shape[1]
    tmm = min(tmm, m)

    def body(a_ref, b_ref, o_ref):
        o_ref[...] = _dot_tn(a_ref[...].astype(BF16), b_ref[...].astype(BF16)).astype(BF16)

    return pl.pallas_call(
        body, name=name,
        grid=(m // tmm, n // bw),
        in_specs=[pl.BlockSpec((t, tmm), lambda i, j: (0, i)), pl.BlockSpec((t, bw), lambda i, j: (0, j))],
        out_specs=pl.BlockSpec((None, tmm, bw), lambda i, j: (j, i, 0)),
        out_shape=jax.ShapeDtypeStruct((n // bw, m, bw), BF16),
        compiler_params=pltpu.CompilerParams(dimension_semantics=("parallel", "parallel")),
    )(a, b)


def _weight_grad_blocks(a, b, bw, blocks, tmm, name, send=()):
    t, m = a.shape
    tmm = min(tmm, m)
    nm = m // tmm
    n = len(send)

    def body(blocks_ref, a_ref, b_ref, *rest):
        o_ref = rest[n]
        i, j = pl.program_id(0), pl.program_id(1)
        if n:
            begin, finish = _pair_exchange_phases(rest[:n], rest[n + 1:2 * n + 1], *rest[2 * n + 1:], by_slot=True)
            pl.when((i == 0) & (j == 0))(begin)
        o_ref[...] = _dot_tn(a_ref[...].astype(BF16), b_ref[...].astype(BF16)).astype(BF16)
        if n:
            pl.when((i == nm - 1) & (j == 3))(finish)

    outs = pl.pallas_call(
        body, name=name,
        grid_spec=pltpu.PrefetchScalarGridSpec(
            num_scalar_prefetch=1, grid=(nm, 4),
            in_specs=[pl.BlockSpec((t, tmm), lambda i, j, blk: (0, i)),
                      pl.BlockSpec((t, bw), lambda i, j, blk: (0, blk[j]))] + [ANY] * n,
            out_specs=[pl.BlockSpec((None, tmm, bw), lambda i, j, blk: (j, i, 0))] + [ANY] * n,
            scratch_shapes=_pair_exchange_sems(n) if n else []),
        out_shape=[jax.ShapeDtypeStruct((4, m, bw), BF16)] + _pair_exchange_shapes(send),
        compiler_params=pltpu.CompilerParams(dimension_semantics=("arbitrary", "arbitrary")),
    )(blocks, a, b, *send)
    return outs[0], outs[1:]


def _pair_sum(grads, recvs, own_blocks, name):
    k = len(grads)
    _, rows, cols = grads[0].shape

    def body(idx_ref, *refs):
        for a in range(k):
            refs[2 * k + a][...] = (refs[a][...].astype(F32) + refs[k + a][...].astype(F32)).astype(BF16)

    slot = pl.BlockSpec((None, rows, cols), lambda r, idx: (r, 0, 0))
    return pl.pallas_call(
        body, name=name,
        grid_spec=pltpu.PrefetchScalarGridSpec(
            num_scalar_prefetch=1, grid=(4,),
            in_specs=[pl.BlockSpec((None, rows, cols), lambda r, idx: (idx[r], 0, 0))] * k + [slot] * k,
            out_specs=[slot] * k),
        out_shape=[jax.ShapeDtypeStruct((4, rows, cols), BF16)] * k,
        compiler_params=pltpu.CompilerParams(dimension_semantics=("parallel",)),
    )(own_blocks, *grads, *recvs)


def _adamw_math(w, g, m, v):
    m = ADAM_B1 * m + (1.0 - ADAM_B1) * g
    v = ADAM_B2 * v + (1.0 - ADAM_B2) * jnp.square(g)
    m_hat = m / (1.0 - ADAM_B1 ** ADAM_STEP)
    v_hat = v / (1.0 - ADAM_B2 ** ADAM_STEP)
    delta = -ADAM_LR * (m_hat / (jnp.sqrt(v_hat) + ADAM_EPS) + ADAM_WD * w)
    return delta, m, v


def _adamw_sharded(shards, tr, name):
    k = len(shards)
    rows, cols = shards[0][2].shape
    tr = min(tr, rows)

    def body(*refs):
        for a in range(k):
            h_ref, r_ref, w_ref, m_ref, v_ref = refs[5 * a:5 * a + 5]
            g_out, d_out, m_out, v_out = refs[5 * k + 4 * a:5 * k + 4 * a + 4]
            g = h_ref[...].astype(F32)
            for r in range(3):
                g = g + r_ref[r].astype(F32)
            g_out[...] = g
            d_out[...], m_out[...], v_out[...] = _adamw_math(w_ref[...], g, m_ref[...], v_ref[...])

    tile = pl.BlockSpec((tr, cols), lambda i: (i, 0))
    outs = pl.pallas_call(
        body, name=name,
        grid=(rows // tr,),
        in_specs=[pl.BlockSpec((None, tr, cols), lambda i: (0, i, 0)),
                  pl.BlockSpec((3, tr, cols), lambda i: (0, i, 0)), tile, tile, tile] * k,
        out_specs=[tile] * (4 * k),
        out_shape=[jax.ShapeDtypeStruct((rows, cols), F32)] * (4 * k),
        compiler_params=pltpu.CompilerParams(dimension_semantics=("parallel",)),
    )(*[op for shard in shards for op in shard])
    return [tuple(outs[4 * a:4 * a + 4]) for a in range(k)]


SMALL_ROWS = 16


def _pack_rows(dg_mix, dg_mem, dg_mlp, dqg, dkg, dconv, lsum):
    def body(a_ref, b_ref, c_ref, q_ref, k_ref, cv_ref, l_ref, o_ref):
        o_ref[...] = jnp.zeros_like(o_ref)
        for r, ref in enumerate((a_ref, b_ref, c_ref)):
            o_ref[r:r + 1, :] = ref[...]
        o_ref[3:4, :X_HEAD_DIM] = q_ref[...]
        o_ref[4:5, :X_HEAD_DIM] = k_ref[...]
        o_ref[5:8, :] = cv_ref[...]
        o_ref[8:9, :] = l_ref[...]

    return pl.pallas_call(body, name="small_pack", out_shape=jax.ShapeDtypeStruct((SMALL_ROWS, D_MODEL), F32))(
        dg_mix, dg_mem, dg_mlp, dqg, dkg, dconv, lsum)


def _small_sum(gathered):
    def body(g_ref, o_ref):
        total = g_ref[0]
        for dev in range(1, N_DEV):
            total = total + g_ref[dev]
        o_ref[...] = jnp.zeros_like(o_ref)
        for piece in range(5):
            o_ref[piece * SMALL_TILE:piece * SMALL_TILE + 1, :] = total[piece:piece + 1]
        o_ref[5 * SMALL_TILE:5 * SMALL_TILE + 3, :] = total[5:8]
        o_ref[6 * SMALL_TILE:6 * SMALL_TILE + 1, :] = total[8:9]

    return pl.pallas_call(body, name="small_grad_sum",
                          out_shape=jax.ShapeDtypeStruct((7 * SMALL_TILE, D_MODEL), F32))(gathered)


def _adamw_small(w, g, m, v):
    def body(w_ref, g_ref, m_ref, v_ref, d_out, m_out, v_out):
        d_out[...], m_out[...], v_out[...] = _adamw_math(w_ref[...], g_ref[...], m_ref[...], v_ref[...])

    return pl.pallas_call(body, name="adamw_small", out_shape=[jax.ShapeDtypeStruct(w.shape, F32)] * 3)(w, g, m, v)


def _pad_tile(a):
    return jnp.pad(a, ((0, SMALL_TILE - a.shape[0]), (0, D_MODEL - a.shape[1])))


def _pack_small(*pieces):
    return jnp.concatenate([_pad_tile(a) for a in pieces], axis=0)


def kernel(x, mem, g_mix, g_mem, w_in, conv_w, w_conv_out, w_sb_out, q_norm_g, k_norm_g, w_mem_kv, w_x_out, w_out, g_mlp, w_up, w_down, loss_target, m_g_mix, m_g_mem, m_w_in, m_conv_w, m_w_conv_out, m_w_sb_out, m_q_norm_g, m_k_norm_g, m_w_mem_kv, m_w_x_out, m_w_out, m_g_mlp, m_w_up, m_w_down, v_g_mix, v_g_mem, v_w_in, v_conv_w, v_w_conv_out, v_w_sb_out, v_q_norm_g, v_k_norm_g, v_w_mem_kv, v_w_x_out, v_w_out, v_g_mlp, v_w_up, v_w_down):
    xpos, ypos, cpos = _mesh_pos()
    me = 4 * xpos + 2 * ypos + cpos
    x2d, mem2d, tgt2d = x[0], mem[0], loss_target[0]
    t = x2d.shape[0]
    tm = min(512, t)
    tm_s = min(256, t)

    big = {
        "w_in": (w_in[0], m_w_in[0], v_w_in[0]),
        "w_conv_out": (w_conv_out[0], m_w_conv_out[0], v_w_conv_out[0]),
        "w_sb_out": (w_sb_out[0], m_w_sb_out[0], v_w_sb_out[0]),
        "w_mem_kv": (w_mem_kv[0], m_w_mem_kv[0], v_w_mem_kv[0]),
        "w_x_out": (w_x_out[0], m_w_x_out[0], v_w_x_out[0]),
        "w_out": (w_out[0], m_w_out[0], v_w_out[0]),
        "w_up": (w_up[0], m_w_up[0], v_w_up[0]),
        "w_down": (w_down[0], m_w_down[0], v_w_down[0]),
    }
    late = [n for n in big if n != "w_in"]
    as_bf16 = lambda group: [big[n][0].astype(BF16) for n in group]
    conv_pad = jnp.pad(conv_w[0], ((0, 8 - 3), (0, 0)))

    proj, h, (w_in_all, conv_all) = _in_proj(x2d, g_mix, _arrival_blocks(xpos, ypos, cpos), tm,
                                             [big["w_in"][0].astype(BF16), conv_pad])
    conv_full = conv_all[:, :3, :].transpose(1, 0, 2).reshape(3, D_MODEL)
    a_conv = _conv_fwd(proj, conv_full, 256)
    tq = min(SB_QUERY_TILE, t)
    pairs = D_MODEL // SB_BLOCK

    def groups_t(cols):
        return cols.reshape(t // tq, tq, pairs, SB_BLOCK).transpose(2, 0, 3, 1)

    kt4 = groups_t(proj[:, COL_SK * D_MODEL:(COL_SK + 1) * D_MODEL])
    vt4 = groups_t(proj[:, COL_SV * D_MODEL:(COL_SV + 1) * D_MODEL])
    o_sb, sb_weights, sb_logits, gathered = _sb_fwd(proj, vt4, tq, as_bf16(late))
    full = dict(zip(late, gathered))
    wkv_all, w_up_all = full["w_mem_kv"], full["w_up"]
    rows_full = lambda a: a.reshape(a.shape[0] * a.shape[1], a.shape[2])
    wc, ws, wx, wo, wd = (rows_full(full[n]) for n in ("w_conv_out", "w_sb_out", "w_x_out", "w_out", "w_down"))
    mem_n, kn, vmem = _mem_prep(mem2d, g_mem, wkv_all, k_norm_g)
    o_x = _x_fwd(proj, q_norm_g, kn, vmem, tm_s)
    x1, y_conv, y_sb, y_x, merged = _merge_fwd(x2d, proj, a_conv, o_sb, o_x, wc, ws, wx, wo, tm_s)
    up, h2, dx2, lsum = _mlp_fwd(x1, g_mlp, w_up_all, wd, tgt2d, tm)

    dup, act, dx1, dg_mlp = _mlp_bwd(x1, g_mlp, w_up_all, wd, up, dx2, tm)
    dgate, dy_conv, dy_sb, dy_x, da_conv, do_sb, do_x = _merge_bwd(dx1, proj, y_conv, y_sb, y_x, wc, ws, wx, wo, tm_s)
    dch, dcb, dcc, dconv = _conv_bwd(proj, conv_full, da_conv, 256)
    wgrads = {
        "w_conv_out": _weight_grad(a_conv, dy_conv, D_MODEL, 512, "dw_conv_out"),
        "w_sb_out": _weight_grad(o_sb, dy_sb, D_MODEL, 512, "dw_sb_out"),
        "w_x_out": _weight_grad(o_x, dy_x, D_MODEL, 512, "dw_x_out"),
        "w_out": _weight_grad(merged, dx1, D_MODEL, 512, "dw_out"),
        "w_up": _weight_grad(h2, dup, w_up_all.shape[2], 512, "dw_up"),
        "w_down": _weight_grad(act, dx2, D_MODEL, 512, "dw_down"),
    }

    own_blocks = jnp.stack([4 * (xpos ^ dx) + 2 * (ypos ^ dy) + cpos for dx in (0, 1) for dy in (0, 1)]).astype(jnp.int32)
    sibling_blocks = own_blocks + (1 - 2 * cpos)
    blocked = lambda n: wgrads[n].reshape((N_DEV,) + big[n][0].shape)
    same_shape = ["w_conv_out", "w_sb_out", "w_x_out", "w_out"]

    def pair_sum(group, from_sibling):
        return dict(zip(group, _pair_sum([blocked(n) for n in group], from_sibling, own_blocks, "pair_sum_" + group[0])))

    behind_x = same_shape + ["w_up", "w_down"]
    dxq, dkn, dvm, dqg, from_sibling = _x_bwd(proj, q_norm_g, kn, vmem, do_x, tm_s, [blocked(n) for n in behind_x])
    pair_sums = pair_sum(same_shape, from_sibling[:4])
    pair_sums.update(pair_sum(["w_up"], from_sibling[4:5]))
    pair_sums.update(pair_sum(["w_down"], from_sibling[5:6]))
    dkv, dg_mem, dkg = _mem_bwd(mem2d, g_mem, wkv_all, k_norm_g, dkn, dvm)
    wgrads["w_mem_kv"] = _weight_grad(mem_n, dkv, wkv_all.shape[2], 512, "dw_mem_kv")
    pair_sums.update(pair_sum(["w_mem_kv"], _pair_exchange([blocked("w_mem_kv")], "grad_pair_exchange_w_mem_kv")))
    dq, dk, dv, from_chips_late = _sb_bwd(proj, kt4, do_sb, o_sb, sb_weights, sb_logits, tq,
                                          [pair_sums[n] for n in late])
    from_chips = dict(zip(late, from_chips_late))
    dproj = jnp.concatenate([dch, dcb, dcc, dq, dk, dv, dxq, dgate], axis=1)
    bw_in = w_in_all.shape[2]
    dw_in_sibling, _ = _weight_grad_blocks(h, dproj, bw_in, sibling_blocks, 512, "dw_in_sibling")
    dw_in_own, (dw_in_recv,) = _weight_grad_blocks(h, dproj, bw_in, own_blocks, 512, "dw_in_own", send=[dw_in_sibling])
    pair_sums["w_in"], = _pair_sum([dw_in_own], [dw_in_recv], jnp.arange(4, dtype=jnp.int32), "pair_sum_w_in")
    grad_x, dg_mix, (from_chips["w_in"],) = _in_proj_bwd(x2d, g_mix, w_in_all, dproj, dx1, tm, [pair_sums["w_in"]])
    res = {}
    for group in [same_shape] + [[n] for n in big if n not in same_shape]:
        updates = _adamw_sharded([(pair_sums[n], from_chips[n]) + big[n] for n in group], 256, "adamw_" + group[0])
        res.update(zip(group, updates))

    part = _pack_rows(dg_mix, dg_mem, dg_mlp, dqg, dkg, dconv, lsum)
    gsum = _small_sum(_small_all_gather(part))
    loss = 0.5 * jnp.sum(gsum[6 * SMALL_TILE]) / D_MODEL
    conv_cols = lax.dynamic_slice(gsum[5 * SMALL_TILE:6 * SMALL_TILE], (0, me * (D_MODEL // N_DEV)),
                                  (SMALL_TILE, D_MODEL // N_DEV))
    g_small = jnp.concatenate([gsum[:5 * SMALL_TILE], _pad_tile(conv_cols)], axis=0)
    w_small = _pack_small(g_mix, g_mem, g_mlp, q_norm_g, k_norm_g, conv_w[0])
    m_small = _pack_small(m_g_mix, m_g_mem, m_g_mlp, m_q_norm_g, m_k_norm_g, m_conv_w[0])
    v_small = _pack_small(v_g_mix, v_g_mem, v_g_mlp, v_q_norm_g, v_k_norm_g, v_conv_w[0])
    d_small, nm_small, nv_small = _adamw_small(w_small, g_small, m_small, v_small)

    def unpack(p):
        return {"g_mix": p[0:1], "g_mem": p[8:9], "g_mlp": p[16:17], "q_norm_g": p[24:25, :X_HEAD_DIM],
                "k_norm_g": p[32:33, :X_HEAD_DIM], "conv_w": p[40:43, :D_MODEL // N_DEV][None]}

    small = [unpack(p) for p in (g_small, d_small, nm_small, nv_small)]
    order = ["g_mix", "g_mem", "w_in", "conv_w", "w_conv_out", "w_sb_out", "q_norm_g", "k_norm_g", "w_mem_kv",
             "w_x_out", "w_out", "g_mlp", "w_up", "w_down"]
    outs = [loss, grad_x[None]]
    for kind in range(4):
        for n in order:
            outs.append(res[n][kind][None] if n in res else small[kind][n])
    return tuple(outs)
```

```python
import jax
import jax.numpy as jnp
from jax import lax
from jax.experimental import pallas as pl
from jax.experimental.pallas import tpu as pltpu

F32 = jnp.float32
BF16 = jnp.bfloat16
MESH = pl.DeviceIdType.MESH

EPS = 1e-6
N_DEV = 8
D_MODEL = 1024
SB_HEAD_DIM = 64
SB_BLOCK = 128
SB_QUERY_TILE = 512
X_HEADS = 4
X_HEAD_DIM = 256
N_BRANCH = 3
COL_CH, COL_CB, COL_CC, COL_SQ, COL_SK, COL_SV, COL_XQ, COL_GATE = 0, 1, 2, 3, 4, 5, 6, 7

ADAM_LR = 0.001
ADAM_B1 = 0.9
ADAM_B2 = 0.999
ADAM_EPS = 1e-08
ADAM_WD = 0.01
ADAM_STEP = 10

SMALL_TILE = 8


def _dot(a, b):
    return jnp.dot(a, b, preferred_element_type=F32)


def _dot_nt(a, b):
    return lax.dot_general(a, b, (((1,), (1,)), ((), ())), preferred_element_type=F32)


def _dot_tn(a, b):
    return lax.dot_general(a, b, (((0,), (0,)), ((), ())), preferred_element_type=F32)


def _rstd(xf):
    return lax.rsqrt(jnp.mean(xf * xf, axis=-1, keepdims=True) + EPS)


def _sigmoid(z):
    return 1.0 / (1.0 + jnp.exp(-z))


def _rms_bwd(dy, xhat, r, g):
    dxhat = dy * g
    return r * (dxhat - xhat * jnp.mean(dxhat * xhat, axis=-1, keepdims=True))


def _mesh_pos():
    return lax.axis_index("x"), lax.axis_index("y"), lax.axis_index("c")


ANY = pl.BlockSpec(memory_space=pl.ANY)


def _gather_shapes(shards):
    return [jax.ShapeDtypeStruct((N_DEV,) + s.shape, s.dtype) for s in shards]


def _gather_sems(n):
    return [pltpu.SemaphoreType.DMA((n, 7)), pltpu.SemaphoreType.DMA((n, 7)), pltpu.SemaphoreType.DMA((n,))]


def _gather_chips(x, y, c):
    return [(x ^ (1 - c), y ^ c), (x ^ c, y ^ (1 - c)), (1 - x, 1 - y)]


def _relayed_chip(chips, order):
    return chips[(1, 0, 2)[order - 4]]


def _gather_phases(ins, outs, send_sems, recv_sems, local_sems, by_arrival=False):
    n = len(ins)
    x, y, c = _mesh_pos()
    me, sibling = (x, y, c), (x, y, 1 - c)
    chips = _gather_chips(x, y, c)

    def blk(a, px, py, pc):
        return outs[a].at[4 * px + 2 * py + pc]

    def copy(a, k, block, to, src=None):
        return pltpu.make_async_remote_copy(
            src_ref=blk(a, *block) if src is None else src, dst_ref=blk(a, *block),
            send_sem=send_sems.at[a, k], recv_sem=recv_sems.at[a, k], device_id=to, device_id_type=MESH)

    def local(a):
        return pltpu.make_async_copy(ins[a], blk(a, *me), local_sems.at[a])

    def own(a):
        return [copy(a, 0, me, sibling, src=ins[a])] + [copy(a, 1 + j, me, (*chips[j], c), src=ins[a]) for j in range(2)]

    def onward(a):
        return copy(a, 3, (*chips[0], c), (*chips[1], c))

    def begin():
        for a in range(n):
            local(a).start()
        for a in range(n):
            for cp in own(a):
                cp.start()

    def arrive(order):
        for a in range(n):
            if order == 0:
                copy(a, 0, sibling, me).wait_recv()
            elif order <= 3:
                chip = chips[order - 1]
                copy(a, order, (*chip, c), me).wait_recv()
                copy(a, 3 + order, (*chip, c), sibling).start()
                if order == 1:
                    onward(a).start()
            else:
                copy(a, order, (*_relayed_chip(chips, order), 1 - c), me).wait_recv()

    def relay():
        for order in (1, 2):
            arrive(order)

    def drain():
        for a in range(n):
            for cp in own(a):
                cp.wait_send()
            onward(a).wait_send()
            for j, chip in enumerate(chips):
                copy(a, 4 + j, (*chip, c), sibling).wait_send()
            local(a).wait()

    def finish():
        for order in (3, 0, 4, 5, 6):
            arrive(order)
        drain()

    if by_arrival:
        return begin, arrive, drain
    return begin, relay, finish


def _pair_exchange(grads, name):
    n = len(grads)

    def body(*refs):
        begin, finish = _pair_exchange_phases(refs[:n], refs[n:2 * n], *refs[2 * n:])
        begin()
        finish()

    return pl.pallas_call(
        body, name=name,
        out_shape=_pair_exchange_shapes(grads),
        in_specs=[ANY] * n, out_specs=[ANY] * n,
        scratch_shapes=_pair_exchange_sems(n),
    )(*grads)


def _pair_exchange_shapes(grads):
    return [jax.ShapeDtypeStruct((4,) + g.shape[1:], g.dtype) for g in grads]


def _pair_exchange_sems(n):
    return [pltpu.SemaphoreType.DMA((n, 4)), pltpu.SemaphoreType.DMA((n, 4))]


def _pair_exchange_phases(ins, outs, send_sems, recv_sems, by_slot=False):
    x, y, c = _mesh_pos()
    xs, ys = (x, 1 - x), (y, 1 - y)

    def copies():
        out = []
        for a in range(len(ins)):
            for r in range(4):
                dx, dy = divmod(r, 2)
                out.append(pltpu.make_async_remote_copy(
                    src_ref=ins[a].at[r if by_slot else 4 * xs[dx] + 2 * ys[dy] + (1 - c)], dst_ref=outs[a].at[r],
                    send_sem=send_sems.at[a, r], recv_sem=recv_sems.at[a, r],
                    device_id=(x, y, 1 - c), device_id_type=MESH))
        return out

    def begin():
        for cp in copies():
            cp.start()

    def finish():
        for cp in copies():
            cp.wait()

    return begin, finish


def _chip_exchange_shapes(sums):
    return [jax.ShapeDtypeStruct((3,) + s.shape[1:], s.dtype) for s in sums]


def _chip_exchange_sems(n):
    return [pltpu.SemaphoreType.DMA((n, 3)), pltpu.SemaphoreType.DMA((n, 3))]


def _chip_exchange_phases(ins, outs, send_sems, recv_sems):
    x, y, c = _mesh_pos()
    xs, ys = (x, 1 - x), (y, 1 - y)

    def copies():
        out = []
        for a in range(len(ins)):
            for r in range(1, 4):
                dx, dy = divmod(r, 2)
                out.append(pltpu.make_async_remote_copy(
                    src_ref=ins[a].at[r], dst_ref=outs[a].at[r - 1],
                    send_sem=send_sems.at[a, r - 1], recv_sem=recv_sems.at[a, r - 1],
                    device_id=(xs[dx], ys[dy], c), device_id_type=MESH))
        return out

    def begin():
        for cp in copies():
            cp.start()

    def finish():
        for cp in copies():
            cp.wait()

    return begin, finish


def _small_all_gather(part):
    rows, cols = part.shape

    def body(in_ref, out_ref, send_sems, recv_sems):
        x, y, c = _mesh_pos()
        xs, ys, cs = (x, 1 - x), (y, 1 - y), (c, 1 - c)
        out_ref[4 * x + 2 * y + c] = in_ref[...]
        copies = []
        for k in range(1, N_DEV):
            dx, dy, dc = k // 4, (k // 2) % 2, k % 2
            copies.append((
                pltpu.make_async_remote_copy(
                    src_ref=in_ref, dst_ref=out_ref.at[4 * x + 2 * y + c],
                    send_sem=send_sems.at[k - 1], recv_sem=recv_sems.at[k - 1],
                    device_id=(xs[dx], ys[dy], cs[dc]), device_id_type=MESH),
                pltpu.make_async_remote_copy(
                    src_ref=in_ref, dst_ref=out_ref.at[4 * xs[dx] + 2 * ys[dy] + cs[dc]],
                    send_sem=send_sems.at[k - 1], recv_sem=recv_sems.at[k - 1],
                    device_id=(xs[dx], ys[dy], cs[dc]), device_id_type=MESH)))
        for send, _ in copies:
            send.start()
        for send, recv in copies:
            recv.wait_recv()
            send.wait_send()

    return pl.pallas_call(
        body, name="small_all_gather",
        out_shape=jax.ShapeDtypeStruct((N_DEV, rows, cols), part.dtype),
        in_specs=[pl.BlockSpec(memory_space=pltpu.VMEM)],
        out_specs=pl.BlockSpec(memory_space=pltpu.VMEM),
        scratch_shapes=[pltpu.SemaphoreType.DMA((N_DEV - 1,)), pltpu.SemaphoreType.DMA((N_DEV - 1,))],
    )(part)


ARRIVAL_ORDER = (0, 1, 4, 2, 5, 3, 6)


def _arrival_blocks(xpos, ypos, cpos):
    chips = _gather_chips(xpos, ypos, cpos)
    by_order = ([4 * xpos + 2 * ypos + (1 - cpos)] + [4 * cx + 2 * cy + cpos for cx, cy in chips]
                + [4 * cx + 2 * cy + (1 - cpos) for cx, cy in (_relayed_chip(chips, o) for o in (4, 5, 6))])
    return jnp.stack([4 * xpos + 2 * ypos + cpos] + [by_order[o] for o in ARRIVAL_ORDER]).astype(jnp.int32)


def _in_proj(x, g_mix, arrival_blocks, tm, shards):
    t, d = x.shape
    bw = shards[0].shape[1]
    nt = t // tm
    n = len(shards)

    def body(blocks_ref, x_ref, g_ref, *rest):
        w_shard = rest[0]
        proj_ref, h_ref = rest[n:n + 2]
        w_all = rest[n + 2]
        h_scr, w_buf, fetch_sems = rest[2 * n + 2:2 * n + 5]
        begin, arrive, drain = _gather_phases(rest[:n], rest[n + 2:2 * n + 2], *rest[2 * n + 5:], by_arrival=True)
        j, i = pl.program_id(0), pl.program_id(1)
        slot = lax.rem(j, 2)

        def fetch(src, into):
            return pltpu.make_async_copy(src, w_buf.at[into], fetch_sems.at[into])

        @pl.when((j == 0) & (i == 0))
        def _():
            begin()
            fetch(w_shard, 0).start()

        @pl.when(j == 0)
        def _():
            xf = x_ref[...]
            hv = (xf * _rstd(xf) * g_ref[...]).astype(BF16)
            h_ref[...] = hv
            h_scr[pl.ds(pl.multiple_of(i * tm, tm), tm), :] = hv

        @pl.when(i == 0)
        def _():
            fetch(w_shard, slot).wait()

        proj_ref[...] = _dot(h_scr[pl.ds(pl.multiple_of(i * tm, tm), tm), :], w_buf[slot]).astype(BF16)

        for nxt in range(1, N_DEV):
            @pl.when((i == (nt - 1) // 2) & (j == nxt - 1))
            def _():
                arrive(ARRIVAL_ORDER[nxt - 1])
                fetch(w_all.at[blocks_ref[nxt]], 1 - slot).start()

        pl.when((i == nt - 1) & (j == N_DEV - 1))(drain)

    first_pass = lambda j, i, blocks: (jnp.where(j == 0, i, nt - 1), 0)
    outs = pl.pallas_call(
        body, name="in_proj",
        grid_spec=pltpu.PrefetchScalarGridSpec(
            num_scalar_prefetch=1, grid=(N_DEV, nt),
            in_specs=[pl.BlockSpec((tm, d), first_pass), pl.BlockSpec((1, d), lambda j, i, blocks: (0, 0))] + [ANY] * n,
            out_specs=[pl.BlockSpec((tm, bw), lambda j, i, blocks: (i, blocks[j])),
                       pl.BlockSpec((tm, d), first_pass)] + [ANY] * n,
            scratch_shapes=[pltpu.VMEM((t, d), BF16), pltpu.VMEM((2, d, bw), BF16), pltpu.SemaphoreType.DMA((2,))]
            + _gather_sems(n)),
        out_shape=[jax.ShapeDtypeStruct((t, N_DEV * bw), BF16), jax.ShapeDtypeStruct((t, d), BF16)] + _gather_shapes(shards),
        compiler_params=pltpu.CompilerParams(dimension_semantics=("arbitrary", "arbitrary")),
    )(arrival_blocks, x, g_mix, *shards)
    return outs[0], outs[1], outs[2:]


def _conv_terms(ch_ref, cb_ref, cc_ref, w_ref):
    ch, cb, cc = ch_ref[...].astype(F32), cb_ref[...].astype(F32), cc_ref[...].astype(F32)
    u = cc * ch
    row = lax.broadcasted_iota(jnp.int32, u.shape, 0)
    u1 = jnp.where(row >= 1, pltpu.roll(u, 1, 0), 0.0)
    u2 = jnp.where(row >= 2, pltpu.roll(u, 2, 0), 0.0)
    w = (w_ref[0:1, :], w_ref[1:2, :], w_ref[2:3, :])
    cv = w[2] * u + w[1] * u1 + w[0] * u2
    return ch, cb, cc, u, u1, u2, cv, w, row


def _conv_fwd(proj, conv_w, cw):
    t = proj.shape[0]
    nper = D_MODEL // cw

    def body(ch_ref, cb_ref, cc_ref, w_ref, a_ref):
        _, cb, _, _, _, _, cv, _, _ = _conv_terms(ch_ref, cb_ref, cc_ref, w_ref)
        a_ref[...] = (cb * cv).astype(BF16)

    def col(piece):
        return pl.BlockSpec((t, cw), lambda j: (0, piece * nper + j))

    return pl.pallas_call(
        body, name="conv_fwd",
        grid=(nper,),
        in_specs=[col(COL_CH), col(COL_CB), col(COL_CC), pl.BlockSpec((3, cw), lambda j: (0, j))],
        out_specs=pl.BlockSpec((t, cw), lambda j: (0, j)),
        out_shape=jax.ShapeDtypeStruct((t, D_MODEL), BF16),
        compiler_params=pltpu.CompilerParams(dimension_semantics=("parallel",)),
    )(proj, proj, proj, conv_w)


def _scan_matrix(inclusive, value):
    s = lax.broadcasted_iota(jnp.int32, (SB_BLOCK, 2 * SB_BLOCK), 0)
    j = lax.rem(lax.broadcasted_iota(jnp.int32, (SB_BLOCK, 2 * SB_BLOCK), 1), SB_BLOCK)
    return jnp.where((j >= s) if inclusive else (j > s), value, 0.0).astype(BF16)


def _suffix_sum(u_mat, xv):
    hi = xv.astype(BF16)
    lo = (xv - hi.astype(F32)).astype(BF16)
    return _dot(u_mat, jnp.concatenate([hi, lo], axis=0))


def _head_rows(vt, h):
    row = lax.broadcasted_iota(jnp.int32, vt.shape, 0)
    return jnp.where((row >= h * SB_HEAD_DIM) & (row < (h + 1) * SB_HEAD_DIM), vt, 0.0).astype(BF16)


def _head_lanes(v, h):
    lane = lax.broadcasted_iota(jnp.int32, v.shape, 1)
    return jnp.where((lane >= h * SB_HEAD_DIM) & (lane < (h + 1) * SB_HEAD_DIM), v, 0.0).astype(BF16)


def _group_suffix(u_mat, xv, carry, negate=False):
    nblk = xv.shape[0] // SB_BLOCK
    parts = [None] * nblk
    for j in reversed(range(nblk)):
        xj = xv[j * SB_BLOCK:(j + 1) * SB_BLOCK]
        parts[j] = _suffix_sum(u_mat, xj) + carry
        total = jnp.sum(xj, axis=0, keepdims=True)
        carry = carry - total if negate else carry + total
    return jnp.concatenate(parts, axis=0), carry


def _sb_probs(kgrp, qt_h, carry, past):
    z = _dot(kgrp, qt_h)
    softplus = jnp.maximum(z, 0.0) + jnp.log(1.0 + jnp.exp(-jnp.abs(z)))
    if past is not None:
        softplus = jnp.where(past, softplus, 0.0)
    later, carry = _group_suffix(_scan_matrix(True, -1.0), softplus, carry, negate=True)
    a = jnp.exp(z + later)
    if past is not None:
        a = jnp.where(past, a, 0.0)
    return a, z, carry


def _sb_schedule(nq):
    steps = [(qi, g) for qi in range(nq) for g in range(qi, -1, -1)]
    return jnp.asarray([s[0] for s in steps], jnp.int32), jnp.asarray([s[1] for s in steps], jnp.int32)


def _past_mask(tq):
    return lax.broadcasted_iota(jnp.int32, (tq, tq), 0) < lax.broadcasted_iota(jnp.int32, (tq, tq), 1)


def _sb_fwd(proj, vt4, tq, shards):
    t = proj.shape[0]
    pairs = D_MODEL // SB_BLOCK
    nq = t // tq
    qi_tab, g_tab = _sb_schedule(nq)
    ns = qi_tab.shape[0]
    n = len(shards)

    def body(qi_ref, g_ref, q_ref, k_ref, vt_ref, *rest):
        o_ref, a_ref, z_ref = rest[n:n + 3]
        acc_ref, carry_ref, qt_ref = rest[2 * n + 3:2 * n + 6]
        begin, relay, finish = _gather_phases(rest[:n], rest[n + 3:2 * n + 3], *rest[2 * n + 6:])
        pi, si = pl.program_id(0), pl.program_id(1)
        diagonal = g_ref[si] == qi_ref[si]
        pl.when((pi == 0) & (si == 0))(begin)
        pl.when((pi == pairs // 2) & (si == 0))(relay)

        @pl.when(diagonal)
        def _():
            acc_ref[...] = jnp.zeros_like(acc_ref)
            carry_ref[...] = jnp.zeros_like(carry_ref)
            qt = q_ref[...].astype(F32).T * (SB_HEAD_DIM ** -0.5)
            for h in range(2):
                qt_ref[h] = _head_rows(qt, h)

        def step(past):
            for h in range(2):
                a, z, carry = _sb_probs(k_ref[...], qt_ref[h], carry_ref[h], past)
                ab = a.astype(BF16)
                a_ref[h] = ab
                z_ref[h] = z.astype(BF16)
                acc_ref[h] += _dot(vt_ref[h * SB_HEAD_DIM:(h + 1) * SB_HEAD_DIM, :], ab)
                carry_ref[h] = carry

        pl.when(diagonal)(lambda: step(_past_mask(tq)))
        pl.when(jnp.logical_not(diagonal))(lambda: step(None))

        @pl.when(g_ref[si] == 0)
        def _():
            o_ref[...] = jnp.concatenate([acc_ref[0], acc_ref[1]], axis=0).T

        pl.when((pi == pairs - 1) & (si == ns - 1))(finish)

    tile = pl.BlockSpec((None, None, 2, tq, tq), lambda p, s, qt_, gt_: (p, s, 0, 0, 0))
    tiles = jax.ShapeDtypeStruct((pairs, ns, 2, tq, tq), BF16)
    outs = pl.pallas_call(
        body, name="sb_fwd",
        grid_spec=pltpu.PrefetchScalarGridSpec(
            num_scalar_prefetch=2, grid=(pairs, ns),
            in_specs=[pl.BlockSpec((tq, SB_BLOCK), lambda p, s, qt_, gt_: (qt_[s], COL_SQ * pairs + p)),
                      pl.BlockSpec((tq, SB_BLOCK), lambda p, s, qt_, gt_: (gt_[s], COL_SK * pairs + p)),
                      pl.BlockSpec((None, None, SB_BLOCK, tq), lambda p, s, qt_, gt_: (p, gt_[s], 0, 0))] + [ANY] * n,
            out_specs=[pl.BlockSpec((tq, SB_BLOCK), lambda p, s, qt_, gt_: (qt_[s], p)), tile, tile] + [ANY] * n,
            scratch_shapes=[pltpu.VMEM((2, SB_HEAD_DIM, tq), F32), pltpu.VMEM((2, 1, tq), F32),
                            pltpu.VMEM((2, SB_BLOCK, tq), BF16)] + _gather_sems(n)),
        out_shape=[jax.ShapeDtypeStruct((t, D_MODEL), F32), tiles, tiles] + _gather_shapes(shards),
        compiler_params=pltpu.CompilerParams(dimension_semantics=("arbitrary", "arbitrary")),
    )(qi_tab, g_tab, proj, proj, vt4, *shards)
    return outs[0], outs[1], outs[2], outs[3:]


def _mem_prep(mem, g_mem, wkv_all, k_norm_g):
    m, d = mem.shape

    def body(mem_ref, g_ref, w_ref, kg_ref, memn_ref, kn_ref, v_ref):
        memf = mem_ref[...]
        memn = (memf * _rstd(memf) * g_ref[...]).astype(BF16)
        memn_ref[...] = memn
        for b in range(N_DEV):
            kv = _dot(memn, w_ref[b])
            if b < X_HEADS:
                kn_ref[:, b * X_HEAD_DIM:(b + 1) * X_HEAD_DIM] = (kv * _rstd(kv) * kg_ref[...]).astype(BF16)
            else:
                h = b - X_HEADS
                v_ref[:, h * X_HEAD_DIM:(h + 1) * X_HEAD_DIM] = kv.astype(BF16)

    return pl.pallas_call(
        body, name="mem_prep",
        out_shape=[jax.ShapeDtypeStruct((m, d), BF16)] * 3,
    )(mem, g_mem, wkv_all, k_norm_g)


def _x_head(xq_ref, qg, kn_ref, h):
    sl = slice(h * X_HEAD_DIM, (h + 1) * X_HEAD_DIM)
    q = xq_ref[:, sl].astype(F32)
    rq = _rstd(q)
    qhat = q * rq
    qn = (qhat * qg).astype(BF16)
    s = _dot_nt(qn, kn_ref[:, sl]) * (X_HEAD_DIM ** -0.5)
    e = jnp.exp(s - jnp.max(s, axis=-1, keepdims=True))
    p = e / jnp.sum(e, axis=-1, keepdims=True)
    return sl, rq, qhat, qn, p


def _x_fwd(proj, q_norm_g, kn, v, tm):
    t = proj.shape[0]
    m = kn.shape[0]

    def body(xq_ref, qg_ref, kn_ref, v_ref, o_ref):
        for h in range(X_HEADS):
            sl, _, _, _, p = _x_head(xq_ref, qg_ref[...], kn_ref, h)
            o_ref[:, sl] = _dot(p.astype(BF16), v_ref[:, sl]).astype(BF16)

    return pl.pallas_call(
        body, name="x_fwd",
        grid=(t // tm,),
        in_specs=[pl.BlockSpec((tm, D_MODEL), lambda i: (i, COL_XQ)),
                  pl.BlockSpec((1, X_HEAD_DIM), lambda i: (0, 0)),
                  pl.BlockSpec((m, D_MODEL), lambda i: (0, 0)),
                  pl.BlockSpec((m, D_MODEL), lambda i: (0, 0))],
        out_specs=pl.BlockSpec((tm, D_MODEL), lambda i: (i, 0)),
        out_shape=jax.ShapeDtypeStruct((t, D_MODEL), BF16),
        compiler_params=pltpu.CompilerParams(dimension_semantics=("parallel",)),
    )(proj, q_norm_g, kn, v)


def _gate_spec(tm, branch):
    return pl.BlockSpec((tm, D_MODEL), lambda i: (i, COL_GATE + branch))


def _merge_fwd(x, proj, a_conv, o_sb, o_x, w_conv_out, w_sb_out, w_x_out, w_out, tm):
    t, d = x.shape

    def body(x_ref, g0_ref, g1_ref, g2_ref, a_ref, s_ref, xo_ref, wc_ref, ws_ref, wx_ref, wo_ref,
             x1_ref, yc_ref, ys_ref, yx_ref, mg_ref):
        merged = jnp.zeros((tm, d), F32)
        for gate_ref, b_ref, w_ref, y_ref in ((g0_ref, a_ref, wc_ref, yc_ref), (g1_ref, s_ref, ws_ref, ys_ref),
                                              (g2_ref, xo_ref, wx_ref, yx_ref)):
            yv = _dot(b_ref[...].astype(BF16), w_ref[...])
            y_ref[...] = yv.astype(BF16)
            merged = merged + _sigmoid(gate_ref[...].astype(F32)) * yv
        mb = merged.astype(BF16)
        mg_ref[...] = mb
        x1_ref[...] = x_ref[...] + _dot(mb, wo_ref[...])

    tile = pl.BlockSpec((tm, d), lambda i: (i, 0))
    wfull = pl.BlockSpec((d, d), lambda i: (0, 0))
    return pl.pallas_call(
        body, name="merge_fwd",
        grid=(t // tm,),
        in_specs=[tile] + [_gate_spec(tm, b) for b in range(N_BRANCH)] + [tile, tile, tile,
                                                                           wfull, wfull, wfull, wfull],
        out_specs=[tile] * 5,
        out_shape=[jax.ShapeDtypeStruct((t, d), F32)] + [jax.ShapeDtypeStruct((t, d), BF16)] * 4,
        compiler_params=pltpu.CompilerParams(dimension_semantics=("parallel",)),
    )(x, proj, proj, proj, a_conv, o_sb, o_x, w_conv_out, w_sb_out, w_x_out, w_out)


def _mlp_fwd(x1, g_mlp, w_up_all, w_down, target, tm):
    t, d = x1.shape
    nb, _, fw = w_up_all.shape

    def body(x1_ref, g_ref, wu_ref, wd_ref, tgt_ref, up_ref, h2_ref, dx2_ref, lsum_ref, acc_ref):
        i, j = pl.program_id(0), pl.program_id(1)

        @pl.when(j == 0)
        def _():
            xf = x1_ref[...]
            h2_ref[...] = (xf * _rstd(xf) * g_ref[...]).astype(BF16)
            acc_ref[...] = jnp.zeros_like(acc_ref)

        @pl.when((i == 0) & (j == 0))
        def _():
            lsum_ref[...] = jnp.zeros_like(lsum_ref)

        up = _dot(h2_ref[...], wu_ref[...])
        up_ref[...] = up.astype(BF16)
        act = jnp.square(jnp.maximum(up, 0.0)).astype(BF16)
        acc_ref[...] += _dot(act, wd_ref[...])

        @pl.when(j == nb - 1)
        def _():
            diff = x1_ref[...] + acc_ref[...] - tgt_ref[...]
            dx2_ref[...] = diff * (1.0 / d)
            lsum_ref[...] += jnp.sum(diff * diff, axis=0, keepdims=True)

    tile = pl.BlockSpec((tm, d), lambda i, j: (i, 0))
    row = pl.BlockSpec((1, d), lambda i, j: (0, 0))
    return pl.pallas_call(
        body, name="mlp_fwd",
        grid=(t // tm, nb),
        in_specs=[tile, row, pl.BlockSpec((None, d, fw), lambda i, j: (j, 0, 0)),
                  pl.BlockSpec((fw, d), lambda i, j: (j, 0)), tile],
        out_specs=[pl.BlockSpec((tm, fw), lambda i, j: (i, j)), tile, tile, row],
        out_shape=[jax.ShapeDtypeStruct((t, nb * fw), BF16), jax.ShapeDtypeStruct((t, d), BF16),
                   jax.ShapeDtypeStruct((t, d), F32), jax.ShapeDtypeStruct((1, d), F32)],
        scratch_shapes=[pltpu.VMEM((tm, d), F32)],
        compiler_params=pltpu.CompilerParams(dimension_semantics=("arbitrary", "arbitrary")),
    )(x1, g_mlp, w_up_all, w_down, target)


def _mlp_bwd(x1, g_mlp, w_up_all, w_down, up, dx2, tm):
    t, d = x1.shape
    nb, _, fw = w_up_all.shape

    def body(x1_ref, g_ref, wu_ref, wd_ref, up_ref, dx2_ref, dup_ref, act_ref, dx1_ref, dg_ref, acc_ref, dyb_ref):
        i, j = pl.program_id(0), pl.program_id(1)

        @pl.when(j == 0)
        def _():
            dyb_ref[...] = dx2_ref[...].astype(BF16)
            acc_ref[...] = jnp.zeros_like(acc_ref)

        @pl.when((i == 0) & (j == 0))
        def _():
            dg_ref[...] = jnp.zeros_like(dg_ref)

        r = jnp.maximum(up_ref[...].astype(F32), 0.0)
        act_ref[...] = jnp.square(r).astype(BF16)
        dup = (_dot_nt(dyb_ref[...], wd_ref[...]) * (2.0 * r)).astype(BF16)
        dup_ref[...] = dup
        acc_ref[...] += _dot_nt(dup, wu_ref[...])

        @pl.when(j == nb - 1)
        def _():
            xf = x1_ref[...]
            rs = _rstd(xf)
            xhat = xf * rs
            dh2 = acc_ref[...]
            dg_ref[...] += jnp.sum(dh2 * xhat, axis=0, keepdims=True)
            dx1_ref[...] = dx2_ref[...] + _rms_bwd(dh2, xhat, rs, g_ref[...])

    tile = pl.BlockSpec((tm, d), lambda i, j: (i, 0))
    row = pl.BlockSpec((1, d), lambda i, j: (0, 0))
    ff = pl.BlockSpec((tm, fw), lambda i, j: (i, j))
    return pl.pallas_call(
        body, name="mlp_bwd",
        grid=(t // tm, nb),
        in_specs=[tile, row, pl.BlockSpec((None, d, fw), lambda i, j: (j, 0, 0)),
                  pl.BlockSpec((fw, d), lambda i, j: (j, 0)), ff, tile],
        out_specs=[ff, ff, tile, row],
        out_shape=[jax.ShapeDtypeStruct((t, nb * fw), BF16), jax.ShapeDtypeStruct((t, nb * fw), BF16),
                   jax.ShapeDtypeStruct((t, d), F32), jax.ShapeDtypeStruct((1, d), F32)],
        scratch_shapes=[pltpu.VMEM((tm, d), F32), pltpu.VMEM((tm, d), BF16)],
        compiler_params=pltpu.CompilerParams(dimension_semantics=("arbitrary", "arbitrary")),
    )(x1, g_mlp, w_up_all, w_down, up, dx2)


def _merge_bwd(dx1, proj, y_conv, y_sb, y_x, w_conv_out, w_sb_out, w_x_out, w_out, tm):
    t, d = dx1.shape

    def body(dx1_ref, g0_ref, g1_ref, g2_ref, yc_ref, ys_ref, yx_ref, wc_ref, ws_ref, wx_ref, wo_ref,
             dgate_ref, dyc_ref, dys_ref, dyx_ref, da_ref, dos_ref, dox_ref):
        dm = _dot_nt(dx1_ref[...].astype(BF16), wo_ref[...])
        for i, (gate_ref, y_ref, w_ref, dy_ref, db_ref) in enumerate(((g0_ref, yc_ref, wc_ref, dyc_ref, da_ref),
                                                                       (g1_ref, ys_ref, ws_ref, dys_ref, dos_ref),
                                                                       (g2_ref, yx_ref, wx_ref, dyx_ref, dox_ref))):
            gt = _sigmoid(gate_ref[...].astype(F32))
            dy = (dm * gt).astype(BF16)
            dy_ref[...] = dy
            dgate_ref[:, i * d:(i + 1) * d] = (dm * y_ref[...].astype(F32) * gt * (1.0 - gt)).astype(BF16)
            db_ref[...] = _dot_nt(dy, w_ref[...]).astype(BF16)

    tile = pl.BlockSpec((tm, d), lambda i: (i, 0))
    wfull = pl.BlockSpec((d, d), lambda i: (0, 0))
    return pl.pallas_call(
        body, name="merge_bwd",
        grid=(t // tm,),
        in_specs=[tile] + [_gate_spec(tm, b) for b in range(N_BRANCH)] + [tile, tile, tile,
                                                                           wfull, wfull, wfull, wfull],
        out_specs=[pl.BlockSpec((tm, N_BRANCH * d), lambda i: (i, 0))] + [tile] * 6,
        out_shape=[jax.ShapeDtypeStruct((t, N_BRANCH * d), BF16)] + [jax.ShapeDtypeStruct((t, d), BF16)] * 6,
        compiler_params=pltpu.CompilerParams(dimension_semantics=("parallel",)),
    )(dx1, proj, proj, proj, y_conv, y_sb, y_x, w_conv_out, w_sb_out, w_x_out, w_out)


def _conv_bwd(proj, conv_w, da, cw):
    t = proj.shape[0]
    nper = D_MODEL // cw

    def body(ch_ref, cb_ref, cc_ref, w_ref, da_ref, dch_ref, dcb_ref, dcc_ref, dw_ref):
        ch, cb, cc, u, u1, u2, cv, w, row = _conv_terms(ch_ref, cb_ref, cc_ref, w_ref)
        dav = da_ref[...].astype(F32)
        dcb_ref[...] = (dav * cv).astype(BF16)
        dcv = dav * cb
        n1 = jnp.where(row < t - 1, pltpu.roll(dcv, t - 1, 0), 0.0)
        n2 = jnp.where(row < t - 2, pltpu.roll(dcv, t - 2, 0), 0.0)
        du = w[2] * dcv + w[1] * n1 + w[0] * n2
        dcc_ref[...] = (du * ch).astype(BF16)
        dch_ref[...] = (du * cc).astype(BF16)
        dw_ref[0:1, :] = jnp.sum(dcv * u2, axis=0, keepdims=True)
        dw_ref[1:2, :] = jnp.sum(dcv * u1, axis=0, keepdims=True)
        dw_ref[2:3, :] = jnp.sum(dcv * u, axis=0, keepdims=True)

    def col(piece):
        return pl.BlockSpec((t, cw), lambda j: (0, piece * nper + j))

    out_col = pl.BlockSpec((t, cw), lambda j: (0, j))
    wspec = pl.BlockSpec((3, cw), lambda j: (0, j))
    return pl.pallas_call(
        body, name="conv_bwd",
        grid=(nper,),
        in_specs=[col(COL_CH), col(COL_CB), col(COL_CC), wspec, out_col],
        out_specs=[out_col, out_col, out_col, wspec],
        out_shape=[jax.ShapeDtypeStruct((t, D_MODEL), BF16)] * 3 + [jax.ShapeDtypeStruct((3, D_MODEL), F32)],
        compiler_params=pltpu.CompilerParams(dimension_semantics=("parallel",)),
    )(proj, proj, proj, conv_w, da)


def _sb_bwd(proj, kt4, do_sb, o_sb, weights, logits, tq, pair_sums):
    t = proj.shape[0]
    nq = t // tq
    pairs = D_MODEL // SB_BLOCK
    scale = SB_HEAD_DIM ** -0.5
    qi_tab, g_tab = _sb_schedule(nq)
    ns = qi_tab.shape[0]
    n = len(pair_sums)

    def body(qi_ref, g_ref, q_ref, v_ref, kt_ref, do_ref, o_ref, a_ref, z_ref, *rest):
        dq_ref, dk_ref, dv_ref = rest[n:n + 3]
        dk_acc, dv_acc, dqt_ref, carry_ref, qm_ref, dom_ref, dot_ref, dsum_ref = rest[2 * n + 3:2 * n + 11]
        begin, finish = _chip_exchange_phases(rest[:n], rest[n + 3:2 * n + 3], *rest[2 * n + 11:])
        pi, si = pl.program_id(0), pl.program_id(1)
        diagonal = g_ref[si] == qi_ref[si]
        pl.when((pi == 0) & (si == 0))(begin)

        @pl.when(si == 0)
        def _():
            dk_acc[...] = jnp.zeros_like(dk_acc)
            dv_acc[...] = jnp.zeros_like(dv_acc)

        @pl.when(diagonal)
        def _():
            dqt_ref[...] = jnp.zeros_like(dqt_ref)
            carry_ref[...] = jnp.zeros_like(carry_ref)
            q = q_ref[...].astype(F32) * scale
            do = do_ref[...].astype(F32)
            dot_ = do.T
            prod = dot_ * o_ref[...].T
            for h in range(2):
                rows = slice(h * SB_HEAD_DIM, (h + 1) * SB_HEAD_DIM)
                qm_ref[h] = _head_lanes(q, h)
                dom_ref[h] = _head_lanes(do, h)
                dot_ref[h] = _head_rows(dot_, h)
                dsum_ref[h] = jnp.sum(prod[rows, :], axis=0, keepdims=True)

        def step(past):
            u_mat = _scan_matrix(False, 1.0)
            ks = pl.multiple_of(g_ref[si] * tq, tq)
            dk_add = jnp.zeros((tq, SB_BLOCK), F32)
            dv_add = jnp.zeros((tq, SB_BLOCK), F32)
            for h in range(2):
                rows = slice(h * SB_HEAD_DIM, (h + 1) * SB_HEAD_DIM)
                ab = a_ref[h]
                gw = _dot(v_ref[...], dot_ref[h]) * ab.astype(F32)
                after, carry = _group_suffix(u_mat, gw, carry_ref[h])
                sig = pl.reciprocal(1.0 + jnp.exp(-z_ref[h].astype(F32)), approx=True)
                dz = gw - sig * (dsum_ref[h] - after)
                if past is not None:
                    dz = jnp.where(past, dz, 0.0)
                dzb = dz.astype(BF16)
                dqt_ref[h] += _dot(kt_ref[rows, :], dzb)
                dk_add = dk_add + _dot(dzb, qm_ref[h])
                dv_add = dv_add + _dot(ab, dom_ref[h])
                carry_ref[h] = carry
            dk_acc[pl.ds(ks, tq), :] += dk_add
            dv_acc[pl.ds(ks, tq), :] += dv_add

        pl.when(diagonal)(lambda: step(_past_mask(tq)))
        pl.when(jnp.logical_not(diagonal))(lambda: step(None))

        @pl.when(g_ref[si] == 0)
        def _():
            dq_ref[...] = (jnp.concatenate([dqt_ref[0], dqt_ref[1]], axis=0).T * scale).astype(BF16)

        @pl.when(si == ns - 1)
        def _():
            dk_ref[...] = dk_acc[...].astype(BF16)
            dv_ref[...] = dv_acc[...].astype(BF16)

        pl.when((pi == pairs - 1) & (si == ns - 1))(finish)

    qblk = lambda base: pl.BlockSpec((tq, SB_BLOCK), lambda p, s, qt_, gt_: (qt_[s], base * pairs + p))
    kgrp = lambda base: pl.BlockSpec((tq, SB_BLOCK), lambda p, s, qt_, gt_: (gt_[s], base * pairs + p))
    seq = pl.BlockSpec((t, SB_BLOCK), lambda p, s, qt_, gt_: (0, p))
    tr = pl.BlockSpec((None, None, SB_BLOCK, tq), lambda p, s, qt_, gt_: (p, gt_[s], 0, 0))
    tile = pl.BlockSpec((None, None, 2, tq, tq), lambda p, s, qt_, gt_: (p, s, 0, 0, 0))
    outs = pl.pallas_call(
        body, name="sb_bwd",
        grid_spec=pltpu.PrefetchScalarGridSpec(
            num_scalar_prefetch=2, grid=(pairs, ns),
            in_specs=[qblk(COL_SQ), kgrp(COL_SV), tr, qblk(0), qblk(0), tile, tile] + [ANY] * n,
            out_specs=[qblk(0), seq, seq] + [ANY] * n,
            scratch_shapes=[pltpu.VMEM((t, SB_BLOCK), F32), pltpu.VMEM((t, SB_BLOCK), F32),
                            pltpu.VMEM((2, SB_HEAD_DIM, tq), F32), pltpu.VMEM((2, 1, tq), F32),
                            pltpu.VMEM((2, tq, SB_BLOCK), BF16), pltpu.VMEM((2, tq, SB_BLOCK), BF16),
                            pltpu.VMEM((2, SB_BLOCK, tq), BF16), pltpu.VMEM((2, 1, tq), F32)] + _chip_exchange_sems(n)),
        out_shape=[jax.ShapeDtypeStruct((t, D_MODEL), BF16)] * 3 + _chip_exchange_shapes(pair_sums),
        compiler_params=pltpu.CompilerParams(dimension_semantics=("arbitrary", "arbitrary")),
    )(qi_tab, g_tab, proj, proj, kt4, do_sb, o_sb, weights, logits, *pair_sums)
    return outs[0], outs[1], outs[2], outs[3:]


def _x_bwd(proj, q_norm_g, kn, v, do_x, tm, grads):
    t = proj.shape[0]
    m = kn.shape[0]
    scale = X_HEAD_DIM ** -0.5
    nt = t // tm
    n = len(grads)

    def body(xq_ref, qg_ref, kn_ref, v_ref, do_ref, *rest):
        dxq_ref, dkn_ref, dv_ref, dqg_ref = rest[n:n + 4]
        begin, finish = _pair_exchange_phases(rest[:n], rest[n + 4:2 * n + 4], *rest[2 * n + 4:])

        @pl.when(pl.program_id(0) == 0)
        def _():
            begin()
            dkn_ref[...] = jnp.zeros_like(dkn_ref)
            dv_ref[...] = jnp.zeros_like(dv_ref)
            dqg_ref[...] = jnp.zeros_like(dqg_ref)

        qg = qg_ref[...]
        for h in range(X_HEADS):
            sl, rq, qhat, qn, p = _x_head(xq_ref, qg, kn_ref, h)
            do_h = do_ref[:, sl]
            dp = _dot_nt(do_h, v_ref[:, sl])
            ds = (p * (dp - jnp.sum(dp * p, axis=-1, keepdims=True)) * scale).astype(BF16)
            dqn = _dot(ds, kn_ref[:, sl])
            dkn_ref[:, sl] += _dot_tn(ds, qn)
            dv_ref[:, sl] += _dot_tn(p.astype(BF16), do_h)
            dqg_ref[...] += jnp.sum(dqn * qhat, axis=0, keepdims=True)
            dxq_ref[:, sl] = _rms_bwd(dqn, qhat, rq, qg).astype(BF16)

        pl.when(pl.program_id(0) == nt - 1)(finish)

    full = pl.BlockSpec((m, D_MODEL), lambda i: (0, 0))
    gain = pl.BlockSpec((1, X_HEAD_DIM), lambda i: (0, 0))
    tile = pl.BlockSpec((tm, D_MODEL), lambda i: (i, 0))
    outs = pl.pallas_call(
        body, name="x_bwd",
        grid=(nt,),
        in_specs=[pl.BlockSpec((tm, D_MODEL), lambda i: (i, COL_XQ)), gain, full, full, tile] + [ANY] * n,
        out_specs=[tile, full, full, gain] + [ANY] * n,
        out_shape=[jax.ShapeDtypeStruct((t, D_MODEL), BF16), jax.ShapeDtypeStruct((m, D_MODEL), F32),
                   jax.ShapeDtypeStruct((m, D_MODEL), F32), jax.ShapeDtypeStruct((1, X_HEAD_DIM), F32)]
        + _pair_exchange_shapes(grads),
        scratch_shapes=_pair_exchange_sems(n),
        compiler_params=pltpu.CompilerParams(dimension_semantics=("arbitrary",)),
    )(proj, q_norm_g, kn, v, do_x, *grads)
    return outs[0], outs[1], outs[2], outs[3], outs[4:]


def _mem_bwd(mem, g_mem, wkv_all, k_norm_g, dkn, dv):
    m, d = mem.shape

    def body(mem_ref, g_ref, w_ref, kg_ref, dkn_ref, dv_ref, dkv_ref, dgm_ref, dkg_ref):
        memf = mem_ref[...]
        mem_hat = memf * _rstd(memf)
        memn = (mem_hat * g_ref[...]).astype(BF16)
        kg = kg_ref[...]
        dmemn = jnp.zeros((m, d), F32)
        dkg = jnp.zeros((1, X_HEAD_DIM), F32)
        for b in range(N_DEV):
            sl = slice(b * X_HEAD_DIM, (b + 1) * X_HEAD_DIM)
            if b < X_HEADS:
                kv = _dot(memn, w_ref[b])
                rk = _rstd(kv)
                khat = kv * rk
                dkn_h = dkn_ref[:, sl]
                dkg = dkg + jnp.sum(dkn_h * khat, axis=0, keepdims=True)
                dblk = _rms_bwd(dkn_h, khat, rk, kg).astype(BF16)
            else:
                hs = slice((b - X_HEADS) * X_HEAD_DIM, (b - X_HEADS + 1) * X_HEAD_DIM)
                dblk = dv_ref[:, hs].astype(BF16)
            dkv_ref[:, sl] = dblk
            dmemn = dmemn + _dot_nt(dblk, w_ref[b])
        dgm_ref[...] = jnp.sum(dmemn * mem_hat, axis=0, keepdims=True)
        dkg_ref[...] = dkg

    return pl.pallas_call(
        body, name="mem_bwd",
        out_shape=[jax.ShapeDtypeStruct((m, 2 * d), BF16), jax.ShapeDtypeStruct((1, d), F32),
                   jax.ShapeDtypeStruct((1, X_HEAD_DIM), F32)],
    )(mem, g_mem, wkv_all, k_norm_g, dkn, dv)


def _in_proj_bwd(x, g_mix, w_in_all, dproj, dx1, tm, pair_sums):
    t, d = x.shape
    nb, _, bw = w_in_all.shape
    nt = t // tm
    n = len(pair_sums)

    def body(x_ref, g_ref, w_ref, dp_ref, dx1_ref, *rest):
        dx_ref, dg_ref = rest[n:n + 2]
        acc_ref = rest[2 * n + 2]
        begin, finish = _chip_exchange_phases(rest[:n], rest[n + 2:2 * n + 2], *rest[2 * n + 3:])
        i, j = pl.program_id(0), pl.program_id(1)
        pl.when((i == 0) & (j == 0))(begin)

        @pl.when(j == 0)
        def _():
            acc_ref[...] = jnp.zeros_like(acc_ref)

        @pl.when((i == 0) & (j == 0))
        def _():
            dg_ref[...] = jnp.zeros_like(dg_ref)

        acc_ref[...] += _dot_nt(dp_ref[...], w_ref[...])

        @pl.when(j == nb - 1)
        def _():
            xf = x_ref[...]
            rs = _rstd(xf)
            xhat = xf * rs
            dh = acc_ref[...]
            dg_ref[...] += jnp.sum(dh * xhat, axis=0, keepdims=True)
            dx_ref[...] = dx1_ref[...] + _rms_bwd(dh, xhat, rs, g_ref[...])

        pl.when((i == nt - 1) & (j == nb - 1))(finish)

    tile = pl.BlockSpec((tm, d), lambda i, j: (i, 0))
    row = pl.BlockSpec((1, d), lambda i, j: (0, 0))
    outs = pl.pallas_call(
        body, name="in_proj_bwd",
        grid=(nt, nb),
        in_specs=[tile, row, pl.BlockSpec((None, d, bw), lambda i, j: (j, 0, 0)),
                  pl.BlockSpec((tm, bw), lambda i, j: (i, j)), tile] + [ANY] * n,
        out_specs=[tile, row] + [ANY] * n,
        out_shape=[jax.ShapeDtypeStruct((t, d), F32), jax.ShapeDtypeStruct((1, d), F32)] + _chip_exchange_shapes(pair_sums),
        scratch_shapes=[pltpu.VMEM((tm, d), F32)] + _chip_exchange_sems(n),
        compiler_params=pltpu.CompilerParams(dimension_semantics=("arbitrary", "arbitrary")),
    )(x, g_mix, w_in_all, dproj, dx1, *pair_sums)
    return outs[0], outs[1], outs[2:]


def _weight_grad(a, b, bw, tmm, name):
    t, m = a.shape
    n = b.shape[1]
    tmm = min(tmm, m)

    def body(a_ref, b_ref, o_ref):
        o_ref[...] = _dot_tn(a_ref[...].astype(BF16), b_ref[...].astype(BF16)).astype(BF16)

    return pl.pallas_call(
        body, name=name,
        grid=(m // tmm, n // bw),
        in_specs=[pl.BlockSpec((t, tmm), lambda i, j: (0, i)), pl.BlockSpec((t, bw), lambda i, j: (0, j))],
        out_specs=pl.BlockSpec((None, tmm, bw), lambda i, j: (j, i, 0)),
        out_shape=jax.ShapeDtypeStruct((n // bw, m, bw), BF16),
        compiler_params=pltpu.CompilerParams(dimension_semantics=("parallel", "parallel")),
    )(a, b)


def _weight_grad_blocks(a, b, bw, blocks, tmm, name, send=()):
    t, m = a.shape
    tmm = min(tmm, m)
    nm = m // tmm
    n = len(send)

    def body(blocks_ref, a_ref, b_ref, *rest):
        o_ref = rest[n]
        i, j = pl.program_id(0), pl.program_id(1)
        if n:
            begin, finish = _pair_exchange_phases(rest[:n], rest[n + 1:2 * n + 1], *rest[2 * n + 1:], by_slot=True)
            pl.when((i == 0) & (j == 0))(begin)
        o_ref[...] = _dot_tn(a_ref[...].astype(BF16), b_ref[...].astype(BF16)).astype(BF16)
        if n:
            pl.when((i == nm - 1) & (j == 3))(finish)

    outs = pl.pallas_call(
        body, name=name,
        grid_spec=pltpu.PrefetchScalarGridSpec(
            num_scalar_prefetch=1, grid=(nm, 4),
            in_specs=[pl.BlockSpec((t, tmm), lambda i, j, blk: (0, i)),
                      pl.BlockSpec((t, bw), lambda i, j, blk: (0, blk[j]))] + [ANY] * n,
            out_specs=[pl.BlockSpec((None, tmm, bw), lambda i, j, blk: (j, i, 0))] + [ANY] * n,
            scratch_shapes=_pair_exchange_sems(n) if n else []),
        out_shape=[jax.ShapeDtypeStruct((4, m, bw), BF16)] + _pair_exchange_shapes(send),
        compiler_params=pltpu.CompilerParams(dimension_semantics=("arbitrary", "arbitrary")),
    )(blocks, a, b, *send)
    return outs[0], outs[1:]


def _pair_sum(grads, recvs, own_blocks, name):
    k = len(grads)
    _, rows, cols = grads[0].shape

    def body(idx_ref, *refs):
        for a in range(k):
            refs[2 * k + a][...] = (refs[a][...].astype(F32) + refs[k + a][...].astype(F32)).astype(BF16)

    slot = pl.BlockSpec((None, rows, cols), lambda r, idx: (r, 0, 0))
    return pl.pallas_call(
        body, name=name,
        grid_spec=pltpu.PrefetchScalarGridSpec(
            num_scalar_prefetch=1, grid=(4,),
            in_specs=[pl.BlockSpec((None, rows, cols), lambda r, idx: (idx[r], 0, 0))] * k + [slot] * k,
            out_specs=[slot] * k),
        out_shape=[jax.ShapeDtypeStruct((4, rows, cols), BF16)] * k,
        compiler_params=pltpu.CompilerParams(dimension_semantics=("parallel",)),
    )(own_blocks, *grads, *recvs)


def _adamw_math(w, g, m, v):
    m = ADAM_B1 * m + (1.0 - ADAM_B1) * g
    v = ADAM_B2 * v + (1.0 - ADAM_B2) * jnp.square(g)
    m_hat = m / (1.0 - ADAM_B1 ** ADAM_STEP)
    v_hat = v / (1.0 - ADAM_B2 ** ADAM_STEP)
    delta = -ADAM_LR * (m_hat / (jnp.sqrt(v_hat) + ADAM_EPS) + ADAM_WD * w)
    return delta, m, v


def _adamw_sharded(shards, tr, name):
    k = len(shards)
    rows, cols = shards[0][2].shape
    tr = min(tr, rows)

    def body(*refs):
        for a in range(k):
            h_ref, r_ref, w_ref, m_ref, v_ref = refs[5 * a:5 * a + 5]
            g_out, d_out, m_out, v_out = refs[5 * k + 4 * a:5 * k + 4 * a + 4]
            g = h_ref[...].astype(F32)
            for r in range(3):
                g = g + r_ref[r].astype(F32)
            g_out[...] = g
            d_out[...], m_out[...], v_out[...] = _adamw_math(w_ref[...], g, m_ref[...], v_ref[...])

    tile = pl.BlockSpec((tr, cols), lambda i: (i, 0))
    outs = pl.pallas_call(
        body, name=name,
        grid=(rows // tr,),
        in_specs=[pl.BlockSpec((None, tr, cols), lambda i: (0, i, 0)),
                  pl.BlockSpec((3, tr, cols), lambda i: (0, i, 0)), tile, tile, tile] * k,
        out_specs=[tile] * (4 * k),
        out_shape=[jax.ShapeDtypeStruct((rows, cols), F32)] * (4 * k),
        compiler_params=pltpu.CompilerParams(dimension_semantics=("parallel",)),
    )(*[op for shard in shards for op in shard])
    return [tuple(outs[4 * a:4 * a + 4]) for a in range(k)]


SMALL_ROWS = 16


def _pack_rows(dg_mix, dg_mem, dg_mlp, dqg, dkg, dconv, lsum):
    def body(a_ref, b_ref, c_ref, q_ref, k_ref, cv_ref, l_ref, o_ref):
        o_ref[...] = jnp.zeros_like(o_ref)
        for r, ref in enumerate((a_ref, b_ref, c_ref)):
            o_ref[r:r + 1, :] = ref[...]
        o_ref[3:4, :X_HEAD_DIM] = q_ref[...]
        o_ref[4:5, :X_HEAD_DIM] = k_ref[...]
        o_ref[5:8, :] = cv_ref[...]
        o_ref[8:9, :] = l_ref[...]

    return pl.pallas_call(body, name="small_pack", out_shape=jax.ShapeDtypeStruct((SMALL_ROWS, D_MODEL), F32))(
        dg_mix, dg_mem, dg_mlp, dqg, dkg, dconv, lsum)


def _small_sum(gathered):
    def body(g_ref, o_ref):
        total = g_ref[0]
        for dev in range(1, N_DEV):
            total = total + g_ref[dev]
        o_ref[...] = jnp.zeros_like(o_ref)
        for piece in range(5):
            o_ref[piece * SMALL_TILE:piece * SMALL_TILE + 1, :] = total[piece:piece + 1]
        o_ref[5 * SMALL_TILE:5 * SMALL_TILE + 3, :] = total[5:8]
        o_ref[6 * SMALL_TILE:6 * SMALL_TILE + 1, :] = total[8:9]

    return pl.pallas_call(body, name="small_grad_sum",
                          out_shape=jax.ShapeDtypeStruct((7 * SMALL_TILE, D_MODEL), F32))(gathered)


def _adamw_small(w, g, m, v):
    def body(w_ref, g_ref, m_ref, v_ref, d_out, m_out, v_out):
        d_out[...], m_out[...], v_out[...] = _adamw_math(w_ref[...], g_ref[...], m_ref[...], v_ref[...])

    return pl.pallas_call(body, name="adamw_small", out_shape=[jax.ShapeDtypeStruct(w.shape, F32)] * 3)(w, g, m, v)


def _pad_tile(a):
    return jnp.pad(a, ((0, SMALL_TILE - a.shape[0]), (0, D_MODEL - a.shape[1])))


def _pack_small(*pieces):
    return jnp.concatenate([_pad_tile(a) for a in pieces], axis=0)


def kernel(x, mem, g_mix, g_mem, w_in, conv_w, w_conv_out, w_sb_out, q_norm_g, k_norm_g, w_mem_kv, w_x_out, w_out, g_mlp, w_up, w_down, loss_target, m_g_mix, m_g_mem, m_w_in, m_conv_w, m_w_conv_out, m_w_sb_out, m_q_norm_g, m_k_norm_g, m_w_mem_kv, m_w_x_out, m_w_out, m_g_mlp, m_w_up, m_w_down, v_g_mix, v_g_mem, v_w_in, v_conv_w, v_w_conv_out, v_w_sb_out, v_q_norm_g, v_k_norm_g, v_w_mem_kv, v_w_x_out, v_w_out, v_g_mlp, v_w_up, v_w_down):
    xpos, ypos, cpos = _mesh_pos()
    me = 4 * xpos + 2 * ypos + cpos
    x2d, mem2d, tgt2d = x[0], mem[0], loss_target[0]
    t = x2d.shape[0]
    tm = min(512, t)
    tm_s = min(256, t)

    big = {
        "w_in": (w_in[0], m_w_in[0], v_w_in[0]),
        "w_conv_out": (w_conv_out[0], m_w_conv_out[0], v_w_conv_out[0]),
        "w_sb_out": (w_sb_out[0], m_w_sb_out[0], v_w_sb_out[0]),
        "w_mem_kv": (w_mem_kv[0], m_w_mem_kv[0], v_w_mem_kv[0]),
        "w_x_out": (w_x_out[0], m_w_x_out[0], v_w_x_out[0]),
        "w_out": (w_out[0], m_w_out[0], v_w_out[0]),
        "w_up": (w_up[0], m_w_up[0], v_w_up[0]),
        "w_down": (w_down[0], m_w_down[0], v_w_down[0]),
    }
    late = [n for n in big if n != "w_in"]
    as_bf16 = lambda group: [big[n][0].astype(BF16) for n in group]
    conv_pad = jnp.pad(conv_w[0], ((0, 8 - 3), (0, 0)))

    proj, h, (w_in_all, conv_all) = _in_proj(x2d, g_mix, _arrival_blocks(xpos, ypos, cpos), tm,
                                             [big["w_in"][0].astype(BF16), conv_pad])
    conv_full = conv_all[:, :3, :].transpose(1, 0, 2).reshape(3, D_MODEL)
    a_conv = _conv_fwd(proj, conv_full, 256)
    tq = min(SB_QUERY_TILE, t)
    pairs = D_MODEL // SB_BLOCK

    def groups_t(cols):
        return cols.reshape(t // tq, tq, pairs, SB_BLOCK).transpose(2, 0, 3, 1)

    kt4 = groups_t(proj[:, COL_SK * D_MODEL:(COL_SK + 1) * D_MODEL])
    vt4 = groups_t(proj[:, COL_SV * D_MODEL:(COL_SV + 1) * D_MODEL])
    o_sb, sb_weights, sb_logits, gathered = _sb_fwd(proj, vt4, tq, as_bf16(late))
    full = dict(zip(late, gathered))
    wkv_all, w_up_all = full["w_mem_kv"], full["w_up"]
    rows_full = lambda a: a.reshape(a.shape[0] * a.shape[1], a.shape[2])
    wc, ws, wx, wo, wd = (rows_full(full[n]) for n in ("w_conv_out", "w_sb_out", "w_x_out", "w_out", "w_down"))
    mem_n, kn, vmem = _mem_prep(mem2d, g_mem, wkv_all, k_norm_g)
    o_x = _x_fwd(proj, q_norm_g, kn, vmem, tm_s)
    x1, y_conv, y_sb, y_x, merged = _merge_fwd(x2d, proj, a_conv, o_sb, o_x, wc, ws, wx, wo, tm_s)
    up, h2, dx2, lsum = _mlp_fwd(x1, g_mlp, w_up_all, wd, tgt2d, tm)

    dup, act, dx1, dg_mlp = _mlp_bwd(x1, g_mlp, w_up_all, wd, up, dx2, tm)
    dgate, dy_conv, dy_sb, dy_x, da_conv, do_sb, do_x = _merge_bwd(dx1, proj, y_conv, y_sb, y_x, wc, ws, wx, wo, tm_s)
    dch, dcb, dcc, dconv = _conv_bwd(proj, conv_full, da_conv, 256)
    wgrads = {
        "w_conv_out": _weight_grad(a_conv, dy_conv, D_MODEL, 512, "dw_conv_out"),
        "w_sb_out": _weight_grad(o_sb, dy_sb, D_MODEL, 512, "dw_sb_out"),
        "w_x_out": _weight_grad(o_x, dy_x, D_MODEL, 512, "dw_x_out"),
        "w_out": _weight_grad(merged, dx1, D_MODEL, 512, "dw_out"),
        "w_up": _weight_grad(h2, dup, w_up_all.shape[2], 512, "dw_up"),
        "w_down": _weight_grad(act, dx2, D_MODEL, 512, "dw_down"),
    }

    own_blocks = jnp.stack([4 * (xpos ^ dx) + 2 * (ypos ^ dy) + cpos for dx in (0, 1) for dy in (0, 1)]).astype(jnp.int32)
    sibling_blocks = own_blocks + (1 - 2 * cpos)
    blocked = lambda n: wgrads[n].reshape((N_DEV,) + big[n][0].shape)
    same_shape = ["w_conv_out", "w_sb_out", "w_x_out", "w_out"]

    def pair_sum(group, from_sibling):
        return dict(zip(group, _pair_sum([blocked(n) for n in group], from_sibling, own_blocks, "pair_sum_" + group[0])))

    behind_x = same_shape + ["w_up", "w_down"]
    dxq, dkn, dvm, dqg, from_sibling = _x_bwd(proj, q_norm_g, kn, vmem, do_x, tm_s, [blocked(n) for n in behind_x])
    pair_sums = pair_sum(same_shape, from_sibling[:4])
    pair_sums.update(pair_sum(["w_up"], from_sibling[4:5]))
    pair_sums.update(pair_sum(["w_down"], from_sibling[5:6]))
    dkv, dg_mem, dkg = _mem_bwd(mem2d, g_mem, wkv_all, k_norm_g, dkn, dvm)
    wgrads["w_mem_kv"] = _weight_grad(mem_n, dkv, wkv_all.shape[2], 512, "dw_mem_kv")
    pair_sums.update(pair_sum(["w_mem_kv"], _pair_exchange([blocked("w_mem_kv")], "grad_pair_exchange_w_mem_kv")))
    dq, dk, dv, from_chips_late = _sb_bwd(proj, kt4, do_sb, o_sb, sb_weights, sb_logits, tq,
                                          [pair_sums[n] for n in late])
    from_chips = dict(zip(late, from_chips_late))
    dproj = jnp.concatenate([dch, dcb, dcc, dq, dk, dv, dxq, dgate], axis=1)
    bw_in = w_in_all.shape[2]
    dw_in_sibling, _ = _weight_grad_blocks(h, dproj, bw_in, sibling_blocks, 512, "dw_in_sibling")
    dw_in_own, (dw_in_recv,) = _weight_grad_blocks(h, dproj, bw_in, own_blocks, 512, "dw_in_own", send=[dw_in_sibling])
    pair_sums["w_in"], = _pair_sum([dw_in_own], [dw_in_recv], jnp.arange(4, dtype=jnp.int32), "pair_sum_w_in")
    grad_x, dg_mix, (from_chips["w_in"],) = _in_proj_bwd(x2d, g_mix, w_in_all, dproj, dx1, tm, [pair_sums["w_in"]])
    res = {}
    for group in [same_shape] + [[n] for n in big if n not in same_shape]:
        updates = _adamw_sharded([(pair_sums[n], from_chips[n]) + big[n] for n in group], 256, "adamw_" + group[0])
        res.update(zip(group, updates))

    part = _pack_rows(dg_mix, dg_mem, dg_mlp, dqg, dkg, dconv, lsum)
    gsum = _small_sum(_small_all_gather(part))
    loss = 0.5 * jnp.sum(gsum[6 * SMALL_TILE]) / D_MODEL
    conv_cols = lax.dynamic_slice(gsum[5 * SMALL_TILE:6 * SMALL_TILE], (0, me * (D_MODEL // N_DEV)),
                                  (SMALL_TILE, D_MODEL // N_DEV))
    g_small = jnp.concatenate([gsum[:5 * SMALL_TILE], _pad_tile(conv_cols)], axis=0)
    w_small = _pack_small(g_mix, g_mem, g_mlp, q_norm_g, k_norm_g, conv_w[0])
    m_small = _pack_small(m_g_mix, m_g_mem, m_g_mlp, m_q_norm_g, m_k_norm_g, m_conv_w[0])
    v_small = _pack_small(v_g_mix, v_g_mem, v_g_mlp, v_q_norm_g, v_k_norm_g, v_conv_w[0])
    d_small, nm_small, nv_small = _adamw_small(w_small, g_small, m_small, v_small)

    def unpack(p):
        return {"g_mix": p[0:1], "g_mem": p[8:9], "g_mlp": p[16:17], "q_norm_g": p[24:25, :X_HEAD_DIM],
                "k_norm_g": p[32:33, :X_HEAD_DIM], "conv_w": p[40:43, :D_MODEL // N_DEV][None]}

    small = [unpack(p) for p in (g_small, d_small, nm_small, nv_small)]
    order = ["g_mix", "g_mem", "w_in", "conv_w", "w_conv_out", "w_sb_out", "q_norm_g", "k_norm_g", "w_mem_kv",
             "w_x_out", "w_out", "g_mlp", "w_up", "w_down"]
    outs = [loss, grad_x[None]]
    for kind in range(4):
        for n in order:
            outs.append(res[n][kind][None] if n in res else small[kind][n])
    return tuple(outs)
```

```python
import jax
import jax.numpy as jnp
from jax import lax
from jax.experimental import pallas as pl
from jax.experimental.pallas import tpu as pltpu

F32 = jnp.float32
BF16 = jnp.bfloat16
MESH = pl.DeviceIdType.MESH

EPS = 1e-6
N_DEV = 8
D_MODEL = 1024
SB_HEAD_DIM = 64
SB_BLOCK = 128
SB_QUERY_TILE = 512
X_HEADS = 4
X_HEAD_DIM = 256
N_BRANCH = 3
COL_CH, COL_CB, COL_CC, COL_SQ, COL_SK, COL_SV, COL_XQ, COL_GATE = 0, 1, 2, 3, 4, 5, 6, 7

ADAM_LR = 0.001
ADAM_B1 = 0.9
ADAM_B2 = 0.999
ADAM_EPS = 1e-08
ADAM_WD = 0.01
ADAM_STEP = 10

SMALL_TILE = 8


def _dot(a, b):
    return jnp.dot(a, b, preferred_element_type=F32)


def _dot_nt(a, b):
    return lax.dot_general(a, b, (((1,), (1,)), ((), ())), preferred_element_type=F32)


def _dot_tn(a, b):
    return lax.dot_general(a, b, (((0,), (0,)), ((), ())), preferred_element_type=F32)


def _rstd(xf):
    return lax.rsqrt(jnp.mean(xf * xf, axis=-1, keepdims=True) + EPS)


def _sigmoid(z):
    return 1.0 / (1.0 + jnp.exp(-z))


def _rms_bwd(dy, xhat, r, g):
    dxhat = dy * g
    return r * (dxhat - xhat * jnp.mean(dxhat * xhat, axis=-1, keepdims=True))


def _mesh_pos():
    return lax.axis_index("x"), lax.axis_index("y"), lax.axis_index("c")


ANY = pl.BlockSpec(memory_space=pl.ANY)


def _gather_shapes(shards):
    return [jax.ShapeDtypeStruct((N_DEV,) + s.shape, s.dtype) for s in shards]


def _gather_sems(n):
    return [pltpu.SemaphoreType.DMA((n, 7)), pltpu.SemaphoreType.DMA((n, 7)), pltpu.SemaphoreType.DMA((n,))]


def _gather_chips(x, y, c):
    return [(x ^ (1 - c), y ^ c), (x ^ c, y ^ (1 - c)), (1 - x, 1 - y)]


def _relayed_chip(chips, order):
    return chips[(1, 0, 2)[order - 4]]


def _gather_phases(ins, outs, send_sems, recv_sems, local_sems, by_arrival=False):
    n = len(ins)
    x, y, c = _mesh_pos()
    me, sibling = (x, y, c), (x, y, 1 - c)
    chips = _gather_chips(x, y, c)

    def blk(a, px, py, pc):
        return outs[a].at[4 * px + 2 * py + pc]

    def copy(a, k, block, to, src=None):
        return pltpu.make_async_remote_copy(
            src_ref=blk(a, *block) if src is None else src, dst_ref=blk(a, *block),
            send_sem=send_sems.at[a, k], recv_sem=recv_sems.at[a, k], device_id=to, device_id_type=MESH)

    def local(a):
        return pltpu.make_async_copy(ins[a], blk(a, *me), local_sems.at[a])

    def own(a):
        return [copy(a, 0, me, sibling, src=ins[a])] + [copy(a, 1 + j, me, (*chips[j], c), src=ins[a]) for j in range(2)]

    def onward(a):
        return copy(a, 3, (*chips[0], c), (*chips[1], c))

    def begin():
        for a in range(n):
            local(a).start()
        for a in range(n):
            for cp in own(a):
                cp.start()

    def arrive(order):
        for a in range(n):
            if order == 0:
                copy(a, 0, sibling, me).wait_recv()
            elif order <= 3:
                chip = chips[order - 1]
                copy(a, order, (*chip, c), me).wait_recv()
                copy(a, 3 + order, (*chip, c), sibling).start()
                if order == 1:
                    onward(a).start()
            else:
                copy(a, order, (*_relayed_chip(chips, order), 1 - c), me).wait_recv()

    def relay():
        for order in (1, 2):
            arrive(order)

    def drain():
        for a in range(n):
            for cp in own(a):
                cp.wait_send()
            onward(a).wait_send()
            for j, chip in enumerate(chips):
                copy(a, 4 + j, (*chip, c), sibling).wait_send()
            local(a).wait()

    def finish():
        for order in (3, 0, 4, 5, 6):
            arrive(order)
        drain()

    if by_arrival:
        return begin, arrive, drain
    return begin, relay, finish


def _pair_exchange(grads, name):
    n = len(grads)

    def body(*refs):
        begin, finish = _pair_exchange_phases(refs[:n], refs[n:2 * n], *refs[2 * n:])
        begin()
        finish()

    return pl.pallas_call(
        body, name=name,
        out_shape=_pair_exchange_shapes(grads),
        in_specs=[ANY] * n, out_specs=[ANY] * n,
        scratch_shapes=_pair_exchange_sems(n),
    )(*grads)


def _pair_exchange_shapes(grads):
    return [jax.ShapeDtypeStruct((4,) + g.shape[1:], g.dtype) for g in grads]


def _pair_exchange_sems(n):
    return [pltpu.SemaphoreType.DMA((n, 4)), pltpu.SemaphoreType.DMA((n, 4))]


def _pair_exchange_phases(ins, outs, send_sems, recv_sems, by_slot=False):
    x, y, c = _mesh_pos()
    xs, ys = (x, 1 - x), (y, 1 - y)

    def copies():
        out = []
        for a in range(len(ins)):
            for r in range(4):
                dx, dy = divmod(r, 2)
                out.append(pltpu.make_async_remote_copy(
                    src_ref=ins[a].at[r if by_slot else 4 * xs[dx] + 2 * ys[dy] + (1 - c)], dst_ref=outs[a].at[r],
                    send_sem=send_sems.at[a, r], recv_sem=recv_sems.at[a, r],
                    device_id=(x, y, 1 - c), device_id_type=MESH))
        return out

    def begin():
        for cp in copies():
            cp.start()

    def finish():
        for cp in copies():
            cp.wait()

    return begin, finish


def _chip_exchange_shapes(sums):
    return [jax.ShapeDtypeStruct((3,) + s.shape[1:], s.dtype) for s in sums]


def _chip_exchange_sems(n):
    return [pltpu.SemaphoreType.DMA((n, 3)), pltpu.SemaphoreType.DMA((n, 3))]


def _chip_exchange_phases(ins, outs, send_sems, recv_sems):
    x, y, c = _mesh_pos()
    xs, ys = (x, 1 - x), (y, 1 - y)

    def copies():
        out = []
        for a in range(len(ins)):
            for r in range(1, 4):
                dx, dy = divmod(r, 2)
                out.append(pltpu.make_async_remote_copy(
                    src_ref=ins[a].at[r], dst_ref=outs[a].at[r - 1],
                    send_sem=send_sems.at[a, r - 1], recv_sem=recv_sems.at[a, r - 1],
                    device_id=(xs[dx], ys[dy], c), device_id_type=MESH))
        return out

    def begin():
        for cp in copies():
            cp.start()

    def finish():
        for cp in copies():
            cp.wait()

    return begin, finish


def _small_all_gather(part):
    rows, cols = part.shape

    def body(in_ref, out_ref, send_sems, recv_sems):
        x, y, c = _mesh_pos()
        xs, ys, cs = (x, 1 - x), (y, 1 - y), (c, 1 - c)
        out_ref[4 * x + 2 * y + c] = in_ref[...]
        copies = []
        for k in range(1, N_DEV):
            dx, dy, dc = k // 4, (k // 2) % 2, k % 2
            copies.append((
                pltpu.make_async_remote_copy(
                    src_ref=in_ref, dst_ref=out_ref.at[4 * x + 2 * y + c],
                    send_sem=send_sems.at[k - 1], recv_sem=recv_sems.at[k - 1],
                    device_id=(xs[dx], ys[dy], cs[dc]), device_id_type=MESH),
                pltpu.make_async_remote_copy(
                    src_ref=in_ref, dst_ref=out_ref.at[4 * xs[dx] + 2 * ys[dy] + cs[dc]],
                    send_sem=send_sems.at[k - 1], recv_sem=recv_sems.at[k - 1],
                    device_id=(xs[dx], ys[dy], cs[dc]), device_id_type=MESH)))
        for send, _ in copies:
            send.start()
        for send, recv in copies:
            recv.wait_recv()
            send.wait_send()

    return pl.pallas_call(
        body, name="small_all_gather",
        out_shape=jax.ShapeDtypeStruct((N_DEV, rows, cols), part.dtype),
        in_specs=[pl.BlockSpec(memory_space=pltpu.VMEM)],
        out_specs=pl.BlockSpec(memory_space=pltpu.VMEM),
        scratch_shapes=[pltpu.SemaphoreType.DMA((N_DEV - 1,)), pltpu.SemaphoreType.DMA((N_DEV - 1,))],
    )(part)


ARRIVAL_ORDER = (0, 1, 4, 2, 5, 3, 6)


def _arrival_blocks(xpos, ypos, cpos):
    chips = _gather_chips(xpos, ypos, cpos)
    by_order = ([4 * xpos + 2 * ypos + (1 - cpos)] + [4 * cx + 2 * cy + cpos for cx, cy in chips]
                + [4 * cx + 2 * cy + (1 - cpos) for cx, cy in (_relayed_chip(chips, o) for o in (4, 5, 6))])
    return jnp.stack([4 * xpos + 2 * ypos + cpos] + [by_order[o] for o in ARRIVAL_ORDER]).astype(jnp.int32)


def _in_proj(x, g_mix, arrival_blocks, tm, shards):
    t, d = x.shape
    bw = shards[0].shape[1]
    nt = t // tm
    n = len(shards)

    def body(blocks_ref, x_ref, g_ref, *rest):
        w_shard = rest[0]
        proj_ref, h_ref = rest[n:n + 2]
        w_all = rest[n + 2]
        h_scr, w_buf, fetch_sems = rest[2 * n + 2:2 * n + 5]
        begin, arrive, drain = _gather_phases(rest[:n], rest[n + 2:2 * n + 2], *rest[2 * n + 5:], by_arrival=True)
        j, i = pl.program_id(0), pl.program_id(1)
        slot = lax.rem(j, 2)

        def fetch(src, into):
            return pltpu.make_async_copy(src, w_buf.at[into], fetch_sems.at[into])

        @pl.when((j == 0) & (i == 0))
        def _():
            begin()
            fetch(w_shard, 0).start()

        @pl.when(j == 0)
        def _():
            xf = x_ref[...]
            hv = (xf * _rstd(xf) * g_ref[...]).astype(BF16)
            h_ref[...] = hv
            h_scr[pl.ds(pl.multiple_of(i * tm, tm), tm), :] = hv

        @pl.when(i == 0)
        def _():
            fetch(w_shard, slot).wait()

        proj_ref[...] = _dot(h_scr[pl.ds(pl.multiple_of(i * tm, tm), tm), :], w_buf[slot]).astype(BF16)

        for nxt in range(1, N_DEV):
            @pl.when((i == 0) & (j == nxt - 1))
            def _():
                arrive(ARRIVAL_ORDER[nxt - 1])
                fetch(w_all.at[blocks_ref[nxt]], 1 - slot).start()

        pl.when((i == nt - 1) & (j == N_DEV - 1))(drain)

    first_pass = lambda j, i, blocks: (jnp.where(j == 0, i, nt - 1), 0)
    outs = pl.pallas_call(
        body, name="in_proj",
        grid_spec=pltpu.PrefetchScalarGridSpec(
            num_scalar_prefetch=1, grid=(N_DEV, nt),
            in_specs=[pl.BlockSpec((tm, d), first_pass), pl.BlockSpec((1, d), lambda j, i, blocks: (0, 0))] + [ANY] * n,
            out_specs=[pl.BlockSpec((tm, bw), lambda j, i, blocks: (i, blocks[j])),
                       pl.BlockSpec((tm, d), first_pass)] + [ANY] * n,
            scratch_shapes=[pltpu.VMEM((t, d), BF16), pltpu.VMEM((2, d, bw), BF16), pltpu.SemaphoreType.DMA((2,))]
            + _gather_sems(n)),
        out_shape=[jax.ShapeDtypeStruct((t, N_DEV * bw), BF16), jax.ShapeDtypeStruct((t, d), BF16)] + _gather_shapes(shards),
        compiler_params=pltpu.CompilerParams(dimension_semantics=("arbitrary", "arbitrary")),
    )(arrival_blocks, x, g_mix, *shards)
    return outs[0], outs[1], outs[2:]


def _conv_terms(ch_ref, cb_ref, cc_ref, w_ref):
    ch, cb, cc = ch_ref[...].astype(F32), cb_ref[...].astype(F32), cc_ref[...].astype(F32)
    u = cc * ch
    row = lax.broadcasted_iota(jnp.int32, u.shape, 0)
    u1 = jnp.where(row >= 1, pltpu.roll(u, 1, 0), 0.0)
    u2 = jnp.where(row >= 2, pltpu.roll(u, 2, 0), 0.0)
    w = (w_ref[0:1, :], w_ref[1:2, :], w_ref[2:3, :])
    cv = w[2] * u + w[1] * u1 + w[0] * u2
    return ch, cb, cc, u, u1, u2, cv, w, row


def _conv_fwd(proj, conv_w, cw):
    t = proj.shape[0]
    nper = D_MODEL // cw

    def body(ch_ref, cb_ref, cc_ref, w_ref, a_ref):
        _, cb, _, _, _, _, cv, _, _ = _conv_terms(ch_ref, cb_ref, cc_ref, w_ref)
        a_ref[...] = (cb * cv).astype(BF16)

    def col(piece):
        return pl.BlockSpec((t, cw), lambda j: (0, piece * nper + j))

    return pl.pallas_call(
        body, name="conv_fwd",
        grid=(nper,),
        in_specs=[col(COL_CH), col(COL_CB), col(COL_CC), pl.BlockSpec((3, cw), lambda j: (0, j))],
        out_specs=pl.BlockSpec((t, cw), lambda j: (0, j)),
        out_shape=jax.ShapeDtypeStruct((t, D_MODEL), BF16),
        compiler_params=pltpu.CompilerParams(dimension_semantics=("parallel",)),
    )(proj, proj, proj, conv_w)


def _scan_matrix(inclusive, value):
    s = lax.broadcasted_iota(jnp.int32, (SB_BLOCK, 2 * SB_BLOCK), 0)
    j = lax.rem(lax.broadcasted_iota(jnp.int32, (SB_BLOCK, 2 * SB_BLOCK), 1), SB_BLOCK)
    return jnp.where((j >= s) if inclusive else (j > s), value, 0.0).astype(BF16)


def _suffix_sum(u_mat, xv):
    hi = xv.astype(BF16)
    lo = (xv - hi.astype(F32)).astype(BF16)
    return _dot(u_mat, jnp.concatenate([hi, lo], axis=0))


def _head_rows(vt, h):
    row = lax.broadcasted_iota(jnp.int32, vt.shape, 0)
    return jnp.where((row >= h * SB_HEAD_DIM) & (row < (h + 1) * SB_HEAD_DIM), vt, 0.0).astype(BF16)


def _head_lanes(v, h):
    lane = lax.broadcasted_iota(jnp.int32, v.shape, 1)
    return jnp.where((lane >= h * SB_HEAD_DIM) & (lane < (h + 1) * SB_HEAD_DIM), v, 0.0).astype(BF16)


def _group_suffix(u_mat, xv, carry, negate=False):
    nblk = xv.shape[0] // SB_BLOCK
    parts = [None] * nblk
    for j in reversed(range(nblk)):
        xj = xv[j * SB_BLOCK:(j + 1) * SB_BLOCK]
        parts[j] = _suffix_sum(u_mat, xj) + carry
        total = jnp.sum(xj, axis=0, keepdims=True)
        carry = carry - total if negate else carry + total
    return jnp.concatenate(parts, axis=0), carry


def _sb_probs(kgrp, qt_h, carry, past):
    z = _dot(kgrp, qt_h)
    softplus = jnp.maximum(z, 0.0) + jnp.log(1.0 + jnp.exp(-jnp.abs(z)))
    if past is not None:
        softplus = jnp.where(past, softplus, 0.0)
    later, carry = _group_suffix(_scan_matrix(True, -1.0), softplus, carry, negate=True)
    a = jnp.exp(z + later)
    if past is not None:
        a = jnp.where(past, a, 0.0)
    return a, z, carry


def _sb_schedule(nq):
    steps = [(qi, g) for qi in range(nq) for g in range(qi, -1, -1)]
    return jnp.asarray([s[0] for s in steps], jnp.int32), jnp.asarray([s[1] for s in steps], jnp.int32)


def _past_mask(tq):
    return lax.broadcasted_iota(jnp.int32, (tq, tq), 0) < lax.broadcasted_iota(jnp.int32, (tq, tq), 1)


def _sb_fwd(proj, vt4, tq, shards):
    t = proj.shape[0]
    pairs = D_MODEL // SB_BLOCK
    nq = t // tq
    qi_tab, g_tab = _sb_schedule(nq)
    ns = qi_tab.shape[0]
    n = len(shards)

    def body(qi_ref, g_ref, q_ref, k_ref, vt_ref, *rest):
        o_ref, a_ref, z_ref = rest[n:n + 3]
        acc_ref, carry_ref, qt_ref = rest[2 * n + 3:2 * n + 6]
        begin, relay, finish = _gather_phases(rest[:n], rest[n + 3:2 * n + 3], *rest[2 * n + 6:])
        pi, si = pl.program_id(0), pl.program_id(1)
        diagonal = g_ref[si] == qi_ref[si]
        pl.when((pi == 0) & (si == 0))(begin)
        pl.when((pi == pairs // 2) & (si == 0))(relay)

        @pl.when(diagonal)
        def _():
            acc_ref[...] = jnp.zeros_like(acc_ref)
            carry_ref[...] = jnp.zeros_like(carry_ref)
            qt = q_ref[...].astype(F32).T * (SB_HEAD_DIM ** -0.5)
            for h in range(2):
                qt_ref[h] = _head_rows(qt, h)

        def step(past):
            for h in range(2):
                a, z, carry = _sb_probs(k_ref[...], qt_ref[h], carry_ref[h], past)
                ab = a.astype(BF16)
                a_ref[h] = ab
                z_ref[h] = z.astype(BF16)
                acc_ref[h] += _dot(vt_ref[h * SB_HEAD_DIM:(h + 1) * SB_HEAD_DIM, :], ab)
                carry_ref[h] = carry

        pl.when(diagonal)(lambda: step(_past_mask(tq)))
        pl.when(jnp.logical_not(diagonal))(lambda: step(None))

        @pl.when(g_ref[si] == 0)
        def _():
            o_ref[...] = jnp.concatenate([acc_ref[0], acc_ref[1]], axis=0).T

        pl.when((pi == pairs - 1) & (si == ns - 1))(finish)

    tile = pl.BlockSpec((None, None, 2, tq, tq), lambda p, s, qt_, gt_: (p, s, 0, 0, 0))
    tiles = jax.ShapeDtypeStruct((pairs, ns, 2, tq, tq), BF16)
    outs = pl.pallas_call(
        body, name="sb_fwd",
        grid_spec=pltpu.PrefetchScalarGridSpec(
            num_scalar_prefetch=2, grid=(pairs, ns),
            in_specs=[pl.BlockSpec((tq, SB_BLOCK), lambda p, s, qt_, gt_: (qt_[s], COL_SQ * pairs + p)),
                      pl.BlockSpec((tq, SB_BLOCK), lambda p, s, qt_, gt_: (gt_[s], COL_SK * pairs + p)),
                      pl.BlockSpec((None, None, SB_BLOCK, tq), lambda p, s, qt_, gt_: (p, gt_[s], 0, 0))] + [ANY] * n,
            out_specs=[pl.BlockSpec((tq, SB_BLOCK), lambda p, s, qt_, gt_: (qt_[s], p)), tile, tile] + [ANY] * n,
            scratch_shapes=[pltpu.VMEM((2, SB_HEAD_DIM, tq), F32), pltpu.VMEM((2, 1, tq), F32),
                            pltpu.VMEM((2, SB_BLOCK, tq), BF16)] + _gather_sems(n)),
        out_shape=[jax.ShapeDtypeStruct((t, D_MODEL), F32), tiles, tiles] + _gather_shapes(shards),
        compiler_params=pltpu.CompilerParams(dimension_semantics=("arbitrary", "arbitrary")),
    )(qi_tab, g_tab, proj, proj, vt4, *shards)
    return outs[0], outs[1], outs[2], outs[3:]


def _mem_prep(mem, g_mem, wkv_all, k_norm_g):
    m, d = mem.shape

    def body(mem_ref, g_ref, w_ref, kg_ref, memn_ref, kn_ref, v_ref):
        memf = mem_ref[...]
        memn = (memf * _rstd(memf) * g_ref[...]).astype(BF16)
        memn_ref[...] = memn
        for b in range(N_DEV):
            kv = _dot(memn, w_ref[b])
            if b < X_HEADS:
                kn_ref[:, b * X_HEAD_DIM:(b + 1) * X_HEAD_DIM] = (kv * _rstd(kv) * kg_ref[...]).astype(BF16)
            else:
                h = b - X_HEADS
                v_ref[:, h * X_HEAD_DIM:(h + 1) * X_HEAD_DIM] = kv.astype(BF16)

    return pl.pallas_call(
        body, name="mem_prep",
        out_shape=[jax.ShapeDtypeStruct((m, d), BF16)] * 3,
    )(mem, g_mem, wkv_all, k_norm_g)


def _x_head(xq_ref, qg, kn_ref, h):
    sl = slice(h * X_HEAD_DIM, (h + 1) * X_HEAD_DIM)
    q = xq_ref[:, sl].astype(F32)
    rq = _rstd(q)
    qhat = q * rq
    qn = (qhat * qg).astype(BF16)
    s = _dot_nt(qn, kn_ref[:, sl]) * (X_HEAD_DIM ** -0.5)
    e = jnp.exp(s - jnp.max(s, axis=-1, keepdims=True))
    p = e / jnp.sum(e, axis=-1, keepdims=True)
    return sl, rq, qhat, qn, p


def _x_fwd(proj, q_norm_g, kn, v, tm):
    t = proj.shape[0]
    m = kn.shape[0]

    def body(xq_ref, qg_ref, kn_ref, v_ref, o_ref):
        for h in range(X_HEADS):
            sl, _, _, _, p = _x_head(xq_ref, qg_ref[...], kn_ref, h)
            o_ref[:, sl] = _dot(p.astype(BF16), v_ref[:, sl]).astype(BF16)

    return pl.pallas_call(
        body, name="x_fwd",
        grid=(t // tm,),
        in_specs=[pl.BlockSpec((tm, D_MODEL), lambda i: (i, COL_XQ)),
                  pl.BlockSpec((1, X_HEAD_DIM), lambda i: (0, 0)),
                  pl.BlockSpec((m, D_MODEL), lambda i: (0, 0)),
                  pl.BlockSpec((m, D_MODEL), lambda i: (0, 0))],
        out_specs=pl.BlockSpec((tm, D_MODEL), lambda i: (i, 0)),
        out_shape=jax.ShapeDtypeStruct((t, D_MODEL), BF16),
        compiler_params=pltpu.CompilerParams(dimension_semantics=("parallel",)),
    )(proj, q_norm_g, kn, v)


def _gate_spec(tm, branch):
    return pl.BlockSpec((tm, D_MODEL), lambda i: (i, COL_GATE + branch))


def _merge_fwd(x, proj, a_conv, o_sb, o_x, w_conv_out, w_sb_out, w_x_out, w_out, tm):
    t, d = x.shape

    def body(x_ref, g0_ref, g1_ref, g2_ref, a_ref, s_ref, xo_ref, wc_ref, ws_ref, wx_ref, wo_ref,
             x1_ref, yc_ref, ys_ref, yx_ref, mg_ref):
        merged = jnp.zeros((tm, d), F32)
        for gate_ref, b_ref, w_ref, y_ref in ((g0_ref, a_ref, wc_ref, yc_ref), (g1_ref, s_ref, ws_ref, ys_ref),
                                              (g2_ref, xo_ref, wx_ref, yx_ref)):
            yv = _dot(b_ref[...].astype(BF16), w_ref[...])
            y_ref[...] = yv.astype(BF16)
            merged = merged + _sigmoid(gate_ref[...].astype(F32)) * yv
        mb = merged.astype(BF16)
        mg_ref[...] = mb
        x1_ref[...] = x_ref[...] + _dot(mb, wo_ref[...])

    tile = pl.BlockSpec((tm, d), lambda i: (i, 0))
    wfull = pl.BlockSpec((d, d), lambda i: (0, 0))
    return pl.pallas_call(
        body, name="merge_fwd",
        grid=(t // tm,),
        in_specs=[tile] + [_gate_spec(tm, b) for b in range(N_BRANCH)] + [tile, tile, tile,
                                                                           wfull, wfull, wfull, wfull],
        out_specs=[tile] * 5,
        out_shape=[jax.ShapeDtypeStruct((t, d), F32)] + [jax.ShapeDtypeStruct((t, d), BF16)] * 4,
        compiler_params=pltpu.CompilerParams(dimension_semantics=("parallel",)),
    )(x, proj, proj, proj, a_conv, o_sb, o_x, w_conv_out, w_sb_out, w_x_out, w_out)


def _mlp_fwd(x1, g_mlp, w_up_all, w_down, target, tm):
    t, d = x1.shape
    nb, _, fw = w_up_all.shape

    def body(x1_ref, g_ref, wu_ref, wd_ref, tgt_ref, up_ref, h2_ref, dx2_ref, lsum_ref, acc_ref):
        i, j = pl.program_id(0), pl.program_id(1)

        @pl.when(j == 0)
        def _():
            xf = x1_ref[...]
            h2_ref[...] = (xf * _rstd(xf) * g_ref[...]).astype(BF16)
            acc_ref[...] = jnp.zeros_like(acc_ref)

        @pl.when((i == 0) & (j == 0))
        def _():
            lsum_ref[...] = jnp.zeros_like(lsum_ref)

        up = _dot(h2_ref[...], wu_ref[...])
        up_ref[...] = up.astype(BF16)
        act = jnp.square(jnp.maximum(up, 0.0)).astype(BF16)
        acc_ref[...] += _dot(act, wd_ref[...])

        @pl.when(j == nb - 1)
        def _():
            diff = x1_ref[...] + acc_ref[...] - tgt_ref[...]
            dx2_ref[...] = diff * (1.0 / d)
            lsum_ref[...] += jnp.sum(diff * diff, axis=0, keepdims=True)

    tile = pl.BlockSpec((tm, d), lambda i, j: (i, 0))
    row = pl.BlockSpec((1, d), lambda i, j: (0, 0))
    return pl.pallas_call(
        body, name="mlp_fwd",
        grid=(t // tm, nb),
        in_specs=[tile, row, pl.BlockSpec((None, d, fw), lambda i, j: (j, 0, 0)),
                  pl.BlockSpec((fw, d), lambda i, j: (j, 0)), tile],
        out_specs=[pl.BlockSpec((tm, fw), lambda i, j: (i, j)), tile, tile, row],
        out_shape=[jax.ShapeDtypeStruct((t, nb * fw), BF16), jax.ShapeDtypeStruct((t, d), BF16),
                   jax.ShapeDtypeStruct((t, d), F32), jax.ShapeDtypeStruct((1, d), F32)],
        scratch_shapes=[pltpu.VMEM((tm, d), F32)],
        compiler_params=pltpu.CompilerParams(dimension_semantics=("arbitrary", "arbitrary")),
    )(x1, g_mlp, w_up_all, w_down, target)


def _mlp_bwd(x1, g_mlp, w_up_all, w_down, up, dx2, tm):
    t, d = x1.shape
    nb, _, fw = w_up_all.shape

    def body(x1_ref, g_ref, wu_ref, wd_ref, up_ref, dx2_ref, dup_ref, act_ref, dx1_ref, dg_ref, acc_ref, dyb_ref):
        i, j = pl.program_id(0), pl.program_id(1)

        @pl.when(j == 0)
        def _():
            dyb_ref[...] = dx2_ref[...].astype(BF16)
            acc_ref[...] = jnp.zeros_like(acc_ref)

        @pl.when((i == 0) & (j == 0))
        def _():
            dg_ref[...] = jnp.zeros_like(dg_ref)

        r = jnp.maximum(up_ref[...].astype(F32), 0.0)
        act_ref[...] = jnp.square(r).astype(BF16)
        dup = (_dot_nt(dyb_ref[...], wd_ref[...]) * (2.0 * r)).astype(BF16)
        dup_ref[...] = dup
        acc_ref[...] += _dot_nt(dup, wu_ref[...])

        @pl.when(j == nb - 1)
        def _():
            xf = x1_ref[...]
            rs = _rstd(xf)
            xhat = xf * rs
            dh2 = acc_ref[...]
            dg_ref[...] += jnp.sum(dh2 * xhat, axis=0, keepdims=True)
            dx1_ref[...] = dx2_ref[...] + _rms_bwd(dh2, xhat, rs, g_ref[...])

    tile = pl.BlockSpec((tm, d), lambda i, j: (i, 0))
    row = pl.BlockSpec((1, d), lambda i, j: (0, 0))
    ff = pl.BlockSpec((tm, fw), lambda i, j: (i, j))
    return pl.pallas_call(
        body, name="mlp_bwd",
        grid=(t // tm, nb),
        in_specs=[tile, row, pl.BlockSpec((None, d, fw), lambda i, j: (j, 0, 0)),
                  pl.BlockSpec((fw, d), lambda i, j: (j, 0)), ff, tile],
        out_specs=[ff, ff, tile, row],
        out_shape=[jax.ShapeDtypeStruct((t, nb * fw), BF16), jax.ShapeDtypeStruct((t, nb * fw), BF16),
                   jax.ShapeDtypeStruct((t, d), F32), jax.ShapeDtypeStruct((1, d), F32)],
        scratch_shapes=[pltpu.VMEM((tm, d), F32), pltpu.VMEM((tm, d), BF16)],
        compiler_params=pltpu.CompilerParams(dimension_semantics=("arbitrary", "arbitrary")),
    )(x1, g_mlp, w_up_all, w_down, up, dx2)


def _merge_bwd(dx1, proj, y_conv, y_sb, y_x, w_conv_out, w_sb_out, w_x_out, w_out, tm):
    t, d = dx1.shape

    def body(dx1_ref, g0_ref, g1_ref, g2_ref, yc_ref, ys_ref, yx_ref, wc_ref, ws_ref, wx_ref, wo_ref,
             dgate_ref, dyc_ref, dys_ref, dyx_ref, da_ref, dos_ref, dox_ref):
        dm = _dot_nt(dx1_ref[...].astype(BF16), wo_ref[...])
        for i, (gate_ref, y_ref, w_ref, dy_ref, db_ref) in enumerate(((g0_ref, yc_ref, wc_ref, dyc_ref, da_ref),
                                                                       (g1_ref, ys_ref, ws_ref, dys_ref, dos_ref),
                                                                       (g2_ref, yx_ref, wx_ref, dyx_ref, dox_ref))):
            gt = _sigmoid(gate_ref[...].astype(F32))
            dy = (dm * gt).astype(BF16)
            dy_ref[...] = dy
            dgate_ref[:, i * d:(i + 1) * d] = (dm * y_ref[...].astype(F32) * gt * (1.0 - gt)).astype(BF16)
            db_ref[...] = _dot_nt(dy, w_ref[...]).astype(BF16)

    tile = pl.BlockSpec((tm, d), lambda i: (i, 0))
    wfull = pl.BlockSpec((d, d), lambda i: (0, 0))
    return pl.pallas_call(
        body, name="merge_bwd",
        grid=(t // tm,),
        in_specs=[tile] + [_gate_spec(tm, b) for b in range(N_BRANCH)] + [tile, tile, tile,
                                                                           wfull, wfull, wfull, wfull],
        out_specs=[pl.BlockSpec((tm, N_BRANCH * d), lambda i: (i, 0))] + [tile] * 6,
        out_shape=[jax.ShapeDtypeStruct((t, N_BRANCH * d), BF16)] + [jax.ShapeDtypeStruct((t, d), BF16)] * 6,
        compiler_params=pltpu.CompilerParams(dimension_semantics=("parallel",)),
    )(dx1, proj, proj, proj, y_conv, y_sb, y_x, w_conv_out, w_sb_out, w_x_out, w_out)


def _conv_bwd(proj, conv_w, da, cw):
    t = proj.shape[0]
    nper = D_MODEL // cw

    def body(ch_ref, cb_ref, cc_ref, w_ref, da_ref, dch_ref, dcb_ref, dcc_ref, dw_ref):
        ch, cb, cc, u, u1, u2, cv, w, row = _conv_terms(ch_ref, cb_ref, cc_ref, w_ref)
        dav = da_ref[...].astype(F32)
        dcb_ref[...] = (dav * cv).astype(BF16)
        dcv = dav * cb
        n1 = jnp.where(row < t - 1, pltpu.roll(dcv, t - 1, 0), 0.0)
        n2 = jnp.where(row < t - 2, pltpu.roll(dcv, t - 2, 0), 0.0)
        du = w[2] * dcv + w[1] * n1 + w[0] * n2
        dcc_ref[...] = (du * ch).astype(BF16)
        dch_ref[...] = (du * cc).astype(BF16)
        dw_ref[0:1, :] = jnp.sum(dcv * u2, axis=0, keepdims=True)
        dw_ref[1:2, :] = jnp.sum(dcv * u1, axis=0, keepdims=True)
        dw_ref[2:3, :] = jnp.sum(dcv * u, axis=0, keepdims=True)

    def col(piece):
        return pl.BlockSpec((t, cw), lambda j: (0, piece * nper + j))

    out_col = pl.BlockSpec((t, cw), lambda j: (0, j))
    wspec = pl.BlockSpec((3, cw), lambda j: (0, j))
    return pl.pallas_call(
        body, name="conv_bwd",
        grid=(nper,),
        in_specs=[col(COL_CH), col(COL_CB), col(COL_CC), wspec, out_col],
        out_specs=[out_col, out_col, out_col, wspec],
        out_shape=[jax.ShapeDtypeStruct((t, D_MODEL), BF16)] * 3 + [jax.ShapeDtypeStruct((3, D_MODEL), F32)],
        compiler_params=pltpu.CompilerParams(dimension_semantics=("parallel",)),
    )(proj, proj, proj, conv_w, da)


def _sb_bwd(proj, kt4, do_sb, o_sb, weights, logits, tq, pair_sums):
    t = proj.shape[0]
    nq = t // tq
    pairs = D_MODEL // SB_BLOCK
    scale = SB_HEAD_DIM ** -0.5
    qi_tab, g_tab = _sb_schedule(nq)
    ns = qi_tab.shape[0]
    n = len(pair_sums)

    def body(qi_ref, g_ref, q_ref, v_ref, kt_ref, do_ref, o_ref, a_ref, z_ref, *rest):
        dq_ref, dk_ref, dv_ref = rest[n:n + 3]
        dk_acc, dv_acc, dqt_ref, carry_ref, qm_ref, dom_ref, dot_ref, dsum_ref = rest[2 * n + 3:2 * n + 11]
        begin, finish = _chip_exchange_phases(rest[:n], rest[n + 3:2 * n + 3], *rest[2 * n + 11:])
        pi, si = pl.program_id(0), pl.program_id(1)
        diagonal = g_ref[si] == qi_ref[si]
        pl.when((pi == 0) & (si == 0))(begin)

        @pl.when(si == 0)
        def _():
            dk_acc[...] = jnp.zeros_like(dk_acc)
            dv_acc[...] = jnp.zeros_like(dv_acc)

        @pl.when(diagonal)
        def _():
            dqt_ref[...] = jnp.zeros_like(dqt_ref)
            carry_ref[...] = jnp.zeros_like(carry_ref)
            q = q_ref[...].astype(F32) * scale
            do = do_ref[...].astype(F32)
            dot_ = do.T
            prod = dot_ * o_ref[...].T
            for h in range(2):
                rows = slice(h * SB_HEAD_DIM, (h + 1) * SB_HEAD_DIM)
                qm_ref[h] = _head_lanes(q, h)
                dom_ref[h] = _head_lanes(do, h)
                dot_ref[h] = _head_rows(dot_, h)
                dsum_ref[h] = jnp.sum(prod[rows, :], axis=0, keepdims=True)

        def step(past):
            u_mat = _scan_matrix(False, 1.0)
            ks = pl.multiple_of(g_ref[si] * tq, tq)
            dk_add = jnp.zeros((tq, SB_BLOCK), F32)
            dv_add = jnp.zeros((tq, SB_BLOCK), F32)
            for h in range(2):
                rows = slice(h * SB_HEAD_DIM, (h + 1) * SB_HEAD_DIM)
                ab = a_ref[h]
                gw = _dot(v_ref[...], dot_ref[h]) * ab.astype(F32)
                after, carry = _group_suffix(u_mat, gw, carry_ref[h])
                sig = pl.reciprocal(1.0 + jnp.exp(-z_ref[h].astype(F32)), approx=True)
                dz = gw - sig * (dsum_ref[h] - after)
                if past is not None:
                    dz = jnp.where(past, dz, 0.0)
                dzb = dz.astype(BF16)
                dqt_ref[h] += _dot(kt_ref[rows, :], dzb)
                dk_add = dk_add + _dot(dzb, qm_ref[h])
                dv_add = dv_add + _dot(ab, dom_ref[h])
                carry_ref[h] = carry
            dk_acc[pl.ds(ks, tq), :] += dk_add
            dv_acc[pl.ds(ks, tq), :] += dv_add

        pl.when(diagonal)(lambda: step(_past_mask(tq)))
        pl.when(jnp.logical_not(diagonal))(lambda: step(None))

        @pl.when(g_ref[si] == 0)
        def _():
            dq_ref[...] = (jnp.concatenate([dqt_ref[0], dqt_ref[1]], axis=0).T * scale).astype(BF16)

        @pl.when(si == ns - 1)
        def _():
            dk_ref[...] = dk_acc[...].astype(BF16)
            dv_ref[...] = dv_acc[...].astype(BF16)

        pl.when((pi == pairs - 1) & (si == ns - 1))(finish)

    qblk = lambda base: pl.BlockSpec((tq, SB_BLOCK), lambda p, s, qt_, gt_: (qt_[s], base * pairs + p))
    kgrp = lambda base: pl.BlockSpec((tq, SB_BLOCK), lambda p, s, qt_, gt_: (gt_[s], base * pairs + p))
    seq = pl.BlockSpec((t, SB_BLOCK), lambda p, s, qt_, gt_: (0, p))
    tr = pl.BlockSpec((None, None, SB_BLOCK, tq), lambda p, s, qt_, gt_: (p, gt_[s], 0, 0))
    tile = pl.BlockSpec((None, None, 2, tq, tq), lambda p, s, qt_, gt_: (p, s, 0, 0, 0))
    outs = pl.pallas_call(
        body, name="sb_bwd",
        grid_spec=pltpu.PrefetchScalarGridSpec(
            num_scalar_prefetch=2, grid=(pairs, ns),
            in_specs=[qblk(COL_SQ), kgrp(COL_SV), tr, qblk(0), qblk(0), tile, tile] + [ANY] * n,
            out_specs=[qblk(0), seq, seq] + [ANY] * n,
            scratch_shapes=[pltpu.VMEM((t, SB_BLOCK), F32), pltpu.VMEM((t, SB_BLOCK), F32),
                            pltpu.VMEM((2, SB_HEAD_DIM, tq), F32), pltpu.VMEM((2, 1, tq), F32),
                            pltpu.VMEM((2, tq, SB_BLOCK), BF16), pltpu.VMEM((2, tq, SB_BLOCK), BF16),
                            pltpu.VMEM((2, SB_BLOCK, tq), BF16), pltpu.VMEM((2, 1, tq), F32)] + _chip_exchange_sems(n)),
        out_shape=[jax.ShapeDtypeStruct((t, D_MODEL), BF16)] * 3 + _chip_exchange_shapes(pair_sums),
        compiler_params=pltpu.CompilerParams(dimension_semantics=("arbitrary", "arbitrary")),
    )(qi_tab, g_tab, proj, proj, kt4, do_sb, o_sb, weights, logits, *pair_sums)
    return outs[0], outs[1], outs[2], outs[3:]


def _x_bwd(proj, q_norm_g, kn, v, do_x, tm, grads):
    t = proj.shape[0]
    m = kn.shape[0]
    scale = X_HEAD_DIM ** -0.5
    nt = t // tm
    n = len(grads)

    def body(xq_ref, qg_ref, kn_ref, v_ref, do_ref, *rest):
        dxq_ref, dkn_ref, dv_ref, dqg_ref = rest[n:n + 4]
        begin, finish = _pair_exchange_phases(rest[:n], rest[n + 4:2 * n + 4], *rest[2 * n + 4:])

        @pl.when(pl.program_id(0) == 0)
        def _():
            begin()
            dkn_ref[...] = jnp.zeros_like(dkn_ref)
            dv_ref[...] = jnp.zeros_like(dv_ref)
            dqg_ref[...] = jnp.zeros_like(dqg_ref)

        qg = qg_ref[...]
        for h in range(X_HEADS):
            sl, rq, qhat, qn, p = _x_head(xq_ref, qg, kn_ref, h)
            do_h = do_ref[:, sl]
            dp = _dot_nt(do_h, v_ref[:, sl])
            ds = (p * (dp - jnp.sum(dp * p, axis=-1, keepdims=True)) * scale).astype(BF16)
            dqn = _dot(ds, kn_ref[:, sl])
            dkn_ref[:, sl] += _dot_tn(ds, qn)
            dv_ref[:, sl] += _dot_tn(p.astype(BF16), do_h)
            dqg_ref[...] += jnp.sum(dqn * qhat, axis=0, keepdims=True)
            dxq_ref[:, sl] = _rms_bwd(dqn, qhat, rq, qg).astype(BF16)

        pl.when(pl.program_id(0) == nt - 1)(finish)

    full = pl.BlockSpec((m, D_MODEL), lambda i: (0, 0))
    gain = pl.BlockSpec((1, X_HEAD_DIM), lambda i: (0, 0))
    tile = pl.BlockSpec((tm, D_MODEL), lambda i: (i, 0))
    outs = pl.pallas_call(
        body, name="x_bwd",
        grid=(nt,),
        in_specs=[pl.BlockSpec((tm, D_MODEL), lambda i: (i, COL_XQ)), gain, full, full, tile] + [ANY] * n,
        out_specs=[tile, full, full, gain] + [ANY] * n,
        out_shape=[jax.ShapeDtypeStruct((t, D_MODEL), BF16), jax.ShapeDtypeStruct((m, D_MODEL), F32),
                   jax.ShapeDtypeStruct((m, D_MODEL), F32), jax.ShapeDtypeStruct((1, X_HEAD_DIM), F32)]
        + _pair_exchange_shapes(grads),
        scratch_shapes=_pair_exchange_sems(n),
        compiler_params=pltpu.CompilerParams(dimension_semantics=("arbitrary",)),
    )(proj, q_norm_g, kn, v, do_x, *grads)
    return outs[0], outs[1], outs[2], outs[3], outs[4:]


def _mem_bwd(mem, g_mem, wkv_all, k_norm_g, dkn, dv):
    m, d = mem.shape

    def body(mem_ref, g_ref, w_ref, kg_ref, dkn_ref, dv_ref, dkv_ref, dgm_ref, dkg_ref):
        memf = mem_ref[...]
        mem_hat = memf * _rstd(memf)
        memn = (mem_hat * g_ref[...]).astype(BF16)
        kg = kg_ref[...]
        dmemn = jnp.zeros((m, d), F32)
        dkg = jnp.zeros((1, X_HEAD_DIM), F32)
        for b in range(N_DEV):
            sl = slice(b * X_HEAD_DIM, (b + 1) * X_HEAD_DIM)
            if b < X_HEADS:
                kv = _dot(memn, w_ref[b])
                rk = _rstd(kv)
                khat = kv * rk
                dkn_h = dkn_ref[:, sl]
                dkg = dkg + jnp.sum(dkn_h * khat, axis=0, keepdims=True)
                dblk = _rms_bwd(dkn_h, khat, rk, kg).astype(BF16)
            else:
                hs = slice((b - X_HEADS) * X_HEAD_DIM, (b - X_HEADS + 1) * X_HEAD_DIM)
                dblk = dv_ref[:, hs].astype(BF16)
            dkv_ref[:, sl] = dblk
            dmemn = dmemn + _dot_nt(dblk, w_ref[b])
        dgm_ref[...] = jnp.sum(dmemn * mem_hat, axis=0, keepdims=True)
        dkg_ref[...] = dkg

    return pl.pallas_call(
        body, name="mem_bwd",
        out_shape=[jax.ShapeDtypeStruct((m, 2 * d), BF16), jax.ShapeDtypeStruct((1, d), F32),
                   jax.ShapeDtypeStruct((1, X_HEAD_DIM), F32)],
    )(mem, g_mem, wkv_all, k_norm_g, dkn, dv)


def _in_proj_bwd(x, g_mix, w_in_all, dproj, dx1, tm, pair_sums):
    t, d = x.shape
    nb, _, bw = w_in_all.shape
    nt = t // tm
    n = len(pair_sums)

    def body(x_ref, g_ref, w_ref, dp_ref, dx1_ref, *rest):
        dx_ref, dg_ref = rest[n:n + 2]
        acc_ref = rest[2 * n + 2]
        begin, finish = _chip_exchange_phases(rest[:n], rest[n + 2:2 * n + 2], *rest[2 * n + 3:])
        i, j = pl.program_id(0), pl.program_id(1)
        pl.when((i == 0) & (j == 0))(begin)

        @pl.when(j == 0)
        def _():
            acc_ref[...] = jnp.zeros_like(acc_ref)

        @pl.when((i == 0) & (j == 0))
        def _():
            dg_ref[...] = jnp.zeros_like(dg_ref)

        acc_ref[...] += _dot_nt(dp_ref[...], w_ref[...])

        @pl.when(j == nb - 1)
        def _():
            xf = x_ref[...]
            rs = _rstd(xf)
            xhat = xf * rs
            dh = acc_ref[...]
            dg_ref[...] += jnp.sum(dh * xhat, axis=0, keepdims=True)
            dx_ref[...] = dx1_ref[...] + _rms_bwd(dh, xhat, rs, g_ref[...])

        pl.when((i == nt - 1) & (j == nb - 1))(finish)

    tile = pl.BlockSpec((tm, d), lambda i, j: (i, 0))
    row = pl.BlockSpec((1, d), lambda i, j: (0, 0))
    outs = pl.pallas_call(
        body, name="in_proj_bwd",
        grid=(nt, nb),
        in_specs=[tile, row, pl.BlockSpec((None, d, bw), lambda i, j: (j, 0, 0)),
                  pl.BlockSpec((tm, bw), lambda i, j: (i, j)), tile] + [ANY] * n,
        out_specs=[tile, row] + [ANY] * n,
        out_shape=[jax.ShapeDtypeStruct((t, d), F32), jax.ShapeDtypeStruct((1, d), F32)] + _chip_exchange_shapes(pair_sums),
        scratch_shapes=[pltpu.VMEM((tm, d), F32)] + _chip_exchange_sems(n),
        compiler_params=pltpu.CompilerParams(dimension_semantics=("arbitrary", "arbitrary")),
    )(x, g_mix, w_in_all, dproj, dx1, *pair_sums)
    return outs[0], outs[1], outs[2:]


def _weight_grad(a, b, bw, tmm, name):
    t, m = a.shape
    n = b.shape[1]
    tmm = min(tmm, m)

    def body(a_ref, b_ref, o_ref):
        o_ref[...] = _dot_tn(a_ref[...].astype(BF16), b_ref[...].astype(BF16)).astype(BF16)

    return pl.pallas_call(
        body, name=name,
        grid=(m // tmm, n // bw),
        in_specs=[pl.BlockSpec((t, tmm), lambda i, j: (0, i)), pl.BlockSpec((t, bw), lambda i, j: (0, j))],
        out_specs=pl.BlockSpec((None, tmm, bw), lambda i, j: (j, i, 0)),
        out_shape=jax.ShapeDtypeStruct((n // bw, m, bw), BF16),
        compiler_params=pltpu.CompilerParams(dimension_semantics=("parallel", "parallel")),
    )(a, b)


def _weight_grad_blocks(a, b, bw, blocks, tmm, name, send=()):
    t, m = a.shape
    tmm = min(tmm, m)
    nm = m // tmm
    n = len(send)

    def body(blocks_ref, a_ref, b_ref, *rest):
        o_ref = rest[n]
        i, j = pl.program_id(0), pl.program_id(1)
        if n:
            begin, finish = _pair_exchange_phases(rest[:n], rest[n + 1:2 * n + 1], *rest[2 * n + 1:], by_slot=True)
            pl.when((i == 0) & (j == 0))(begin)
        o_ref[...] = _dot_tn(a_ref[...].astype(BF16), b_ref[...].astype(BF16)).astype(BF16)
        if n:
            pl.when((i == nm - 1) & (j == 3))(finish)

    outs = pl.pallas_call(
        body, name=name,
        grid_spec=pltpu.PrefetchScalarGridSpec(
            num_scalar_prefetch=1, grid=(nm, 4),
            in_specs=[pl.BlockSpec((t, tmm), lambda i, j, blk: (0, i)),
                      pl.BlockSpec((t, bw), lambda i, j, blk: (0, blk[j]))] + [ANY] * n,
            out_specs=[pl.BlockSpec((None, tmm, bw), lambda i, j, blk: (j, i, 0))] + [ANY] * n,
            scratch_shapes=_pair_exchange_sems(n) if n else []),
        out_shape=[jax.ShapeDtypeStruct((4, m, bw), BF16)] + _pair_exchange_shapes(send),
        compiler_params=pltpu.CompilerParams(dimension_semantics=("arbitrary", "arbitrary")),
    )(blocks, a, b, *send)
    return outs[0], outs[1:]


def _pair_sum(grads, recvs, own_blocks, name):
    k = len(grads)
    _, rows, cols = grads[0].shape

    def body(idx_ref, *refs):
        for a in range(k):
            refs[2 * k + a][...] = (refs[a][...].astype(F32) + refs[k + a][...].astype(F32)).astype(BF16)

    slot = pl.BlockSpec((None, rows, cols), lambda r, idx: (r, 0, 0))
    return pl.pallas_call(
        body, name=name,
        grid_spec=pltpu.PrefetchScalarGridSpec(
            num_scalar_prefetch=1, grid=(4,),
            in_specs=[pl.BlockSpec((None, rows, cols), lambda r, idx: (idx[r], 0, 0))] * k + [slot] * k,
            out_specs=[slot] * k),
        out_shape=[jax.ShapeDtypeStruct((4, rows, cols), BF16)] * k,
        compiler_params=pltpu.CompilerParams(dimension_semantics=("parallel",)),
    )(own_blocks, *grads, *recvs)


def _adamw_math(w, g, m, v):
    m = ADAM_B1 * m + (1.0 - ADAM_B1) * g
    v = ADAM_B2 * v + (1.0 - ADAM_B2) * jnp.square(g)
    m_hat = m / (1.0 - ADAM_B1 ** ADAM_STEP)
    v_hat = v / (1.0 - ADAM_B2 ** ADAM_STEP)
    delta = -ADAM_LR * (m_hat / (jnp.sqrt(v_hat) + ADAM_EPS) + ADAM_WD * w)
    return delta, m, v


def _adamw_sharded(shards, tr, name):
    k = len(shards)
    rows, cols = shards[0][2].shape
    tr = min(tr, rows)

    def body(*refs):
        for a in range(k):
            h_ref, r_ref, w_ref, m_ref, v_ref = refs[5 * a:5 * a + 5]
            g_out, d_out, m_out, v_out = refs[5 * k + 4 * a:5 * k + 4 * a + 4]
            g = h_ref[...].astype(F32)
            for r in range(3):
                g = g + r_ref[r].astype(F32)
            g_out[...] = g
            d_out[...], m_out[...], v_out[...] = _adamw_math(w_ref[...], g, m_ref[...], v_ref[...])

    tile = pl.BlockSpec((tr, cols), lambda i: (i, 0))
    outs = pl.pallas_call(
        body, name=name,
        grid=(rows // tr,),
        in_specs=[pl.BlockSpec((None, tr, cols), lambda i: (0, i, 0)),
                  pl.BlockSpec((3, tr, cols), lambda i: (0, i, 0)), tile, tile, tile] * k,
        out_specs=[tile] * (4 * k),
        out_shape=[jax.ShapeDtypeStruct((rows, cols), F32)] * (4 * k),
        compiler_params=pltpu.CompilerParams(dimension_semantics=("parallel",)),
    )(*[op for shard in shards for op in shard])
    return [tuple(outs[4 * a:4 * a + 4]) for a in range(k)]


SMALL_ROWS = 16


def _pack_rows(dg_mix, dg_mem, dg_mlp, dqg, dkg, dconv, lsum):
    def body(a_ref, b_ref, c_ref, q_ref, k_ref, cv_ref, l_ref, o_ref):
        o_ref[...] = jnp.zeros_like(o_ref)
        for r, ref in enumerate((a_ref, b_ref, c_ref)):
            o_ref[r:r + 1, :] = ref[...]
        o_ref[3:4, :X_HEAD_DIM] = q_ref[...]
        o_ref[4:5, :X_HEAD_DIM] = k_ref[...]
        o_ref[5:8, :] = cv_ref[...]
        o_ref[8:9, :] = l_ref[...]

    return pl.pallas_call(body, name="small_pack", out_shape=jax.ShapeDtypeStruct((SMALL_ROWS, D_MODEL), F32))(
        dg_mix, dg_mem, dg_mlp, dqg, dkg, dconv, lsum)


def _small_sum(gathered):
    def body(g_ref, o_ref):
        total = g_ref[0]
        for dev in range(1, N_DEV):
            total = total + g_ref[dev]
        o_ref[...] = jnp.zeros_like(o_ref)
        for piece in range(5):
            o_ref[piece * SMALL_TILE:piece * SMALL_TILE + 1, :] = total[piece:piece + 1]
        o_ref[5 * SMALL_TILE:5 * SMALL_TILE + 3, :] = total[5:8]
        o_ref[6 * SMALL_TILE:6 * SMALL_TILE + 1, :] = total[8:9]

    return pl.pallas_call(body, name="small_grad_sum",
                          out_shape=jax.ShapeDtypeStruct((7 * SMALL_TILE, D_MODEL), F32))(gathered)


def _adamw_small(w, g, m, v):
    def body(w_ref, g_ref, m_ref, v_ref, d_out, m_out, v_out):
        d_out[...], m_out[...], v_out[...] = _adamw_math(w_ref[...], g_ref[...], m_ref[...], v_ref[...])

    return pl.pallas_call(body, name="adamw_small", out_shape=[jax.ShapeDtypeStruct(w.shape, F32)] * 3)(w, g, m, v)


def _pad_tile(a):
    return jnp.pad(a, ((0, SMALL_TILE - a.shape[0]), (0, D_MODEL - a.shape[1])))


def _pack_small(*pieces):
    return jnp.concatenate([_pad_tile(a) for a in pieces], axis=0)


def kernel(x, mem, g_mix, g_mem, w_in, conv_w, w_conv_out, w_sb_out, q_norm_g, k_norm_g, w_mem_kv, w_x_out, w_out, g_mlp, w_up, w_down, loss_target, m_g_mix, m_g_mem, m_w_in, m_conv_w, m_w_conv_out, m_w_sb_out, m_q_norm_g, m_k_norm_g, m_w_mem_kv, m_w_x_out, m_w_out, m_g_mlp, m_w_up, m_w_down, v_g_mix, v_g_mem, v_w_in, v_conv_w, v_w_conv_out, v_w_sb_out, v_q_norm_g, v_k_norm_g, v_w_mem_kv, v_w_x_out, v_w_out, v_g_mlp, v_w_up, v_w_down):
    xpos, ypos, cpos = _mesh_pos()
    me = 4 * xpos + 2 * ypos + cpos
    x2d, mem2d, tgt2d = x[0], mem[0], loss_target[0]
    t = x2d.shape[0]
    tm = min(512, t)
    tm_s = min(256, t)

    big = {
        "w_in": (w_in[0], m_w_in[0], v_w_in[0]),
        "w_conv_out": (w_conv_out[0], m_w_conv_out[0], v_w_conv_out[0]),
        "w_sb_out": (w_sb_out[0], m_w_sb_out[0], v_w_sb_out[0]),
        "w_mem_kv": (w_mem_kv[0], m_w_mem_kv[0], v_w_mem_kv[0]),
        "w_x_out": (w_x_out[0], m_w_x_out[0], v_w_x_out[0]),
        "w_out": (w_out[0], m_w_out[0], v_w_out[0]),
        "w_up": (w_up[0], m_w_up[0], v_w_up[0]),
        "w_down": (w_down[0], m_w_down[0], v_w_down[0]),
    }
    late = [n for n in big if n != "w_in"]
    as_bf16 = lambda group: [big[n][0].astype(BF16) for n in group]
    conv_pad = jnp.pad(conv_w[0], ((0, 8 - 3), (0, 0)))

    proj, h, (w_in_all, conv_all) = _in_proj(x2d, g_mix, _arrival_blocks(xpos, ypos, cpos), tm,
                                             [big["w_in"][0].astype(BF16), conv_pad])
    conv_full = conv_all[:, :3, :].transpose(1, 0, 2).reshape(3, D_MODEL)
    a_conv = _conv_fwd(proj, conv_full, 256)
    tq = min(SB_QUERY_TILE, t)
    pairs = D_MODEL // SB_BLOCK

    def groups_t(cols):
        return cols.reshape(t // tq, tq, pairs, SB_BLOCK).transpose(2, 0, 3, 1)

    kt4 = groups_t(proj[:, COL_SK * D_MODEL:(COL_SK + 1) * D_MODEL])
    vt4 = groups_t(proj[:, COL_SV * D_MODEL:(COL_SV + 1) * D_MODEL])
    o_sb, sb_weights, sb_logits, gathered = _sb_fwd(proj, vt4, tq, as_bf16(late))
    full = dict(zip(late, gathered))
    wkv_all, w_up_all = full["w_mem_kv"], full["w_up"]
    rows_full = lambda a: a.reshape(a.shape[0] * a.shape[1], a.shape[2])
    wc, ws, wx, wo, wd = (rows_full(full[n]) for n in ("w_conv_out", "w_sb_out", "w_x_out", "w_out", "w_down"))
    mem_n, kn, vmem = _mem_prep(mem2d, g_mem, wkv_all, k_norm_g)
    o_x = _x_fwd(proj, q_norm_g, kn, vmem, tm_s)
    x1, y_conv, y_sb, y_x, merged = _merge_fwd(x2d, proj, a_conv, o_sb, o_x, wc, ws, wx, wo, tm_s)
    up, h2, dx2, lsum = _mlp_fwd(x1, g_mlp, w_up_all, wd, tgt2d, tm)

    dup, act, dx1, dg_mlp = _mlp_bwd(x1, g_mlp, w_up_all, wd, up, dx2, tm)
    dgate, dy_conv, dy_sb, dy_x, da_conv, do_sb, do_x = _merge_bwd(dx1, proj, y_conv, y_sb, y_x, wc, ws, wx, wo, tm_s)
    dch, dcb, dcc, dconv = _conv_bwd(proj, conv_full, da_conv, 256)
    wgrads = {
        "w_conv_out": _weight_grad(a_conv, dy_conv, D_MODEL, 512, "dw_conv_out"),
        "w_sb_out": _weight_grad(o_sb, dy_sb, D_MODEL, 512, "dw_sb_out"),
        "w_x_out": _weight_grad(o_x, dy_x, D_MODEL, 512, "dw_x_out"),
        "w_out": _weight_grad(merged, dx1, D_MODEL, 512, "dw_out"),
        "w_up": _weight_grad(h2, dup, w_up_all.shape[2], 512, "dw_up"),
        "w_down": _weight_grad(act, dx2, D_MODEL, 512, "dw_down"),
    }

    own_blocks = jnp.stack([4 * (xpos ^ dx) + 2 * (ypos ^ dy) + cpos for dx in (0, 1) for dy in (0, 1)]).astype(jnp.int32)
    sibling_blocks = own_blocks + (1 - 2 * cpos)
    blocked = lambda n: wgrads[n].reshape((N_DEV,) + big[n][0].shape)
    same_shape = ["w_conv_out", "w_sb_out", "w_x_out", "w_out"]

    def pair_sum(group, from_sibling):
        return dict(zip(group, _pair_sum([blocked(n) for n in group], from_sibling, own_blocks, "pair_sum_" + group[0])))

    behind_x = same_shape + ["w_up", "w_down"]
    dxq, dkn, dvm, dqg, from_sibling = _x_bwd(proj, q_norm_g, kn, vmem, do_x, tm_s, [blocked(n) for n in behind_x])
    pair_sums = pair_sum(same_shape, from_sibling[:4])
    pair_sums.update(pair_sum(["w_up"], from_sibling[4:5]))
    pair_sums.update(pair_sum(["w_down"], from_sibling[5:6]))
    dkv, dg_mem, dkg = _mem_bwd(mem2d, g_mem, wkv_all, k_norm_g, dkn, dvm)
    wgrads["w_mem_kv"] = _weight_grad(mem_n, dkv, wkv_all.shape[2], 512, "dw_mem_kv")
    pair_sums.update(pair_sum(["w_mem_kv"], _pair_exchange([blocked("w_mem_kv")], "grad_pair_exchange_w_mem_kv")))
    dq, dk, dv, from_chips_late = _sb_bwd(proj, kt4, do_sb, o_sb, sb_weights, sb_logits, tq,
                                          [pair_sums[n] for n in late])
    from_chips = dict(zip(late, from_chips_late))
    dproj = jnp.concatenate([dch, dcb, dcc, dq, dk, dv, dxq, dgate], axis=1)
    bw_in = w_in_all.shape[2]
    dw_in_sibling, _ = _weight_grad_blocks(h, dproj, bw_in, sibling_blocks, 512, "dw_in_sibling")
    dw_in_own, (dw_in_recv,) = _weight_grad_blocks(h, dproj, bw_in, own_blocks, 512, "dw_in_own", send=[dw_in_sibling])
    pair_sums["w_in"], = _pair_sum([dw_in_own], [dw_in_recv], jnp.arange(4, dtype=jnp.int32), "pair_sum_w_in")
    grad_x, dg_mix, (from_chips["w_in"],) = _in_proj_bwd(x2d, g_mix, w_in_all, dproj, dx1, tm, [pair_sums["w_in"]])
    res = {}
    for group in [same_shape] + [[n] for n in big if n not in same_shape]:
        updates = _adamw_sharded([(pair_sums[n], from_chips[n]) + big[n] for n in group], 256, "adamw_" + group[0])
        res.update(zip(group, updates))

    part = _pack_rows(dg_mix, dg_mem, dg_mlp, dqg, dkg, dconv, lsum)
    gsum = _small_sum(_small_all_gather(part))
    loss = 0.5 * jnp.sum(gsum[6 * SMALL_TILE]) / D_MODEL
    conv_cols = lax.dynamic_slice(gsum[5 * SMALL_TILE:6 * SMALL_TILE], (0, me * (D_MODEL // N_DEV)),
                                  (SMALL_TILE, D_MODEL // N_DEV))
    g_small = jnp.concatenate([gsum[:5 * SMALL_TILE], _pad_tile(conv_cols)], axis=0)
    w_small = _pack_small(g_mix, g_mem, g_mlp, q_norm_g, k_norm_g, conv_w[0])
    m_small = _pack_small(m_g_mix, m_g_mem, m_g_mlp, m_q_norm_g, m_k_norm_g, m_conv_w[0])
    v_small = _pack_small(v_g_mix, v_g_mem, v_g_mlp, v_q_norm_g, v_k_norm_g, v_conv_w[0])
    d_small, nm_small, nv_small = _adamw_small(w_small, g_small, m_small, v_small)

    def unpack(p):
        return {"g_mix": p[0:1], "g_mem": p[8:9], "g_mlp": p[16:17], "q_norm_g": p[24:25, :X_HEAD_DIM],
                "k_norm_g": p[32:33, :X_HEAD_DIM], "conv_w": p[40:43, :D_MODEL // N_DEV][None]}

    small = [unpack(p) for p in (g_small, d_small, nm_small, nv_small)]
    order = ["g_mix", "g_mem", "w_in", "conv_w", "w_conv_out", "w_sb_out", "q_norm_g", "k_norm_g", "w_mem_kv",
             "w_x_out", "w_out", "g_mlp", "w_up", "w_down"]
    outs = [loss, grad_x[None]]
    for kind in range(4):
        for n in order:
            outs.append(res[n][kind][None] if n in res else small[kind][n])
    return tuple(outs)
```
